```python
import jax, jax.numpy as jnp
from jax import lax
import numpy as np

D_MODEL = 1024
BATCH = 8
SEQ = 8192
DEPTH = 1

HEAD_DIM = 64
ATTN_HEADS_PER_GROUP = 8
DILATED_GROUPS = ((128, 1), (512, 4), (2048, 16))
N_DIL = len(DILATED_GROUPS)
ATTN_WIDTH = ATTN_HEADS_PER_GROUP * HEAD_DIM
ROPE_DIM = HEAD_DIM // 4
ROPE_THETA = 500000.0
BLK = 128
SGU_CHUNK = 128
SGU_GROUPS = 8
SGU_WIDTH = D_MODEL // 2
SGU_GROUP_DIM = SGU_WIDTH // SGU_GROUPS
D_FF = -(-8 * D_MODEL // (3 * 256)) * 256
QKV_COLS = 3 * N_DIL * ATTN_WIDTH
IN_COLS = QKV_COLS + 2 * SGU_WIDTH + 2 * D_MODEL
EPS = 1e-6

kernel_name = "hybrid_dilated_attn_gmlp_gated_block"


def rmsnorm(x, g):
    xf = x.astype(jnp.float32)
    y = xf * lax.rsqrt(jnp.mean(xf * xf, axis=-1, keepdims=True) + EPS)
    return (y * g.astype(jnp.float32)).astype(x.dtype)


def layernorm(x, g, b):
    xf = x.astype(jnp.float32)
    mu = jnp.mean(xf, axis=-1, keepdims=True)
    xc = xf - mu
    y = xc * lax.rsqrt(jnp.mean(xc * xc, axis=-1, keepdims=True) + EPS)
    return (y * g.astype(jnp.float32) + b.astype(jnp.float32)).astype(x.dtype)


def partial_rope(t, positions):
    half = ROPE_DIM // 2
    inv_freq = ROPE_THETA ** (-jnp.arange(0, ROPE_DIM, 2, dtype=jnp.float32) / ROPE_DIM)
    ang = positions.astype(jnp.float32)[..., None] * inv_freq
    cos = jnp.cos(ang)[:, :, None, :]
    sin = jnp.sin(ang)[:, :, None, :]
    tf = t.astype(jnp.float32)
    x1, x2 = tf[..., :half], tf[..., half:ROPE_DIM]
    rot = jnp.concatenate([x1 * cos - x2 * sin, x2 * cos + x1 * sin, tf[..., ROPE_DIM:]], axis=-1)
    return rot.astype(t.dtype)


def dilated_attention(q, k, v, window, dilation):
    B, S, H, Dh = q.shape
    span = window // dilation
    L = -(-S // dilation)
    L_pad = -(-L // BLK) * BLK
    S_pad = L_pad * dilation
    nb = L_pad // BLK
    pad = ((0, 0), (0, S_pad - S), (0, 0), (0, 0))

    def strided(t):
        t = jnp.pad(t, pad).reshape(B, L_pad, dilation, H, Dh).transpose(0, 2, 1, 3, 4)
        return t.reshape(B, dilation, nb, BLK, H, Dh)

    def with_prev(t):
        prev = jnp.pad(t, ((0, 0), (0, 0), (1, 0), (0, 0), (0, 0), (0, 0)))[:, :, :-1]
        return jnp.concatenate([prev, t], axis=3)

    qs = strided(q * (Dh ** -0.5))
    kb = with_prev(strided(k))
    vb = with_prev(strided(v))
    s = jnp.einsum('brnqhd,brnkhd->brnhqk', qs, kb, preferred_element_type=jnp.float32)

    i = jnp.arange(BLK)[:, None]
    j = jnp.arange(2 * BLK)[None, :]
    diff = BLK + i - j
    band = (diff >= 0) & (diff <= span)
    key_exists = (jnp.arange(nb)[:, None, None] > 0) | (j >= BLK)[None]
    mask = band[None] & key_exists
    s = jnp.where(mask[None, None, :, None], s, -jnp.inf)

    m = jnp.max(s, axis=-1, keepdims=True)
    p = jnp.exp(s - m)
    den = jnp.sum(p, axis=-1)
    lse = m[..., 0] + jnp.log(den)
    o = jnp.einsum('brnhqk,brnkhd->brnqhd', p, vb.astype(jnp.float32))
    o = o / jnp.swapaxes(den, -1, -2)[..., None]

    o = o.reshape(B, dilation, L_pad, H, Dh).transpose(0, 2, 1, 3, 4).reshape(B, S_pad, H, Dh)[:, :S]
    lse = jnp.swapaxes(lse, -1, -2).reshape(B, dilation, L_pad, H).transpose(0, 2, 1, 3)
    lse = lse.reshape(B, S_pad, H)[:, :S]
    return o.astype(q.dtype), lse


def spatial_gating(uv, ln_g, ln_b, w_s, b_s):
    B, S, _ = uv.shape
    z = jax.nn.gelu(uv, approximate=False)
    u, v = z[..., :SGU_WIDTH], z[..., SGU_WIDTH:]
    v = layernorm(v, ln_g, ln_b)
    vc = v.reshape(B, S // SGU_CHUNK, SGU_CHUNK, SGU_GROUPS, SGU_GROUP_DIM)
    causal = jnp.tril(jnp.ones((SGU_CHUNK, SGU_CHUNK), dtype=bool))
    w_causal = jnp.where(causal[None], w_s, jnp.zeros_like(w_s))
    mixed = jnp.einsum('gts,bnsgc->bntgc', w_causal, vc)
    mixed = mixed + jnp.transpose(b_s)[None, None, :, :, None]
    return u * mixed.reshape(B, S, SGU_WIDTH)


def _fwd_setup_inputs(seed: int = 0) -> dict:
    key = jax.random.key(seed)
    ks = jax.random.split(key, 18)
    f32 = jnp.float32
    x = jax.random.normal(ks[0], (BATCH, SEQ, D_MODEL), f32)
    offset = jax.random.randint(ks[1], (BATCH, 1), 0, 4096, dtype=jnp.int32)
    positions = offset + jnp.arange(SEQ, dtype=jnp.int32)[None, :]
    nrm = lambda k, shape, fan_in: jax.random.normal(k, shape, f32) * (fan_in ** -0.5)
    return {
        "x": x,
        "positions": positions,
        "norm1_g": 1.0 + 0.02 * jax.random.normal(ks[2], (DEPTH, D_MODEL), f32),
        "w_in": nrm(ks[3], (DEPTH, D_MODEL, IN_COLS), D_MODEL),
        "sgu_ln_g": 1.0 + 0.02 * jax.random.normal(ks[4], (DEPTH, SGU_WIDTH), f32),
        "sgu_ln_b": 0.02 * jax.random.normal(ks[5], (DEPTH, SGU_WIDTH), f32),
        "w_spatial": nrm(ks[6], (DEPTH, SGU_GROUPS, SGU_CHUNK, SGU_CHUNK), SGU_CHUNK),
        "b_spatial": 1.0 + 0.1 * jax.random.normal(ks[7], (DEPTH, SGU_GROUPS, SGU_CHUNK), f32),
        "w_proj_attn": nrm(ks[8], (DEPTH, ATTN_WIDTH, D_MODEL), ATTN_WIDTH),
        "w_proj_sgu": nrm(ks[9], (DEPTH, SGU_WIDTH, D_MODEL), SGU_WIDTH),
        "w_out": nrm(ks[10], (DEPTH, D_MODEL, D_MODEL), D_MODEL),
        "norm2_g": 1.0 + 0.02 * jax.random.normal(ks[11], (DEPTH, D_MODEL), f32),
        "w_ffn_gate": nrm(ks[12], (DEPTH, D_MODEL, D_FF), D_MODEL),
        "w_ffn_up": nrm(ks[13], (DEPTH, D_MODEL, D_FF), D_MODEL),
        "w_ffn_down": nrm(ks[14], (DEPTH, D_FF, D_MODEL), D_FF),
        "final_g": 1.0 + 0.02 * jax.random.normal(ks[15], (D_MODEL,), f32),
    }


def _fwd_reference(x, positions, norm1_g, w_in, sgu_ln_g, sgu_ln_b, w_spatial, b_spatial,
              w_proj_attn, w_proj_sgu, w_out, norm2_g, w_ffn_gate, w_ffn_up, w_ffn_down,
              final_g):
    B, S, _ = x.shape
    for l in range(DEPTH):
        h = rmsnorm(x, norm1_g[l])
        proj = h @ w_in[l]
        qkv = proj[..., :QKV_COLS].reshape(B, S, 3, N_DIL, ATTN_HEADS_PER_GROUP, HEAD_DIM)
        uv = proj[..., QKV_COLS:QKV_COLS + 2 * SGU_WIDTH]
        gate_a = jax.nn.sigmoid(proj[..., QKV_COLS + 2 * SGU_WIDTH:QKV_COLS + 2 * SGU_WIDTH + D_MODEL])
        gate_b = jax.nn.sigmoid(proj[..., QKV_COLS + 2 * SGU_WIDTH + D_MODEL:])

        outs, lses = [], []
        for g, (window, dilation) in enumerate(DILATED_GROUPS):
            q = partial_rope(qkv[:, :, 0, g], positions)
            k = partial_rope(qkv[:, :, 1, g], positions)
            o, lse = dilated_attention(q, k, qkv[:, :, 2, g], window, dilation)
            outs.append(o)
            lses.append(lse)
        alpha = jax.nn.softmax(jnp.stack(lses, axis=0), axis=0)
        attn = jnp.sum(alpha[..., None].astype(x.dtype) * jnp.stack(outs, axis=0), axis=0)
        attn = attn.reshape(B, S, ATTN_WIDTH)

        sgu = spatial_gating(uv, sgu_ln_g[l], sgu_ln_b[l], w_spatial[l], b_spatial[l])

        merged = gate_a * (attn @ w_proj_attn[l]) + gate_b * (sgu @ w_proj_sgu[l])
        x = x + merged @ w_out[l]

        h2 = rmsnorm(x, norm2_g[l])
        ff = jax.nn.silu(h2 @ w_ffn_gate[l]) * (h2 @ w_ffn_up[l])
        x = x + ff @ w_ffn_down[l]
    return rmsnorm(x, final_g)


import jax as _jax
import jax.numpy as _jnp

TWIN_FORMAT = 'train_step'
FWD_PARAMS = ['x', 'positions', 'norm1_g', 'w_in', 'sgu_ln_g', 'sgu_ln_b', 'w_spatial', 'b_spatial', 'w_proj_attn', 'w_proj_sgu', 'w_out', 'norm2_g', 'w_ffn_gate', 'w_ffn_up', 'w_ffn_down', 'final_g']
TWIN_WEIGHTS = ['norm1_g', 'w_in', 'sgu_ln_g', 'sgu_ln_b', 'w_spatial', 'b_spatial', 'w_proj_attn', 'w_proj_sgu', 'w_out', 'norm2_g', 'w_ffn_gate', 'w_ffn_up', 'w_ffn_down', 'final_g']
TWIN_DIFF_INPUT = 'x'
TWIN_INPUTS = ['x', 'positions', 'norm1_g', 'w_in', 'sgu_ln_g', 'sgu_ln_b', 'w_spatial', 'b_spatial', 'w_proj_attn', 'w_proj_sgu', 'w_out', 'norm2_g', 'w_ffn_gate', 'w_ffn_up', 'w_ffn_down', 'final_g', 'loss_target', 'm_norm1_g', 'm_w_in', 'm_sgu_ln_g', 'm_sgu_ln_b', 'm_w_spatial', 'm_b_spatial', 'm_w_proj_attn', 'm_w_proj_sgu', 'm_w_out', 'm_norm2_g', 'm_w_ffn_gate', 'm_w_ffn_up', 'm_w_ffn_down', 'm_final_g', 'v_norm1_g', 'v_w_in', 'v_sgu_ln_g', 'v_sgu_ln_b', 'v_w_spatial', 'v_b_spatial', 'v_w_proj_attn', 'v_w_proj_sgu', 'v_w_out', 'v_norm2_g', 'v_w_ffn_gate', 'v_w_ffn_up', 'v_w_ffn_down', 'v_final_g']
TWIN_OUTPUTS = ['loss', 'grad_x', 'grad_norm1_g', 'grad_w_in', 'grad_sgu_ln_g', 'grad_sgu_ln_b', 'grad_w_spatial', 'grad_b_spatial', 'grad_w_proj_attn', 'grad_w_proj_sgu', 'grad_w_out', 'grad_norm2_g', 'grad_w_ffn_gate', 'grad_w_ffn_up', 'grad_w_ffn_down', 'grad_final_g', 'delta_norm1_g', 'delta_w_in', 'delta_sgu_ln_g', 'delta_sgu_ln_b', 'delta_w_spatial', 'delta_b_spatial', 'delta_w_proj_attn', 'delta_w_proj_sgu', 'delta_w_out', 'delta_norm2_g', 'delta_w_ffn_gate', 'delta_w_ffn_up', 'delta_w_ffn_down', 'delta_final_g', 'new_m_norm1_g', 'new_m_w_in', 'new_m_sgu_ln_g', 'new_m_sgu_ln_b', 'new_m_w_spatial', 'new_m_b_spatial', 'new_m_w_proj_attn', 'new_m_w_proj_sgu', 'new_m_w_out', 'new_m_norm2_g', 'new_m_w_ffn_gate', 'new_m_w_ffn_up', 'new_m_w_ffn_down', 'new_m_final_g', 'new_v_norm1_g', 'new_v_w_in', 'new_v_sgu_ln_g', 'new_v_sgu_ln_b', 'new_v_w_spatial', 'new_v_b_spatial', 'new_v_w_proj_attn', 'new_v_w_proj_sgu', 'new_v_w_out', 'new_v_norm2_g', 'new_v_w_ffn_gate', 'new_v_w_ffn_up', 'new_v_w_ffn_down', 'new_v_final_g']
TWIN_LEAF_KINDS = {'loss': 'loss', 'grad_x': 'grad_x', 'grad_norm1_g': 'grad_w', 'grad_w_in': 'grad_w', 'grad_sgu_ln_g': 'grad_w', 'grad_sgu_ln_b': 'grad_w', 'grad_w_spatial': 'grad_w', 'grad_b_spatial': 'grad_w', 'grad_w_proj_attn': 'grad_w', 'grad_w_proj_sgu': 'grad_w', 'grad_w_out': 'grad_w', 'grad_norm2_g': 'grad_w', 'grad_w_ffn_gate': 'grad_w', 'grad_w_ffn_up': 'grad_w', 'grad_w_ffn_down': 'grad_w', 'grad_final_g': 'grad_w', 'delta_norm1_g': 'delta_w', 'delta_w_in': 'delta_w', 'delta_sgu_ln_g': 'delta_w', 'delta_sgu_ln_b': 'delta_w', 'delta_w_spatial': 'delta_w', 'delta_b_spatial': 'delta_w', 'delta_w_proj_attn': 'delta_w', 'delta_w_proj_sgu': 'delta_w', 'delta_w_out': 'delta_w', 'delta_norm2_g': 'delta_w', 'delta_w_ffn_gate': 'delta_w', 'delta_w_ffn_up': 'delta_w', 'delta_w_ffn_down': 'delta_w', 'delta_final_g': 'delta_w', 'new_m_norm1_g': 'new_m', 'new_m_w_in': 'new_m', 'new_m_sgu_ln_g': 'new_m', 'new_m_sgu_ln_b': 'new_m', 'new_m_w_spatial': 'new_m', 'new_m_b_spatial': 'new_m', 'new_m_w_proj_attn': 'new_m', 'new_m_w_proj_sgu': 'new_m', 'new_m_w_out': 'new_m', 'new_m_norm2_g': 'new_m', 'new_m_w_ffn_gate': 'new_m', 'new_m_w_ffn_up': 'new_m', 'new_m_w_ffn_down': 'new_m', 'new_m_final_g': 'new_m', 'new_v_norm1_g': 'new_v', 'new_v_w_in': 'new_v', 'new_v_sgu_ln_g': 'new_v', 'new_v_sgu_ln_b': 'new_v', 'new_v_w_spatial': 'new_v', 'new_v_b_spatial': 'new_v', 'new_v_w_proj_attn': 'new_v', 'new_v_w_proj_sgu': 'new_v', 'new_v_w_out': 'new_v', 'new_v_norm2_g': 'new_v', 'new_v_w_ffn_gate': 'new_v', 'new_v_w_ffn_up': 'new_v', 'new_v_w_ffn_down': 'new_v', 'new_v_final_g': 'new_v'}


def _forward(args):
    return _fwd_reference(*[args[k] for k in FWD_PARAMS])


def _output_shape():
    def fwd():
        inp = _fwd_setup_inputs(0)
        return _fwd_reference(*[inp[k] for k in FWD_PARAMS])
    out = _jax.eval_shape(fwd)
    return out.shape, out.dtype

N_MICROBATCH = 1
ADAM_LR = 0.001
ADAM_B1 = 0.9
ADAM_B2 = 0.999
ADAM_EPS = 1e-08
ADAM_WD = 0.01
ADAM_STEP = 10
PER_EXAMPLE_BATCH_AXIS = {'x': 0, 'positions': 0, 'loss_target': 0}
SHARED_INPUTS = []
_WEIGHT_DTYPES = {'norm1_g': _jnp.float32, 'w_in': _jnp.float32, 'sgu_ln_g': _jnp.float32, 'sgu_ln_b': _jnp.float32, 'w_spatial': _jnp.float32, 'b_spatial': _jnp.float32, 'w_proj_attn': _jnp.float32, 'w_proj_sgu': _jnp.float32, 'w_out': _jnp.float32, 'norm2_g': _jnp.float32, 'w_ffn_gate': _jnp.float32, 'w_ffn_up': _jnp.float32, 'w_ffn_down': _jnp.float32, 'final_g': _jnp.float32}
MOMENT_SCALE = {'norm1_g': 1.491610e-01, 'w_in': 5.406257e-02, 'sgu_ln_g': 9.506598e-02, 'sgu_ln_b': 8.869512e-02, 'w_spatial': 6.335175e-02, 'b_spatial': 9.725221e-02, 'w_proj_attn': 2.615643e-02, 'w_proj_sgu': 1.134839e-01, 'w_out': 1.128862e-01, 'norm2_g': 1.884111e-01, 'w_ffn_gate': 7.507889e-02, 'w_ffn_up': 7.280616e-02, 'w_ffn_down': 1.208215e-01, 'final_g': 6.400416e+01}


def _to_microbatches(a, axis):
    t = _jnp.moveaxis(a, axis, 0)
    t = t.reshape((N_MICROBATCH, t.shape[0] // N_MICROBATCH) + t.shape[1:])
    return _jnp.moveaxis(t, 1, axis + 1)


def setup_inputs(seed: int = 0) -> dict:
    inp = _fwd_setup_inputs(seed)
    key = _jax.random.fold_in(_jax.random.key(seed), 7919)
    shape, _ = _output_shape()
    out = dict(inp)
    out["loss_target"] = _jax.random.normal(_jax.random.fold_in(key, 0), shape, _jnp.float32)
    for i, name in enumerate(TWIN_WEIGHTS):
        w = inp[name].astype(_jnp.float32)
        if MOMENT_SCALE is None:
            s = _jnp.sqrt(_jnp.mean(_jnp.square(w)) + 1e-30)
        else:
            s = MOMENT_SCALE[name]
        km, kv = _jax.random.split(_jax.random.fold_in(key, i + 1))
        out[name] = w
        out["m_" + name] = s * _jax.random.normal(km, w.shape, _jnp.float32)
        out["v_" + name] = (s * s) * _jax.random.uniform(kv, w.shape, _jnp.float32, 0.5, 1.5)
    if N_MICROBATCH > 1:
        for name, axis in PER_EXAMPLE_BATCH_AXIS.items():
            out[name] = _to_microbatches(out[name], axis)
    return {'x': out['x'], 'positions': out['positions'], 'norm1_g': out['norm1_g'], 'w_in': out['w_in'], 'sgu_ln_g': out['sgu_ln_g'], 'sgu_ln_b': out['sgu_ln_b'], 'w_spatial': out['w_spatial'], 'b_spatial': out['b_spatial'], 'w_proj_attn': out['w_proj_attn'], 'w_proj_sgu': out['w_proj_sgu'], 'w_out': out['w_out'], 'norm2_g': out['norm2_g'], 'w_ffn_gate': out['w_ffn_gate'], 'w_ffn_up': out['w_ffn_up'], 'w_ffn_down': out['w_ffn_down'], 'final_g': out['final_g'], 'loss_target': out['loss_target'], 'm_norm1_g': out['m_norm1_g'], 'm_w_in': out['m_w_in'], 'm_sgu_ln_g': out['m_sgu_ln_g'], 'm_sgu_ln_b': out['m_sgu_ln_b'], 'm_w_spatial': out['m_w_spatial'], 'm_b_spatial': out['m_b_spatial'], 'm_w_proj_attn': out['m_w_proj_attn'], 'm_w_proj_sgu': out['m_w_proj_sgu'], 'm_w_out': out['m_w_out'], 'm_norm2_g': out['m_norm2_g'], 'm_w_ffn_gate': out['m_w_ffn_gate'], 'm_w_ffn_up': out['m_w_ffn_up'], 'm_w_ffn_down': out['m_w_ffn_down'], 'm_final_g': out['m_final_g'], 'v_norm1_g': out['v_norm1_g'], 'v_w_in': out['v_w_in'], 'v_sgu_ln_g': out['v_sgu_ln_g'], 'v_sgu_ln_b': out['v_sgu_ln_b'], 'v_w_spatial': out['v_w_spatial'], 'v_b_spatial': out['v_b_spatial'], 'v_w_proj_attn': out['v_w_proj_attn'], 'v_w_proj_sgu': out['v_w_proj_sgu'], 'v_w_out': out['v_w_out'], 'v_norm2_g': out['v_norm2_g'], 'v_w_ffn_gate': out['v_w_ffn_gate'], 'v_w_ffn_up': out['v_w_ffn_up'], 'v_w_ffn_down': out['v_w_ffn_down'], 'v_final_g': out['v_final_g']}


def _loss(weights, diff, rest, loss_target):
    with _jax.named_scope("forward"):
        args = {**rest, TWIN_DIFF_INPUT: diff, **{k: w.astype(_WEIGHT_DTYPES[k]) for k, w in weights.items()}}
        y = _forward(args)
    with _jax.named_scope("loss_head"):
        err = _jnp.square(y.astype(_jnp.float32) - loss_target)
        return 0.5 * _jnp.sum(_jnp.mean(err, axis=-1)) if err.ndim else 0.5 * err


def _adamw(w, g, m, v):
    m = ADAM_B1 * m + (1.0 - ADAM_B1) * g
    v = ADAM_B2 * v + (1.0 - ADAM_B2) * _jnp.square(g)
    m_hat = m / (1.0 - ADAM_B1 ** ADAM_STEP)
    v_hat = v / (1.0 - ADAM_B2 ** ADAM_STEP)
    delta = -ADAM_LR * (m_hat / (_jnp.sqrt(v_hat) + ADAM_EPS) + ADAM_WD * w)
    return delta, m, v


def reference(x, positions, norm1_g, w_in, sgu_ln_g, sgu_ln_b, w_spatial, b_spatial, w_proj_attn, w_proj_sgu, w_out, norm2_g, w_ffn_gate, w_ffn_up, w_ffn_down, final_g, loss_target, m_norm1_g, m_w_in, m_sgu_ln_g, m_sgu_ln_b, m_w_spatial, m_b_spatial, m_w_proj_attn, m_w_proj_sgu, m_w_out, m_norm2_g, m_w_ffn_gate, m_w_ffn_up, m_w_ffn_down, m_final_g, v_norm1_g, v_w_in, v_sgu_ln_g, v_sgu_ln_b, v_w_spatial, v_b_spatial, v_w_proj_attn, v_w_proj_sgu, v_w_out, v_norm2_g, v_w_ffn_gate, v_w_ffn_up, v_w_ffn_down, v_final_g):
    given = dict(x=x, positions=positions, norm1_g=norm1_g, w_in=w_in, sgu_ln_g=sgu_ln_g, sgu_ln_b=sgu_ln_b, w_spatial=w_spatial, b_spatial=b_spatial, w_proj_attn=w_proj_attn, w_proj_sgu=w_proj_sgu, w_out=w_out, norm2_g=norm2_g, w_ffn_gate=w_ffn_gate, w_ffn_up=w_ffn_up, w_ffn_down=w_ffn_down, final_g=final_g, loss_target=loss_target, m_norm1_g=m_norm1_g, m_w_in=m_w_in, m_sgu_ln_g=m_sgu_ln_g, m_sgu_ln_b=m_sgu_ln_b, m_w_spatial=m_w_spatial, m_b_spatial=m_b_spatial, m_w_proj_attn=m_w_proj_attn, m_w_proj_sgu=m_w_proj_sgu, m_w_out=m_w_out, m_norm2_g=m_norm2_g, m_w_ffn_gate=m_w_ffn_gate, m_w_ffn_up=m_w_ffn_up, m_w_ffn_down=m_w_ffn_down, m_final_g=m_final_g, v_norm1_g=v_norm1_g, v_w_in=v_w_in, v_sgu_ln_g=v_sgu_ln_g, v_sgu_ln_b=v_sgu_ln_b, v_w_spatial=v_w_spatial, v_b_spatial=v_b_spatial, v_w_proj_attn=v_w_proj_attn, v_w_proj_sgu=v_w_proj_sgu, v_w_out=v_w_out, v_norm2_g=v_norm2_g, v_w_ffn_gate=v_w_ffn_gate, v_w_ffn_up=v_w_ffn_up, v_w_ffn_down=v_w_ffn_down, v_final_g=v_final_g)
    weights = {n: given[n] for n in TWIN_WEIGHTS}
    shared = {n: given[n] for n in SHARED_INPUTS}
    per_example = {n: given[n] for n in ['x', 'positions']}
    grad_fn = _jax.value_and_grad(_loss, argnums=(0, 1))

    def one_microbatch(ex, loss_target):
        ex = dict(ex)
        diff = ex.pop(TWIN_DIFF_INPUT)
        return grad_fn(weights, diff, {**shared, **ex}, loss_target)

    if N_MICROBATCH == 1:
        loss, (grad_w, grad_x) = one_microbatch(per_example, given["loss_target"])
    else:
        def body(carry, xs):
            loss_sum, grad_sum = carry
            l_k, (gw_k, gx_k) = one_microbatch(xs[0], xs[1])
            with _jax.named_scope("update"):
                return (loss_sum + l_k, _jax.tree.map(_jnp.add, grad_sum, gw_k)), gx_k

        init = (_jnp.zeros((), _jnp.float32), _jax.tree.map(_jnp.zeros_like, weights))
        (loss, grad_w), grad_x = _jax.lax.scan(body, init, (per_example, given["loss_target"]))
    with _jax.named_scope("update"):
        delta_w, new_m, new_v = {}, {}, {}
        for n in TWIN_WEIGHTS:
            delta_w[n], new_m[n], new_v[n] = _adamw(weights[n], grad_w[n], given["m_" + n], given["v_" + n])
    return (loss, grad_x, *[grad_w[n] for n in TWIN_WEIGHTS], *[delta_w[n] for n in TWIN_WEIGHTS],
            *[new_m[n] for n in TWIN_WEIGHTS], *[new_v[n] for n in TWIN_WEIGHTS])
```

```python
import functools
import math

import jax
import jax.numpy as jnp
from jax import lax
from jax.experimental import pallas as pl
from jax.experimental.pallas import tpu as pltpu

F32 = jnp.float32
BF16 = jnp.bfloat16

D_MODEL = 1024
HEAD_PAIR = 128
ATTN_W = 512
DILATIONS = (1, 4, 16)
BLK = 128
ROPE_HALF = 8
ROPE_THETA = 500000.0
SGU_W = 512
QKV_COLS = 4608
D_FF = 2816
FF_TILE = 1408
EPS = 1e-6
N_DEV = 8
MASKED = -1e30

ADAM_LR = 0.001
ADAM_B1 = 0.9
ADAM_B2 = 0.999
ADAM_EPS = 1e-08
ADAM_WD = 0.01
ADAM_STEP = 10

MIB = 1024 * 1024
MESH = pl.DeviceIdType.MESH
ANY = pl.BlockSpec(memory_space=pl.ANY)


def _params(sem, vmem_mib):
    return pltpu.CompilerParams(dimension_semantics=sem, vmem_limit_bytes=vmem_mib * MIB)


def _nt(a, b):
    return lax.dot_general(a, b, (((1,), (1,)), ((), ())), preferred_element_type=F32)


def _nn(a, b):
    return lax.dot_general(a, b, (((1,), (0,)), ((), ())), preferred_element_type=F32)


def _tn(a, b):
    return lax.dot_general(a, b, (((0,), (0,)), ((), ())), preferred_element_type=F32)


def _all_gather_rows(shards, name, vmem=False):
    n = len(shards)

    def body(*refs):
        ins, outs = refs[:n], refs[n:2 * n]
        send_sems, recv_sems, local_sems = refs[2 * n:]
        x, y, c = lax.axis_index("x"), lax.axis_index("y"), lax.axis_index("c")
        me, sibling = (x, y, c), (x, y, 1 - c)
        chips = [(1 - x, y), (x, 1 - y), (1 - x, 1 - y)]

        def rows(m, px, py, pc):
            r = ins[m].shape[0]
            return outs[m].at[pl.ds((4 * px + 2 * py + pc) * r, r), :]

        def copy(m, k, block, to, src=None):
            return pltpu.make_async_remote_copy(
                src_ref=rows(m, *block) if src is None else src, dst_ref=rows(m, *block),
                send_sem=send_sems.at[m, k], recv_sem=recv_sems.at[m, k],
                device_id=to, device_id_type=MESH)

        mine = [pltpu.make_async_copy(ins[m], rows(m, *me), local_sems.at[m]) for m in range(n)]
        for cp in mine:
            cp.start()
        first = []
        for m in range(n):
            first.append(copy(m, 0, me, sibling, src=ins[m]))
            first += [copy(m, 1 + j, me, (*chip, c), src=ins[m]) for j, chip in enumerate(chips)]
        for cp in first:
            cp.start()
        passed = []
        for m in range(n):
            for j, chip in enumerate(chips):
                copy(m, 1 + j, (*chip, c), me).wait_recv()
                fwd = copy(m, 4 + j, (*chip, c), sibling)
                fwd.start()
                passed.append(fwd)
        for m in range(n):
            copy(m, 0, sibling, me).wait_recv()
            for j, chip in enumerate(chips):
                copy(m, 4 + j, (*chip, 1 - c), me).wait_recv()
        for cp in first + passed:
            cp.wait_send()
        for cp in mine:
            cp.wait()

    spec = pl.BlockSpec(memory_space=pltpu.VMEM) if vmem else ANY
    return pl.pallas_call(
        body, name=name,
        out_shape=[jax.ShapeDtypeStruct((N_DEV * s.shape[0], s.shape[1]), s.dtype) for s in shards],
        in_specs=[spec] * n, out_specs=[spec] * n,
        scratch_shapes=[pltpu.SemaphoreType.DMA((n, 7)), pltpu.SemaphoreType.DMA((n, 7)),
                        pltpu.SemaphoreType.DMA((n,))],
        compiler_params=pltpu.CompilerParams(vmem_limit_bytes=32 * MIB),
    )(*shards)


def _pair_exchange(grads):
    n = len(grads)

    def body(*refs):
        ins, outs = refs[:n], refs[n:2 * n]
        send_sems, recv_sems = refs[2 * n:]
        x, y, c = lax.axis_index("x"), lax.axis_index("y"), lax.axis_index("c")
        copies = [pltpu.make_async_remote_copy(
            src_ref=ins[m].at[:, 1 - c], dst_ref=outs[m],
            send_sem=send_sems.at[m], recv_sem=recv_sems.at[m],
            device_id=(x, y, 1 - c), device_id_type=MESH) for m in range(n)]
        for cp in copies:
            cp.start()
        for cp in copies:
            cp.wait()

    return pl.pallas_call(
        body, name="grad_pair_exchange",
        out_shape=[jax.ShapeDtypeStruct((4,) + g.shape[2:], g.dtype) for g in grads],
        in_specs=[ANY] * n, out_specs=[ANY] * n,
        scratch_shapes=[pltpu.SemaphoreType.DMA((n,)), pltpu.SemaphoreType.DMA((n,))],
    )(*grads)


def _chip_exchange(pair_sums):
    n = len(pair_sums)

    def body(*refs):
        ins, outs = refs[:n], refs[n:2 * n]
        send_sems, recv_sems = refs[2 * n:]
        x, y, c = lax.axis_index("x"), lax.axis_index("y"), lax.axis_index("c")
        chips = [(1 - x, y), (x, 1 - y), (1 - x, 1 - y)]

        def copy(m, j, q_block, to):
            return pltpu.make_async_remote_copy(
                src_ref=ins[m].at[q_block], dst_ref=outs[m].at[j],
                send_sem=send_sems.at[m, j], recv_sem=recv_sems.at[m, j],
                device_id=to, device_id_type=MESH)

        sends = [copy(m, j, 2 * px + py, (px, py, c))
                 for m in range(n) for j, (px, py) in enumerate(chips)]
        for cp in sends:
            cp.start()
        for cp in sends:
            cp.wait_recv()
        for cp in sends:
            cp.wait_send()

    return pl.pallas_call(
        body, name="grad_chip_exchange",
        out_shape=[jax.ShapeDtypeStruct((3,) + p.shape[1:], p.dtype) for p in pair_sums],
        in_specs=[ANY] * n, out_specs=[ANY] * n,
        scratch_shapes=[pltpu.SemaphoreType.DMA((n, 3)), pltpu.SemaphoreType.DMA((n, 3))],
    )(*pair_sums)


N_SLABS = ATTN_W // HEAD_PAIR


def _slab_scratch(tm):
    return pltpu.VMEM((N_SLABS, tm, HEAD_PAIR), F32)


def _rows_by_residue(dst_ref, slab_ref, r, tr, dtype):
    for rho in range(r):
        for s in range(N_SLABS):
            dst_ref[rho, :, s * HEAD_PAIR:(s + 1) * HEAD_PAIR] = (
                slab_ref[s, pl.ds(rho, tr, stride=r), :].astype(dtype))


def _rows_by_token(slab_ref, src_ref, r, tr):
    for rho in range(r):
        for s in range(N_SLABS):
            slab_ref[s, pl.ds(rho, tr, stride=r), :] = src_ref[rho, :, s * HEAD_PAIR:(s + 1) * HEAD_PAIR]


def _rope_tables(pos_col, inv_freq_row, tm):
    T = pos_col.shape[0]

    def body(pos_ref, invf_ref, cos_ref, s1_ref, s2_ref):
        ang = pos_ref[...].astype(F32) * invf_ref[...]
        lane = lax.broadcasted_iota(jnp.int32, (1, HEAD_PAIR), 1) % 64
        cs, sn = jnp.cos(ang), jnp.sin(ang)
        cos_ref[...] = jnp.where(lane < 2 * ROPE_HALF, cs, 1.0)
        s1_ref[...] = jnp.where(lane < ROPE_HALF, -sn, 0.0)
        s2_ref[...] = jnp.where((lane >= ROPE_HALF) & (lane < 2 * ROPE_HALF), sn, 0.0)

    tab = jax.ShapeDtypeStruct((T, HEAD_PAIR), F32)
    row = pl.BlockSpec((tm, HEAD_PAIR), lambda i: (i, 0))
    return pl.pallas_call(
        body, name="rope_tables", grid=(T // tm,), out_shape=[tab] * 3,
        in_specs=[pl.BlockSpec((tm, 1), lambda i: (i, 0)), pl.BlockSpec((1, HEAD_PAIR), lambda i: (0, 0))],
        out_specs=[row] * 3, compiler_params=_params(("parallel",), 16),
    )(pos_col, inv_freq_row)


def _rope(y, cos, s1, s2):
    w = y.shape[1]
    rep = w // HEAD_PAIR
    return (y * jnp.tile(cos, (1, rep)) + pltpu.roll(y, w - ROPE_HALF, 1) * jnp.tile(s1, (1, rep))
            + pltpu.roll(y, ROPE_HALF, 1) * jnp.tile(s2, (1, rep)))


def _rope_transposed(dy, cos, s1, s2):
    w = dy.shape[1]
    rep = w // HEAD_PAIR
    return (dy * jnp.tile(cos, (1, rep)) + pltpu.roll(dy * jnp.tile(s1, (1, rep)), ROPE_HALF, 1)
            + pltpu.roll(dy * jnp.tile(s2, (1, rep)), w - ROPE_HALF, 1))


def _rmsnorm_fwd(x, g, name, tm):
    T = x.shape[0]

    def body(x_ref, g_ref, h_ref):
        xf = x_ref[...]
        r = lax.rsqrt(jnp.mean(xf * xf, axis=-1, keepdims=True) + EPS)
        h_ref[...] = (xf * r * g_ref[...]).astype(BF16)

    return pl.pallas_call(
        body, name=name, grid=(T // tm,), out_shape=jax.ShapeDtypeStruct((T, D_MODEL), BF16),
        in_specs=[pl.BlockSpec((tm, D_MODEL), lambda i: (i, 0)), pl.BlockSpec((1, D_MODEL), lambda i: (0, 0))],
        out_specs=pl.BlockSpec((tm, D_MODEL), lambda i: (i, 0)), compiler_params=_params(("parallel",), 24),
    )(x, g)


def _qkv_proj(h, win_t, tables, group, tm):
    T = h.shape[0]
    r = DILATIONS[group]
    tr = tm // r

    def body(h_ref, w_ref, cos_ref, s1_ref, s2_ref, o_ref, acc_ref):
        kind = pl.program_id(1)
        y = _nt(h_ref[...], w_ref[...])
        scale = jnp.where(kind == 0, 0.125, 1.0).astype(F32)
        val = jnp.where(kind < 2, _rope(y, cos_ref[...], s1_ref[...], s2_ref[...]) * scale, y)
        if r == 1:
            o_ref[0] = val.astype(BF16)
        else:
            for s in range(N_SLABS):
                acc_ref[s] = val[:, s * HEAD_PAIR:(s + 1) * HEAD_PAIR]
            _rows_by_residue(o_ref, acc_ref, r, tr, BF16)

    row = pl.BlockSpec((tm, HEAD_PAIR), lambda i, k: (i, 0))
    return pl.pallas_call(
        body, name=f"qkv_proj_g{group}", grid=(T // tm, 3),
        out_shape=jax.ShapeDtypeStruct((r, T // r, 3 * ATTN_W), BF16),
        in_specs=[pl.BlockSpec((tm, D_MODEL), lambda i, k: (i, 0)),
                  pl.BlockSpec((ATTN_W, D_MODEL), lambda i, k: (3 * k + group, 0)), row, row, row],
        out_specs=pl.BlockSpec((r, tr, ATTN_W), lambda i, k: (0, i, k)),
        scratch_shapes=[_slab_scratch(tm)],
        compiler_params=_params(("parallel", "arbitrary"), 32),
    )(h, win_t, *tables)


def _mm_nt(a, b_t, *, row_block0, n_blocks, tn, tm, out_dtype, name):
    T, K = a.shape

    def body(a_ref, b_ref, o_ref):
        o_ref[...] = _nt(a_ref[...], b_ref[...]).astype(out_dtype)

    return pl.pallas_call(
        body, name=name, grid=(T // tm, n_blocks),
        out_shape=jax.ShapeDtypeStruct((T, n_blocks * tn), out_dtype),
        in_specs=[pl.BlockSpec((tm, K), lambda i, j: (i, 0)),
                  pl.BlockSpec((tn, K), lambda i, j: (row_block0 + j, 0))],
        out_specs=pl.BlockSpec((tm, tn), lambda i, j: (i, j)),
        compiler_params=_params(("parallel", "arbitrary"), 40),
    )(a, b_t)


def _band_mask(n):
    row = lax.broadcasted_iota(jnp.int32, (BLK, 2 * BLK), 0)
    col = lax.broadcasted_iota(jnp.int32, (BLK, 2 * BLK), 1)
    has_prev = (jnp.zeros_like(row) + n) > 0
    return ((col < BLK) & (col >= row) & has_prev) | ((col >= BLK) & (col - BLK <= row))


def _head_lanes():
    lane = lax.broadcasted_iota(jnp.int32, (1, HEAD_PAIR), 1)
    return lane < 64, lane >= 64


def _attn_fwd(qkv, group):
    r, L, _ = qkv.shape
    nb = L // BLK

    def body(q_ref, kp_ref, kc_ref, vp_ref, vc_ref, o_ref, lse_ref):
        valid = _band_mask(pl.program_id(1))
        head0, head1 = _head_lanes()
        for p in range(ATTN_W // HEAD_PAIR):
            sl = slice(p * HEAD_PAIR, (p + 1) * HEAD_PAIR)
            q = q_ref[:, sl]
            k2 = jnp.concatenate([kp_ref[:, sl], kc_ref[:, sl]], axis=0)
            v2 = jnp.concatenate([vp_ref[:, sl], vc_ref[:, sl]], axis=0)
            outs, lses = [], []
            for half in (head0, head1):
                s = jnp.where(valid, _nt(jnp.where(half, q, jnp.zeros_like(q)), k2), MASKED)
                m = jnp.max(s, axis=1, keepdims=True)
                e = jnp.exp(s - m)
                den = jnp.sum(e, axis=1, keepdims=True)
                outs.append(_nn(e.astype(BF16), v2) / den)
                lses.append(m + jnp.log(den))
            o_ref[:, sl] = jnp.where(head0, outs[0], outs[1])
            lse_ref[:, sl] = jnp.where(head0, lses[0], lses[1])

    def blk(kind, prev):
        if prev:
            return pl.BlockSpec((None, BLK, ATTN_W), lambda rho, n: (rho, jnp.maximum(n - 1, 0), kind))
        return pl.BlockSpec((None, BLK, ATTN_W), lambda rho, n: (rho, n, kind))

    out = pl.BlockSpec((None, BLK, ATTN_W), lambda rho, n: (rho, n, 0))
    res = jax.ShapeDtypeStruct((r, L, ATTN_W), F32)
    return pl.pallas_call(
        body, name=f"attn_fwd_g{group}", grid=(r, nb), out_shape=[res, res],
        in_specs=[blk(0, False), blk(1, True), blk(1, False), blk(2, True), blk(2, False)],
        out_specs=[out, out], compiler_params=_params(("parallel", "arbitrary"), 24),
    )(qkv, qkv, qkv, qkv, qkv)


def _attn_combine(outs, lses, tm):
    T = outs[0].shape[1]

    def body(o0, l0, o1, l1, o2, l2, attn_ref, lse_ref, so1, sl1, so2, sl2):
        for o_in, l_in, so, sl, r in ((o1, l1, so1, sl1, DILATIONS[1]), (o2, l2, so2, sl2, DILATIONS[2])):
            _rows_by_token(so, o_in, r, tm // r)
            _rows_by_token(sl, l_in, r, tm // r)
        for s in range(N_SLABS):
            lanes = slice(s * HEAD_PAIR, (s + 1) * HEAD_PAIR)
            a0, a1, a2 = l0[0, :, lanes], sl1[s], sl2[s]
            mx = jnp.maximum(jnp.maximum(a0, a1), a2)
            e0, e1, e2 = jnp.exp(a0 - mx), jnp.exp(a1 - mx), jnp.exp(a2 - mx)
            tot = e0 + e1 + e2
            attn_ref[:, lanes] = (e0 * o0[0, :, lanes] + e1 * so1[s] + e2 * so2[s]) / tot
            lse_ref[:, lanes] = mx + jnp.log(tot)

    def grouped(r):
        return pl.BlockSpec((r, tm // r, ATTN_W), lambda i: (0, i, 0))

    tok = pl.BlockSpec((tm, ATTN_W), lambda i: (i, 0))
    res = jax.ShapeDtypeStruct((T, ATTN_W), F32)
    ins, specs = [], []
    for g, r in enumerate(DILATIONS):
        ins += [outs[g], lses[g]]
        specs += [grouped(r), grouped(r)]
    return pl.pallas_call(
        body, name="attn_combine", grid=(T // tm,), out_shape=[res, res], in_specs=specs, out_specs=[tok, tok],
        scratch_shapes=[_slab_scratch(tm)] * 4, compiler_params=_params(("parallel",), 32),
    )(*ins)


def _gelu(x):
    return 0.5 * x * (1.0 + lax.erf(x * (1.0 / math.sqrt(2.0))))


def _gelu_grad(x):
    return 0.5 * (1.0 + lax.erf(x * (1.0 / math.sqrt(2.0)))) + x * jnp.exp(-0.5 * x * x) * (1.0 / math.sqrt(2.0 * math.pi))


def _causal():
    row = lax.broadcasted_iota(jnp.int32, (BLK, BLK), 0)
    col = lax.broadcasted_iota(jnp.int32, (BLK, BLK), 1)
    return col <= row


def _bias_lanes(bt):
    grp = lax.broadcasted_iota(jnp.int32, (1, SGU_W), 1) // 64
    out = jnp.zeros((BLK, SGU_W), F32)
    for g in range(8):
        out = jnp.where(grp == g, bt[:, g:g + 1], out)
    return out


def _sgu_normalise(uv, ln_g, ln_b):
    z = _gelu(uv)
    u, v = z[:, :SGU_W], z[:, SGU_W:]
    mu = jnp.mean(v, axis=-1, keepdims=True)
    xc = v - mu
    rstd = lax.rsqrt(jnp.mean(xc * xc, axis=-1, keepdims=True) + EPS)
    xh = xc * rstd
    return u, xh, rstd, xh * ln_g + ln_b


def _sgu_mix(wc_ref, vb, head0):
    chunks = []
    for ch in range(vb.shape[0] // BLK):
        pairs = []
        for p in range(SGU_W // HEAD_PAIR):
            v_pair = vb[ch * BLK:(ch + 1) * BLK, p * HEAD_PAIR:(p + 1) * HEAD_PAIR]
            pairs.append(jnp.where(head0, _nn(wc_ref[2 * p], v_pair), _nn(wc_ref[2 * p + 1], v_pair)))
        chunks.append(jnp.concatenate(pairs, axis=1))
    return jnp.concatenate(chunks, axis=0)


def _sgu_fwd(uvg, ln_g, ln_b, w_s, b_t, tm):
    T = uvg.shape[0]

    def body(uv_ref, g_ref, b_ref, w_ref, bt_ref, o_ref, wc_ref, bias_ref):
        @pl.when(pl.program_id(0) == 0)
        def _():
            causal = _causal()
            for g in range(8):
                wc_ref[g] = jnp.where(causal, w_ref[g], 0.0).astype(BF16)
            bias_ref[...] = _bias_lanes(bt_ref[...])

        u, _, _, vn = _sgu_normalise(uv_ref[...], g_ref[...], b_ref[...])
        mixed = _sgu_mix(wc_ref, vn.astype(BF16), _head_lanes()[0])
        o_ref[...] = (u * (mixed + jnp.tile(bias_ref[...], (tm // BLK, 1)))).astype(BF16)

    vec = pl.BlockSpec((1, SGU_W), lambda i: (0, 0))
    return pl.pallas_call(
        body, name="sgu_fwd", grid=(T // tm,), out_shape=jax.ShapeDtypeStruct((T, SGU_W), BF16),
        in_specs=[pl.BlockSpec((tm, 2 * SGU_W), lambda i: (i, 0)), vec, vec,
                  pl.BlockSpec((8, BLK, BLK), lambda i: (0, 0, 0)), pl.BlockSpec((BLK, 8), lambda i: (0, 0))],
        out_specs=pl.BlockSpec((tm, SGU_W), lambda i: (i, 0)),
        scratch_shapes=[pltpu.VMEM((8, BLK, BLK), BF16), pltpu.VMEM((BLK, SGU_W), F32)],
        compiler_params=_params(("arbitrary",), 32),
    )(uvg, ln_g, ln_b, w_s, b_t)


def _merge_fwd(attn, sgu, wpa_t, wps_t, uvg, tm):
    T = attn.shape[0]

    def body(attn_ref, sgu_ref, wpa_ref, wps_ref, ga_ref, gb_ref, pa_ref, ps_ref, m_ref):
        pa = _nt(attn_ref[...].astype(BF16), wpa_ref[...])
        ps = _nt(sgu_ref[...], wps_ref[...])
        pa_ref[...] = pa
        ps_ref[...] = ps
        m_ref[...] = (jax.nn.sigmoid(ga_ref[...]) * pa + jax.nn.sigmoid(gb_ref[...]) * ps).astype(BF16)

    half = pl.BlockSpec((tm, ATTN_W), lambda i: (i, 0))
    wide = pl.BlockSpec((tm, D_MODEL), lambda i: (i, 0))
    w = pl.BlockSpec((D_MODEL, ATTN_W), lambda i: (0, 0))
    res = jax.ShapeDtypeStruct((T, D_MODEL), F32)
    return pl.pallas_call(
        body, name="merge_fwd", grid=(T // tm,),
        out_shape=[res, res, jax.ShapeDtypeStruct((T, D_MODEL), BF16)],
        in_specs=[half, half, w, w, pl.BlockSpec((tm, D_MODEL), lambda i: (i, 1)),
                  pl.BlockSpec((tm, D_MODEL), lambda i: (i, 2))],
        out_specs=[wide, wide, wide], compiler_params=_params(("parallel",), 40),
    )(attn, sgu, wpa_t, wps_t, uvg, uvg)


def _out_residual(merged, w_out, x, tm):
    T = x.shape[0]

    def body(m_ref, w_ref, x_ref, o_ref):
        o_ref[...] = x_ref[...] + _nn(m_ref[...], w_ref[...])

    wide = pl.BlockSpec((tm, D_MODEL), lambda i: (i, 0))
    return pl.pallas_call(
        body, name="out_residual", grid=(T // tm,), out_shape=jax.ShapeDtypeStruct((T, D_MODEL), F32),
        in_specs=[wide, pl.BlockSpec((D_MODEL, D_MODEL), lambda i: (0, 0)), wide], out_specs=wide,
        compiler_params=_params(("parallel",), 32),
    )(merged, w_out, x)


def _ffn_fwd(h2, wg_t, wu_t, tm):
    T = h2.shape[0]

    def body(h_ref, wg_ref, wu_ref, gate_ref, up_ref, ff_ref):
        h = h_ref[...]
        gate, up = _nt(h, wg_ref[...]), _nt(h, wu_ref[...])
        gate_ref[...] = gate.astype(BF16)
        up_ref[...] = up.astype(BF16)
        ff_ref[...] = (gate * jax.nn.sigmoid(gate) * up).astype(BF16)

    w = pl.BlockSpec((FF_TILE, D_MODEL), lambda i, j: (j, 0))
    o = pl.BlockSpec((tm, FF_TILE), lambda i, j: (i, j))
    res = jax.ShapeDtypeStruct((T, D_FF), BF16)
    return pl.pallas_call(
        body, name="ffn_fwd", grid=(T // tm, D_FF // FF_TILE), out_shape=[res, res, res],
        in_specs=[pl.BlockSpec((tm, D_MODEL), lambda i, j: (i, 0)), w, w], out_specs=[o, o, o],
        compiler_params=_params(("parallel", "arbitrary"), 48),
    )(h2, wg_t, wu_t)


def _down_loss(ff, w_down, x1, final_g, target, tm):
    T = x1.shape[0]
    nk = D_FF // FF_TILE

    def body(ff_ref, w_ref, x1_ref, g_ref, t_ref, dx_ref, dxb_ref, loss_ref, dg_ref, acc_ref):
        i, k = pl.program_id(0), pl.program_id(1)

        @pl.when((i == 0) & (k == 0))
        def _():
            loss_ref[...] = jnp.zeros_like(loss_ref)
            dg_ref[...] = jnp.zeros_like(dg_ref)

        @pl.when(k == 0)
        def _():
            acc_ref[...] = x1_ref[...]

        acc_ref[...] += _nn(ff_ref[...], w_ref[...])

        @pl.when(k == nk - 1)
        def _():
            x2 = acc_ref[...]
            g = g_ref[...]
            r = lax.rsqrt(jnp.mean(x2 * x2, axis=-1, keepdims=True) + EPS)
            xh = x2 * r
            err = xh * g - t_ref[...]
            loss_ref[...] += jnp.sum(err * err, axis=0, keepdims=True) * (0.5 / D_MODEL)
            dy = err * (1.0 / D_MODEL)
            dg_ref[...] += jnp.sum(dy * xh, axis=0, keepdims=True)
            dxh = dy * g
            dx = r * (dxh - xh * jnp.mean(dxh * xh, axis=-1, keepdims=True))
            dx_ref[...] = dx
            dxb_ref[...] = dx.astype(BF16)

    wide = pl.BlockSpec((tm, D_MODEL), lambda i, k: (i, 0))
    vec = pl.BlockSpec((1, D_MODEL), lambda i, k: (0, 0))
    vec_shape = jax.ShapeDtypeStruct((1, D_MODEL), F32)
    return pl.pallas_call(
        body, name="down_loss", grid=(T // tm, nk),
        out_shape=[jax.ShapeDtypeStruct((T, D_MODEL), F32), jax.ShapeDtypeStruct((T, D_MODEL), BF16),
                   vec_shape, vec_shape],
        in_specs=[pl.BlockSpec((tm, FF_TILE), lambda i, k: (i, k)),
                  pl.BlockSpec((FF_TILE, D_MODEL), lambda i, k: (k, 0)), wide, vec, wide],
        out_specs=[wide, wide, vec, vec], scratch_shapes=[pltpu.VMEM((tm, D_MODEL), F32)],
        compiler_params=_params(("arbitrary", "arbitrary"), 48),
    )(ff, w_down, x1, final_g, target)


def _ffn_bwd_gates(dx2b, w_down, gate, up, tm):
    T = dx2b.shape[0]

    def body(dx_ref, w_ref, gate_ref, up_ref, dgate_ref, dup_ref):
        dff = _nt(dx_ref[...], w_ref[...])
        gate, up = gate_ref[...].astype(F32), up_ref[...].astype(F32)
        sg = jax.nn.sigmoid(gate)
        dgate_ref[...] = (dff * up * sg * (1.0 + gate * (1.0 - sg))).astype(BF16)
        dup_ref[...] = (dff * gate * sg).astype(BF16)

    o = pl.BlockSpec((tm, FF_TILE), lambda i, j: (i, j))
    res = jax.ShapeDtypeStruct((T, D_FF), BF16)
    return pl.pallas_call(
        body, name="ffn_bwd_gates", grid=(T // tm, D_FF // FF_TILE), out_shape=[res, res],
        in_specs=[pl.BlockSpec((tm, D_MODEL), lambda i, j: (i, 0)),
                  pl.BlockSpec((FF_TILE, D_MODEL), lambda i, j: (j, 0)), o, o],
        out_specs=[o, o], compiler_params=_params(("parallel", "arbitrary"), 48),
    )(dx2b, w_down, gate, up)


def _mm_tn(a, b, *, tmm, tk, name, into=None, row_block0=0, rows_total=None):
    T, M = a.shape
    N = b.shape[1]
    rows_total = M if rows_total is None else rows_total

    def body(*refs):
        a_ref, b_ref, o_ref = refs[0], refs[1], refs[-1]

        @pl.when(pl.program_id(1) == 0)
        def _():
            o_ref[...] = jnp.zeros_like(o_ref)

        o_ref[...] += _tn(a_ref[...], b_ref[...])

    ins = [a, b] + ([] if into is None else [into])
    specs = [pl.BlockSpec((tk, tmm), lambda i, k: (k, i)), pl.BlockSpec((tk, N), lambda i, k: (k, 0))]
    return pl.pallas_call(
        body, name=name, grid=(M // tmm, T // tk),
        out_shape=jax.ShapeDtypeStruct((rows_total, N), F32),
        in_specs=specs + ([] if into is None else [ANY]),
        out_specs=pl.BlockSpec((tmm, N), lambda i, k: (row_block0 + i, 0)),
        input_output_aliases={} if into is None else {2: 0},
        compiler_params=_params(("parallel", "arbitrary"), 48),
    )(*ins)


def _mm_nn_norm_bwd(a_list, b_list, spans, tk, x_in, g, d_res, name, tm):
    T = x_in.shape[0]
    npairs = len(a_list)
    nk = max(s[1] for s in spans)

    def body(*refs):
        a_refs, b_refs = refs[:npairs], refs[npairs:2 * npairs]
        x_ref, g_ref, d_ref, dx_ref, dxb_ref, dg_ref, acc_ref = refs[2 * npairs:]
        i, k = pl.program_id(0), pl.program_id(1)

        @pl.when((i == 0) & (k == 0))
        def _():
            dg_ref[...] = jnp.zeros_like(dg_ref)

        @pl.when(k == 0)
        def _():
            acc_ref[...] = jnp.zeros_like(acc_ref)

        for p in range(npairs):
            @pl.when((k >= spans[p][0]) & (k < spans[p][1]))
            def _(p=p):
                acc_ref[...] += _nn(a_refs[p][...], b_refs[p][...])

        @pl.when(k == nk - 1)
        def _():
            xin, gg, dh = x_ref[...], g_ref[...], acc_ref[...]
            r = lax.rsqrt(jnp.mean(xin * xin, axis=-1, keepdims=True) + EPS)
            xh = xin * r
            dg_ref[...] += jnp.sum(dh * xh, axis=0, keepdims=True)
            dxh = dh * gg
            dx = d_ref[...] + r * (dxh - xh * jnp.mean(dxh * xh, axis=-1, keepdims=True))
            dx_ref[...] = dx
            dxb_ref[...] = dx.astype(BF16)

    def a_spec(p):
        lo, hi, _ = spans[p]
        return pl.BlockSpec((tm, tk), lambda i, k: (i, jnp.clip(k - lo, 0, hi - lo - 1)))

    def b_spec(p):
        lo, hi, off = spans[p]
        return pl.BlockSpec((tk, D_MODEL), lambda i, k: (off + jnp.clip(k - lo, 0, hi - lo - 1), 0))

    wide = pl.BlockSpec((tm, D_MODEL), lambda i, k: (i, 0))
    vec = pl.BlockSpec((1, D_MODEL), lambda i, k: (0, 0))
    return pl.pallas_call(
        body, name=name, grid=(T // tm, nk),
        out_shape=[jax.ShapeDtypeStruct((T, D_MODEL), F32), jax.ShapeDtypeStruct((T, D_MODEL), BF16),
                   jax.ShapeDtypeStruct((1, D_MODEL), F32)],
        in_specs=[a_spec(p) for p in range(npairs)] + [b_spec(p) for p in range(npairs)] + [wide, vec, wide],
        out_specs=[wide, wide, vec], scratch_shapes=[pltpu.VMEM((tm, D_MODEL), F32)],
        compiler_params=_params(("arbitrary", "arbitrary"), 48),
    )(*a_list, *b_list, x_in, g, d_res)


def _merge_bwd(dx1b, w_out, pa, ps, uvg, tm):
    T = dx1b.shape[0]

    def body(dx_ref, w_ref, pa_ref, ps_ref, ga_ref, gb_ref, dpa_ref, dps_ref, dg_ref):
        dm = _nt(dx_ref[...], w_ref[...])
        ga, gb = jax.nn.sigmoid(ga_ref[...]), jax.nn.sigmoid(gb_ref[...])
        dpa_ref[...] = (dm * ga).astype(BF16)
        dps_ref[...] = (dm * gb).astype(BF16)
        dg_ref[:, :D_MODEL] = (dm * pa_ref[...] * ga * (1.0 - ga)).astype(BF16)
        dg_ref[:, D_MODEL:] = (dm * ps_ref[...] * gb * (1.0 - gb)).astype(BF16)

    wide = pl.BlockSpec((tm, D_MODEL), lambda i: (i, 0))
    res = jax.ShapeDtypeStruct((T, D_MODEL), BF16)
    return pl.pallas_call(
        body, name="merge_bwd", grid=(T // tm,),
        out_shape=[res, res, jax.ShapeDtypeStruct((T, 2 * D_MODEL), BF16)],
        in_specs=[wide, pl.BlockSpec((D_MODEL, D_MODEL), lambda i: (0, 0)), wide, wide,
                  pl.BlockSpec((tm, D_MODEL), lambda i: (i, 1)), pl.BlockSpec((tm, D_MODEL), lambda i: (i, 2))],
        out_specs=[wide, wide, pl.BlockSpec((tm, 2 * D_MODEL), lambda i: (i, 0))],
        compiler_params=_params(("parallel",), 48),
    )(dx1b, w_out, pa, ps, uvg, uvg)


def _proj_bwd(dpa, dps, wpa_t, wps_t, tm):
    T = dpa.shape[0]

    def body(dpa_ref, dps_ref, wpa_ref, wps_ref, dattn_ref, dsgu_ref):
        dattn_ref[...] = _nn(dpa_ref[...], wpa_ref[...])
        dsgu_ref[...] = _nn(dps_ref[...], wps_ref[...])

    wide = pl.BlockSpec((tm, D_MODEL), lambda i: (i, 0))
    half = pl.BlockSpec((tm, ATTN_W), lambda i: (i, 0))
    w = pl.BlockSpec((D_MODEL, ATTN_W), lambda i: (0, 0))
    res = jax.ShapeDtypeStruct((T, ATTN_W), F32)
    return pl.pallas_call(
        body, name="proj_bwd", grid=(T // tm,), out_shape=[res, res], in_specs=[wide, wide, w, w],
        out_specs=[half, half], compiler_params=_params(("parallel",), 32),
    )(dpa, dps, wpa_t, wps_t)


def _sgu_bwd(uvg, dsgu, ln_g, ln_b, w_s, b_t, tm):
    T = uvg.shape[0]
    nsteps = T // tm

    def body(uv_ref, ds_ref, g_ref, b_ref, w_ref, bt_ref, duv_ref, dw_ref, dbt_ref, dg_ref, db_ref,
             wc_ref, wct_ref, bias_ref, dbias_ref):
        step = pl.program_id(0)
        head0, head1 = _head_lanes()

        @pl.when(step == 0)
        def _():
            causal = _causal()
            for g in range(8):
                wc = jnp.where(causal, w_ref[g], 0.0)
                wc_ref[g] = wc.astype(BF16)
                wct_ref[g] = wc.T.astype(BF16)
            bias_ref[...] = _bias_lanes(bt_ref[...])
            dbias_ref[...] = jnp.zeros_like(dbias_ref)
            dw_ref[...] = jnp.zeros_like(dw_ref)
            dg_ref[...] = jnp.zeros_like(dg_ref)
            db_ref[...] = jnp.zeros_like(db_ref)

        uv = uv_ref[...]
        ln_gain = g_ref[...]
        u, xh, rstd, vn = _sgu_normalise(uv, ln_gain, b_ref[...])
        vb = vn.astype(BF16)
        mixed = _sgu_mix(wc_ref, vb, head0) + jnp.tile(bias_ref[...], (tm // BLK, 1))
        dout = ds_ref[...]
        du = dout * mixed
        dmixed = dout * u
        dmb = dmixed.astype(BF16)
        dvn_chunks = []
        for ch in range(tm // BLK):
            rows = slice(ch * BLK, (ch + 1) * BLK)
            dbias_ref[...] += dmixed[rows]
            pairs = []
            for p in range(SGU_W // HEAD_PAIR):
                lanes = slice(p * HEAD_PAIR, (p + 1) * HEAD_PAIR)
                dm_pair, v_pair = dmb[rows, lanes], vb[rows, lanes]
                acc = jnp.zeros((BLK, HEAD_PAIR), F32)
                for hh, half in enumerate((head0, head1)):
                    dm_h = jnp.where(half, dm_pair, jnp.zeros_like(dm_pair))
                    dw_ref[2 * p + hh] += _nt(dm_h, v_pair)
                    acc += _nn(wct_ref[2 * p + hh], dm_h)
                pairs.append(acc)
            dvn_chunks.append(jnp.concatenate(pairs, axis=1))
        dvn = jnp.concatenate(dvn_chunks, axis=0)
        dg_ref[...] += jnp.sum(dvn * xh, axis=0, keepdims=True)
        db_ref[...] += jnp.sum(dvn, axis=0, keepdims=True)
        dxh = dvn * ln_gain
        dv = rstd * (dxh - jnp.mean(dxh, axis=-1, keepdims=True) - xh * jnp.mean(dxh * xh, axis=-1, keepdims=True))
        dgelu = _gelu_grad(uv)
        duv_ref[:, :SGU_W] = (du * dgelu[:, :SGU_W]).astype(BF16)
        duv_ref[:, SGU_W:] = (dv * dgelu[:, SGU_W:]).astype(BF16)

        @pl.when(step == nsteps - 1)
        def _():
            causal = _causal()
            for g in range(8):
                dw_ref[g] = jnp.where(causal, dw_ref[g], 0.0)
            grp = lax.broadcasted_iota(jnp.int32, (1, SGU_W), 1) // 64
            col = lax.broadcasted_iota(jnp.int32, (1, 8), 1)
            dbias = dbias_ref[...]
            out = jnp.zeros((BLK, 8), F32)
            for g in range(8):
                s = jnp.sum(jnp.where(grp == g, dbias, 0.0), axis=1, keepdims=True)
                out = jnp.where(col == g, s, out)
            dbt_ref[...] = out

    vec = pl.BlockSpec((1, SGU_W), lambda i: (0, 0))
    w3 = pl.BlockSpec((8, BLK, BLK), lambda i: (0, 0, 0))
    bt = pl.BlockSpec((BLK, 8), lambda i: (0, 0))
    return pl.pallas_call(
        body, name="sgu_bwd", grid=(nsteps,),
        out_shape=[jax.ShapeDtypeStruct((T, 2 * SGU_W), BF16), jax.ShapeDtypeStruct((8, BLK, BLK), F32),
                   jax.ShapeDtypeStruct((BLK, 8), F32), jax.ShapeDtypeStruct((1, SGU_W), F32),
                   jax.ShapeDtypeStruct((1, SGU_W), F32)],
        in_specs=[pl.BlockSpec((tm, 2 * SGU_W), lambda i: (i, 0)), pl.BlockSpec((tm, SGU_W), lambda i: (i, 0)),
                  vec, vec, w3, bt],
        out_specs=[pl.BlockSpec((tm, 2 * SGU_W), lambda i: (i, 0)), w3, bt, vec, vec],
        scratch_shapes=[pltpu.VMEM((8, BLK, BLK), BF16), pltpu.VMEM((8, BLK, BLK), BF16),
                        pltpu.VMEM((BLK, SGU_W), F32), pltpu.VMEM((BLK, SGU_W), F32)],
        compiler_params=_params(("arbitrary",), 40),
    )(uvg, dsgu, ln_g, ln_b, w_s, b_t)


def _attn_bwd_prepare(dattn, attn, lse, tm):
    T = dattn.shape[0]

    def body(da_ref, at_ref, lse_ref, *outs):
        scr_da, scr_lse, scr_d = outs[-3:]
        outs = outs[:-3]
        head0, _ = _head_lanes()
        for s in range(N_SLABS):
            lanes = slice(s * HEAD_PAIR, (s + 1) * HEAD_PAIR)
            da = da_ref[:, lanes]
            pp = da * at_ref[:, lanes]
            d0 = jnp.sum(jnp.where(head0, pp, 0.0), axis=1, keepdims=True)
            d1 = jnp.sum(jnp.where(head0, 0.0, pp), axis=1, keepdims=True)
            d_pair = jnp.where(head0, d0, d1)
            scr_da[s] = da
            scr_lse[s] = lse_ref[:, lanes]
            scr_d[s] = d_pair
            outs[0][0, :, lanes] = da.astype(BF16)
            outs[1][0, :, lanes] = lse_ref[:, lanes]
            outs[2][0, :, lanes] = d_pair
        for g, r in enumerate(DILATIONS):
            if r > 1:
                for dst, scr, dt in zip(outs[3 * g:3 * g + 3], (scr_da, scr_lse, scr_d), (BF16, F32, F32)):
                    _rows_by_residue(dst, scr, r, tm // r, dt)

    tok = pl.BlockSpec((tm, ATTN_W), lambda i: (i, 0))
    shapes, specs = [], []
    for r in DILATIONS:
        spec = pl.BlockSpec((r, tm // r, ATTN_W), lambda i: (0, i, 0))
        shapes += [jax.ShapeDtypeStruct((r, T // r, ATTN_W), dt) for dt in (BF16, F32, F32)]
        specs += [spec] * 3
    return pl.pallas_call(
        body, name="attn_bwd_prepare", grid=(T // tm,), out_shape=shapes, in_specs=[tok, tok, tok],
        out_specs=specs, scratch_shapes=[_slab_scratch(tm)] * 3,
        compiler_params=_params(("parallel",), 40),
    )(dattn, attn, lse)


def _attn_bwd(qkv, dattn, lse, dsum, group):
    r, L, _ = qkv.shape
    nb = L // BLK

    def body(q_ref, kp_ref, kc_ref, vp_ref, vc_ref, da_ref, lse_ref, d_ref, dq_ref, dk_ref, dv_ref,
             carry_k, carry_v):
        n = pl.program_id(1)

        @pl.when(n == 0)
        def _():
            carry_k[...] = jnp.zeros_like(carry_k)
            carry_v[...] = jnp.zeros_like(carry_v)

        @pl.when(n < nb)
        def _():
            valid = _band_mask(n)
            head0, head1 = _head_lanes()
            for p in range(ATTN_W // HEAD_PAIR):
                sl = slice(p * HEAD_PAIR, (p + 1) * HEAD_PAIR)
                q, da = q_ref[:, sl], da_ref[:, sl]
                k2 = jnp.concatenate([kp_ref[:, sl], kc_ref[:, sl]], axis=0)
                v2 = jnp.concatenate([vp_ref[:, sl], vc_ref[:, sl]], axis=0)
                lse_pair, d_pair = lse_ref[:, sl], d_ref[:, sl]
                dq = jnp.zeros((BLK, HEAD_PAIR), F32)
                dk2 = jnp.zeros((2 * BLK, HEAD_PAIR), F32)
                dv2 = jnp.zeros((2 * BLK, HEAD_PAIR), F32)
                for hh, half in enumerate((head0, head1)):
                    col = 64 * hh
                    q_h = jnp.where(half, q, jnp.zeros_like(q))
                    da_h = jnp.where(half, da, jnp.zeros_like(da))
                    s = _nt(q_h, k2)
                    prob = jnp.where(valid, jnp.exp(s - lse_pair[:, col:col + 1]), 0.0)
                    dprob = _nt(da_h, v2)
                    ds = (prob * (dprob - d_pair[:, col:col + 1])).astype(BF16)
                    dq += jnp.where(half, _nn(ds, k2), 0.0)
                    dk2 += _tn(ds, q_h)
                    dv2 += _tn(prob.astype(BF16), da_h)
                dq_ref[:, sl] = dq
                dk_ref[:, sl] = carry_k[:, sl] + dk2[:BLK]
                dv_ref[:, sl] = carry_v[:, sl] + dv2[:BLK]
                carry_k[:, sl] = dk2[BLK:]
                carry_v[:, sl] = dv2[BLK:]

        @pl.when(n == nb)
        def _():
            dk_ref[...] = carry_k[...]
            dv_ref[...] = carry_v[...]

    def cur(kind):
        return pl.BlockSpec((None, BLK, ATTN_W), lambda rho, n: (rho, jnp.minimum(n, nb - 1), kind))

    def prev(kind):
        return pl.BlockSpec((None, BLK, ATTN_W), lambda rho, n: (rho, jnp.clip(n - 1, 0, nb - 1), kind))

    res = jax.ShapeDtypeStruct((r, L, ATTN_W), F32)
    return pl.pallas_call(
        body, name=f"attn_bwd_g{group}", grid=(r, nb + 1), out_shape=[res, res, res],
        in_specs=[cur(0), prev(1), cur(1), prev(2), cur(2), cur(0), cur(0), cur(0)],
        out_specs=[cur(0), prev(0), prev(0)],
        scratch_shapes=[pltpu.VMEM((BLK, ATTN_W), F32), pltpu.VMEM((BLK, ATTN_W), F32)],
        compiler_params=_params(("parallel", "arbitrary"), 32),
    )(qkv, qkv, qkv, qkv, qkv, dattn, lse, dsum)


def _dqkv_token_order(dqkv_groups, tables, tm):
    T = tables[0].shape[0]

    def body(*refs):
        ins = refs[:9]
        cos_ref, s1_ref, s2_ref, o_ref, scr = refs[9:]
        cos, s1, s2 = cos_ref[...], s1_ref[...], s2_ref[...]
        for g, r in enumerate(DILATIONS):
            for kind in range(3):
                src = ins[3 * g + kind]
                if r > 1:
                    _rows_by_token(scr, src, r, tm // r)
                for s in range(N_SLABS):
                    val = scr[s] if r > 1 else src[0, :, s * HEAD_PAIR:(s + 1) * HEAD_PAIR]
                    if kind < 2:
                        val = _rope_transposed(val, cos, s1, s2)
                    if kind == 0:
                        val = val * 0.125
                    at = (3 * kind + g) * ATTN_W + s * HEAD_PAIR
                    o_ref[:, at:at + HEAD_PAIR] = val.astype(BF16)

    specs = []
    for r in DILATIONS:
        specs += [pl.BlockSpec((r, tm // r, ATTN_W), lambda i: (0, i, 0))] * 3
    row = pl.BlockSpec((tm, HEAD_PAIR), lambda i: (i, 0))
    flat = [a for grp in dqkv_groups for a in grp]
    return pl.pallas_call(
        body, name="dqkv_token_order", grid=(T // tm,), out_shape=jax.ShapeDtypeStruct((T, QKV_COLS), BF16),
        in_specs=specs + [row] * 3, out_specs=pl.BlockSpec((tm, QKV_COLS), lambda i: (i, 0)),
        scratch_shapes=[_slab_scratch(tm)], compiler_params=_params(("parallel",), 48),
    )(*flat, *tables)


def _row_tile(rows):
    for cand in (320, 256, 176, 128):
        if rows % cand == 0:
            return cand
    return rows


def _pair_sum(grad4, recv, chip):
    _, _, rows, cols = grad4.shape
    tr = _row_tile(rows)

    def body(ids_ref, g_ref, r_ref, gown_ref, rown_ref, sum_ref, own_ref):
        sum_ref[...] = (g_ref[...] + r_ref[...]).astype(BF16)

        @pl.when(pl.program_id(1) == 0)
        def _():
            own_ref[...] = gown_ref[...] + rown_ref[...]

    grid_spec = pltpu.PrefetchScalarGridSpec(
        num_scalar_prefetch=1, grid=(rows // tr, 4),
        in_specs=[pl.BlockSpec((None, None, tr, cols), lambda i, q, ids: (q, ids[1], i, 0)),
                  pl.BlockSpec((None, tr, cols), lambda i, q, ids: (q, i, 0)),
                  pl.BlockSpec((None, None, tr, cols), lambda i, q, ids: (ids[0], ids[1], i, 0)),
                  pl.BlockSpec((None, tr, cols), lambda i, q, ids: (ids[0], i, 0))],
        out_specs=[pl.BlockSpec((None, tr, cols), lambda i, q, ids: (q, i, 0)),
                   pl.BlockSpec((tr, cols), lambda i, q, ids: (i, 0))])
    return pl.pallas_call(
        body, name=f"grad_pair_sum_{rows}x{cols}", grid_spec=grid_spec,
        out_shape=[jax.ShapeDtypeStruct((4, rows, cols), BF16), jax.ShapeDtypeStruct((rows, cols), F32)],
        compiler_params=_params(("arbitrary", "arbitrary"), 32),
    )(chip, grad4, recv, grad4, recv)


def _chip_sum(own, others, name):
    rows, cols = own.shape
    tr = _row_tile(rows)

    def body(own_ref, oth_ref, o_ref):
        total = own_ref[...]
        for j in range(3):
            total = total + oth_ref[j].astype(F32)
        o_ref[...] = total

    blk = pl.BlockSpec((tr, cols), lambda i: (i, 0))
    return pl.pallas_call(
        body, name=name, grid=(rows // tr,), out_shape=jax.ShapeDtypeStruct((rows, cols), F32),
        in_specs=[blk, pl.BlockSpec((3, tr, cols), lambda i: (0, i, 0))], out_specs=blk,
        compiler_params=_params(("parallel",), 32),
    )(own, others)


def _adam_math(w, g, m, v):
    m = ADAM_B1 * m + (1.0 - ADAM_B1) * g
    v = ADAM_B2 * v + (1.0 - ADAM_B2) * (g * g)
    m_hat = m / (1.0 - ADAM_B1 ** ADAM_STEP)
    v_hat = v / (1.0 - ADAM_B2 ** ADAM_STEP)
    delta = -ADAM_LR * (m_hat / (jnp.sqrt(v_hat) + ADAM_EPS) + ADAM_WD * w)
    return delta, m, v


def _adamw(w, g, m, v, name):
    rows, cols = w.shape
    tr = _row_tile(rows)

    def body(w_ref, g_ref, m_ref, v_ref, d_ref, nm_ref, nv_ref):
        d_ref[...], nm_ref[...], nv_ref[...] = _adam_math(w_ref[...], g_ref[...], m_ref[...], v_ref[...])

    blk = pl.BlockSpec((tr, cols), lambda i: (i, 0))
    res = jax.ShapeDtypeStruct((rows, cols), F32)
    return pl.pallas_call(
        body, name=name, grid=(rows // tr,), out_shape=[res, res, res], in_specs=[blk] * 4, out_specs=[blk] * 3,
        compiler_params=_params(("parallel",), 32),
    )(w, g, m, v)


def _small_update(parts, w, m, v, loss_rows):
    rows = w.shape[0]

    def body(p_ref, w_ref, m_ref, v_ref, g_ref, d_ref, nm_ref, nv_ref, loss_ref):
        g = p_ref[0:rows, :]
        for dev in range(1, N_DEV):
            g = g + p_ref[dev * rows:(dev + 1) * rows, :]
        g_ref[...] = g
        d_ref[...], nm_ref[...], nv_ref[...] = _adam_math(w_ref[...], g, m_ref[...], v_ref[...])
        loss_ref[...] = jnp.sum(jnp.sum(g[rows - loss_rows:, :], axis=1, keepdims=True), axis=0, keepdims=True)

    res = jax.ShapeDtypeStruct((rows, HEAD_PAIR), F32)
    return pl.pallas_call(
        body, name="small_update", out_shape=[res, res, res, res, jax.ShapeDtypeStruct((1, 1), F32)],
        compiler_params=pltpu.CompilerParams(vmem_limit_bytes=32 * MIB),
    )(parts, w, m, v)


def kernel(x, positions, norm1_g, w_in, sgu_ln_g, sgu_ln_b, w_spatial, b_spatial, w_proj_attn, w_proj_sgu, w_out, norm2_g, w_ffn_gate, w_ffn_up, w_ffn_down, final_g, loss_target, m_norm1_g, m_w_in, m_sgu_ln_g, m_sgu_ln_b, m_w_spatial, m_b_spatial, m_w_proj_attn, m_w_proj_sgu, m_w_out, m_norm2_g, m_w_ffn_gate, m_w_ffn_up, m_w_ffn_down, m_final_g, v_norm1_g, v_w_in, v_sgu_ln_g, v_sgu_ln_b, v_w_spatial, v_b_spatial, v_w_proj_attn, v_w_proj_sgu, v_w_out, v_norm2_g, v_w_ffn_gate, v_w_ffn_up, v_w_ffn_down, v_final_g):
    T = x.shape[1]
    tm = 512
    xt = x[0]
    target = loss_target[0]
    chip = jnp.stack([2 * lax.axis_index("x") + lax.axis_index("y"), lax.axis_index("c")]).astype(jnp.int32)

    shards = [jnp.transpose(w_in[0]).astype(BF16), jnp.transpose(w_proj_attn[0]).astype(BF16),
              jnp.transpose(w_proj_sgu[0]).astype(BF16), w_out[0].astype(BF16),
              jnp.transpose(w_ffn_gate[0]).astype(BF16), jnp.transpose(w_ffn_up[0]).astype(BF16),
              w_ffn_down[0].astype(BF16)]
    win_t, wpa_t, wps_t, wout, wg_t, wu_t, wd = _all_gather_rows(shards, "weights_all_gather")

    inv_freq = ROPE_THETA ** (-jnp.arange(0, 2 * ROPE_HALF, 2, dtype=F32) / (2 * ROPE_HALF))
    inv_freq_row = jnp.tile(jnp.concatenate([inv_freq, inv_freq, jnp.zeros((48,), F32)]), 2).reshape(1, HEAD_PAIR)
    tables = _rope_tables(positions.reshape(T, 1), inv_freq_row, tm)
    b_t = jnp.transpose(b_spatial[0])

    h = _rmsnorm_fwd(xt, norm1_g, "norm1_fwd", tm)
    qkv = [_qkv_proj(h, win_t, tables, g, tm) for g in range(3)]
    uvg = _mm_nt(h, win_t, row_block0=QKV_COLS // 768, n_blocks=4, tn=768, tm=tm, out_dtype=F32, name="uv_gate_proj")
    fwd = [_attn_fwd(qkv[g], g) for g in range(3)]
    attn, lse = _attn_combine([f[0] for f in fwd], [f[1] for f in fwd], tm)
    sgu = _sgu_fwd(uvg, sgu_ln_g, sgu_ln_b, w_spatial[0], b_t, tm)
    pa, ps, merged = _merge_fwd(attn, sgu, wpa_t, wps_t, uvg, tm)
    x1 = _out_residual(merged, wout, xt, tm)
    h2 = _rmsnorm_fwd(x1, norm2_g, "norm2_fwd", tm)
    gate, up, ff = _ffn_fwd(h2, wg_t, wu_t, tm)
    dx2, dx2b, loss_cols, d_final_g = _down_loss(ff, wd, x1, final_g.reshape(1, D_MODEL), target, tm)

    dgate, dup = _ffn_bwd_gates(dx2b, wd, gate, up, tm)
    tk = min(1024, T)
    d_wd = _mm_tn(ff, dx2b, tmm=FF_TILE, tk=tk, name="grad_w_ffn_down")
    d_wg_t = _mm_tn(dgate, h2, tmm=FF_TILE, tk=tk, name="grad_w_ffn_gate")
    d_wu_t = _mm_tn(dup, h2, tmm=FF_TILE, tk=tk, name="grad_w_ffn_up")
    nkf = D_FF // FF_TILE
    dx1, dx1b, d_norm2 = _mm_nn_norm_bwd([dgate, dup], [wg_t, wu_t], [(0, nkf, 0), (nkf, 2 * nkf, 0)], FF_TILE,
                                         x1, norm2_g, dx2, "ffn_bwd_norm2", tm)
    dpa, dps, dgates = _merge_bwd(dx1b, wout, pa, ps, uvg, tm)
    d_wout = _mm_tn(merged, dx1b, tmm=D_MODEL, tk=tk, name="grad_w_out")
    dattn, dsgu = _proj_bwd(dpa, dps, wpa_t, wps_t, tm)
    d_wpa_t = _mm_tn(dpa, attn.astype(BF16), tmm=D_MODEL, tk=tk, name="grad_w_proj_attn")
    d_wps_t = _mm_tn(dps, sgu, tmm=D_MODEL, tk=tk, name="grad_w_proj_sgu")
    duv, d_ws, d_bs_t, d_ln_g, d_ln_b = _sgu_bwd(uvg, dsgu, sgu_ln_g, sgu_ln_b, w_spatial[0], b_t, tm)
    prep = _attn_bwd_prepare(dattn, attn, lse, tm)
    dqkv_groups = [_attn_bwd(qkv[g], prep[3 * g], prep[3 * g + 1], prep[3 * g + 2], g) for g in range(3)]
    dqkv = _dqkv_token_order(dqkv_groups, tables, tm)
    dx, _, d_norm1 = _mm_nn_norm_bwd([dqkv, duv, dgates], [win_t, win_t, win_t],
                                     [(0, 9, 0), (9, 11, 9), (11, 15, 11)], ATTN_W,
                                     xt, norm1_g, dx1, "in_bwd_norm1", tm)
    in_cols = win_t.shape[0]
    d_win_t = _mm_tn(dqkv, h, tmm=1536, tk=tk, name="grad_w_in_qkv", rows_total=in_cols)
    d_win_t = _mm_tn(duv, h, tmm=512, tk=tk, name="grad_w_in_uv", into=d_win_t, row_block0=9, rows_total=in_cols)
    d_win_t = _mm_tn(dgates, h, tmm=512, tk=tk, name="grad_w_in_gates", into=d_win_t, row_block0=11, rows_total=in_cols)

    big = [d_win_t, d_wpa_t, d_wps_t, d_wout, d_wg_t, d_wu_t, d_wd]
    big4 = [g.reshape(4, 2, g.shape[0] // N_DEV, g.shape[1]) for g in big]
    from_sibling = _pair_exchange(big4)
    sums, owns = [], []
    for g4, rv in zip(big4, from_sibling):
        s, o = _pair_sum(g4, rv, chip)
        sums.append(s)
        owns.append(o)
    from_chips = _chip_exchange(sums)
    names = ["w_in", "w_proj_attn", "w_proj_sgu", "w_out", "w_ffn_gate", "w_ffn_up", "w_ffn_down"]
    reduced = [_chip_sum(o, f, "grad_total_" + nm) for o, f, nm in zip(owns, from_chips, names)]
    transposed = (True, True, True, False, True, True, False)
    g_big = [jnp.transpose(r) if t else r for r, t in zip(reduced, transposed)]

    small_w = [norm1_g, sgu_ln_g, sgu_ln_b, w_spatial, b_spatial, norm2_g, final_g]
    small_m = [m_norm1_g, m_sgu_ln_g, m_sgu_ln_b, m_w_spatial, m_b_spatial, m_norm2_g, m_final_g]
    small_v = [v_norm1_g, v_sgu_ln_g, v_sgu_ln_b, v_w_spatial, v_b_spatial, v_norm2_g, v_final_g]
    small_g = [d_norm1, d_ln_g, d_ln_b, d_ws, jnp.transpose(d_bs_t), d_norm2, d_final_g]
    zeros = jnp.zeros((D_MODEL,), F32)

    def flat(parts, last):
        return jnp.concatenate([p.reshape(-1) for p in parts] + [last]).reshape(-1, HEAD_PAIR)

    part = flat(small_g, loss_cols.reshape(-1))
    gathered = _all_gather_rows([part], "small_grads_all_gather", vmem=True)[0]
    g_s, d_s, nm_s, nv_s, loss = _small_update(gathered, flat(small_w, zeros), flat(small_m, zeros),
                                               flat(small_v, zeros), D_MODEL // HEAD_PAIR)

    def unflat(vec):
        vec = vec.reshape(-1)
        out, at = [], 0
        for wgt in small_w:
            out.append(vec[at:at + wgt.size].reshape(wgt.shape))
            at += wgt.size
        return out

    small = [unflat(a) for a in (g_s, d_s, nm_s, nv_s)]

    big_w = [w_in, w_proj_attn, w_proj_sgu, w_out, w_ffn_gate, w_ffn_up, w_ffn_down]
    big_m = [m_w_in, m_w_proj_attn, m_w_proj_sgu, m_w_out, m_w_ffn_gate, m_w_ffn_up, m_w_ffn_down]
    big_v = [v_w_in, v_w_proj_attn, v_w_proj_sgu, v_w_out, v_w_ffn_gate, v_w_ffn_up, v_w_ffn_down]
    big_out = []
    for wgt, g, mm, vv, nm in zip(big_w, g_big, big_m, big_v, names):
        d, nm_, nv_ = _adamw(wgt[0], g, mm[0], vv[0], "adamw_" + nm)
        big_out.append([a[None] for a in (g, d, nm_, nv_)])

    small_at = {0: 0, 2: 1, 3: 2, 4: 3, 5: 4, 9: 5, 13: 6}
    big_at = {1: 0, 6: 1, 7: 2, 8: 3, 10: 4, 11: 5, 12: 6}
    outs = [loss[0, 0], dx[None]]
    for kind in range(4):
        for idx in range(14):
            outs.append(small[kind][small_at[idx]] if idx in small_at else big_out[big_at[idx]][kind])
    return tuple(outs)
```

```python
import functools
import math

import jax
import jax.numpy as jnp
from jax import lax
from jax.experimental import pallas as pl
from jax.experimental.pallas import tpu as pltpu

F32 = jnp.float32
BF16 = jnp.bfloat16

D_MODEL = 1024
HEAD_PAIR = 128
ATTN_W = 512
DILATIONS = (1, 4, 16)
BLK = 128
ROPE_HALF = 8
ROPE_THETA = 500000.0
SGU_W = 512
QKV_COLS = 4608
IN_COLS = 7680
D_FF = 2816
FF_TILE = 1408
EPS = 1e-6
N_DEV = 8
MASKED = -1e30

ADAM_LR = 0.001
ADAM_B1 = 0.9
ADAM_B2 = 0.999
ADAM_EPS = 1e-08
ADAM_WD = 0.01
ADAM_STEP = 10

MIB = 1024 * 1024
MESH = pl.DeviceIdType.MESH
ANY = pl.BlockSpec(memory_space=pl.ANY)


def _params(sem, vmem_mib):
    return pltpu.CompilerParams(dimension_semantics=sem, vmem_limit_bytes=vmem_mib * MIB)


def _nt(a, b):
    return lax.dot_general(a, b, (((1,), (1,)), ((), ())), preferred_element_type=F32)


def _nn(a, b):
    return lax.dot_general(a, b, (((1,), (0,)), ((), ())), preferred_element_type=F32)


def _tn(a, b):
    return lax.dot_general(a, b, (((0,), (0,)), ((), ())), preferred_element_type=F32)


def _all_gather_rows(shards, name, vmem=False):
    n = len(shards)

    def body(*refs):
        ins, outs = refs[:n], refs[n:2 * n]
        send_sems, recv_sems, local_sems = refs[2 * n:]
        x, y, c = lax.axis_index("x"), lax.axis_index("y"), lax.axis_index("c")
        me, sibling = (x, y, c), (x, y, 1 - c)
        chips = [(1 - x, y), (x, 1 - y), (1 - x, 1 - y)]

        def rows(m, px, py, pc):
            r = ins[m].shape[0]
            return outs[m].at[pl.ds((4 * px + 2 * py + pc) * r, r), :]

        def copy(m, k, block, to, src=None):
            return pltpu.make_async_remote_copy(
                src_ref=rows(m, *block) if src is None else src, dst_ref=rows(m, *block),
                send_sem=send_sems.at[m, k], recv_sem=recv_sems.at[m, k],
                device_id=to, device_id_type=MESH)

        mine = [pltpu.make_async_copy(ins[m], rows(m, *me), local_sems.at[m]) for m in range(n)]
        for cp in mine:
            cp.start()
        first = []
        for m in range(n):
            first.append(copy(m, 0, me, sibling, src=ins[m]))
            first += [copy(m, 1 + j, me, (*chip, c), src=ins[m]) for j, chip in enumerate(chips)]
        for cp in first:
            cp.start()
        passed = []
        for m in range(n):
            for j, chip in enumerate(chips):
                copy(m, 1 + j, (*chip, c), me).wait_recv()
                fwd = copy(m, 4 + j, (*chip, c), sibling)
                fwd.start()
                passed.append(fwd)
        for m in range(n):
            copy(m, 0, sibling, me).wait_recv()
            for j, chip in enumerate(chips):
                copy(m, 4 + j, (*chip, 1 - c), me).wait_recv()
        for cp in first + passed:
            cp.wait_send()
        for cp in mine:
            cp.wait()

    spec = pl.BlockSpec(memory_space=pltpu.VMEM) if vmem else ANY
    return pl.pallas_call(
        body, name=name,
        out_shape=[jax.ShapeDtypeStruct((N_DEV * s.shape[0], s.shape[1]), s.dtype) for s in shards],
        in_specs=[spec] * n, out_specs=[spec] * n,
        scratch_shapes=[pltpu.SemaphoreType.DMA((n, 7)), pltpu.SemaphoreType.DMA((n, 7)),
                        pltpu.SemaphoreType.DMA((n,))],
        compiler_params=pltpu.CompilerParams(vmem_limit_bytes=32 * MIB),
    )(*shards)


def _pair_exchange(grads):
    n = len(grads)

    def body(*refs):
        ins, outs = refs[:n], refs[n:2 * n]
        send_sems, recv_sems = refs[2 * n:]
        x, y, c = lax.axis_index("x"), lax.axis_index("y"), lax.axis_index("c")
        copies = [pltpu.make_async_remote_copy(
            src_ref=ins[m].at[:, 1 - c], dst_ref=outs[m],
            send_sem=send_sems.at[m], recv_sem=recv_sems.at[m],
            device_id=(x, y, 1 - c), device_id_type=MESH) for m in range(n)]
        for cp in copies:
            cp.start()
        for cp in copies:
            cp.wait()

    return pl.pallas_call(
        body, name="grad_pair_exchange",
        out_shape=[jax.ShapeDtypeStruct((4,) + g.shape[2:], g.dtype) for g in grads],
        in_specs=[ANY] * n, out_specs=[ANY] * n,
        scratch_shapes=[pltpu.SemaphoreType.DMA((n,)), pltpu.SemaphoreType.DMA((n,))],
    )(*grads)


def _chip_exchange(pair_sums):
    n = len(pair_sums)

    def body(*refs):
        ins, outs = refs[:n], refs[n:2 * n]
        send_sems, recv_sems = refs[2 * n:]
        x, y, c = lax.axis_index("x"), lax.axis_index("y"), lax.axis_index("c")
        chips = [(1 - x, y), (x, 1 - y), (1 - x, 1 - y)]

        def copy(m, j, q_block, to):
            return pltpu.make_async_remote_copy(
                src_ref=ins[m].at[q_block], dst_ref=outs[m].at[j],
                send_sem=send_sems.at[m, j], recv_sem=recv_sems.at[m, j],
                device_id=to, device_id_type=MESH)

        sends = [copy(m, j, 2 * px + py, (px, py, c))
                 for m in range(n) for j, (px, py) in enumerate(chips)]
        for cp in sends:
            cp.start()
        for cp in sends:
            cp.wait_recv()
        for cp in sends:
            cp.wait_send()

    return pl.pallas_call(
        body, name="grad_chip_exchange",
        out_shape=[jax.ShapeDtypeStruct((3,) + p.shape[1:], p.dtype) for p in pair_sums],
        in_specs=[ANY] * n, out_specs=[ANY] * n,
        scratch_shapes=[pltpu.SemaphoreType.DMA((n, 3)), pltpu.SemaphoreType.DMA((n, 3))],
    )(*pair_sums)


N_SLABS = ATTN_W // HEAD_PAIR


def _slab_scratch(tm):
    return pltpu.VMEM((N_SLABS, tm, HEAD_PAIR), F32)


def _rows_by_residue(dst_ref, slab_ref, r, tr, dtype):
    for rho in range(r):
        for s in range(N_SLABS):
            dst_ref[rho, :, s * HEAD_PAIR:(s + 1) * HEAD_PAIR] = (
                slab_ref[s, pl.ds(rho, tr, stride=r), :].astype(dtype))


def _rows_by_token(slab_ref, src_ref, r, tr):
    for rho in range(r):
        for s in range(N_SLABS):
            slab_ref[s, pl.ds(rho, tr, stride=r), :] = src_ref[rho, :, s * HEAD_PAIR:(s + 1) * HEAD_PAIR]


def _rope_tables(pos_col, inv_freq_row, tm):
    T = pos_col.shape[0]

    def body(pos_ref, invf_ref, cos_ref, s1_ref, s2_ref):
        ang = pos_ref[...].astype(F32) * invf_ref[...]
        lane = lax.broadcasted_iota(jnp.int32, (1, HEAD_PAIR), 1) % 64
        cs, sn = jnp.cos(ang), jnp.sin(ang)
        cos_ref[...] = jnp.where(lane < 2 * ROPE_HALF, cs, 1.0)
        s1_ref[...] = jnp.where(lane < ROPE_HALF, -sn, 0.0)
        s2_ref[...] = jnp.where((lane >= ROPE_HALF) & (lane < 2 * ROPE_HALF), sn, 0.0)

    tab = jax.ShapeDtypeStruct((T, HEAD_PAIR), F32)
    row = pl.BlockSpec((tm, HEAD_PAIR), lambda i: (i, 0))
    return pl.pallas_call(
        body, name="rope_tables", grid=(T // tm,), out_shape=[tab] * 3,
        in_specs=[pl.BlockSpec((tm, 1), lambda i: (i, 0)), pl.BlockSpec((1, HEAD_PAIR), lambda i: (0, 0))],
        out_specs=[row] * 3, compiler_params=_params(("parallel",), 16),
    )(pos_col, inv_freq_row)


def _rope(y, cos, s1, s2):
    w = y.shape[1]
    rep = w // HEAD_PAIR
    return (y * jnp.tile(cos, (1, rep)) + pltpu.roll(y, w - ROPE_HALF, 1) * jnp.tile(s1, (1, rep))
            + pltpu.roll(y, ROPE_HALF, 1) * jnp.tile(s2, (1, rep)))


def _rope_transposed(dy, cos, s1, s2):
    w = dy.shape[1]
    rep = w // HEAD_PAIR
    return (dy * jnp.tile(cos, (1, rep)) + pltpu.roll(dy * jnp.tile(s1, (1, rep)), ROPE_HALF, 1)
            + pltpu.roll(dy * jnp.tile(s2, (1, rep)), w - ROPE_HALF, 1))


def _rmsnorm_fwd(x, g, name, tm):
    T = x.shape[0]

    def body(x_ref, g_ref, h_ref):
        xf = x_ref[...]
        r = lax.rsqrt(jnp.mean(xf * xf, axis=-1, keepdims=True) + EPS)
        h_ref[...] = (xf * r * g_ref[...]).astype(BF16)

    return pl.pallas_call(
        body, name=name, grid=(T // tm,), out_shape=jax.ShapeDtypeStruct((T, D_MODEL), BF16),
        in_specs=[pl.BlockSpec((tm, D_MODEL), lambda i: (i, 0)), pl.BlockSpec((1, D_MODEL), lambda i: (0, 0))],
        out_specs=pl.BlockSpec((tm, D_MODEL), lambda i: (i, 0)), compiler_params=_params(("parallel",), 24),
    )(x, g)


def _resident(shape, block=None):
    at = (0,) * len(shape) if block is None else block
    return pl.BlockSpec(shape, lambda *_: at, pipeline_mode=pl.Buffered(1))


def _qkv_proj(h, win_t, tables, tm):
    T = h.shape[0]

    def body(h_ref, w_ref, cos_ref, s1_ref, s2_ref, o0, o1, o2, *slabs):
        hv = h_ref[...]
        cos, s1, s2 = cos_ref[...], s1_ref[...], s2_ref[...]
        for kind in range(3):
            for g, (o_ref, r) in enumerate(zip((o0, o1, o2), DILATIONS)):
                blk = 3 * kind + g
                y = _nt(hv, w_ref[blk * ATTN_W:(blk + 1) * ATTN_W, :])
                if kind < 2:
                    y = _rope(y, cos, s1, s2)
                if kind == 0:
                    y = y * 0.125
                cols = slice(kind * ATTN_W, (kind + 1) * ATTN_W)
                if r == 1:
                    o_ref[0, :, cols] = y.astype(BF16)
                    continue
                slab = slabs[blk % len(slabs)]
                for s in range(N_SLABS):
                    slab[s] = y[:, s * HEAD_PAIR:(s + 1) * HEAD_PAIR]
                for rho in range(r):
                    for s in range(N_SLABS):
                        at = kind * ATTN_W + s * HEAD_PAIR
                        o_ref[rho, :, at:at + HEAD_PAIR] = slab[s, pl.ds(rho, tm // r, stride=r), :].astype(BF16)

    row = pl.BlockSpec((tm, HEAD_PAIR), lambda i: (i, 0))
    return pl.pallas_call(
        body, name="qkv_proj", grid=(T // tm,),
        out_shape=[jax.ShapeDtypeStruct((r, T // r, 3 * ATTN_W), BF16) for r in DILATIONS],
        in_specs=[pl.BlockSpec((tm, D_MODEL), lambda i: (i, 0)), _resident((QKV_COLS, D_MODEL)), row, row, row],
        out_specs=[pl.BlockSpec((r, tm // r, 3 * ATTN_W), lambda i: (0, i, 0)) for r in DILATIONS],
        scratch_shapes=[_slab_scratch(tm)] * 3,
        compiler_params=_params(("parallel",), 48),
    )(h, win_t, *tables)


def _uv_gate_proj(h, win_t, tm):
    T = h.shape[0]
    half = (IN_COLS - QKV_COLS) // 2

    def body(h_ref, wa_ref, wb_ref, o_ref):
        hv = h_ref[...]
        o_ref[:, :half] = _nt(hv, wa_ref[...])
        o_ref[:, half:] = _nt(hv, wb_ref[...])

    blk0 = QKV_COLS // half
    return pl.pallas_call(
        body, name="uv_gate_proj", grid=(T // tm,), out_shape=jax.ShapeDtypeStruct((T, 2 * half), F32),
        in_specs=[pl.BlockSpec((tm, D_MODEL), lambda i: (i, 0)), _resident((half, D_MODEL), (blk0, 0)),
                  _resident((half, D_MODEL), (blk0 + 1, 0))],
        out_specs=pl.BlockSpec((tm, 2 * half), lambda i: (i, 0)), compiler_params=_params(("parallel",), 40),
    )(h, win_t, win_t)


def _band_mask(n):
    row = lax.broadcasted_iota(jnp.int32, (BLK, 2 * BLK), 0)
    col = lax.broadcasted_iota(jnp.int32, (BLK, 2 * BLK), 1)
    has_prev = (jnp.zeros_like(row) + n) > 0
    return ((col < BLK) & (col >= row) & has_prev) | ((col >= BLK) & (col - BLK <= row))


def _head_lanes():
    lane = lax.broadcasted_iota(jnp.int32, (1, HEAD_PAIR), 1)
    return lane < 64, lane >= 64


def _attn_fwd(qkv, group):
    r, L, _ = qkv.shape
    nb = L // BLK

    def body(q_ref, kp_ref, kc_ref, vp_ref, vc_ref, o_ref, lse_ref):
        valid = _band_mask(pl.program_id(1))
        head0, head1 = _head_lanes()
        for p in range(ATTN_W // HEAD_PAIR):
            sl = slice(p * HEAD_PAIR, (p + 1) * HEAD_PAIR)
            q = q_ref[:, sl]
            k2 = jnp.concatenate([kp_ref[:, sl], kc_ref[:, sl]], axis=0)
            v2 = jnp.concatenate([vp_ref[:, sl], vc_ref[:, sl]], axis=0)
            outs, lses = [], []
            for half in (head0, head1):
                s = jnp.where(valid, _nt(jnp.where(half, q, jnp.zeros_like(q)), k2), MASKED)
                m = jnp.max(s, axis=1, keepdims=True)
                e = jnp.exp(s - m)
                den = jnp.sum(e, axis=1, keepdims=True)
                outs.append(_nn(e.astype(BF16), v2) / den)
                lses.append(m + jnp.log(den))
            o_ref[:, sl] = jnp.where(head0, outs[0], outs[1])
            lse_ref[:, sl] = jnp.where(head0, lses[0], lses[1])

    def blk(kind, prev):
        if prev:
            return pl.BlockSpec((None, BLK, ATTN_W), lambda rho, n: (rho, jnp.maximum(n - 1, 0), kind))
        return pl.BlockSpec((None, BLK, ATTN_W), lambda rho, n: (rho, n, kind))

    out = pl.BlockSpec((None, BLK, ATTN_W), lambda rho, n: (rho, n, 0))
    res = jax.ShapeDtypeStruct((r, L, ATTN_W), F32)
    return pl.pallas_call(
        body, name=f"attn_fwd_g{group}", grid=(r, nb), out_shape=[res, res],
        in_specs=[blk(0, False), blk(1, True), blk(1, False), blk(2, True), blk(2, False)],
        out_specs=[out, out], compiler_params=_params(("parallel", "arbitrary"), 24),
    )(qkv, qkv, qkv, qkv, qkv)


def _attn_combine(outs, lses, tm):
    T = outs[0].shape[1]

    def body(o0, l0, o1, l1, o2, l2, attn_ref, lse_ref, so1, sl1, so2, sl2):
        for o_in, l_in, so, sl, r in ((o1, l1, so1, sl1, DILATIONS[1]), (o2, l2, so2, sl2, DILATIONS[2])):
            _rows_by_token(so, o_in, r, tm // r)
            _rows_by_token(sl, l_in, r, tm // r)
        for s in range(N_SLABS):
            lanes = slice(s * HEAD_PAIR, (s + 1) * HEAD_PAIR)
            a0, a1, a2 = l0[0, :, lanes], sl1[s], sl2[s]
            mx = jnp.maximum(jnp.maximum(a0, a1), a2)
            e0, e1, e2 = jnp.exp(a0 - mx), jnp.exp(a1 - mx), jnp.exp(a2 - mx)
            tot = e0 + e1 + e2
            attn_ref[:, lanes] = (e0 * o0[0, :, lanes] + e1 * so1[s] + e2 * so2[s]) / tot
            lse_ref[:, lanes] = mx + jnp.log(tot)

    def grouped(r):
        return pl.BlockSpec((r, tm // r, ATTN_W), lambda i: (0, i, 0))

    tok = pl.BlockSpec((tm, ATTN_W), lambda i: (i, 0))
    res = jax.ShapeDtypeStruct((T, ATTN_W), F32)
    ins, specs = [], []
    for g, r in enumerate(DILATIONS):
        ins += [outs[g], lses[g]]
        specs += [grouped(r), grouped(r)]
    return pl.pallas_call(
        body, name="attn_combine", grid=(T // tm,), out_shape=[res, res], in_specs=specs, out_specs=[tok, tok],
        scratch_shapes=[_slab_scratch(tm)] * 4, compiler_params=_params(("parallel",), 32),
    )(*ins)


def _gelu(x):
    return 0.5 * x * (1.0 + lax.erf(x * (1.0 / math.sqrt(2.0))))


def _gelu_grad(x):
    return 0.5 * (1.0 + lax.erf(x * (1.0 / math.sqrt(2.0)))) + x * jnp.exp(-0.5 * x * x) * (1.0 / math.sqrt(2.0 * math.pi))


def _causal():
    row = lax.broadcasted_iota(jnp.int32, (BLK, BLK), 0)
    col = lax.broadcasted_iota(jnp.int32, (BLK, BLK), 1)
    return col <= row


def _bias_lanes(bt):
    grp = lax.broadcasted_iota(jnp.int32, (1, SGU_W), 1) // 64
    out = jnp.zeros((BLK, SGU_W), F32)
    for g in range(8):
        out = jnp.where(grp == g, bt[:, g:g + 1], out)
    return out


def _sgu_normalise(uv, ln_g, ln_b):
    z = _gelu(uv)
    u, v = z[:, :SGU_W], z[:, SGU_W:]
    mu = jnp.mean(v, axis=-1, keepdims=True)
    xc = v - mu
    rstd = lax.rsqrt(jnp.mean(xc * xc, axis=-1, keepdims=True) + EPS)
    xh = xc * rstd
    return u, xh, rstd, xh * ln_g + ln_b


def _sgu_mix(wc_ref, vb, head0):
    chunks = []
    for ch in range(vb.shape[0] // BLK):
        pairs = []
        for p in range(SGU_W // HEAD_PAIR):
            v_pair = vb[ch * BLK:(ch + 1) * BLK, p * HEAD_PAIR:(p + 1) * HEAD_PAIR]
            pairs.append(jnp.where(head0, _nn(wc_ref[2 * p], v_pair), _nn(wc_ref[2 * p + 1], v_pair)))
        chunks.append(jnp.concatenate(pairs, axis=1))
    return jnp.concatenate(chunks, axis=0)


def _sgu_fwd(uvg, ln_g, ln_b, w_s, b_t, tm):
    T = uvg.shape[0]

    def body(uv_ref, g_ref, b_ref, w_ref, bt_ref, o_ref, wc_ref, bias_ref):
        @pl.when(pl.program_id(0) == 0)
        def _():
            causal = _causal()
            for g in range(8):
                wc_ref[g] = jnp.where(causal, w_ref[g], 0.0).astype(BF16)
            bias_ref[...] = _bias_lanes(bt_ref[...])

        u, _, _, vn = _sgu_normalise(uv_ref[...], g_ref[...], b_ref[...])
        mixed = _sgu_mix(wc_ref, vn.astype(BF16), _head_lanes()[0])
        o_ref[...] = (u * (mixed + jnp.tile(bias_ref[...], (tm // BLK, 1)))).astype(BF16)

    vec = pl.BlockSpec((1, SGU_W), lambda i: (0, 0))
    return pl.pallas_call(
        body, name="sgu_fwd", grid=(T // tm,), out_shape=jax.ShapeDtypeStruct((T, SGU_W), BF16),
        in_specs=[pl.BlockSpec((tm, 2 * SGU_W), lambda i: (i, 0)), vec, vec,
                  pl.BlockSpec((8, BLK, BLK), lambda i: (0, 0, 0)), pl.BlockSpec((BLK, 8), lambda i: (0, 0))],
        out_specs=pl.BlockSpec((tm, SGU_W), lambda i: (i, 0)),
        scratch_shapes=[pltpu.VMEM((8, BLK, BLK), BF16), pltpu.VMEM((BLK, SGU_W), F32)],
        compiler_params=_params(("arbitrary",), 32),
    )(uvg, ln_g, ln_b, w_s, b_t)


def _merge_fwd(attn, sgu, wpa_t, wps_t, uvg, tm):
    T = attn.shape[0]

    def body(attn_ref, sgu_ref, wpa_ref, wps_ref, ga_ref, gb_ref, pa_ref, ps_ref, m_ref):
        pa = _nt(attn_ref[...].astype(BF16), wpa_ref[...])
        ps = _nt(sgu_ref[...], wps_ref[...])
        pa_ref[...] = pa
        ps_ref[...] = ps
        m_ref[...] = (jax.nn.sigmoid(ga_ref[...]) * pa + jax.nn.sigmoid(gb_ref[...]) * ps).astype(BF16)

    half = pl.BlockSpec((tm, ATTN_W), lambda i: (i, 0))
    wide = pl.BlockSpec((tm, D_MODEL), lambda i: (i, 0))
    w = pl.BlockSpec((D_MODEL, ATTN_W), lambda i: (0, 0))
    res = jax.ShapeDtypeStruct((T, D_MODEL), F32)
    return pl.pallas_call(
        body, name="merge_fwd", grid=(T // tm,),
        out_shape=[res, res, jax.ShapeDtypeStruct((T, D_MODEL), BF16)],
        in_specs=[half, half, w, w, pl.BlockSpec((tm, D_MODEL), lambda i: (i, 1)),
                  pl.BlockSpec((tm, D_MODEL), lambda i: (i, 2))],
        out_specs=[wide, wide, wide], compiler_params=_params(("parallel",), 40),
    )(attn, sgu, wpa_t, wps_t, uvg, uvg)


def _out_residual(merged, w_out, x, g2, tm):
    T = x.shape[0]

    def body(m_ref, w_ref, x_ref, g_ref, o_ref, h_ref):
        x1 = x_ref[...] + _nn(m_ref[...], w_ref[...])
        o_ref[...] = x1
        r = lax.rsqrt(jnp.mean(x1 * x1, axis=-1, keepdims=True) + EPS)
        h_ref[...] = (x1 * r * g_ref[...]).astype(BF16)

    wide = pl.BlockSpec((tm, D_MODEL), lambda i: (i, 0))
    return pl.pallas_call(
        body, name="out_residual", grid=(T // tm,),
        out_shape=[jax.ShapeDtypeStruct((T, D_MODEL), F32), jax.ShapeDtypeStruct((T, D_MODEL), BF16)],
        in_specs=[wide, _resident((D_MODEL, D_MODEL)), wide, pl.BlockSpec((1, D_MODEL), lambda i: (0, 0))],
        out_specs=[wide, wide], compiler_params=_params(("parallel",), 32),
    )(merged, w_out, x, g2)


def _ffn_fwd(h2, wg_t, wu_t, tm):
    T = h2.shape[0]

    def body(h_ref, wg_ref, wu_ref, gate_ref, up_ref, ff_ref):
        h = h_ref[...]
        for j in range(D_FF // FF_TILE):
            cols = slice(j * FF_TILE, (j + 1) * FF_TILE)
            gate, up = _nt(h, wg_ref[cols, :]), _nt(h, wu_ref[cols, :])
            gate_ref[:, cols] = gate.astype(BF16)
            up_ref[:, cols] = up.astype(BF16)
            ff_ref[:, cols] = (gate * jax.nn.sigmoid(gate) * up).astype(BF16)

    w = _resident((D_FF, D_MODEL))
    o = pl.BlockSpec((tm, D_FF), lambda i: (i, 0))
    res = jax.ShapeDtypeStruct((T, D_FF), BF16)
    return pl.pallas_call(
        body, name="ffn_fwd", grid=(T // tm,), out_shape=[res, res, res],
        in_specs=[pl.BlockSpec((tm, D_MODEL), lambda i: (i, 0)), w, w], out_specs=[o, o, o],
        compiler_params=_params(("parallel",), 52),
    )(h2, wg_t, wu_t)


def _down_loss(ff, w_down, x1, final_g, target, tm):
    T = x1.shape[0]

    def body(ff_ref, w_ref, x1_ref, g_ref, t_ref, dx_ref, dxb_ref, loss_ref, dg_ref):
        @pl.when(pl.program_id(0) == 0)
        def _():
            loss_ref[...] = jnp.zeros_like(loss_ref)
            dg_ref[...] = jnp.zeros_like(dg_ref)

        x2 = x1_ref[...] + _nn(ff_ref[...], w_ref[...])
        g = g_ref[...]
        r = lax.rsqrt(jnp.mean(x2 * x2, axis=-1, keepdims=True) + EPS)
        xh = x2 * r
        err = xh * g - t_ref[...]
        loss_ref[...] += jnp.sum(err * err, axis=0, keepdims=True) * (0.5 / D_MODEL)
        dy = err * (1.0 / D_MODEL)
        dg_ref[...] += jnp.sum(dy * xh, axis=0, keepdims=True)
        dxh = dy * g
        dx = r * (dxh - xh * jnp.mean(dxh * xh, axis=-1, keepdims=True))
        dx_ref[...] = dx
        dxb_ref[...] = dx.astype(BF16)

    wide = pl.BlockSpec((tm, D_MODEL), lambda i: (i, 0))
    vec = pl.BlockSpec((1, D_MODEL), lambda i: (0, 0))
    vec_shape = jax.ShapeDtypeStruct((1, D_MODEL), F32)
    return pl.pallas_call(
        body, name="down_loss", grid=(T // tm,),
        out_shape=[jax.ShapeDtypeStruct((T, D_MODEL), F32), jax.ShapeDtypeStruct((T, D_MODEL), BF16),
                   vec_shape, vec_shape],
        in_specs=[pl.BlockSpec((tm, D_FF), lambda i: (i, 0)), _resident((D_FF, D_MODEL)), wide, vec, wide],
        out_specs=[wide, wide, vec, vec],
        compiler_params=_params(("arbitrary",), 40),
    )(ff, w_down, x1, final_g, target)


def _rmsnorm_bwd(dh, xin, g, d_res, dg_ref):
    r = lax.rsqrt(jnp.mean(xin * xin, axis=-1, keepdims=True) + EPS)
    xh = xin * r
    dg_ref[...] += jnp.sum(dh * xh, axis=0, keepdims=True)
    dxh = dh * g
    return d_res + r * (dxh - xh * jnp.mean(dxh * xh, axis=-1, keepdims=True))


def _ffn_bwd(dx2, dx2b, w_down, wg_t, wu_t, gate, up, x1, g2, tm):
    T = dx2.shape[0]

    def body(dx_ref, dxb_ref, wd_ref, wg_ref, wu_ref, gate_ref, up_ref, x1_ref, g_ref,
             dgate_ref, dup_ref, dx1_ref, dx1b_ref, dg_ref):
        @pl.when(pl.program_id(0) == 0)
        def _():
            dg_ref[...] = jnp.zeros_like(dg_ref)

        dxb = dxb_ref[...]
        dh = jnp.zeros((tm, D_MODEL), F32)
        for j in range(D_FF // FF_TILE):
            cols = slice(j * FF_TILE, (j + 1) * FF_TILE)
            dff = _nt(dxb, wd_ref[cols, :])
            gate, up = gate_ref[:, cols].astype(F32), up_ref[:, cols].astype(F32)
            sg = jax.nn.sigmoid(gate)
            dgate = (dff * up * sg * (1.0 + gate * (1.0 - sg))).astype(BF16)
            dup = (dff * gate * sg).astype(BF16)
            dgate_ref[:, cols] = dgate
            dup_ref[:, cols] = dup
            dh += _nn(dgate, wg_ref[cols, :]) + _nn(dup, wu_ref[cols, :])
        dx1 = _rmsnorm_bwd(dh, x1_ref[...], g_ref[...], dx_ref[...], dg_ref)
        dx1_ref[...] = dx1
        dx1b_ref[...] = dx1.astype(BF16)

    wide = pl.BlockSpec((tm, D_MODEL), lambda i: (i, 0))
    ffw = pl.BlockSpec((tm, D_FF), lambda i: (i, 0))
    vec = pl.BlockSpec((1, D_MODEL), lambda i: (0, 0))
    w = _resident((D_FF, D_MODEL))
    ff_shape = jax.ShapeDtypeStruct((T, D_FF), BF16)
    return pl.pallas_call(
        body, name="ffn_bwd", grid=(T // tm,),
        out_shape=[ff_shape, ff_shape, jax.ShapeDtypeStruct((T, D_MODEL), F32),
                   jax.ShapeDtypeStruct((T, D_MODEL), BF16), jax.ShapeDtypeStruct((1, D_MODEL), F32)],
        in_specs=[wide, wide, w, w, w, ffw, ffw, wide, vec],
        out_specs=[ffw, ffw, wide, wide, vec], compiler_params=_params(("arbitrary",), 56),
    )(dx2, dx2b, w_down, wg_t, wu_t, gate, up, x1, g2)


def _mm_tn(a, b, *, tmm, tk, name, into=None, row_block0=0, rows_total=None):
    T, M = a.shape
    N = b.shape[1]
    rows_total = M if rows_total is None else rows_total

    def body(*refs):
        a_ref, b_ref, o_ref = refs[0], refs[1], refs[-1]

        @pl.when(pl.program_id(1) == 0)
        def _():
            o_ref[...] = jnp.zeros_like(o_ref)

        o_ref[...] += _tn(a_ref[...], b_ref[...])

    ins = [a, b] + ([] if into is None else [into])
    specs = [pl.BlockSpec((tk, tmm), lambda i, k: (k, i)), pl.BlockSpec((tk, N), lambda i, k: (k, 0))]
    return pl.pallas_call(
        body, name=name, grid=(M // tmm, T // tk),
        out_shape=jax.ShapeDtypeStruct((rows_total, N), F32),
        in_specs=specs + ([] if into is None else [ANY]),
        out_specs=pl.BlockSpec((tmm, N), lambda i, k: (row_block0 + i, 0)),
        input_output_aliases={} if into is None else {2: 0},
        compiler_params=_params(("parallel", "arbitrary"), 48),
    )(*ins)


def _in_bwd(dqkv, duv, dgates, win_t, x, g1, dx1, tm):
    T = x.shape[0]
    uv0, gates0 = QKV_COLS, QKV_COLS + 2 * SGU_W

    def body(dq_ref, du_ref, dgt_ref, w_ref, x_ref, g_ref, d_ref, dx_ref, dg_ref):
        @pl.when(pl.program_id(0) == 0)
        def _():
            dg_ref[...] = jnp.zeros_like(dg_ref)

        dh = (_nn(dq_ref[...], w_ref[:uv0, :]) + _nn(du_ref[...], w_ref[uv0:gates0, :])
              + _nn(dgt_ref[...], w_ref[gates0:, :]))
        dx_ref[...] = _rmsnorm_bwd(dh, x_ref[...], g_ref[...], d_ref[...], dg_ref)

    def cols(n):
        return pl.BlockSpec((tm, n), lambda i: (i, 0))

    wide = cols(D_MODEL)
    vec = pl.BlockSpec((1, D_MODEL), lambda i: (0, 0))
    return pl.pallas_call(
        body, name="in_bwd", grid=(T // tm,),
        out_shape=[jax.ShapeDtypeStruct((T, D_MODEL), F32), jax.ShapeDtypeStruct((1, D_MODEL), F32)],
        in_specs=[cols(QKV_COLS), cols(2 * SGU_W), cols(2 * D_MODEL), _resident((IN_COLS, D_MODEL)), wide, vec, wide],
        out_specs=[wide, vec], compiler_params=_params(("arbitrary",), 56),
    )(dqkv, duv, dgates, win_t, x, g1, dx1)


def _merge_bwd(dx1b, w_out, pa, ps, uvg, tm):
    T = dx1b.shape[0]

    def body(dx_ref, w_ref, pa_ref, ps_ref, ga_ref, gb_ref, dpa_ref, dps_ref, dg_ref):
        dm = _nt(dx_ref[...], w_ref[...])
        ga, gb = jax.nn.sigmoid(ga_ref[...]), jax.nn.sigmoid(gb_ref[...])
        dpa_ref[...] = (dm * ga).astype(BF16)
        dps_ref[...] = (dm * gb).astype(BF16)
        dg_ref[:, :D_MODEL] = (dm * pa_ref[...] * ga * (1.0 - ga)).astype(BF16)
        dg_ref[:, D_MODEL:] = (dm * ps_ref[...] * gb * (1.0 - gb)).astype(BF16)

    wide = pl.BlockSpec((tm, D_MODEL), lambda i: (i, 0))
    res = jax.ShapeDtypeStruct((T, D_MODEL), BF16)
    return pl.pallas_call(
        body, name="merge_bwd", grid=(T // tm,),
        out_shape=[res, res, jax.ShapeDtypeStruct((T, 2 * D_MODEL), BF16)],
        in_specs=[wide, pl.BlockSpec((D_MODEL, D_MODEL), lambda i: (0, 0)), wide, wide,
                  pl.BlockSpec((tm, D_MODEL), lambda i: (i, 1)), pl.BlockSpec((tm, D_MODEL), lambda i: (i, 2))],
        out_specs=[wide, wide, pl.BlockSpec((tm, 2 * D_MODEL), lambda i: (i, 0))],
        compiler_params=_params(("parallel",), 48),
    )(dx1b, w_out, pa, ps, uvg, uvg)


def _proj_bwd(dpa, dps, wpa_t, wps_t, tm):
    T = dpa.shape[0]

    def body(dpa_ref, dps_ref, wpa_ref, wps_ref, dattn_ref, dsgu_ref):
        dattn_ref[...] = _nn(dpa_ref[...], wpa_ref[...])
        dsgu_ref[...] = _nn(dps_ref[...], wps_ref[...])

    wide = pl.BlockSpec((tm, D_MODEL), lambda i: (i, 0))
    half = pl.BlockSpec((tm, ATTN_W), lambda i: (i, 0))
    w = pl.BlockSpec((D_MODEL, ATTN_W), lambda i: (0, 0))
    res = jax.ShapeDtypeStruct((T, ATTN_W), F32)
    return pl.pallas_call(
        body, name="proj_bwd", grid=(T // tm,), out_shape=[res, res], in_specs=[wide, wide, w, w],
        out_specs=[half, half], compiler_params=_params(("parallel",), 32),
    )(dpa, dps, wpa_t, wps_t)


def _sgu_bwd(uvg, dsgu, ln_g, ln_b, w_s, b_t, tm):
    T = uvg.shape[0]
    nsteps = T // tm

    def body(uv_ref, ds_ref, g_ref, b_ref, w_ref, bt_ref, duv_ref, dw_ref, dbt_ref, dg_ref, db_ref,
             wc_ref, wct_ref, bias_ref, dbias_ref):
        step = pl.program_id(0)
        head0, head1 = _head_lanes()

        @pl.when(step == 0)
        def _():
            causal = _causal()
            for g in range(8):
                wc = jnp.where(causal, w_ref[g], 0.0)
                wc_ref[g] = wc.astype(BF16)
                wct_ref[g] = wc.T.astype(BF16)
            bias_ref[...] = _bias_lanes(bt_ref[...])
            dbias_ref[...] = jnp.zeros_like(dbias_ref)
            dw_ref[...] = jnp.zeros_like(dw_ref)
            dg_ref[...] = jnp.zeros_like(dg_ref)
            db_ref[...] = jnp.zeros_like(db_ref)

        uv = uv_ref[...]
        ln_gain = g_ref[...]
        u, xh, rstd, vn = _sgu_normalise(uv, ln_gain, b_ref[...])
        vb = vn.astype(BF16)
        mixed = _sgu_mix(wc_ref, vb, head0) + jnp.tile(bias_ref[...], (tm // BLK, 1))
        dout = ds_ref[...]
        du = dout * mixed
        dmixed = dout * u
        dmb = dmixed.astype(BF16)
        dvn_chunks = []
        for ch in range(tm // BLK):
            rows = slice(ch * BLK, (ch + 1) * BLK)
            dbias_ref[...] += dmixed[rows]
            pairs = []
            for p in range(SGU_W // HEAD_PAIR):
                lanes = slice(p * HEAD_PAIR, (p + 1) * HEAD_PAIR)
                dm_pair, v_pair = dmb[rows, lanes], vb[rows, lanes]
                acc = jnp.zeros((BLK, HEAD_PAIR), F32)
                for hh, half in enumerate((head0, head1)):
                    dm_h = jnp.where(half, dm_pair, jnp.zeros_like(dm_pair))
                    dw_ref[2 * p + hh] += _nt(dm_h, v_pair)
                    acc += _nn(wct_ref[2 * p + hh], dm_h)
                pairs.append(acc)
            dvn_chunks.append(jnp.concatenate(pairs, axis=1))
        dvn = jnp.concatenate(dvn_chunks, axis=0)
        dg_ref[...] += jnp.sum(dvn * xh, axis=0, keepdims=True)
        db_ref[...] += jnp.sum(dvn, axis=0, keepdims=True)
        dxh = dvn * ln_gain
        dv = rstd * (dxh - jnp.mean(dxh, axis=-1, keepdims=True) - xh * jnp.mean(dxh * xh, axis=-1, keepdims=True))
        dgelu = _gelu_grad(uv)
        duv_ref[:, :SGU_W] = (du * dgelu[:, :SGU_W]).astype(BF16)
        duv_ref[:, SGU_W:] = (dv * dgelu[:, SGU_W:]).astype(BF16)

        @pl.when(step == nsteps - 1)
        def _():
            causal = _causal()
            for g in range(8):
                dw_ref[g] = jnp.where(causal, dw_ref[g], 0.0)
            grp = lax.broadcasted_iota(jnp.int32, (1, SGU_W), 1) // 64
            col = lax.broadcasted_iota(jnp.int32, (1, 8), 1)
            dbias = dbias_ref[...]
            out = jnp.zeros((BLK, 8), F32)
            for g in range(8):
                s = jnp.sum(jnp.where(grp == g, dbias, 0.0), axis=1, keepdims=True)
                out = jnp.where(col == g, s, out)
            dbt_ref[...] = out

    vec = pl.BlockSpec((1, SGU_W), lambda i: (0, 0))
    w3 = pl.BlockSpec((8, BLK, BLK), lambda i: (0, 0, 0))
    bt = pl.BlockSpec((BLK, 8), lambda i: (0, 0))
    return pl.pallas_call(
        body, name="sgu_bwd", grid=(nsteps,),
        out_shape=[jax.ShapeDtypeStruct((T, 2 * SGU_W), BF16), jax.ShapeDtypeStruct((8, BLK, BLK), F32),
                   jax.ShapeDtypeStruct((BLK, 8), F32), jax.ShapeDtypeStruct((1, SGU_W), F32),
                   jax.ShapeDtypeStruct((1, SGU_W), F32)],
        in_specs=[pl.BlockSpec((tm, 2 * SGU_W), lambda i: (i, 0)), pl.BlockSpec((tm, SGU_W), lambda i: (i, 0)),
                  vec, vec, w3, bt],
        out_specs=[pl.BlockSpec((tm, 2 * SGU_W), lambda i: (i, 0)), w3, bt, vec, vec],
        scratch_shapes=[pltpu.VMEM((8, BLK, BLK), BF16), pltpu.VMEM((8, BLK, BLK), BF16),
                        pltpu.VMEM((BLK, SGU_W), F32), pltpu.VMEM((BLK, SGU_W), F32)],
        compiler_params=_params(("arbitrary",), 40),
    )(uvg, dsgu, ln_g, ln_b, w_s, b_t)


def _attn_bwd_prepare(dattn, attn, lse, tm):
    T = dattn.shape[0]

    def body(da_ref, at_ref, lse_ref, *outs):
        scr_da, scr_lse, scr_d = outs[-3:]
        outs = outs[:-3]
        head0, _ = _head_lanes()
        for s in range(N_SLABS):
            lanes = slice(s * HEAD_PAIR, (s + 1) * HEAD_PAIR)
            da = da_ref[:, lanes]
            pp = da * at_ref[:, lanes]
            d0 = jnp.sum(jnp.where(head0, pp, 0.0), axis=1, keepdims=True)
            d1 = jnp.sum(jnp.where(head0, 0.0, pp), axis=1, keepdims=True)
            d_pair = jnp.where(head0, d0, d1)
            scr_da[s] = da
            scr_lse[s] = lse_ref[:, lanes]
            scr_d[s] = d_pair
            outs[0][0, :, lanes] = da.astype(BF16)
            outs[1][0, :, lanes] = lse_ref[:, lanes]
            outs[2][0, :, lanes] = d_pair
        for g, r in enumerate(DILATIONS):
            if r > 1:
                for dst, scr, dt in zip(outs[3 * g:3 * g + 3], (scr_da, scr_lse, scr_d), (BF16, F32, F32)):
                    _rows_by_residue(dst, scr, r, tm // r, dt)

    tok = pl.BlockSpec((tm, ATTN_W), lambda i: (i, 0))
    shapes, specs = [], []
    for r in DILATIONS:
        spec = pl.BlockSpec((r, tm // r, ATTN_W), lambda i: (0, i, 0))
        shapes += [jax.ShapeDtypeStruct((r, T // r, ATTN_W), dt) for dt in (BF16, F32, F32)]
        specs += [spec] * 3
    return pl.pallas_call(
        body, name="attn_bwd_prepare", grid=(T // tm,), out_shape=shapes, in_specs=[tok, tok, tok],
        out_specs=specs, scratch_shapes=[_slab_scratch(tm)] * 3,
        compiler_params=_params(("parallel",), 40),
    )(dattn, attn, lse)


def _attn_bwd(qkv, dattn, lse, dsum, group):
    r, L, _ = qkv.shape
    nb = L // BLK

    def body(q_ref, kp_ref, kc_ref, vp_ref, vc_ref, da_ref, lse_ref, d_ref, dq_ref, dk_ref, dv_ref,
             carry_k, carry_v):
        n = pl.program_id(1)

        @pl.when(n == 0)
        def _():
            carry_k[...] = jnp.zeros_like(carry_k)
            carry_v[...] = jnp.zeros_like(carry_v)

        @pl.when(n < nb)
        def _():
            valid = _band_mask(n)
            head0, head1 = _head_lanes()
            for p in range(ATTN_W // HEAD_PAIR):
                sl = slice(p * HEAD_PAIR, (p + 1) * HEAD_PAIR)
                q, da = q_ref[:, sl], da_ref[:, sl]
                k2 = jnp.concatenate([kp_ref[:, sl], kc_ref[:, sl]], axis=0)
                v2 = jnp.concatenate([vp_ref[:, sl], vc_ref[:, sl]], axis=0)
                lse_pair, d_pair = lse_ref[:, sl], d_ref[:, sl]
                dq = jnp.zeros((BLK, HEAD_PAIR), F32)
                dk2 = jnp.zeros((2 * BLK, HEAD_PAIR), F32)
                dv2 = jnp.zeros((2 * BLK, HEAD_PAIR), F32)
                for hh, half in enumerate((head0, head1)):
                    col = 64 * hh
                    q_h = jnp.where(half, q, jnp.zeros_like(q))
                    da_h = jnp.where(half, da, jnp.zeros_like(da))
                    s = _nt(q_h, k2)
                    prob = jnp.where(valid, jnp.exp(s - lse_pair[:, col:col + 1]), 0.0)
                    dprob = _nt(da_h, v2)
                    ds = (prob * (dprob - d_pair[:, col:col + 1])).astype(BF16)
                    dq += jnp.where(half, _nn(ds, k2), 0.0)
                    dk2 += _tn(ds, q_h)
                    dv2 += _tn(prob.astype(BF16), da_h)
                dq_ref[:, sl] = dq
                dk_ref[:, sl] = carry_k[:, sl] + dk2[:BLK]
                dv_ref[:, sl] = carry_v[:, sl] + dv2[:BLK]
                carry_k[:, sl] = dk2[BLK:]
                carry_v[:, sl] = dv2[BLK:]

        @pl.when(n == nb)
        def _():
            dk_ref[...] = carry_k[...]
            dv_ref[...] = carry_v[...]

    def cur(kind):
        return pl.BlockSpec((None, BLK, ATTN_W), lambda rho, n: (rho, jnp.minimum(n, nb - 1), kind))

    def prev(kind):
        return pl.BlockSpec((None, BLK, ATTN_W), lambda rho, n: (rho, jnp.clip(n - 1, 0, nb - 1), kind))

    res = jax.ShapeDtypeStruct((r, L, ATTN_W), F32)
    return pl.pallas_call(
        body, name=f"attn_bwd_g{group}", grid=(r, nb + 1), out_shape=[res, res, res],
        in_specs=[cur(0), prev(1), cur(1), prev(2), cur(2), cur(0), cur(0), cur(0)],
        out_specs=[cur(0), prev(0), prev(0)],
        scratch_shapes=[pltpu.VMEM((BLK, ATTN_W), F32), pltpu.VMEM((BLK, ATTN_W), F32)],
        compiler_params=_params(("parallel", "arbitrary"), 32),
    )(qkv, qkv, qkv, qkv, qkv, dattn, lse, dsum)


def _dqkv_token_order(dqkv_groups, tables, tm):
    T = tables[0].shape[0]

    def body(*refs):
        ins = refs[:9]
        cos_ref, s1_ref, s2_ref, o_ref, scr = refs[9:]
        cos, s1, s2 = cos_ref[...], s1_ref[...], s2_ref[...]
        for g, r in enumerate(DILATIONS):
            for kind in range(3):
                src = ins[3 * g + kind]
                if r > 1:
                    _rows_by_token(scr, src, r, tm // r)
                for s in range(N_SLABS):
                    val = scr[s] if r > 1 else src[0, :, s * HEAD_PAIR:(s + 1) * HEAD_PAIR]
                    if kind < 2:
                        val = _rope_transposed(val, cos, s1, s2)
                    if kind == 0:
                        val = val * 0.125
                    at = (3 * kind + g) * ATTN_W + s * HEAD_PAIR
                    o_ref[:, at:at + HEAD_PAIR] = val.astype(BF16)

    specs = []
    for r in DILATIONS:
        specs += [pl.BlockSpec((r, tm // r, ATTN_W), lambda i: (0, i, 0))] * 3
    row = pl.BlockSpec((tm, HEAD_PAIR), lambda i: (i, 0))
    flat = [a for grp in dqkv_groups for a in grp]
    return pl.pallas_call(
        body, name="dqkv_token_order", grid=(T // tm,), out_shape=jax.ShapeDtypeStruct((T, QKV_COLS), BF16),
        in_specs=specs + [row] * 3, out_specs=pl.BlockSpec((tm, QKV_COLS), lambda i: (i, 0)),
        scratch_shapes=[_slab_scratch(tm)], compiler_params=_params(("parallel",), 48),
    )(*flat, *tables)


def _row_tile(rows):
    for cand in (320, 256, 176, 128):
        if rows % cand == 0:
            return cand
    return rows


def _pair_sum(grad4, recv, chip):
    _, _, rows, cols = grad4.shape
    tr = _row_tile(rows)

    def body(ids_ref, g_ref, r_ref, gown_ref, rown_ref, sum_ref, own_ref):
        sum_ref[...] = (g_ref[...] + r_ref[...]).astype(BF16)

        @pl.when(pl.program_id(1) == 0)
        def _():
            own_ref[...] = gown_ref[...] + rown_ref[...]

    grid_spec = pltpu.PrefetchScalarGridSpec(
        num_scalar_prefetch=1, grid=(rows // tr, 4),
        in_specs=[pl.BlockSpec((None, None, tr, cols), lambda i, q, ids: (q, ids[1], i, 0)),
                  pl.BlockSpec((None, tr, cols), lambda i, q, ids: (q, i, 0)),
                  pl.BlockSpec((None, None, tr, cols), lambda i, q, ids: (ids[0], ids[1], i, 0)),
                  pl.BlockSpec((None, tr, cols), lambda i, q, ids: (ids[0], i, 0))],
        out_specs=[pl.BlockSpec((None, tr, cols), lambda i, q, ids: (q, i, 0)),
                   pl.BlockSpec((tr, cols), lambda i, q, ids: (i, 0))])
    return pl.pallas_call(
        body, name=f"grad_pair_sum_{rows}x{cols}", grid_spec=grid_spec,
        out_shape=[jax.ShapeDtypeStruct((4, rows, cols), BF16), jax.ShapeDtypeStruct((rows, cols), F32)],
        compiler_params=_params(("arbitrary", "arbitrary"), 32),
    )(chip, grad4, recv, grad4, recv)


def _chip_sum(own, others, name):
    rows, cols = own.shape
    tr = _row_tile(rows)

    def body(own_ref, oth_ref, o_ref):
        total = own_ref[...]
        for j in range(3):
            total = total + oth_ref[j].astype(F32)
        o_ref[...] = total

    blk = pl.BlockSpec((tr, cols), lambda i: (i, 0))
    return pl.pallas_call(
        body, name=name, grid=(rows // tr,), out_shape=jax.ShapeDtypeStruct((rows, cols), F32),
        in_specs=[blk, pl.BlockSpec((3, tr, cols), lambda i: (0, i, 0))], out_specs=blk,
        compiler_params=_params(("parallel",), 32),
    )(own, others)


def _adam_math(w, g, m, v):
    m = ADAM_B1 * m + (1.0 - ADAM_B1) * g
    v = ADAM_B2 * v + (1.0 - ADAM_B2) * (g * g)
    m_hat = m / (1.0 - ADAM_B1 ** ADAM_STEP)
    v_hat = v / (1.0 - ADAM_B2 ** ADAM_STEP)
    delta = -ADAM_LR * (m_hat / (jnp.sqrt(v_hat) + ADAM_EPS) + ADAM_WD * w)
    return delta, m, v


def _adamw(w, g, m, v, name):
    rows, cols = w.shape
    tr = _row_tile(rows)

    def body(w_ref, g_ref, m_ref, v_ref, d_ref, nm_ref, nv_ref):
        d_ref[...], nm_ref[...], nv_ref[...] = _adam_math(w_ref[...], g_ref[...], m_ref[...], v_ref[...])

    blk = pl.BlockSpec((tr, cols), lambda i: (i, 0))
    res = jax.ShapeDtypeStruct((rows, cols), F32)
    return pl.pallas_call(
        body, name=name, grid=(rows // tr,), out_shape=[res, res, res], in_specs=[blk] * 4, out_specs=[blk] * 3,
        compiler_params=_params(("parallel",), 32),
    )(w, g, m, v)


def _small_update(parts, w, m, v, loss_rows):
    rows = w.shape[0]

    def body(p_ref, w_ref, m_ref, v_ref, g_ref, d_ref, nm_ref, nv_ref, loss_ref):
        g = p_ref[0:rows, :]
        for dev in range(1, N_DEV):
            g = g + p_ref[dev * rows:(dev + 1) * rows, :]
        g_ref[...] = g
        d_ref[...], nm_ref[...], nv_ref[...] = _adam_math(w_ref[...], g, m_ref[...], v_ref[...])
        loss_ref[...] = jnp.sum(jnp.sum(g[rows - loss_rows:, :], axis=1, keepdims=True), axis=0, keepdims=True)

    res = jax.ShapeDtypeStruct((rows, HEAD_PAIR), F32)
    return pl.pallas_call(
        body, name="small_update", out_shape=[res, res, res, res, jax.ShapeDtypeStruct((1, 1), F32)],
        compiler_params=pltpu.CompilerParams(vmem_limit_bytes=32 * MIB),
    )(parts, w, m, v)


def kernel(x, positions, norm1_g, w_in, sgu_ln_g, sgu_ln_b, w_spatial, b_spatial, w_proj_attn, w_proj_sgu, w_out, norm2_g, w_ffn_gate, w_ffn_up, w_ffn_down, final_g, loss_target, m_norm1_g, m_w_in, m_sgu_ln_g, m_sgu_ln_b, m_w_spatial, m_b_spatial, m_w_proj_attn, m_w_proj_sgu, m_w_out, m_norm2_g, m_w_ffn_gate, m_w_ffn_up, m_w_ffn_down, m_final_g, v_norm1_g, v_w_in, v_sgu_ln_g, v_sgu_ln_b, v_w_spatial, v_b_spatial, v_w_proj_attn, v_w_proj_sgu, v_w_out, v_norm2_g, v_w_ffn_gate, v_w_ffn_up, v_w_ffn_down, v_final_g):
    T = x.shape[1]
    tm = 512
    xt = x[0]
    target = loss_target[0]
    chip = jnp.stack([2 * lax.axis_index("x") + lax.axis_index("y"), lax.axis_index("c")]).astype(jnp.int32)

    shards = [jnp.transpose(w_in[0]).astype(BF16), jnp.transpose(w_proj_attn[0]).astype(BF16),
              jnp.transpose(w_proj_sgu[0]).astype(BF16), w_out[0].astype(BF16),
              jnp.transpose(w_ffn_gate[0]).astype(BF16), jnp.transpose(w_ffn_up[0]).astype(BF16),
              w_ffn_down[0].astype(BF16)]
    win_t, wpa_t, wps_t, wout, wg_t, wu_t, wd = _all_gather_rows(shards, "weights_all_gather")

    inv_freq = ROPE_THETA ** (-jnp.arange(0, 2 * ROPE_HALF, 2, dtype=F32) / (2 * ROPE_HALF))
    inv_freq_row = jnp.tile(jnp.concatenate([inv_freq, inv_freq, jnp.zeros((48,), F32)]), 2).reshape(1, HEAD_PAIR)
    tables = _rope_tables(positions.reshape(T, 1), inv_freq_row, tm)
    b_t = jnp.transpose(b_spatial[0])

    h = _rmsnorm_fwd(xt, norm1_g, "norm1_fwd", tm)
    qkv = _qkv_proj(h, win_t, tables, tm)
    uvg = _uv_gate_proj(h, win_t, tm)
    fwd = [_attn_fwd(qkv[g], g) for g in range(3)]
    attn, lse = _attn_combine([f[0] for f in fwd], [f[1] for f in fwd], tm)
    sgu = _sgu_fwd(uvg, sgu_ln_g, sgu_ln_b, w_spatial[0], b_t, tm)
    pa, ps, merged = _merge_fwd(attn, sgu, wpa_t, wps_t, uvg, tm)
    x1, h2 = _out_residual(merged, wout, xt, norm2_g, tm)
    gate, up, ff = _ffn_fwd(h2, wg_t, wu_t, tm)
    dx2, dx2b, loss_cols, d_final_g = _down_loss(ff, wd, x1, final_g.reshape(1, D_MODEL), target, tm)

    dgate, dup, dx1, dx1b, d_norm2 = _ffn_bwd(dx2, dx2b, wd, wg_t, wu_t, gate, up, x1, norm2_g, tm // 2)
    tk = min(1024, T)
    d_wd = _mm_tn(ff, dx2b, tmm=FF_TILE, tk=tk, name="grad_w_ffn_down")
    d_wg_t = _mm_tn(dgate, h2, tmm=FF_TILE, tk=tk, name="grad_w_ffn_gate")
    d_wu_t = _mm_tn(dup, h2, tmm=FF_TILE, tk=tk, name="grad_w_ffn_up")
    dpa, dps, dgates = _merge_bwd(dx1b, wout, pa, ps, uvg, tm)
    d_wout = _mm_tn(merged, dx1b, tmm=D_MODEL, tk=tk, name="grad_w_out")
    dattn, dsgu = _proj_bwd(dpa, dps, wpa_t, wps_t, tm)
    d_wpa_t = _mm_tn(dpa, attn.astype(BF16), tmm=D_MODEL, tk=tk, name="grad_w_proj_attn")
    d_wps_t = _mm_tn(dps, sgu, tmm=D_MODEL, tk=tk, name="grad_w_proj_sgu")
    duv, d_ws, d_bs_t, d_ln_g, d_ln_b = _sgu_bwd(uvg, dsgu, sgu_ln_g, sgu_ln_b, w_spatial[0], b_t, tm)
    prep = _attn_bwd_prepare(dattn, attn, lse, tm)
    dqkv_groups = [_attn_bwd(qkv[g], prep[3 * g], prep[3 * g + 1], prep[3 * g + 2], g) for g in range(3)]
    dqkv = _dqkv_token_order(dqkv_groups, tables, tm)
    dx, d_norm1 = _in_bwd(dqkv, duv, dgates, win_t, xt, norm1_g, dx1, tm)
    in_cols = win_t.shape[0]
    d_win_t = _mm_tn(dqkv, h, tmm=1536, tk=tk, name="grad_w_in_qkv", rows_total=in_cols)
    d_win_t = _mm_tn(duv, h, tmm=512, tk=tk, name="grad_w_in_uv", into=d_win_t, row_block0=9, rows_total=in_cols)
    d_win_t = _mm_tn(dgates, h, tmm=512, tk=tk, name="grad_w_in_gates", into=d_win_t, row_block0=11, rows_total=in_cols)

    big = [d_win_t, d_wpa_t, d_wps_t, d_wout, d_wg_t, d_wu_t, d_wd]
    big4 = [g.reshape(4, 2, g.shape[0] // N_DEV, g.shape[1]) for g in big]
    from_sibling = _pair_exchange(big4)
    sums, owns = [], []
    for g4, rv in zip(big4, from_sibling):
        s, o = _pair_sum(g4, rv, chip)
        sums.append(s)
        owns.append(o)
    from_chips = _chip_exchange(sums)
    names = ["w_in", "w_proj_attn", "w_proj_sgu", "w_out", "w_ffn_gate", "w_ffn_up", "w_ffn_down"]
    reduced = [_chip_sum(o, f, "grad_total_" + nm) for o, f, nm in zip(owns, from_chips, names)]
    transposed = (True, True, True, False, True, True, False)
    g_big = [jnp.transpose(r) if t else r for r, t in zip(reduced, transposed)]

    small_w = [norm1_g, sgu_ln_g, sgu_ln_b, w_spatial, b_spatial, norm2_g, final_g]
    small_m = [m_norm1_g, m_sgu_ln_g, m_sgu_ln_b, m_w_spatial, m_b_spatial, m_norm2_g, m_final_g]
    small_v = [v_norm1_g, v_sgu_ln_g, v_sgu_ln_b, v_w_spatial, v_b_spatial, v_norm2_g, v_final_g]
    small_g = [d_norm1, d_ln_g, d_ln_b, d_ws, jnp.transpose(d_bs_t), d_norm2, d_final_g]
    zeros = jnp.zeros((D_MODEL,), F32)

    def flat(parts, last):
        return jnp.concatenate([p.reshape(-1) for p in parts] + [last]).reshape(-1, HEAD_PAIR)

    part = flat(small_g, loss_cols.reshape(-1))
    gathered = _all_gather_rows([part], "small_grads_all_gather", vmem=True)[0]
    g_s, d_s, nm_s, nv_s, loss = _small_update(gathered, flat(small_w, zeros), flat(small_m, zeros),
                                               flat(small_v, zeros), D_MODEL // HEAD_PAIR)

    def unflat(vec):
        vec = vec.reshape(-1)
        out, at = [], 0
        for wgt in small_w:
            out.append(vec[at:at + wgt.size].reshape(wgt.shape))
            at += wgt.size
        return out

    small = [unflat(a) for a in (g_s, d_s, nm_s, nv_s)]

    big_w = [w_in, w_proj_attn, w_proj_sgu, w_out, w_ffn_gate, w_ffn_up, w_ffn_down]
    big_m = [m_w_in, m_w_proj_attn, m_w_proj_sgu, m_w_out, m_w_ffn_gate, m_w_ffn_up, m_w_ffn_down]
    big_v = [v_w_in, v_w_proj_attn, v_w_proj_sgu, v_w_out, v_w_ffn_gate, v_w_ffn_up, v_w_ffn_down]
    big_out = []
    for wgt, g, mm, vv, nm in zip(big_w, g_big, big_m, big_v, names):
        d, nm_, nv_ = _adamw(wgt[0], g, mm[0], vv[0], "adamw_" + nm)
        big_out.append([a[None] for a in (g, d, nm_, nv_)])

    small_at = {0: 0, 2: 1, 3: 2, 4: 3, 5: 4, 9: 5, 13: 6}
    big_at = {1: 0, 6: 1, 7: 2, 8: 3, 10: 4, 11: 5, 12: 6}
    outs = [loss[0, 0], dx[None]]
    for kind in range(4):
        for idx in range(14):
            outs.append(small[kind][small_at[idx]] if idx in small_at else big_out[big_at[idx]][kind])
    return tuple(outs)
```

```python
import functools
import math

import jax
import jax.numpy as jnp
from jax import lax
from jax.experimental import pallas as pl
from jax.experimental.pallas import tpu as pltpu

F32 = jnp.float32
BF16 = jnp.bfloat16

D_MODEL = 1024
HEAD_PAIR = 128
ATTN_W = 512
DILATIONS = (1, 4, 16)
BLK = 128
ROPE_HALF = 8
ROPE_THETA = 500000.0
SGU_W = 512
QKV_COLS = 4608
IN_COLS = 7680
D_FF = 2816
FF_TILE = 1408
EPS = 1e-6
N_DEV = 8
MASKED = -1e30

ADAM_LR = 0.001
ADAM_B1 = 0.9
ADAM_B2 = 0.999
ADAM_EPS = 1e-08
ADAM_WD = 0.01
ADAM_STEP = 10

MIB = 1024 * 1024
MESH = pl.DeviceIdType.MESH
ANY = pl.BlockSpec(memory_space=pl.ANY)


def _params(sem, vmem_mib):
    return pltpu.CompilerParams(dimension_semantics=sem, vmem_limit_bytes=vmem_mib * MIB)


def _nt(a, b):
    return lax.dot_general(a, b, (((1,), (1,)), ((), ())), preferred_element_type=F32)


def _nn(a, b):
    return lax.dot_general(a, b, (((1,), (0,)), ((), ())), preferred_element_type=F32)


def _tn(a, b):
    return lax.dot_general(a, b, (((0,), (0,)), ((), ())), preferred_element_type=F32)


class _Exchange:
    def __init__(self, arrays, out_shapes, sem_shapes, phases):
        self.arrays, self.out_shapes, self.sem_shapes, self.phases = list(arrays), out_shapes, sem_shapes, phases

    def run_alone(self, name, vmem=False):
        n_in, n_out = len(self.arrays), len(self.out_shapes)

        def body(*refs):
            start, middle, finish = self.phases(refs[:n_in], refs[n_in:n_in + n_out], refs[n_in + n_out:])
            start()
            middle()
            finish()

        spec = pl.BlockSpec(memory_space=pltpu.VMEM) if vmem else ANY
        return pl.pallas_call(
            body, name=name, out_shape=self.out_shapes, in_specs=[spec] * n_in, out_specs=[spec] * n_out,
            scratch_shapes=self.sem_shapes, compiler_params=pltpu.CompilerParams(vmem_limit_bytes=32 * MIB),
        )(*self.arrays)


def _call(body, *, name, grid, in_specs, out_specs, out_shape, args, vmem_mib, scratch_shapes=(), carried=None):
    n_in, n_out, n_scr = len(in_specs), len(out_specs), len(scratch_shapes)
    sem = ("arbitrary",) * len(grid)
    if carried is None:
        outs = pl.pallas_call(
            body, name=name, grid=grid, in_specs=in_specs, out_specs=out_specs, out_shape=out_shape,
            scratch_shapes=list(scratch_shapes), compiler_params=_params(sem, vmem_mib))(*args)
        return list(outs), []
    c_in, c_out = len(carried.arrays), len(carried.out_shapes)
    total = math.prod(grid)

    def full(*refs):
        own_in, car_in = refs[:n_in], refs[n_in:n_in + c_in]
        at = n_in + c_in
        own_out, car_out = refs[at:at + n_out], refs[at + n_out:at + n_out + c_out]
        at += n_out + c_out
        own_scr, sems = refs[at:at + n_scr], refs[at + n_scr:]
        step = pl.program_id(0)
        for axis in range(1, len(grid)):
            step = step * grid[axis] + pl.program_id(axis)
        start, middle, finish = carried.phases(car_in, car_out, sems)
        pl.when(step == 0)(start)
        pl.when(step == total // 2)(middle)
        body(*own_in, *own_out, *own_scr)
        pl.when(step == total - 1)(finish)

    outs = pl.pallas_call(
        full, name=name, grid=grid, in_specs=list(in_specs) + [ANY] * c_in,
        out_specs=list(out_specs) + [ANY] * c_out, out_shape=list(out_shape) + list(carried.out_shapes),
        scratch_shapes=list(scratch_shapes) + list(carried.sem_shapes),
        compiler_params=_params(sem, vmem_mib))(*args, *carried.arrays)
    return list(outs[:n_out]), list(outs[n_out:])


def _all_gather(shards):
    n = len(shards)

    def phases(ins, outs, sems):
        send_sems, recv_sems, local_sems = sems
        x, y, c = lax.axis_index("x"), lax.axis_index("y"), lax.axis_index("c")
        me, sibling = (x, y, c), (x, y, 1 - c)
        chips = [(1 - x, y), (x, 1 - y), (1 - x, 1 - y)]

        def rows(m, px, py, pc):
            r = ins[m].shape[0]
            return outs[m].at[pl.ds((4 * px + 2 * py + pc) * r, r), :]

        def copy(m, k, block, to, src=None):
            return pltpu.make_async_remote_copy(
                src_ref=rows(m, *block) if src is None else src, dst_ref=rows(m, *block),
                send_sem=send_sems.at[m, k], recv_sem=recv_sems.at[m, k],
                device_id=to, device_id_type=MESH)

        def mine(m):
            return pltpu.make_async_copy(ins[m], rows(m, *me), local_sems.at[m])

        def first(m):
            return [copy(m, 0, me, sibling, src=ins[m])] + [
                copy(m, 1 + j, me, (*chip, c), src=ins[m]) for j, chip in enumerate(chips)]

        def passed(m):
            return [copy(m, 4 + j, (*chip, c), sibling) for j, chip in enumerate(chips)]

        def start():
            for m in range(n):
                mine(m).start()
            for m in range(n):
                for cp in first(m):
                    cp.start()

        def middle():
            for m in range(n):
                for j, chip in enumerate(chips):
                    copy(m, 1 + j, (*chip, c), me).wait_recv()
                    passed(m)[j].start()

        def finish():
            for m in range(n):
                copy(m, 0, sibling, me).wait_recv()
                for j, chip in enumerate(chips):
                    copy(m, 4 + j, (*chip, 1 - c), me).wait_recv()
            for m in range(n):
                for cp in first(m) + passed(m):
                    cp.wait_send()
                mine(m).wait()

        return start, middle, finish

    return _Exchange(
        shards, [jax.ShapeDtypeStruct((N_DEV * s.shape[0], s.shape[1]), s.dtype) for s in shards],
        [pltpu.SemaphoreType.DMA((n, 7)), pltpu.SemaphoreType.DMA((n, 7)), pltpu.SemaphoreType.DMA((n,))], phases)


def _pair_exchange(grads):
    n = len(grads)

    def phases(ins, outs, sems):
        send_sems, recv_sems = sems
        x, y, c = lax.axis_index("x"), lax.axis_index("y"), lax.axis_index("c")

        def copy(m):
            return pltpu.make_async_remote_copy(
                src_ref=ins[m].at[:, 1 - c], dst_ref=outs[m], send_sem=send_sems.at[m], recv_sem=recv_sems.at[m],
                device_id=(x, y, 1 - c), device_id_type=MESH)

        def start():
            for m in range(n):
                copy(m).start()

        def finish():
            for m in range(n):
                copy(m).wait()

        return start, lambda: None, finish

    return _Exchange(grads, [jax.ShapeDtypeStruct((4,) + g.shape[2:], g.dtype) for g in grads],
                     [pltpu.SemaphoreType.DMA((n,)), pltpu.SemaphoreType.DMA((n,))], phases)


def _chip_exchange(pair_sums):
    n = len(pair_sums)

    def phases(ins, outs, sems):
        send_sems, recv_sems = sems
        x, y, c = lax.axis_index("x"), lax.axis_index("y"), lax.axis_index("c")
        chips = [(1 - x, y), (x, 1 - y), (1 - x, 1 - y)]

        def copies():
            return [pltpu.make_async_remote_copy(
                src_ref=ins[m].at[2 * px + py], dst_ref=outs[m].at[j],
                send_sem=send_sems.at[m, j], recv_sem=recv_sems.at[m, j],
                device_id=(px, py, c), device_id_type=MESH)
                for m in range(n) for j, (px, py) in enumerate(chips)]

        def start():
            for cp in copies():
                cp.start()

        def finish():
            for cp in copies():
                cp.wait_recv()
            for cp in copies():
                cp.wait_send()

        return start, lambda: None, finish

    return _Exchange(pair_sums, [jax.ShapeDtypeStruct((3,) + p.shape[1:], p.dtype) for p in pair_sums],
                     [pltpu.SemaphoreType.DMA((n, 3)), pltpu.SemaphoreType.DMA((n, 3))], phases)


N_SLABS = ATTN_W // HEAD_PAIR


def _slab_scratch(tm):
    return pltpu.VMEM((N_SLABS, tm, HEAD_PAIR), F32)


def _rows_by_residue(dst_ref, slab_ref, r, tr, dtype):
    for rho in range(r):
        for s in range(N_SLABS):
            dst_ref[rho, :, s * HEAD_PAIR:(s + 1) * HEAD_PAIR] = (
                slab_ref[s, pl.ds(rho, tr, stride=r), :].astype(dtype))


def _rows_by_token(slab_ref, src_ref, r, tr):
    for rho in range(r):
        for s in range(N_SLABS):
            slab_ref[s, pl.ds(rho, tr, stride=r), :] = src_ref[rho, :, s * HEAD_PAIR:(s + 1) * HEAD_PAIR]


def _rope_tables(pos_col, inv_freq_row, tm):
    T = pos_col.shape[0]

    def body(pos_ref, invf_ref, cos_ref, s1_ref, s2_ref):
        ang = pos_ref[...].astype(F32) * invf_ref[...]
        lane = lax.broadcasted_iota(jnp.int32, (1, HEAD_PAIR), 1) % 64
        cs, sn = jnp.cos(ang), jnp.sin(ang)
        cos_ref[...] = jnp.where(lane < 2 * ROPE_HALF, cs, 1.0)
        s1_ref[...] = jnp.where(lane < ROPE_HALF, -sn, 0.0)
        s2_ref[...] = jnp.where((lane >= ROPE_HALF) & (lane < 2 * ROPE_HALF), sn, 0.0)

    tab = jax.ShapeDtypeStruct((T, HEAD_PAIR), F32)
    row = pl.BlockSpec((tm, HEAD_PAIR), lambda i: (i, 0))
    return pl.pallas_call(
        body, name="rope_tables", grid=(T // tm,), out_shape=[tab] * 3,
        in_specs=[pl.BlockSpec((tm, 1), lambda i: (i, 0)), pl.BlockSpec((1, HEAD_PAIR), lambda i: (0, 0))],
        out_specs=[row] * 3, compiler_params=_params(("parallel",), 16),
    )(pos_col, inv_freq_row)


def _rope(y, cos, s1, s2):
    w = y.shape[1]
    rep = w // HEAD_PAIR
    return (y * jnp.tile(cos, (1, rep)) + pltpu.roll(y, w - ROPE_HALF, 1) * jnp.tile(s1, (1, rep))
            + pltpu.roll(y, ROPE_HALF, 1) * jnp.tile(s2, (1, rep)))


def _rope_transposed(dy, cos, s1, s2):
    w = dy.shape[1]
    rep = w // HEAD_PAIR
    return (dy * jnp.tile(cos, (1, rep)) + pltpu.roll(dy * jnp.tile(s1, (1, rep)), ROPE_HALF, 1)
            + pltpu.roll(dy * jnp.tile(s2, (1, rep)), w - ROPE_HALF, 1))


def _rmsnorm_fwd(x, g, name, tm):
    T = x.shape[0]

    def body(x_ref, g_ref, h_ref):
        xf = x_ref[...]
        r = lax.rsqrt(jnp.mean(xf * xf, axis=-1, keepdims=True) + EPS)
        h_ref[...] = (xf * r * g_ref[...]).astype(BF16)

    return pl.pallas_call(
        body, name=name, grid=(T // tm,), out_shape=jax.ShapeDtypeStruct((T, D_MODEL), BF16),
        in_specs=[pl.BlockSpec((tm, D_MODEL), lambda i: (i, 0)), pl.BlockSpec((1, D_MODEL), lambda i: (0, 0))],
        out_specs=pl.BlockSpec((tm, D_MODEL), lambda i: (i, 0)), compiler_params=_params(("parallel",), 24),
    )(x, g)


def _resident(shape, block=None):
    at = (0,) * len(shape) if block is None else block
    return pl.BlockSpec(shape, lambda *_: at, pipeline_mode=pl.Buffered(1))


def _qkv_proj(h, win_t, tables, tm, carried=None):
    T = h.shape[0]

    def body(h_ref, w_ref, cos_ref, s1_ref, s2_ref, o0, o1, o2, *slabs):
        hv = h_ref[...]
        cos, s1, s2 = cos_ref[...], s1_ref[...], s2_ref[...]
        for kind in range(3):
            for g, (o_ref, r) in enumerate(zip((o0, o1, o2), DILATIONS)):
                blk = 3 * kind + g
                y = _nt(hv, w_ref[blk * ATTN_W:(blk + 1) * ATTN_W, :])
                if kind < 2:
                    y = _rope(y, cos, s1, s2)
                if kind == 0:
                    y = y * 0.125
                cols = slice(kind * ATTN_W, (kind + 1) * ATTN_W)
                if r == 1:
                    o_ref[0, :, cols] = y.astype(BF16)
                    continue
                slab = slabs[blk % len(slabs)]
                for s in range(N_SLABS):
                    slab[s] = y[:, s * HEAD_PAIR:(s + 1) * HEAD_PAIR]
                for rho in range(r):
                    for s in range(N_SLABS):
                        at = kind * ATTN_W + s * HEAD_PAIR
                        o_ref[rho, :, at:at + HEAD_PAIR] = slab[s, pl.ds(rho, tm // r, stride=r), :].astype(BF16)

    row = pl.BlockSpec((tm, HEAD_PAIR), lambda i: (i, 0))
    return _call(
        body, name="qkv_proj", grid=(T // tm,),
        out_shape=[jax.ShapeDtypeStruct((r, T // r, 3 * ATTN_W), BF16) for r in DILATIONS],
        in_specs=[pl.BlockSpec((tm, D_MODEL), lambda i: (i, 0)), _resident((QKV_COLS, D_MODEL)), row, row, row],
        out_specs=[pl.BlockSpec((r, tm // r, 3 * ATTN_W), lambda i: (0, i, 0)) for r in DILATIONS],
        scratch_shapes=[_slab_scratch(tm)] * 3, vmem_mib=48, args=(h, win_t, *tables), carried=carried)


def _uv_gate_proj(h, win_t, tm, carried=None):
    T = h.shape[0]
    half = (IN_COLS - QKV_COLS) // 2

    def body(h_ref, wa_ref, wb_ref, o_ref):
        hv = h_ref[...]
        o_ref[:, :half] = _nt(hv, wa_ref[...])
        o_ref[:, half:] = _nt(hv, wb_ref[...])

    blk0 = QKV_COLS // half
    return _call(
        body, name="uv_gate_proj", grid=(T // tm,), out_shape=[jax.ShapeDtypeStruct((T, 2 * half), F32)],
        in_specs=[pl.BlockSpec((tm, D_MODEL), lambda i: (i, 0)), _resident((half, D_MODEL), (blk0, 0)),
                  _resident((half, D_MODEL), (blk0 + 1, 0))],
        out_specs=[pl.BlockSpec((tm, 2 * half), lambda i: (i, 0))], vmem_mib=40, args=(h, win_t, win_t),
        carried=carried)


def _band_mask(n):
    row = lax.broadcasted_iota(jnp.int32, (BLK, 2 * BLK), 0)
    col = lax.broadcasted_iota(jnp.int32, (BLK, 2 * BLK), 1)
    has_prev = (jnp.zeros_like(row) + n) > 0
    return ((col < BLK) & (col >= row) & has_prev) | ((col >= BLK) & (col - BLK <= row))


def _head_lanes():
    lane = lax.broadcasted_iota(jnp.int32, (1, HEAD_PAIR), 1)
    return lane < 64, lane >= 64


def _attn_fwd(qkv, group):
    r, L, _ = qkv.shape
    nb = L // BLK

    def body(q_ref, kp_ref, kc_ref, vp_ref, vc_ref, o_ref, lse_ref):
        valid = _band_mask(pl.program_id(1))
        head0, head1 = _head_lanes()
        for p in range(ATTN_W // HEAD_PAIR):
            sl = slice(p * HEAD_PAIR, (p + 1) * HEAD_PAIR)
            q = q_ref[:, sl]
            k2 = jnp.concatenate([kp_ref[:, sl], kc_ref[:, sl]], axis=0)
            v2 = jnp.concatenate([vp_ref[:, sl], vc_ref[:, sl]], axis=0)
            outs, lses = [], []
            for half in (head0, head1):
                s = jnp.where(valid, _nt(jnp.where(half, q, jnp.zeros_like(q)), k2), MASKED)
                m = jnp.max(s, axis=1, keepdims=True)
                e = jnp.exp(s - m)
                den = jnp.sum(e, axis=1, keepdims=True)
                outs.append(_nn(e.astype(BF16), v2) / den)
                lses.append(m + jnp.log(den))
            o_ref[:, sl] = jnp.where(head0, outs[0], outs[1])
            lse_ref[:, sl] = jnp.where(head0, lses[0], lses[1])

    def blk(kind, prev):
        if prev:
            return pl.BlockSpec((None, BLK, ATTN_W), lambda rho, n: (rho, jnp.maximum(n - 1, 0), kind))
        return pl.BlockSpec((None, BLK, ATTN_W), lambda rho, n: (rho, n, kind))

    out = pl.BlockSpec((None, BLK, ATTN_W), lambda rho, n: (rho, n, 0))
    res = jax.ShapeDtypeStruct((r, L, ATTN_W), F32)
    return pl.pallas_call(
        body, name=f"attn_fwd_g{group}", grid=(r, nb), out_shape=[res, res],
        in_specs=[blk(0, False), blk(1, True), blk(1, False), blk(2, True), blk(2, False)],
        out_specs=[out, out], compiler_params=_params(("parallel", "arbitrary"), 24),
    )(qkv, qkv, qkv, qkv, qkv)


def _attn_combine(outs, lses, tm):
    T = outs[0].shape[1]

    def body(o0, l0, o1, l1, o2, l2, attn_ref, lse_ref, so1, sl1, so2, sl2):
        for o_in, l_in, so, sl, r in ((o1, l1, so1, sl1, DILATIONS[1]), (o2, l2, so2, sl2, DILATIONS[2])):
            _rows_by_token(so, o_in, r, tm // r)
            _rows_by_token(sl, l_in, r, tm // r)
        for s in range(N_SLABS):
            lanes = slice(s * HEAD_PAIR, (s + 1) * HEAD_PAIR)
            a0, a1, a2 = l0[0, :, lanes], sl1[s], sl2[s]
            mx = jnp.maximum(jnp.maximum(a0, a1), a2)
            e0, e1, e2 = jnp.exp(a0 - mx), jnp.exp(a1 - mx), jnp.exp(a2 - mx)
            tot = e0 + e1 + e2
            attn_ref[:, lanes] = (e0 * o0[0, :, lanes] + e1 * so1[s] + e2 * so2[s]) / tot
            lse_ref[:, lanes] = mx + jnp.log(tot)

    def grouped(r):
        return pl.BlockSpec((r, tm // r, ATTN_W), lambda i: (0, i, 0))

    tok = pl.BlockSpec((tm, ATTN_W), lambda i: (i, 0))
    res = jax.ShapeDtypeStruct((T, ATTN_W), F32)
    ins, specs = [], []
    for g, r in enumerate(DILATIONS):
        ins += [outs[g], lses[g]]
        specs += [grouped(r), grouped(r)]
    return pl.pallas_call(
        body, name="attn_combine", grid=(T // tm,), out_shape=[res, res], in_specs=specs, out_specs=[tok, tok],
        scratch_shapes=[_slab_scratch(tm)] * 4, compiler_params=_params(("parallel",), 32),
    )(*ins)


def _gelu(x):
    return 0.5 * x * (1.0 + lax.erf(x * (1.0 / math.sqrt(2.0))))


def _gelu_grad(x):
    return 0.5 * (1.0 + lax.erf(x * (1.0 / math.sqrt(2.0)))) + x * jnp.exp(-0.5 * x * x) * (1.0 / math.sqrt(2.0 * math.pi))


def _causal():
    row = lax.broadcasted_iota(jnp.int32, (BLK, BLK), 0)
    col = lax.broadcasted_iota(jnp.int32, (BLK, BLK), 1)
    return col <= row


def _bias_lanes(bt):
    grp = lax.broadcasted_iota(jnp.int32, (1, SGU_W), 1) // 64
    out = jnp.zeros((BLK, SGU_W), F32)
    for g in range(8):
        out = jnp.where(grp == g, bt[:, g:g + 1], out)
    return out


def _sgu_normalise(uv, ln_g, ln_b):
    z = _gelu(uv)
    u, v = z[:, :SGU_W], z[:, SGU_W:]
    mu = jnp.mean(v, axis=-1, keepdims=True)
    xc = v - mu
    rstd = lax.rsqrt(jnp.mean(xc * xc, axis=-1, keepdims=True) + EPS)
    xh = xc * rstd
    return u, xh, rstd, xh * ln_g + ln_b


def _sgu_mix(wc_ref, vb, head0):
    chunks = []
    for ch in range(vb.shape[0] // BLK):
        pairs = []
        for p in range(SGU_W // HEAD_PAIR):
            v_pair = vb[ch * BLK:(ch + 1) * BLK, p * HEAD_PAIR:(p + 1) * HEAD_PAIR]
            pairs.append(jnp.where(head0, _nn(wc_ref[2 * p], v_pair), _nn(wc_ref[2 * p + 1], v_pair)))
        chunks.append(jnp.concatenate(pairs, axis=1))
    return jnp.concatenate(chunks, axis=0)


def _sgu_fwd(uvg, ln_g, ln_b, w_s, b_t, tm):
    T = uvg.shape[0]

    def body(uv_ref, g_ref, b_ref, w_ref, bt_ref, o_ref, wc_ref, bias_ref):
        @pl.when(pl.program_id(0) == 0)
        def _():
            causal = _causal()
            for g in range(8):
                wc_ref[g] = jnp.where(causal, w_ref[g], 0.0).astype(BF16)
            bias_ref[...] = _bias_lanes(bt_ref[...])

        u, _, _, vn = _sgu_normalise(uv_ref[...], g_ref[...], b_ref[...])
        mixed = _sgu_mix(wc_ref, vn.astype(BF16), _head_lanes()[0])
        o_ref[...] = (u * (mixed + jnp.tile(bias_ref[...], (tm // BLK, 1)))).astype(BF16)

    vec = pl.BlockSpec((1, SGU_W), lambda i: (0, 0))
    return pl.pallas_call(
        body, name="sgu_fwd", grid=(T // tm,), out_shape=jax.ShapeDtypeStruct((T, SGU_W), BF16),
        in_specs=[pl.BlockSpec((tm, 2 * SGU_W), lambda i: (i, 0)), vec, vec,
                  pl.BlockSpec((8, BLK, BLK), lambda i: (0, 0, 0)), pl.BlockSpec((BLK, 8), lambda i: (0, 0))],
        out_specs=pl.BlockSpec((tm, SGU_W), lambda i: (i, 0)),
        scratch_shapes=[pltpu.VMEM((8, BLK, BLK), BF16), pltpu.VMEM((BLK, SGU_W), F32)],
        compiler_params=_params(("arbitrary",), 32),
    )(uvg, ln_g, ln_b, w_s, b_t)


def _merge_fwd(attn, sgu, wpa_t, wps_t, uvg, tm):
    T = attn.shape[0]

    def body(attn_ref, sgu_ref, wpa_ref, wps_ref, ga_ref, gb_ref, pa_ref, ps_ref, m_ref):
        pa = _nt(attn_ref[...].astype(BF16), wpa_ref[...])
        ps = _nt(sgu_ref[...], wps_ref[...])
        pa_ref[...] = pa
        ps_ref[...] = ps
        m_ref[...] = (jax.nn.sigmoid(ga_ref[...]) * pa + jax.nn.sigmoid(gb_ref[...]) * ps).astype(BF16)

    half = pl.BlockSpec((tm, ATTN_W), lambda i: (i, 0))
    wide = pl.BlockSpec((tm, D_MODEL), lambda i: (i, 0))
    w = pl.BlockSpec((D_MODEL, ATTN_W), lambda i: (0, 0))
    res = jax.ShapeDtypeStruct((T, D_MODEL), F32)
    return pl.pallas_call(
        body, name="merge_fwd", grid=(T // tm,),
        out_shape=[res, res, jax.ShapeDtypeStruct((T, D_MODEL), BF16)],
        in_specs=[half, half, w, w, pl.BlockSpec((tm, D_MODEL), lambda i: (i, 1)),
                  pl.BlockSpec((tm, D_MODEL), lambda i: (i, 2))],
        out_specs=[wide, wide, wide], compiler_params=_params(("parallel",), 40),
    )(attn, sgu, wpa_t, wps_t, uvg, uvg)


def _out_residual(merged, w_out, x, g2, tm):
    T = x.shape[0]

    def body(m_ref, w_ref, x_ref, g_ref, o_ref, h_ref):
        x1 = x_ref[...] + _nn(m_ref[...], w_ref[...])
        o_ref[...] = x1
        r = lax.rsqrt(jnp.mean(x1 * x1, axis=-1, keepdims=True) + EPS)
        h_ref[...] = (x1 * r * g_ref[...]).astype(BF16)

    wide = pl.BlockSpec((tm, D_MODEL), lambda i: (i, 0))
    return pl.pallas_call(
        body, name="out_residual", grid=(T // tm,),
        out_shape=[jax.ShapeDtypeStruct((T, D_MODEL), F32), jax.ShapeDtypeStruct((T, D_MODEL), BF16)],
        in_specs=[wide, _resident((D_MODEL, D_MODEL)), wide, pl.BlockSpec((1, D_MODEL), lambda i: (0, 0))],
        out_specs=[wide, wide], compiler_params=_params(("parallel",), 32),
    )(merged, w_out, x, g2)


def _ffn_fwd(h2, wg_t, wu_t, tm):
    T = h2.shape[0]

    def body(h_ref, wg_ref, wu_ref, gate_ref, up_ref, ff_ref):
        h = h_ref[...]
        for j in range(D_FF // FF_TILE):
            cols = slice(j * FF_TILE, (j + 1) * FF_TILE)
            gate, up = _nt(h, wg_ref[cols, :]), _nt(h, wu_ref[cols, :])
            gate_ref[:, cols] = gate.astype(BF16)
            up_ref[:, cols] = up.astype(BF16)
            ff_ref[:, cols] = (gate * jax.nn.sigmoid(gate) * up).astype(BF16)

    w = _resident((D_FF, D_MODEL))
    o = pl.BlockSpec((tm, D_FF), lambda i: (i, 0))
    res = jax.ShapeDtypeStruct((T, D_FF), BF16)
    return pl.pallas_call(
        body, name="ffn_fwd", grid=(T // tm,), out_shape=[res, res, res],
        in_specs=[pl.BlockSpec((tm, D_MODEL), lambda i: (i, 0)), w, w], out_specs=[o, o, o],
        compiler_params=_params(("parallel",), 52),
    )(h2, wg_t, wu_t)


def _down_loss(ff, w_down, x1, final_g, target, tm):
    T = x1.shape[0]

    def body(ff_ref, w_ref, x1_ref, g_ref, t_ref, dx_ref, dxb_ref, loss_ref, dg_ref):
        @pl.when(pl.program_id(0) == 0)
        def _():
            loss_ref[...] = jnp.zeros_like(loss_ref)
            dg_ref[...] = jnp.zeros_like(dg_ref)

        x2 = x1_ref[...] + _nn(ff_ref[...], w_ref[...])
        g = g_ref[...]
        r = lax.rsqrt(jnp.mean(x2 * x2, axis=-1, keepdims=True) + EPS)
        xh = x2 * r
        err = xh * g - t_ref[...]
        loss_ref[...] += jnp.sum(err * err, axis=0, keepdims=True) * (0.5 / D_MODEL)
        dy = err * (1.0 / D_MODEL)
        dg_ref[...] += jnp.sum(dy * xh, axis=0, keepdims=True)
        dxh = dy * g
        dx = r * (dxh - xh * jnp.mean(dxh * xh, axis=-1, keepdims=True))
        dx_ref[...] = dx
        dxb_ref[...] = dx.astype(BF16)

    wide = pl.BlockSpec((tm, D_MODEL), lambda i: (i, 0))
    vec = pl.BlockSpec((1, D_MODEL), lambda i: (0, 0))
    vec_shape = jax.ShapeDtypeStruct((1, D_MODEL), F32)
    return pl.pallas_call(
        body, name="down_loss", grid=(T // tm,),
        out_shape=[jax.ShapeDtypeStruct((T, D_MODEL), F32), jax.ShapeDtypeStruct((T, D_MODEL), BF16),
                   vec_shape, vec_shape],
        in_specs=[pl.BlockSpec((tm, D_FF), lambda i: (i, 0)), _resident((D_FF, D_MODEL)), wide, vec, wide],
        out_specs=[wide, wide, vec, vec],
        compiler_params=_params(("arbitrary",), 40),
    )(ff, w_down, x1, final_g, target)


def _rmsnorm_bwd(dh, xin, g, d_res, dg_ref):
    r = lax.rsqrt(jnp.mean(xin * xin, axis=-1, keepdims=True) + EPS)
    xh = xin * r
    dg_ref[...] += jnp.sum(dh * xh, axis=0, keepdims=True)
    dxh = dh * g
    return d_res + r * (dxh - xh * jnp.mean(dxh * xh, axis=-1, keepdims=True))


def _ffn_bwd(dx2, dx2b, w_down, wg_t, wu_t, gate, up, x1, g2, tm):
    T = dx2.shape[0]

    def body(dx_ref, dxb_ref, wd_ref, wg_ref, wu_ref, gate_ref, up_ref, x1_ref, g_ref,
             dgate_ref, dup_ref, dx1_ref, dx1b_ref, dg_ref):
        @pl.when(pl.program_id(0) == 0)
        def _():
            dg_ref[...] = jnp.zeros_like(dg_ref)

        dxb = dxb_ref[...]
        dh = jnp.zeros((tm, D_MODEL), F32)
        for j in range(D_FF // FF_TILE):
            cols = slice(j * FF_TILE, (j + 1) * FF_TILE)
            dff = _nt(dxb, wd_ref[cols, :])
            gate, up = gate_ref[:, cols].astype(F32), up_ref[:, cols].astype(F32)
            sg = jax.nn.sigmoid(gate)
            dgate = (dff * up * sg * (1.0 + gate * (1.0 - sg))).astype(BF16)
            dup = (dff * gate * sg).astype(BF16)
            dgate_ref[:, cols] = dgate
            dup_ref[:, cols] = dup
            dh += _nn(dgate, wg_ref[cols, :]) + _nn(dup, wu_ref[cols, :])
        dx1 = _rmsnorm_bwd(dh, x1_ref[...], g_ref[...], dx_ref[...], dg_ref)
        dx1_ref[...] = dx1
        dx1b_ref[...] = dx1.astype(BF16)

    wide = pl.BlockSpec((tm, D_MODEL), lambda i: (i, 0))
    ffw = pl.BlockSpec((tm, D_FF), lambda i: (i, 0))
    vec = pl.BlockSpec((1, D_MODEL), lambda i: (0, 0))
    w = _resident((D_FF, D_MODEL))
    ff_shape = jax.ShapeDtypeStruct((T, D_FF), BF16)
    return pl.pallas_call(
        body, name="ffn_bwd", grid=(T // tm,),
        out_shape=[ff_shape, ff_shape, jax.ShapeDtypeStruct((T, D_MODEL), F32),
                   jax.ShapeDtypeStruct((T, D_MODEL), BF16), jax.ShapeDtypeStruct((1, D_MODEL), F32)],
        in_specs=[wide, wide, w, w, w, ffw, ffw, wide, vec],
        out_specs=[ffw, ffw, wide, wide, vec], compiler_params=_params(("arbitrary",), 56),
    )(dx2, dx2b, w_down, wg_t, wu_t, gate, up, x1, g2)


def _mm_tn(a, b, *, tmm, tk, name, into=None, row_block0=0, rows_total=None):
    T, M = a.shape
    N = b.shape[1]
    rows_total = M if rows_total is None else rows_total

    def body(*refs):
        a_ref, b_ref, o_ref = refs[0], refs[1], refs[-1]

        @pl.when(pl.program_id(1) == 0)
        def _():
            o_ref[...] = jnp.zeros_like(o_ref)

        o_ref[...] += _tn(a_ref[...], b_ref[...])

    ins = [a, b] + ([] if into is None else [into])
    specs = [pl.BlockSpec((tk, tmm), lambda i, k: (k, i)), pl.BlockSpec((tk, N), lambda i, k: (k, 0))]
    return pl.pallas_call(
        body, name=name, grid=(M // tmm, T // tk),
        out_shape=jax.ShapeDtypeStruct((rows_total, N), F32),
        in_specs=specs + ([] if into is None else [ANY]),
        out_specs=pl.BlockSpec((tmm, N), lambda i, k: (row_block0 + i, 0)),
        input_output_aliases={} if into is None else {2: 0},
        compiler_params=_params(("parallel", "arbitrary"), 48),
    )(*ins)


def _in_bwd(dqkv, duv, dgates, win_t, x, g1, dx1, tm, carried=None):
    T = x.shape[0]
    uv0, gates0 = QKV_COLS, QKV_COLS + 2 * SGU_W

    def body(dq_ref, du_ref, dgt_ref, w_ref, x_ref, g_ref, d_ref, dx_ref, dg_ref):
        @pl.when(pl.program_id(0) == 0)
        def _():
            dg_ref[...] = jnp.zeros_like(dg_ref)

        dh = (_nn(dq_ref[...], w_ref[:uv0, :]) + _nn(du_ref[...], w_ref[uv0:gates0, :])
              + _nn(dgt_ref[...], w_ref[gates0:, :]))
        dx_ref[...] = _rmsnorm_bwd(dh, x_ref[...], g_ref[...], d_ref[...], dg_ref)

    def cols(n):
        return pl.BlockSpec((tm, n), lambda i: (i, 0))

    wide = cols(D_MODEL)
    vec = pl.BlockSpec((1, D_MODEL), lambda i: (0, 0))
    return _call(
        body, name="in_bwd", grid=(T // tm,),
        out_shape=[jax.ShapeDtypeStruct((T, D_MODEL), F32), jax.ShapeDtypeStruct((1, D_MODEL), F32)],
        in_specs=[cols(QKV_COLS), cols(2 * SGU_W), cols(2 * D_MODEL), _resident((IN_COLS, D_MODEL)), wide, vec, wide],
        out_specs=[wide, vec], vmem_mib=56, args=(dqkv, duv, dgates, win_t, x, g1, dx1), carried=carried)


def _merge_bwd(dx1b, w_out, pa, ps, uvg, tm, carried=None):
    T = dx1b.shape[0]

    def body(dx_ref, w_ref, pa_ref, ps_ref, ga_ref, gb_ref, dpa_ref, dps_ref, dg_ref):
        dm = _nt(dx_ref[...], w_ref[...])
        ga, gb = jax.nn.sigmoid(ga_ref[...]), jax.nn.sigmoid(gb_ref[...])
        dpa_ref[...] = (dm * ga).astype(BF16)
        dps_ref[...] = (dm * gb).astype(BF16)
        dg_ref[:, :D_MODEL] = (dm * pa_ref[...] * ga * (1.0 - ga)).astype(BF16)
        dg_ref[:, D_MODEL:] = (dm * ps_ref[...] * gb * (1.0 - gb)).astype(BF16)

    wide = pl.BlockSpec((tm, D_MODEL), lambda i: (i, 0))
    res = jax.ShapeDtypeStruct((T, D_MODEL), BF16)
    return _call(
        body, name="merge_bwd", grid=(T // tm,),
        out_shape=[res, res, jax.ShapeDtypeStruct((T, 2 * D_MODEL), BF16)],
        in_specs=[wide, _resident((D_MODEL, D_MODEL)), wide, wide,
                  pl.BlockSpec((tm, D_MODEL), lambda i: (i, 1)), pl.BlockSpec((tm, D_MODEL), lambda i: (i, 2))],
        out_specs=[wide, wide, pl.BlockSpec((tm, 2 * D_MODEL), lambda i: (i, 0))],
        vmem_mib=48, args=(dx1b, w_out, pa, ps, uvg, uvg), carried=carried)


def _proj_bwd(dpa, dps, wpa_t, wps_t, tm):
    T = dpa.shape[0]

    def body(dpa_ref, dps_ref, wpa_ref, wps_ref, dattn_ref, dsgu_ref):
        dattn_ref[...] = _nn(dpa_ref[...], wpa_ref[...])
        dsgu_ref[...] = _nn(dps_ref[...], wps_ref[...])

    wide = pl.BlockSpec((tm, D_MODEL), lambda i: (i, 0))
    half = pl.BlockSpec((tm, ATTN_W), lambda i: (i, 0))
    w = pl.BlockSpec((D_MODEL, ATTN_W), lambda i: (0, 0))
    res = jax.ShapeDtypeStruct((T, ATTN_W), F32)
    return pl.pallas_call(
        body, name="proj_bwd", grid=(T // tm,), out_shape=[res, res], in_specs=[wide, wide, w, w],
        out_specs=[half, half], compiler_params=_params(("parallel",), 32),
    )(dpa, dps, wpa_t, wps_t)


def _sgu_bwd(uvg, dsgu, ln_g, ln_b, w_s, b_t, tm, carried=None):
    T = uvg.shape[0]
    nsteps = T // tm

    def body(uv_ref, ds_ref, g_ref, b_ref, w_ref, bt_ref, duv_ref, dw_ref, dbt_ref, dg_ref, db_ref,
             wc_ref, wct_ref, bias_ref, dbias_ref):
        step = pl.program_id(0)
        head0, head1 = _head_lanes()

        @pl.when(step == 0)
        def _():
            causal = _causal()
            for g in range(8):
                wc = jnp.where(causal, w_ref[g], 0.0)
                wc_ref[g] = wc.astype(BF16)
                wct_ref[g] = wc.T.astype(BF16)
            bias_ref[...] = _bias_lanes(bt_ref[...])
            dbias_ref[...] = jnp.zeros_like(dbias_ref)
            dw_ref[...] = jnp.zeros_like(dw_ref)
            dg_ref[...] = jnp.zeros_like(dg_ref)
            db_ref[...] = jnp.zeros_like(db_ref)

        uv = uv_ref[...]
        ln_gain = g_ref[...]
        u, xh, rstd, vn = _sgu_normalise(uv, ln_gain, b_ref[...])
        vb = vn.astype(BF16)
        mixed = _sgu_mix(wc_ref, vb, head0) + jnp.tile(bias_ref[...], (tm // BLK, 1))
        dout = ds_ref[...]
        du = dout * mixed
        dmixed = dout * u
        dmb = dmixed.astype(BF16)
        dvn_chunks = []
        for ch in range(tm // BLK):
            rows = slice(ch * BLK, (ch + 1) * BLK)
            dbias_ref[...] += dmixed[rows]
            pairs = []
            for p in range(SGU_W // HEAD_PAIR):
                lanes = slice(p * HEAD_PAIR, (p + 1) * HEAD_PAIR)
                dm_pair, v_pair = dmb[rows, lanes], vb[rows, lanes]
                acc = jnp.zeros((BLK, HEAD_PAIR), F32)
                for hh, half in enumerate((head0, head1)):
                    dm_h = jnp.where(half, dm_pair, jnp.zeros_like(dm_pair))
                    dw_ref[2 * p + hh] += _nt(dm_h, v_pair)
                    acc += _nn(wct_ref[2 * p + hh], dm_h)
                pairs.append(acc)
            dvn_chunks.append(jnp.concatenate(pairs, axis=1))
        dvn = jnp.concatenate(dvn_chunks, axis=0)
        dg_ref[...] += jnp.sum(dvn * xh, axis=0, keepdims=True)
        db_ref[...] += jnp.sum(dvn, axis=0, keepdims=True)
        dxh = dvn * ln_gain
        dv = rstd * (dxh - jnp.mean(dxh, axis=-1, keepdims=True) - xh * jnp.mean(dxh * xh, axis=-1, keepdims=True))
        dgelu = _gelu_grad(uv)
        duv_ref[:, :SGU_W] = (du * dgelu[:, :SGU_W]).astype(BF16)
        duv_ref[:, SGU_W:] = (dv * dgelu[:, SGU_W:]).astype(BF16)

        @pl.when(step == nsteps - 1)
        def _():
            causal = _causal()
            for g in range(8):
                dw_ref[g] = jnp.where(causal, dw_ref[g], 0.0)
            grp = lax.broadcasted_iota(jnp.int32, (1, SGU_W), 1) // 64
            col = lax.broadcasted_iota(jnp.int32, (1, 8), 1)
            dbias = dbias_ref[...]
            out = jnp.zeros((BLK, 8), F32)
            for g in range(8):
                s = jnp.sum(jnp.where(grp == g, dbias, 0.0), axis=1, keepdims=True)
                out = jnp.where(col == g, s, out)
            dbt_ref[...] = out

    vec = pl.BlockSpec((1, SGU_W), lambda i: (0, 0))
    w3 = pl.BlockSpec((8, BLK, BLK), lambda i: (0, 0, 0))
    bt = pl.BlockSpec((BLK, 8), lambda i: (0, 0))
    return _call(
        body, name="sgu_bwd", grid=(nsteps,),
        out_shape=[jax.ShapeDtypeStruct((T, 2 * SGU_W), BF16), jax.ShapeDtypeStruct((8, BLK, BLK), F32),
                   jax.ShapeDtypeStruct((BLK, 8), F32), jax.ShapeDtypeStruct((1, SGU_W), F32),
                   jax.ShapeDtypeStruct((1, SGU_W), F32)],
        in_specs=[pl.BlockSpec((tm, 2 * SGU_W), lambda i: (i, 0)), pl.BlockSpec((tm, SGU_W), lambda i: (i, 0)),
                  vec, vec, w3, bt],
        out_specs=[pl.BlockSpec((tm, 2 * SGU_W), lambda i: (i, 0)), w3, bt, vec, vec],
        scratch_shapes=[pltpu.VMEM((8, BLK, BLK), BF16), pltpu.VMEM((8, BLK, BLK), BF16),
                        pltpu.VMEM((BLK, SGU_W), F32), pltpu.VMEM((BLK, SGU_W), F32)],
        vmem_mib=40, args=(uvg, dsgu, ln_g, ln_b, w_s, b_t), carried=carried)


def _attn_bwd_prepare(dattn, attn, lse, tm):
    T = dattn.shape[0]

    def body(da_ref, at_ref, lse_ref, *outs):
        scr_da, scr_lse, scr_d = outs[-3:]
        outs = outs[:-3]
        head0, _ = _head_lanes()
        for s in range(N_SLABS):
            lanes = slice(s * HEAD_PAIR, (s + 1) * HEAD_PAIR)
            da = da_ref[:, lanes]
            pp = da * at_ref[:, lanes]
            d0 = jnp.sum(jnp.where(head0, pp, 0.0), axis=1, keepdims=True)
            d1 = jnp.sum(jnp.where(head0, 0.0, pp), axis=1, keepdims=True)
            d_pair = jnp.where(head0, d0, d1)
            scr_da[s] = da
            scr_lse[s] = lse_ref[:, lanes]
            scr_d[s] = d_pair
            outs[0][0, :, lanes] = da.astype(BF16)
            outs[1][0, :, lanes] = lse_ref[:, lanes]
            outs[2][0, :, lanes] = d_pair
        for g, r in enumerate(DILATIONS):
            if r > 1:
                for dst, scr, dt in zip(outs[3 * g:3 * g + 3], (scr_da, scr_lse, scr_d), (BF16, F32, F32)):
                    _rows_by_residue(dst, scr, r, tm // r, dt)

    tok = pl.BlockSpec((tm, ATTN_W), lambda i: (i, 0))
    shapes, specs = [], []
    for r in DILATIONS:
        spec = pl.BlockSpec((r, tm // r, ATTN_W), lambda i: (0, i, 0))
        shapes += [jax.ShapeDtypeStruct((r, T // r, ATTN_W), dt) for dt in (BF16, F32, F32)]
        specs += [spec] * 3
    return pl.pallas_call(
        body, name="attn_bwd_prepare", grid=(T // tm,), out_shape=shapes, in_specs=[tok, tok, tok],
        out_specs=specs, scratch_shapes=[_slab_scratch(tm)] * 3,
        compiler_params=_params(("parallel",), 40),
    )(dattn, attn, lse)


def _attn_bwd(qkv, dattn, lse, dsum, group, carried=None):
    r, L, _ = qkv.shape
    nb = L // BLK

    def body(q_ref, kp_ref, kc_ref, vp_ref, vc_ref, da_ref, lse_ref, d_ref, dq_ref, dk_ref, dv_ref,
             carry_k, carry_v):
        n = pl.program_id(1)

        @pl.when(n == 0)
        def _():
            carry_k[...] = jnp.zeros_like(carry_k)
            carry_v[...] = jnp.zeros_like(carry_v)

        @pl.when(n < nb)
        def _():
            valid = _band_mask(n)
            head0, head1 = _head_lanes()
            for p in range(ATTN_W // HEAD_PAIR):
                sl = slice(p * HEAD_PAIR, (p + 1) * HEAD_PAIR)
                q, da = q_ref[:, sl], da_ref[:, sl]
                k2 = jnp.concatenate([kp_ref[:, sl], kc_ref[:, sl]], axis=0)
                v2 = jnp.concatenate([vp_ref[:, sl], vc_ref[:, sl]], axis=0)
                lse_pair, d_pair = lse_ref[:, sl], d_ref[:, sl]
                dq = jnp.zeros((BLK, HEAD_PAIR), F32)
                dk2 = jnp.zeros((2 * BLK, HEAD_PAIR), F32)
                dv2 = jnp.zeros((2 * BLK, HEAD_PAIR), F32)
                for hh, half in enumerate((head0, head1)):
                    col = 64 * hh
                    q_h = jnp.where(half, q, jnp.zeros_like(q))
                    da_h = jnp.where(half, da, jnp.zeros_like(da))
                    s = _nt(q_h, k2)
                    prob = jnp.where(valid, jnp.exp(s - lse_pair[:, col:col + 1]), 0.0)
                    dprob = _nt(da_h, v2)
                    ds = (prob * (dprob - d_pair[:, col:col + 1])).astype(BF16)
                    dq += jnp.where(half, _nn(ds, k2), 0.0)
                    dk2 += _tn(ds, q_h)
                    dv2 += _tn(prob.astype(BF16), da_h)
                dq_ref[:, sl] = dq
                dk_ref[:, sl] = carry_k[:, sl] + dk2[:BLK]
                dv_ref[:, sl] = carry_v[:, sl] + dv2[:BLK]
                carry_k[:, sl] = dk2[BLK:]
                carry_v[:, sl] = dv2[BLK:]

        @pl.when(n == nb)
        def _():
            dk_ref[...] = carry_k[...]
            dv_ref[...] = carry_v[...]

    def cur(kind):
        return pl.BlockSpec((None, BLK, ATTN_W), lambda rho, n: (rho, jnp.minimum(n, nb - 1), kind))

    def prev(kind):
        return pl.BlockSpec((None, BLK, ATTN_W), lambda rho, n: (rho, jnp.clip(n - 1, 0, nb - 1), kind))

    res = jax.ShapeDtypeStruct((r, L, ATTN_W), F32)
    return _call(
        body, name=f"attn_bwd_g{group}", grid=(r, nb + 1), out_shape=[res, res, res],
        in_specs=[cur(0), prev(1), cur(1), prev(2), cur(2), cur(0), cur(0), cur(0)],
        out_specs=[cur(0), prev(0), prev(0)],
        scratch_shapes=[pltpu.VMEM((BLK, ATTN_W), F32), pltpu.VMEM((BLK, ATTN_W), F32)],
        vmem_mib=32, args=(qkv, qkv, qkv, qkv, qkv, dattn, lse, dsum), carried=carried)


def _dqkv_token_order(dqkv_groups, tables, tm):
    T = tables[0].shape[0]

    def body(*refs):
        ins = refs[:9]
        cos_ref, s1_ref, s2_ref, o_ref, scr = refs[9:]
        cos, s1, s2 = cos_ref[...], s1_ref[...], s2_ref[...]
        for g, r in enumerate(DILATIONS):
            for kind in range(3):
                src = ins[3 * g + kind]
                if r > 1:
                    _rows_by_token(scr, src, r, tm // r)
                for s in range(N_SLABS):
                    val = scr[s] if r > 1 else src[0, :, s * HEAD_PAIR:(s + 1) * HEAD_PAIR]
                    if kind < 2:
                        val = _rope_transposed(val, cos, s1, s2)
                    if kind == 0:
                        val = val * 0.125
                    at = (3 * kind + g) * ATTN_W + s * HEAD_PAIR
                    o_ref[:, at:at + HEAD_PAIR] = val.astype(BF16)

    specs = []
    for r in DILATIONS:
        specs += [pl.BlockSpec((r, tm // r, ATTN_W), lambda i: (0, i, 0))] * 3
    row = pl.BlockSpec((tm, HEAD_PAIR), lambda i: (i, 0))
    flat = [a for grp in dqkv_groups for a in grp]
    return pl.pallas_call(
        body, name="dqkv_token_order", grid=(T // tm,), out_shape=jax.ShapeDtypeStruct((T, QKV_COLS), BF16),
        in_specs=specs + [row] * 3, out_specs=pl.BlockSpec((tm, QKV_COLS), lambda i: (i, 0)),
        scratch_shapes=[_slab_scratch(tm)], compiler_params=_params(("parallel",), 48),
    )(*flat, *tables)


def _row_tile(rows):
    for cand in (320, 256, 176, 128):
        if rows % cand == 0:
            return cand
    return rows


def _pair_sum(grad4, recv, chip, name):
    _, _, rows, cols = grad4.shape
    tr = _row_tile(rows)

    def body(ids_ref, g_ref, r_ref, gown_ref, rown_ref, sum_ref, own_ref):
        sum_ref[...] = (g_ref[...] + r_ref[...]).astype(BF16)

        @pl.when(pl.program_id(1) == 0)
        def _():
            own_ref[...] = gown_ref[...] + rown_ref[...]

    grid_spec = pltpu.PrefetchScalarGridSpec(
        num_scalar_prefetch=1, grid=(rows // tr, 4),
        in_specs=[pl.BlockSpec((None, None, tr, cols), lambda i, q, ids: (q, ids[1], i, 0)),
                  pl.BlockSpec((None, tr, cols), lambda i, q, ids: (q, i, 0)),
                  pl.BlockSpec((None, None, tr, cols), lambda i, q, ids: (ids[0], ids[1], i, 0)),
                  pl.BlockSpec((None, tr, cols), lambda i, q, ids: (ids[0], i, 0))],
        out_specs=[pl.BlockSpec((None, tr, cols), lambda i, q, ids: (q, i, 0)),
                   pl.BlockSpec((tr, cols), lambda i, q, ids: (i, 0))])
    return pl.pallas_call(
        body, name=name, grid_spec=grid_spec,
        out_shape=[jax.ShapeDtypeStruct((4, rows, cols), BF16), jax.ShapeDtypeStruct((rows, cols), F32)],
        compiler_params=_params(("arbitrary", "arbitrary"), 32),
    )(chip, grad4, recv, grad4, recv)


def _chip_sum(own, others, name):
    rows, cols = own.shape
    tr = _row_tile(rows)

    def body(own_ref, oth_ref, o_ref):
        total = own_ref[...]
        for j in range(3):
            total = total + oth_ref[j].astype(F32)
        o_ref[...] = total

    blk = pl.BlockSpec((tr, cols), lambda i: (i, 0))
    return pl.pallas_call(
        body, name=name, grid=(rows // tr,), out_shape=jax.ShapeDtypeStruct((rows, cols), F32),
        in_specs=[blk, pl.BlockSpec((3, tr, cols), lambda i: (0, i, 0))], out_specs=blk,
        compiler_params=_params(("parallel",), 32),
    )(own, others)


def _adam_math(w, g, m, v):
    m = ADAM_B1 * m + (1.0 - ADAM_B1) * g
    v = ADAM_B2 * v + (1.0 - ADAM_B2) * (g * g)
    m_hat = m / (1.0 - ADAM_B1 ** ADAM_STEP)
    v_hat = v / (1.0 - ADAM_B2 ** ADAM_STEP)
    delta = -ADAM_LR * (m_hat / (jnp.sqrt(v_hat) + ADAM_EPS) + ADAM_WD * w)
    return delta, m, v


def _adamw(w, g, m, v, name):
    rows, cols = w.shape
    tr = _row_tile(rows)

    def body(w_ref, g_ref, m_ref, v_ref, d_ref, nm_ref, nv_ref):
        d_ref[...], nm_ref[...], nv_ref[...] = _adam_math(w_ref[...], g_ref[...], m_ref[...], v_ref[...])

    blk = pl.BlockSpec((tr, cols), lambda i: (i, 0))
    res = jax.ShapeDtypeStruct((rows, cols), F32)
    return pl.pallas_call(
        body, name=name, grid=(rows // tr,), out_shape=[res, res, res], in_specs=[blk] * 4, out_specs=[blk] * 3,
        compiler_params=_params(("parallel",), 32),
    )(w, g, m, v)


def _small_update(parts, w, m, v, loss_rows):
    rows = w.shape[0]

    def body(p_ref, w_ref, m_ref, v_ref, g_ref, d_ref, nm_ref, nv_ref, loss_ref):
        g = p_ref[0:rows, :]
        for dev in range(1, N_DEV):
            g = g + p_ref[dev * rows:(dev + 1) * rows, :]
        g_ref[...] = g
        d_ref[...], nm_ref[...], nv_ref[...] = _adam_math(w_ref[...], g, m_ref[...], v_ref[...])
        loss_ref[...] = jnp.sum(jnp.sum(g[rows - loss_rows:, :], axis=1, keepdims=True), axis=0, keepdims=True)

    res = jax.ShapeDtypeStruct((rows, HEAD_PAIR), F32)
    return pl.pallas_call(
        body, name="small_update", out_shape=[res, res, res, res, jax.ShapeDtypeStruct((1, 1), F32)],
        compiler_params=pltpu.CompilerParams(vmem_limit_bytes=32 * MIB),
    )(parts, w, m, v)


def kernel(x, positions, norm1_g, w_in, sgu_ln_g, sgu_ln_b, w_spatial, b_spatial, w_proj_attn, w_proj_sgu, w_out, norm2_g, w_ffn_gate, w_ffn_up, w_ffn_down, final_g, loss_target, m_norm1_g, m_w_in, m_sgu_ln_g, m_sgu_ln_b, m_w_spatial, m_b_spatial, m_w_proj_attn, m_w_proj_sgu, m_w_out, m_norm2_g, m_w_ffn_gate, m_w_ffn_up, m_w_ffn_down, m_final_g, v_norm1_g, v_w_in, v_sgu_ln_g, v_sgu_ln_b, v_w_spatial, v_b_spatial, v_w_proj_attn, v_w_proj_sgu, v_w_out, v_norm2_g, v_w_ffn_gate, v_w_ffn_up, v_w_ffn_down, v_final_g):
    T = x.shape[1]
    tm = 512
    xt = x[0]
    target = loss_target[0]
    chip = jnp.stack([2 * lax.axis_index("x") + lax.axis_index("y"), lax.axis_index("c")]).astype(jnp.int32)

    def bf16_rows(w, transpose):
        return (jnp.transpose(w[0]) if transpose else w[0]).astype(BF16)

    (win_t,) = _all_gather([bf16_rows(w_in, True)]).run_alone("w_in_all_gather")
    ffn_in = _all_gather([bf16_rows(w_ffn_gate, True), bf16_rows(w_ffn_up, True)])
    rest = _all_gather([bf16_rows(w_ffn_down, False), bf16_rows(w_out, False),
                        bf16_rows(w_proj_attn, True), bf16_rows(w_proj_sgu, True)])

    inv_freq = ROPE_THETA ** (-jnp.arange(0, 2 * ROPE_HALF, 2, dtype=F32) / (2 * ROPE_HALF))
    inv_freq_row = jnp.tile(jnp.concatenate([inv_freq, inv_freq, jnp.zeros((48,), F32)]), 2).reshape(1, HEAD_PAIR)
    tables = _rope_tables(positions.reshape(T, 1), inv_freq_row, tm)
    b_t = jnp.transpose(b_spatial[0])

    h = _rmsnorm_fwd(xt, norm1_g, "norm1_fwd", tm)
    qkv, (wg_t, wu_t) = _qkv_proj(h, win_t, tables, tm, carried=ffn_in)
    (uvg,), (wd, wout, wpa_t, wps_t) = _uv_gate_proj(h, win_t, tm, carried=rest)
    fwd = [_attn_fwd(qkv[g], g) for g in range(3)]
    attn, lse = _attn_combine([f[0] for f in fwd], [f[1] for f in fwd], tm)
    sgu = _sgu_fwd(uvg, sgu_ln_g, sgu_ln_b, w_spatial[0], b_t, tm)
    pa, ps, merged = _merge_fwd(attn, sgu, wpa_t, wps_t, uvg, tm)
    x1, h2 = _out_residual(merged, wout, xt, norm2_g, tm)
    gate, up, ff = _ffn_fwd(h2, wg_t, wu_t, tm)
    dx2, dx2b, loss_cols, d_final_g = _down_loss(ff, wd, x1, final_g.reshape(1, D_MODEL), target, tm)

    dgate, dup, dx1, dx1b, d_norm2 = _ffn_bwd(dx2, dx2b, wd, wg_t, wu_t, gate, up, x1, norm2_g, tm // 2)
    tk = min(1024, T)
    d_wd = _mm_tn(ff, dx2b, tmm=FF_TILE, tk=tk, name="grad_w_ffn_down")
    d_wg_t = _mm_tn(dgate, h2, tmm=FF_TILE, tk=tk, name="grad_w_ffn_gate")
    d_wu_t = _mm_tn(dup, h2, tmm=FF_TILE, tk=tk, name="grad_w_ffn_up")

    def by_owner(grads):
        return [g.reshape(4, 2, g.shape[0] // N_DEV, g.shape[1]) for g in grads]

    def pair_sums(grads4, from_sibling, names):
        both = [_pair_sum(g4, rv, chip, "grad_pair_sum_" + nm) for g4, rv, nm in zip(grads4, from_sibling, names)]
        return [b[0] for b in both], [b[1] for b in both]

    ffn_names = ["w_ffn_gate", "w_ffn_up", "w_ffn_down"]
    ffn4 = by_owner([d_wg_t, d_wu_t, d_wd])
    (dpa, dps, dgates), ffn_sib = _merge_bwd(dx1b, wout, pa, ps, uvg, tm, carried=_pair_exchange(ffn4))
    ffn_sums, ffn_own = pair_sums(ffn4, ffn_sib, ffn_names)

    d_wout = _mm_tn(merged, dx1b, tmm=D_MODEL, tk=tk, name="grad_w_out")
    dattn, dsgu = _proj_bwd(dpa, dps, wpa_t, wps_t, tm)
    d_wpa_t = _mm_tn(dpa, attn.astype(BF16), tmm=D_MODEL, tk=tk, name="grad_w_proj_attn")
    d_wps_t = _mm_tn(dps, sgu, tmm=D_MODEL, tk=tk, name="grad_w_proj_sgu")
    mid_names = ["w_proj_attn", "w_proj_sgu", "w_out"]
    mid4 = by_owner([d_wpa_t, d_wps_t, d_wout])
    (duv, d_ws, d_bs_t, d_ln_g, d_ln_b), mid_sib = _sgu_bwd(uvg, dsgu, sgu_ln_g, sgu_ln_b, w_spatial[0], b_t, tm,
                                                           carried=_pair_exchange(mid4))
    mid_sums, mid_own = pair_sums(mid4, mid_sib, mid_names)

    prep = _attn_bwd_prepare(dattn, attn, lse, tm)
    dqkv0, ffn_far = _attn_bwd(qkv[0], prep[0], prep[1], prep[2], 0, carried=_chip_exchange(ffn_sums))
    dqkv1, mid_far = _attn_bwd(qkv[1], prep[3], prep[4], prep[5], 1, carried=_chip_exchange(mid_sums))
    dqkv2, _ = _attn_bwd(qkv[2], prep[6], prep[7], prep[8], 2)
    dqkv = _dqkv_token_order([dqkv0, dqkv1, dqkv2], tables, tm)
    d_win_t = _mm_tn(dqkv, h, tmm=1536, tk=tk, name="grad_w_in_qkv", rows_total=IN_COLS)
    d_win_t = _mm_tn(duv, h, tmm=512, tk=tk, name="grad_w_in_uv", into=d_win_t, row_block0=9, rows_total=IN_COLS)
    d_win_t = _mm_tn(dgates, h, tmm=512, tk=tk, name="grad_w_in_gates", into=d_win_t, row_block0=11, rows_total=IN_COLS)
    in4 = by_owner([d_win_t])
    in_sib = _pair_exchange(in4).run_alone("w_in_grad_pair_exchange")
    in_sums, in_own = pair_sums(in4, in_sib, ["w_in"])
    (dx, d_norm1), in_far = _in_bwd(dqkv, duv, dgates, win_t, xt, norm1_g, dx1, tm, carried=_chip_exchange(in_sums))

    names = ["w_in"] + mid_names + ffn_names
    reduced = [_chip_sum(o, f, "grad_total_" + nm)
               for o, f, nm in zip(in_own + mid_own + ffn_own, in_far + mid_far + ffn_far, names)]
    transposed = (True, True, True, False, True, True, False)
    g_big = [jnp.transpose(r) if t else r for r, t in zip(reduced, transposed)]

    small_w = [norm1_g, sgu_ln_g, sgu_ln_b, w_spatial, b_spatial, norm2_g, final_g]
    small_m = [m_norm1_g, m_sgu_ln_g, m_sgu_ln_b, m_w_spatial, m_b_spatial, m_norm2_g, m_final_g]
    small_v = [v_norm1_g, v_sgu_ln_g, v_sgu_ln_b, v_w_spatial, v_b_spatial, v_norm2_g, v_final_g]
    small_g = [d_norm1, d_ln_g, d_ln_b, d_ws, jnp.transpose(d_bs_t), d_norm2, d_final_g]
    zeros = jnp.zeros((D_MODEL,), F32)

    def flat(parts, last):
        return jnp.concatenate([p.reshape(-1) for p in parts] + [last]).reshape(-1, HEAD_PAIR)

    part = flat(small_g, loss_cols.reshape(-1))
    (gathered,) = _all_gather([part]).run_alone("small_grads_all_gather", vmem=True)
    g_s, d_s, nm_s, nv_s, loss = _small_update(gathered, flat(small_w, zeros), flat(small_m, zeros),
                                               flat(small_v, zeros), D_MODEL // HEAD_PAIR)

    def unflat(vec):
        vec = vec.reshape(-1)
        out, at = [], 0
        for wgt in small_w:
            out.append(vec[at:at + wgt.size].reshape(wgt.shape))
            at += wgt.size
        return out

    small = [unflat(a) for a in (g_s, d_s, nm_s, nv_s)]

    big_w = [w_in, w_proj_attn, w_proj_sgu, w_out, w_ffn_gate, w_ffn_up, w_ffn_down]
    big_m = [m_w_in, m_w_proj_attn, m_w_proj_sgu, m_w_out, m_w_ffn_gate, m_w_ffn_up, m_w_ffn_down]
    big_v = [v_w_in, v_w_proj_attn, v_w_proj_sgu, v_w_out, v_w_ffn_gate, v_w_ffn_up, v_w_ffn_down]
    big_out = []
    for wgt, g, mm, vv, nm in zip(big_w, g_big, big_m, big_v, names):
        d, nm_, nv_ = _adamw(wgt[0], g, mm[0], vv[0], "adamw_" + nm)
        big_out.append([a[None] for a in (g, d, nm_, nv_)])

    small_at = {0: 0, 2: 1, 3: 2, 4: 3, 5: 4, 9: 5, 13: 6}
    big_at = {1: 0, 6: 1, 7: 2, 8: 3, 10: 4, 11: 5, 12: 6}
    outs = [loss[0, 0], dx[None]]
    for kind in range(4):
        for idx in range(14):
            outs.append(small[kind][small_at[idx]] if idx in small_at else big_out[big_at[idx]][kind])
    return tuple(outs)
```

```python
import functools
import math

import jax
import jax.numpy as jnp
from jax import lax
from jax.experimental import pallas as pl
from jax.experimental.pallas import tpu as pltpu

F32 = jnp.float32
BF16 = jnp.bfloat16

D_MODEL = 1024
HEAD_PAIR = 128
ATTN_W = 512
DILATIONS = (1, 4, 16)
BLK = 128
ROPE_HALF = 8
ROPE_THETA = 500000.0
SGU_W = 512
QKV_COLS = 4608
IN_COLS = 7680
D_FF = 2816
FF_TILE = 1408
EPS = 1e-6
N_DEV = 8
MASKED = -1e30

ADAM_LR = 0.001
ADAM_B1 = 0.9
ADAM_B2 = 0.999
ADAM_EPS = 1e-08
ADAM_WD = 0.01
ADAM_STEP = 10

MIB = 1024 * 1024
MESH = pl.DeviceIdType.MESH
ANY = pl.BlockSpec(memory_space=pl.ANY)


def _array(shape, dtype):
    return pltpu.HBM(tuple(shape), dtype)


def _params(sem, vmem_mib):
    return pltpu.CompilerParams(dimension_semantics=sem, vmem_limit_bytes=vmem_mib * MIB)


def _nt(a, b):
    return lax.dot_general(a, b, (((1,), (1,)), ((), ())), preferred_element_type=F32)


def _nn(a, b):
    return lax.dot_general(a, b, (((1,), (0,)), ((), ())), preferred_element_type=F32)


def _tn(a, b):
    return lax.dot_general(a, b, (((0,), (0,)), ((), ())), preferred_element_type=F32)


class _Exchange:
    def __init__(self, arrays, out_shapes, sem_shapes, phases):
        self.arrays, self.out_shapes, self.sem_shapes, self.phases = list(arrays), out_shapes, sem_shapes, phases

    def run_alone(self, name, vmem=False):
        n_in, n_out = len(self.arrays), len(self.out_shapes)

        def body(*refs):
            start, middle, finish = self.phases(refs[:n_in], refs[n_in:n_in + n_out], refs[n_in + n_out:])
            start()
            middle()
            finish()

        spec = pl.BlockSpec(memory_space=pltpu.VMEM) if vmem else ANY
        shapes = [jax.ShapeDtypeStruct(s.shape, s.dtype) for s in self.out_shapes] if vmem else self.out_shapes
        return pl.pallas_call(
            body, name=name, out_shape=shapes, in_specs=[spec] * n_in, out_specs=[spec] * n_out,
            scratch_shapes=self.sem_shapes, compiler_params=pltpu.CompilerParams(vmem_limit_bytes=32 * MIB),
        )(*self.arrays)


def _call(body, *, name, grid, in_specs, out_specs, out_shape, args, vmem_mib, scratch_shapes=(), carried=None):
    n_in, n_out, n_scr = len(in_specs), len(out_specs), len(scratch_shapes)
    sem = ("arbitrary",) * len(grid)
    if carried is None:
        outs = pl.pallas_call(
            body, name=name, grid=grid, in_specs=in_specs, out_specs=out_specs, out_shape=out_shape,
            scratch_shapes=list(scratch_shapes), compiler_params=_params(sem, vmem_mib))(*args)
        return list(outs), []
    c_in, c_out = len(carried.arrays), len(carried.out_shapes)
    total = math.prod(grid)

    def full(*refs):
        own_in, car_in = refs[:n_in], refs[n_in:n_in + c_in]
        at = n_in + c_in
        own_out, car_out = refs[at:at + n_out], refs[at + n_out:at + n_out + c_out]
        at += n_out + c_out
        own_scr, sems = refs[at:at + n_scr], refs[at + n_scr:]
        step = pl.program_id(0)
        for axis in range(1, len(grid)):
            step = step * grid[axis] + pl.program_id(axis)
        start, middle, finish = carried.phases(car_in, car_out, sems)
        pl.when(step == 0)(start)
        pl.when(step == (3 * total) // 4)(middle)
        body(*own_in, *own_out, *own_scr)
        pl.when(step == total - 1)(finish)

    outs = pl.pallas_call(
        full, name=name, grid=grid, in_specs=list(in_specs) + [ANY] * c_in,
        out_specs=list(out_specs) + [ANY] * c_out, out_shape=list(out_shape) + list(carried.out_shapes),
        scratch_shapes=list(scratch_shapes) + list(carried.sem_shapes),
        compiler_params=_params(sem, vmem_mib))(*args, *carried.arrays)
    return list(outs[:n_out]), list(outs[n_out:])


def _all_gather(shards):
    n = len(shards)

    def phases(ins, outs, sems):
        send_sems, recv_sems, local_sems = sems
        x, y, c = lax.axis_index("x"), lax.axis_index("y"), lax.axis_index("c")
        me, sibling = (x, y, c), (x, y, 1 - c)
        chips = [(1 - x, y), (x, 1 - y), (1 - x, 1 - y)]

        def rows(m, px, py, pc):
            r = ins[m].shape[0]
            return outs[m].at[pl.ds((4 * px + 2 * py + pc) * r, r), :]

        def copy(m, k, block, to, src=None):
            return pltpu.make_async_remote_copy(
                src_ref=rows(m, *block) if src is None else src, dst_ref=rows(m, *block),
                send_sem=send_sems.at[m, k], recv_sem=recv_sems.at[m, k],
                device_id=to, device_id_type=MESH)

        def mine(m):
            return pltpu.make_async_copy(ins[m], rows(m, *me), local_sems.at[m])

        def first(m):
            return [copy(m, 0, me, sibling, src=ins[m])] + [
                copy(m, 1 + j, me, (*chip, c), src=ins[m]) for j, chip in enumerate(chips)]

        def passed(m):
            return [copy(m, 4 + j, (*chip, c), sibling) for j, chip in enumerate(chips)]

        def start():
            for m in range(n):
                mine(m).start()
            for m in range(n):
                for cp in first(m):
                    cp.start()

        def middle():
            for m in range(n):
                for j, chip in enumerate(chips):
                    copy(m, 1 + j, (*chip, c), me).wait_recv()
                    passed(m)[j].start()

        def finish():
            for m in range(n):
                copy(m, 0, sibling, me).wait_recv()
                for j, chip in enumerate(chips):
                    copy(m, 4 + j, (*chip, 1 - c), me).wait_recv()
            for m in range(n):
                for cp in first(m) + passed(m):
                    cp.wait_send()
                mine(m).wait()

        return start, middle, finish

    return _Exchange(
        shards, [_array((N_DEV * s.shape[0], s.shape[1]), s.dtype) for s in shards],
        [pltpu.SemaphoreType.DMA((n, 7)), pltpu.SemaphoreType.DMA((n, 7)), pltpu.SemaphoreType.DMA((n,))], phases)


def _pair_exchange(grads):
    n = len(grads)

    def phases(ins, outs, sems):
        send_sems, recv_sems = sems
        x, y, c = lax.axis_index("x"), lax.axis_index("y"), lax.axis_index("c")

        def copy(m):
            return pltpu.make_async_remote_copy(
                src_ref=ins[m].at[:, 1 - c], dst_ref=outs[m], send_sem=send_sems.at[m], recv_sem=recv_sems.at[m],
                device_id=(x, y, 1 - c), device_id_type=MESH)

        def start():
            for m in range(n):
                copy(m).start()

        def finish():
            for m in range(n):
                copy(m).wait()

        return start, lambda: None, finish

    return _Exchange(grads, [_array((4,) + g.shape[2:], g.dtype) for g in grads],
                     [pltpu.SemaphoreType.DMA((n,)), pltpu.SemaphoreType.DMA((n,))], phases)


def _chip_exchange(pair_sums):
    n = len(pair_sums)

    def phases(ins, outs, sems):
        send_sems, recv_sems = sems
        x, y, c = lax.axis_index("x"), lax.axis_index("y"), lax.axis_index("c")
        chips = [(1 - x, y), (x, 1 - y), (1 - x, 1 - y)]

        def copies():
            return [pltpu.make_async_remote_copy(
                src_ref=ins[m].at[2 * px + py], dst_ref=outs[m].at[j],
                send_sem=send_sems.at[m, j], recv_sem=recv_sems.at[m, j],
                device_id=(px, py, c), device_id_type=MESH)
                for m in range(n) for j, (px, py) in enumerate(chips)]

        def start():
            for cp in copies():
                cp.start()

        def finish():
            for cp in copies():
                cp.wait_recv()
            for cp in copies():
                cp.wait_send()

        return start, lambda: None, finish

    return _Exchange(pair_sums, [_array((3,) + p.shape[1:], p.dtype) for p in pair_sums],
                     [pltpu.SemaphoreType.DMA((n, 3)), pltpu.SemaphoreType.DMA((n, 3))], phases)


N_SLABS = ATTN_W // HEAD_PAIR


def _slab_scratch(tm):
    return pltpu.VMEM((N_SLABS, tm, HEAD_PAIR), F32)


def _rows_by_residue(dst_ref, slab_ref, r, tr, dtype):
    for rho in range(r):
        for s in range(N_SLABS):
            dst_ref[rho, :, s * HEAD_PAIR:(s + 1) * HEAD_PAIR] = (
                slab_ref[s, pl.ds(rho, tr, stride=r), :].astype(dtype))


def _rows_by_token(slab_ref, src_ref, r, tr):
    for rho in range(r):
        for s in range(N_SLABS):
            slab_ref[s, pl.ds(rho, tr, stride=r), :] = src_ref[rho, :, s * HEAD_PAIR:(s + 1) * HEAD_PAIR]


def _rope_tables(pos_col, inv_freq_row, tm):
    T = pos_col.shape[0]

    def body(pos_ref, invf_ref, cos_ref, s1_ref, s2_ref):
        ang = pos_ref[...].astype(F32) * invf_ref[...]
        lane = lax.broadcasted_iota(jnp.int32, (1, HEAD_PAIR), 1) % 64
        cs, sn = jnp.cos(ang), jnp.sin(ang)
        cos_ref[...] = jnp.where(lane < 2 * ROPE_HALF, cs, 1.0)
        s1_ref[...] = jnp.where(lane < ROPE_HALF, -sn, 0.0)
        s2_ref[...] = jnp.where((lane >= ROPE_HALF) & (lane < 2 * ROPE_HALF), sn, 0.0)

    tab = _array((T, HEAD_PAIR), F32)
    row = pl.BlockSpec((tm, HEAD_PAIR), lambda i: (i, 0))
    return pl.pallas_call(
        body, name="rope_tables", grid=(T // tm,), out_shape=[tab] * 3,
        in_specs=[pl.BlockSpec((tm, 1), lambda i: (i, 0)), pl.BlockSpec((1, HEAD_PAIR), lambda i: (0, 0))],
        out_specs=[row] * 3, compiler_params=_params(("parallel",), 16),
    )(pos_col, inv_freq_row)


def _rope(y, cos, s1, s2):
    w = y.shape[1]
    rep = w // HEAD_PAIR
    return (y * jnp.tile(cos, (1, rep)) + pltpu.roll(y, w - ROPE_HALF, 1) * jnp.tile(s1, (1, rep))
            + pltpu.roll(y, ROPE_HALF, 1) * jnp.tile(s2, (1, rep)))


def _rope_transposed(dy, cos, s1, s2):
    w = dy.shape[1]
    rep = w // HEAD_PAIR
    return (dy * jnp.tile(cos, (1, rep)) + pltpu.roll(dy * jnp.tile(s1, (1, rep)), ROPE_HALF, 1)
            + pltpu.roll(dy * jnp.tile(s2, (1, rep)), w - ROPE_HALF, 1))


def _rmsnorm_fwd(x, g, name, tm):
    T = x.shape[0]

    def body(x_ref, g_ref, h_ref):
        xf = x_ref[...]
        r = lax.rsqrt(jnp.mean(xf * xf, axis=-1, keepdims=True) + EPS)
        h_ref[...] = (xf * r * g_ref[...]).astype(BF16)

    return pl.pallas_call(
        body, name=name, grid=(T // tm,), out_shape=_array((T, D_MODEL), BF16),
        in_specs=[pl.BlockSpec((tm, D_MODEL), lambda i: (i, 0)), pl.BlockSpec((1, D_MODEL), lambda i: (0, 0))],
        out_specs=pl.BlockSpec((tm, D_MODEL), lambda i: (i, 0)), compiler_params=_params(("parallel",), 24),
    )(x, g)


def _resident(shape, block=None):
    at = (0,) * len(shape) if block is None else block
    return pl.BlockSpec(shape, lambda *_: at, pipeline_mode=pl.Buffered(1))


def _qkv_proj(h, win_t, tables, tm, carried=None):
    T = h.shape[0]

    def body(h_ref, w_ref, cos_ref, s1_ref, s2_ref, o0, o1, o2, *slabs):
        hv = h_ref[...]
        cos, s1, s2 = cos_ref[...], s1_ref[...], s2_ref[...]
        for kind in range(3):
            for g, (o_ref, r) in enumerate(zip((o0, o1, o2), DILATIONS)):
                blk = 3 * kind + g
                y = _nt(hv, w_ref[blk * ATTN_W:(blk + 1) * ATTN_W, :])
                if kind < 2:
                    y = _rope(y, cos, s1, s2)
                if kind == 0:
                    y = y * 0.125
                cols = slice(kind * ATTN_W, (kind + 1) * ATTN_W)
                if r == 1:
                    o_ref[0, :, cols] = y.astype(BF16)
                    continue
                slab = slabs[blk % len(slabs)]
                for s in range(N_SLABS):
                    slab[s] = y[:, s * HEAD_PAIR:(s + 1) * HEAD_PAIR]
                for rho in range(r):
                    for s in range(N_SLABS):
                        at = kind * ATTN_W + s * HEAD_PAIR
                        o_ref[rho, :, at:at + HEAD_PAIR] = slab[s, pl.ds(rho, tm // r, stride=r), :].astype(BF16)

    row = pl.BlockSpec((tm, HEAD_PAIR), lambda i: (i, 0))
    return _call(
        body, name="qkv_proj", grid=(T // tm,),
        out_shape=[_array((r, T // r, 3 * ATTN_W), BF16) for r in DILATIONS],
        in_specs=[pl.BlockSpec((tm, D_MODEL), lambda i: (i, 0)), _resident((QKV_COLS, D_MODEL)), row, row, row],
        out_specs=[pl.BlockSpec((r, tm // r, 3 * ATTN_W), lambda i: (0, i, 0)) for r in DILATIONS],
        scratch_shapes=[_slab_scratch(tm)] * 3, vmem_mib=48, args=(h, win_t, *tables), carried=carried)


def _uv_gate_proj(h, win_t, tm, carried=None):
    T = h.shape[0]
    half = (IN_COLS - QKV_COLS) // 2

    def body(h_ref, wa_ref, wb_ref, o_ref):
        hv = h_ref[...]
        o_ref[:, :half] = _nt(hv, wa_ref[...])
        o_ref[:, half:] = _nt(hv, wb_ref[...])

    blk0 = QKV_COLS // half
    return _call(
        body, name="uv_gate_proj", grid=(T // tm,), out_shape=[_array((T, 2 * half), F32)],
        in_specs=[pl.BlockSpec((tm, D_MODEL), lambda i: (i, 0)), _resident((half, D_MODEL), (blk0, 0)),
                  _resident((half, D_MODEL), (blk0 + 1, 0))],
        out_specs=[pl.BlockSpec((tm, 2 * half), lambda i: (i, 0))], vmem_mib=40, args=(h, win_t, win_t),
        carried=carried)


def _band_mask(n):
    row = lax.broadcasted_iota(jnp.int32, (BLK, 2 * BLK), 0)
    col = lax.broadcasted_iota(jnp.int32, (BLK, 2 * BLK), 1)
    has_prev = (jnp.zeros_like(row) + n) > 0
    return ((col < BLK) & (col >= row) & has_prev) | ((col >= BLK) & (col - BLK <= row))


def _head_lanes():
    lane = lax.broadcasted_iota(jnp.int32, (1, HEAD_PAIR), 1)
    return lane < 64, lane >= 64


def _attn_fwd(qkv, group, carried=None):
    r, L, _ = qkv.shape
    nb = L // BLK

    def body(q_ref, kp_ref, kc_ref, vp_ref, vc_ref, o_ref, lse_ref):
        valid = _band_mask(pl.program_id(1))
        head0, head1 = _head_lanes()
        for p in range(ATTN_W // HEAD_PAIR):
            sl = slice(p * HEAD_PAIR, (p + 1) * HEAD_PAIR)
            q = q_ref[:, sl]
            k2 = jnp.concatenate([kp_ref[:, sl], kc_ref[:, sl]], axis=0)
            v2 = jnp.concatenate([vp_ref[:, sl], vc_ref[:, sl]], axis=0)
            outs, lses = [], []
            for half in (head0, head1):
                s = jnp.where(valid, _nt(jnp.where(half, q, jnp.zeros_like(q)), k2), MASKED)
                m = jnp.max(s, axis=1, keepdims=True)
                e = jnp.exp(s - m)
                den = jnp.sum(e, axis=1, keepdims=True)
                outs.append(_nn(e.astype(BF16), v2) / den)
                lses.append(m + jnp.log(den))
            o_ref[:, sl] = jnp.where(head0, outs[0], outs[1])
            lse_ref[:, sl] = jnp.where(head0, lses[0], lses[1])

    def blk(kind, prev):
        if prev:
            return pl.BlockSpec((None, BLK, ATTN_W), lambda rho, n: (rho, jnp.maximum(n - 1, 0), kind))
        return pl.BlockSpec((None, BLK, ATTN_W), lambda rho, n: (rho, n, kind))

    out = pl.BlockSpec((None, BLK, ATTN_W), lambda rho, n: (rho, n, 0))
    res = _array((r, L, ATTN_W), F32)
    return _call(
        body, name=f"attn_fwd_g{group}", grid=(r, nb), out_shape=[res, res],
        in_specs=[blk(0, False), blk(1, True), blk(1, False), blk(2, True), blk(2, False)],
        out_specs=[out, out], vmem_mib=24, args=(qkv, qkv, qkv, qkv, qkv), carried=carried)


def _attn_combine(outs, lses, tm):
    T = outs[0].shape[1]

    def body(o0, l0, o1, l1, o2, l2, attn_ref, lse_ref, so1, sl1, so2, sl2):
        for o_in, l_in, so, sl, r in ((o1, l1, so1, sl1, DILATIONS[1]), (o2, l2, so2, sl2, DILATIONS[2])):
            _rows_by_token(so, o_in, r, tm // r)
            _rows_by_token(sl, l_in, r, tm // r)
        for s in range(N_SLABS):
            lanes = slice(s * HEAD_PAIR, (s + 1) * HEAD_PAIR)
            a0, a1, a2 = l0[0, :, lanes], sl1[s], sl2[s]
            mx = jnp.maximum(jnp.maximum(a0, a1), a2)
            e0, e1, e2 = jnp.exp(a0 - mx), jnp.exp(a1 - mx), jnp.exp(a2 - mx)
            tot = e0 + e1 + e2
            attn_ref[:, lanes] = (e0 * o0[0, :, lanes] + e1 * so1[s] + e2 * so2[s]) / tot
            lse_ref[:, lanes] = mx + jnp.log(tot)

    def grouped(r):
        return pl.BlockSpec((r, tm // r, ATTN_W), lambda i: (0, i, 0))

    tok = pl.BlockSpec((tm, ATTN_W), lambda i: (i, 0))
    res = _array((T, ATTN_W), F32)
    ins, specs = [], []
    for g, r in enumerate(DILATIONS):
        ins += [outs[g], lses[g]]
        specs += [grouped(r), grouped(r)]
    return pl.pallas_call(
        body, name="attn_combine", grid=(T // tm,), out_shape=[res, res], in_specs=specs, out_specs=[tok, tok],
        scratch_shapes=[_slab_scratch(tm)] * 4, compiler_params=_params(("parallel",), 32),
    )(*ins)


def _gelu(x):
    return 0.5 * x * (1.0 + lax.erf(x * (1.0 / math.sqrt(2.0))))


def _gelu_grad(x):
    return 0.5 * (1.0 + lax.erf(x * (1.0 / math.sqrt(2.0)))) + x * jnp.exp(-0.5 * x * x) * (1.0 / math.sqrt(2.0 * math.pi))


def _causal():
    row = lax.broadcasted_iota(jnp.int32, (BLK, BLK), 0)
    col = lax.broadcasted_iota(jnp.int32, (BLK, BLK), 1)
    return col <= row


def _bias_lanes(bt):
    grp = lax.broadcasted_iota(jnp.int32, (1, SGU_W), 1) // 64
    out = jnp.zeros((BLK, SGU_W), F32)
    for g in range(8):
        out = jnp.where(grp == g, bt[:, g:g + 1], out)
    return out


def _sgu_normalise(uv, ln_g, ln_b):
    z = _gelu(uv)
    u, v = z[:, :SGU_W], z[:, SGU_W:]
    mu = jnp.mean(v, axis=-1, keepdims=True)
    xc = v - mu
    rstd = lax.rsqrt(jnp.mean(xc * xc, axis=-1, keepdims=True) + EPS)
    xh = xc * rstd
    return u, xh, rstd, xh * ln_g + ln_b


def _sgu_mix(wc_ref, vb, head0):
    chunks = []
    for ch in range(vb.shape[0] // BLK):
        pairs = []
        for p in range(SGU_W // HEAD_PAIR):
            v_pair = vb[ch * BLK:(ch + 1) * BLK, p * HEAD_PAIR:(p + 1) * HEAD_PAIR]
            pairs.append(jnp.where(head0, _nn(wc_ref[2 * p], v_pair), _nn(wc_ref[2 * p + 1], v_pair)))
        chunks.append(jnp.concatenate(pairs, axis=1))
    return jnp.concatenate(chunks, axis=0)


def _sgu_fwd(uvg, ln_g, ln_b, w_s, b_t, tm):
    T = uvg.shape[0]

    def body(uv_ref, g_ref, b_ref, w_ref, bt_ref, o_ref, wc_ref, bias_ref):
        @pl.when(pl.program_id(0) == 0)
        def _():
            causal = _causal()
            for g in range(8):
                wc_ref[g] = jnp.where(causal, w_ref[g], 0.0).astype(BF16)
            bias_ref[...] = _bias_lanes(bt_ref[...])

        u, _, _, vn = _sgu_normalise(uv_ref[...], g_ref[...], b_ref[...])
        mixed = _sgu_mix(wc_ref, vn.astype(BF16), _head_lanes()[0])
        o_ref[...] = (u * (mixed + jnp.tile(bias_ref[...], (tm // BLK, 1)))).astype(BF16)

    vec = pl.BlockSpec((1, SGU_W), lambda i: (0, 0))
    return pl.pallas_call(
        body, name="sgu_fwd", grid=(T // tm,), out_shape=_array((T, SGU_W), BF16),
        in_specs=[pl.BlockSpec((tm, 2 * SGU_W), lambda i: (i, 0)), vec, vec,
                  pl.BlockSpec((8, BLK, BLK), lambda i: (0, 0, 0)), pl.BlockSpec((BLK, 8), lambda i: (0, 0))],
        out_specs=pl.BlockSpec((tm, SGU_W), lambda i: (i, 0)),
        scratch_shapes=[pltpu.VMEM((8, BLK, BLK), BF16), pltpu.VMEM((BLK, SGU_W), F32)],
        compiler_params=_params(("arbitrary",), 32),
    )(uvg, ln_g, ln_b, w_s, b_t)


def _merge_fwd(attn, sgu, wpa_t, wps_t, uvg, tm):
    T = attn.shape[0]

    def body(attn_ref, sgu_ref, wpa_ref, wps_ref, ga_ref, gb_ref, pa_ref, ps_ref, m_ref):
        pa = _nt(attn_ref[...].astype(BF16), wpa_ref[...])
        ps = _nt(sgu_ref[...], wps_ref[...])
        pa_ref[...] = pa
        ps_ref[...] = ps
        m_ref[...] = (jax.nn.sigmoid(ga_ref[...]) * pa + jax.nn.sigmoid(gb_ref[...]) * ps).astype(BF16)

    half = pl.BlockSpec((tm, ATTN_W), lambda i: (i, 0))
    wide = pl.BlockSpec((tm, D_MODEL), lambda i: (i, 0))
    w = pl.BlockSpec((D_MODEL, ATTN_W), lambda i: (0, 0))
    res = _array((T, D_MODEL), F32)
    return pl.pallas_call(
        body, name="merge_fwd", grid=(T // tm,),
        out_shape=[res, res, _array((T, D_MODEL), BF16)],
        in_specs=[half, half, w, w, pl.BlockSpec((tm, D_MODEL), lambda i: (i, 1)),
                  pl.BlockSpec((tm, D_MODEL), lambda i: (i, 2))],
        out_specs=[wide, wide, wide], compiler_params=_params(("parallel",), 40),
    )(attn, sgu, wpa_t, wps_t, uvg, uvg)


def _out_residual(merged, w_out, x, g2, tm):
    T = x.shape[0]

    def body(m_ref, w_ref, x_ref, g_ref, o_ref, h_ref):
        x1 = x_ref[...] + _nn(m_ref[...], w_ref[...])
        o_ref[...] = x1
        r = lax.rsqrt(jnp.mean(x1 * x1, axis=-1, keepdims=True) + EPS)
        h_ref[...] = (x1 * r * g_ref[...]).astype(BF16)

    wide = pl.BlockSpec((tm, D_MODEL), lambda i: (i, 0))
    return pl.pallas_call(
        body, name="out_residual", grid=(T // tm,),
        out_shape=[_array((T, D_MODEL), F32), _array((T, D_MODEL), BF16)],
        in_specs=[wide, _resident((D_MODEL, D_MODEL)), wide, pl.BlockSpec((1, D_MODEL), lambda i: (0, 0))],
        out_specs=[wide, wide], compiler_params=_params(("parallel",), 32),
    )(merged, w_out, x, g2)


def _ffn_fwd(h2, wg_t, wu_t, tm):
    T = h2.shape[0]

    def body(h_ref, wg_ref, wu_ref, gate_ref, up_ref, ff_ref):
        h = h_ref[...]
        for j in range(D_FF // FF_TILE):
            cols = slice(j * FF_TILE, (j + 1) * FF_TILE)
            gate, up = _nt(h, wg_ref[cols, :]), _nt(h, wu_ref[cols, :])
            gate_ref[:, cols] = gate.astype(BF16)
            up_ref[:, cols] = up.astype(BF16)
            ff_ref[:, cols] = (gate * jax.nn.sigmoid(gate) * up).astype(BF16)

    w = _resident((D_FF, D_MODEL))
    o = pl.BlockSpec((tm, D_FF), lambda i: (i, 0))
    res = _array((T, D_FF), BF16)
    return pl.pallas_call(
        body, name="ffn_fwd", grid=(T // tm,), out_shape=[res, res, res],
        in_specs=[pl.BlockSpec((tm, D_MODEL), lambda i: (i, 0)), w, w], out_specs=[o, o, o],
        compiler_params=_params(("parallel",), 52),
    )(h2, wg_t, wu_t)


def _down_loss(ff, w_down, x1, final_g, target, tm):
    T = x1.shape[0]

    def body(ff_ref, w_ref, x1_ref, g_ref, t_ref, dx_ref, dxb_ref, loss_ref, dg_ref):
        @pl.when(pl.program_id(0) == 0)
        def _():
            loss_ref[...] = jnp.zeros_like(loss_ref)
            dg_ref[...] = jnp.zeros_like(dg_ref)

        x2 = x1_ref[...] + _nn(ff_ref[...], w_ref[...])
        g = g_ref[...]
        r = lax.rsqrt(jnp.mean(x2 * x2, axis=-1, keepdims=True) + EPS)
        xh = x2 * r
        err = xh * g - t_ref[...]
        loss_ref[...] += jnp.sum(err * err, axis=0, keepdims=True) * (0.5 / D_MODEL)
        dy = err * (1.0 / D_MODEL)
        dg_ref[...] += jnp.sum(dy * xh, axis=0, keepdims=True)
        dxh = dy * g
        dx = r * (dxh - xh * jnp.mean(dxh * xh, axis=-1, keepdims=True))
        dx_ref[...] = dx
        dxb_ref[...] = dx.astype(BF16)

    wide = pl.BlockSpec((tm, D_MODEL), lambda i: (i, 0))
    vec = pl.BlockSpec((1, D_MODEL), lambda i: (0, 0))
    vec_shape = _array((1, D_MODEL), F32)
    return pl.pallas_call(
        body, name="down_loss", grid=(T // tm,),
        out_shape=[_array((T, D_MODEL), F32), _array((T, D_MODEL), BF16),
                   vec_shape, vec_shape],
        in_specs=[pl.BlockSpec((tm, D_FF), lambda i: (i, 0)), _resident((D_FF, D_MODEL)), wide, vec, wide],
        out_specs=[wide, wide, vec, vec],
        compiler_params=_params(("arbitrary",), 40),
    )(ff, w_down, x1, final_g, target)


def _rmsnorm_bwd(dh, xin, g, d_res, dg_ref):
    r = lax.rsqrt(jnp.mean(xin * xin, axis=-1, keepdims=True) + EPS)
    xh = xin * r
    dg_ref[...] += jnp.sum(dh * xh, axis=0, keepdims=True)
    dxh = dh * g
    return d_res + r * (dxh - xh * jnp.mean(dxh * xh, axis=-1, keepdims=True))


def _ffn_bwd(dx2, dx2b, w_down, wg_t, wu_t, gate, up, x1, g2, tm):
    T = dx2.shape[0]

    def body(dx_ref, dxb_ref, wd_ref, wg_ref, wu_ref, gate_ref, up_ref, x1_ref, g_ref,
             dgate_ref, dup_ref, dx1_ref, dx1b_ref, dg_ref):
        @pl.when(pl.program_id(0) == 0)
        def _():
            dg_ref[...] = jnp.zeros_like(dg_ref)

        dxb = dxb_ref[...]
        dh = jnp.zeros((tm, D_MODEL), F32)
        for j in range(D_FF // FF_TILE):
            cols = slice(j * FF_TILE, (j + 1) * FF_TILE)
            dff = _nt(dxb, wd_ref[cols, :])
            gate, up = gate_ref[:, cols].astype(F32), up_ref[:, cols].astype(F32)
            sg = jax.nn.sigmoid(gate)
            dgate = (dff * up * sg * (1.0 + gate * (1.0 - sg))).astype(BF16)
            dup = (dff * gate * sg).astype(BF16)
            dgate_ref[:, cols] = dgate
            dup_ref[:, cols] = dup
            dh += _nn(dgate, wg_ref[cols, :]) + _nn(dup, wu_ref[cols, :])
        dx1 = _rmsnorm_bwd(dh, x1_ref[...], g_ref[...], dx_ref[...], dg_ref)
        dx1_ref[...] = dx1
        dx1b_ref[...] = dx1.astype(BF16)

    wide = pl.BlockSpec((tm, D_MODEL), lambda i: (i, 0))
    ffw = pl.BlockSpec((tm, D_FF), lambda i: (i, 0))
    vec = pl.BlockSpec((1, D_MODEL), lambda i: (0, 0))
    w = _resident((D_FF, D_MODEL))
    ff_shape = _array((T, D_FF), BF16)
    return pl.pallas_call(
        body, name="ffn_bwd", grid=(T // tm,),
        out_shape=[ff_shape, ff_shape, _array((T, D_MODEL), F32),
                   _array((T, D_MODEL), BF16), _array((1, D_MODEL), F32)],
        in_specs=[wide, wide, w, w, w, ffw, ffw, wide, vec],
        out_specs=[ffw, ffw, wide, wide, vec], compiler_params=_params(("arbitrary",), 56),
    )(dx2, dx2b, w_down, wg_t, wu_t, gate, up, x1, g2)


def _mm_tn(a, b, *, tmm, tk, name, into=None, row_block0=0, rows_total=None):
    T, M = a.shape
    N = b.shape[1]
    rows_total = M if rows_total is None else rows_total

    def body(*refs):
        a_ref, b_ref, o_ref = refs[0], refs[1], refs[-1]

        @pl.when(pl.program_id(1) == 0)
        def _():
            o_ref[...] = jnp.zeros_like(o_ref)

        o_ref[...] += _tn(a_ref[...], b_ref[...])

    ins = [a, b] + ([] if into is None else [into])
    specs = [pl.BlockSpec((tk, tmm), lambda i, k: (k, i)), pl.BlockSpec((tk, N), lambda i, k: (k, 0))]
    return pl.pallas_call(
        body, name=name, grid=(M // tmm, T // tk),
        out_shape=_array((rows_total, N), F32),
        in_specs=specs + ([] if into is None else [ANY]),
        out_specs=pl.BlockSpec((tmm, N), lambda i, k: (row_block0 + i, 0)),
        input_output_aliases={} if into is None else {2: 0},
        compiler_params=_params(("parallel", "arbitrary"), 48),
    )(*ins)


def _in_bwd(dqkv, duv, dgates, win_t, x, g1, dx1, tm, carried=None):
    T = x.shape[0]
    uv0, gates0 = QKV_COLS, QKV_COLS + 2 * SGU_W

    def body(dq_ref, du_ref, dgt_ref, w_ref, x_ref, g_ref, d_ref, dx_ref, dg_ref):
        @pl.when(pl.program_id(0) == 0)
        def _():
            dg_ref[...] = jnp.zeros_like(dg_ref)

        dh = (_nn(dq_ref[...], w_ref[:uv0, :]) + _nn(du_ref[...], w_ref[uv0:gates0, :])
              + _nn(dgt_ref[...], w_ref[gates0:, :]))
        dx_ref[...] = _rmsnorm_bwd(dh, x_ref[...], g_ref[...], d_ref[...], dg_ref)

    def cols(n):
        return pl.BlockSpec((tm, n), lambda i: (i, 0))

    wide = cols(D_MODEL)
    vec = pl.BlockSpec((1, D_MODEL), lambda i: (0, 0))
    return _call(
        body, name="in_bwd", grid=(T // tm,),
        out_shape=[_array((T, D_MODEL), F32), _array((1, D_MODEL), F32)],
        in_specs=[cols(QKV_COLS), cols(2 * SGU_W), cols(2 * D_MODEL), _resident((IN_COLS, D_MODEL)), wide, vec, wide],
        out_specs=[wide, vec], vmem_mib=56, args=(dqkv, duv, dgates, win_t, x, g1, dx1), carried=carried)


def _merge_bwd(dx1b, w_out, pa, ps, uvg, tm, carried=None):
    T = dx1b.shape[0]

    def body(dx_ref, w_ref, pa_ref, ps_ref, ga_ref, gb_ref, dpa_ref, dps_ref, dg_ref):
        dm = _nt(dx_ref[...], w_ref[...])
        ga, gb = jax.nn.sigmoid(ga_ref[...]), jax.nn.sigmoid(gb_ref[...])
        dpa_ref[...] = (dm * ga).astype(BF16)
        dps_ref[...] = (dm * gb).astype(BF16)
        dg_ref[:, :D_MODEL] = (dm * pa_ref[...] * ga * (1.0 - ga)).astype(BF16)
        dg_ref[:, D_MODEL:] = (dm * ps_ref[...] * gb * (1.0 - gb)).astype(BF16)

    wide = pl.BlockSpec((tm, D_MODEL), lambda i: (i, 0))
    res = _array((T, D_MODEL), BF16)
    return _call(
        body, name="merge_bwd", grid=(T // tm,),
        out_shape=[res, res, _array((T, 2 * D_MODEL), BF16)],
        in_specs=[wide, _resident((D_MODEL, D_MODEL)), wide, wide,
                  pl.BlockSpec((tm, D_MODEL), lambda i: (i, 1)), pl.BlockSpec((tm, D_MODEL), lambda i: (i, 2))],
        out_specs=[wide, wide, pl.BlockSpec((tm, 2 * D_MODEL), lambda i: (i, 0))],
        vmem_mib=48, args=(dx1b, w_out, pa, ps, uvg, uvg), carried=carried)


def _proj_bwd(dpa, dps, wpa_t, wps_t, tm):
    T = dpa.shape[0]

    def body(dpa_ref, dps_ref, wpa_ref, wps_ref, dattn_ref, dsgu_ref):
        dattn_ref[...] = _nn(dpa_ref[...], wpa_ref[...])
        dsgu_ref[...] = _nn(dps_ref[...], wps_ref[...])

    wide = pl.BlockSpec((tm, D_MODEL), lambda i: (i, 0))
    half = pl.BlockSpec((tm, ATTN_W), lambda i: (i, 0))
    w = pl.BlockSpec((D_MODEL, ATTN_W), lambda i: (0, 0))
    res = _array((T, ATTN_W), F32)
    return pl.pallas_call(
        body, name="proj_bwd", grid=(T // tm,), out_shape=[res, res], in_specs=[wide, wide, w, w],
        out_specs=[half, half], compiler_params=_params(("parallel",), 32),
    )(dpa, dps, wpa_t, wps_t)


def _sgu_bwd(uvg, dsgu, ln_g, ln_b, w_s, b_t, tm, carried=None):
    T = uvg.shape[0]
    nsteps = T // tm

    def body(uv_ref, ds_ref, g_ref, b_ref, w_ref, bt_ref, duv_ref, dw_ref, dbt_ref, dg_ref, db_ref,
             wc_ref, wct_ref, bias_ref, dbias_ref):
        step = pl.program_id(0)
        head0, head1 = _head_lanes()

        @pl.when(step == 0)
        def _():
            causal = _causal()
            for g in range(8):
                wc = jnp.where(causal, w_ref[g], 0.0)
                wc_ref[g] = wc.astype(BF16)
                wct_ref[g] = wc.T.astype(BF16)
            bias_ref[...] = _bias_lanes(bt_ref[...])
            dbias_ref[...] = jnp.zeros_like(dbias_ref)
            dw_ref[...] = jnp.zeros_like(dw_ref)
            dg_ref[...] = jnp.zeros_like(dg_ref)
            db_ref[...] = jnp.zeros_like(db_ref)

        uv = uv_ref[...]
        ln_gain = g_ref[...]
        u, xh, rstd, vn = _sgu_normalise(uv, ln_gain, b_ref[...])
        vb = vn.astype(BF16)
        mixed = _sgu_mix(wc_ref, vb, head0) + jnp.tile(bias_ref[...], (tm // BLK, 1))
        dout = ds_ref[...]
        du = dout * mixed
        dmixed = dout * u
        dmb = dmixed.astype(BF16)
        dvn_chunks = []
        for ch in range(tm // BLK):
            rows = slice(ch * BLK, (ch + 1) * BLK)
            dbias_ref[...] += dmixed[rows]
            pairs = []
            for p in range(SGU_W // HEAD_PAIR):
                lanes = slice(p * HEAD_PAIR, (p + 1) * HEAD_PAIR)
                dm_pair, v_pair = dmb[rows, lanes], vb[rows, lanes]
                acc = jnp.zeros((BLK, HEAD_PAIR), F32)
                for hh, half in enumerate((head0, head1)):
                    dm_h = jnp.where(half, dm_pair, jnp.zeros_like(dm_pair))
                    dw_ref[2 * p + hh] += _nt(dm_h, v_pair)
                    acc += _nn(wct_ref[2 * p + hh], dm_h)
                pairs.append(acc)
            dvn_chunks.append(jnp.concatenate(pairs, axis=1))
        dvn = jnp.concatenate(dvn_chunks, axis=0)
        dg_ref[...] += jnp.sum(dvn * xh, axis=0, keepdims=True)
        db_ref[...] += jnp.sum(dvn, axis=0, keepdims=True)
        dxh = dvn * ln_gain
        dv = rstd * (dxh - jnp.mean(dxh, axis=-1, keepdims=True) - xh * jnp.mean(dxh * xh, axis=-1, keepdims=True))
        dgelu = _gelu_grad(uv)
        duv_ref[:, :SGU_W] = (du * dgelu[:, :SGU_W]).astype(BF16)
        duv_ref[:, SGU_W:] = (dv * dgelu[:, SGU_W:]).astype(BF16)

        @pl.when(step == nsteps - 1)
        def _():
            causal = _causal()
            for g in range(8):
                dw_ref[g] = jnp.where(causal, dw_ref[g], 0.0)
            grp = lax.broadcasted_iota(jnp.int32, (1, SGU_W), 1) // 64
            col = lax.broadcasted_iota(jnp.int32, (1, 8), 1)
            dbias = dbias_ref[...]
            out = jnp.zeros((BLK, 8), F32)
            for g in range(8):
                s = jnp.sum(jnp.where(grp == g, dbias, 0.0), axis=1, keepdims=True)
                out = jnp.where(col == g, s, out)
            dbt_ref[...] = out

    vec = pl.BlockSpec((1, SGU_W), lambda i: (0, 0))
    w3 = pl.BlockSpec((8, BLK, BLK), lambda i: (0, 0, 0))
    bt = pl.BlockSpec((BLK, 8), lambda i: (0, 0))
    return _call(
        body, name="sgu_bwd", grid=(nsteps,),
        out_shape=[_array((T, 2 * SGU_W), BF16), _array((8, BLK, BLK), F32),
                   _array((BLK, 8), F32), _array((1, SGU_W), F32),
                   _array((1, SGU_W), F32)],
        in_specs=[pl.BlockSpec((tm, 2 * SGU_W), lambda i: (i, 0)), pl.BlockSpec((tm, SGU_W), lambda i: (i, 0)),
                  vec, vec, w3, bt],
        out_specs=[pl.BlockSpec((tm, 2 * SGU_W), lambda i: (i, 0)), w3, bt, vec, vec],
        scratch_shapes=[pltpu.VMEM((8, BLK, BLK), BF16), pltpu.VMEM((8, BLK, BLK), BF16),
                        pltpu.VMEM((BLK, SGU_W), F32), pltpu.VMEM((BLK, SGU_W), F32)],
        vmem_mib=40, args=(uvg, dsgu, ln_g, ln_b, w_s, b_t), carried=carried)


def _attn_bwd_prepare(dattn, attn, lse, tm):
    T = dattn.shape[0]

    def body(da_ref, at_ref, lse_ref, *outs):
        scr_da, scr_lse, scr_d = outs[-3:]
        outs = outs[:-3]
        head0, _ = _head_lanes()
        for s in range(N_SLABS):
            lanes = slice(s * HEAD_PAIR, (s + 1) * HEAD_PAIR)
            da = da_ref[:, lanes]
            pp = da * at_ref[:, lanes]
            d0 = jnp.sum(jnp.where(head0, pp, 0.0), axis=1, keepdims=True)
            d1 = jnp.sum(jnp.where(head0, 0.0, pp), axis=1, keepdims=True)
            d_pair = jnp.where(head0, d0, d1)
            scr_da[s] = da
            scr_lse[s] = lse_ref[:, lanes]
            scr_d[s] = d_pair
            outs[0][0, :, lanes] = da.astype(BF16)
            outs[1][0, :, lanes] = lse_ref[:, lanes]
            outs[2][0, :, lanes] = d_pair
        for g, r in enumerate(DILATIONS):
            if r > 1:
                for dst, scr, dt in zip(outs[3 * g:3 * g + 3], (scr_da, scr_lse, scr_d), (BF16, F32, F32)):
                    _rows_by_residue(dst, scr, r, tm // r, dt)

    tok = pl.BlockSpec((tm, ATTN_W), lambda i: (i, 0))
    shapes, specs = [], []
    for r in DILATIONS:
        spec = pl.BlockSpec((r, tm // r, ATTN_W), lambda i: (0, i, 0))
        shapes += [_array((r, T // r, ATTN_W), dt) for dt in (BF16, F32, F32)]
        specs += [spec] * 3
    return pl.pallas_call(
        body, name="attn_bwd_prepare", grid=(T // tm,), out_shape=shapes, in_specs=[tok, tok, tok],
        out_specs=specs, scratch_shapes=[_slab_scratch(tm)] * 3,
        compiler_params=_params(("parallel",), 40),
    )(dattn, attn, lse)


def _attn_bwd(qkv, dattn, lse, dsum, group, carried=None):
    r, L, _ = qkv.shape
    nb = L // BLK

    def body(q_ref, kp_ref, kc_ref, vp_ref, vc_ref, da_ref, lse_ref, d_ref, dq_ref, dk_ref, dv_ref,
             carry_k, carry_v):
        n = pl.program_id(1)

        @pl.when(n == 0)
        def _():
            carry_k[...] = jnp.zeros_like(carry_k)
            carry_v[...] = jnp.zeros_like(carry_v)

        @pl.when(n < nb)
        def _():
            valid = _band_mask(n)
            head0, head1 = _head_lanes()
            for p in range(ATTN_W // HEAD_PAIR):
                sl = slice(p * HEAD_PAIR, (p + 1) * HEAD_PAIR)
                q, da = q_ref[:, sl], da_ref[:, sl]
                k2 = jnp.concatenate([kp_ref[:, sl], kc_ref[:, sl]], axis=0)
                v2 = jnp.concatenate([vp_ref[:, sl], vc_ref[:, sl]], axis=0)
                lse_pair, d_pair = lse_ref[:, sl], d_ref[:, sl]
                dq = jnp.zeros((BLK, HEAD_PAIR), F32)
                dk2 = jnp.zeros((2 * BLK, HEAD_PAIR), F32)
                dv2 = jnp.zeros((2 * BLK, HEAD_PAIR), F32)
                for hh, half in enumerate((head0, head1)):
                    col = 64 * hh
                    q_h = jnp.where(half, q, jnp.zeros_like(q))
                    da_h = jnp.where(half, da, jnp.zeros_like(da))
                    s = _nt(q_h, k2)
                    prob = jnp.where(valid, jnp.exp(s - lse_pair[:, col:col + 1]), 0.0)
                    dprob = _nt(da_h, v2)
                    ds = (prob * (dprob - d_pair[:, col:col + 1])).astype(BF16)
                    dq += jnp.where(half, _nn(ds, k2), 0.0)
                    dk2 += _tn(ds, q_h)
                    dv2 += _tn(prob.astype(BF16), da_h)
                dq_ref[:, sl] = dq
                dk_ref[:, sl] = carry_k[:, sl] + dk2[:BLK]
                dv_ref[:, sl] = carry_v[:, sl] + dv2[:BLK]
                carry_k[:, sl] = dk2[BLK:]
                carry_v[:, sl] = dv2[BLK:]

        @pl.when(n == nb)
        def _():
            dk_ref[...] = carry_k[...]
            dv_ref[...] = carry_v[...]

    def cur(kind):
        return pl.BlockSpec((None, BLK, ATTN_W), lambda rho, n: (rho, jnp.minimum(n, nb - 1), kind))

    def prev(kind):
        return pl.BlockSpec((None, BLK, ATTN_W), lambda rho, n: (rho, jnp.clip(n - 1, 0, nb - 1), kind))

    res = _array((r, L, ATTN_W), F32)
    return _call(
        body, name=f"attn_bwd_g{group}", grid=(r, nb + 1), out_shape=[res, res, res],
        in_specs=[cur(0), prev(1), cur(1), prev(2), cur(2), cur(0), cur(0), cur(0)],
        out_specs=[cur(0), prev(0), prev(0)],
        scratch_shapes=[pltpu.VMEM((BLK, ATTN_W), F32), pltpu.VMEM((BLK, ATTN_W), F32)],
        vmem_mib=32, args=(qkv, qkv, qkv, qkv, qkv, dattn, lse, dsum), carried=carried)


def _dqkv_token_order(dqkv_groups, tables, tm):
    T = tables[0].shape[0]

    def body(*refs):
        ins = refs[:9]
        cos_ref, s1_ref, s2_ref, o_ref, scr = refs[9:]
        cos, s1, s2 = cos_ref[...], s1_ref[...], s2_ref[...]
        for g, r in enumerate(DILATIONS):
            for kind in range(3):
                src = ins[3 * g + kind]
                if r > 1:
                    _rows_by_token(scr, src, r, tm // r)
                for s in range(N_SLABS):
                    val = scr[s] if r > 1 else src[0, :, s * HEAD_PAIR:(s + 1) * HEAD_PAIR]
                    if kind < 2:
                        val = _rope_transposed(val, cos, s1, s2)
                    if kind == 0:
                        val = val * 0.125
                    at = (3 * kind + g) * ATTN_W + s * HEAD_PAIR
                    o_ref[:, at:at + HEAD_PAIR] = val.astype(BF16)

    specs = []
    for r in DILATIONS:
        specs += [pl.BlockSpec((r, tm // r, ATTN_W), lambda i: (0, i, 0))] * 3
    row = pl.BlockSpec((tm, HEAD_PAIR), lambda i: (i, 0))
    flat = [a for grp in dqkv_groups for a in grp]
    return pl.pallas_call(
        body, name="dqkv_token_order", grid=(T // tm,), out_shape=_array((T, QKV_COLS), BF16),
        in_specs=specs + [row] * 3, out_specs=pl.BlockSpec((tm, QKV_COLS), lambda i: (i, 0)),
        scratch_shapes=[_slab_scratch(tm)], compiler_params=_params(("parallel",), 48),
    )(*flat, *tables)


def _row_tile(rows):
    for cand in (320, 256, 176, 128):
        if rows % cand == 0:
            return cand
    return rows


def _pair_sum(grad4, recv, chip, name):
    _, _, rows, cols = grad4.shape
    tr = _row_tile(rows)

    def body(ids_ref, g_ref, r_ref, gown_ref, rown_ref, sum_ref, own_ref):
        sum_ref[...] = (g_ref[...] + r_ref[...]).astype(BF16)

        @pl.when(pl.program_id(1) == 0)
        def _():
            own_ref[...] = gown_ref[...] + rown_ref[...]

    grid_spec = pltpu.PrefetchScalarGridSpec(
        num_scalar_prefetch=1, grid=(rows // tr, 4),
        in_specs=[pl.BlockSpec((None, None, tr, cols), lambda i, q, ids: (q, ids[1], i, 0)),
                  pl.BlockSpec((None, tr, cols), lambda i, q, ids: (q, i, 0)),
                  pl.BlockSpec((None, None, tr, cols), lambda i, q, ids: (ids[0], ids[1], i, 0)),
                  pl.BlockSpec((None, tr, cols), lambda i, q, ids: (ids[0], i, 0))],
        out_specs=[pl.BlockSpec((None, tr, cols), lambda i, q, ids: (q, i, 0)),
                   pl.BlockSpec((tr, cols), lambda i, q, ids: (i, 0))])
    return pl.pallas_call(
        body, name=name, grid_spec=grid_spec,
        out_shape=[_array((4, rows, cols), BF16), _array((rows, cols), F32)],
        compiler_params=_params(("arbitrary", "arbitrary"), 32),
    )(chip, grad4, recv, grad4, recv)


def _chip_sum(own, others, name):
    rows, cols = own.shape
    tr = _row_tile(rows)

    def body(own_ref, oth_ref, o_ref):
        total = own_ref[...]
        for j in range(3):
            total = total + oth_ref[j].astype(F32)
        o_ref[...] = total

    blk = pl.BlockSpec((tr, cols), lambda i: (i, 0))
    return pl.pallas_call(
        body, name=name, grid=(rows // tr,), out_shape=_array((rows, cols), F32),
        in_specs=[blk, pl.BlockSpec((3, tr, cols), lambda i: (0, i, 0))], out_specs=blk,
        compiler_params=_params(("parallel",), 32),
    )(own, others)


def _adam_math(w, g, m, v):
    m = ADAM_B1 * m + (1.0 - ADAM_B1) * g
    v = ADAM_B2 * v + (1.0 - ADAM_B2) * (g * g)
    m_hat = m / (1.0 - ADAM_B1 ** ADAM_STEP)
    v_hat = v / (1.0 - ADAM_B2 ** ADAM_STEP)
    delta = -ADAM_LR * (m_hat / (jnp.sqrt(v_hat) + ADAM_EPS) + ADAM_WD * w)
    return delta, m, v


def _adamw(w, g, m, v, name):
    rows, cols = w.shape
    tr = _row_tile(rows)

    def body(w_ref, g_ref, m_ref, v_ref, d_ref, nm_ref, nv_ref):
        d_ref[...], nm_ref[...], nv_ref[...] = _adam_math(w_ref[...], g_ref[...], m_ref[...], v_ref[...])

    blk = pl.BlockSpec((tr, cols), lambda i: (i, 0))
    res = _array((rows, cols), F32)
    return pl.pallas_call(
        body, name=name, grid=(rows // tr,), out_shape=[res, res, res], in_specs=[blk] * 4, out_specs=[blk] * 3,
        compiler_params=_params(("parallel",), 32),
    )(w, g, m, v)


def _small_update(parts, w, m, v, loss_rows):
    rows = w.shape[0]

    def body(p_ref, w_ref, m_ref, v_ref, g_ref, d_ref, nm_ref, nv_ref, loss_ref):
        g = p_ref[0:rows, :]
        for dev in range(1, N_DEV):
            g = g + p_ref[dev * rows:(dev + 1) * rows, :]
        g_ref[...] = g
        d_ref[...], nm_ref[...], nv_ref[...] = _adam_math(w_ref[...], g, m_ref[...], v_ref[...])
        loss_ref[...] = jnp.sum(jnp.sum(g[rows - loss_rows:, :], axis=1, keepdims=True), axis=0, keepdims=True)

    res = jax.ShapeDtypeStruct((rows, HEAD_PAIR), F32)
    return pl.pallas_call(
        body, name="small_update", out_shape=[res, res, res, res, jax.ShapeDtypeStruct((1, 1), F32)],
        compiler_params=pltpu.CompilerParams(vmem_limit_bytes=32 * MIB),
    )(parts, w, m, v)


def kernel(x, positions, norm1_g, w_in, sgu_ln_g, sgu_ln_b, w_spatial, b_spatial, w_proj_attn, w_proj_sgu, w_out, norm2_g, w_ffn_gate, w_ffn_up, w_ffn_down, final_g, loss_target, m_norm1_g, m_w_in, m_sgu_ln_g, m_sgu_ln_b, m_w_spatial, m_b_spatial, m_w_proj_attn, m_w_proj_sgu, m_w_out, m_norm2_g, m_w_ffn_gate, m_w_ffn_up, m_w_ffn_down, m_final_g, v_norm1_g, v_w_in, v_sgu_ln_g, v_sgu_ln_b, v_w_spatial, v_b_spatial, v_w_proj_attn, v_w_proj_sgu, v_w_out, v_norm2_g, v_w_ffn_gate, v_w_ffn_up, v_w_ffn_down, v_final_g):
    T = x.shape[1]
    tm = 512
    xt = x[0]
    target = loss_target[0]
    chip = jnp.stack([2 * lax.axis_index("x") + lax.axis_index("y"), lax.axis_index("c")]).astype(jnp.int32)

    def bf16_rows(w, transpose):
        return (jnp.transpose(w[0]) if transpose else w[0]).astype(BF16)

    (win_t,) = _all_gather([bf16_rows(w_in, True)]).run_alone("w_in_all_gather")
    ffn_in = _all_gather([bf16_rows(w_ffn_gate, True), bf16_rows(w_ffn_up, True)])
    mixers = _all_gather([bf16_rows(w_out, False), bf16_rows(w_proj_attn, True), bf16_rows(w_proj_sgu, True)])
    ffn_out = _all_gather([bf16_rows(w_ffn_down, False)])

    inv_freq = ROPE_THETA ** (-jnp.arange(0, 2 * ROPE_HALF, 2, dtype=F32) / (2 * ROPE_HALF))
    inv_freq_row = jnp.tile(jnp.concatenate([inv_freq, inv_freq, jnp.zeros((48,), F32)]), 2).reshape(1, HEAD_PAIR)
    tables = _rope_tables(positions.reshape(T, 1), inv_freq_row, tm)
    b_t = jnp.transpose(b_spatial[0])

    h = _rmsnorm_fwd(xt, norm1_g, "norm1_fwd", tm)
    qkv, (wg_t, wu_t) = _qkv_proj(h, win_t, tables, tm, carried=ffn_in)
    (uvg,), (wout, wpa_t, wps_t) = _uv_gate_proj(h, win_t, tm, carried=mixers)
    fwd0, (wd,) = _attn_fwd(qkv[0], 0, carried=ffn_out)
    fwd = [fwd0, _attn_fwd(qkv[1], 1)[0], _attn_fwd(qkv[2], 2)[0]]
    attn, lse = _attn_combine([f[0] for f in fwd], [f[1] for f in fwd], tm)
    sgu = _sgu_fwd(uvg, sgu_ln_g, sgu_ln_b, w_spatial[0], b_t, tm)
    pa, ps, merged = _merge_fwd(attn, sgu, wpa_t, wps_t, uvg, tm)
    x1, h2 = _out_residual(merged, wout, xt, norm2_g, tm)
    gate, up, ff = _ffn_fwd(h2, wg_t, wu_t, tm)
    dx2, dx2b, loss_cols, d_final_g = _down_loss(ff, wd, x1, final_g.reshape(1, D_MODEL), target, tm)

    dgate, dup, dx1, dx1b, d_norm2 = _ffn_bwd(dx2, dx2b, wd, wg_t, wu_t, gate, up, x1, norm2_g, tm // 2)
    tk = min(1024, T)
    d_wd = _mm_tn(ff, dx2b, tmm=FF_TILE, tk=tk, name="grad_w_ffn_down")
    d_wg_t = _mm_tn(dgate, h2, tmm=FF_TILE, tk=tk, name="grad_w_ffn_gate")
    d_wu_t = _mm_tn(dup, h2, tmm=FF_TILE, tk=tk, name="grad_w_ffn_up")

    def by_owner(grads):
        return [g.reshape(4, 2, g.shape[0] // N_DEV, g.shape[1]) for g in grads]

    def pair_sums(grads4, from_sibling, names):
        both = [_pair_sum(g4, rv, chip, "grad_pair_sum_" + nm) for g4, rv, nm in zip(grads4, from_sibling, names)]
        return [b[0] for b in both], [b[1] for b in both]

    ffn_names = ["w_ffn_gate", "w_ffn_up", "w_ffn_down"]
    ffn4 = by_owner([d_wg_t, d_wu_t, d_wd])
    (dpa, dps, dgates), ffn_sib = _merge_bwd(dx1b, wout, pa, ps, uvg, tm, carried=_pair_exchange(ffn4))
    ffn_sums, ffn_own = pair_sums(ffn4, ffn_sib, ffn_names)

    d_wout = _mm_tn(merged, dx1b, tmm=D_MODEL, tk=tk, name="grad_w_out")
    dattn, dsgu = _proj_bwd(dpa, dps, wpa_t, wps_t, tm)
    d_wpa_t = _mm_tn(dpa, attn.astype(BF16), tmm=D_MODEL, tk=tk, name="grad_w_proj_attn")
    d_wps_t = _mm_tn(dps, sgu, tmm=D_MODEL, tk=tk, name="grad_w_proj_sgu")
    mid_names = ["w_proj_attn", "w_proj_sgu", "w_out"]
    mid4 = by_owner([d_wpa_t, d_wps_t, d_wout])
    (duv, d_ws, d_bs_t, d_ln_g, d_ln_b), mid_sib = _sgu_bwd(uvg, dsgu, sgu_ln_g, sgu_ln_b, w_spatial[0], b_t, tm,
                                                           carried=_pair_exchange(mid4))
    mid_sums, mid_own = pair_sums(mid4, mid_sib, mid_names)

    prep = _attn_bwd_prepare(dattn, attn, lse, tm)
    dqkv0, ffn_far = _attn_bwd(qkv[0], prep[0], prep[1], prep[2], 0, carried=_chip_exchange(ffn_sums))
    dqkv1, mid_far = _attn_bwd(qkv[1], prep[3], prep[4], prep[5], 1, carried=_chip_exchange(mid_sums))
    dqkv2, _ = _attn_bwd(qkv[2], prep[6], prep[7], prep[8], 2)
    dqkv = _dqkv_token_order([dqkv0, dqkv1, dqkv2], tables, tm)
    d_win_t = _mm_tn(dqkv, h, tmm=1536, tk=tk, name="grad_w_in_qkv", rows_total=IN_COLS)
    d_win_t = _mm_tn(duv, h, tmm=512, tk=tk, name="grad_w_in_uv", into=d_win_t, row_block0=9, rows_total=IN_COLS)
    d_win_t = _mm_tn(dgates, h, tmm=512, tk=tk, name="grad_w_in_gates", into=d_win_t, row_block0=11, rows_total=IN_COLS)
    in4 = by_owner([d_win_t])
    in_sib = _pair_exchange(in4).run_alone("w_in_grad_pair_exchange")
    in_sums, in_own = pair_sums(in4, in_sib, ["w_in"])
    (dx, d_norm1), in_far = _in_bwd(dqkv, duv, dgates, win_t, xt, norm1_g, dx1, tm, carried=_chip_exchange(in_sums))

    names = ["w_in"] + mid_names + ffn_names
    reduced = [_chip_sum(o, f, "grad_total_" + nm)
               for o, f, nm in zip(in_own + mid_own + ffn_own, in_far + mid_far + ffn_far, names)]
    transposed = (True, True, True, False, True, True, False)
    g_big = [jnp.transpose(r) if t else r for r, t in zip(reduced, transposed)]

    small_w = [norm1_g, sgu_ln_g, sgu_ln_b, w_spatial, b_spatial, norm2_g, final_g]
    small_m = [m_norm1_g, m_sgu_ln_g, m_sgu_ln_b, m_w_spatial, m_b_spatial, m_norm2_g, m_final_g]
    small_v = [v_norm1_g, v_sgu_ln_g, v_sgu_ln_b, v_w_spatial, v_b_spatial, v_norm2_g, v_final_g]
    small_g = [d_norm1, d_ln_g, d_ln_b, d_ws, jnp.transpose(d_bs_t), d_norm2, d_final_g]
    zeros = jnp.zeros((D_MODEL,), F32)

    def flat(parts, last):
        return jnp.concatenate([p.reshape(-1) for p in parts] + [last]).reshape(-1, HEAD_PAIR)

    part = flat(small_g, loss_cols.reshape(-1))
    (gathered,) = _all_gather([part]).run_alone("small_grads_all_gather", vmem=True)
    g_s, d_s, nm_s, nv_s, loss = _small_update(gathered, flat(small_w, zeros), flat(small_m, zeros),
                                               flat(small_v, zeros), D_MODEL // HEAD_PAIR)

    def unflat(vec):
        vec = vec.reshape(-1)
        out, at = [], 0
        for wgt in small_w:
            out.append(vec[at:at + wgt.size].reshape(wgt.shape))
            at += wgt.size
        return out

    small = [unflat(a) for a in (g_s, d_s, nm_s, nv_s)]

    big_w = [w_in, w_proj_attn, w_proj_sgu, w_out, w_ffn_gate, w_ffn_up, w_ffn_down]
    big_m = [m_w_in, m_w_proj_attn, m_w_proj_sgu, m_w_out, m_w_ffn_gate, m_w_ffn_up, m_w_ffn_down]
    big_v = [v_w_in, v_w_proj_attn, v_w_proj_sgu, v_w_out, v_w_ffn_gate, v_w_ffn_up, v_w_ffn_down]
    big_out = []
    for wgt, g, mm, vv, nm in zip(big_w, g_big, big_m, big_v, names):
        d, nm_, nv_ = _adamw(wgt[0], g, mm[0], vv[0], "adamw_" + nm)
        big_out.append([a[None] for a in (g, d, nm_, nv_)])

    small_at = {0: 0, 2: 1, 3: 2, 4: 3, 5: 4, 9: 5, 13: 6}
    big_at = {1: 0, 6: 1, 7: 2, 8: 3, 10: 4, 11: 5, 12: 6}
    outs = [loss[0, 0], dx[None]]
    for kind in range(4):
        for idx in range(14):
            outs.append(small[kind][small_at[idx]] if idx in small_at else big_out[big_at[idx]][kind])
    return tuple(outs)
```

```python
import functools
import math

import jax
import jax.numpy as jnp
from jax import lax
from jax.experimental import pallas as pl
from jax.experimental.pallas import tpu as pltpu

F32 = jnp.float32
BF16 = jnp.bfloat16

D_MODEL = 1024
HEAD_PAIR = 128
ATTN_W = 512
DILATIONS = (1, 4, 16)
BLK = 128
ROPE_HALF = 8
ROPE_THETA = 500000.0
SGU_W = 512
QKV_COLS = 4608
IN_COLS = 7680
D_FF = 2816
FF_TILE = 1408
EPS = 1e-6
N_DEV = 8
MASKED = -1e30

ADAM_LR = 0.001
ADAM_B1 = 0.9
ADAM_B2 = 0.999
ADAM_EPS = 1e-08
ADAM_WD = 0.01
ADAM_STEP = 10

MIB = 1024 * 1024
MESH = pl.DeviceIdType.MESH
ANY = pl.BlockSpec(memory_space=pl.ANY)


def _array(shape, dtype):
    return pltpu.HBM(tuple(shape), dtype)


PIN_BYTES = 4 * MIB


def _pin(x):
    if x.size * x.dtype.itemsize < PIN_BYTES:
        return x
    return pltpu.with_memory_space_constraint(x, pltpu.HBM)


def _pallas(body, **kwargs):
    call = pl.pallas_call(body, **kwargs)
    return lambda *args: call(*[_pin(a) for a in args])


def _params(sem, vmem_mib):
    return pltpu.CompilerParams(dimension_semantics=sem, vmem_limit_bytes=vmem_mib * MIB)


def _nt(a, b):
    return lax.dot_general(a, b, (((1,), (1,)), ((), ())), preferred_element_type=F32)


def _nn(a, b):
    return lax.dot_general(a, b, (((1,), (0,)), ((), ())), preferred_element_type=F32)


def _tn(a, b):
    return lax.dot_general(a, b, (((0,), (0,)), ((), ())), preferred_element_type=F32)


class _Exchange:
    def __init__(self, arrays, out_shapes, sem_shapes, phases):
        self.arrays, self.out_shapes, self.sem_shapes, self.phases = list(arrays), out_shapes, sem_shapes, phases

    def run_alone(self, name, vmem=False):
        n_in, n_out = len(self.arrays), len(self.out_shapes)

        def body(*refs):
            start, middle, finish = self.phases(refs[:n_in], refs[n_in:n_in + n_out], refs[n_in + n_out:])
            start()
            middle()
            finish()

        spec = pl.BlockSpec(memory_space=pltpu.VMEM) if vmem else ANY
        shapes = [jax.ShapeDtypeStruct(s.shape, s.dtype) for s in self.out_shapes] if vmem else self.out_shapes
        return _pallas(
            body, name=name, out_shape=shapes, in_specs=[spec] * n_in, out_specs=[spec] * n_out,
            scratch_shapes=self.sem_shapes, compiler_params=pltpu.CompilerParams(vmem_limit_bytes=32 * MIB),
        )(*self.arrays)


def _call(body, *, name, grid, in_specs, out_specs, out_shape, args, vmem_mib, scratch_shapes=(), carried=None):
    n_in, n_out, n_scr = len(in_specs), len(out_specs), len(scratch_shapes)
    sem = ("arbitrary",) * len(grid)
    if carried is None:
        outs = _pallas(
            body, name=name, grid=grid, in_specs=in_specs, out_specs=out_specs, out_shape=out_shape,
            scratch_shapes=list(scratch_shapes), compiler_params=_params(sem, vmem_mib))(*args)
        return list(outs), []
    c_in, c_out = len(carried.arrays), len(carried.out_shapes)
    total = math.prod(grid)

    def full(*refs):
        own_in, car_in = refs[:n_in], refs[n_in:n_in + c_in]
        at = n_in + c_in
        own_out, car_out = refs[at:at + n_out], refs[at + n_out:at + n_out + c_out]
        at += n_out + c_out
        own_scr, sems = refs[at:at + n_scr], refs[at + n_scr:]
        step = pl.program_id(0)
        for axis in range(1, len(grid)):
            step = step * grid[axis] + pl.program_id(axis)
        start, middle, finish = carried.phases(car_in, car_out, sems)
        pl.when(step == 0)(start)
        pl.when(step == (3 * total) // 4)(middle)
        body(*own_in, *own_out, *own_scr)
        pl.when(step == total - 1)(finish)

    outs = _pallas(
        full, name=name, grid=grid, in_specs=list(in_specs) + [ANY] * c_in,
        out_specs=list(out_specs) + [ANY] * c_out, out_shape=list(out_shape) + list(carried.out_shapes),
        scratch_shapes=list(scratch_shapes) + list(carried.sem_shapes),
        compiler_params=_params(sem, vmem_mib))(*args, *carried.arrays)
    return list(outs[:n_out]), list(outs[n_out:])


def _all_gather(shards):
    n = len(shards)

    def phases(ins, outs, sems):
        send_sems, recv_sems, local_sems = sems
        x, y, c = lax.axis_index("x"), lax.axis_index("y"), lax.axis_index("c")
        me, sibling = (x, y, c), (x, y, 1 - c)
        chips = [(1 - x, y), (x, 1 - y), (1 - x, 1 - y)]

        def rows(m, px, py, pc):
            r = ins[m].shape[0]
            return outs[m].at[pl.ds((4 * px + 2 * py + pc) * r, r), :]

        def copy(m, k, block, to, src=None):
            return pltpu.make_async_remote_copy(
                src_ref=rows(m, *block) if src is None else src, dst_ref=rows(m, *block),
                send_sem=send_sems.at[m, k], recv_sem=recv_sems.at[m, k],
                device_id=to, device_id_type=MESH)

        def mine(m):
            return pltpu.make_async_copy(ins[m], rows(m, *me), local_sems.at[m])

        def first(m):
            return [copy(m, 0, me, sibling, src=ins[m])] + [
                copy(m, 1 + j, me, (*chip, c), src=ins[m]) for j, chip in enumerate(chips)]

        def passed(m):
            return [copy(m, 4 + j, (*chip, c), sibling) for j, chip in enumerate(chips)]

        def start():
            for m in range(n):
                mine(m).start()
            for m in range(n):
                for cp in first(m):
                    cp.start()

        def middle():
            for m in range(n):
                for j, chip in enumerate(chips):
                    copy(m, 1 + j, (*chip, c), me).wait_recv()
                    passed(m)[j].start()

        def finish():
            for m in range(n):
                copy(m, 0, sibling, me).wait_recv()
                for j, chip in enumerate(chips):
                    copy(m, 4 + j, (*chip, 1 - c), me).wait_recv()
            for m in range(n):
                for cp in first(m) + passed(m):
                    cp.wait_send()
                mine(m).wait()

        return start, middle, finish

    return _Exchange(
        shards, [_array((N_DEV * s.shape[0], s.shape[1]), s.dtype) for s in shards],
        [pltpu.SemaphoreType.DMA((n, 7)), pltpu.SemaphoreType.DMA((n, 7)), pltpu.SemaphoreType.DMA((n,))], phases)


def _pair_exchange(grads):
    n = len(grads)

    def phases(ins, outs, sems):
        send_sems, recv_sems = sems
        x, y, c = lax.axis_index("x"), lax.axis_index("y"), lax.axis_index("c")

        def copy(m):
            return pltpu.make_async_remote_copy(
                src_ref=ins[m].at[:, 1 - c], dst_ref=outs[m], send_sem=send_sems.at[m], recv_sem=recv_sems.at[m],
                device_id=(x, y, 1 - c), device_id_type=MESH)

        def start():
            for m in range(n):
                copy(m).start()

        def finish():
            for m in range(n):
                copy(m).wait()

        return start, lambda: None, finish

    return _Exchange(grads, [_array((4,) + g.shape[2:], g.dtype) for g in grads],
                     [pltpu.SemaphoreType.DMA((n,)), pltpu.SemaphoreType.DMA((n,))], phases)


def _chip_exchange(pair_sums):
    n = len(pair_sums)

    def phases(ins, outs, sems):
        send_sems, recv_sems = sems
        x, y, c = lax.axis_index("x"), lax.axis_index("y"), lax.axis_index("c")
        chips = [(1 - x, y), (x, 1 - y), (1 - x, 1 - y)]

        def copies():
            return [pltpu.make_async_remote_copy(
                src_ref=ins[m].at[2 * px + py], dst_ref=outs[m].at[j],
                send_sem=send_sems.at[m, j], recv_sem=recv_sems.at[m, j],
                device_id=(px, py, c), device_id_type=MESH)
                for m in range(n) for j, (px, py) in enumerate(chips)]

        def start():
            for cp in copies():
                cp.start()

        def finish():
            for cp in copies():
                cp.wait_recv()
            for cp in copies():
                cp.wait_send()

        return start, lambda: None, finish

    return _Exchange(pair_sums, [_array((3,) + p.shape[1:], p.dtype) for p in pair_sums],
                     [pltpu.SemaphoreType.DMA((n, 3)), pltpu.SemaphoreType.DMA((n, 3))], phases)


N_SLABS = ATTN_W // HEAD_PAIR


def _slab_scratch(tm, n=N_SLABS):
    return pltpu.VMEM((n, tm, HEAD_PAIR), F32)


def _rows_by_residue(dst_ref, slab_ref, r, tr, dtype, n=N_SLABS):
    for rho in range(r):
        for s in range(n):
            dst_ref[rho, :, s * HEAD_PAIR:(s + 1) * HEAD_PAIR] = (
                slab_ref[s, pl.ds(rho, tr, stride=r), :].astype(dtype))


def _rows_by_token(slab_ref, src_ref, r, tr, n=N_SLABS):
    for rho in range(r):
        for s in range(n):
            slab_ref[s, pl.ds(rho, tr, stride=r), :] = (
                src_ref[rho, :, s * HEAD_PAIR:(s + 1) * HEAD_PAIR].astype(F32))


def _rope_tables(pos_col, inv_freq_row, tm):
    T = pos_col.shape[0]

    def body(pos_ref, invf_ref, cos_ref, s1_ref, s2_ref):
        ang = pos_ref[...].astype(F32) * invf_ref[...]
        lane = lax.broadcasted_iota(jnp.int32, (1, HEAD_PAIR), 1) % 64
        cs, sn = jnp.cos(ang), jnp.sin(ang)
        cos_ref[...] = jnp.where(lane < 2 * ROPE_HALF, cs, 1.0)
        s1_ref[...] = jnp.where(lane < ROPE_HALF, -sn, 0.0)
        s2_ref[...] = jnp.where((lane >= ROPE_HALF) & (lane < 2 * ROPE_HALF), sn, 0.0)

    tab = _array((T, HEAD_PAIR), F32)
    row = pl.BlockSpec((tm, HEAD_PAIR), lambda i: (i, 0))
    return _pallas(
        body, name="rope_tables", grid=(T // tm,), out_shape=[tab] * 3,
        in_specs=[pl.BlockSpec((tm, 1), lambda i: (i, 0)), pl.BlockSpec((1, HEAD_PAIR), lambda i: (0, 0))],
        out_specs=[row] * 3, compiler_params=_params(("parallel",), 16),
    )(pos_col, inv_freq_row)


def _rope(y, cos, s1, s2):
    w = y.shape[1]
    rep = w // HEAD_PAIR
    return (y * jnp.tile(cos, (1, rep)) + pltpu.roll(y, w - ROPE_HALF, 1) * jnp.tile(s1, (1, rep))
            + pltpu.roll(y, ROPE_HALF, 1) * jnp.tile(s2, (1, rep)))


def _rope_transposed(dy, cos, s1, s2):
    w = dy.shape[1]
    rep = w // HEAD_PAIR
    return (dy * jnp.tile(cos, (1, rep)) + pltpu.roll(dy * jnp.tile(s1, (1, rep)), ROPE_HALF, 1)
            + pltpu.roll(dy * jnp.tile(s2, (1, rep)), w - ROPE_HALF, 1))


def _rmsnorm_fwd(x, g, name, tm):
    T = x.shape[0]

    def body(x_ref, g_ref, h_ref):
        xf = x_ref[...]
        r = lax.rsqrt(jnp.mean(xf * xf, axis=-1, keepdims=True) + EPS)
        h_ref[...] = (xf * r * g_ref[...]).astype(BF16)

    return _pallas(
        body, name=name, grid=(T // tm,), out_shape=_array((T, D_MODEL), BF16),
        in_specs=[pl.BlockSpec((tm, D_MODEL), lambda i: (i, 0)), pl.BlockSpec((1, D_MODEL), lambda i: (0, 0))],
        out_specs=pl.BlockSpec((tm, D_MODEL), lambda i: (i, 0)), compiler_params=_params(("parallel",), 24),
    )(x, g)


def _resident(shape, block=None):
    at = (0,) * len(shape) if block is None else block
    return pl.BlockSpec(shape, lambda *_: at, pipeline_mode=pl.Buffered(1))


def _qkv_proj(h, win_t, tables, tm, carried=None):
    T = h.shape[0]

    def body(h_ref, w_ref, cos_ref, s1_ref, s2_ref, o0, o1, o2, *slabs):
        hv = h_ref[...]
        cos, s1, s2 = cos_ref[...], s1_ref[...], s2_ref[...]
        for kind in range(3):
            for g, (o_ref, r) in enumerate(zip((o0, o1, o2), DILATIONS)):
                blk = 3 * kind + g
                y = _nt(hv, w_ref[blk * ATTN_W:(blk + 1) * ATTN_W, :])
                if kind < 2:
                    y = _rope(y, cos, s1, s2)
                if kind == 0:
                    y = y * 0.125
                cols = slice(kind * ATTN_W, (kind + 1) * ATTN_W)
                if r == 1:
                    o_ref[0, :, cols] = y.astype(BF16)
                    continue
                slab = slabs[blk % len(slabs)]
                for s in range(N_SLABS):
                    slab[s] = y[:, s * HEAD_PAIR:(s + 1) * HEAD_PAIR]
                for rho in range(r):
                    for s in range(N_SLABS):
                        at = kind * ATTN_W + s * HEAD_PAIR
                        o_ref[rho, :, at:at + HEAD_PAIR] = slab[s, pl.ds(rho, tm // r, stride=r), :].astype(BF16)

    row = pl.BlockSpec((tm, HEAD_PAIR), lambda i: (i, 0))
    return _call(
        body, name="qkv_proj", grid=(T // tm,),
        out_shape=[_array((r, T // r, 3 * ATTN_W), BF16) for r in DILATIONS],
        in_specs=[pl.BlockSpec((tm, D_MODEL), lambda i: (i, 0)), _resident((QKV_COLS, D_MODEL)), row, row, row],
        out_specs=[pl.BlockSpec((r, tm // r, 3 * ATTN_W), lambda i: (0, i, 0)) for r in DILATIONS],
        scratch_shapes=[_slab_scratch(tm)] * 3, vmem_mib=48, args=(h, win_t, *tables), carried=carried)


def _uv_gate_proj(h, win_t, tm, carried=None):
    T = h.shape[0]
    half = (IN_COLS - QKV_COLS) // 2

    def body(h_ref, wa_ref, wb_ref, o_ref):
        hv = h_ref[...]
        o_ref[:, :half] = _nt(hv, wa_ref[...])
        o_ref[:, half:] = _nt(hv, wb_ref[...])

    blk0 = QKV_COLS // half
    return _call(
        body, name="uv_gate_proj", grid=(T // tm,), out_shape=[_array((T, 2 * half), F32)],
        in_specs=[pl.BlockSpec((tm, D_MODEL), lambda i: (i, 0)), _resident((half, D_MODEL), (blk0, 0)),
                  _resident((half, D_MODEL), (blk0 + 1, 0))],
        out_specs=[pl.BlockSpec((tm, 2 * half), lambda i: (i, 0))], vmem_mib=40, args=(h, win_t, win_t),
        carried=carried)


def _band_mask(n):
    row = lax.broadcasted_iota(jnp.int32, (2 * BLK, 2 * BLK), 0) & (BLK - 1)
    col = lax.broadcasted_iota(jnp.int32, (2 * BLK, 2 * BLK), 1)
    has_prev = (jnp.zeros_like(row) + n) > 0
    return ((col < BLK) & (col >= row) & has_prev) | ((col >= BLK) & (col - BLK <= row))


def _head_lanes():
    lane = lax.broadcasted_iota(jnp.int32, (1, HEAD_PAIR), 1)
    return lane < 64, lane >= 64


def _stack_heads(x, head0, head1):
    zero = jnp.zeros_like(x)
    return jnp.concatenate([jnp.where(head0, x, zero), jnp.where(head1, x, zero)], axis=0)


def _unstack_heads(y, head0):
    return jnp.where(head0, y[:BLK], y[BLK:])


def _per_head(stats, col):
    return jnp.concatenate([stats[:, col:col + 1], stats[:, col + 1:col + 2]], axis=0)


def _attn_fwd(qkv, group, carried=None):
    r, L, _ = qkv.shape
    nb = L // BLK

    def body(q_ref, kp_ref, kc_ref, vp_ref, vc_ref, o_ref, lse_ref):
        valid = _band_mask(pl.program_id(1))
        head0, head1 = _head_lanes()
        lane = lax.broadcasted_iota(jnp.int32, (1, HEAD_PAIR), 1)
        stats = jnp.zeros((BLK, HEAD_PAIR), F32)
        for p in range(N_SLABS):
            sl = slice(p * HEAD_PAIR, (p + 1) * HEAD_PAIR)
            k2 = jnp.concatenate([kp_ref[:, sl], kc_ref[:, sl]], axis=0)
            v2 = jnp.concatenate([vp_ref[:, sl], vc_ref[:, sl]], axis=0)
            s = jnp.where(valid, _nt(_stack_heads(q_ref[:, sl], head0, head1), k2), MASKED)
            m = jnp.max(s, axis=1, keepdims=True)
            e = jnp.exp(s - m)
            den = jnp.sum(e, axis=1, keepdims=True)
            o_ref[:, sl] = _unstack_heads(_nn(e.astype(BF16), v2) / den, head0)
            lse = m + jnp.log(den)
            stats = jnp.where(lane == 2 * p, lse[:BLK], jnp.where(lane == 2 * p + 1, lse[BLK:], stats))
        lse_ref[...] = stats

    def blk(kind, prev):
        if prev:
            return pl.BlockSpec((None, BLK, ATTN_W), lambda rho, n: (rho, jnp.maximum(n - 1, 0), kind))
        return pl.BlockSpec((None, BLK, ATTN_W), lambda rho, n: (rho, n, kind))

    return _call(
        body, name=f"attn_fwd_g{group}", grid=(r, nb),
        out_shape=[_array((r, L, ATTN_W), F32), _array((r, L, HEAD_PAIR), F32)],
        in_specs=[blk(0, False), blk(1, True), blk(1, False), blk(2, True), blk(2, False)],
        out_specs=[pl.BlockSpec((None, BLK, ATTN_W), lambda rho, n: (rho, n, 0)),
                   pl.BlockSpec((None, BLK, HEAD_PAIR), lambda rho, n: (rho, n, 0))],
        vmem_mib=24, args=(qkv, qkv, qkv, qkv, qkv), carried=carried)


def _attn_combine(outs, lses, tm):
    T = outs[0].shape[1]

    def body(o0, l0, o1, l1, o2, l2, attn_ref, lse_ref, so1, sl1, so2, sl2):
        for o_in, l_in, so, sl, r in ((o1, l1, so1, sl1, DILATIONS[1]), (o2, l2, so2, sl2, DILATIONS[2])):
            _rows_by_token(so, o_in, r, tm // r)
            _rows_by_token(sl, l_in, r, tm // r, n=1)
        head0, _ = _head_lanes()
        a0, a1, a2 = l0[0], sl1[0], sl2[0]
        mx = jnp.maximum(jnp.maximum(a0, a1), a2)
        e0, e1, e2 = jnp.exp(a0 - mx), jnp.exp(a1 - mx), jnp.exp(a2 - mx)
        tot = e0 + e1 + e2
        lse_ref[...] = mx + jnp.log(tot)
        w0, w1, w2 = e0 / tot, e1 / tot, e2 / tot
        for s in range(N_SLABS):
            lanes = slice(s * HEAD_PAIR, (s + 1) * HEAD_PAIR)

            def lanes_of(w):
                return jnp.where(head0, w[:, 2 * s:2 * s + 1], w[:, 2 * s + 1:2 * s + 2])

            attn_ref[:, lanes] = lanes_of(w0) * o0[0, :, lanes] + lanes_of(w1) * so1[s] + lanes_of(w2) * so2[s]

    ins, specs = [], []
    for g, r in enumerate(DILATIONS):
        ins += [outs[g], lses[g]]
        specs += [pl.BlockSpec((r, tm // r, ATTN_W), lambda i: (0, i, 0)),
                  pl.BlockSpec((r, tm // r, HEAD_PAIR), lambda i: (0, i, 0))]
    return _pallas(
        body, name="attn_combine", grid=(T // tm,),
        out_shape=[_array((T, ATTN_W), F32), _array((T, HEAD_PAIR), F32)], in_specs=specs,
        out_specs=[pl.BlockSpec((tm, ATTN_W), lambda i: (i, 0)), pl.BlockSpec((tm, HEAD_PAIR), lambda i: (i, 0))],
        scratch_shapes=[_slab_scratch(tm), _slab_scratch(tm, 1), _slab_scratch(tm), _slab_scratch(tm, 1)],
        compiler_params=_params(("parallel",), 32),
    )(*ins)


def _gelu(x):
    return 0.5 * x * (1.0 + lax.erf(x * (1.0 / math.sqrt(2.0))))


def _gelu_grad(x):
    return 0.5 * (1.0 + lax.erf(x * (1.0 / math.sqrt(2.0)))) + x * jnp.exp(-0.5 * x * x) * (1.0 / math.sqrt(2.0 * math.pi))


def _causal():
    row = lax.broadcasted_iota(jnp.int32, (BLK, BLK), 0)
    col = lax.broadcasted_iota(jnp.int32, (BLK, BLK), 1)
    return col <= row


def _bias_lanes(bt):
    grp = lax.broadcasted_iota(jnp.int32, (1, SGU_W), 1) // 64
    out = jnp.zeros((BLK, SGU_W), F32)
    for g in range(8):
        out = jnp.where(grp == g, bt[:, g:g + 1], out)
    return out


def _sgu_normalise(uv, ln_g, ln_b):
    z = _gelu(uv)
    u, v = z[:, :SGU_W], z[:, SGU_W:]
    mu = jnp.mean(v, axis=-1, keepdims=True)
    xc = v - mu
    rstd = lax.rsqrt(jnp.mean(xc * xc, axis=-1, keepdims=True) + EPS)
    xh = xc * rstd
    return u, xh, rstd, xh * ln_g + ln_b


def _sgu_mix(wc_ref, vb, head0):
    chunks = []
    for ch in range(vb.shape[0] // BLK):
        pairs = []
        for p in range(SGU_W // HEAD_PAIR):
            v_pair = vb[ch * BLK:(ch + 1) * BLK, p * HEAD_PAIR:(p + 1) * HEAD_PAIR]
            pairs.append(jnp.where(head0, _nn(wc_ref[2 * p], v_pair), _nn(wc_ref[2 * p + 1], v_pair)))
        chunks.append(jnp.concatenate(pairs, axis=1))
    return jnp.concatenate(chunks, axis=0)


def _sgu_fwd(uvg, ln_g, ln_b, w_s, b_t, tm):
    T = uvg.shape[0]

    def body(uv_ref, g_ref, b_ref, w_ref, bt_ref, o_ref, wc_ref, bias_ref):
        @pl.when(pl.program_id(0) == 0)
        def _():
            causal = _causal()
            for g in range(8):
                wc_ref[g] = jnp.where(causal, w_ref[g], 0.0).astype(BF16)
            bias_ref[...] = _bias_lanes(bt_ref[...])

        u, _, _, vn = _sgu_normalise(uv_ref[...], g_ref[...], b_ref[...])
        mixed = _sgu_mix(wc_ref, vn.astype(BF16), _head_lanes()[0])
        o_ref[...] = (u * (mixed + jnp.tile(bias_ref[...], (tm // BLK, 1)))).astype(BF16)

    vec = pl.BlockSpec((1, SGU_W), lambda i: (0, 0))
    return _pallas(
        body, name="sgu_fwd", grid=(T // tm,), out_shape=_array((T, SGU_W), BF16),
        in_specs=[pl.BlockSpec((tm, 2 * SGU_W), lambda i: (i, 0)), vec, vec,
                  pl.BlockSpec((8, BLK, BLK), lambda i: (0, 0, 0)), pl.BlockSpec((BLK, 8), lambda i: (0, 0))],
        out_specs=pl.BlockSpec((tm, SGU_W), lambda i: (i, 0)),
        scratch_shapes=[pltpu.VMEM((8, BLK, BLK), BF16), pltpu.VMEM((BLK, SGU_W), F32)],
        compiler_params=_params(("arbitrary",), 32),
    )(uvg, ln_g, ln_b, w_s, b_t)


def _merge_fwd(attn, sgu, wpa_t, wps_t, uvg, tm):
    T = attn.shape[0]

    def body(attn_ref, sgu_ref, wpa_ref, wps_ref, ga_ref, gb_ref, pa_ref, ps_ref, m_ref):
        pa = _nt(attn_ref[...].astype(BF16), wpa_ref[...])
        ps = _nt(sgu_ref[...], wps_ref[...])
        pa_ref[...] = pa
        ps_ref[...] = ps
        m_ref[...] = (jax.nn.sigmoid(ga_ref[...]) * pa + jax.nn.sigmoid(gb_ref[...]) * ps).astype(BF16)

    half = pl.BlockSpec((tm, ATTN_W), lambda i: (i, 0))
    wide = pl.BlockSpec((tm, D_MODEL), lambda i: (i, 0))
    w = pl.BlockSpec((D_MODEL, ATTN_W), lambda i: (0, 0))
    res = _array((T, D_MODEL), F32)
    return _pallas(
        body, name="merge_fwd", grid=(T // tm,),
        out_shape=[res, res, _array((T, D_MODEL), BF16)],
        in_specs=[half, half, w, w, pl.BlockSpec((tm, D_MODEL), lambda i: (i, 1)),
                  pl.BlockSpec((tm, D_MODEL), lambda i: (i, 2))],
        out_specs=[wide, wide, wide], compiler_params=_params(("parallel",), 40),
    )(attn, sgu, wpa_t, wps_t, uvg, uvg)


def _out_residual(merged, w_out, x, g2, tm):
    T = x.shape[0]

    def body(m_ref, w_ref, x_ref, g_ref, o_ref, h_ref):
        x1 = x_ref[...] + _nn(m_ref[...], w_ref[...])
        o_ref[...] = x1
        r = lax.rsqrt(jnp.mean(x1 * x1, axis=-1, keepdims=True) + EPS)
        h_ref[...] = (x1 * r * g_ref[...]).astype(BF16)

    wide = pl.BlockSpec((tm, D_MODEL), lambda i: (i, 0))
    return _pallas(
        body, name="out_residual", grid=(T // tm,),
        out_shape=[_array((T, D_MODEL), F32), _array((T, D_MODEL), BF16)],
        in_specs=[wide, _resident((D_MODEL, D_MODEL)), wide, pl.BlockSpec((1, D_MODEL), lambda i: (0, 0))],
        out_specs=[wide, wide], compiler_params=_params(("parallel",), 32),
    )(merged, w_out, x, g2)


def _ffn_fwd(h2, wg_t, wu_t, tm):
    T = h2.shape[0]

    def body(h_ref, wg_ref, wu_ref, gate_ref, up_ref, ff_ref):
        h = h_ref[...]
        for j in range(D_FF // FF_TILE):
            cols = slice(j * FF_TILE, (j + 1) * FF_TILE)
            gate, up = _nt(h, wg_ref[cols, :]), _nt(h, wu_ref[cols, :])
            gate_ref[:, cols] = gate.astype(BF16)
            up_ref[:, cols] = up.astype(BF16)
            ff_ref[:, cols] = (gate * jax.nn.sigmoid(gate) * up).astype(BF16)

    w = _resident((D_FF, D_MODEL))
    o = pl.BlockSpec((tm, D_FF), lambda i: (i, 0))
    res = _array((T, D_FF), BF16)
    return _pallas(
        body, name="ffn_fwd", grid=(T // tm,), out_shape=[res, res, res],
        in_specs=[pl.BlockSpec((tm, D_MODEL), lambda i: (i, 0)), w, w], out_specs=[o, o, o],
        compiler_params=_params(("parallel",), 52),
    )(h2, wg_t, wu_t)


def _down_loss(ff, w_down, x1, final_g, target, tm):
    T = x1.shape[0]

    def body(ff_ref, w_ref, x1_ref, g_ref, t_ref, dx_ref, dxb_ref, loss_ref, dg_ref):
        @pl.when(pl.program_id(0) == 0)
        def _():
            loss_ref[...] = jnp.zeros_like(loss_ref)
            dg_ref[...] = jnp.zeros_like(dg_ref)

        x2 = x1_ref[...] + _nn(ff_ref[...], w_ref[...])
        g = g_ref[...]
        r = lax.rsqrt(jnp.mean(x2 * x2, axis=-1, keepdims=True) + EPS)
        xh = x2 * r
        err = xh * g - t_ref[...]
        loss_ref[...] += jnp.sum(err * err, axis=0, keepdims=True) * (0.5 / D_MODEL)
        dy = err * (1.0 / D_MODEL)
        dg_ref[...] += jnp.sum(dy * xh, axis=0, keepdims=True)
        dxh = dy * g
        dx = r * (dxh - xh * jnp.mean(dxh * xh, axis=-1, keepdims=True))
        dx_ref[...] = dx
        dxb_ref[...] = dx.astype(BF16)

    wide = pl.BlockSpec((tm, D_MODEL), lambda i: (i, 0))
    vec = pl.BlockSpec((1, D_MODEL), lambda i: (0, 0))
    vec_shape = _array((1, D_MODEL), F32)
    return _pallas(
        body, name="down_loss", grid=(T // tm,),
        out_shape=[_array((T, D_MODEL), F32), _array((T, D_MODEL), BF16),
                   vec_shape, vec_shape],
        in_specs=[pl.BlockSpec((tm, D_FF), lambda i: (i, 0)), _resident((D_FF, D_MODEL)), wide, vec, wide],
        out_specs=[wide, wide, vec, vec],
        compiler_params=_params(("arbitrary",), 40),
    )(ff, w_down, x1, final_g, target)


def _rmsnorm_bwd(dh, xin, g, d_res, dg_ref):
    r = lax.rsqrt(jnp.mean(xin * xin, axis=-1, keepdims=True) + EPS)
    xh = xin * r
    dg_ref[...] += jnp.sum(dh * xh, axis=0, keepdims=True)
    dxh = dh * g
    return d_res + r * (dxh - xh * jnp.mean(dxh * xh, axis=-1, keepdims=True))


def _ffn_bwd(dx2, dx2b, w_down, wg_t, wu_t, gate, up, x1, g2, tm):
    T = dx2.shape[0]

    def body(dx_ref, dxb_ref, wd_ref, wg_ref, wu_ref, gate_ref, up_ref, x1_ref, g_ref,
             dgate_ref, dup_ref, dx1_ref, dx1b_ref, dg_ref):
        @pl.when(pl.program_id(0) == 0)
        def _():
            dg_ref[...] = jnp.zeros_like(dg_ref)

        dxb = dxb_ref[...]
        dh = jnp.zeros((tm, D_MODEL), F32)
        for j in range(D_FF // FF_TILE):
            cols = slice(j * FF_TILE, (j + 1) * FF_TILE)
            dff = _nt(dxb, wd_ref[cols, :])
            gate, up = gate_ref[:, cols].astype(F32), up_ref[:, cols].astype(F32)
            sg = jax.nn.sigmoid(gate)
            dgate = (dff * up * sg * (1.0 + gate * (1.0 - sg))).astype(BF16)
            dup = (dff * gate * sg).astype(BF16)
            dgate_ref[:, cols] = dgate
            dup_ref[:, cols] = dup
            dh += _nn(dgate, wg_ref[cols, :]) + _nn(dup, wu_ref[cols, :])
        dx1 = _rmsnorm_bwd(dh, x1_ref[...], g_ref[...], dx_ref[...], dg_ref)
        dx1_ref[...] = dx1
        dx1b_ref[...] = dx1.astype(BF16)

    wide = pl.BlockSpec((tm, D_MODEL), lambda i: (i, 0))
    ffw = pl.BlockSpec((tm, D_FF), lambda i: (i, 0))
    vec = pl.BlockSpec((1, D_MODEL), lambda i: (0, 0))
    w = _resident((D_FF, D_MODEL))
    ff_shape = _array((T, D_FF), BF16)
    return _pallas(
        body, name="ffn_bwd", grid=(T // tm,),
        out_shape=[ff_shape, ff_shape, _array((T, D_MODEL), F32),
                   _array((T, D_MODEL), BF16), _array((1, D_MODEL), F32)],
        in_specs=[wide, wide, w, w, w, ffw, ffw, wide, vec],
        out_specs=[ffw, ffw, wide, wide, vec], compiler_params=_params(("arbitrary",), 56),
    )(dx2, dx2b, w_down, wg_t, wu_t, gate, up, x1, g2)


def _mm_tn(a, b, *, tmm, tk, name, into=None, row_block0=0, rows_total=None):
    T, M = a.shape
    N = b.shape[1]
    rows_total = M if rows_total is None else rows_total

    def body(*refs):
        a_ref, b_ref, o_ref = refs[0], refs[1], refs[-1]

        @pl.when(pl.program_id(1) == 0)
        def _():
            o_ref[...] = jnp.zeros_like(o_ref)

        o_ref[...] += _tn(a_ref[...], b_ref[...])

    ins = [a, b] + ([] if into is None else [into])
    specs = [pl.BlockSpec((tk, tmm), lambda i, k: (k, i)), pl.BlockSpec((tk, N), lambda i, k: (k, 0))]
    return _pallas(
        body, name=name, grid=(M // tmm, T // tk),
        out_shape=_array((rows_total, N), F32),
        in_specs=specs + ([] if into is None else [ANY]),
        out_specs=pl.BlockSpec((tmm, N), lambda i, k: (row_block0 + i, 0)),
        input_output_aliases={} if into is None else {2: 0},
        compiler_params=_params(("parallel", "arbitrary"), 48),
    )(*ins)


def _in_bwd(dqkv, duv, dgates, win_t, x, g1, dx1, tm, carried=None):
    T = x.shape[0]
    uv0, gates0 = QKV_COLS, QKV_COLS + 2 * SGU_W

    def body(dq_ref, du_ref, dgt_ref, w_ref, x_ref, g_ref, d_ref, dx_ref, dg_ref):
        @pl.when(pl.program_id(0) == 0)
        def _():
            dg_ref[...] = jnp.zeros_like(dg_ref)

        dh = (_nn(dq_ref[...], w_ref[:uv0, :]) + _nn(du_ref[...], w_ref[uv0:gates0, :])
              + _nn(dgt_ref[...], w_ref[gates0:, :]))
        dx_ref[...] = _rmsnorm_bwd(dh, x_ref[...], g_ref[...], d_ref[...], dg_ref)

    def cols(n):
        return pl.BlockSpec((tm, n), lambda i: (i, 0))

    wide = cols(D_MODEL)
    vec = pl.BlockSpec((1, D_MODEL), lambda i: (0, 0))
    return _call(
        body, name="in_bwd", grid=(T // tm,),
        out_shape=[_array((T, D_MODEL), F32), _array((1, D_MODEL), F32)],
        in_specs=[cols(QKV_COLS), cols(2 * SGU_W), cols(2 * D_MODEL), _resident((IN_COLS, D_MODEL)), wide, vec, wide],
        out_specs=[wide, vec], vmem_mib=56, args=(dqkv, duv, dgates, win_t, x, g1, dx1), carried=carried)


def _merge_bwd(dx1b, w_out, pa, ps, uvg, tm, carried=None):
    T = dx1b.shape[0]

    def body(dx_ref, w_ref, pa_ref, ps_ref, ga_ref, gb_ref, dpa_ref, dps_ref, dg_ref):
        dm = _nt(dx_ref[...], w_ref[...])
        ga, gb = jax.nn.sigmoid(ga_ref[...]), jax.nn.sigmoid(gb_ref[...])
        dpa_ref[...] = (dm * ga).astype(BF16)
        dps_ref[...] = (dm * gb).astype(BF16)
        dg_ref[:, :D_MODEL] = (dm * pa_ref[...] * ga * (1.0 - ga)).astype(BF16)
        dg_ref[:, D_MODEL:] = (dm * ps_ref[...] * gb * (1.0 - gb)).astype(BF16)

    wide = pl.BlockSpec((tm, D_MODEL), lambda i: (i, 0))
    res = _array((T, D_MODEL), BF16)
    return _call(
        body, name="merge_bwd", grid=(T // tm,),
        out_shape=[res, res, _array((T, 2 * D_MODEL), BF16)],
        in_specs=[wide, _resident((D_MODEL, D_MODEL)), wide, wide,
                  pl.BlockSpec((tm, D_MODEL), lambda i: (i, 1)), pl.BlockSpec((tm, D_MODEL), lambda i: (i, 2))],
        out_specs=[wide, wide, pl.BlockSpec((tm, 2 * D_MODEL), lambda i: (i, 0))],
        vmem_mib=48, args=(dx1b, w_out, pa, ps, uvg, uvg), carried=carried)


def _proj_bwd(dpa, dps, wpa_t, wps_t, tm):
    T = dpa.shape[0]

    def body(dpa_ref, dps_ref, wpa_ref, wps_ref, dattn_ref, dsgu_ref):
        dattn_ref[...] = _nn(dpa_ref[...], wpa_ref[...])
        dsgu_ref[...] = _nn(dps_ref[...], wps_ref[...])

    wide = pl.BlockSpec((tm, D_MODEL), lambda i: (i, 0))
    half = pl.BlockSpec((tm, ATTN_W), lambda i: (i, 0))
    w = pl.BlockSpec((D_MODEL, ATTN_W), lambda i: (0, 0))
    res = _array((T, ATTN_W), F32)
    return _pallas(
        body, name="proj_bwd", grid=(T // tm,), out_shape=[res, res], in_specs=[wide, wide, w, w],
        out_specs=[half, half], compiler_params=_params(("parallel",), 32),
    )(dpa, dps, wpa_t, wps_t)


def _sgu_bwd(uvg, dsgu, ln_g, ln_b, w_s, b_t, tm, carried=None):
    T = uvg.shape[0]
    nsteps = T // tm

    def body(uv_ref, ds_ref, g_ref, b_ref, w_ref, bt_ref, duv_ref, dw_ref, dbt_ref, dg_ref, db_ref,
             wc_ref, wct_ref, bias_ref, dbias_ref):
        step = pl.program_id(0)
        head0, head1 = _head_lanes()

        @pl.when(step == 0)
        def _():
            causal = _causal()
            for g in range(8):
                wc = jnp.where(causal, w_ref[g], 0.0)
                wc_ref[g] = wc.astype(BF16)
                wct_ref[g] = wc.T.astype(BF16)
            bias_ref[...] = _bias_lanes(bt_ref[...])
            dbias_ref[...] = jnp.zeros_like(dbias_ref)
            dw_ref[...] = jnp.zeros_like(dw_ref)
            dg_ref[...] = jnp.zeros_like(dg_ref)
            db_ref[...] = jnp.zeros_like(db_ref)

        uv = uv_ref[...]
        ln_gain = g_ref[...]
        u, xh, rstd, vn = _sgu_normalise(uv, ln_gain, b_ref[...])
        vb = vn.astype(BF16)
        mixed = _sgu_mix(wc_ref, vb, head0) + jnp.tile(bias_ref[...], (tm // BLK, 1))
        dout = ds_ref[...]
        du = dout * mixed
        dmixed = dout * u
        dmb = dmixed.astype(BF16)
        dvn_chunks = []
        for ch in range(tm // BLK):
            rows = slice(ch * BLK, (ch + 1) * BLK)
            dbias_ref[...] += dmixed[rows]
            pairs = []
            for p in range(SGU_W // HEAD_PAIR):
                lanes = slice(p * HEAD_PAIR, (p + 1) * HEAD_PAIR)
                dm_pair, v_pair = dmb[rows, lanes], vb[rows, lanes]
                acc = jnp.zeros((BLK, HEAD_PAIR), F32)
                for hh, half in enumerate((head0, head1)):
                    dm_h = jnp.where(half, dm_pair, jnp.zeros_like(dm_pair))
                    dw_ref[2 * p + hh] += _nt(dm_h, v_pair)
                    acc += _nn(wct_ref[2 * p + hh], dm_h)
                pairs.append(acc)
            dvn_chunks.append(jnp.concatenate(pairs, axis=1))
        dvn = jnp.concatenate(dvn_chunks, axis=0)
        dg_ref[...] += jnp.sum(dvn * xh, axis=0, keepdims=True)
        db_ref[...] += jnp.sum(dvn, axis=0, keepdims=True)
        dxh = dvn * ln_gain
        dv = rstd * (dxh - jnp.mean(dxh, axis=-1, keepdims=True) - xh * jnp.mean(dxh * xh, axis=-1, keepdims=True))
        dgelu = _gelu_grad(uv)
        duv_ref[:, :SGU_W] = (du * dgelu[:, :SGU_W]).astype(BF16)
        duv_ref[:, SGU_W:] = (dv * dgelu[:, SGU_W:]).astype(BF16)

        @pl.when(step == nsteps - 1)
        def _():
            causal = _causal()
            for g in range(8):
                dw_ref[g] = jnp.where(causal, dw_ref[g], 0.0)
            grp = lax.broadcasted_iota(jnp.int32, (1, SGU_W), 1) // 64
            col = lax.broadcasted_iota(jnp.int32, (1, 8), 1)
            dbias = dbias_ref[...]
            out = jnp.zeros((BLK, 8), F32)
            for g in range(8):
                s = jnp.sum(jnp.where(grp == g, dbias, 0.0), axis=1, keepdims=True)
                out = jnp.where(col == g, s, out)
            dbt_ref[...] = out

    vec = pl.BlockSpec((1, SGU_W), lambda i: (0, 0))
    w3 = pl.BlockSpec((8, BLK, BLK), lambda i: (0, 0, 0))
    bt = pl.BlockSpec((BLK, 8), lambda i: (0, 0))
    return _call(
        body, name="sgu_bwd", grid=(nsteps,),
        out_shape=[_array((T, 2 * SGU_W), BF16), _array((8, BLK, BLK), F32),
                   _array((BLK, 8), F32), _array((1, SGU_W), F32),
                   _array((1, SGU_W), F32)],
        in_specs=[pl.BlockSpec((tm, 2 * SGU_W), lambda i: (i, 0)), pl.BlockSpec((tm, SGU_W), lambda i: (i, 0)),
                  vec, vec, w3, bt],
        out_specs=[pl.BlockSpec((tm, 2 * SGU_W), lambda i: (i, 0)), w3, bt, vec, vec],
        scratch_shapes=[pltpu.VMEM((8, BLK, BLK), BF16), pltpu.VMEM((8, BLK, BLK), BF16),
                        pltpu.VMEM((BLK, SGU_W), F32), pltpu.VMEM((BLK, SGU_W), F32)],
        vmem_mib=40, args=(uvg, dsgu, ln_g, ln_b, w_s, b_t), carried=carried)


D_LANE0 = 8


def _attn_bwd_prepare(dattn, attn, lse, tm):
    T = dattn.shape[0]

    def body(da_ref, at_ref, lse_ref, *outs):
        scr_da, scr_st = outs[-2:]
        outs = outs[:-2]
        head0, _ = _head_lanes()
        lane = lax.broadcasted_iota(jnp.int32, (1, HEAD_PAIR), 1)
        stats = lse_ref[...]
        for s in range(N_SLABS):
            lanes = slice(s * HEAD_PAIR, (s + 1) * HEAD_PAIR)
            da = da_ref[:, lanes]
            pp = da * at_ref[:, lanes]
            d0 = jnp.sum(jnp.where(head0, pp, 0.0), axis=1, keepdims=True)
            d1 = jnp.sum(jnp.where(head0, 0.0, pp), axis=1, keepdims=True)
            stats = jnp.where(lane == D_LANE0 + 2 * s, d0, jnp.where(lane == D_LANE0 + 2 * s + 1, d1, stats))
            scr_da[s] = da
            outs[0][0, :, lanes] = da.astype(BF16)
        scr_st[0] = stats
        outs[1][0] = stats
        for g, r in enumerate(DILATIONS):
            if r > 1:
                _rows_by_residue(outs[2 * g], scr_da, r, tm // r, BF16)
                _rows_by_residue(outs[2 * g + 1], scr_st, r, tm // r, F32, n=1)

    tok = pl.BlockSpec((tm, ATTN_W), lambda i: (i, 0))
    shapes, specs = [], []
    for r in DILATIONS:
        shapes += [_array((r, T // r, ATTN_W), BF16), _array((r, T // r, HEAD_PAIR), F32)]
        specs += [pl.BlockSpec((r, tm // r, ATTN_W), lambda i: (0, i, 0)),
                  pl.BlockSpec((r, tm // r, HEAD_PAIR), lambda i: (0, i, 0))]
    return _pallas(
        body, name="attn_bwd_prepare", grid=(T // tm,), out_shape=shapes,
        in_specs=[tok, tok, pl.BlockSpec((tm, HEAD_PAIR), lambda i: (i, 0))],
        out_specs=specs, scratch_shapes=[_slab_scratch(tm), _slab_scratch(tm, 1)],
        compiler_params=_params(("parallel",), 40),
    )(dattn, attn, lse)


def _attn_bwd(qkv, dattn, stats, group, carried=None):
    r, L, _ = qkv.shape
    nb = L // BLK

    def body(q_ref, kp_ref, kc_ref, vp_ref, vc_ref, da_ref, st_ref, dq_ref, dk_ref, dv_ref, carry_k, carry_v):
        n = pl.program_id(1)

        @pl.when(n == 0)
        def _():
            carry_k[...] = jnp.zeros_like(carry_k)
            carry_v[...] = jnp.zeros_like(carry_v)

        @pl.when(n < nb)
        def _():
            valid = _band_mask(n)
            head0, head1 = _head_lanes()
            st = st_ref[...]
            for p in range(N_SLABS):
                sl = slice(p * HEAD_PAIR, (p + 1) * HEAD_PAIR)
                k2 = jnp.concatenate([kp_ref[:, sl], kc_ref[:, sl]], axis=0)
                v2 = jnp.concatenate([vp_ref[:, sl], vc_ref[:, sl]], axis=0)
                qs = _stack_heads(q_ref[:, sl], head0, head1)
                das = _stack_heads(da_ref[:, sl], head0, head1)
                prob = jnp.where(valid, jnp.exp(_nt(qs, k2) - _per_head(st, 2 * p)), 0.0)
                ds = (prob * (_nt(das, v2) - _per_head(st, D_LANE0 + 2 * p))).astype(BF16)
                dk2 = _tn(ds, qs)
                dv2 = _tn(prob.astype(BF16), das)
                dq_ref[:, sl] = _unstack_heads(_nn(ds, k2), head0)
                dk_ref[:, sl] = carry_k[:, sl] + dk2[:BLK]
                dv_ref[:, sl] = (carry_v[:, sl] + dv2[:BLK]).astype(BF16)
                carry_k[:, sl] = dk2[BLK:]
                carry_v[:, sl] = dv2[BLK:]

        @pl.when(n == nb)
        def _():
            dk_ref[...] = carry_k[...]
            dv_ref[...] = carry_v[...].astype(BF16)

    def cur(kind, width=ATTN_W):
        return pl.BlockSpec((None, BLK, width), lambda rho, n: (rho, jnp.minimum(n, nb - 1), kind))

    def prev(kind):
        return pl.BlockSpec((None, BLK, ATTN_W), lambda rho, n: (rho, jnp.clip(n - 1, 0, nb - 1), kind))

    res = _array((r, L, ATTN_W), F32)
    return _call(
        body, name=f"attn_bwd_g{group}", grid=(r, nb + 1), out_shape=[res, res, _array((r, L, ATTN_W), BF16)],
        in_specs=[cur(0), prev(1), cur(1), prev(2), cur(2), cur(0), cur(0, HEAD_PAIR)],
        out_specs=[cur(0), prev(0), prev(0)],
        scratch_shapes=[pltpu.VMEM((BLK, ATTN_W), F32), pltpu.VMEM((BLK, ATTN_W), F32)],
        vmem_mib=32, args=(qkv, qkv, qkv, qkv, qkv, dattn, stats), carried=carried)


def _dqkv_token_order(dqkv_groups, tables, tm):
    T = tables[0].shape[0]

    def body(*refs):
        ins = refs[:9]
        cos_ref, s1_ref, s2_ref, o_ref, scr = refs[9:]
        cos, s1, s2 = cos_ref[...], s1_ref[...], s2_ref[...]
        for g, r in enumerate(DILATIONS):
            for kind in range(3):
                src = ins[3 * g + kind]
                if r > 1:
                    _rows_by_token(scr, src, r, tm // r)
                for s in range(N_SLABS):
                    val = scr[s] if r > 1 else src[0, :, s * HEAD_PAIR:(s + 1) * HEAD_PAIR]
                    if kind < 2:
                        val = _rope_transposed(val, cos, s1, s2)
                    if kind == 0:
                        val = val * 0.125
                    at = (3 * kind + g) * ATTN_W + s * HEAD_PAIR
                    o_ref[:, at:at + HEAD_PAIR] = val.astype(BF16)

    specs = []
    for r in DILATIONS:
        specs += [pl.BlockSpec((r, tm // r, ATTN_W), lambda i: (0, i, 0))] * 3
    row = pl.BlockSpec((tm, HEAD_PAIR), lambda i: (i, 0))
    flat = [a for grp in dqkv_groups for a in grp]
    return _pallas(
        body, name="dqkv_token_order", grid=(T // tm,), out_shape=_array((T, QKV_COLS), BF16),
        in_specs=specs + [row] * 3, out_specs=pl.BlockSpec((tm, QKV_COLS), lambda i: (i, 0)),
        scratch_shapes=[_slab_scratch(tm)], compiler_params=_params(("parallel",), 48),
    )(*flat, *tables)


def _row_tile(rows):
    for cand in (320, 256, 176, 128):
        if rows % cand == 0:
            return cand
    return rows


def _pair_sum(grad4, recv, chip, name):
    _, _, rows, cols = grad4.shape
    tr = _row_tile(rows)

    def body(ids_ref, g_ref, r_ref, gown_ref, rown_ref, sum_ref, own_ref):
        sum_ref[...] = (g_ref[...] + r_ref[...]).astype(BF16)

        @pl.when(pl.program_id(1) == 0)
        def _():
            own_ref[...] = gown_ref[...] + rown_ref[...]

    grid_spec = pltpu.PrefetchScalarGridSpec(
        num_scalar_prefetch=1, grid=(rows // tr, 4),
        in_specs=[pl.BlockSpec((None, None, tr, cols), lambda i, q, ids: (q, ids[1], i, 0)),
                  pl.BlockSpec((None, tr, cols), lambda i, q, ids: (q, i, 0)),
                  pl.BlockSpec((None, None, tr, cols), lambda i, q, ids: (ids[0], ids[1], i, 0)),
                  pl.BlockSpec((None, tr, cols), lambda i, q, ids: (ids[0], i, 0))],
        out_specs=[pl.BlockSpec((None, tr, cols), lambda i, q, ids: (q, i, 0)),
                   pl.BlockSpec((tr, cols), lambda i, q, ids: (i, 0))])
    return _pallas(
        body, name=name, grid_spec=grid_spec,
        out_shape=[_array((4, rows, cols), BF16), _array((rows, cols), F32)],
        compiler_params=_params(("arbitrary", "arbitrary"), 32),
    )(chip, grad4, recv, grad4, recv)


def _chip_sum(own, others, name):
    rows, cols = own.shape
    tr = _row_tile(rows)

    def body(own_ref, oth_ref, o_ref):
        total = own_ref[...]
        for j in range(3):
            total = total + oth_ref[j].astype(F32)
        o_ref[...] = total

    blk = pl.BlockSpec((tr, cols), lambda i: (i, 0))
    return _pallas(
        body, name=name, grid=(rows // tr,), out_shape=_array((rows, cols), F32),
        in_specs=[blk, pl.BlockSpec((3, tr, cols), lambda i: (0, i, 0))], out_specs=blk,
        compiler_params=_params(("parallel",), 32),
    )(own, others)


def _adam_math(w, g, m, v):
    m = ADAM_B1 * m + (1.0 - ADAM_B1) * g
    v = ADAM_B2 * v + (1.0 - ADAM_B2) * (g * g)
    m_hat = m / (1.0 - ADAM_B1 ** ADAM_STEP)
    v_hat = v / (1.0 - ADAM_B2 ** ADAM_STEP)
    delta = -ADAM_LR * (m_hat / (jnp.sqrt(v_hat) + ADAM_EPS) + ADAM_WD * w)
    return delta, m, v


def _adamw(w, g, m, v, name):
    rows, cols = w.shape
    tr = _row_tile(rows)

    def body(w_ref, g_ref, m_ref, v_ref, d_ref, nm_ref, nv_ref):
        d_ref[...], nm_ref[...], nv_ref[...] = _adam_math(w_ref[...], g_ref[...], m_ref[...], v_ref[...])

    blk = pl.BlockSpec((tr, cols), lambda i: (i, 0))
    res = _array((rows, cols), F32)
    return _pallas(
        body, name=name, grid=(rows // tr,), out_shape=[res, res, res], in_specs=[blk] * 4, out_specs=[blk] * 3,
        compiler_params=_params(("parallel",), 32),
    )(w, g, m, v)


def _small_update(parts, w, m, v, loss_rows):
    rows = w.shape[0]

    def body(p_ref, w_ref, m_ref, v_ref, g_ref, d_ref, nm_ref, nv_ref, loss_ref):
        g = p_ref[0:rows, :]
        for dev in range(1, N_DEV):
            g = g + p_ref[dev * rows:(dev + 1) * rows, :]
        g_ref[...] = g
        d_ref[...], nm_ref[...], nv_ref[...] = _adam_math(w_ref[...], g, m_ref[...], v_ref[...])
        loss_ref[...] = jnp.sum(jnp.sum(g[rows - loss_rows:, :], axis=1, keepdims=True), axis=0, keepdims=True)

    res = jax.ShapeDtypeStruct((rows, HEAD_PAIR), F32)
    return _pallas(
        body, name="small_update", out_shape=[res, res, res, res, jax.ShapeDtypeStruct((1, 1), F32)],
        compiler_params=pltpu.CompilerParams(vmem_limit_bytes=32 * MIB),
    )(parts, w, m, v)


def kernel(x, positions, norm1_g, w_in, sgu_ln_g, sgu_ln_b, w_spatial, b_spatial, w_proj_attn, w_proj_sgu, w_out, norm2_g, w_ffn_gate, w_ffn_up, w_ffn_down, final_g, loss_target, m_norm1_g, m_w_in, m_sgu_ln_g, m_sgu_ln_b, m_w_spatial, m_b_spatial, m_w_proj_attn, m_w_proj_sgu, m_w_out, m_norm2_g, m_w_ffn_gate, m_w_ffn_up, m_w_ffn_down, m_final_g, v_norm1_g, v_w_in, v_sgu_ln_g, v_sgu_ln_b, v_w_spatial, v_b_spatial, v_w_proj_attn, v_w_proj_sgu, v_w_out, v_norm2_g, v_w_ffn_gate, v_w_ffn_up, v_w_ffn_down, v_final_g):
    T = x.shape[1]
    tm = 512
    xt = x[0]
    target = loss_target[0]
    chip = jnp.stack([2 * lax.axis_index("x") + lax.axis_index("y"), lax.axis_index("c")]).astype(jnp.int32)

    def bf16_rows(w, transpose):
        return (jnp.transpose(w[0]) if transpose else w[0]).astype(BF16)

    (win_t,) = _all_gather([bf16_rows(w_in, True)]).run_alone("w_in_all_gather")
    ffn_in = _all_gather([bf16_rows(w_ffn_gate, True), bf16_rows(w_ffn_up, True)])
    mixers = _all_gather([bf16_rows(w_out, False), bf16_rows(w_proj_attn, True), bf16_rows(w_proj_sgu, True)])
    ffn_out = _all_gather([bf16_rows(w_ffn_down, False)])

    inv_freq = ROPE_THETA ** (-jnp.arange(0, 2 * ROPE_HALF, 2, dtype=F32) / (2 * ROPE_HALF))
    inv_freq_row = jnp.tile(jnp.concatenate([inv_freq, inv_freq, jnp.zeros((48,), F32)]), 2).reshape(1, HEAD_PAIR)
    tables = _rope_tables(positions.reshape(T, 1), inv_freq_row, tm)
    b_t = jnp.transpose(b_spatial[0])

    h = _rmsnorm_fwd(xt, norm1_g, "norm1_fwd", tm)
    qkv, (wg_t, wu_t) = _qkv_proj(h, win_t, tables, tm, carried=ffn_in)
    (uvg,), (wout, wpa_t, wps_t) = _uv_gate_proj(h, win_t, tm, carried=mixers)
    fwd0, (wd,) = _attn_fwd(qkv[0], 0, carried=ffn_out)
    fwd = [fwd0, _attn_fwd(qkv[1], 1)[0], _attn_fwd(qkv[2], 2)[0]]
    attn, lse = _attn_combine([f[0] for f in fwd], [f[1] for f in fwd], tm)
    sgu = _sgu_fwd(uvg, sgu_ln_g, sgu_ln_b, w_spatial[0], b_t, tm)
    pa, ps, merged = _merge_fwd(attn, sgu, wpa_t, wps_t, uvg, tm)
    x1, h2 = _out_residual(merged, wout, xt, norm2_g, tm)
    gate, up, ff = _ffn_fwd(h2, wg_t, wu_t, tm)
    dx2, dx2b, loss_cols, d_final_g = _down_loss(ff, wd, x1, final_g.reshape(1, D_MODEL), target, tm)

    dgate, dup, dx1, dx1b, d_norm2 = _ffn_bwd(dx2, dx2b, wd, wg_t, wu_t, gate, up, x1, norm2_g, tm // 2)
    tk = min(1024, T)
    d_wd = _mm_tn(ff, dx2b, tmm=FF_TILE, tk=tk, name="grad_w_ffn_down")
    d_wg_t = _mm_tn(dgate, h2, tmm=FF_TILE, tk=tk, name="grad_w_ffn_gate")
    d_wu_t = _mm_tn(dup, h2, tmm=FF_TILE, tk=tk, name="grad_w_ffn_up")

    def by_owner(grads):
        return [g.reshape(4, 2, g.shape[0] // N_DEV, g.shape[1]) for g in grads]

    def pair_sums(grads4, from_sibling, names):
        both = [_pair_sum(g4, rv, chip, "grad_pair_sum_" + nm) for g4, rv, nm in zip(grads4, from_sibling, names)]
        return [b[0] for b in both], [b[1] for b in both]

    ffn_names = ["w_ffn_gate", "w_ffn_up", "w_ffn_down"]
    ffn4 = by_owner([d_wg_t, d_wu_t, d_wd])
    (dpa, dps, dgates), ffn_sib = _merge_bwd(dx1b, wout, pa, ps, uvg, tm, carried=_pair_exchange(ffn4))
    ffn_sums, ffn_own = pair_sums(ffn4, ffn_sib, ffn_names)

    d_wout = _mm_tn(merged, dx1b, tmm=D_MODEL, tk=tk, name="grad_w_out")
    dattn, dsgu = _proj_bwd(dpa, dps, wpa_t, wps_t, tm)
    d_wpa_t = _mm_tn(dpa, attn.astype(BF16), tmm=D_MODEL, tk=tk, name="grad_w_proj_attn")
    d_wps_t = _mm_tn(dps, sgu, tmm=D_MODEL, tk=tk, name="grad_w_proj_sgu")
    mid_names = ["w_proj_attn", "w_proj_sgu", "w_out"]
    mid4 = by_owner([d_wpa_t, d_wps_t, d_wout])
    (duv, d_ws, d_bs_t, d_ln_g, d_ln_b), mid_sib = _sgu_bwd(uvg, dsgu, sgu_ln_g, sgu_ln_b, w_spatial[0], b_t, tm,
                                                           carried=_pair_exchange(mid4))
    mid_sums, mid_own = pair_sums(mid4, mid_sib, mid_names)

    prep = _attn_bwd_prepare(dattn, attn, lse, tm)
    dqkv0, ffn_far = _attn_bwd(qkv[0], prep[0], prep[1], 0, carried=_chip_exchange(ffn_sums))
    dqkv1, mid_far = _attn_bwd(qkv[1], prep[2], prep[3], 1, carried=_chip_exchange(mid_sums))
    dqkv2, _ = _attn_bwd(qkv[2], prep[4], prep[5], 2)
    dqkv = _dqkv_token_order([dqkv0, dqkv1, dqkv2], tables, tm)
    d_win_t = _mm_tn(dqkv, h, tmm=1536, tk=tk, name="grad_w_in_qkv", rows_total=IN_COLS)
    d_win_t = _mm_tn(duv, h, tmm=512, tk=tk, name="grad_w_in_uv", into=d_win_t, row_block0=9, rows_total=IN_COLS)
    d_win_t = _mm_tn(dgates, h, tmm=512, tk=tk, name="grad_w_in_gates", into=d_win_t, row_block0=11, rows_total=IN_COLS)
    in4 = by_owner([d_win_t])
    in_sib = _pair_exchange(in4).run_alone("w_in_grad_pair_exchange")
    in_sums, in_own = pair_sums(in4, in_sib, ["w_in"])
    (dx, d_norm1), in_far = _in_bwd(dqkv, duv, dgates, win_t, xt, norm1_g, dx1, tm, carried=_chip_exchange(in_sums))

    names = ["w_in"] + mid_names + ffn_names
    reduced = [_chip_sum(o, f, "grad_total_" + nm)
               for o, f, nm in zip(in_own + mid_own + ffn_own, in_far + mid_far + ffn_far, names)]
    transposed = (True, True, True, False, True, True, False)
    g_big = [jnp.transpose(r) if t else r for r, t in zip(reduced, transposed)]

    small_w = [norm1_g, sgu_ln_g, sgu_ln_b, w_spatial, b_spatial, norm2_g, final_g]
    small_m = [m_norm1_g, m_sgu_ln_g, m_sgu_ln_b, m_w_spatial, m_b_spatial, m_norm2_g, m_final_g]
    small_v = [v_norm1_g, v_sgu_ln_g, v_sgu_ln_b, v_w_spatial, v_b_spatial, v_norm2_g, v_final_g]
    small_g = [d_norm1, d_ln_g, d_ln_b, d_ws, jnp.transpose(d_bs_t), d_norm2, d_final_g]
    zeros = jnp.zeros((D_MODEL,), F32)

    def flat(parts, last):
        return jnp.concatenate([p.reshape(-1) for p in parts] + [last]).reshape(-1, HEAD_PAIR)

    part = flat(small_g, loss_cols.reshape(-1))
    (gathered,) = _all_gather([part]).run_alone("small_grads_all_gather", vmem=True)
    g_s, d_s, nm_s, nv_s, loss = _small_update(gathered, flat(small_w, zeros), flat(small_m, zeros),
                                               flat(small_v, zeros), D_MODEL // HEAD_PAIR)

    def unflat(vec):
        vec = vec.reshape(-1)
        out, at = [], 0
        for wgt in small_w:
            out.append(vec[at:at + wgt.size].reshape(wgt.shape))
            at += wgt.size
        return out

    small = [unflat(a) for a in (g_s, d_s, nm_s, nv_s)]

    big_w = [w_in, w_proj_attn, w_proj_sgu, w_out, w_ffn_gate, w_ffn_up, w_ffn_down]
    big_m = [m_w_in, m_w_proj_attn, m_w_proj_sgu, m_w_out, m_w_ffn_gate, m_w_ffn_up, m_w_ffn_down]
    big_v = [v_w_in, v_w_proj_attn, v_w_proj_sgu, v_w_out, v_w_ffn_gate, v_w_ffn_up, v_w_ffn_down]
    big_out = []
    for wgt, g, mm, vv, nm in zip(big_w, g_big, big_m, big_v, names):
        d, nm_, nv_ = _adamw(wgt[0], g, mm[0], vv[0], "adamw_" + nm)
        big_out.append([a[None] for a in (g, d, nm_, nv_)])

    small_at = {0: 0, 2: 1, 3: 2, 4: 3, 5: 4, 9: 5, 13: 6}
    big_at = {1: 0, 6: 1, 7: 2, 8: 3, 10: 4, 11: 5, 12: 6}
    outs = [loss[0, 0], dx[None]]
    for kind in range(4):
        for idx in range(14):
            outs.append(small[kind][small_at[idx]] if idx in small_at else big_out[big_at[idx]][kind])
    return tuple(outs)
```

```python
import functools
import math

import jax
import jax.numpy as jnp
from jax import lax
from jax.experimental import pallas as pl
from jax.experimental.pallas import tpu as pltpu

F32 = jnp.float32
BF16 = jnp.bfloat16

D_MODEL = 1024
HEAD_PAIR = 128
ATTN_W = 512
DILATIONS = (1, 4, 16)
BLK = 128
ROPE_HALF = 8
ROPE_THETA = 500000.0
SGU_W = 512
QKV_COLS = 4608
IN_COLS = 7680
D_FF = 2816
FF_TILE = 1408
EPS = 1e-6
N_DEV = 8
MASKED = -1e30

ADAM_LR = 0.001
ADAM_B1 = 0.9
ADAM_B2 = 0.999
ADAM_EPS = 1e-08
ADAM_WD = 0.01
ADAM_STEP = 10

MIB = 1024 * 1024
MESH = pl.DeviceIdType.MESH
ANY = pl.BlockSpec(memory_space=pl.ANY)


def _array(shape, dtype):
    return pltpu.HBM(tuple(shape), dtype)


PIN_BYTES = MIB // 4


def _pin(x):
    if x.size * x.dtype.itemsize < PIN_BYTES:
        return x
    return pltpu.with_memory_space_constraint(x, pltpu.HBM)


def _pallas(body, **kwargs):
    call = pl.pallas_call(body, **kwargs)
    return lambda *args: call(*[_pin(a) for a in args])


def _params(sem, vmem_mib):
    return pltpu.CompilerParams(dimension_semantics=sem, vmem_limit_bytes=vmem_mib * MIB)


def _nt(a, b):
    return lax.dot_general(a, b, (((1,), (1,)), ((), ())), preferred_element_type=F32)


def _nn(a, b):
    return lax.dot_general(a, b, (((1,), (0,)), ((), ())), preferred_element_type=F32)


def _tn(a, b):
    return lax.dot_general(a, b, (((0,), (0,)), ((), ())), preferred_element_type=F32)


class _Exchange:
    def __init__(self, arrays, out_shapes, sem_shapes, phases):
        self.arrays, self.out_shapes, self.sem_shapes, self.phases = list(arrays), out_shapes, sem_shapes, phases

    def run_alone(self, name, vmem=False):
        n_in, n_out = len(self.arrays), len(self.out_shapes)

        def body(*refs):
            start, middle, finish = self.phases(refs[:n_in], refs[n_in:n_in + n_out], refs[n_in + n_out:])
            start()
            middle()
            finish()

        spec = pl.BlockSpec(memory_space=pltpu.VMEM) if vmem else ANY
        shapes = [jax.ShapeDtypeStruct(s.shape, s.dtype) for s in self.out_shapes] if vmem else self.out_shapes
        return (pl.pallas_call if vmem else _pallas)(
            body, name=name, out_shape=shapes, in_specs=[spec] * n_in, out_specs=[spec] * n_out,
            scratch_shapes=self.sem_shapes, compiler_params=pltpu.CompilerParams(vmem_limit_bytes=32 * MIB),
        )(*self.arrays)


def _call(body, *, name, grid, in_specs, out_specs, out_shape, args, vmem_mib, scratch_shapes=(), carried=None):
    n_in, n_out, n_scr = len(in_specs), len(out_specs), len(scratch_shapes)
    sem = ("arbitrary",) * len(grid)
    if carried is None:
        outs = _pallas(
            body, name=name, grid=grid, in_specs=in_specs, out_specs=out_specs, out_shape=out_shape,
            scratch_shapes=list(scratch_shapes), compiler_params=_params(sem, vmem_mib))(*args)
        return list(outs), []
    c_in, c_out = len(carried.arrays), len(carried.out_shapes)
    total = math.prod(grid)

    def full(*refs):
        own_in, car_in = refs[:n_in], refs[n_in:n_in + c_in]
        at = n_in + c_in
        own_out, car_out = refs[at:at + n_out], refs[at + n_out:at + n_out + c_out]
        at += n_out + c_out
        own_scr, sems = refs[at:at + n_scr], refs[at + n_scr:]
        step = pl.program_id(0)
        for axis in range(1, len(grid)):
            step = step * grid[axis] + pl.program_id(axis)
        start, middle, finish = carried.phases(car_in, car_out, sems)
        pl.when(step == 0)(start)
        pl.when(step == (3 * total) // 4)(middle)
        body(*own_in, *own_out, *own_scr)
        pl.when(step == total - 1)(finish)

    outs = _pallas(
        full, name=name, grid=grid, in_specs=list(in_specs) + [ANY] * c_in,
        out_specs=list(out_specs) + [ANY] * c_out, out_shape=list(out_shape) + list(carried.out_shapes),
        scratch_shapes=list(scratch_shapes) + list(carried.sem_shapes),
        compiler_params=_params(sem, vmem_mib))(*args, *carried.arrays)
    return list(outs[:n_out]), list(outs[n_out:])


def _all_gather(shards):
    n = len(shards)

    def phases(ins, outs, sems):
        send_sems, recv_sems, local_sems = sems
        x, y, c = lax.axis_index("x"), lax.axis_index("y"), lax.axis_index("c")
        me, sibling = (x, y, c), (x, y, 1 - c)
        chips = [(1 - x, y), (x, 1 - y), (1 - x, 1 - y)]

        def rows(m, px, py, pc):
            r = ins[m].shape[0]
            return outs[m].at[pl.ds((4 * px + 2 * py + pc) * r, r), :]

        def copy(m, k, block, to, src=None):
            return pltpu.make_async_remote_copy(
                src_ref=rows(m, *block) if src is None else src, dst_ref=rows(m, *block),
                send_sem=send_sems.at[m, k], recv_sem=recv_sems.at[m, k],
                device_id=to, device_id_type=MESH)

        def mine(m):
            return pltpu.make_async_copy(ins[m], rows(m, *me), local_sems.at[m])

        def first(m):
            return [copy(m, 0, me, sibling, src=ins[m])] + [
                copy(m, 1 + j, me, (*chip, c), src=ins[m]) for j, chip in enumerate(chips)]

        def passed(m):
            return [copy(m, 4 + j, (*chip, c), sibling) for j, chip in enumerate(chips)]

        def start():
            for m in range(n):
                mine(m).start()
            for m in range(n):
                for cp in first(m):
                    cp.start()

        def middle():
            for m in range(n):
                for j, chip in enumerate(chips):
                    copy(m, 1 + j, (*chip, c), me).wait_recv()
                    passed(m)[j].start()

        def finish():
            for m in range(n):
                copy(m, 0, sibling, me).wait_recv()
                for j, chip in enumerate(chips):
                    copy(m, 4 + j, (*chip, 1 - c), me).wait_recv()
            for m in range(n):
                for cp in first(m) + passed(m):
                    cp.wait_send()
                mine(m).wait()

        return start, middle, finish

    return _Exchange(
        shards, [_array((N_DEV * s.shape[0], s.shape[1]), s.dtype) for s in shards],
        [pltpu.SemaphoreType.DMA((n, 7)), pltpu.SemaphoreType.DMA((n, 7)), pltpu.SemaphoreType.DMA((n,))], phases)


def _pair_exchange(grads):
    n = len(grads)

    def phases(ins, outs, sems):
        send_sems, recv_sems = sems
        x, y, c = lax.axis_index("x"), lax.axis_index("y"), lax.axis_index("c")

        def copy(m):
            return pltpu.make_async_remote_copy(
                src_ref=ins[m].at[:, 1 - c], dst_ref=outs[m], send_sem=send_sems.at[m], recv_sem=recv_sems.at[m],
                device_id=(x, y, 1 - c), device_id_type=MESH)

        def start():
            for m in range(n):
                copy(m).start()

        def finish():
            for m in range(n):
                copy(m).wait()

        return start, lambda: None, finish

    return _Exchange(grads, [_array((4,) + g.shape[2:], g.dtype) for g in grads],
                     [pltpu.SemaphoreType.DMA((n,)), pltpu.SemaphoreType.DMA((n,))], phases)


def _chip_exchange(pair_sums):
    n = len(pair_sums)

    def phases(ins, outs, sems):
        send_sems, recv_sems = sems
        x, y, c = lax.axis_index("x"), lax.axis_index("y"), lax.axis_index("c")
        chips = [(1 - x, y), (x, 1 - y), (1 - x, 1 - y)]

        def copies():
            return [pltpu.make_async_remote_copy(
                src_ref=ins[m].at[2 * px + py], dst_ref=outs[m].at[j],
                send_sem=send_sems.at[m, j], recv_sem=recv_sems.at[m, j],
                device_id=(px, py, c), device_id_type=MESH)
                for m in range(n) for j, (px, py) in enumerate(chips)]

        def start():
            for cp in copies():
                cp.start()

        def finish():
            for cp in copies():
                cp.wait_recv()
            for cp in copies():
                cp.wait_send()

        return start, lambda: None, finish

    return _Exchange(pair_sums, [_array((3,) + p.shape[1:], p.dtype) for p in pair_sums],
                     [pltpu.SemaphoreType.DMA((n, 3)), pltpu.SemaphoreType.DMA((n, 3))], phases)


N_SLABS = ATTN_W // HEAD_PAIR


def _slab_scratch(tm, n=N_SLABS):
    return pltpu.VMEM((n, tm, HEAD_PAIR), F32)


def _rows_by_residue(dst_ref, slab_ref, r, tr, dtype, n=N_SLABS):
    for rho in range(r):
        for s in range(n):
            dst_ref[rho, :, s * HEAD_PAIR:(s + 1) * HEAD_PAIR] = (
                slab_ref[s, pl.ds(rho, tr, stride=r), :].astype(dtype))


def _rows_by_token(slab_ref, src_ref, r, tr, n=N_SLABS):
    for rho in range(r):
        for s in range(n):
            slab_ref[s, pl.ds(rho, tr, stride=r), :] = (
                src_ref[rho, :, s * HEAD_PAIR:(s + 1) * HEAD_PAIR].astype(F32))


def _rope_tables(pos_col, inv_freq_row, tm):
    T = pos_col.shape[0]

    def body(pos_ref, invf_ref, cos_ref, s1_ref, s2_ref):
        ang = pos_ref[...].astype(F32) * invf_ref[...]
        lane = lax.broadcasted_iota(jnp.int32, (1, HEAD_PAIR), 1) % 64
        cs, sn = jnp.cos(ang), jnp.sin(ang)
        cos_ref[...] = jnp.where(lane < 2 * ROPE_HALF, cs, 1.0)
        s1_ref[...] = jnp.where(lane < ROPE_HALF, -sn, 0.0)
        s2_ref[...] = jnp.where((lane >= ROPE_HALF) & (lane < 2 * ROPE_HALF), sn, 0.0)

    tab = _array((T, HEAD_PAIR), F32)
    row = pl.BlockSpec((tm, HEAD_PAIR), lambda i: (i, 0))
    return _pallas(
        body, name="rope_tables", grid=(T // tm,), out_shape=[tab] * 3,
        in_specs=[pl.BlockSpec((tm, 1), lambda i: (i, 0)), pl.BlockSpec((1, HEAD_PAIR), lambda i: (0, 0))],
        out_specs=[row] * 3, compiler_params=_params(("parallel",), 16),
    )(pos_col, inv_freq_row)


def _rope(y, cos, s1, s2):
    w = y.shape[1]
    rep = w // HEAD_PAIR
    return (y * jnp.tile(cos, (1, rep)) + pltpu.roll(y, w - ROPE_HALF, 1) * jnp.tile(s1, (1, rep))
            + pltpu.roll(y, ROPE_HALF, 1) * jnp.tile(s2, (1, rep)))


def _rope_transposed(dy, cos, s1, s2):
    w = dy.shape[1]
    rep = w // HEAD_PAIR
    return (dy * jnp.tile(cos, (1, rep)) + pltpu.roll(dy * jnp.tile(s1, (1, rep)), ROPE_HALF, 1)
            + pltpu.roll(dy * jnp.tile(s2, (1, rep)), w - ROPE_HALF, 1))


def _rmsnorm_fwd(x, g, name, tm):
    T = x.shape[0]

    def body(x_ref, g_ref, h_ref):
        xf = x_ref[...]
        r = lax.rsqrt(jnp.mean(xf * xf, axis=-1, keepdims=True) + EPS)
        h_ref[...] = (xf * r * g_ref[...]).astype(BF16)

    return _pallas(
        body, name=name, grid=(T // tm,), out_shape=_array((T, D_MODEL), BF16),
        in_specs=[pl.BlockSpec((tm, D_MODEL), lambda i: (i, 0)), pl.BlockSpec((1, D_MODEL), lambda i: (0, 0))],
        out_specs=pl.BlockSpec((tm, D_MODEL), lambda i: (i, 0)), compiler_params=_params(("parallel",), 24),
    )(x, g)


def _resident(shape, block=None):
    at = (0,) * len(shape) if block is None else block
    return pl.BlockSpec(shape, lambda *_: at, pipeline_mode=pl.Buffered(1))


def _qkv_proj(h, win_t, tables, tm, carried=None):
    T = h.shape[0]

    def body(h_ref, w_ref, cos_ref, s1_ref, s2_ref, o0, o1, o2, *slabs):
        hv = h_ref[...]
        cos, s1, s2 = cos_ref[...], s1_ref[...], s2_ref[...]
        for kind in range(3):
            for g, (o_ref, r) in enumerate(zip((o0, o1, o2), DILATIONS)):
                blk = 3 * kind + g
                y = _nt(hv, w_ref[blk * ATTN_W:(blk + 1) * ATTN_W, :])
                if kind < 2:
                    y = _rope(y, cos, s1, s2)
                if kind == 0:
                    y = y * 0.125
                cols = slice(kind * ATTN_W, (kind + 1) * ATTN_W)
                if r == 1:
                    o_ref[0, :, cols] = y.astype(BF16)
                    continue
                slab = slabs[blk % len(slabs)]
                for s in range(N_SLABS):
                    slab[s] = y[:, s * HEAD_PAIR:(s + 1) * HEAD_PAIR]
                for rho in range(r):
                    for s in range(N_SLABS):
                        at = kind * ATTN_W + s * HEAD_PAIR
                        o_ref[rho, :, at:at + HEAD_PAIR] = slab[s, pl.ds(rho, tm // r, stride=r), :].astype(BF16)

    row = pl.BlockSpec((tm, HEAD_PAIR), lambda i: (i, 0))
    return _call(
        body, name="qkv_proj", grid=(T // tm,),
        out_shape=[_array((r, T // r, 3 * ATTN_W), BF16) for r in DILATIONS],
        in_specs=[pl.BlockSpec((tm, D_MODEL), lambda i: (i, 0)), _resident((QKV_COLS, D_MODEL)), row, row, row],
        out_specs=[pl.BlockSpec((r, tm // r, 3 * ATTN_W), lambda i: (0, i, 0)) for r in DILATIONS],
        scratch_shapes=[_slab_scratch(tm)] * 3, vmem_mib=48, args=(h, win_t, *tables), carried=carried)


def _uv_gate_proj(h, win_t, tm, carried=None):
    T = h.shape[0]
    half = (IN_COLS - QKV_COLS) // 2

    def body(h_ref, wa_ref, wb_ref, o_ref):
        hv = h_ref[...]
        o_ref[:, :half] = _nt(hv, wa_ref[...])
        o_ref[:, half:] = _nt(hv, wb_ref[...])

    blk0 = QKV_COLS // half
    return _call(
        body, name="uv_gate_proj", grid=(T // tm,), out_shape=[_array((T, 2 * half), F32)],
        in_specs=[pl.BlockSpec((tm, D_MODEL), lambda i: (i, 0)), _resident((half, D_MODEL), (blk0, 0)),
                  _resident((half, D_MODEL), (blk0 + 1, 0))],
        out_specs=[pl.BlockSpec((tm, 2 * half), lambda i: (i, 0))], vmem_mib=40, args=(h, win_t, win_t),
        carried=carried)


def _band_mask(n):
    row = lax.broadcasted_iota(jnp.int32, (2 * BLK, 2 * BLK), 0) & (BLK - 1)
    col = lax.broadcasted_iota(jnp.int32, (2 * BLK, 2 * BLK), 1)
    has_prev = (jnp.zeros_like(row) + n) > 0
    return ((col < BLK) & (col >= row) & has_prev) | ((col >= BLK) & (col - BLK <= row))


def _head_lanes():
    lane = lax.broadcasted_iota(jnp.int32, (1, HEAD_PAIR), 1)
    return lane < 64, lane >= 64


def _stack_heads(x, head0, head1):
    zero = jnp.zeros_like(x)
    return jnp.concatenate([jnp.where(head0, x, zero), jnp.where(head1, x, zero)], axis=0)


def _unstack_heads(y, head0):
    return jnp.where(head0, y[:BLK], y[BLK:])


def _per_head(stats, col):
    return jnp.concatenate([stats[:, col:col + 1], stats[:, col + 1:col + 2]], axis=0)


def _attn_fwd(qkv, group, carried=None):
    r, L, _ = qkv.shape
    nb = L // BLK

    def body(q_ref, kp_ref, kc_ref, vp_ref, vc_ref, o_ref, lse_ref):
        valid = _band_mask(pl.program_id(1))
        head0, head1 = _head_lanes()
        lane = lax.broadcasted_iota(jnp.int32, (1, HEAD_PAIR), 1)
        stats = jnp.zeros((BLK, HEAD_PAIR), F32)
        for p in range(N_SLABS):
            sl = slice(p * HEAD_PAIR, (p + 1) * HEAD_PAIR)
            k2 = jnp.concatenate([kp_ref[:, sl], kc_ref[:, sl]], axis=0)
            v2 = jnp.concatenate([vp_ref[:, sl], vc_ref[:, sl]], axis=0)
            s = jnp.where(valid, _nt(_stack_heads(q_ref[:, sl], head0, head1), k2), MASKED)
            m = jnp.max(s, axis=1, keepdims=True)
            e = jnp.exp(s - m)
            den = jnp.sum(e, axis=1, keepdims=True)
            o_ref[:, sl] = _unstack_heads(_nn(e.astype(BF16), v2) / den, head0)
            lse = m + jnp.log(den)
            stats = jnp.where(lane == 2 * p, lse[:BLK], jnp.where(lane == 2 * p + 1, lse[BLK:], stats))
        lse_ref[...] = stats

    def blk(kind, prev):
        if prev:
            return pl.BlockSpec((None, BLK, ATTN_W), lambda rho, n: (rho, jnp.maximum(n - 1, 0), kind))
        return pl.BlockSpec((None, BLK, ATTN_W), lambda rho, n: (rho, n, kind))

    return _call(
        body, name=f"attn_fwd_g{group}", grid=(r, nb),
        out_shape=[_array((r, L, ATTN_W), F32), _array((r, L, HEAD_PAIR), F32)],
        in_specs=[blk(0, False), blk(1, True), blk(1, False), blk(2, True), blk(2, False)],
        out_specs=[pl.BlockSpec((None, BLK, ATTN_W), lambda rho, n: (rho, n, 0)),
                   pl.BlockSpec((None, BLK, HEAD_PAIR), lambda rho, n: (rho, n, 0))],
        vmem_mib=24, args=(qkv, qkv, qkv, qkv, qkv), carried=carried)


def _attn_combine(outs, lses, tm):
    T = outs[0].shape[1]

    def body(o0, l0, o1, l1, o2, l2, attn_ref, lse_ref, so1, sl1, so2, sl2):
        for o_in, l_in, so, sl, r in ((o1, l1, so1, sl1, DILATIONS[1]), (o2, l2, so2, sl2, DILATIONS[2])):
            _rows_by_token(so, o_in, r, tm // r)
            _rows_by_token(sl, l_in, r, tm // r, n=1)
        head0, _ = _head_lanes()
        a0, a1, a2 = l0[0], sl1[0], sl2[0]
        mx = jnp.maximum(jnp.maximum(a0, a1), a2)
        e0, e1, e2 = jnp.exp(a0 - mx), jnp.exp(a1 - mx), jnp.exp(a2 - mx)
        tot = e0 + e1 + e2
        lse_ref[...] = mx + jnp.log(tot)
        w0, w1, w2 = e0 / tot, e1 / tot, e2 / tot
        for s in range(N_SLABS):
            lanes = slice(s * HEAD_PAIR, (s + 1) * HEAD_PAIR)

            def lanes_of(w):
                return jnp.where(head0, w[:, 2 * s:2 * s + 1], w[:, 2 * s + 1:2 * s + 2])

            attn_ref[:, lanes] = lanes_of(w0) * o0[0, :, lanes] + lanes_of(w1) * so1[s] + lanes_of(w2) * so2[s]

    ins, specs = [], []
    for g, r in enumerate(DILATIONS):
        ins += [outs[g], lses[g]]
        specs += [pl.BlockSpec((r, tm // r, ATTN_W), lambda i: (0, i, 0)),
                  pl.BlockSpec((r, tm // r, HEAD_PAIR), lambda i: (0, i, 0))]
    return _pallas(
        body, name="attn_combine", grid=(T // tm,),
        out_shape=[_array((T, ATTN_W), F32), _array((T, HEAD_PAIR), F32)], in_specs=specs,
        out_specs=[pl.BlockSpec((tm, ATTN_W), lambda i: (i, 0)), pl.BlockSpec((tm, HEAD_PAIR), lambda i: (i, 0))],
        scratch_shapes=[_slab_scratch(tm), _slab_scratch(tm, 1), _slab_scratch(tm), _slab_scratch(tm, 1)],
        compiler_params=_params(("parallel",), 32),
    )(*ins)


def _gelu(x):
    return 0.5 * x * (1.0 + lax.erf(x * (1.0 / math.sqrt(2.0))))


def _gelu_grad(x):
    return 0.5 * (1.0 + lax.erf(x * (1.0 / math.sqrt(2.0)))) + x * jnp.exp(-0.5 * x * x) * (1.0 / math.sqrt(2.0 * math.pi))


def _causal():
    row = lax.broadcasted_iota(jnp.int32, (BLK, BLK), 0)
    col = lax.broadcasted_iota(jnp.int32, (BLK, BLK), 1)
    return col <= row


def _bias_lanes(bt):
    grp = lax.broadcasted_iota(jnp.int32, (1, SGU_W), 1) // 64
    out = jnp.zeros((BLK, SGU_W), F32)
    for g in range(8):
        out = jnp.where(grp == g, bt[:, g:g + 1], out)
    return out


def _sgu_normalise(uv, ln_g, ln_b):
    z = _gelu(uv)
    u, v = z[:, :SGU_W], z[:, SGU_W:]
    mu = jnp.mean(v, axis=-1, keepdims=True)
    xc = v - mu
    rstd = lax.rsqrt(jnp.mean(xc * xc, axis=-1, keepdims=True) + EPS)
    xh = xc * rstd
    return u, xh, rstd, xh * ln_g + ln_b


def _sgu_mix(wc_ref, vb, head0):
    chunks = []
    for ch in range(vb.shape[0] // BLK):
        pairs = []
        for p in range(SGU_W // HEAD_PAIR):
            v_pair = vb[ch * BLK:(ch + 1) * BLK, p * HEAD_PAIR:(p + 1) * HEAD_PAIR]
            pairs.append(jnp.where(head0, _nn(wc_ref[2 * p], v_pair), _nn(wc_ref[2 * p + 1], v_pair)))
        chunks.append(jnp.concatenate(pairs, axis=1))
    return jnp.concatenate(chunks, axis=0)


def _sgu_fwd(uvg, ln_g, ln_b, w_s, b_t, tm):
    T = uvg.shape[0]

    def body(uv_ref, g_ref, b_ref, w_ref, bt_ref, o_ref, wc_ref, bias_ref):
        @pl.when(pl.program_id(0) == 0)
        def _():
            causal = _causal()
            for g in range(8):
                wc_ref[g] = jnp.where(causal, w_ref[g], 0.0).astype(BF16)
            bias_ref[...] = _bias_lanes(bt_ref[...])

        u, _, _, vn = _sgu_normalise(uv_ref[...], g_ref[...], b_ref[...])
        mixed = _sgu_mix(wc_ref, vn.astype(BF16), _head_lanes()[0])
        o_ref[...] = (u * (mixed + jnp.tile(bias_ref[...], (tm // BLK, 1)))).astype(BF16)

    vec = pl.BlockSpec((1, SGU_W), lambda i: (0, 0))
    return _pallas(
        body, name="sgu_fwd", grid=(T // tm,), out_shape=_array((T, SGU_W), BF16),
        in_specs=[pl.BlockSpec((tm, 2 * SGU_W), lambda i: (i, 0)), vec, vec,
                  pl.BlockSpec((8, BLK, BLK), lambda i: (0, 0, 0)), pl.BlockSpec((BLK, 8), lambda i: (0, 0))],
        out_specs=pl.BlockSpec((tm, SGU_W), lambda i: (i, 0)),
        scratch_shapes=[pltpu.VMEM((8, BLK, BLK), BF16), pltpu.VMEM((BLK, SGU_W), F32)],
        compiler_params=_params(("arbitrary",), 32),
    )(uvg, ln_g, ln_b, w_s, b_t)


def _merge_fwd(attn, sgu, wpa_t, wps_t, uvg, tm):
    T = attn.shape[0]

    def body(attn_ref, sgu_ref, wpa_ref, wps_ref, ga_ref, gb_ref, pa_ref, ps_ref, m_ref):
        pa = _nt(attn_ref[...].astype(BF16), wpa_ref[...])
        ps = _nt(sgu_ref[...], wps_ref[...])
        pa_ref[...] = pa
        ps_ref[...] = ps
        m_ref[...] = (jax.nn.sigmoid(ga_ref[...]) * pa + jax.nn.sigmoid(gb_ref[...]) * ps).astype(BF16)

    half = pl.BlockSpec((tm, ATTN_W), lambda i: (i, 0))
    wide = pl.BlockSpec((tm, D_MODEL), lambda i: (i, 0))
    w = pl.BlockSpec((D_MODEL, ATTN_W), lambda i: (0, 0))
    res = _array((T, D_MODEL), F32)
    return _pallas(
        body, name="merge_fwd", grid=(T // tm,),
        out_shape=[res, res, _array((T, D_MODEL), BF16)],
        in_specs=[half, half, w, w, pl.BlockSpec((tm, D_MODEL), lambda i: (i, 1)),
                  pl.BlockSpec((tm, D_MODEL), lambda i: (i, 2))],
        out_specs=[wide, wide, wide], compiler_params=_params(("parallel",), 40),
    )(attn, sgu, wpa_t, wps_t, uvg, uvg)


def _out_residual(merged, w_out, x, g2, tm):
    T = x.shape[0]

    def body(m_ref, w_ref, x_ref, g_ref, o_ref, h_ref):
        x1 = x_ref[...] + _nn(m_ref[...], w_ref[...])
        o_ref[...] = x1
        r = lax.rsqrt(jnp.mean(x1 * x1, axis=-1, keepdims=True) + EPS)
        h_ref[...] = (x1 * r * g_ref[...]).astype(BF16)

    wide = pl.BlockSpec((tm, D_MODEL), lambda i: (i, 0))
    return _pallas(
        body, name="out_residual", grid=(T // tm,),
        out_shape=[_array((T, D_MODEL), F32), _array((T, D_MODEL), BF16)],
        in_specs=[wide, _resident((D_MODEL, D_MODEL)), wide, pl.BlockSpec((1, D_MODEL), lambda i: (0, 0))],
        out_specs=[wide, wide], compiler_params=_params(("parallel",), 32),
    )(merged, w_out, x, g2)


def _ffn_fwd(h2, wg_t, wu_t, tm):
    T = h2.shape[0]

    def body(h_ref, wg_ref, wu_ref, gate_ref, up_ref, ff_ref):
        h = h_ref[...]
        for j in range(D_FF // FF_TILE):
            cols = slice(j * FF_TILE, (j + 1) * FF_TILE)
            gate, up = _nt(h, wg_ref[cols, :]), _nt(h, wu_ref[cols, :])
            gate_ref[:, cols] = gate.astype(BF16)
            up_ref[:, cols] = up.astype(BF16)
            ff_ref[:, cols] = (gate * jax.nn.sigmoid(gate) * up).astype(BF16)

    w = _resident((D_FF, D_MODEL))
    o = pl.BlockSpec((tm, D_FF), lambda i: (i, 0))
    res = _array((T, D_FF), BF16)
    return _pallas(
        body, name="ffn_fwd", grid=(T // tm,), out_shape=[res, res, res],
        in_specs=[pl.BlockSpec((tm, D_MODEL), lambda i: (i, 0)), w, w], out_specs=[o, o, o],
        compiler_params=_params(("parallel",), 52),
    )(h2, wg_t, wu_t)


def _down_loss(ff, w_down, x1, final_g, target, tm):
    T = x1.shape[0]

    def body(ff_ref, w_ref, x1_ref, g_ref, t_ref, dx_ref, dxb_ref, loss_ref, dg_ref):
        @pl.when(pl.program_id(0) == 0)
        def _():
            loss_ref[...] = jnp.zeros_like(loss_ref)
            dg_ref[...] = jnp.zeros_like(dg_ref)

        x2 = x1_ref[...] + _nn(ff_ref[...], w_ref[...])
        g = g_ref[...]
        r = lax.rsqrt(jnp.mean(x2 * x2, axis=-1, keepdims=True) + EPS)
        xh = x2 * r
        err = xh * g - t_ref[...]
        loss_ref[...] += jnp.sum(err * err, axis=0, keepdims=True) * (0.5 / D_MODEL)
        dy = err * (1.0 / D_MODEL)
        dg_ref[...] += jnp.sum(dy * xh, axis=0, keepdims=True)
        dxh = dy * g
        dx = r * (dxh - xh * jnp.mean(dxh * xh, axis=-1, keepdims=True))
        dx_ref[...] = dx
        dxb_ref[...] = dx.astype(BF16)

    wide = pl.BlockSpec((tm, D_MODEL), lambda i: (i, 0))
    vec = pl.BlockSpec((1, D_MODEL), lambda i: (0, 0))
    vec_shape = _array((1, D_MODEL), F32)
    return _pallas(
        body, name="down_loss", grid=(T // tm,),
        out_shape=[_array((T, D_MODEL), F32), _array((T, D_MODEL), BF16),
                   vec_shape, vec_shape],
        in_specs=[pl.BlockSpec((tm, D_FF), lambda i: (i, 0)), _resident((D_FF, D_MODEL)), wide, vec, wide],
        out_specs=[wide, wide, vec, vec],
        compiler_params=_params(("arbitrary",), 40),
    )(ff, w_down, x1, final_g, target)


def _rmsnorm_bwd(dh, xin, g, d_res, dg_ref):
    r = lax.rsqrt(jnp.mean(xin * xin, axis=-1, keepdims=True) + EPS)
    xh = xin * r
    dg_ref[...] += jnp.sum(dh * xh, axis=0, keepdims=True)
    dxh = dh * g
    return d_res + r * (dxh - xh * jnp.mean(dxh * xh, axis=-1, keepdims=True))


def _ffn_bwd_gates(dx2b, w_down, gate, up, tm):
    T = dx2b.shape[0]

    def body(dxb_ref, wd_ref, gate_ref, up_ref, dgate_ref, dup_ref):
        dxb = dxb_ref[...]
        for j in range(D_FF // FF_TILE):
            cols = slice(j * FF_TILE, (j + 1) * FF_TILE)
            dff = _nt(dxb, wd_ref[cols, :])
            gate, up = gate_ref[:, cols].astype(F32), up_ref[:, cols].astype(F32)
            sg = jax.nn.sigmoid(gate)
            dgate_ref[:, cols] = (dff * up * sg * (1.0 + gate * (1.0 - sg))).astype(BF16)
            dup_ref[:, cols] = (dff * gate * sg).astype(BF16)

    ffw = pl.BlockSpec((tm, D_FF), lambda i: (i, 0))
    ff_shape = _array((T, D_FF), BF16)
    return _pallas(
        body, name="ffn_bwd_gates", grid=(T // tm,), out_shape=[ff_shape, ff_shape],
        in_specs=[pl.BlockSpec((tm, D_MODEL), lambda i: (i, 0)), _resident((D_FF, D_MODEL)), ffw, ffw],
        out_specs=[ffw, ffw], compiler_params=_params(("parallel",), 48),
    )(dx2b, w_down, gate, up)


def _ffn_bwd_in(dgate, dup, wg_t, wu_t, x1, g2, dx2, tm):
    T = dx2.shape[0]

    def body(dgate_ref, dup_ref, wg_ref, wu_ref, x1_ref, g_ref, dx_ref, dx1_ref, dx1b_ref, dg_ref):
        @pl.when(pl.program_id(0) == 0)
        def _():
            dg_ref[...] = jnp.zeros_like(dg_ref)

        dh = _nn(dgate_ref[...], wg_ref[...]) + _nn(dup_ref[...], wu_ref[...])
        dx1 = _rmsnorm_bwd(dh, x1_ref[...], g_ref[...], dx_ref[...], dg_ref)
        dx1_ref[...] = dx1
        dx1b_ref[...] = dx1.astype(BF16)

    wide = pl.BlockSpec((tm, D_MODEL), lambda i: (i, 0))
    ffw = pl.BlockSpec((tm, D_FF), lambda i: (i, 0))
    vec = pl.BlockSpec((1, D_MODEL), lambda i: (0, 0))
    w = _resident((D_FF, D_MODEL))
    return _pallas(
        body, name="ffn_bwd_in", grid=(T // tm,),
        out_shape=[_array((T, D_MODEL), F32), _array((T, D_MODEL), BF16), _array((1, D_MODEL), F32)],
        in_specs=[ffw, ffw, w, w, wide, vec, wide], out_specs=[wide, wide, vec],
        compiler_params=_params(("arbitrary",), 48),
    )(dgate, dup, wg_t, wu_t, x1, g2, dx2)


def _mm_tn(a, b, *, tmm, tk, name, into=None, row_block0=0, rows_total=None):
    T, M = a.shape
    N = b.shape[1]
    rows_total = M if rows_total is None else rows_total

    def body(*refs):
        a_ref, b_ref, o_ref = refs[0], refs[1], refs[-1]

        @pl.when(pl.program_id(1) == 0)
        def _():
            o_ref[...] = jnp.zeros_like(o_ref)

        o_ref[...] += _tn(a_ref[...], b_ref[...])

    ins = [a, b] + ([] if into is None else [into])
    specs = [pl.BlockSpec((tk, tmm), lambda i, k: (k, i)), pl.BlockSpec((tk, N), lambda i, k: (k, 0))]
    return _pallas(
        body, name=name, grid=(M // tmm, T // tk),
        out_shape=_array((rows_total, N), F32),
        in_specs=specs + ([] if into is None else [ANY]),
        out_specs=pl.BlockSpec((tmm, N), lambda i, k: (row_block0 + i, 0)),
        input_output_aliases={} if into is None else {2: 0},
        compiler_params=_params(("parallel", "arbitrary"), 48),
    )(*ins)


def _in_bwd(dqkv, duv, dgates, win_t, x, g1, dx1, tm, carried=None):
    T = x.shape[0]
    uv0, gates0 = QKV_COLS, QKV_COLS + 2 * SGU_W

    def body(dq_ref, du_ref, dgt_ref, w_ref, x_ref, g_ref, d_ref, dx_ref, dg_ref):
        @pl.when(pl.program_id(0) == 0)
        def _():
            dg_ref[...] = jnp.zeros_like(dg_ref)

        dh = (_nn(dq_ref[...], w_ref[:uv0, :]) + _nn(du_ref[...], w_ref[uv0:gates0, :])
              + _nn(dgt_ref[...], w_ref[gates0:, :]))
        dx_ref[...] = _rmsnorm_bwd(dh, x_ref[...], g_ref[...], d_ref[...], dg_ref)

    def cols(n):
        return pl.BlockSpec((tm, n), lambda i: (i, 0))

    wide = cols(D_MODEL)
    vec = pl.BlockSpec((1, D_MODEL), lambda i: (0, 0))
    return _call(
        body, name="in_bwd", grid=(T // tm,),
        out_shape=[_array((T, D_MODEL), F32), _array((1, D_MODEL), F32)],
        in_specs=[cols(QKV_COLS), cols(2 * SGU_W), cols(2 * D_MODEL), _resident((IN_COLS, D_MODEL)), wide, vec, wide],
        out_specs=[wide, vec], vmem_mib=56, args=(dqkv, duv, dgates, win_t, x, g1, dx1), carried=carried)


def _merge_bwd(dx1b, w_out, pa, ps, uvg, tm, carried=None):
    T = dx1b.shape[0]

    def body(dx_ref, w_ref, pa_ref, ps_ref, ga_ref, gb_ref, dpa_ref, dps_ref, dg_ref):
        dm = _nt(dx_ref[...], w_ref[...])
        ga, gb = jax.nn.sigmoid(ga_ref[...]), jax.nn.sigmoid(gb_ref[...])
        dpa_ref[...] = (dm * ga).astype(BF16)
        dps_ref[...] = (dm * gb).astype(BF16)
        dg_ref[:, :D_MODEL] = (dm * pa_ref[...] * ga * (1.0 - ga)).astype(BF16)
        dg_ref[:, D_MODEL:] = (dm * ps_ref[...] * gb * (1.0 - gb)).astype(BF16)

    wide = pl.BlockSpec((tm, D_MODEL), lambda i: (i, 0))
    res = _array((T, D_MODEL), BF16)
    return _call(
        body, name="merge_bwd", grid=(T // tm,),
        out_shape=[res, res, _array((T, 2 * D_MODEL), BF16)],
        in_specs=[wide, _resident((D_MODEL, D_MODEL)), wide, wide,
                  pl.BlockSpec((tm, D_MODEL), lambda i: (i, 1)), pl.BlockSpec((tm, D_MODEL), lambda i: (i, 2))],
        out_specs=[wide, wide, pl.BlockSpec((tm, 2 * D_MODEL), lambda i: (i, 0))],
        vmem_mib=48, args=(dx1b, w_out, pa, ps, uvg, uvg), carried=carried)


def _proj_bwd(dpa, dps, wpa_t, wps_t, tm):
    T = dpa.shape[0]

    def body(dpa_ref, dps_ref, wpa_ref, wps_ref, dattn_ref, dsgu_ref):
        dattn_ref[...] = _nn(dpa_ref[...], wpa_ref[...])
        dsgu_ref[...] = _nn(dps_ref[...], wps_ref[...])

    wide = pl.BlockSpec((tm, D_MODEL), lambda i: (i, 0))
    half = pl.BlockSpec((tm, ATTN_W), lambda i: (i, 0))
    w = pl.BlockSpec((D_MODEL, ATTN_W), lambda i: (0, 0))
    res = _array((T, ATTN_W), F32)
    return _pallas(
        body, name="proj_bwd", grid=(T // tm,), out_shape=[res, res], in_specs=[wide, wide, w, w],
        out_specs=[half, half], compiler_params=_params(("parallel",), 32),
    )(dpa, dps, wpa_t, wps_t)


def _sgu_bwd(uvg, dsgu, ln_g, ln_b, w_s, b_t, tm, carried=None):
    T = uvg.shape[0]
    nsteps = T // tm

    def body(uv_ref, ds_ref, g_ref, b_ref, w_ref, bt_ref, duv_ref, dw_ref, dbt_ref, dg_ref, db_ref,
             wc_ref, wct_ref, bias_ref, dbias_ref):
        step = pl.program_id(0)
        head0, head1 = _head_lanes()

        @pl.when(step == 0)
        def _():
            causal = _causal()
            for g in range(8):
                wc = jnp.where(causal, w_ref[g], 0.0)
                wc_ref[g] = wc.astype(BF16)
                wct_ref[g] = wc.T.astype(BF16)
            bias_ref[...] = _bias_lanes(bt_ref[...])
            dbias_ref[...] = jnp.zeros_like(dbias_ref)
            dw_ref[...] = jnp.zeros_like(dw_ref)
            dg_ref[...] = jnp.zeros_like(dg_ref)
            db_ref[...] = jnp.zeros_like(db_ref)

        uv = uv_ref[...]
        ln_gain = g_ref[...]
        u, xh, rstd, vn = _sgu_normalise(uv, ln_gain, b_ref[...])
        vb = vn.astype(BF16)
        mixed = _sgu_mix(wc_ref, vb, head0) + jnp.tile(bias_ref[...], (tm // BLK, 1))
        dout = ds_ref[...]
        du = dout * mixed
        dmixed = dout * u
        dmb = dmixed.astype(BF16)
        dvn_chunks = []
        for ch in range(tm // BLK):
            rows = slice(ch * BLK, (ch + 1) * BLK)
            dbias_ref[...] += dmixed[rows]
            pairs = []
            for p in range(SGU_W // HEAD_PAIR):
                lanes = slice(p * HEAD_PAIR, (p + 1) * HEAD_PAIR)
                dm_pair, v_pair = dmb[rows, lanes], vb[rows, lanes]
                acc = jnp.zeros((BLK, HEAD_PAIR), F32)
                for hh, half in enumerate((head0, head1)):
                    dm_h = jnp.where(half, dm_pair, jnp.zeros_like(dm_pair))
                    dw_ref[2 * p + hh] += _nt(dm_h, v_pair)
                    acc += _nn(wct_ref[2 * p + hh], dm_h)
                pairs.append(acc)
            dvn_chunks.append(jnp.concatenate(pairs, axis=1))
        dvn = jnp.concatenate(dvn_chunks, axis=0)
        dg_ref[...] += jnp.sum(dvn * xh, axis=0, keepdims=True)
        db_ref[...] += jnp.sum(dvn, axis=0, keepdims=True)
        dxh = dvn * ln_gain
        dv = rstd * (dxh - jnp.mean(dxh, axis=-1, keepdims=True) - xh * jnp.mean(dxh * xh, axis=-1, keepdims=True))
        dgelu = _gelu_grad(uv)
        duv_ref[:, :SGU_W] = (du * dgelu[:, :SGU_W]).astype(BF16)
        duv_ref[:, SGU_W:] = (dv * dgelu[:, SGU_W:]).astype(BF16)

        @pl.when(step == nsteps - 1)
        def _():
            causal = _causal()
            for g in range(8):
                dw_ref[g] = jnp.where(causal, dw_ref[g], 0.0)
            grp = lax.broadcasted_iota(jnp.int32, (1, SGU_W), 1) // 64
            col = lax.broadcasted_iota(jnp.int32, (1, 8), 1)
            dbias = dbias_ref[...]
            out = jnp.zeros((BLK, 8), F32)
            for g in range(8):
                s = jnp.sum(jnp.where(grp == g, dbias, 0.0), axis=1, keepdims=True)
                out = jnp.where(col == g, s, out)
            dbt_ref[...] = out

    vec = pl.BlockSpec((1, SGU_W), lambda i: (0, 0))
    w3 = pl.BlockSpec((8, BLK, BLK), lambda i: (0, 0, 0))
    bt = pl.BlockSpec((BLK, 8), lambda i: (0, 0))
    return _call(
        body, name="sgu_bwd", grid=(nsteps,),
        out_shape=[_array((T, 2 * SGU_W), BF16), _array((8, BLK, BLK), F32),
                   _array((BLK, 8), F32), _array((1, SGU_W), F32),
                   _array((1, SGU_W), F32)],
        in_specs=[pl.BlockSpec((tm, 2 * SGU_W), lambda i: (i, 0)), pl.BlockSpec((tm, SGU_W), lambda i: (i, 0)),
                  vec, vec, w3, bt],
        out_specs=[pl.BlockSpec((tm, 2 * SGU_W), lambda i: (i, 0)), w3, bt, vec, vec],
        scratch_shapes=[pltpu.VMEM((8, BLK, BLK), BF16), pltpu.VMEM((8, BLK, BLK), BF16),
                        pltpu.VMEM((BLK, SGU_W), F32), pltpu.VMEM((BLK, SGU_W), F32)],
        vmem_mib=40, args=(uvg, dsgu, ln_g, ln_b, w_s, b_t), carried=carried)


D_LANE0 = 8


def _attn_bwd_prepare(dattn, attn, lse, tm):
    T = dattn.shape[0]

    def body(da_ref, at_ref, lse_ref, *outs):
        scr_da, scr_st = outs[-2:]
        outs = outs[:-2]
        head0, _ = _head_lanes()
        lane = lax.broadcasted_iota(jnp.int32, (1, HEAD_PAIR), 1)
        stats = lse_ref[...]
        for s in range(N_SLABS):
            lanes = slice(s * HEAD_PAIR, (s + 1) * HEAD_PAIR)
            da = da_ref[:, lanes]
            pp = da * at_ref[:, lanes]
            d0 = jnp.sum(jnp.where(head0, pp, 0.0), axis=1, keepdims=True)
            d1 = jnp.sum(jnp.where(head0, 0.0, pp), axis=1, keepdims=True)
            stats = jnp.where(lane == D_LANE0 + 2 * s, d0, jnp.where(lane == D_LANE0 + 2 * s + 1, d1, stats))
            scr_da[s] = da
            outs[0][0, :, lanes] = da.astype(BF16)
        scr_st[0] = stats
        outs[1][0] = stats
        for g, r in enumerate(DILATIONS):
            if r > 1:
                _rows_by_residue(outs[2 * g], scr_da, r, tm // r, BF16)
                _rows_by_residue(outs[2 * g + 1], scr_st, r, tm // r, F32, n=1)

    tok = pl.BlockSpec((tm, ATTN_W), lambda i: (i, 0))
    shapes, specs = [], []
    for r in DILATIONS:
        shapes += [_array((r, T // r, ATTN_W), BF16), _array((r, T // r, HEAD_PAIR), F32)]
        specs += [pl.BlockSpec((r, tm // r, ATTN_W), lambda i: (0, i, 0)),
                  pl.BlockSpec((r, tm // r, HEAD_PAIR), lambda i: (0, i, 0))]
    return _pallas(
        body, name="attn_bwd_prepare", grid=(T // tm,), out_shape=shapes,
        in_specs=[tok, tok, pl.BlockSpec((tm, HEAD_PAIR), lambda i: (i, 0))],
        out_specs=specs, scratch_shapes=[_slab_scratch(tm), _slab_scratch(tm, 1)],
        compiler_params=_params(("parallel",), 40),
    )(dattn, attn, lse)


def _attn_bwd(qkv, dattn, stats, group, carried=None):
    r, L, _ = qkv.shape
    nb = L // BLK

    def body(q_ref, kp_ref, kc_ref, vp_ref, vc_ref, da_ref, st_ref, dq_ref, dk_ref, dv_ref, carry_k, carry_v):
        n = pl.program_id(1)

        @pl.when(n == 0)
        def _():
            carry_k[...] = jnp.zeros_like(carry_k)
            carry_v[...] = jnp.zeros_like(carry_v)

        @pl.when(n < nb)
        def _():
            valid = _band_mask(n)
            head0, head1 = _head_lanes()
            st = st_ref[...]
            for p in range(N_SLABS):
                sl = slice(p * HEAD_PAIR, (p + 1) * HEAD_PAIR)
                k2 = jnp.concatenate([kp_ref[:, sl], kc_ref[:, sl]], axis=0)
                v2 = jnp.concatenate([vp_ref[:, sl], vc_ref[:, sl]], axis=0)
                qs = _stack_heads(q_ref[:, sl], head0, head1)
                das = _stack_heads(da_ref[:, sl], head0, head1)
                prob = jnp.where(valid, jnp.exp(_nt(qs, k2) - _per_head(st, 2 * p)), 0.0)
                ds = (prob * (_nt(das, v2) - _per_head(st, D_LANE0 + 2 * p))).astype(BF16)
                dk2 = _tn(ds, qs)
                dv2 = _tn(prob.astype(BF16), das)
                dq_ref[:, sl] = _unstack_heads(_nn(ds, k2), head0)
                dk_ref[:, sl] = carry_k[:, sl] + dk2[:BLK]
                dv_ref[:, sl] = (carry_v[:, sl] + dv2[:BLK]).astype(BF16)
                carry_k[:, sl] = dk2[BLK:]
                carry_v[:, sl] = dv2[BLK:]

        @pl.when(n == nb)
        def _():
            dk_ref[...] = carry_k[...]
            dv_ref[...] = carry_v[...].astype(BF16)

    def cur(kind, width=ATTN_W):
        return pl.BlockSpec((None, BLK, width), lambda rho, n: (rho, jnp.minimum(n, nb - 1), kind))

    def prev(kind):
        return pl.BlockSpec((None, BLK, ATTN_W), lambda rho, n: (rho, jnp.clip(n - 1, 0, nb - 1), kind))

    res = _array((r, L, ATTN_W), F32)
    return _call(
        body, name=f"attn_bwd_g{group}", grid=(r, nb + 1), out_shape=[res, res, _array((r, L, ATTN_W), BF16)],
        in_specs=[cur(0), prev(1), cur(1), prev(2), cur(2), cur(0), cur(0, HEAD_PAIR)],
        out_specs=[cur(0), prev(0), prev(0)],
        scratch_shapes=[pltpu.VMEM((BLK, ATTN_W), F32), pltpu.VMEM((BLK, ATTN_W), F32)],
        vmem_mib=32, args=(qkv, qkv, qkv, qkv, qkv, dattn, stats), carried=carried)


def _dqkv_token_order(dqkv_groups, tables, tm):
    T = tables[0].shape[0]

    def body(*refs):
        ins = refs[:9]
        cos_ref, s1_ref, s2_ref, o_ref, scr = refs[9:]
        cos, s1, s2 = cos_ref[...], s1_ref[...], s2_ref[...]
        for g, r in enumerate(DILATIONS):
            for kind in range(3):
                src = ins[3 * g + kind]
                if r > 1:
                    _rows_by_token(scr, src, r, tm // r)
                for s in range(N_SLABS):
                    val = scr[s] if r > 1 else src[0, :, s * HEAD_PAIR:(s + 1) * HEAD_PAIR]
                    if kind < 2:
                        val = _rope_transposed(val, cos, s1, s2)
                    if kind == 0:
                        val = val * 0.125
                    at = (3 * kind + g) * ATTN_W + s * HEAD_PAIR
                    o_ref[:, at:at + HEAD_PAIR] = val.astype(BF16)

    specs = []
    for r in DILATIONS:
        specs += [pl.BlockSpec((r, tm // r, ATTN_W), lambda i: (0, i, 0))] * 3
    row = pl.BlockSpec((tm, HEAD_PAIR), lambda i: (i, 0))
    flat = [a for grp in dqkv_groups for a in grp]
    return _pallas(
        body, name="dqkv_token_order", grid=(T // tm,), out_shape=_array((T, QKV_COLS), BF16),
        in_specs=specs + [row] * 3, out_specs=pl.BlockSpec((tm, QKV_COLS), lambda i: (i, 0)),
        scratch_shapes=[_slab_scratch(tm)], compiler_params=_params(("parallel",), 48),
    )(*flat, *tables)


def _row_tile(rows):
    for cand in (320, 256, 176, 128):
        if rows % cand == 0:
            return cand
    return rows


def _pair_sum(grad4, recv, chip, name):
    _, _, rows, cols = grad4.shape
    tr = _row_tile(rows)

    def body(ids_ref, g_ref, r_ref, gown_ref, rown_ref, sum_ref, own_ref):
        sum_ref[...] = (g_ref[...] + r_ref[...]).astype(BF16)

        @pl.when(pl.program_id(1) == 0)
        def _():
            own_ref[...] = gown_ref[...] + rown_ref[...]

    grid_spec = pltpu.PrefetchScalarGridSpec(
        num_scalar_prefetch=1, grid=(rows // tr, 4),
        in_specs=[pl.BlockSpec((None, None, tr, cols), lambda i, q, ids: (q, ids[1], i, 0)),
                  pl.BlockSpec((None, tr, cols), lambda i, q, ids: (q, i, 0)),
                  pl.BlockSpec((None, None, tr, cols), lambda i, q, ids: (ids[0], ids[1], i, 0)),
                  pl.BlockSpec((None, tr, cols), lambda i, q, ids: (ids[0], i, 0))],
        out_specs=[pl.BlockSpec((None, tr, cols), lambda i, q, ids: (q, i, 0)),
                   pl.BlockSpec((tr, cols), lambda i, q, ids: (i, 0))])
    return _pallas(
        body, name=name, grid_spec=grid_spec,
        out_shape=[_array((4, rows, cols), BF16), _array((rows, cols), F32)],
        compiler_params=_params(("arbitrary", "arbitrary"), 32),
    )(chip, grad4, recv, grad4, recv)


def _chip_sum(own, others, name):
    rows, cols = own.shape
    tr = _row_tile(rows)

    def body(own_ref, oth_ref, o_ref):
        total = own_ref[...]
        for j in range(3):
            total = total + oth_ref[j].astype(F32)
        o_ref[...] = total

    blk = pl.BlockSpec((tr, cols), lambda i: (i, 0))
    return _pallas(
        body, name=name, grid=(rows // tr,), out_shape=_array((rows, cols), F32),
        in_specs=[blk, pl.BlockSpec((3, tr, cols), lambda i: (0, i, 0))], out_specs=blk,
        compiler_params=_params(("parallel",), 32),
    )(own, others)


def _adam_math(w, g, m, v):
    m = ADAM_B1 * m + (1.0 - ADAM_B1) * g
    v = ADAM_B2 * v + (1.0 - ADAM_B2) * (g * g)
    m_hat = m / (1.0 - ADAM_B1 ** ADAM_STEP)
    v_hat = v / (1.0 - ADAM_B2 ** ADAM_STEP)
    delta = -ADAM_LR * (m_hat / (jnp.sqrt(v_hat) + ADAM_EPS) + ADAM_WD * w)
    return delta, m, v


def _adamw(w, g, m, v, name):
    rows, cols = w.shape
    tr = _row_tile(rows)

    def body(w_ref, g_ref, m_ref, v_ref, d_ref, nm_ref, nv_ref):
        d_ref[...], nm_ref[...], nv_ref[...] = _adam_math(w_ref[...], g_ref[...], m_ref[...], v_ref[...])

    blk = pl.BlockSpec((tr, cols), lambda i: (i, 0))
    res = _array((rows, cols), F32)
    return _pallas(
        body, name=name, grid=(rows // tr,), out_shape=[res, res, res], in_specs=[blk] * 4, out_specs=[blk] * 3,
        compiler_params=_params(("parallel",), 32),
    )(w, g, m, v)


def _small_update(late_parts, early_parts, w, m, v, loss_rows):
    rows = w.shape[0]

    def total(p_ref):
        n = p_ref.shape[0] // N_DEV
        acc = p_ref[0:n, :]
        for dev in range(1, N_DEV):
            acc = acc + p_ref[dev * n:(dev + 1) * n, :]
        return acc

    def body(late_ref, early_ref, w_ref, m_ref, v_ref, g_ref, d_ref, nm_ref, nv_ref, loss_ref):
        g = jnp.concatenate([total(late_ref), total(early_ref)], axis=0)
        g_ref[...] = g
        d_ref[...], nm_ref[...], nv_ref[...] = _adam_math(w_ref[...], g, m_ref[...], v_ref[...])
        loss_ref[...] = jnp.sum(jnp.sum(g[rows - loss_rows:, :], axis=1, keepdims=True), axis=0, keepdims=True)

    res = jax.ShapeDtypeStruct((rows, HEAD_PAIR), F32)
    return _pallas(
        body, name="small_update", out_shape=[res, res, res, res, jax.ShapeDtypeStruct((1, 1), F32)],
        compiler_params=pltpu.CompilerParams(vmem_limit_bytes=32 * MIB),
    )(late_parts, early_parts, w, m, v)


def kernel(x, positions, norm1_g, w_in, sgu_ln_g, sgu_ln_b, w_spatial, b_spatial, w_proj_attn, w_proj_sgu, w_out, norm2_g, w_ffn_gate, w_ffn_up, w_ffn_down, final_g, loss_target, m_norm1_g, m_w_in, m_sgu_ln_g, m_sgu_ln_b, m_w_spatial, m_b_spatial, m_w_proj_attn, m_w_proj_sgu, m_w_out, m_norm2_g, m_w_ffn_gate, m_w_ffn_up, m_w_ffn_down, m_final_g, v_norm1_g, v_w_in, v_sgu_ln_g, v_sgu_ln_b, v_w_spatial, v_b_spatial, v_w_proj_attn, v_w_proj_sgu, v_w_out, v_norm2_g, v_w_ffn_gate, v_w_ffn_up, v_w_ffn_down, v_final_g):
    T = x.shape[1]
    tm = 512
    xt = x[0]
    target = loss_target[0]
    chip = jnp.stack([2 * lax.axis_index("x") + lax.axis_index("y"), lax.axis_index("c")]).astype(jnp.int32)

    def bf16_rows(w, transpose):
        return (jnp.transpose(w[0]) if transpose else w[0]).astype(BF16)

    (win_t,) = _all_gather([bf16_rows(w_in, True)]).run_alone("w_in_all_gather")
    ffn_in = _all_gather([bf16_rows(w_ffn_gate, True), bf16_rows(w_ffn_up, True)])
    mixers = _all_gather([bf16_rows(w_out, False), bf16_rows(w_proj_attn, True), bf16_rows(w_proj_sgu, True)])
    ffn_out = _all_gather([bf16_rows(w_ffn_down, False)])

    inv_freq = ROPE_THETA ** (-jnp.arange(0, 2 * ROPE_HALF, 2, dtype=F32) / (2 * ROPE_HALF))
    inv_freq_row = jnp.tile(jnp.concatenate([inv_freq, inv_freq, jnp.zeros((48,), F32)]), 2).reshape(1, HEAD_PAIR)
    tables = _rope_tables(positions.reshape(T, 1), inv_freq_row, tm)
    b_t = jnp.transpose(b_spatial[0])

    h = _rmsnorm_fwd(xt, norm1_g, "norm1_fwd", tm)
    qkv, (wg_t, wu_t) = _qkv_proj(h, win_t, tables, tm, carried=ffn_in)
    (uvg,), (wout, wpa_t, wps_t) = _uv_gate_proj(h, win_t, tm, carried=mixers)
    fwd0, (wd,) = _attn_fwd(qkv[0], 0, carried=ffn_out)
    fwd = [fwd0, _attn_fwd(qkv[1], 1)[0], _attn_fwd(qkv[2], 2)[0]]
    attn, lse = _attn_combine([f[0] for f in fwd], [f[1] for f in fwd], tm)
    sgu = _sgu_fwd(uvg, sgu_ln_g, sgu_ln_b, w_spatial[0], b_t, tm)
    pa, ps, merged = _merge_fwd(attn, sgu, wpa_t, wps_t, uvg, tm)
    x1, h2 = _out_residual(merged, wout, xt, norm2_g, tm)
    gate, up, ff = _ffn_fwd(h2, wg_t, wu_t, tm)
    dx2, dx2b, loss_cols, d_final_g = _down_loss(ff, wd, x1, final_g.reshape(1, D_MODEL), target, tm)

    dgate, dup = _ffn_bwd_gates(dx2b, wd, gate, up, tm)
    dx1, dx1b, d_norm2 = _ffn_bwd_in(dgate, dup, wg_t, wu_t, x1, norm2_g, dx2, tm)
    tk = min(1024, T)
    d_wd = _mm_tn(ff, dx2b, tmm=FF_TILE, tk=tk, name="grad_w_ffn_down")
    d_wg_t = _mm_tn(dgate, h2, tmm=FF_TILE, tk=tk, name="grad_w_ffn_gate")
    d_wu_t = _mm_tn(dup, h2, tmm=FF_TILE, tk=tk, name="grad_w_ffn_up")

    def by_owner(grads):
        return [g.reshape(4, 2, g.shape[0] // N_DEV, g.shape[1]) for g in grads]

    def pair_sums(grads4, from_sibling, names):
        both = [_pair_sum(g4, rv, chip, "grad_pair_sum_" + nm) for g4, rv, nm in zip(grads4, from_sibling, names)]
        return [b[0] for b in both], [b[1] for b in both]

    ffn_names = ["w_ffn_gate", "w_ffn_up", "w_ffn_down"]
    ffn4 = by_owner([d_wg_t, d_wu_t, d_wd])
    (dpa, dps, dgates), ffn_sib = _merge_bwd(dx1b, wout, pa, ps, uvg, tm, carried=_pair_exchange(ffn4))
    ffn_sums, ffn_own = pair_sums(ffn4, ffn_sib, ffn_names)

    d_wout = _mm_tn(merged, dx1b, tmm=D_MODEL, tk=tk, name="grad_w_out")
    dattn, dsgu = _proj_bwd(dpa, dps, wpa_t, wps_t, tm)
    d_wpa_t = _mm_tn(dpa, attn.astype(BF16), tmm=D_MODEL, tk=tk, name="grad_w_proj_attn")
    d_wps_t = _mm_tn(dps, sgu, tmm=D_MODEL, tk=tk, name="grad_w_proj_sgu")
    mid_names = ["w_proj_attn", "w_proj_sgu", "w_out"]
    mid4 = by_owner([d_wpa_t, d_wps_t, d_wout])
    (duv, d_ws, d_bs_t, d_ln_g, d_ln_b), mid_sib = _sgu_bwd(uvg, dsgu, sgu_ln_g, sgu_ln_b, w_spatial[0], b_t, tm,
                                                           carried=_pair_exchange(mid4))
    mid_sums, mid_own = pair_sums(mid4, mid_sib, mid_names)

    prep = _attn_bwd_prepare(dattn, attn, lse, tm)
    dqkv0, ffn_far = _attn_bwd(qkv[0], prep[0], prep[1], 0, carried=_chip_exchange(ffn_sums))
    dqkv1, mid_far = _attn_bwd(qkv[1], prep[2], prep[3], 1, carried=_chip_exchange(mid_sums))
    def flat(parts):
        return jnp.concatenate([p.reshape(-1) for p in parts]).reshape(-1, HEAD_PAIR)

    early_part = flat([d_ln_g, d_ln_b, d_ws, jnp.transpose(d_bs_t), d_norm2, d_final_g, loss_cols])
    dqkv2, (early_parts,) = _attn_bwd(qkv[2], prep[4], prep[5], 2, carried=_all_gather([early_part]))
    dqkv = _dqkv_token_order([dqkv0, dqkv1, dqkv2], tables, tm)
    d_win_t = _mm_tn(dqkv, h, tmm=1536, tk=tk, name="grad_w_in_qkv", rows_total=IN_COLS)
    d_win_t = _mm_tn(duv, h, tmm=512, tk=tk, name="grad_w_in_uv", into=d_win_t, row_block0=9, rows_total=IN_COLS)
    d_win_t = _mm_tn(dgates, h, tmm=512, tk=tk, name="grad_w_in_gates", into=d_win_t, row_block0=11, rows_total=IN_COLS)
    in4 = by_owner([d_win_t])
    in_sib = _pair_exchange(in4).run_alone("w_in_grad_pair_exchange")
    in_sums, in_own = pair_sums(in4, in_sib, ["w_in"])
    (dx, d_norm1), in_far = _in_bwd(dqkv, duv, dgates, win_t, xt, norm1_g, dx1, tm, carried=_chip_exchange(in_sums))

    names = ["w_in"] + mid_names + ffn_names
    reduced = [_chip_sum(o, f, "grad_total_" + nm)
               for o, f, nm in zip(in_own + mid_own + ffn_own, in_far + mid_far + ffn_far, names)]
    transposed = (True, True, True, False, True, True, False)
    g_big = [jnp.transpose(r) if t else r for r, t in zip(reduced, transposed)]

    small_w = [norm1_g, sgu_ln_g, sgu_ln_b, w_spatial, b_spatial, norm2_g, final_g]
    small_m = [m_norm1_g, m_sgu_ln_g, m_sgu_ln_b, m_w_spatial, m_b_spatial, m_norm2_g, m_final_g]
    small_v = [v_norm1_g, v_sgu_ln_g, v_sgu_ln_b, v_w_spatial, v_b_spatial, v_norm2_g, v_final_g]
    zeros = jnp.zeros((D_MODEL,), F32)
    (late_parts,) = _all_gather([flat([d_norm1])]).run_alone("norm1_grad_all_gather", vmem=True)
    g_s, d_s, nm_s, nv_s, loss = _small_update(late_parts, early_parts, flat(small_w + [zeros]), flat(small_m + [zeros]),
                                               flat(small_v + [zeros]), D_MODEL // HEAD_PAIR)

    def unflat(vec):
        vec = vec.reshape(-1)
        out, at = [], 0
        for wgt in small_w:
            out.append(vec[at:at + wgt.size].reshape(wgt.shape))
            at += wgt.size
        return out

    small = [unflat(a) for a in (g_s, d_s, nm_s, nv_s)]

    big_w = [w_in, w_proj_attn, w_proj_sgu, w_out, w_ffn_gate, w_ffn_up, w_ffn_down]
    big_m = [m_w_in, m_w_proj_attn, m_w_proj_sgu, m_w_out, m_w_ffn_gate, m_w_ffn_up, m_w_ffn_down]
    big_v = [v_w_in, v_w_proj_attn, v_w_proj_sgu, v_w_out, v_w_ffn_gate, v_w_ffn_up, v_w_ffn_down]
    big_out = []
    for wgt, g, mm, vv, nm in zip(big_w, g_big, big_m, big_v, names):
        d, nm_, nv_ = _adamw(wgt[0], g, mm[0], vv[0], "adamw_" + nm)
        big_out.append([a[None] for a in (g, d, nm_, nv_)])

    small_at = {0: 0, 2: 1, 3: 2, 4: 3, 5: 4, 9: 5, 13: 6}
    big_at = {1: 0, 6: 1, 7: 2, 8: 3, 10: 4, 11: 5, 12: 6}
    outs = [loss[0, 0], dx[None]]
    for kind in range(4):
        for idx in range(14):
            outs.append(small[kind][small_at[idx]] if idx in small_at else big_out[big_at[idx]][kind])
    return tuple(outs)
```

```python
import functools
import math

import jax
import jax.numpy as jnp
from jax import lax
from jax.experimental import pallas as pl
from jax.experimental.pallas import tpu as pltpu

F32 = jnp.float32
BF16 = jnp.bfloat16

D_MODEL = 1024
HEAD_PAIR = 128
ATTN_W = 512
DILATIONS = (1, 4, 16)
BLK = 128
ROPE_HALF = 8
ROPE_THETA = 500000.0
SGU_W = 512
QKV_COLS = 4608
IN_COLS = 7680
D_FF = 2816
FF_TILE = 1408
EPS = 1e-6
N_DEV = 8
MASKED = -1e30

ADAM_LR = 0.001
ADAM_B1 = 0.9
ADAM_B2 = 0.999
ADAM_EPS = 1e-08
ADAM_WD = 0.01
ADAM_STEP = 10

MIB = 1024 * 1024
MESH = pl.DeviceIdType.MESH
ANY = pl.BlockSpec(memory_space=pl.ANY)


def _array(shape, dtype):
    return pltpu.HBM(tuple(shape), dtype)


PIN_BYTES = 4 * MIB


def _pin(x):
    if x.size * x.dtype.itemsize < PIN_BYTES:
        return x
    return pltpu.with_memory_space_constraint(x, pltpu.HBM)


def _pallas(body, **kwargs):
    call = pl.pallas_call(body, **kwargs)
    return lambda *args: call(*[_pin(a) for a in args])


def _params(sem, vmem_mib):
    return pltpu.CompilerParams(dimension_semantics=sem, vmem_limit_bytes=vmem_mib * MIB)


def _nt(a, b):
    return lax.dot_general(a, b, (((1,), (1,)), ((), ())), preferred_element_type=F32)


def _nn(a, b):
    return lax.dot_general(a, b, (((1,), (0,)), ((), ())), preferred_element_type=F32)


def _tn(a, b):
    return lax.dot_general(a, b, (((0,), (0,)), ((), ())), preferred_element_type=F32)


class _Exchange:
    def __init__(self, arrays, out_shapes, sem_shapes, phases):
        self.arrays, self.out_shapes, self.sem_shapes, self.phases = list(arrays), out_shapes, sem_shapes, phases

    def run_alone(self, name, vmem=False):
        n_in, n_out = len(self.arrays), len(self.out_shapes)

        def body(*refs):
            start, middle, finish = self.phases(refs[:n_in], refs[n_in:n_in + n_out], refs[n_in + n_out:])
            start()
            middle()
            finish()

        spec = pl.BlockSpec(memory_space=pltpu.VMEM) if vmem else ANY
        shapes = [jax.ShapeDtypeStruct(s.shape, s.dtype) for s in self.out_shapes] if vmem else self.out_shapes
        return (pl.pallas_call if vmem else _pallas)(
            body, name=name, out_shape=shapes, in_specs=[spec] * n_in, out_specs=[spec] * n_out,
            scratch_shapes=self.sem_shapes, compiler_params=pltpu.CompilerParams(vmem_limit_bytes=32 * MIB),
        )(*self.arrays)


def _call(body, *, name, grid, in_specs, out_specs, out_shape, args, vmem_mib, scratch_shapes=(), carried=None):
    n_in, n_out, n_scr = len(in_specs), len(out_specs), len(scratch_shapes)
    sem = ("arbitrary",) * len(grid)
    if carried is None:
        outs = _pallas(
            body, name=name, grid=grid, in_specs=in_specs, out_specs=out_specs, out_shape=out_shape,
            scratch_shapes=list(scratch_shapes), compiler_params=_params(sem, vmem_mib))(*args)
        return list(outs), []
    c_in, c_out = len(carried.arrays), len(carried.out_shapes)
    total = math.prod(grid)

    def full(*refs):
        own_in, car_in = refs[:n_in], refs[n_in:n_in + c_in]
        at = n_in + c_in
        own_out, car_out = refs[at:at + n_out], refs[at + n_out:at + n_out + c_out]
        at += n_out + c_out
        own_scr, sems = refs[at:at + n_scr], refs[at + n_scr:]
        step = pl.program_id(0)
        for axis in range(1, len(grid)):
            step = step * grid[axis] + pl.program_id(axis)
        start, middle, finish = carried.phases(car_in, car_out, sems)
        pl.when(step == 0)(start)
        pl.when(step == (3 * total) // 4)(middle)
        body(*own_in, *own_out, *own_scr)
        pl.when(step == total - 1)(finish)

    outs = _pallas(
        full, name=name, grid=grid, in_specs=list(in_specs) + [ANY] * c_in,
        out_specs=list(out_specs) + [ANY] * c_out, out_shape=list(out_shape) + list(carried.out_shapes),
        scratch_shapes=list(scratch_shapes) + list(carried.sem_shapes),
        compiler_params=_params(sem, vmem_mib))(*args, *carried.arrays)
    return list(outs[:n_out]), list(outs[n_out:])


def _all_gather(shards):
    n = len(shards)

    def phases(ins, outs, sems):
        send_sems, recv_sems, local_sems = sems
        x, y, c = lax.axis_index("x"), lax.axis_index("y"), lax.axis_index("c")
        me, sibling = (x, y, c), (x, y, 1 - c)
        chips = [(1 - x, y), (x, 1 - y), (1 - x, 1 - y)]

        def rows(m, px, py, pc):
            r = ins[m].shape[0]
            return outs[m].at[pl.ds((4 * px + 2 * py + pc) * r, r), :]

        def copy(m, k, block, to, src=None):
            return pltpu.make_async_remote_copy(
                src_ref=rows(m, *block) if src is None else src, dst_ref=rows(m, *block),
                send_sem=send_sems.at[m, k], recv_sem=recv_sems.at[m, k],
                device_id=to, device_id_type=MESH)

        def mine(m):
            return pltpu.make_async_copy(ins[m], rows(m, *me), local_sems.at[m])

        def first(m):
            return [copy(m, 0, me, sibling, src=ins[m])] + [
                copy(m, 1 + j, me, (*chip, c), src=ins[m]) for j, chip in enumerate(chips)]

        def passed(m):
            return [copy(m, 4 + j, (*chip, c), sibling) for j, chip in enumerate(chips)]

        def start():
            for m in range(n):
                mine(m).start()
            for m in range(n):
                for cp in first(m):
                    cp.start()

        def middle():
            for m in range(n):
                for j, chip in enumerate(chips):
                    copy(m, 1 + j, (*chip, c), me).wait_recv()
                    passed(m)[j].start()

        def finish():
            for m in range(n):
                copy(m, 0, sibling, me).wait_recv()
                for j, chip in enumerate(chips):
                    copy(m, 4 + j, (*chip, 1 - c), me).wait_recv()
            for m in range(n):
                for cp in first(m) + passed(m):
                    cp.wait_send()
                mine(m).wait()

        return start, middle, finish

    return _Exchange(
        shards, [_array((N_DEV * s.shape[0], s.shape[1]), s.dtype) for s in shards],
        [pltpu.SemaphoreType.DMA((n, 7)), pltpu.SemaphoreType.DMA((n, 7)), pltpu.SemaphoreType.DMA((n,))], phases)


def _pair_exchange(grads):
    n = len(grads)

    def phases(ins, outs, sems):
        send_sems, recv_sems = sems
        x, y, c = lax.axis_index("x"), lax.axis_index("y"), lax.axis_index("c")

        def copy(m):
            return pltpu.make_async_remote_copy(
                src_ref=ins[m].at[:, 1 - c], dst_ref=outs[m], send_sem=send_sems.at[m], recv_sem=recv_sems.at[m],
                device_id=(x, y, 1 - c), device_id_type=MESH)

        def start():
            for m in range(n):
                copy(m).start()

        def finish():
            for m in range(n):
                copy(m).wait()

        return start, lambda: None, finish

    return _Exchange(grads, [_array((4,) + g.shape[2:], g.dtype) for g in grads],
                     [pltpu.SemaphoreType.DMA((n,)), pltpu.SemaphoreType.DMA((n,))], phases)


def _chip_exchange(pair_sums):
    n = len(pair_sums)

    def phases(ins, outs, sems):
        send_sems, recv_sems = sems
        x, y, c = lax.axis_index("x"), lax.axis_index("y"), lax.axis_index("c")
        chips = [(1 - x, y), (x, 1 - y), (1 - x, 1 - y)]

        def copies():
            return [pltpu.make_async_remote_copy(
                src_ref=ins[m].at[2 * px + py], dst_ref=outs[m].at[j],
                send_sem=send_sems.at[m, j], recv_sem=recv_sems.at[m, j],
                device_id=(px, py, c), device_id_type=MESH)
                for m in range(n) for j, (px, py) in enumerate(chips)]

        def start():
            for cp in copies():
                cp.start()

        def finish():
            for cp in copies():
                cp.wait_recv()
            for cp in copies():
                cp.wait_send()

        return start, lambda: None, finish

    return _Exchange(pair_sums, [_array((3,) + p.shape[1:], p.dtype) for p in pair_sums],
                     [pltpu.SemaphoreType.DMA((n, 3)), pltpu.SemaphoreType.DMA((n, 3))], phases)


N_SLABS = ATTN_W // HEAD_PAIR


def _slab_scratch(tm, n=N_SLABS):
    return pltpu.VMEM((n, tm, HEAD_PAIR), F32)


def _rows_by_residue(dst_ref, slab_ref, r, tr, dtype, n=N_SLABS):
    for rho in range(r):
        for s in range(n):
            dst_ref[rho, :, s * HEAD_PAIR:(s + 1) * HEAD_PAIR] = (
                slab_ref[s, pl.ds(rho, tr, stride=r), :].astype(dtype))


def _rows_by_token(slab_ref, src_ref, r, tr, n=N_SLABS):
    for rho in range(r):
        for s in range(n):
            slab_ref[s, pl.ds(rho, tr, stride=r), :] = (
                src_ref[rho, :, s * HEAD_PAIR:(s + 1) * HEAD_PAIR].astype(F32))


def _rope_tables(pos_col, inv_freq_row, tm):
    T = pos_col.shape[0]

    def body(pos_ref, invf_ref, cos_ref, s1_ref, s2_ref):
        ang = pos_ref[...].astype(F32) * invf_ref[...]
        lane = lax.broadcasted_iota(jnp.int32, (1, HEAD_PAIR), 1) % 64
        cs, sn = jnp.cos(ang), jnp.sin(ang)
        cos_ref[...] = jnp.where(lane < 2 * ROPE_HALF, cs, 1.0)
        s1_ref[...] = jnp.where(lane < ROPE_HALF, -sn, 0.0)
        s2_ref[...] = jnp.where((lane >= ROPE_HALF) & (lane < 2 * ROPE_HALF), sn, 0.0)

    tab = _array((T, HEAD_PAIR), F32)
    row = pl.BlockSpec((tm, HEAD_PAIR), lambda i: (i, 0))
    return _pallas(
        body, name="rope_tables", grid=(T // tm,), out_shape=[tab] * 3,
        in_specs=[pl.BlockSpec((tm, 1), lambda i: (i, 0)), pl.BlockSpec((1, HEAD_PAIR), lambda i: (0, 0))],
        out_specs=[row] * 3, compiler_params=_params(("parallel",), 16),
    )(pos_col, inv_freq_row)


def _rope(y, cos, s1, s2):
    w = y.shape[1]
    rep = w // HEAD_PAIR
    return (y * jnp.tile(cos, (1, rep)) + pltpu.roll(y, w - ROPE_HALF, 1) * jnp.tile(s1, (1, rep))
            + pltpu.roll(y, ROPE_HALF, 1) * jnp.tile(s2, (1, rep)))


def _rope_transposed(dy, cos, s1, s2):
    w = dy.shape[1]
    rep = w // HEAD_PAIR
    return (dy * jnp.tile(cos, (1, rep)) + pltpu.roll(dy * jnp.tile(s1, (1, rep)), ROPE_HALF, 1)
            + pltpu.roll(dy * jnp.tile(s2, (1, rep)), w - ROPE_HALF, 1))


def _rmsnorm_fwd(x, g, name, tm, carried=None):
    T = x.shape[0]

    def body(x_ref, g_ref, h_ref):
        xf = x_ref[...]
        r = lax.rsqrt(jnp.mean(xf * xf, axis=-1, keepdims=True) + EPS)
        h_ref[...] = (xf * r * g_ref[...]).astype(BF16)

    return _call(
        body, name=name, grid=(T // tm,), out_shape=[_array((T, D_MODEL), BF16)],
        in_specs=[pl.BlockSpec((tm, D_MODEL), lambda i: (i, 0)), pl.BlockSpec((1, D_MODEL), lambda i: (0, 0))],
        out_specs=[pl.BlockSpec((tm, D_MODEL), lambda i: (i, 0))], vmem_mib=24, args=(x, g), carried=carried)


def _resident(shape, block=None):
    at = (0,) * len(shape) if block is None else block
    return pl.BlockSpec(shape, lambda *_: at, pipeline_mode=pl.Buffered(1))


def _qkv_proj(h, win_t, tables, tm, carried=None):
    T = h.shape[0]

    def body(h_ref, w_ref, cos_ref, s1_ref, s2_ref, o0, o1, o2, *slabs):
        hv = h_ref[...]
        cos, s1, s2 = cos_ref[...], s1_ref[...], s2_ref[...]
        for kind in range(3):
            for g, (o_ref, r) in enumerate(zip((o0, o1, o2), DILATIONS)):
                blk = 3 * kind + g
                y = _nt(hv, w_ref[blk * ATTN_W:(blk + 1) * ATTN_W, :])
                if kind < 2:
                    y = _rope(y, cos, s1, s2)
                if kind == 0:
                    y = y * 0.125
                cols = slice(kind * ATTN_W, (kind + 1) * ATTN_W)
                if r == 1:
                    o_ref[0, :, cols] = y.astype(BF16)
                    continue
                slab = slabs[blk % len(slabs)]
                for s in range(N_SLABS):
                    slab[s] = y[:, s * HEAD_PAIR:(s + 1) * HEAD_PAIR]
                for rho in range(r):
                    for s in range(N_SLABS):
                        at = kind * ATTN_W + s * HEAD_PAIR
                        o_ref[rho, :, at:at + HEAD_PAIR] = slab[s, pl.ds(rho, tm // r, stride=r), :].astype(BF16)

    row = pl.BlockSpec((tm, HEAD_PAIR), lambda i: (i, 0))
    return _call(
        body, name="qkv_proj", grid=(T // tm,),
        out_shape=[_array((r, T // r, 3 * ATTN_W), BF16) for r in DILATIONS],
        in_specs=[pl.BlockSpec((tm, D_MODEL), lambda i: (i, 0)), _resident((QKV_COLS, D_MODEL)), row, row, row],
        out_specs=[pl.BlockSpec((r, tm // r, 3 * ATTN_W), lambda i: (0, i, 0)) for r in DILATIONS],
        scratch_shapes=[_slab_scratch(tm)] * 3, vmem_mib=48, args=(h, win_t, *tables), carried=carried)


def _uv_gate_proj(h, win_t, tm, carried=None):
    T = h.shape[0]
    half = (IN_COLS - QKV_COLS) // 2

    def body(h_ref, wa_ref, wb_ref, o_ref):
        hv = h_ref[...]
        o_ref[:, :half] = _nt(hv, wa_ref[...]).astype(BF16)
        o_ref[:, half:] = _nt(hv, wb_ref[...]).astype(BF16)

    blk0 = QKV_COLS // half
    return _call(
        body, name="uv_gate_proj", grid=(T // tm,), out_shape=[_array((T, 2 * half), BF16)],
        in_specs=[pl.BlockSpec((tm, D_MODEL), lambda i: (i, 0)), _resident((half, D_MODEL), (blk0, 0)),
                  _resident((half, D_MODEL), (blk0 + 1, 0))],
        out_specs=[pl.BlockSpec((tm, 2 * half), lambda i: (i, 0))], vmem_mib=40, args=(h, win_t, win_t),
        carried=carried)


def _band_mask(n):
    row = lax.broadcasted_iota(jnp.int32, (2 * BLK, 2 * BLK), 0) & (BLK - 1)
    col = lax.broadcasted_iota(jnp.int32, (2 * BLK, 2 * BLK), 1)
    has_prev = (jnp.zeros_like(row) + n) > 0
    return ((col < BLK) & (col >= row) & has_prev) | ((col >= BLK) & (col - BLK <= row))


def _head_lanes():
    lane = lax.broadcasted_iota(jnp.int32, (1, HEAD_PAIR), 1)
    return lane < 64, lane >= 64


def _stack_heads(x, head0, head1):
    zero = jnp.zeros_like(x)
    return jnp.concatenate([jnp.where(head0, x, zero), jnp.where(head1, x, zero)], axis=0)


def _unstack_heads(y, head0):
    return jnp.where(head0, y[:BLK], y[BLK:])


def _per_head(stats, col):
    return jnp.concatenate([stats[:, col:col + 1], stats[:, col + 1:col + 2]], axis=0)


def _attn_fwd(qkv, group, carried=None):
    r, L, _ = qkv.shape
    nb = L // BLK

    def body(q_ref, kp_ref, kc_ref, vp_ref, vc_ref, o_ref, lse_ref):
        valid = _band_mask(pl.program_id(1))
        head0, head1 = _head_lanes()
        lane = lax.broadcasted_iota(jnp.int32, (1, HEAD_PAIR), 1)
        stats = jnp.zeros((BLK, HEAD_PAIR), F32)
        for p in range(N_SLABS):
            sl = slice(p * HEAD_PAIR, (p + 1) * HEAD_PAIR)
            k2 = jnp.concatenate([kp_ref[:, sl], kc_ref[:, sl]], axis=0)
            v2 = jnp.concatenate([vp_ref[:, sl], vc_ref[:, sl]], axis=0)
            s = jnp.where(valid, _nt(_stack_heads(q_ref[:, sl], head0, head1), k2), MASKED)
            m = jnp.max(s, axis=1, keepdims=True)
            e = jnp.exp(s - m)
            den = jnp.sum(e, axis=1, keepdims=True)
            o_ref[:, sl] = _unstack_heads(_nn(e.astype(BF16), v2) / den, head0)
            lse = m + jnp.log(den)
            stats = jnp.where(lane == 2 * p, lse[:BLK], jnp.where(lane == 2 * p + 1, lse[BLK:], stats))
        lse_ref[...] = stats

    def blk(kind, prev):
        if prev:
            return pl.BlockSpec((None, BLK, ATTN_W), lambda rho, n: (rho, jnp.maximum(n - 1, 0), kind))
        return pl.BlockSpec((None, BLK, ATTN_W), lambda rho, n: (rho, n, kind))

    return _call(
        body, name=f"attn_fwd_g{group}", grid=(r, nb),
        out_shape=[_array((r, L, ATTN_W), F32), _array((r, L, HEAD_PAIR), F32)],
        in_specs=[blk(0, False), blk(1, True), blk(1, False), blk(2, True), blk(2, False)],
        out_specs=[pl.BlockSpec((None, BLK, ATTN_W), lambda rho, n: (rho, n, 0)),
                   pl.BlockSpec((None, BLK, HEAD_PAIR), lambda rho, n: (rho, n, 0))],
        vmem_mib=24, args=(qkv, qkv, qkv, qkv, qkv), carried=carried)


def _attn_combine(outs, lses, tm):
    T = outs[0].shape[1]

    def body(o0, l0, o1, l1, o2, l2, attn_ref, lse_ref, so1, sl1, so2, sl2):
        for o_in, l_in, so, sl, r in ((o1, l1, so1, sl1, DILATIONS[1]), (o2, l2, so2, sl2, DILATIONS[2])):
            _rows_by_token(so, o_in, r, tm // r)
            _rows_by_token(sl, l_in, r, tm // r, n=1)
        head0, _ = _head_lanes()
        a0, a1, a2 = l0[0], sl1[0], sl2[0]
        mx = jnp.maximum(jnp.maximum(a0, a1), a2)
        e0, e1, e2 = jnp.exp(a0 - mx), jnp.exp(a1 - mx), jnp.exp(a2 - mx)
        tot = e0 + e1 + e2
        lse_ref[...] = mx + jnp.log(tot)
        w0, w1, w2 = e0 / tot, e1 / tot, e2 / tot
        for s in range(N_SLABS):
            lanes = slice(s * HEAD_PAIR, (s + 1) * HEAD_PAIR)

            def lanes_of(w):
                return jnp.where(head0, w[:, 2 * s:2 * s + 1], w[:, 2 * s + 1:2 * s + 2])

            attn_ref[:, lanes] = lanes_of(w0) * o0[0, :, lanes] + lanes_of(w1) * so1[s] + lanes_of(w2) * so2[s]

    ins, specs = [], []
    for g, r in enumerate(DILATIONS):
        ins += [outs[g], lses[g]]
        specs += [pl.BlockSpec((r, tm // r, ATTN_W), lambda i: (0, i, 0)),
                  pl.BlockSpec((r, tm // r, HEAD_PAIR), lambda i: (0, i, 0))]
    return _pallas(
        body, name="attn_combine", grid=(T // tm,),
        out_shape=[_array((T, ATTN_W), F32), _array((T, HEAD_PAIR), F32)], in_specs=specs,
        out_specs=[pl.BlockSpec((tm, ATTN_W), lambda i: (i, 0)), pl.BlockSpec((tm, HEAD_PAIR), lambda i: (i, 0))],
        scratch_shapes=[_slab_scratch(tm), _slab_scratch(tm, 1), _slab_scratch(tm), _slab_scratch(tm, 1)],
        compiler_params=_params(("parallel",), 32),
    )(*ins)


def _gelu(x):
    return 0.5 * x * (1.0 + lax.erf(x * (1.0 / math.sqrt(2.0))))


def _gelu_grad(x):
    return 0.5 * (1.0 + lax.erf(x * (1.0 / math.sqrt(2.0)))) + x * jnp.exp(-0.5 * x * x) * (1.0 / math.sqrt(2.0 * math.pi))


def _causal():
    row = lax.broadcasted_iota(jnp.int32, (BLK, BLK), 0)
    col = lax.broadcasted_iota(jnp.int32, (BLK, BLK), 1)
    return col <= row


def _bias_lanes(bt):
    grp = lax.broadcasted_iota(jnp.int32, (1, SGU_W), 1) // 64
    out = jnp.zeros((BLK, SGU_W), F32)
    for g in range(8):
        out = jnp.where(grp == g, bt[:, g:g + 1], out)
    return out


def _sgu_normalise(uv, ln_g, ln_b):
    z = _gelu(uv)
    u, v = z[:, :SGU_W], z[:, SGU_W:]
    mu = jnp.mean(v, axis=-1, keepdims=True)
    xc = v - mu
    rstd = lax.rsqrt(jnp.mean(xc * xc, axis=-1, keepdims=True) + EPS)
    xh = xc * rstd
    return u, xh, rstd, xh * ln_g + ln_b


def _sgu_mix(wc_ref, vb, head0):
    chunks = []
    for ch in range(vb.shape[0] // BLK):
        pairs = []
        for p in range(SGU_W // HEAD_PAIR):
            v_pair = vb[ch * BLK:(ch + 1) * BLK, p * HEAD_PAIR:(p + 1) * HEAD_PAIR]
            pairs.append(jnp.where(head0, _nn(wc_ref[2 * p], v_pair), _nn(wc_ref[2 * p + 1], v_pair)))
        chunks.append(jnp.concatenate(pairs, axis=1))
    return jnp.concatenate(chunks, axis=0)


def _sgu_fwd(uvg, ln_g, ln_b, w_s, b_t, tm):
    T = uvg.shape[0]

    def body(uv_ref, g_ref, b_ref, w_ref, bt_ref, o_ref, wc_ref, bias_ref):
        @pl.when(pl.program_id(0) == 0)
        def _():
            causal = _causal()
            for g in range(8):
                wc_ref[g] = jnp.where(causal, w_ref[g], 0.0).astype(BF16)
            bias_ref[...] = _bias_lanes(bt_ref[...])

        u, _, _, vn = _sgu_normalise(uv_ref[...].astype(F32), g_ref[...], b_ref[...])
        mixed = _sgu_mix(wc_ref, vn.astype(BF16), _head_lanes()[0])
        o_ref[...] = (u * (mixed + jnp.tile(bias_ref[...], (tm // BLK, 1)))).astype(BF16)

    vec = pl.BlockSpec((1, SGU_W), lambda i: (0, 0))
    return _pallas(
        body, name="sgu_fwd", grid=(T // tm,), out_shape=_array((T, SGU_W), BF16),
        in_specs=[pl.BlockSpec((tm, 2 * SGU_W), lambda i: (i, 0)), vec, vec,
                  pl.BlockSpec((8, BLK, BLK), lambda i: (0, 0, 0)), pl.BlockSpec((BLK, 8), lambda i: (0, 0))],
        out_specs=pl.BlockSpec((tm, SGU_W), lambda i: (i, 0)),
        scratch_shapes=[pltpu.VMEM((8, BLK, BLK), BF16), pltpu.VMEM((BLK, SGU_W), F32)],
        compiler_params=_params(("arbitrary",), 32),
    )(uvg, ln_g, ln_b, w_s, b_t)


def _merge_fwd(attn, sgu, wpa_t, wps_t, uvg, tm):
    T = attn.shape[0]

    def body(attn_ref, sgu_ref, wpa_ref, wps_ref, ga_ref, gb_ref, pa_ref, ps_ref, m_ref):
        pa = _nt(attn_ref[...].astype(BF16), wpa_ref[...])
        ps = _nt(sgu_ref[...], wps_ref[...])
        pa_ref[...] = pa.astype(BF16)
        ps_ref[...] = ps.astype(BF16)
        ga, gb = ga_ref[...].astype(F32), gb_ref[...].astype(F32)
        m_ref[...] = (jax.nn.sigmoid(ga) * pa + jax.nn.sigmoid(gb) * ps).astype(BF16)

    half = pl.BlockSpec((tm, ATTN_W), lambda i: (i, 0))
    wide = pl.BlockSpec((tm, D_MODEL), lambda i: (i, 0))
    w = pl.BlockSpec((D_MODEL, ATTN_W), lambda i: (0, 0))
    res = _array((T, D_MODEL), BF16)
    return _pallas(
        body, name="merge_fwd", grid=(T // tm,),
        out_shape=[res, res, _array((T, D_MODEL), BF16)],
        in_specs=[half, half, w, w, pl.BlockSpec((tm, D_MODEL), lambda i: (i, 1)),
                  pl.BlockSpec((tm, D_MODEL), lambda i: (i, 2))],
        out_specs=[wide, wide, wide], compiler_params=_params(("parallel",), 40),
    )(attn, sgu, wpa_t, wps_t, uvg, uvg)


def _out_residual(merged, w_out, x, g2, tm):
    T = x.shape[0]

    def body(m_ref, w_ref, x_ref, g_ref, o_ref, h_ref):
        x1 = x_ref[...] + _nn(m_ref[...], w_ref[...])
        o_ref[...] = x1
        r = lax.rsqrt(jnp.mean(x1 * x1, axis=-1, keepdims=True) + EPS)
        h_ref[...] = (x1 * r * g_ref[...]).astype(BF16)

    wide = pl.BlockSpec((tm, D_MODEL), lambda i: (i, 0))
    return _pallas(
        body, name="out_residual", grid=(T // tm,),
        out_shape=[_array((T, D_MODEL), F32), _array((T, D_MODEL), BF16)],
        in_specs=[wide, _resident((D_MODEL, D_MODEL)), wide, pl.BlockSpec((1, D_MODEL), lambda i: (0, 0))],
        out_specs=[wide, wide], compiler_params=_params(("parallel",), 32),
    )(merged, w_out, x, g2)


def _ffn_fwd(h2, wg_t, wu_t, tm):
    T = h2.shape[0]

    def body(h_ref, wg_ref, wu_ref, gate_ref, up_ref, ff_ref):
        h = h_ref[...]
        for j in range(D_FF // FF_TILE):
            cols = slice(j * FF_TILE, (j + 1) * FF_TILE)
            gate, up = _nt(h, wg_ref[cols, :]), _nt(h, wu_ref[cols, :])
            gate_ref[:, cols] = gate.astype(BF16)
            up_ref[:, cols] = up.astype(BF16)
            ff_ref[:, cols] = (gate * jax.nn.sigmoid(gate) * up).astype(BF16)

    w = _resident((D_FF, D_MODEL))
    o = pl.BlockSpec((tm, D_FF), lambda i: (i, 0))
    res = _array((T, D_FF), BF16)
    return _pallas(
        body, name="ffn_fwd", grid=(T // tm,), out_shape=[res, res, res],
        in_specs=[pl.BlockSpec((tm, D_MODEL), lambda i: (i, 0)), w, w], out_specs=[o, o, o],
        compiler_params=_params(("parallel",), 52),
    )(h2, wg_t, wu_t)


def _down_loss(ff, w_down, x1, final_g, target, tm):
    T = x1.shape[0]

    def body(ff_ref, w_ref, x1_ref, g_ref, t_ref, dx_ref, dxb_ref, loss_ref, dg_ref):
        @pl.when(pl.program_id(0) == 0)
        def _():
            loss_ref[...] = jnp.zeros_like(loss_ref)
            dg_ref[...] = jnp.zeros_like(dg_ref)

        x2 = x1_ref[...] + _nn(ff_ref[...], w_ref[...])
        g = g_ref[...]
        r = lax.rsqrt(jnp.mean(x2 * x2, axis=-1, keepdims=True) + EPS)
        xh = x2 * r
        err = xh * g - t_ref[...]
        loss_ref[...] += jnp.sum(err * err, axis=0, keepdims=True) * (0.5 / D_MODEL)
        dy = err * (1.0 / D_MODEL)
        dg_ref[...] += jnp.sum(dy * xh, axis=0, keepdims=True)
        dxh = dy * g
        dx = r * (dxh - xh * jnp.mean(dxh * xh, axis=-1, keepdims=True))
        dx_ref[...] = dx
        dxb_ref[...] = dx.astype(BF16)

    wide = pl.BlockSpec((tm, D_MODEL), lambda i: (i, 0))
    vec = pl.BlockSpec((1, D_MODEL), lambda i: (0, 0))
    vec_shape = _array((1, D_MODEL), F32)
    return _pallas(
        body, name="down_loss", grid=(T // tm,),
        out_shape=[_array((T, D_MODEL), F32), _array((T, D_MODEL), BF16),
                   vec_shape, vec_shape],
        in_specs=[pl.BlockSpec((tm, D_FF), lambda i: (i, 0)), _resident((D_FF, D_MODEL)), wide, vec, wide],
        out_specs=[wide, wide, vec, vec],
        compiler_params=_params(("arbitrary",), 40),
    )(ff, w_down, x1, final_g, target)


def _rmsnorm_bwd(dh, xin, g, d_res, dg_ref):
    r = lax.rsqrt(jnp.mean(xin * xin, axis=-1, keepdims=True) + EPS)
    xh = xin * r
    dg_ref[...] += jnp.sum(dh * xh, axis=0, keepdims=True)
    dxh = dh * g
    return d_res + r * (dxh - xh * jnp.mean(dxh * xh, axis=-1, keepdims=True))


def _ffn_bwd(dx2, dx2b, w_down, wg_t, wu_t, gate, up, x1, g2, tm):
    T = dx2.shape[0]

    def body(dx_ref, dxb_ref, wd_ref, wg_ref, wu_ref, gate_ref, up_ref, x1_ref, g_ref,
             dgate_ref, dup_ref, dx1_ref, dx1b_ref, dg_ref):
        @pl.when(pl.program_id(0) == 0)
        def _():
            dg_ref[...] = jnp.zeros_like(dg_ref)

        dxb = dxb_ref[...]
        dh = jnp.zeros((tm, D_MODEL), F32)
        for j in range(D_FF // FF_TILE):
            cols = slice(j * FF_TILE, (j + 1) * FF_TILE)
            dff = _nt(dxb, wd_ref[cols, :])
            gate, up = gate_ref[:, cols].astype(F32), up_ref[:, cols].astype(F32)
            sg = jax.nn.sigmoid(gate)
            dgate = (dff * up * sg * (1.0 + gate * (1.0 - sg))).astype(BF16)
            dup = (dff * gate * sg).astype(BF16)
            dgate_ref[:, cols] = dgate
            dup_ref[:, cols] = dup
            dh += _nn(dgate, wg_ref[cols, :]) + _nn(dup, wu_ref[cols, :])
        dx1 = _rmsnorm_bwd(dh, x1_ref[...], g_ref[...], dx_ref[...], dg_ref)
        dx1_ref[...] = dx1
        dx1b_ref[...] = dx1.astype(BF16)

    wide = pl.BlockSpec((tm, D_MODEL), lambda i: (i, 0))
    ffw = pl.BlockSpec((tm, D_FF), lambda i: (i, 0))
    vec = pl.BlockSpec((1, D_MODEL), lambda i: (0, 0))
    w = _resident((D_FF, D_MODEL))
    ff_shape = _array((T, D_FF), BF16)
    return _pallas(
        body, name="ffn_bwd", grid=(T // tm,),
        out_shape=[ff_shape, ff_shape, _array((T, D_MODEL), F32),
                   _array((T, D_MODEL), BF16), _array((1, D_MODEL), F32)],
        in_specs=[wide, wide, w, w, w, ffw, ffw, wide, vec],
        out_specs=[ffw, ffw, wide, wide, vec], compiler_params=_params(("arbitrary",), 56),
    )(dx2, dx2b, w_down, wg_t, wu_t, gate, up, x1, g2)


def _mm_tn(a, b, *, tmm, tk, name, into=None, row_block0=0, rows_total=None):
    T, M = a.shape
    N = b.shape[1]
    rows_total = M if rows_total is None else rows_total

    def body(*refs):
        a_ref, b_ref, o_ref = refs[0], refs[1], refs[-1]

        @pl.when(pl.program_id(1) == 0)
        def _():
            o_ref[...] = jnp.zeros_like(o_ref)

        o_ref[...] += _tn(a_ref[...], b_ref[...])

    ins = [a, b] + ([] if into is None else [into])
    specs = [pl.BlockSpec((tk, tmm), lambda i, k: (k, i)), pl.BlockSpec((tk, N), lambda i, k: (k, 0))]
    return _pallas(
        body, name=name, grid=(M // tmm, T // tk),
        out_shape=_array((rows_total, N), F32),
        in_specs=specs + ([] if into is None else [ANY]),
        out_specs=pl.BlockSpec((tmm, N), lambda i, k: (row_block0 + i, 0)),
        input_output_aliases={} if into is None else {2: 0},
        compiler_params=_params(("parallel", "arbitrary"), 48),
    )(*ins)


def _in_bwd(dqkv, duv, dgates, win_t, x, g1, dx1, tm, carried=None):
    T = x.shape[0]
    uv0, gates0 = QKV_COLS, QKV_COLS + 2 * SGU_W

    def body(dq_ref, du_ref, dgt_ref, w_ref, x_ref, g_ref, d_ref, dx_ref, dg_ref):
        @pl.when(pl.program_id(0) == 0)
        def _():
            dg_ref[...] = jnp.zeros_like(dg_ref)

        dh = (_nn(dq_ref[...], w_ref[:uv0, :]) + _nn(du_ref[...], w_ref[uv0:gates0, :])
              + _nn(dgt_ref[...], w_ref[gates0:, :]))
        dx_ref[...] = _rmsnorm_bwd(dh, x_ref[...], g_ref[...], d_ref[...], dg_ref)

    def cols(n):
        return pl.BlockSpec((tm, n), lambda i: (i, 0))

    wide = cols(D_MODEL)
    vec = pl.BlockSpec((1, D_MODEL), lambda i: (0, 0))
    return _call(
        body, name="in_bwd", grid=(T // tm,),
        out_shape=[_array((T, D_MODEL), F32), _array((1, D_MODEL), F32)],
        in_specs=[cols(QKV_COLS), cols(2 * SGU_W), cols(2 * D_MODEL), _resident((IN_COLS, D_MODEL)), wide, vec, wide],
        out_specs=[wide, vec], vmem_mib=56, args=(dqkv, duv, dgates, win_t, x, g1, dx1), carried=carried)


def _merge_bwd(dx1b, w_out, pa, ps, uvg, tm, carried=None):
    T = dx1b.shape[0]

    def body(dx_ref, w_ref, pa_ref, ps_ref, ga_ref, gb_ref, dpa_ref, dps_ref, dg_ref):
        dm = _nt(dx_ref[...], w_ref[...])
        ga, gb = jax.nn.sigmoid(ga_ref[...].astype(F32)), jax.nn.sigmoid(gb_ref[...].astype(F32))
        dpa_ref[...] = (dm * ga).astype(BF16)
        dps_ref[...] = (dm * gb).astype(BF16)
        dg_ref[:, :D_MODEL] = (dm * pa_ref[...].astype(F32) * ga * (1.0 - ga)).astype(BF16)
        dg_ref[:, D_MODEL:] = (dm * ps_ref[...].astype(F32) * gb * (1.0 - gb)).astype(BF16)

    wide = pl.BlockSpec((tm, D_MODEL), lambda i: (i, 0))
    res = _array((T, D_MODEL), BF16)
    return _call(
        body, name="merge_bwd", grid=(T // tm,),
        out_shape=[res, res, _array((T, 2 * D_MODEL), BF16)],
        in_specs=[wide, _resident((D_MODEL, D_MODEL)), wide, wide,
                  pl.BlockSpec((tm, D_MODEL), lambda i: (i, 1)), pl.BlockSpec((tm, D_MODEL), lambda i: (i, 2))],
        out_specs=[wide, wide, pl.BlockSpec((tm, 2 * D_MODEL), lambda i: (i, 0))],
        vmem_mib=48, args=(dx1b, w_out, pa, ps, uvg, uvg), carried=carried)


def _proj_bwd(dpa, dps, wpa_t, wps_t, tm):
    T = dpa.shape[0]

    def body(dpa_ref, dps_ref, wpa_ref, wps_ref, dattn_ref, dsgu_ref):
        dattn_ref[...] = _nn(dpa_ref[...], wpa_ref[...])
        dsgu_ref[...] = _nn(dps_ref[...], wps_ref[...])

    wide = pl.BlockSpec((tm, D_MODEL), lambda i: (i, 0))
    half = pl.BlockSpec((tm, ATTN_W), lambda i: (i, 0))
    w = pl.BlockSpec((D_MODEL, ATTN_W), lambda i: (0, 0))
    res = _array((T, ATTN_W), F32)
    return _pallas(
        body, name="proj_bwd", grid=(T // tm,), out_shape=[res, res], in_specs=[wide, wide, w, w],
        out_specs=[half, half], compiler_params=_params(("parallel",), 32),
    )(dpa, dps, wpa_t, wps_t)


def _sgu_bwd(uvg, dsgu, ln_g, ln_b, w_s, b_t, tm, carried=None):
    T = uvg.shape[0]
    nsteps = T // tm

    def body(uv_ref, ds_ref, g_ref, b_ref, w_ref, bt_ref, duv_ref, dw_ref, dbt_ref, dg_ref, db_ref,
             wc_ref, wct_ref, bias_ref, dbias_ref):
        step = pl.program_id(0)
        head0, head1 = _head_lanes()

        @pl.when(step == 0)
        def _():
            causal = _causal()
            for g in range(8):
                wc = jnp.where(causal, w_ref[g], 0.0)
                wc_ref[g] = wc.astype(BF16)
                wct_ref[g] = wc.T.astype(BF16)
            bias_ref[...] = _bias_lanes(bt_ref[...])
            dbias_ref[...] = jnp.zeros_like(dbias_ref)
            dw_ref[...] = jnp.zeros_like(dw_ref)
            dg_ref[...] = jnp.zeros_like(dg_ref)
            db_ref[...] = jnp.zeros_like(db_ref)

        uv = uv_ref[...].astype(F32)
        ln_gain = g_ref[...]
        u, xh, rstd, vn = _sgu_normalise(uv, ln_gain, b_ref[...])
        vb = vn.astype(BF16)
        mixed = _sgu_mix(wc_ref, vb, head0) + jnp.tile(bias_ref[...], (tm // BLK, 1))
        dout = ds_ref[...]
        du = dout * mixed
        dmixed = dout * u
        dmb = dmixed.astype(BF16)
        dvn_chunks = []
        for ch in range(tm // BLK):
            rows = slice(ch * BLK, (ch + 1) * BLK)
            dbias_ref[...] += dmixed[rows]
            pairs = []
            for p in range(SGU_W // HEAD_PAIR):
                lanes = slice(p * HEAD_PAIR, (p + 1) * HEAD_PAIR)
                dm_pair, v_pair = dmb[rows, lanes], vb[rows, lanes]
                acc = jnp.zeros((BLK, HEAD_PAIR), F32)
                for hh, half in enumerate((head0, head1)):
                    dm_h = jnp.where(half, dm_pair, jnp.zeros_like(dm_pair))
                    dw_ref[2 * p + hh] += _nt(dm_h, v_pair)
                    acc += _nn(wct_ref[2 * p + hh], dm_h)
                pairs.append(acc)
            dvn_chunks.append(jnp.concatenate(pairs, axis=1))
        dvn = jnp.concatenate(dvn_chunks, axis=0)
        dg_ref[...] += jnp.sum(dvn * xh, axis=0, keepdims=True)
        db_ref[...] += jnp.sum(dvn, axis=0, keepdims=True)
        dxh = dvn * ln_gain
        dv = rstd * (dxh - jnp.mean(dxh, axis=-1, keepdims=True) - xh * jnp.mean(dxh * xh, axis=-1, keepdims=True))
        dgelu = _gelu_grad(uv)
        duv_ref[:, :SGU_W] = (du * dgelu[:, :SGU_W]).astype(BF16)
        duv_ref[:, SGU_W:] = (dv * dgelu[:, SGU_W:]).astype(BF16)

        @pl.when(step == nsteps - 1)
        def _():
            causal = _causal()
            for g in range(8):
                dw_ref[g] = jnp.where(causal, dw_ref[g], 0.0)
            grp = lax.broadcasted_iota(jnp.int32, (1, SGU_W), 1) // 64
            col = lax.broadcasted_iota(jnp.int32, (1, 8), 1)
            dbias = dbias_ref[...]
            out = jnp.zeros((BLK, 8), F32)
            for g in range(8):
                s = jnp.sum(jnp.where(grp == g, dbias, 0.0), axis=1, keepdims=True)
                out = jnp.where(col == g, s, out)
            dbt_ref[...] = out

    vec = pl.BlockSpec((1, SGU_W), lambda i: (0, 0))
    w3 = pl.BlockSpec((8, BLK, BLK), lambda i: (0, 0, 0))
    bt = pl.BlockSpec((BLK, 8), lambda i: (0, 0))
    return _call(
        body, name="sgu_bwd", grid=(nsteps,),
        out_shape=[_array((T, 2 * SGU_W), BF16), _array((8, BLK, BLK), F32),
                   _array((BLK, 8), F32), _array((1, SGU_W), F32),
                   _array((1, SGU_W), F32)],
        in_specs=[pl.BlockSpec((tm, 2 * SGU_W), lambda i: (i, 0)), pl.BlockSpec((tm, SGU_W), lambda i: (i, 0)),
                  vec, vec, w3, bt],
        out_specs=[pl.BlockSpec((tm, 2 * SGU_W), lambda i: (i, 0)), w3, bt, vec, vec],
        scratch_shapes=[pltpu.VMEM((8, BLK, BLK), BF16), pltpu.VMEM((8, BLK, BLK), BF16),
                        pltpu.VMEM((BLK, SGU_W), F32), pltpu.VMEM((BLK, SGU_W), F32)],
        vmem_mib=40, args=(uvg, dsgu, ln_g, ln_b, w_s, b_t), carried=carried)


D_LANE0 = 8


def _attn_bwd_prepare(dattn, attn, lse, tm):
    T = dattn.shape[0]

    def body(da_ref, at_ref, lse_ref, *outs):
        scr_da, scr_st = outs[-2:]
        outs = outs[:-2]
        head0, _ = _head_lanes()
        lane = lax.broadcasted_iota(jnp.int32, (1, HEAD_PAIR), 1)
        stats = lse_ref[...]
        for s in range(N_SLABS):
            lanes = slice(s * HEAD_PAIR, (s + 1) * HEAD_PAIR)
            da = da_ref[:, lanes]
            pp = da * at_ref[:, lanes]
            d0 = jnp.sum(jnp.where(head0, pp, 0.0), axis=1, keepdims=True)
            d1 = jnp.sum(jnp.where(head0, 0.0, pp), axis=1, keepdims=True)
            stats = jnp.where(lane == D_LANE0 + 2 * s, d0, jnp.where(lane == D_LANE0 + 2 * s + 1, d1, stats))
            scr_da[s] = da
            outs[0][0, :, lanes] = da.astype(BF16)
        scr_st[0] = stats
        outs[1][0] = stats
        for g, r in enumerate(DILATIONS):
            if r > 1:
                _rows_by_residue(outs[2 * g], scr_da, r, tm // r, BF16)
                _rows_by_residue(outs[2 * g + 1], scr_st, r, tm // r, F32, n=1)

    tok = pl.BlockSpec((tm, ATTN_W), lambda i: (i, 0))
    shapes, specs = [], []
    for r in DILATIONS:
        shapes += [_array((r, T // r, ATTN_W), BF16), _array((r, T // r, HEAD_PAIR), F32)]
        specs += [pl.BlockSpec((r, tm // r, ATTN_W), lambda i: (0, i, 0)),
                  pl.BlockSpec((r, tm // r, HEAD_PAIR), lambda i: (0, i, 0))]
    return _pallas(
        body, name="attn_bwd_prepare", grid=(T // tm,), out_shape=shapes,
        in_specs=[tok, tok, pl.BlockSpec((tm, HEAD_PAIR), lambda i: (i, 0))],
        out_specs=specs, scratch_shapes=[_slab_scratch(tm), _slab_scratch(tm, 1)],
        compiler_params=_params(("parallel",), 40),
    )(dattn, attn, lse)


def _attn_bwd(qkv, dattn, stats, group, carried=None):
    r, L, _ = qkv.shape
    nb = L // BLK

    def body(q_ref, kp_ref, kc_ref, vp_ref, vc_ref, da_ref, st_ref, dq_ref, dk_ref, dv_ref, carry_k, carry_v):
        n = pl.program_id(1)

        @pl.when(n == 0)
        def _():
            carry_k[...] = jnp.zeros_like(carry_k)
            carry_v[...] = jnp.zeros_like(carry_v)

        @pl.when(n < nb)
        def _():
            valid = _band_mask(n)
            head0, head1 = _head_lanes()
            st = st_ref[...]
            for p in range(N_SLABS):
                sl = slice(p * HEAD_PAIR, (p + 1) * HEAD_PAIR)
                k2 = jnp.concatenate([kp_ref[:, sl], kc_ref[:, sl]], axis=0)
                v2 = jnp.concatenate([vp_ref[:, sl], vc_ref[:, sl]], axis=0)
                qs = _stack_heads(q_ref[:, sl], head0, head1)
                das = _stack_heads(da_ref[:, sl], head0, head1)
                prob = jnp.where(valid, jnp.exp(_nt(qs, k2) - _per_head(st, 2 * p)), 0.0)
                ds = (prob * (_nt(das, v2) - _per_head(st, D_LANE0 + 2 * p))).astype(BF16)
                dk2 = _tn(ds, qs)
                dv2 = _tn(prob.astype(BF16), das)
                dq_ref[:, sl] = _unstack_heads(_nn(ds, k2), head0)
                dk_ref[:, sl] = carry_k[:, sl] + dk2[:BLK]
                dv_ref[:, sl] = (carry_v[:, sl] + dv2[:BLK]).astype(BF16)
                carry_k[:, sl] = dk2[BLK:]
                carry_v[:, sl] = dv2[BLK:]

        @pl.when(n == nb)
        def _():
            dk_ref[...] = carry_k[...]
            dv_ref[...] = carry_v[...].astype(BF16)

    def cur(kind, width=ATTN_W):
        return pl.BlockSpec((None, BLK, width), lambda rho, n: (rho, jnp.minimum(n, nb - 1), kind))

    def prev(kind):
        return pl.BlockSpec((None, BLK, ATTN_W), lambda rho, n: (rho, jnp.clip(n - 1, 0, nb - 1), kind))

    res = _array((r, L, ATTN_W), F32)
    return _call(
        body, name=f"attn_bwd_g{group}", grid=(r, nb + 1), out_shape=[res, res, _array((r, L, ATTN_W), BF16)],
        in_specs=[cur(0), prev(1), cur(1), prev(2), cur(2), cur(0), cur(0, HEAD_PAIR)],
        out_specs=[cur(0), prev(0), prev(0)],
        scratch_shapes=[pltpu.VMEM((BLK, ATTN_W), F32), pltpu.VMEM((BLK, ATTN_W), F32)],
        vmem_mib=32, args=(qkv, qkv, qkv, qkv, qkv, dattn, stats), carried=carried)


def _dqkv_token_order(dqkv_groups, tables, tm):
    T = tables[0].shape[0]

    def body(*refs):
        ins = refs[:9]
        cos_ref, s1_ref, s2_ref, o_ref, scr = refs[9:]
        cos, s1, s2 = cos_ref[...], s1_ref[...], s2_ref[...]
        for g, r in enumerate(DILATIONS):
            for kind in range(3):
                src = ins[3 * g + kind]
                if r > 1:
                    _rows_by_token(scr, src, r, tm // r)
                for s in range(N_SLABS):
                    val = scr[s] if r > 1 else src[0, :, s * HEAD_PAIR:(s + 1) * HEAD_PAIR]
                    if kind < 2:
                        val = _rope_transposed(val, cos, s1, s2)
                    if kind == 0:
                        val = val * 0.125
                    at = (3 * kind + g) * ATTN_W + s * HEAD_PAIR
                    o_ref[:, at:at + HEAD_PAIR] = val.astype(BF16)

    specs = []
    for r in DILATIONS:
        specs += [pl.BlockSpec((r, tm // r, ATTN_W), lambda i: (0, i, 0))] * 3
    row = pl.BlockSpec((tm, HEAD_PAIR), lambda i: (i, 0))
    flat = [a for grp in dqkv_groups for a in grp]
    return _pallas(
        body, name="dqkv_token_order", grid=(T // tm,), out_shape=_array((T, QKV_COLS), BF16),
        in_specs=specs + [row] * 3, out_specs=pl.BlockSpec((tm, QKV_COLS), lambda i: (i, 0)),
        scratch_shapes=[_slab_scratch(tm)], compiler_params=_params(("parallel",), 48),
    )(*flat, *tables)


def _row_tile(rows):
    for cand in (320, 256, 176, 128):
        if rows % cand == 0:
            return cand
    return rows


def _pair_sum(grad4, recv, chip, name):
    _, _, rows, cols = grad4.shape
    tr = _row_tile(rows)

    def body(ids_ref, g_ref, r_ref, gown_ref, rown_ref, sum_ref, own_ref):
        sum_ref[...] = (g_ref[...] + r_ref[...]).astype(BF16)

        @pl.when(pl.program_id(1) == 0)
        def _():
            own_ref[...] = gown_ref[...] + rown_ref[...]

    grid_spec = pltpu.PrefetchScalarGridSpec(
        num_scalar_prefetch=1, grid=(rows // tr, 4),
        in_specs=[pl.BlockSpec((None, None, tr, cols), lambda i, q, ids: (q, ids[1], i, 0)),
                  pl.BlockSpec((None, tr, cols), lambda i, q, ids: (q, i, 0)),
                  pl.BlockSpec((None, None, tr, cols), lambda i, q, ids: (ids[0], ids[1], i, 0)),
                  pl.BlockSpec((None, tr, cols), lambda i, q, ids: (ids[0], i, 0))],
        out_specs=[pl.BlockSpec((None, tr, cols), lambda i, q, ids: (q, i, 0)),
                   pl.BlockSpec((tr, cols), lambda i, q, ids: (i, 0))])
    return _pallas(
        body, name=name, grid_spec=grid_spec,
        out_shape=[_array((4, rows, cols), BF16), _array((rows, cols), F32)],
        compiler_params=_params(("arbitrary", "arbitrary"), 32),
    )(chip, grad4, recv, grad4, recv)


def _chip_sum(own, others, name):
    rows, cols = own.shape
    tr = _row_tile(rows)

    def body(own_ref, oth_ref, o_ref):
        total = own_ref[...]
        for j in range(3):
            total = total + oth_ref[j].astype(F32)
        o_ref[...] = total

    blk = pl.BlockSpec((tr, cols), lambda i: (i, 0))
    return _pallas(
        body, name=name, grid=(rows // tr,), out_shape=_array((rows, cols), F32),
        in_specs=[blk, pl.BlockSpec((3, tr, cols), lambda i: (0, i, 0))], out_specs=blk,
        compiler_params=_params(("parallel",), 32),
    )(own, others)


def _adam_math(w, g, m, v):
    m = ADAM_B1 * m + (1.0 - ADAM_B1) * g
    v = ADAM_B2 * v + (1.0 - ADAM_B2) * (g * g)
    m_hat = m / (1.0 - ADAM_B1 ** ADAM_STEP)
    v_hat = v / (1.0 - ADAM_B2 ** ADAM_STEP)
    delta = -ADAM_LR * (m_hat / (jnp.sqrt(v_hat) + ADAM_EPS) + ADAM_WD * w)
    return delta, m, v


def _adamw(w, g, m, v, name):
    rows, cols = w.shape
    tr = _row_tile(rows)

    def body(w_ref, g_ref, m_ref, v_ref, d_ref, nm_ref, nv_ref):
        d_ref[...], nm_ref[...], nv_ref[...] = _adam_math(w_ref[...], g_ref[...], m_ref[...], v_ref[...])

    blk = pl.BlockSpec((tr, cols), lambda i: (i, 0))
    res = _array((rows, cols), F32)
    return _pallas(
        body, name=name, grid=(rows // tr,), out_shape=[res, res, res], in_specs=[blk] * 4, out_specs=[blk] * 3,
        compiler_params=_params(("parallel",), 32),
    )(w, g, m, v)


def _small_update(late_parts, early_parts, w, m, v, loss_rows):
    rows = w.shape[0]

    def total(p_ref):
        n = p_ref.shape[0] // N_DEV
        acc = p_ref[0:n, :]
        for dev in range(1, N_DEV):
            acc = acc + p_ref[dev * n:(dev + 1) * n, :]
        return acc

    def body(late_ref, early_ref, w_ref, m_ref, v_ref, g_ref, d_ref, nm_ref, nv_ref, loss_ref):
        g = jnp.concatenate([total(late_ref), total(early_ref)], axis=0)
        g_ref[...] = g
        d_ref[...], nm_ref[...], nv_ref[...] = _adam_math(w_ref[...], g, m_ref[...], v_ref[...])
        loss_ref[...] = jnp.sum(jnp.sum(g[rows - loss_rows:, :], axis=1, keepdims=True), axis=0, keepdims=True)

    res = jax.ShapeDtypeStruct((rows, HEAD_PAIR), F32)
    return _pallas(
        body, name="small_update", out_shape=[res, res, res, res, jax.ShapeDtypeStruct((1, 1), F32)],
        compiler_params=pltpu.CompilerParams(vmem_limit_bytes=32 * MIB),
    )(late_parts, early_parts, w, m, v)


def kernel(x, positions, norm1_g, w_in, sgu_ln_g, sgu_ln_b, w_spatial, b_spatial, w_proj_attn, w_proj_sgu, w_out, norm2_g, w_ffn_gate, w_ffn_up, w_ffn_down, final_g, loss_target, m_norm1_g, m_w_in, m_sgu_ln_g, m_sgu_ln_b, m_w_spatial, m_b_spatial, m_w_proj_attn, m_w_proj_sgu, m_w_out, m_norm2_g, m_w_ffn_gate, m_w_ffn_up, m_w_ffn_down, m_final_g, v_norm1_g, v_w_in, v_sgu_ln_g, v_sgu_ln_b, v_w_spatial, v_b_spatial, v_w_proj_attn, v_w_proj_sgu, v_w_out, v_norm2_g, v_w_ffn_gate, v_w_ffn_up, v_w_ffn_down, v_final_g):
    T = x.shape[1]
    tm = 512
    xt = x[0]
    target = loss_target[0]
    chip = jnp.stack([2 * lax.axis_index("x") + lax.axis_index("y"), lax.axis_index("c")]).astype(jnp.int32)

    def bf16_rows(w, transpose):
        return (jnp.transpose(w[0]) if transpose else w[0]).astype(BF16)

    w_in_gather = _all_gather([bf16_rows(w_in, True)])
    ffn_in = _all_gather([bf16_rows(w_ffn_gate, True), bf16_rows(w_ffn_up, True)])
    mixers = _all_gather([bf16_rows(w_out, False), bf16_rows(w_proj_attn, True), bf16_rows(w_proj_sgu, True)])
    ffn_out = _all_gather([bf16_rows(w_ffn_down, False)])

    inv_freq = ROPE_THETA ** (-jnp.arange(0, 2 * ROPE_HALF, 2, dtype=F32) / (2 * ROPE_HALF))
    inv_freq_row = jnp.tile(jnp.concatenate([inv_freq, inv_freq, jnp.zeros((48,), F32)]), 2).reshape(1, HEAD_PAIR)
    tables = _rope_tables(positions.reshape(T, 1), inv_freq_row, tm)
    b_t = jnp.transpose(b_spatial[0])

    (h,), (win_t,) = _rmsnorm_fwd(xt, norm1_g, "norm1_fwd", tm, carried=w_in_gather)
    qkv, (wg_t, wu_t) = _qkv_proj(h, win_t, tables, tm, carried=ffn_in)
    (uvg,), (wout, wpa_t, wps_t) = _uv_gate_proj(h, win_t, tm, carried=mixers)
    fwd0, (wd,) = _attn_fwd(qkv[0], 0, carried=ffn_out)
    fwd = [fwd0, _attn_fwd(qkv[1], 1)[0], _attn_fwd(qkv[2], 2)[0]]
    attn, lse = _attn_combine([f[0] for f in fwd], [f[1] for f in fwd], tm)
    sgu = _sgu_fwd(uvg, sgu_ln_g, sgu_ln_b, w_spatial[0], b_t, tm)
    pa, ps, merged = _merge_fwd(attn, sgu, wpa_t, wps_t, uvg, tm)
    x1, h2 = _out_residual(merged, wout, xt, norm2_g, tm)
    gate, up, ff = _ffn_fwd(h2, wg_t, wu_t, tm)
    dx2, dx2b, loss_cols, d_final_g = _down_loss(ff, wd, x1, final_g.reshape(1, D_MODEL), target, tm)

    dgate, dup, dx1, dx1b, d_norm2 = _ffn_bwd(dx2, dx2b, wd, wg_t, wu_t, gate, up, x1, norm2_g, tm // 2)
    tk = min(2048, T)
    d_wd = _mm_tn(ff, dx2b, tmm=FF_TILE, tk=tk, name="grad_w_ffn_down")
    d_wg_t = _mm_tn(dgate, h2, tmm=FF_TILE, tk=tk, name="grad_w_ffn_gate")
    d_wu_t = _mm_tn(dup, h2, tmm=FF_TILE, tk=tk, name="grad_w_ffn_up")

    def by_owner(grads):
        return [g.reshape(4, 2, g.shape[0] // N_DEV, g.shape[1]) for g in grads]

    def pair_sums(grads4, from_sibling, names):
        both = [_pair_sum(g4, rv, chip, "grad_pair_sum_" + nm) for g4, rv, nm in zip(grads4, from_sibling, names)]
        return [b[0] for b in both], [b[1] for b in both]

    ffn_names = ["w_ffn_gate", "w_ffn_up", "w_ffn_down"]
    ffn4 = by_owner([d_wg_t, d_wu_t, d_wd])
    (dpa, dps, dgates), ffn_sib = _merge_bwd(dx1b, wout, pa, ps, uvg, tm, carried=_pair_exchange(ffn4))
    ffn_sums, ffn_own = pair_sums(ffn4, ffn_sib, ffn_names)

    d_wout = _mm_tn(merged, dx1b, tmm=D_MODEL, tk=tk, name="grad_w_out")
    dattn, dsgu = _proj_bwd(dpa, dps, wpa_t, wps_t, tm)
    d_wpa_t = _mm_tn(dpa, attn.astype(BF16), tmm=D_MODEL, tk=tk, name="grad_w_proj_attn")
    d_wps_t = _mm_tn(dps, sgu, tmm=D_MODEL, tk=tk, name="grad_w_proj_sgu")
    mid_names = ["w_proj_attn", "w_proj_sgu", "w_out"]
    mid4 = by_owner([d_wpa_t, d_wps_t, d_wout])
    (duv, d_ws, d_bs_t, d_ln_g, d_ln_b), mid_sib = _sgu_bwd(uvg, dsgu, sgu_ln_g, sgu_ln_b, w_spatial[0], b_t, tm,
                                                           carried=_pair_exchange(mid4))
    mid_sums, mid_own = pair_sums(mid4, mid_sib, mid_names)

    prep = _attn_bwd_prepare(dattn, attn, lse, tm)
    dqkv0, ffn_far = _attn_bwd(qkv[0], prep[0], prep[1], 0, carried=_chip_exchange(ffn_sums))
    dqkv1, mid_far = _attn_bwd(qkv[1], prep[2], prep[3], 1, carried=_chip_exchange(mid_sums))
    def flat(parts):
        return jnp.concatenate([p.reshape(-1) for p in parts]).reshape(-1, HEAD_PAIR)

    early_part = flat([d_ln_g, d_ln_b, d_ws, jnp.transpose(d_bs_t), d_norm2, d_final_g, loss_cols])
    dqkv2, (early_parts,) = _attn_bwd(qkv[2], prep[4], prep[5], 2, carried=_all_gather([early_part]))
    dqkv = _dqkv_token_order([dqkv0, dqkv1, dqkv2], tables, tm)
    d_win_t = _mm_tn(dqkv, h, tmm=1536, tk=tk, name="grad_w_in_qkv", rows_total=IN_COLS)
    d_win_t = _mm_tn(duv, h, tmm=512, tk=tk, name="grad_w_in_uv", into=d_win_t, row_block0=9, rows_total=IN_COLS)
    d_win_t = _mm_tn(dgates, h, tmm=512, tk=tk, name="grad_w_in_gates", into=d_win_t, row_block0=11, rows_total=IN_COLS)
    in4 = by_owner([d_win_t])
    in_sib = _pair_exchange(in4).run_alone("w_in_grad_pair_exchange")
    in_sums, in_own = pair_sums(in4, in_sib, ["w_in"])
    (dx, d_norm1), in_far = _in_bwd(dqkv, duv, dgates, win_t, xt, norm1_g, dx1, tm, carried=_chip_exchange(in_sums))

    names = ["w_in"] + mid_names + ffn_names
    reduced = [_chip_sum(o, f, "grad_total_" + nm)
               for o, f, nm in zip(in_own + mid_own + ffn_own, in_far + mid_far + ffn_far, names)]
    transposed = (True, True, True, False, True, True, False)
    g_big = [jnp.transpose(r) if t else r for r, t in zip(reduced, transposed)]

    small_w = [norm1_g, sgu_ln_g, sgu_ln_b, w_spatial, b_spatial, norm2_g, final_g]
    small_m = [m_norm1_g, m_sgu_ln_g, m_sgu_ln_b, m_w_spatial, m_b_spatial, m_norm2_g, m_final_g]
    small_v = [v_norm1_g, v_sgu_ln_g, v_sgu_ln_b, v_w_spatial, v_b_spatial, v_norm2_g, v_final_g]
    zeros = jnp.zeros((D_MODEL,), F32)
    (late_parts,) = _all_gather([flat([d_norm1])]).run_alone("norm1_grad_all_gather", vmem=True)
    g_s, d_s, nm_s, nv_s, loss = _small_update(late_parts, early_parts, flat(small_w + [zeros]), flat(small_m + [zeros]),
                                               flat(small_v + [zeros]), D_MODEL // HEAD_PAIR)

    def unflat(vec):
        vec = vec.reshape(-1)
        out, at = [], 0
        for wgt in small_w:
            out.append(vec[at:at + wgt.size].reshape(wgt.shape))
            at += wgt.size
        return out

    small = [unflat(a) for a in (g_s, d_s, nm_s, nv_s)]

    big_w = [w_in, w_proj_attn, w_proj_sgu, w_out, w_ffn_gate, w_ffn_up, w_ffn_down]
    big_m = [m_w_in, m_w_proj_attn, m_w_proj_sgu, m_w_out, m_w_ffn_gate, m_w_ffn_up, m_w_ffn_down]
    big_v = [v_w_in, v_w_proj_attn, v_w_proj_sgu, v_w_out, v_w_ffn_gate, v_w_ffn_up, v_w_ffn_down]
    big_out = []
    for wgt, g, mm, vv, nm in zip(big_w, g_big, big_m, big_v, names):
        d, nm_, nv_ = _adamw(wgt[0], g, mm[0], vv[0], "adamw_" + nm)
        big_out.append([a[None] for a in (g, d, nm_, nv_)])

    small_at = {0: 0, 2: 1, 3: 2, 4: 3, 5: 4, 9: 5, 13: 6}
    big_at = {1: 0, 6: 1, 7: 2, 8: 3, 10: 4, 11: 5, 12: 6}
    outs = [loss[0, 0], dx[None]]
    for kind in range(4):
        for idx in range(14):
            outs.append(small[kind][small_at[idx]] if idx in small_at else big_out[big_at[idx]][kind])
    return tuple(outs)
```

```python
import functools
import math

import jax
import jax.numpy as jnp
from jax import lax
from jax.experimental import pallas as pl
from jax.experimental.pallas import tpu as pltpu

F32 = jnp.float32
BF16 = jnp.bfloat16

D_MODEL = 1024
HEAD_PAIR = 128
ATTN_W = 512
DILATIONS = (1, 4, 16)
BLK = 128
ROPE_HALF = 8
ROPE_THETA = 500000.0
SGU_W = 512
QKV_COLS = 4608
IN_COLS = 7680
D_FF = 2816
FF_TILE = 1408
EPS = 1e-6
N_DEV = 8
MASKED = -1e30

ADAM_LR = 0.001
ADAM_B1 = 0.9
ADAM_B2 = 0.999
ADAM_EPS = 1e-08
ADAM_WD = 0.01
ADAM_STEP = 10

MIB = 1024 * 1024
MESH = pl.DeviceIdType.MESH
ANY = pl.BlockSpec(memory_space=pl.ANY)


def _array(shape, dtype):
    return pltpu.HBM(tuple(shape), dtype)


PIN_BYTES = 4 * MIB


def _pin(x):
    if x.size * x.dtype.itemsize < PIN_BYTES:
        return x
    return pltpu.with_memory_space_constraint(x, pltpu.HBM)


def _pallas(body, **kwargs):
    call = pl.pallas_call(body, **kwargs)
    return lambda *args: call(*[_pin(a) for a in args])


def _params(sem, vmem_mib):
    return pltpu.CompilerParams(dimension_semantics=sem, vmem_limit_bytes=vmem_mib * MIB)


def _nt(a, b):
    return lax.dot_general(a, b, (((1,), (1,)), ((), ())), preferred_element_type=F32)


def _nn(a, b):
    return lax.dot_general(a, b, (((1,), (0,)), ((), ())), preferred_element_type=F32)


def _tn(a, b):
    return lax.dot_general(a, b, (((0,), (0,)), ((), ())), preferred_element_type=F32)


class _Exchange:
    def __init__(self, arrays, out_shapes, sem_shapes, phases):
        self.arrays, self.out_shapes, self.sem_shapes, self.phases = list(arrays), out_shapes, sem_shapes, phases

    def run_alone(self, name, vmem=False):
        n_in, n_out = len(self.arrays), len(self.out_shapes)

        def body(*refs):
            start, middle, finish = self.phases(refs[:n_in], refs[n_in:n_in + n_out], refs[n_in + n_out:])
            start()
            middle()
            finish()

        spec = pl.BlockSpec(memory_space=pltpu.VMEM) if vmem else ANY
        shapes = [jax.ShapeDtypeStruct(s.shape, s.dtype) for s in self.out_shapes] if vmem else self.out_shapes
        return (pl.pallas_call if vmem else _pallas)(
            body, name=name, out_shape=shapes, in_specs=[spec] * n_in, out_specs=[spec] * n_out,
            scratch_shapes=self.sem_shapes, compiler_params=pltpu.CompilerParams(vmem_limit_bytes=32 * MIB),
        )(*self.arrays)


def _call(body, *, name, grid, in_specs, out_specs, out_shape, args, vmem_mib, scratch_shapes=(), carried=None):
    n_in, n_out, n_scr = len(in_specs), len(out_specs), len(scratch_shapes)
    sem = ("arbitrary",) * len(grid)
    if carried is None:
        outs = _pallas(
            body, name=name, grid=grid, in_specs=in_specs, out_specs=out_specs, out_shape=out_shape,
            scratch_shapes=list(scratch_shapes), compiler_params=_params(sem, vmem_mib))(*args)
        return list(outs), []
    c_in, c_out = len(carried.arrays), len(carried.out_shapes)
    total = math.prod(grid)

    def full(*refs):
        own_in, car_in = refs[:n_in], refs[n_in:n_in + c_in]
        at = n_in + c_in
        own_out, car_out = refs[at:at + n_out], refs[at + n_out:at + n_out + c_out]
        at += n_out + c_out
        own_scr, sems = refs[at:at + n_scr], refs[at + n_scr:]
        step = pl.program_id(0)
        for axis in range(1, len(grid)):
            step = step * grid[axis] + pl.program_id(axis)
        start, middle, finish = carried.phases(car_in, car_out, sems)
        pl.when(step == 0)(start)
        pl.when(step == (3 * total) // 4)(middle)
        body(*own_in, *own_out, *own_scr)
        pl.when(step == total - 1)(finish)

    outs = _pallas(
        full, name=name, grid=grid, in_specs=list(in_specs) + [ANY] * c_in,
        out_specs=list(out_specs) + [ANY] * c_out, out_shape=list(out_shape) + list(carried.out_shapes),
        scratch_shapes=list(scratch_shapes) + list(carried.sem_shapes),
        compiler_params=_params(sem, vmem_mib))(*args, *carried.arrays)
    return list(outs[:n_out]), list(outs[n_out:])


def _all_gather(shards):
    n = len(shards)

    def phases(ins, outs, sems):
        send_sems, recv_sems, local_sems = sems
        x, y, c = lax.axis_index("x"), lax.axis_index("y"), lax.axis_index("c")
        me, sibling = (x, y, c), (x, y, 1 - c)
        chips = [(1 - x, y), (x, 1 - y), (1 - x, 1 - y)]

        def rows(m, px, py, pc):
            r = ins[m].shape[0]
            return outs[m].at[pl.ds((4 * px + 2 * py + pc) * r, r), :]

        def copy(m, k, block, to, src=None):
            return pltpu.make_async_remote_copy(
                src_ref=rows(m, *block) if src is None else src, dst_ref=rows(m, *block),
                send_sem=send_sems.at[m, k], recv_sem=recv_sems.at[m, k],
                device_id=to, device_id_type=MESH)

        def mine(m):
            return pltpu.make_async_copy(ins[m], rows(m, *me), local_sems.at[m])

        def first(m):
            return [copy(m, 0, me, sibling, src=ins[m])] + [
                copy(m, 1 + j, me, (*chip, c), src=ins[m]) for j, chip in enumerate(chips)]

        def passed(m):
            return [copy(m, 4 + j, (*chip, c), sibling) for j, chip in enumerate(chips)]

        def start():
            for m in range(n):
                mine(m).start()
            for m in range(n):
                for cp in first(m):
                    cp.start()

        def middle():
            for m in range(n):
                for j, chip in enumerate(chips):
                    copy(m, 1 + j, (*chip, c), me).wait_recv()
                    passed(m)[j].start()

        def finish():
            for m in range(n):
                copy(m, 0, sibling, me).wait_recv()
                for j, chip in enumerate(chips):
                    copy(m, 4 + j, (*chip, 1 - c), me).wait_recv()
            for m in range(n):
                for cp in first(m) + passed(m):
                    cp.wait_send()
                mine(m).wait()

        return start, middle, finish

    return _Exchange(
        shards, [_array((N_DEV * s.shape[0], s.shape[1]), s.dtype) for s in shards],
        [pltpu.SemaphoreType.DMA((n, 7)), pltpu.SemaphoreType.DMA((n, 7)), pltpu.SemaphoreType.DMA((n,))], phases)


def _pair_exchange(grads):
    n = len(grads)

    def phases(ins, outs, sems):
        send_sems, recv_sems = sems
        x, y, c = lax.axis_index("x"), lax.axis_index("y"), lax.axis_index("c")

        def copy(m):
            return pltpu.make_async_remote_copy(
                src_ref=ins[m].at[:, 1 - c], dst_ref=outs[m], send_sem=send_sems.at[m], recv_sem=recv_sems.at[m],
                device_id=(x, y, 1 - c), device_id_type=MESH)

        def start():
            for m in range(n):
                copy(m).start()

        def finish():
            for m in range(n):
                copy(m).wait()

        return start, lambda: None, finish

    return _Exchange(grads, [_array((4,) + g.shape[2:], g.dtype) for g in grads],
                     [pltpu.SemaphoreType.DMA((n,)), pltpu.SemaphoreType.DMA((n,))], phases)


def _chip_exchange(pair_sums):
    n = len(pair_sums)

    def phases(ins, outs, sems):
        send_sems, recv_sems = sems
        x, y, c = lax.axis_index("x"), lax.axis_index("y"), lax.axis_index("c")
        chips = [(1 - x, y), (x, 1 - y), (1 - x, 1 - y)]

        def copies():
            return [pltpu.make_async_remote_copy(
                src_ref=ins[m].at[2 * px + py], dst_ref=outs[m].at[j],
                send_sem=send_sems.at[m, j], recv_sem=recv_sems.at[m, j],
                device_id=(px, py, c), device_id_type=MESH)
                for m in range(n) for j, (px, py) in enumerate(chips)]

        def start():
            for cp in copies():
                cp.start()

        def finish():
            for cp in copies():
                cp.wait_recv()
            for cp in copies():
                cp.wait_send()

        return start, lambda: None, finish

    return _Exchange(pair_sums, [_array((3,) + p.shape[1:], p.dtype) for p in pair_sums],
                     [pltpu.SemaphoreType.DMA((n, 3)), pltpu.SemaphoreType.DMA((n, 3))], phases)


N_SLABS = ATTN_W // HEAD_PAIR


def _slab_scratch(tm, n=N_SLABS):
    return pltpu.VMEM((n, tm, HEAD_PAIR), F32)


def _rows_by_residue(dst_ref, slab_ref, r, tr, dtype, n=N_SLABS):
    for rho in range(r):
        for s in range(n):
            dst_ref[rho, :, s * HEAD_PAIR:(s + 1) * HEAD_PAIR] = (
                slab_ref[s, pl.ds(rho, tr, stride=r), :].astype(dtype))


def _rows_by_token(slab_ref, src_ref, r, tr, n=N_SLABS):
    for rho in range(r):
        for s in range(n):
            slab_ref[s, pl.ds(rho, tr, stride=r), :] = (
                src_ref[rho, :, s * HEAD_PAIR:(s + 1) * HEAD_PAIR].astype(F32))


def _rope_tables(pos_col, inv_freq_row, tm):
    T = pos_col.shape[0]

    def body(pos_ref, invf_ref, cos_ref, s1_ref, s2_ref):
        ang = pos_ref[...].astype(F32) * invf_ref[...]
        lane = lax.broadcasted_iota(jnp.int32, (1, HEAD_PAIR), 1) % 64
        cs, sn = jnp.cos(ang), jnp.sin(ang)
        cos_ref[...] = jnp.where(lane < 2 * ROPE_HALF, cs, 1.0)
        s1_ref[...] = jnp.where(lane < ROPE_HALF, -sn, 0.0)
        s2_ref[...] = jnp.where((lane >= ROPE_HALF) & (lane < 2 * ROPE_HALF), sn, 0.0)

    tab = _array((T, HEAD_PAIR), F32)
    row = pl.BlockSpec((tm, HEAD_PAIR), lambda i: (i, 0))
    return _pallas(
        body, name="rope_tables", grid=(T // tm,), out_shape=[tab] * 3,
        in_specs=[pl.BlockSpec((tm, 1), lambda i: (i, 0)), pl.BlockSpec((1, HEAD_PAIR), lambda i: (0, 0))],
        out_specs=[row] * 3, compiler_params=_params(("parallel",), 16),
    )(pos_col, inv_freq_row)


def _rope(y, cos, s1, s2):
    w = y.shape[1]
    rep = w // HEAD_PAIR
    return (y * jnp.tile(cos, (1, rep)) + pltpu.roll(y, w - ROPE_HALF, 1) * jnp.tile(s1, (1, rep))
            + pltpu.roll(y, ROPE_HALF, 1) * jnp.tile(s2, (1, rep)))


def _rope_transposed(dy, cos, s1, s2):
    w = dy.shape[1]
    rep = w // HEAD_PAIR
    return (dy * jnp.tile(cos, (1, rep)) + pltpu.roll(dy * jnp.tile(s1, (1, rep)), ROPE_HALF, 1)
            + pltpu.roll(dy * jnp.tile(s2, (1, rep)), w - ROPE_HALF, 1))


def _rmsnorm_fwd(x, g, name, tm, carried=None):
    T = x.shape[0]

    def body(x_ref, g_ref, h_ref):
        xf = x_ref[...]
        r = lax.rsqrt(jnp.mean(xf * xf, axis=-1, keepdims=True) + EPS)
        h_ref[...] = (xf * r * g_ref[...]).astype(BF16)

    return _call(
        body, name=name, grid=(T // tm,), out_shape=[_array((T, D_MODEL), BF16)],
        in_specs=[pl.BlockSpec((tm, D_MODEL), lambda i: (i, 0)), pl.BlockSpec((1, D_MODEL), lambda i: (0, 0))],
        out_specs=[pl.BlockSpec((tm, D_MODEL), lambda i: (i, 0))], vmem_mib=24, args=(x, g), carried=carried)


def _resident(shape, block=None):
    at = (0,) * len(shape) if block is None else block
    return pl.BlockSpec(shape, lambda *_: at, pipeline_mode=pl.Buffered(1))


def _qkv_proj(h, win_t, tables, tm, carried=None):
    T = h.shape[0]

    def body(h_ref, w_ref, cos_ref, s1_ref, s2_ref, o0, o1, o2, *slabs):
        hv = h_ref[...]
        cos, s1, s2 = cos_ref[...], s1_ref[...], s2_ref[...]
        for kind in range(3):
            for g, (o_ref, r) in enumerate(zip((o0, o1, o2), DILATIONS)):
                blk = 3 * kind + g
                y = _nt(hv, w_ref[blk * ATTN_W:(blk + 1) * ATTN_W, :])
                if kind < 2:
                    y = _rope(y, cos, s1, s2)
                if kind == 0:
                    y = y * 0.125
                cols = slice(kind * ATTN_W, (kind + 1) * ATTN_W)
                if r == 1:
                    o_ref[0, :, cols] = y.astype(BF16)
                    continue
                slab = slabs[blk % len(slabs)]
                for s in range(N_SLABS):
                    slab[s] = y[:, s * HEAD_PAIR:(s + 1) * HEAD_PAIR]
                for rho in range(r):
                    for s in range(N_SLABS):
                        at = kind * ATTN_W + s * HEAD_PAIR
                        o_ref[rho, :, at:at + HEAD_PAIR] = slab[s, pl.ds(rho, tm // r, stride=r), :].astype(BF16)

    row = pl.BlockSpec((tm, HEAD_PAIR), lambda i: (i, 0))
    return _call(
        body, name="qkv_proj", grid=(T // tm,),
        out_shape=[_array((r, T // r, 3 * ATTN_W), BF16) for r in DILATIONS],
        in_specs=[pl.BlockSpec((tm, D_MODEL), lambda i: (i, 0)), _resident((QKV_COLS, D_MODEL)), row, row, row],
        out_specs=[pl.BlockSpec((r, tm // r, 3 * ATTN_W), lambda i: (0, i, 0)) for r in DILATIONS],
        scratch_shapes=[_slab_scratch(tm)] * 3, vmem_mib=48, args=(h, win_t, *tables), carried=carried)


def _uv_gate_proj(h, win_t, tm, carried=None):
    T = h.shape[0]
    half = (IN_COLS - QKV_COLS) // 2

    def body(h_ref, wa_ref, wb_ref, o_ref):
        hv = h_ref[...]
        o_ref[:, :half] = _nt(hv, wa_ref[...]).astype(BF16)
        o_ref[:, half:] = _nt(hv, wb_ref[...]).astype(BF16)

    blk0 = QKV_COLS // half
    return _call(
        body, name="uv_gate_proj", grid=(T // tm,), out_shape=[_array((T, 2 * half), BF16)],
        in_specs=[pl.BlockSpec((tm, D_MODEL), lambda i: (i, 0)), _resident((half, D_MODEL), (blk0, 0)),
                  _resident((half, D_MODEL), (blk0 + 1, 0))],
        out_specs=[pl.BlockSpec((tm, 2 * half), lambda i: (i, 0))], vmem_mib=40, args=(h, win_t, win_t),
        carried=carried)


def _band_mask(n):
    row = lax.broadcasted_iota(jnp.int32, (2 * BLK, 2 * BLK), 0) & (BLK - 1)
    col = lax.broadcasted_iota(jnp.int32, (2 * BLK, 2 * BLK), 1)
    has_prev = (jnp.zeros_like(row) + n) > 0
    return ((col < BLK) & (col >= row) & has_prev) | ((col >= BLK) & (col - BLK <= row))


def _head_lanes():
    lane = lax.broadcasted_iota(jnp.int32, (1, HEAD_PAIR), 1)
    return lane < 64, lane >= 64


def _stack_heads(x, head0, head1):
    zero = jnp.zeros_like(x)
    return jnp.concatenate([jnp.where(head0, x, zero), jnp.where(head1, x, zero)], axis=0)


def _unstack_heads(y, head0):
    return jnp.where(head0, y[:BLK], y[BLK:])


def _per_head(stats, col):
    return jnp.concatenate([stats[:, col:col + 1], stats[:, col + 1:col + 2]], axis=0)


def _attn_sub_blocks(length):
    return 2 if length % (2 * BLK) == 0 else 1


def _attn_fwd(qkv, group, carried=None):
    r, L, _ = qkv.shape
    nsub = _attn_sub_blocks(L)
    rows = nsub * BLK

    def body(q_ref, kp_ref, kc_ref, vp_ref, vc_ref, o_ref, lse_ref):
        head0, head1 = _head_lanes()
        lane = lax.broadcasted_iota(jnp.int32, (1, HEAD_PAIR), 1)
        for j in range(nsub):
            cur = slice(j * BLK, (j + 1) * BLK)
            before = slice((j - 1) * BLK, j * BLK)
            valid = _band_mask(pl.program_id(1) + j)
            stats = jnp.zeros((BLK, HEAD_PAIR), F32)
            for p in range(N_SLABS):
                sl = slice(p * HEAD_PAIR, (p + 1) * HEAD_PAIR)
                k_prev, v_prev = (kp_ref[:, sl], vp_ref[:, sl]) if j == 0 else (kc_ref[before, sl], vc_ref[before, sl])
                k2 = jnp.concatenate([k_prev, kc_ref[cur, sl]], axis=0)
                v2 = jnp.concatenate([v_prev, vc_ref[cur, sl]], axis=0)
                s = jnp.where(valid, _nt(_stack_heads(q_ref[cur, sl], head0, head1), k2), MASKED)
                m = jnp.max(s, axis=1, keepdims=True)
                e = jnp.exp(s - m)
                den = jnp.sum(e, axis=1, keepdims=True)
                o_ref[cur, sl] = _unstack_heads(_nn(e.astype(BF16), v2) / den, head0)
                lse = m + jnp.log(den)
                stats = jnp.where(lane == 2 * p, lse[:BLK], jnp.where(lane == 2 * p + 1, lse[BLK:], stats))
            lse_ref[cur, :] = stats

    def cur_blk(kind, width=ATTN_W):
        return pl.BlockSpec((None, rows, width), lambda rho, n: (rho, n, kind))

    def prev_blk(kind):
        return pl.BlockSpec((None, BLK, ATTN_W), lambda rho, n: (rho, jnp.maximum(n * nsub - 1, 0), kind))

    return _call(
        body, name=f"attn_fwd_g{group}", grid=(r, L // rows),
        out_shape=[_array((r, L, ATTN_W), F32), _array((r, L, HEAD_PAIR), F32)],
        in_specs=[cur_blk(0), prev_blk(1), cur_blk(1), prev_blk(2), cur_blk(2)],
        out_specs=[cur_blk(0), cur_blk(0, HEAD_PAIR)],
        vmem_mib=24, args=(qkv, qkv, qkv, qkv, qkv), carried=carried)


def _attn_combine(outs, lses, tm):
    T = outs[0].shape[1]

    def body(o0, l0, o1, l1, o2, l2, attn_ref, attn_b_ref, lse_ref, so1, sl1, so2, sl2):
        for o_in, l_in, so, sl, r in ((o1, l1, so1, sl1, DILATIONS[1]), (o2, l2, so2, sl2, DILATIONS[2])):
            _rows_by_token(so, o_in, r, tm // r)
            _rows_by_token(sl, l_in, r, tm // r, n=1)
        head0, _ = _head_lanes()
        a0, a1, a2 = l0[0], sl1[0], sl2[0]
        mx = jnp.maximum(jnp.maximum(a0, a1), a2)
        e0, e1, e2 = jnp.exp(a0 - mx), jnp.exp(a1 - mx), jnp.exp(a2 - mx)
        tot = e0 + e1 + e2
        lse_ref[...] = mx + jnp.log(tot)
        w0, w1, w2 = e0 / tot, e1 / tot, e2 / tot
        for s in range(N_SLABS):
            lanes = slice(s * HEAD_PAIR, (s + 1) * HEAD_PAIR)

            def lanes_of(w):
                return jnp.where(head0, w[:, 2 * s:2 * s + 1], w[:, 2 * s + 1:2 * s + 2])

            mixed = lanes_of(w0) * o0[0, :, lanes] + lanes_of(w1) * so1[s] + lanes_of(w2) * so2[s]
            attn_ref[:, lanes] = mixed
            attn_b_ref[:, lanes] = mixed.astype(BF16)

    ins, specs = [], []
    for g, r in enumerate(DILATIONS):
        ins += [outs[g], lses[g]]
        specs += [pl.BlockSpec((r, tm // r, ATTN_W), lambda i: (0, i, 0)),
                  pl.BlockSpec((r, tm // r, HEAD_PAIR), lambda i: (0, i, 0))]
    return _pallas(
        body, name="attn_combine", grid=(T // tm,),
        out_shape=[_array((T, ATTN_W), F32), _array((T, ATTN_W), BF16), _array((T, HEAD_PAIR), F32)], in_specs=specs,
        out_specs=[pl.BlockSpec((tm, ATTN_W), lambda i: (i, 0)), pl.BlockSpec((tm, ATTN_W), lambda i: (i, 0)),
                   pl.BlockSpec((tm, HEAD_PAIR), lambda i: (i, 0))],
        scratch_shapes=[_slab_scratch(tm), _slab_scratch(tm, 1), _slab_scratch(tm), _slab_scratch(tm, 1)],
        compiler_params=_params(("parallel",), 32),
    )(*ins)


def _gelu(x):
    return 0.5 * x * (1.0 + lax.erf(x * (1.0 / math.sqrt(2.0))))


def _gelu_grad(x):
    return 0.5 * (1.0 + lax.erf(x * (1.0 / math.sqrt(2.0)))) + x * jnp.exp(-0.5 * x * x) * (1.0 / math.sqrt(2.0 * math.pi))


def _causal():
    row = lax.broadcasted_iota(jnp.int32, (BLK, BLK), 0)
    col = lax.broadcasted_iota(jnp.int32, (BLK, BLK), 1)
    return col <= row


def _bias_lanes(bt):
    grp = lax.broadcasted_iota(jnp.int32, (1, SGU_W), 1) // 64
    out = jnp.zeros((BLK, SGU_W), F32)
    for g in range(8):
        out = jnp.where(grp == g, bt[:, g:g + 1], out)
    return out


def _sgu_normalise(uv, ln_g, ln_b):
    z = _gelu(uv)
    u, v = z[:, :SGU_W], z[:, SGU_W:]
    mu = jnp.mean(v, axis=-1, keepdims=True)
    xc = v - mu
    rstd = lax.rsqrt(jnp.mean(xc * xc, axis=-1, keepdims=True) + EPS)
    xh = xc * rstd
    return u, xh, rstd, xh * ln_g + ln_b


def _sgu_mix(wc_ref, vb, head0):
    chunks = []
    for ch in range(vb.shape[0] // BLK):
        pairs = []
        for p in range(SGU_W // HEAD_PAIR):
            v_pair = vb[ch * BLK:(ch + 1) * BLK, p * HEAD_PAIR:(p + 1) * HEAD_PAIR]
            pairs.append(jnp.where(head0, _nn(wc_ref[2 * p], v_pair), _nn(wc_ref[2 * p + 1], v_pair)))
        chunks.append(jnp.concatenate(pairs, axis=1))
    return jnp.concatenate(chunks, axis=0)


def _sgu_fwd(uvg, ln_g, ln_b, w_s, b_t, tm):
    T = uvg.shape[0]

    def body(uv_ref, g_ref, b_ref, w_ref, bt_ref, o_ref, wc_ref, bias_ref):
        @pl.when(pl.program_id(0) == 0)
        def _():
            causal = _causal()
            for g in range(8):
                wc_ref[g] = jnp.where(causal, w_ref[g], 0.0).astype(BF16)
            bias_ref[...] = _bias_lanes(bt_ref[...])

        u, _, _, vn = _sgu_normalise(uv_ref[...].astype(F32), g_ref[...], b_ref[...])
        mixed = _sgu_mix(wc_ref, vn.astype(BF16), _head_lanes()[0])
        o_ref[...] = (u * (mixed + jnp.tile(bias_ref[...], (tm // BLK, 1)))).astype(BF16)

    vec = pl.BlockSpec((1, SGU_W), lambda i: (0, 0))
    return _pallas(
        body, name="sgu_fwd", grid=(T // tm,), out_shape=_array((T, SGU_W), BF16),
        in_specs=[pl.BlockSpec((tm, 2 * SGU_W), lambda i: (i, 0)), vec, vec,
                  pl.BlockSpec((8, BLK, BLK), lambda i: (0, 0, 0)), pl.BlockSpec((BLK, 8), lambda i: (0, 0))],
        out_specs=pl.BlockSpec((tm, SGU_W), lambda i: (i, 0)),
        scratch_shapes=[pltpu.VMEM((8, BLK, BLK), BF16), pltpu.VMEM((BLK, SGU_W), F32)],
        compiler_params=_params(("arbitrary",), 32),
    )(uvg, ln_g, ln_b, w_s, b_t)


def _merge_fwd(attn, sgu, wpa_t, wps_t, uvg, tm):
    T = attn.shape[0]

    def body(attn_ref, sgu_ref, wpa_ref, wps_ref, ga_ref, gb_ref, pa_ref, ps_ref, m_ref):
        pa = _nt(attn_ref[...], wpa_ref[...])
        ps = _nt(sgu_ref[...], wps_ref[...])
        pa_ref[...] = pa.astype(BF16)
        ps_ref[...] = ps.astype(BF16)
        ga, gb = ga_ref[...].astype(F32), gb_ref[...].astype(F32)
        m_ref[...] = (jax.nn.sigmoid(ga) * pa + jax.nn.sigmoid(gb) * ps).astype(BF16)

    half = pl.BlockSpec((tm, ATTN_W), lambda i: (i, 0))
    wide = pl.BlockSpec((tm, D_MODEL), lambda i: (i, 0))
    w = pl.BlockSpec((D_MODEL, ATTN_W), lambda i: (0, 0))
    res = _array((T, D_MODEL), BF16)
    return _pallas(
        body, name="merge_fwd", grid=(T // tm,),
        out_shape=[res, res, _array((T, D_MODEL), BF16)],
        in_specs=[half, half, w, w, pl.BlockSpec((tm, D_MODEL), lambda i: (i, 1)),
                  pl.BlockSpec((tm, D_MODEL), lambda i: (i, 2))],
        out_specs=[wide, wide, wide], compiler_params=_params(("parallel",), 40),
    )(attn, sgu, wpa_t, wps_t, uvg, uvg)


def _out_residual(merged, w_out, x, g2, tm):
    T = x.shape[0]

    def body(m_ref, w_ref, x_ref, g_ref, o_ref, h_ref):
        x1 = x_ref[...] + _nn(m_ref[...], w_ref[...])
        o_ref[...] = x1
        r = lax.rsqrt(jnp.mean(x1 * x1, axis=-1, keepdims=True) + EPS)
        h_ref[...] = (x1 * r * g_ref[...]).astype(BF16)

    wide = pl.BlockSpec((tm, D_MODEL), lambda i: (i, 0))
    return _pallas(
        body, name="out_residual", grid=(T // tm,),
        out_shape=[_array((T, D_MODEL), F32), _array((T, D_MODEL), BF16)],
        in_specs=[wide, _resident((D_MODEL, D_MODEL)), wide, pl.BlockSpec((1, D_MODEL), lambda i: (0, 0))],
        out_specs=[wide, wide], compiler_params=_params(("parallel",), 32),
    )(merged, w_out, x, g2)


def _ffn_fwd(h2, wg_t, wu_t, tm):
    T = h2.shape[0]

    def body(h_ref, wg_ref, wu_ref, gate_ref, up_ref, ff_ref):
        h = h_ref[...]
        for j in range(D_FF // FF_TILE):
            cols = slice(j * FF_TILE, (j + 1) * FF_TILE)
            gate, up = _nt(h, wg_ref[cols, :]), _nt(h, wu_ref[cols, :])
            gate_ref[:, cols] = gate.astype(BF16)
            up_ref[:, cols] = up.astype(BF16)
            ff_ref[:, cols] = (gate * jax.nn.sigmoid(gate) * up).astype(BF16)

    w = _resident((D_FF, D_MODEL))
    o = pl.BlockSpec((tm, D_FF), lambda i: (i, 0))
    res = _array((T, D_FF), BF16)
    return _pallas(
        body, name="ffn_fwd", grid=(T // tm,), out_shape=[res, res, res],
        in_specs=[pl.BlockSpec((tm, D_MODEL), lambda i: (i, 0)), w, w], out_specs=[o, o, o],
        compiler_params=_params(("parallel",), 52),
    )(h2, wg_t, wu_t)


def _down_loss(ff, w_down, x1, final_g, target, tm):
    T = x1.shape[0]

    def body(ff_ref, w_ref, x1_ref, g_ref, t_ref, dx_ref, dxb_ref, loss_ref, dg_ref):
        @pl.when(pl.program_id(0) == 0)
        def _():
            loss_ref[...] = jnp.zeros_like(loss_ref)
            dg_ref[...] = jnp.zeros_like(dg_ref)

        x2 = x1_ref[...] + _nn(ff_ref[...], w_ref[...])
        g = g_ref[...]
        r = lax.rsqrt(jnp.mean(x2 * x2, axis=-1, keepdims=True) + EPS)
        xh = x2 * r
        err = xh * g - t_ref[...]
        loss_ref[...] += jnp.sum(err * err, axis=0, keepdims=True) * (0.5 / D_MODEL)
        dy = err * (1.0 / D_MODEL)
        dg_ref[...] += jnp.sum(dy * xh, axis=0, keepdims=True)
        dxh = dy * g
        dx = r * (dxh - xh * jnp.mean(dxh * xh, axis=-1, keepdims=True))
        dx_ref[...] = dx
        dxb_ref[...] = dx.astype(BF16)

    wide = pl.BlockSpec((tm, D_MODEL), lambda i: (i, 0))
    vec = pl.BlockSpec((1, D_MODEL), lambda i: (0, 0))
    vec_shape = _array((1, D_MODEL), F32)
    return _pallas(
        body, name="down_loss", grid=(T // tm,),
        out_shape=[_array((T, D_MODEL), F32), _array((T, D_MODEL), BF16),
                   vec_shape, vec_shape],
        in_specs=[pl.BlockSpec((tm, D_FF), lambda i: (i, 0)), _resident((D_FF, D_MODEL)), wide, vec, wide],
        out_specs=[wide, wide, vec, vec],
        compiler_params=_params(("arbitrary",), 40),
    )(ff, w_down, x1, final_g, target)


def _rmsnorm_bwd(dh, xin, g, d_res, dg_ref):
    r = lax.rsqrt(jnp.mean(xin * xin, axis=-1, keepdims=True) + EPS)
    xh = xin * r
    dg_ref[...] += jnp.sum(dh * xh, axis=0, keepdims=True)
    dxh = dh * g
    return d_res + r * (dxh - xh * jnp.mean(dxh * xh, axis=-1, keepdims=True))


def _ffn_bwd(dx2, dx2b, w_down, wg_t, wu_t, gate, up, x1, g2, tm):
    T = dx2.shape[0]

    def body(dx_ref, dxb_ref, wd_ref, wg_ref, wu_ref, gate_ref, up_ref, x1_ref, g_ref,
             dgate_ref, dup_ref, dx1_ref, dx1b_ref, dg_ref):
        @pl.when(pl.program_id(0) == 0)
        def _():
            dg_ref[...] = jnp.zeros_like(dg_ref)

        dxb = dxb_ref[...]
        dh = jnp.zeros((tm, D_MODEL), F32)
        for j in range(D_FF // FF_TILE):
            cols = slice(j * FF_TILE, (j + 1) * FF_TILE)
            dff = _nt(dxb, wd_ref[cols, :])
            gate, up = gate_ref[:, cols].astype(F32), up_ref[:, cols].astype(F32)
            sg = jax.nn.sigmoid(gate)
            dgate = (dff * up * sg * (1.0 + gate * (1.0 - sg))).astype(BF16)
            dup = (dff * gate * sg).astype(BF16)
            dgate_ref[:, cols] = dgate
            dup_ref[:, cols] = dup
            dh += _nn(dgate, wg_ref[cols, :]) + _nn(dup, wu_ref[cols, :])
        dx1 = _rmsnorm_bwd(dh, x1_ref[...], g_ref[...], dx_ref[...], dg_ref)
        dx1_ref[...] = dx1
        dx1b_ref[...] = dx1.astype(BF16)

    wide = pl.BlockSpec((tm, D_MODEL), lambda i: (i, 0))
    ffw = pl.BlockSpec((tm, D_FF), lambda i: (i, 0))
    vec = pl.BlockSpec((1, D_MODEL), lambda i: (0, 0))
    w = _resident((D_FF, D_MODEL))
    ff_shape = _array((T, D_FF), BF16)
    return _pallas(
        body, name="ffn_bwd", grid=(T // tm,),
        out_shape=[ff_shape, ff_shape, _array((T, D_MODEL), F32),
                   _array((T, D_MODEL), BF16), _array((1, D_MODEL), F32)],
        in_specs=[wide, wide, w, w, w, ffw, ffw, wide, vec],
        out_specs=[ffw, ffw, wide, wide, vec], compiler_params=_params(("arbitrary",), 56),
    )(dx2, dx2b, w_down, wg_t, wu_t, gate, up, x1, g2)


def _mm_tn(a, b, *, tmm, tk, name, into=None, row_block0=0, rows_total=None):
    T, M = a.shape
    N = b.shape[1]
    rows_total = M if rows_total is None else rows_total

    def body(*refs):
        a_ref, b_ref, o_ref = refs[0], refs[1], refs[-1]

        @pl.when(pl.program_id(1) == 0)
        def _():
            o_ref[...] = jnp.zeros_like(o_ref)

        o_ref[...] += _tn(a_ref[...], b_ref[...])

    ins = [a, b] + ([] if into is None else [into])
    specs = [pl.BlockSpec((tk, tmm), lambda i, k: (k, i)), pl.BlockSpec((tk, N), lambda i, k: (k, 0))]
    return _pallas(
        body, name=name, grid=(M // tmm, T // tk),
        out_shape=_array((rows_total, N), F32),
        in_specs=specs + ([] if into is None else [ANY]),
        out_specs=pl.BlockSpec((tmm, N), lambda i, k: (row_block0 + i, 0)),
        input_output_aliases={} if into is None else {2: 0},
        compiler_params=_params(("parallel", "arbitrary"), 48),
    )(*ins)


def _in_bwd(dqkv, duv, dgates, win_t, x, g1, dx1, tm, tile0, ntiles, name, so_far=None, carried=None):
    T = x.shape[0]
    uv0, gates0 = QKV_COLS, QKV_COLS + 2 * SGU_W

    def body(dq_ref, du_ref, dgt_ref, w_ref, x_ref, g_ref, d_ref, *rest):
        dx_ref, dg_ref = rest[-2:]

        @pl.when(pl.program_id(0) == 0)
        def _():
            dg_ref[...] = jnp.zeros_like(dg_ref) if so_far is None else rest[1][...]

        dh = (_nn(dq_ref[...], w_ref[:uv0, :]) + _nn(du_ref[...], w_ref[uv0:gates0, :])
              + _nn(dgt_ref[...], w_ref[gates0:, :]))
        dx_ref[...] = _rmsnorm_bwd(dh, x_ref[...], g_ref[...], d_ref[...], dg_ref)

    def cols(n):
        return pl.BlockSpec((tm, n), lambda i: (tile0 + i, 0))

    wide = cols(D_MODEL)
    vec = pl.BlockSpec((1, D_MODEL), lambda i: (0, 0))
    in_specs = [cols(QKV_COLS), cols(2 * SGU_W), cols(2 * D_MODEL), _resident((IN_COLS, D_MODEL)), wide, vec, wide]
    args = (dqkv, duv, dgates, win_t, x, g1, dx1)
    if so_far is None:
        return _call(
            body, name=name, grid=(ntiles,), out_shape=[_array((T, D_MODEL), F32), _array((1, D_MODEL), F32)],
            in_specs=in_specs, out_specs=[wide, vec], vmem_mib=56, args=args, carried=carried)
    return _pallas(
        body, name=name, grid=(ntiles,), out_shape=[_array((T, D_MODEL), F32), _array((1, D_MODEL), F32)],
        in_specs=in_specs + [ANY, vec], out_specs=[wide, vec], input_output_aliases={len(in_specs): 0},
        compiler_params=_params(("arbitrary",), 56),
    )(*args, *so_far)


def _merge_bwd(dx1b, w_out, pa, ps, uvg, tm, carried=None):
    T = dx1b.shape[0]

    def body(dx_ref, w_ref, pa_ref, ps_ref, ga_ref, gb_ref, dpa_ref, dps_ref, dg_ref):
        dm = _nt(dx_ref[...], w_ref[...])
        ga, gb = jax.nn.sigmoid(ga_ref[...].astype(F32)), jax.nn.sigmoid(gb_ref[...].astype(F32))
        dpa_ref[...] = (dm * ga).astype(BF16)
        dps_ref[...] = (dm * gb).astype(BF16)
        dg_ref[:, :D_MODEL] = (dm * pa_ref[...].astype(F32) * ga * (1.0 - ga)).astype(BF16)
        dg_ref[:, D_MODEL:] = (dm * ps_ref[...].astype(F32) * gb * (1.0 - gb)).astype(BF16)

    wide = pl.BlockSpec((tm, D_MODEL), lambda i: (i, 0))
    res = _array((T, D_MODEL), BF16)
    return _call(
        body, name="merge_bwd", grid=(T // tm,),
        out_shape=[res, res, _array((T, 2 * D_MODEL), BF16)],
        in_specs=[wide, _resident((D_MODEL, D_MODEL)), wide, wide,
                  pl.BlockSpec((tm, D_MODEL), lambda i: (i, 1)), pl.BlockSpec((tm, D_MODEL), lambda i: (i, 2))],
        out_specs=[wide, wide, pl.BlockSpec((tm, 2 * D_MODEL), lambda i: (i, 0))],
        vmem_mib=48, args=(dx1b, w_out, pa, ps, uvg, uvg), carried=carried)


def _proj_bwd(dpa, dps, wpa_t, wps_t, tm):
    T = dpa.shape[0]

    def body(dpa_ref, dps_ref, wpa_ref, wps_ref, dattn_ref, dsgu_ref):
        dattn_ref[...] = _nn(dpa_ref[...], wpa_ref[...])
        dsgu_ref[...] = _nn(dps_ref[...], wps_ref[...])

    wide = pl.BlockSpec((tm, D_MODEL), lambda i: (i, 0))
    half = pl.BlockSpec((tm, ATTN_W), lambda i: (i, 0))
    w = pl.BlockSpec((D_MODEL, ATTN_W), lambda i: (0, 0))
    res = _array((T, ATTN_W), F32)
    return _pallas(
        body, name="proj_bwd", grid=(T // tm,), out_shape=[res, res], in_specs=[wide, wide, w, w],
        out_specs=[half, half], compiler_params=_params(("parallel",), 32),
    )(dpa, dps, wpa_t, wps_t)


def _sgu_bwd(uvg, dsgu, ln_g, ln_b, w_s, b_t, tm, carried=None):
    T = uvg.shape[0]
    nsteps = T // tm

    def body(uv_ref, ds_ref, g_ref, b_ref, w_ref, bt_ref, duv_ref, dw_ref, dbt_ref, dg_ref, db_ref,
             wc_ref, wct_ref, bias_ref, dbias_ref):
        step = pl.program_id(0)
        head0, head1 = _head_lanes()

        @pl.when(step == 0)
        def _():
            causal = _causal()
            for g in range(8):
                wc = jnp.where(causal, w_ref[g], 0.0)
                wc_ref[g] = wc.astype(BF16)
                wct_ref[g] = wc.T.astype(BF16)
            bias_ref[...] = _bias_lanes(bt_ref[...])
            dbias_ref[...] = jnp.zeros_like(dbias_ref)
            dw_ref[...] = jnp.zeros_like(dw_ref)
            dg_ref[...] = jnp.zeros_like(dg_ref)
            db_ref[...] = jnp.zeros_like(db_ref)

        uv = uv_ref[...].astype(F32)
        ln_gain = g_ref[...]
        u, xh, rstd, vn = _sgu_normalise(uv, ln_gain, b_ref[...])
        vb = vn.astype(BF16)
        mixed = _sgu_mix(wc_ref, vb, head0) + jnp.tile(bias_ref[...], (tm // BLK, 1))
        dout = ds_ref[...]
        du = dout * mixed
        dmixed = dout * u
        dmb = dmixed.astype(BF16)
        dvn_chunks = []
        for ch in range(tm // BLK):
            rows = slice(ch * BLK, (ch + 1) * BLK)
            dbias_ref[...] += dmixed[rows]
            pairs = []
            for p in range(SGU_W // HEAD_PAIR):
                lanes = slice(p * HEAD_PAIR, (p + 1) * HEAD_PAIR)
                dm_pair, v_pair = dmb[rows, lanes], vb[rows, lanes]
                acc = jnp.zeros((BLK, HEAD_PAIR), F32)
                for hh, half in enumerate((head0, head1)):
                    dm_h = jnp.where(half, dm_pair, jnp.zeros_like(dm_pair))
                    dw_ref[2 * p + hh] += _nt(dm_h, v_pair)
                    acc += _nn(wct_ref[2 * p + hh], dm_h)
                pairs.append(acc)
            dvn_chunks.append(jnp.concatenate(pairs, axis=1))
        dvn = jnp.concatenate(dvn_chunks, axis=0)
        dg_ref[...] += jnp.sum(dvn * xh, axis=0, keepdims=True)
        db_ref[...] += jnp.sum(dvn, axis=0, keepdims=True)
        dxh = dvn * ln_gain
        dv = rstd * (dxh - jnp.mean(dxh, axis=-1, keepdims=True) - xh * jnp.mean(dxh * xh, axis=-1, keepdims=True))
        dgelu = _gelu_grad(uv)
        duv_ref[:, :SGU_W] = (du * dgelu[:, :SGU_W]).astype(BF16)
        duv_ref[:, SGU_W:] = (dv * dgelu[:, SGU_W:]).astype(BF16)

        @pl.when(step == nsteps - 1)
        def _():
            causal = _causal()
            for g in range(8):
                dw_ref[g] = jnp.where(causal, dw_ref[g], 0.0)
            grp = lax.broadcasted_iota(jnp.int32, (1, SGU_W), 1) // 64
            col = lax.broadcasted_iota(jnp.int32, (1, 8), 1)
            dbias = dbias_ref[...]
            out = jnp.zeros((BLK, 8), F32)
            for g in range(8):
                s = jnp.sum(jnp.where(grp == g, dbias, 0.0), axis=1, keepdims=True)
                out = jnp.where(col == g, s, out)
            dbt_ref[...] = out

    vec = pl.BlockSpec((1, SGU_W), lambda i: (0, 0))
    w3 = pl.BlockSpec((8, BLK, BLK), lambda i: (0, 0, 0))
    bt = pl.BlockSpec((BLK, 8), lambda i: (0, 0))
    return _call(
        body, name="sgu_bwd", grid=(nsteps,),
        out_shape=[_array((T, 2 * SGU_W), BF16), _array((8, BLK, BLK), F32),
                   _array((BLK, 8), F32), _array((1, SGU_W), F32),
                   _array((1, SGU_W), F32)],
        in_specs=[pl.BlockSpec((tm, 2 * SGU_W), lambda i: (i, 0)), pl.BlockSpec((tm, SGU_W), lambda i: (i, 0)),
                  vec, vec, w3, bt],
        out_specs=[pl.BlockSpec((tm, 2 * SGU_W), lambda i: (i, 0)), w3, bt, vec, vec],
        scratch_shapes=[pltpu.VMEM((8, BLK, BLK), BF16), pltpu.VMEM((8, BLK, BLK), BF16),
                        pltpu.VMEM((BLK, SGU_W), F32), pltpu.VMEM((BLK, SGU_W), F32)],
        vmem_mib=40, args=(uvg, dsgu, ln_g, ln_b, w_s, b_t), carried=carried)


D_LANE0 = 8


def _attn_bwd_prepare(dattn, attn, lse, tm):
    T = dattn.shape[0]

    def body(da_ref, at_ref, lse_ref, *outs):
        scr_da, scr_st = outs[-2:]
        outs = outs[:-2]
        head0, _ = _head_lanes()
        lane = lax.broadcasted_iota(jnp.int32, (1, HEAD_PAIR), 1)
        stats = lse_ref[...]
        for s in range(N_SLABS):
            lanes = slice(s * HEAD_PAIR, (s + 1) * HEAD_PAIR)
            da = da_ref[:, lanes]
            pp = da * at_ref[:, lanes]
            d0 = jnp.sum(jnp.where(head0, pp, 0.0), axis=1, keepdims=True)
            d1 = jnp.sum(jnp.where(head0, 0.0, pp), axis=1, keepdims=True)
            stats = jnp.where(lane == D_LANE0 + 2 * s, d0, jnp.where(lane == D_LANE0 + 2 * s + 1, d1, stats))
            scr_da[s] = da
            outs[0][0, :, lanes] = da.astype(BF16)
        scr_st[0] = stats
        outs[1][0] = stats
        for g, r in enumerate(DILATIONS):
            if r > 1:
                _rows_by_residue(outs[2 * g], scr_da, r, tm // r, BF16)
                _rows_by_residue(outs[2 * g + 1], scr_st, r, tm // r, F32, n=1)

    tok = pl.BlockSpec((tm, ATTN_W), lambda i: (i, 0))
    shapes, specs = [], []
    for r in DILATIONS:
        shapes += [_array((r, T // r, ATTN_W), BF16), _array((r, T // r, HEAD_PAIR), F32)]
        specs += [pl.BlockSpec((r, tm // r, ATTN_W), lambda i: (0, i, 0)),
                  pl.BlockSpec((r, tm // r, HEAD_PAIR), lambda i: (0, i, 0))]
    return _pallas(
        body, name="attn_bwd_prepare", grid=(T // tm,), out_shape=shapes,
        in_specs=[tok, tok, pl.BlockSpec((tm, HEAD_PAIR), lambda i: (i, 0))],
        out_specs=specs, scratch_shapes=[_slab_scratch(tm), _slab_scratch(tm, 1)],
        compiler_params=_params(("parallel",), 40),
    )(dattn, attn, lse)


def _attn_bwd(qkv, dattn, stats, group, carried=None):
    r, L, _ = qkv.shape
    nsub = _attn_sub_blocks(L)
    rows = nsub * BLK
    nb = L // rows
    keep = rows - BLK

    def body(q_ref, kp_ref, kc_ref, vp_ref, vc_ref, da_ref, st_ref, dq_ref, dk_ref, dv_ref, carry_k, carry_v):
        n = pl.program_id(1)

        @pl.when(n == 0)
        def _():
            carry_k[...] = jnp.zeros_like(carry_k)
            carry_v[...] = jnp.zeros_like(carry_v)

        @pl.when(n < nb)
        def _():
            head0, head1 = _head_lanes()
            for p in range(N_SLABS):
                sl = slice(p * HEAD_PAIR, (p + 1) * HEAD_PAIR)
                dk_parts = [jnp.zeros((BLK, HEAD_PAIR), F32) for _ in range(nsub + 1)]
                dv_parts = [jnp.zeros((BLK, HEAD_PAIR), F32) for _ in range(nsub + 1)]
                for j in range(nsub):
                    cur = slice(j * BLK, (j + 1) * BLK)
                    before = slice((j - 1) * BLK, j * BLK)
                    valid = _band_mask(n + j)
                    st = st_ref[cur, :]
                    k_prev, v_prev = (kp_ref[:, sl], vp_ref[:, sl]) if j == 0 else (kc_ref[before, sl], vc_ref[before, sl])
                    k2 = jnp.concatenate([k_prev, kc_ref[cur, sl]], axis=0)
                    v2 = jnp.concatenate([v_prev, vc_ref[cur, sl]], axis=0)
                    qs = _stack_heads(q_ref[cur, sl], head0, head1)
                    das = _stack_heads(da_ref[cur, sl], head0, head1)
                    prob = jnp.where(valid, jnp.exp(_nt(qs, k2) - _per_head(st, 2 * p)), 0.0)
                    ds = (prob * (_nt(das, v2) - _per_head(st, D_LANE0 + 2 * p))).astype(BF16)
                    dk2 = _tn(ds, qs)
                    dv2 = _tn(prob.astype(BF16), das)
                    dq_ref[cur, sl] = _unstack_heads(_nn(ds, k2), head0)
                    dk_parts[j] += dk2[:BLK]
                    dk_parts[j + 1] += dk2[BLK:]
                    dv_parts[j] += dv2[:BLK]
                    dv_parts[j + 1] += dv2[BLK:]
                if keep:
                    dk_ref[:keep, sl] = carry_k[:keep, sl]
                    dv_ref[:keep, sl] = carry_v[:keep, sl].astype(BF16)
                dk_ref[keep:, sl] = carry_k[keep:, sl] + dk_parts[0]
                dv_ref[keep:, sl] = (carry_v[keep:, sl] + dv_parts[0]).astype(BF16)
                for j in range(nsub):
                    carry_k[j * BLK:(j + 1) * BLK, sl] = dk_parts[j + 1]
                    carry_v[j * BLK:(j + 1) * BLK, sl] = dv_parts[j + 1]

        @pl.when(n == nb)
        def _():
            dk_ref[...] = carry_k[...]
            dv_ref[...] = carry_v[...].astype(BF16)

    def cur_blk(kind, width=ATTN_W):
        return pl.BlockSpec((None, rows, width), lambda rho, n: (rho, jnp.minimum(n, nb - 1), kind))

    def last_blk(kind):
        return pl.BlockSpec((None, rows, ATTN_W), lambda rho, n: (rho, jnp.clip(n - 1, 0, nb - 1), kind))

    def prev_keys(kind):
        return pl.BlockSpec((None, BLK, ATTN_W),
                            lambda rho, n: (rho, jnp.clip(n * nsub - 1, 0, nb * nsub - 1), kind))

    res = _array((r, L, ATTN_W), F32)
    return _call(
        body, name=f"attn_bwd_g{group}", grid=(r, nb + 1), out_shape=[res, res, _array((r, L, ATTN_W), BF16)],
        in_specs=[cur_blk(0), prev_keys(1), cur_blk(1), prev_keys(2), cur_blk(2), cur_blk(0), cur_blk(0, HEAD_PAIR)],
        out_specs=[cur_blk(0), last_blk(0), last_blk(0)],
        scratch_shapes=[pltpu.VMEM((rows, ATTN_W), F32), pltpu.VMEM((rows, ATTN_W), F32)],
        vmem_mib=32, args=(qkv, qkv, qkv, qkv, qkv, dattn, stats), carried=carried)


def _dqkv_token_order(dqkv_groups, tables, tm):
    T = tables[0].shape[0]

    def body(*refs):
        ins = refs[:9]
        cos_ref, s1_ref, s2_ref, o_ref, scr = refs[9:]
        cos, s1, s2 = cos_ref[...], s1_ref[...], s2_ref[...]
        for g, r in enumerate(DILATIONS):
            for kind in range(3):
                src = ins[3 * g + kind]
                if r > 1:
                    _rows_by_token(scr, src, r, tm // r)
                for s in range(N_SLABS):
                    val = scr[s] if r > 1 else src[0, :, s * HEAD_PAIR:(s + 1) * HEAD_PAIR]
                    if kind < 2:
                        val = _rope_transposed(val, cos, s1, s2)
                    if kind == 0:
                        val = val * 0.125
                    at = (3 * kind + g) * ATTN_W + s * HEAD_PAIR
                    o_ref[:, at:at + HEAD_PAIR] = val.astype(BF16)

    specs = []
    for r in DILATIONS:
        specs += [pl.BlockSpec((r, tm // r, ATTN_W), lambda i: (0, i, 0))] * 3
    row = pl.BlockSpec((tm, HEAD_PAIR), lambda i: (i, 0))
    flat = [a for grp in dqkv_groups for a in grp]
    return _pallas(
        body, name="dqkv_token_order", grid=(T // tm,), out_shape=_array((T, QKV_COLS), BF16),
        in_specs=specs + [row] * 3, out_specs=pl.BlockSpec((tm, QKV_COLS), lambda i: (i, 0)),
        scratch_shapes=[_slab_scratch(tm)], compiler_params=_params(("parallel",), 48),
    )(*flat, *tables)


def _row_tile(rows):
    for cand in (320, 256, 176, 128):
        if rows % cand == 0:
            return cand
    return rows


def _pair_sum(grad4, recv, chip, name):
    _, _, rows, cols = grad4.shape
    tr = _row_tile(rows)

    def body(ids_ref, g_ref, r_ref, gown_ref, rown_ref, sum_ref, own_ref):
        sum_ref[...] = (g_ref[...] + r_ref[...]).astype(BF16)

        @pl.when(pl.program_id(1) == 0)
        def _():
            own_ref[...] = gown_ref[...] + rown_ref[...]

    grid_spec = pltpu.PrefetchScalarGridSpec(
        num_scalar_prefetch=1, grid=(rows // tr, 4),
        in_specs=[pl.BlockSpec((None, None, tr, cols), lambda i, q, ids: (q, ids[1], i, 0)),
                  pl.BlockSpec((None, tr, cols), lambda i, q, ids: (q, i, 0)),
                  pl.BlockSpec((None, None, tr, cols), lambda i, q, ids: (ids[0], ids[1], i, 0)),
                  pl.BlockSpec((None, tr, cols), lambda i, q, ids: (ids[0], i, 0))],
        out_specs=[pl.BlockSpec((None, tr, cols), lambda i, q, ids: (q, i, 0)),
                   pl.BlockSpec((tr, cols), lambda i, q, ids: (i, 0))])
    return _pallas(
        body, name=name, grid_spec=grid_spec,
        out_shape=[_array((4, rows, cols), BF16), _array((rows, cols), F32)],
        compiler_params=_params(("arbitrary", "arbitrary"), 32),
    )(chip, grad4, recv, grad4, recv)


def _chip_sum(own, others, name):
    rows, cols = own.shape
    tr = _row_tile(rows)

    def body(own_ref, oth_ref, o_ref):
        total = own_ref[...]
        for j in range(3):
            total = total + oth_ref[j].astype(F32)
        o_ref[...] = total

    blk = pl.BlockSpec((tr, cols), lambda i: (i, 0))
    return _pallas(
        body, name=name, grid=(rows // tr,), out_shape=_array((rows, cols), F32),
        in_specs=[blk, pl.BlockSpec((3, tr, cols), lambda i: (0, i, 0))], out_specs=blk,
        compiler_params=_params(("parallel",), 32),
    )(own, others)


def _adam_math(w, g, m, v):
    m = ADAM_B1 * m + (1.0 - ADAM_B1) * g
    v = ADAM_B2 * v + (1.0 - ADAM_B2) * (g * g)
    m_hat = m / (1.0 - ADAM_B1 ** ADAM_STEP)
    v_hat = v / (1.0 - ADAM_B2 ** ADAM_STEP)
    delta = -ADAM_LR * (m_hat / (jnp.sqrt(v_hat) + ADAM_EPS) + ADAM_WD * w)
    return delta, m, v


def _adamw(w, g, m, v, name):
    rows, cols = w.shape
    tr = _row_tile(rows)

    def body(w_ref, g_ref, m_ref, v_ref, d_ref, nm_ref, nv_ref):
        d_ref[...], nm_ref[...], nv_ref[...] = _adam_math(w_ref[...], g_ref[...], m_ref[...], v_ref[...])

    blk = pl.BlockSpec((tr, cols), lambda i: (i, 0))
    res = _array((rows, cols), F32)
    return _pallas(
        body, name=name, grid=(rows // tr,), out_shape=[res, res, res], in_specs=[blk] * 4, out_specs=[blk] * 3,
        compiler_params=_params(("parallel",), 32),
    )(w, g, m, v)


def _small_update(late_parts, early_parts, w, m, v, loss_rows):
    rows = w.shape[0]

    def total(p_ref):
        n = p_ref.shape[0] // N_DEV
        acc = p_ref[0:n, :]
        for dev in range(1, N_DEV):
            acc = acc + p_ref[dev * n:(dev + 1) * n, :]
        return acc

    def body(late_ref, early_ref, w_ref, m_ref, v_ref, g_ref, d_ref, nm_ref, nv_ref, loss_ref):
        g = jnp.concatenate([total(late_ref), total(early_ref)], axis=0)
        g_ref[...] = g
        d_ref[...], nm_ref[...], nv_ref[...] = _adam_math(w_ref[...], g, m_ref[...], v_ref[...])
        loss_ref[...] = jnp.sum(jnp.sum(g[rows - loss_rows:, :], axis=1, keepdims=True), axis=0, keepdims=True)

    res = jax.ShapeDtypeStruct((rows, HEAD_PAIR), F32)
    return _pallas(
        body, name="small_update", out_shape=[res, res, res, res, jax.ShapeDtypeStruct((1, 1), F32)],
        compiler_params=pltpu.CompilerParams(vmem_limit_bytes=32 * MIB),
    )(late_parts, early_parts, w, m, v)


def kernel(x, positions, norm1_g, w_in, sgu_ln_g, sgu_ln_b, w_spatial, b_spatial, w_proj_attn, w_proj_sgu, w_out, norm2_g, w_ffn_gate, w_ffn_up, w_ffn_down, final_g, loss_target, m_norm1_g, m_w_in, m_sgu_ln_g, m_sgu_ln_b, m_w_spatial, m_b_spatial, m_w_proj_attn, m_w_proj_sgu, m_w_out, m_norm2_g, m_w_ffn_gate, m_w_ffn_up, m_w_ffn_down, m_final_g, v_norm1_g, v_w_in, v_sgu_ln_g, v_sgu_ln_b, v_w_spatial, v_b_spatial, v_w_proj_attn, v_w_proj_sgu, v_w_out, v_norm2_g, v_w_ffn_gate, v_w_ffn_up, v_w_ffn_down, v_final_g):
    T = x.shape[1]
    tm = 512
    xt = x[0]
    target = loss_target[0]
    chip = jnp.stack([2 * lax.axis_index("x") + lax.axis_index("y"), lax.axis_index("c")]).astype(jnp.int32)

    def bf16_rows(w, transpose):
        return (jnp.transpose(w[0]) if transpose else w[0]).astype(BF16)

    w_in_gather = _all_gather([bf16_rows(w_in, True)])
    ffn_in = _all_gather([bf16_rows(w_ffn_gate, True), bf16_rows(w_ffn_up, True)])
    mixers = _all_gather([bf16_rows(w_out, False), bf16_rows(w_proj_attn, True), bf16_rows(w_proj_sgu, True)])
    ffn_out = _all_gather([bf16_rows(w_ffn_down, False)])

    inv_freq = ROPE_THETA ** (-jnp.arange(0, 2 * ROPE_HALF, 2, dtype=F32) / (2 * ROPE_HALF))
    inv_freq_row = jnp.tile(jnp.concatenate([inv_freq, inv_freq, jnp.zeros((48,), F32)]), 2).reshape(1, HEAD_PAIR)
    tables = _rope_tables(positions.reshape(T, 1), inv_freq_row, tm)
    b_t = jnp.transpose(b_spatial[0])

    (h,), (win_t,) = _rmsnorm_fwd(xt, norm1_g, "norm1_fwd", tm, carried=w_in_gather)
    qkv, (wg_t, wu_t) = _qkv_proj(h, win_t, tables, tm, carried=ffn_in)
    (uvg,), (wout, wpa_t, wps_t) = _uv_gate_proj(h, win_t, tm, carried=mixers)
    fwd0, (wd,) = _attn_fwd(qkv[0], 0, carried=ffn_out)
    fwd = [fwd0, _attn_fwd(qkv[1], 1)[0], _attn_fwd(qkv[2], 2)[0]]
    attn, attn_b, lse = _attn_combine([f[0] for f in fwd], [f[1] for f in fwd], tm)
    sgu = _sgu_fwd(uvg, sgu_ln_g, sgu_ln_b, w_spatial[0], b_t, tm)
    pa, ps, merged = _merge_fwd(attn_b, sgu, wpa_t, wps_t, uvg, tm)
    x1, h2 = _out_residual(merged, wout, xt, norm2_g, tm)
    gate, up, ff = _ffn_fwd(h2, wg_t, wu_t, tm)
    dx2, dx2b, loss_cols, d_final_g = _down_loss(ff, wd, x1, final_g.reshape(1, D_MODEL), target, tm)

    dgate, dup, dx1, dx1b, d_norm2 = _ffn_bwd(dx2, dx2b, wd, wg_t, wu_t, gate, up, x1, norm2_g, tm // 2)
    tk = min(2048, T)
    d_wd = _mm_tn(ff, dx2b, tmm=FF_TILE, tk=tk, name="grad_w_ffn_down")
    d_wg_t = _mm_tn(dgate, h2, tmm=FF_TILE, tk=tk, name="grad_w_ffn_gate")
    d_wu_t = _mm_tn(dup, h2, tmm=FF_TILE, tk=tk, name="grad_w_ffn_up")

    def by_owner(grads):
        return [g.reshape(4, 2, g.shape[0] // N_DEV, g.shape[1]) for g in grads]

    def pair_sums(grads4, from_sibling, names):
        both = [_pair_sum(g4, rv, chip, "grad_pair_sum_" + nm) for g4, rv, nm in zip(grads4, from_sibling, names)]
        return [b[0] for b in both], [b[1] for b in both]

    ffn_names = ["w_ffn_gate", "w_ffn_up", "w_ffn_down"]
    ffn4 = by_owner([d_wg_t, d_wu_t, d_wd])
    (dpa, dps, dgates), ffn_sib = _merge_bwd(dx1b, wout, pa, ps, uvg, tm, carried=_pair_exchange(ffn4))
    ffn_sums, ffn_own = pair_sums(ffn4, ffn_sib, ffn_names)

    d_wout = _mm_tn(merged, dx1b, tmm=D_MODEL, tk=tk, name="grad_w_out")
    dattn, dsgu = _proj_bwd(dpa, dps, wpa_t, wps_t, tm)
    d_wpa_t = _mm_tn(dpa, attn_b, tmm=D_MODEL, tk=tk, name="grad_w_proj_attn")
    d_wps_t = _mm_tn(dps, sgu, tmm=D_MODEL, tk=tk, name="grad_w_proj_sgu")
    mid_names = ["w_proj_attn", "w_proj_sgu", "w_out"]
    mid4 = by_owner([d_wpa_t, d_wps_t, d_wout])
    (duv, d_ws, d_bs_t, d_ln_g, d_ln_b), mid_sib = _sgu_bwd(uvg, dsgu, sgu_ln_g, sgu_ln_b, w_spatial[0], b_t, tm,
                                                           carried=_pair_exchange(mid4))
    mid_sums, mid_own = pair_sums(mid4, mid_sib, mid_names)

    prep = _attn_bwd_prepare(dattn, attn, lse, tm)
    dqkv0, ffn_far = _attn_bwd(qkv[0], prep[0], prep[1], 0, carried=_chip_exchange(ffn_sums))
    dqkv1, mid_far = _attn_bwd(qkv[1], prep[2], prep[3], 1, carried=_chip_exchange(mid_sums))
    def flat(parts):
        return jnp.concatenate([p.reshape(-1) for p in parts]).reshape(-1, HEAD_PAIR)

    early_part = flat([d_ln_g, d_ln_b, d_ws, jnp.transpose(d_bs_t), d_norm2, d_final_g, loss_cols])
    dqkv2, (early_parts,) = _attn_bwd(qkv[2], prep[4], prep[5], 2, carried=_all_gather([early_part]))
    dqkv = _dqkv_token_order([dqkv0, dqkv1, dqkv2], tables, tm)
    d_win_t = _mm_tn(dqkv, h, tmm=1536, tk=tk, name="grad_w_in_qkv", rows_total=IN_COLS)
    d_win_t = _mm_tn(duv, h, tmm=512, tk=tk, name="grad_w_in_uv", into=d_win_t, row_block0=9, rows_total=IN_COLS)
    d_win_t = _mm_tn(dgates, h, tmm=512, tk=tk, name="grad_w_in_gates", into=d_win_t, row_block0=11, rows_total=IN_COLS)
    in4 = by_owner([d_win_t])
    in_sib = _pair_exchange(in4).run_alone("w_in_grad_pair_exchange")
    in_sums, in_own = pair_sums(in4, in_sib, ["w_in"])
    half_tiles = T // tm // 2
    first_half, in_far = _in_bwd(dqkv, duv, dgates, win_t, xt, norm1_g, dx1, tm, 0, half_tiles, "in_bwd_first",
                                 carried=_chip_exchange(in_sums))
    dx, d_norm1 = _in_bwd(dqkv, duv, dgates, win_t, xt, norm1_g, dx1, tm, half_tiles, half_tiles, "in_bwd_second",
                          so_far=first_half)

    names = ["w_in"] + mid_names + ffn_names
    reduced = [_chip_sum(o, f, "grad_total_" + nm)
               for o, f, nm in zip(in_own + mid_own + ffn_own, in_far + mid_far + ffn_far, names)]
    transposed = (True, True, True, False, True, True, False)
    g_big = [jnp.transpose(r) if t else r for r, t in zip(reduced, transposed)]

    small_w = [norm1_g, sgu_ln_g, sgu_ln_b, w_spatial, b_spatial, norm2_g, final_g]
    small_m = [m_norm1_g, m_sgu_ln_g, m_sgu_ln_b, m_w_spatial, m_b_spatial, m_norm2_g, m_final_g]
    small_v = [v_norm1_g, v_sgu_ln_g, v_sgu_ln_b, v_w_spatial, v_b_spatial, v_norm2_g, v_final_g]
    zeros = jnp.zeros((D_MODEL,), F32)
    (late_parts,) = _all_gather([flat([d_norm1])]).run_alone("norm1_grad_all_gather", vmem=True)
    g_s, d_s, nm_s, nv_s, loss = _small_update(late_parts, early_parts, flat(small_w + [zeros]), flat(small_m + [zeros]),
                                               flat(small_v + [zeros]), D_MODEL // HEAD_PAIR)

    def unflat(vec):
        vec = vec.reshape(-1)
        out, at = [], 0
        for wgt in small_w:
            out.append(vec[at:at + wgt.size].reshape(wgt.shape))
            at += wgt.size
        return out

    small = [unflat(a) for a in (g_s, d_s, nm_s, nv_s)]

    big_w = [w_in, w_proj_attn, w_proj_sgu, w_out, w_ffn_gate, w_ffn_up, w_ffn_down]
    big_m = [m_w_in, m_w_proj_attn, m_w_proj_sgu, m_w_out, m_w_ffn_gate, m_w_ffn_up, m_w_ffn_down]
    big_v = [v_w_in, v_w_proj_attn, v_w_proj_sgu, v_w_out, v_w_ffn_gate, v_w_ffn_up, v_w_ffn_down]
    big_out = []
    for wgt, g, mm, vv, nm in zip(big_w, g_big, big_m, big_v, names):
        d, nm_, nv_ = _adamw(wgt[0], g, mm[0], vv[0], "adamw_" + nm)
        big_out.append([a[None] for a in (g, d, nm_, nv_)])

    small_at = {0: 0, 2: 1, 3: 2, 4: 3, 5: 4, 9: 5, 13: 6}
    big_at = {1: 0, 6: 1, 7: 2, 8: 3, 10: 4, 11: 5, 12: 6}
    outs = [loss[0, 0], dx[None]]
    for kind in range(4):
        for idx in range(14):
            outs.append(small[kind][small_at[idx]] if idx in small_at else big_out[big_at[idx]][kind])
    return tuple(outs)
```

```python
import functools
import math

import jax
import jax.numpy as jnp
from jax import lax
from jax.experimental import pallas as pl
from jax.experimental.pallas import tpu as pltpu

F32 = jnp.float32
BF16 = jnp.bfloat16

D_MODEL = 1024
HEAD_PAIR = 128
ATTN_W = 512
DILATIONS = (1, 4, 16)
BLK = 128
ROPE_HALF = 8
ROPE_THETA = 500000.0
SGU_W = 512
QKV_COLS = 4608
IN_COLS = 7680
D_FF = 2816
FF_TILE = 1408
EPS = 1e-6
N_DEV = 8
MASKED = -1e30

ADAM_LR = 0.001
ADAM_B1 = 0.9
ADAM_B2 = 0.999
ADAM_EPS = 1e-08
ADAM_WD = 0.01
ADAM_STEP = 10

MIB = 1024 * 1024
MESH = pl.DeviceIdType.MESH
ANY = pl.BlockSpec(memory_space=pl.ANY)


def _array(shape, dtype):
    return pltpu.HBM(tuple(shape), dtype)


PIN_BYTES = 4 * MIB


def _pin(x):
    if x.size * x.dtype.itemsize < PIN_BYTES:
        return x
    return pltpu.with_memory_space_constraint(x, pltpu.HBM)


def _pallas(body, **kwargs):
    call = pl.pallas_call(body, **kwargs)
    return lambda *args: call(*[_pin(a) for a in args])


def _params(sem, vmem_mib):
    return pltpu.CompilerParams(dimension_semantics=sem, vmem_limit_bytes=vmem_mib * MIB)


def _nt(a, b):
    return lax.dot_general(a, b, (((1,), (1,)), ((), ())), preferred_element_type=F32)


def _nn(a, b):
    return lax.dot_general(a, b, (((1,), (0,)), ((), ())), preferred_element_type=F32)


def _tn(a, b):
    return lax.dot_general(a, b, (((0,), (0,)), ((), ())), preferred_element_type=F32)


class _Exchange:
    def __init__(self, arrays, out_shapes, sem_shapes, phases):
        self.arrays, self.out_shapes, self.sem_shapes, self.phases = list(arrays), out_shapes, sem_shapes, phases

    def run_alone(self, name, vmem=False):
        n_in, n_out = len(self.arrays), len(self.out_shapes)

        def body(*refs):
            start, middle, finish = self.phases(refs[:n_in], refs[n_in:n_in + n_out], refs[n_in + n_out:])
            start()
            middle()
            finish()

        spec = pl.BlockSpec(memory_space=pltpu.VMEM) if vmem else ANY
        shapes = [jax.ShapeDtypeStruct(s.shape, s.dtype) for s in self.out_shapes] if vmem else self.out_shapes
        return (pl.pallas_call if vmem else _pallas)(
            body, name=name, out_shape=shapes, in_specs=[spec] * n_in, out_specs=[spec] * n_out,
            scratch_shapes=self.sem_shapes, compiler_params=pltpu.CompilerParams(vmem_limit_bytes=32 * MIB),
        )(*self.arrays)


def _call(body, *, name, grid, in_specs, out_specs, out_shape, args, vmem_mib, scratch_shapes=(), carried=None):
    n_in, n_out, n_scr = len(in_specs), len(out_specs), len(scratch_shapes)
    sem = ("arbitrary",) * len(grid)
    if carried is None:
        outs = _pallas(
            body, name=name, grid=grid, in_specs=in_specs, out_specs=out_specs, out_shape=out_shape,
            scratch_shapes=list(scratch_shapes), compiler_params=_params(sem, vmem_mib))(*args)
        return list(outs), []
    c_in, c_out = len(carried.arrays), len(carried.out_shapes)
    total = math.prod(grid)

    def full(*refs):
        own_in, car_in = refs[:n_in], refs[n_in:n_in + c_in]
        at = n_in + c_in
        own_out, car_out = refs[at:at + n_out], refs[at + n_out:at + n_out + c_out]
        at += n_out + c_out
        own_scr, sems = refs[at:at + n_scr], refs[at + n_scr:]
        step = pl.program_id(0)
        for axis in range(1, len(grid)):
            step = step * grid[axis] + pl.program_id(axis)
        start, middle, finish = carried.phases(car_in, car_out, sems)
        pl.when(step == 0)(start)
        pl.when(step == (3 * total) // 4)(middle)
        body(*own_in, *own_out, *own_scr)
        pl.when(step == total - 1)(finish)

    outs = _pallas(
        full, name=name, grid=grid, in_specs=list(in_specs) + [ANY] * c_in,
        out_specs=list(out_specs) + [ANY] * c_out, out_shape=list(out_shape) + list(carried.out_shapes),
        scratch_shapes=list(scratch_shapes) + list(carried.sem_shapes),
        compiler_params=_params(sem, vmem_mib))(*args, *carried.arrays)
    return list(outs[:n_out]), list(outs[n_out:])


def _all_gather(shards):
    n = len(shards)

    def phases(ins, outs, sems):
        send_sems, recv_sems, local_sems = sems
        x, y, c = lax.axis_index("x"), lax.axis_index("y"), lax.axis_index("c")
        me, sibling = (x, y, c), (x, y, 1 - c)
        chips = [(1 - x, y), (x, 1 - y), (1 - x, 1 - y)]

        def rows(m, px, py, pc):
            r = ins[m].shape[0]
            return outs[m].at[pl.ds((4 * px + 2 * py + pc) * r, r), :]

        def copy(m, k, block, to, src=None):
            return pltpu.make_async_remote_copy(
                src_ref=rows(m, *block) if src is None else src, dst_ref=rows(m, *block),
                send_sem=send_sems.at[m, k], recv_sem=recv_sems.at[m, k],
                device_id=to, device_id_type=MESH)

        def mine(m):
            return pltpu.make_async_copy(ins[m], rows(m, *me), local_sems.at[m])

        def first(m):
            return [copy(m, 0, me, sibling, src=ins[m])] + [
                copy(m, 1 + j, me, (*chip, c), src=ins[m]) for j, chip in enumerate(chips)]

        def passed(m):
            return [copy(m, 4 + j, (*chip, c), sibling) for j, chip in enumerate(chips)]

        def start():
            for m in range(n):
                mine(m).start()
            for m in range(n):
                for cp in first(m):
                    cp.start()

        def middle():
            for m in range(n):
                for j, chip in enumerate(chips):
                    copy(m, 1 + j, (*chip, c), me).wait_recv()
                    passed(m)[j].start()

        def finish():
            for m in range(n):
                copy(m, 0, sibling, me).wait_recv()
                for j, chip in enumerate(chips):
                    copy(m, 4 + j, (*chip, 1 - c), me).wait_recv()
            for m in range(n):
                for cp in first(m) + passed(m):
                    cp.wait_send()
                mine(m).wait()

        return start, middle, finish

    return _Exchange(
        shards, [_array((N_DEV * s.shape[0], s.shape[1]), s.dtype) for s in shards],
        [pltpu.SemaphoreType.DMA((n, 7)), pltpu.SemaphoreType.DMA((n, 7)), pltpu.SemaphoreType.DMA((n,))], phases)


def _pair_exchange(grads):
    n = len(grads)

    def phases(ins, outs, sems):
        send_sems, recv_sems = sems
        x, y, c = lax.axis_index("x"), lax.axis_index("y"), lax.axis_index("c")

        def copy(m):
            return pltpu.make_async_remote_copy(
                src_ref=ins[m].at[:, 1 - c], dst_ref=outs[m], send_sem=send_sems.at[m], recv_sem=recv_sems.at[m],
                device_id=(x, y, 1 - c), device_id_type=MESH)

        def start():
            for m in range(n):
                copy(m).start()

        def finish():
            for m in range(n):
                copy(m).wait()

        return start, lambda: None, finish

    return _Exchange(grads, [_array((4,) + g.shape[2:], g.dtype) for g in grads],
                     [pltpu.SemaphoreType.DMA((n,)), pltpu.SemaphoreType.DMA((n,))], phases)


def _chip_exchange(pair_sums):
    n = len(pair_sums)

    def phases(ins, outs, sems):
        send_sems, recv_sems = sems
        x, y, c = lax.axis_index("x"), lax.axis_index("y"), lax.axis_index("c")
        chips = [(1 - x, y), (x, 1 - y), (1 - x, 1 - y)]

        def copies():
            return [pltpu.make_async_remote_copy(
                src_ref=ins[m].at[2 * px + py], dst_ref=outs[m].at[j],
                send_sem=send_sems.at[m, j], recv_sem=recv_sems.at[m, j],
                device_id=(px, py, c), device_id_type=MESH)
                for m in range(n) for j, (px, py) in enumerate(chips)]

        def start():
            for cp in copies():
                cp.start()

        def finish():
            for cp in copies():
                cp.wait_recv()
            for cp in copies():
                cp.wait_send()

        return start, lambda: None, finish

    return _Exchange(pair_sums, [_array((3,) + p.shape[1:], p.dtype) for p in pair_sums],
                     [pltpu.SemaphoreType.DMA((n, 3)), pltpu.SemaphoreType.DMA((n, 3))], phases)


N_SLABS = ATTN_W // HEAD_PAIR


def _slab_scratch(tm, n=N_SLABS):
    return pltpu.VMEM((n, tm, HEAD_PAIR), F32)


def _rows_by_residue(dst_ref, slab_ref, r, tr, dtype, n=N_SLABS):
    for rho in range(r):
        for s in range(n):
            dst_ref[rho, :, s * HEAD_PAIR:(s + 1) * HEAD_PAIR] = (
                slab_ref[s, pl.ds(rho, tr, stride=r), :].astype(dtype))


def _rows_by_token(slab_ref, src_ref, r, tr, n=N_SLABS):
    for rho in range(r):
        for s in range(n):
            slab_ref[s, pl.ds(rho, tr, stride=r), :] = (
                src_ref[rho, :, s * HEAD_PAIR:(s + 1) * HEAD_PAIR].astype(F32))


def _rope_tables(pos_col, inv_freq_row, tm):
    T = pos_col.shape[0]

    def body(pos_ref, invf_ref, cos_ref, s1_ref, s2_ref):
        ang = pos_ref[...].astype(F32) * invf_ref[...]
        lane = lax.broadcasted_iota(jnp.int32, (1, HEAD_PAIR), 1) % 64
        cs, sn = jnp.cos(ang), jnp.sin(ang)
        cos_ref[...] = jnp.where(lane < 2 * ROPE_HALF, cs, 1.0)
        s1_ref[...] = jnp.where(lane < ROPE_HALF, -sn, 0.0)
        s2_ref[...] = jnp.where((lane >= ROPE_HALF) & (lane < 2 * ROPE_HALF), sn, 0.0)

    tab = _array((T, HEAD_PAIR), F32)
    row = pl.BlockSpec((tm, HEAD_PAIR), lambda i: (i, 0))
    return _pallas(
        body, name="rope_tables", grid=(T // tm,), out_shape=[tab] * 3,
        in_specs=[pl.BlockSpec((tm, 1), lambda i: (i, 0)), pl.BlockSpec((1, HEAD_PAIR), lambda i: (0, 0))],
        out_specs=[row] * 3, compiler_params=_params(("parallel",), 16),
    )(pos_col, inv_freq_row)


def _rope(y, cos, s1, s2):
    w = y.shape[1]
    rep = w // HEAD_PAIR
    return (y * jnp.tile(cos, (1, rep)) + pltpu.roll(y, w - ROPE_HALF, 1) * jnp.tile(s1, (1, rep))
            + pltpu.roll(y, ROPE_HALF, 1) * jnp.tile(s2, (1, rep)))


def _rope_transposed(dy, cos, s1, s2):
    w = dy.shape[1]
    rep = w // HEAD_PAIR
    return (dy * jnp.tile(cos, (1, rep)) + pltpu.roll(dy * jnp.tile(s1, (1, rep)), ROPE_HALF, 1)
            + pltpu.roll(dy * jnp.tile(s2, (1, rep)), w - ROPE_HALF, 1))


def _rmsnorm_fwd(x, g, name, tm, carried=None):
    T = x.shape[0]

    def body(x_ref, g_ref, h_ref):
        xf = x_ref[...]
        r = lax.rsqrt(jnp.mean(xf * xf, axis=-1, keepdims=True) + EPS)
        h_ref[...] = (xf * r * g_ref[...]).astype(BF16)

    return _call(
        body, name=name, grid=(T // tm,), out_shape=[_array((T, D_MODEL), BF16)],
        in_specs=[pl.BlockSpec((tm, D_MODEL), lambda i: (i, 0)), pl.BlockSpec((1, D_MODEL), lambda i: (0, 0))],
        out_specs=[pl.BlockSpec((tm, D_MODEL), lambda i: (i, 0))], vmem_mib=24, args=(x, g), carried=carried)


def _resident(shape, block=None):
    at = (0,) * len(shape) if block is None else block
    return pl.BlockSpec(shape, lambda *_: at, pipeline_mode=pl.Buffered(1))


def _qkv_proj(h, win_t, tables, tm, carried=None):
    T = h.shape[0]

    def body(h_ref, w_ref, cos_ref, s1_ref, s2_ref, o0, o1, o2, *slabs):
        hv = h_ref[...]
        cos, s1, s2 = cos_ref[...], s1_ref[...], s2_ref[...]
        for kind in range(3):
            for g, (o_ref, r) in enumerate(zip((o0, o1, o2), DILATIONS)):
                blk = 3 * kind + g
                y = _nt(hv, w_ref[blk * ATTN_W:(blk + 1) * ATTN_W, :])
                if kind < 2:
                    y = _rope(y, cos, s1, s2)
                if kind == 0:
                    y = y * 0.125
                cols = slice(kind * ATTN_W, (kind + 1) * ATTN_W)
                if r == 1:
                    o_ref[0, :, cols] = y.astype(BF16)
                    continue
                slab = slabs[blk % len(slabs)]
                for s in range(N_SLABS):
                    slab[s] = y[:, s * HEAD_PAIR:(s + 1) * HEAD_PAIR]
                for rho in range(r):
                    for s in range(N_SLABS):
                        at = kind * ATTN_W + s * HEAD_PAIR
                        o_ref[rho, :, at:at + HEAD_PAIR] = slab[s, pl.ds(rho, tm // r, stride=r), :].astype(BF16)

    row = pl.BlockSpec((tm, HEAD_PAIR), lambda i: (i, 0))
    return _call(
        body, name="qkv_proj", grid=(T // tm,),
        out_shape=[_array((r, T // r, 3 * ATTN_W), BF16) for r in DILATIONS],
        in_specs=[pl.BlockSpec((tm, D_MODEL), lambda i: (i, 0)), _resident((QKV_COLS, D_MODEL)), row, row, row],
        out_specs=[pl.BlockSpec((r, tm // r, 3 * ATTN_W), lambda i: (0, i, 0)) for r in DILATIONS],
        scratch_shapes=[_slab_scratch(tm)] * 3, vmem_mib=48, args=(h, win_t, *tables), carried=carried)


def _uv_gate_proj(h, win_t, tm, carried=None):
    T = h.shape[0]
    half = (IN_COLS - QKV_COLS) // 2

    def body(h_ref, wa_ref, wb_ref, o_ref):
        hv = h_ref[...]
        o_ref[:, :half] = _nt(hv, wa_ref[...]).astype(BF16)
        o_ref[:, half:] = _nt(hv, wb_ref[...]).astype(BF16)

    blk0 = QKV_COLS // half
    return _call(
        body, name="uv_gate_proj", grid=(T // tm,), out_shape=[_array((T, 2 * half), BF16)],
        in_specs=[pl.BlockSpec((tm, D_MODEL), lambda i: (i, 0)), _resident((half, D_MODEL), (blk0, 0)),
                  _resident((half, D_MODEL), (blk0 + 1, 0))],
        out_specs=[pl.BlockSpec((tm, 2 * half), lambda i: (i, 0))], vmem_mib=40, args=(h, win_t, win_t),
        carried=carried)


def _band_mask(n):
    row = lax.broadcasted_iota(jnp.int32, (2 * BLK, 2 * BLK), 0) & (BLK - 1)
    col = lax.broadcasted_iota(jnp.int32, (2 * BLK, 2 * BLK), 1)
    has_prev = (jnp.zeros_like(row) + n) > 0
    return ((col < BLK) & (col >= row) & has_prev) | ((col >= BLK) & (col - BLK <= row))


def _head_lanes():
    lane = lax.broadcasted_iota(jnp.int32, (1, HEAD_PAIR), 1)
    return lane < 64, lane >= 64


def _stack_heads(x, head0, head1):
    zero = jnp.zeros_like(x)
    return jnp.concatenate([jnp.where(head0, x, zero), jnp.where(head1, x, zero)], axis=0)


def _unstack_heads(y, head0):
    return jnp.where(head0, y[:BLK], y[BLK:])


def _per_head(stats, col):
    return jnp.concatenate([stats[:, col:col + 1], stats[:, col + 1:col + 2]], axis=0)


def _attn_sub_blocks(length):
    for n in (4, 2):
        if length % (n * BLK) == 0:
            return n
    return 1


def _attn_fwd(qkv, group, carried=None):
    r, L, _ = qkv.shape
    nsub = _attn_sub_blocks(L)
    rows = nsub * BLK

    def body(q_ref, kp_ref, kc_ref, vp_ref, vc_ref, o_ref, lse_ref):
        head0, head1 = _head_lanes()
        lane = lax.broadcasted_iota(jnp.int32, (1, HEAD_PAIR), 1)
        for j in range(nsub):
            cur = slice(j * BLK, (j + 1) * BLK)
            before = slice((j - 1) * BLK, j * BLK)
            valid = _band_mask(pl.program_id(1) + j)
            stats = jnp.zeros((BLK, HEAD_PAIR), F32)
            for p in range(N_SLABS):
                sl = slice(p * HEAD_PAIR, (p + 1) * HEAD_PAIR)
                k_prev, v_prev = (kp_ref[:, sl], vp_ref[:, sl]) if j == 0 else (kc_ref[before, sl], vc_ref[before, sl])
                k2 = jnp.concatenate([k_prev, kc_ref[cur, sl]], axis=0)
                v2 = jnp.concatenate([v_prev, vc_ref[cur, sl]], axis=0)
                s = jnp.where(valid, _nt(_stack_heads(q_ref[cur, sl], head0, head1), k2), MASKED)
                m = jnp.max(s, axis=1, keepdims=True)
                e = jnp.exp(s - m)
                den = jnp.sum(e, axis=1, keepdims=True)
                o_ref[cur, sl] = _unstack_heads(_nn(e.astype(BF16), v2) / den, head0)
                lse = m + jnp.log(den)
                stats = jnp.where(lane == 2 * p, lse[:BLK], jnp.where(lane == 2 * p + 1, lse[BLK:], stats))
            lse_ref[cur, :] = stats

    def cur_blk(kind, width=ATTN_W):
        return pl.BlockSpec((None, rows, width), lambda rho, n: (rho, n, kind))

    def prev_blk(kind):
        return pl.BlockSpec((None, BLK, ATTN_W), lambda rho, n: (rho, jnp.maximum(n * nsub - 1, 0), kind))

    return _call(
        body, name=f"attn_fwd_g{group}", grid=(r, L // rows),
        out_shape=[_array((r, L, ATTN_W), F32), _array((r, L, HEAD_PAIR), F32)],
        in_specs=[cur_blk(0), prev_blk(1), cur_blk(1), prev_blk(2), cur_blk(2)],
        out_specs=[cur_blk(0), cur_blk(0, HEAD_PAIR)],
        vmem_mib=24, args=(qkv, qkv, qkv, qkv, qkv), carried=carried)


def _attn_combine(outs, lses, tm):
    T = outs[0].shape[1]

    def body(o0, l0, o1, l1, o2, l2, attn_ref, attn_b_ref, lse_ref, so1, sl1, so2, sl2):
        for o_in, l_in, so, sl, r in ((o1, l1, so1, sl1, DILATIONS[1]), (o2, l2, so2, sl2, DILATIONS[2])):
            _rows_by_token(so, o_in, r, tm // r)
            _rows_by_token(sl, l_in, r, tm // r, n=1)
        head0, _ = _head_lanes()
        a0, a1, a2 = l0[0], sl1[0], sl2[0]
        mx = jnp.maximum(jnp.maximum(a0, a1), a2)
        e0, e1, e2 = jnp.exp(a0 - mx), jnp.exp(a1 - mx), jnp.exp(a2 - mx)
        tot = e0 + e1 + e2
        lse_ref[...] = mx + jnp.log(tot)
        w0, w1, w2 = e0 / tot, e1 / tot, e2 / tot
        for s in range(N_SLABS):
            lanes = slice(s * HEAD_PAIR, (s + 1) * HEAD_PAIR)

            def lanes_of(w):
                return jnp.where(head0, w[:, 2 * s:2 * s + 1], w[:, 2 * s + 1:2 * s + 2])

            mixed = lanes_of(w0) * o0[0, :, lanes] + lanes_of(w1) * so1[s] + lanes_of(w2) * so2[s]
            attn_ref[:, lanes] = mixed
            attn_b_ref[:, lanes] = mixed.astype(BF16)

    ins, specs = [], []
    for g, r in enumerate(DILATIONS):
        ins += [outs[g], lses[g]]
        specs += [pl.BlockSpec((r, tm // r, ATTN_W), lambda i: (0, i, 0)),
                  pl.BlockSpec((r, tm // r, HEAD_PAIR), lambda i: (0, i, 0))]
    return _pallas(
        body, name="attn_combine", grid=(T // tm,),
        out_shape=[_array((T, ATTN_W), F32), _array((T, ATTN_W), BF16), _array((T, HEAD_PAIR), F32)], in_specs=specs,
        out_specs=[pl.BlockSpec((tm, ATTN_W), lambda i: (i, 0)), pl.BlockSpec((tm, ATTN_W), lambda i: (i, 0)),
                   pl.BlockSpec((tm, HEAD_PAIR), lambda i: (i, 0))],
        scratch_shapes=[_slab_scratch(tm), _slab_scratch(tm, 1), _slab_scratch(tm), _slab_scratch(tm, 1)],
        compiler_params=_params(("parallel",), 32),
    )(*ins)


def _gelu(x):
    return 0.5 * x * (1.0 + lax.erf(x * (1.0 / math.sqrt(2.0))))


def _gelu_grad(x):
    return 0.5 * (1.0 + lax.erf(x * (1.0 / math.sqrt(2.0)))) + x * jnp.exp(-0.5 * x * x) * (1.0 / math.sqrt(2.0 * math.pi))


def _causal():
    row = lax.broadcasted_iota(jnp.int32, (BLK, BLK), 0)
    col = lax.broadcasted_iota(jnp.int32, (BLK, BLK), 1)
    return col <= row


def _bias_lanes(bt):
    grp = lax.broadcasted_iota(jnp.int32, (1, SGU_W), 1) // 64
    out = jnp.zeros((BLK, SGU_W), F32)
    for g in range(8):
        out = jnp.where(grp == g, bt[:, g:g + 1], out)
    return out


def _sgu_normalise(uv, ln_g, ln_b):
    z = _gelu(uv)
    u, v = z[:, :SGU_W], z[:, SGU_W:]
    mu = jnp.mean(v, axis=-1, keepdims=True)
    xc = v - mu
    rstd = lax.rsqrt(jnp.mean(xc * xc, axis=-1, keepdims=True) + EPS)
    xh = xc * rstd
    return u, xh, rstd, xh * ln_g + ln_b


def _sgu_mix(wc_ref, vb, head0):
    chunks = []
    for ch in range(vb.shape[0] // BLK):
        pairs = []
        for p in range(SGU_W // HEAD_PAIR):
            v_pair = vb[ch * BLK:(ch + 1) * BLK, p * HEAD_PAIR:(p + 1) * HEAD_PAIR]
            pairs.append(jnp.where(head0, _nn(wc_ref[2 * p], v_pair), _nn(wc_ref[2 * p + 1], v_pair)))
        chunks.append(jnp.concatenate(pairs, axis=1))
    return jnp.concatenate(chunks, axis=0)


def _sgu_fwd(uvg, ln_g, ln_b, w_s, b_t, tm):
    T = uvg.shape[0]

    def body(uv_ref, g_ref, b_ref, w_ref, bt_ref, o_ref, wc_ref, bias_ref):
        @pl.when(pl.program_id(0) == 0)
        def _():
            causal = _causal()
            for g in range(8):
                wc_ref[g] = jnp.where(causal, w_ref[g], 0.0).astype(BF16)
            bias_ref[...] = _bias_lanes(bt_ref[...])

        u, _, _, vn = _sgu_normalise(uv_ref[...].astype(F32), g_ref[...], b_ref[...])
        mixed = _sgu_mix(wc_ref, vn.astype(BF16), _head_lanes()[0])
        o_ref[...] = (u * (mixed + jnp.tile(bias_ref[...], (tm // BLK, 1)))).astype(BF16)

    vec = pl.BlockSpec((1, SGU_W), lambda i: (0, 0))
    return _pallas(
        body, name="sgu_fwd", grid=(T // tm,), out_shape=_array((T, SGU_W), BF16),
        in_specs=[pl.BlockSpec((tm, 2 * SGU_W), lambda i: (i, 0)), vec, vec,
                  pl.BlockSpec((8, BLK, BLK), lambda i: (0, 0, 0)), pl.BlockSpec((BLK, 8), lambda i: (0, 0))],
        out_specs=pl.BlockSpec((tm, SGU_W), lambda i: (i, 0)),
        scratch_shapes=[pltpu.VMEM((8, BLK, BLK), BF16), pltpu.VMEM((BLK, SGU_W), F32)],
        compiler_params=_params(("arbitrary",), 32),
    )(uvg, ln_g, ln_b, w_s, b_t)


def _merge_fwd(attn, sgu, wpa_t, wps_t, uvg, w_out, x, g2, tm):
    T = attn.shape[0]

    def body(attn_ref, sgu_ref, wpa_ref, wps_ref, ga_ref, gb_ref, wo_ref, x_ref, g_ref,
             pa_ref, ps_ref, m_ref, h_ref, x1_ref):
        pa = _nt(attn_ref[...], wpa_ref[...])
        ps = _nt(sgu_ref[...], wps_ref[...])
        pa_ref[...] = pa.astype(BF16)
        ps_ref[...] = ps.astype(BF16)
        ga, gb = ga_ref[...].astype(F32), gb_ref[...].astype(F32)
        merged = (jax.nn.sigmoid(ga) * pa + jax.nn.sigmoid(gb) * ps).astype(BF16)
        m_ref[...] = merged
        x1 = x_ref[...] + _nn(merged, wo_ref[...])
        x1_ref[...] = x1
        r = lax.rsqrt(jnp.mean(x1 * x1, axis=-1, keepdims=True) + EPS)
        h_ref[...] = (x1 * r * g_ref[...]).astype(BF16)

    half = pl.BlockSpec((tm, ATTN_W), lambda i: (i, 0))
    wide = pl.BlockSpec((tm, D_MODEL), lambda i: (i, 0))
    w = _resident((D_MODEL, ATTN_W))
    res = _array((T, D_MODEL), BF16)
    return _pallas(
        body, name="merge_fwd", grid=(T // tm,), out_shape=[res, res, res, res, _array((T, D_MODEL), F32)],
        in_specs=[half, half, w, w, pl.BlockSpec((tm, D_MODEL), lambda i: (i, 1)),
                  pl.BlockSpec((tm, D_MODEL), lambda i: (i, 2)), _resident((D_MODEL, D_MODEL)), wide,
                  pl.BlockSpec((1, D_MODEL), lambda i: (0, 0))],
        out_specs=[wide] * 5, compiler_params=_params(("parallel",), 48),
    )(attn, sgu, wpa_t, wps_t, uvg, uvg, w_out, x, g2)


def _ffn_fwd(h2, wg_t, wu_t, tm):
    T = h2.shape[0]

    def body(h_ref, wg_ref, wu_ref, gate_ref, up_ref, ff_ref):
        h = h_ref[...]
        for j in range(D_FF // FF_TILE):
            cols = slice(j * FF_TILE, (j + 1) * FF_TILE)
            gate, up = _nt(h, wg_ref[cols, :]), _nt(h, wu_ref[cols, :])
            gate_ref[:, cols] = gate.astype(BF16)
            up_ref[:, cols] = up.astype(BF16)
            ff_ref[:, cols] = (gate * jax.nn.sigmoid(gate) * up).astype(BF16)

    w = _resident((D_FF, D_MODEL))
    o = pl.BlockSpec((tm, D_FF), lambda i: (i, 0))
    res = _array((T, D_FF), BF16)
    return _pallas(
        body, name="ffn_fwd", grid=(T // tm,), out_shape=[res, res, res],
        in_specs=[pl.BlockSpec((tm, D_MODEL), lambda i: (i, 0)), w, w], out_specs=[o, o, o],
        compiler_params=_params(("parallel",), 52),
    )(h2, wg_t, wu_t)


def _down_loss(ff, w_down, x1, final_g, target, tm):
    T = x1.shape[0]

    def body(ff_ref, w_ref, x1_ref, g_ref, t_ref, dx_ref, dxb_ref, loss_ref, dg_ref):
        @pl.when(pl.program_id(0) == 0)
        def _():
            loss_ref[...] = jnp.zeros_like(loss_ref)
            dg_ref[...] = jnp.zeros_like(dg_ref)

        x2 = x1_ref[...] + _nn(ff_ref[...], w_ref[...])
        g = g_ref[...]
        r = lax.rsqrt(jnp.mean(x2 * x2, axis=-1, keepdims=True) + EPS)
        xh = x2 * r
        err = xh * g - t_ref[...]
        loss_ref[...] += jnp.sum(err * err, axis=0, keepdims=True) * (0.5 / D_MODEL)
        dy = err * (1.0 / D_MODEL)
        dg_ref[...] += jnp.sum(dy * xh, axis=0, keepdims=True)
        dxh = dy * g
        dx = r * (dxh - xh * jnp.mean(dxh * xh, axis=-1, keepdims=True))
        dx_ref[...] = dx
        dxb_ref[...] = dx.astype(BF16)

    wide = pl.BlockSpec((tm, D_MODEL), lambda i: (i, 0))
    vec = pl.BlockSpec((1, D_MODEL), lambda i: (0, 0))
    vec_shape = _array((1, D_MODEL), F32)
    return _pallas(
        body, name="down_loss", grid=(T // tm,),
        out_shape=[_array((T, D_MODEL), F32), _array((T, D_MODEL), BF16),
                   vec_shape, vec_shape],
        in_specs=[pl.BlockSpec((tm, D_FF), lambda i: (i, 0)), _resident((D_FF, D_MODEL)), wide, vec, wide],
        out_specs=[wide, wide, vec, vec],
        compiler_params=_params(("arbitrary",), 40),
    )(ff, w_down, x1, final_g, target)


def _rmsnorm_bwd(dh, xin, g, d_res, dg_ref):
    r = lax.rsqrt(jnp.mean(xin * xin, axis=-1, keepdims=True) + EPS)
    xh = xin * r
    dg_ref[...] += jnp.sum(dh * xh, axis=0, keepdims=True)
    dxh = dh * g
    return d_res + r * (dxh - xh * jnp.mean(dxh * xh, axis=-1, keepdims=True))


def _ffn_bwd(dx2, dx2b, w_down, wg_t, wu_t, gate, up, x1, g2, tm):
    T = dx2.shape[0]

    def body(dx_ref, dxb_ref, wd_ref, wg_ref, wu_ref, gate_ref, up_ref, x1_ref, g_ref,
             dgate_ref, dup_ref, dx1_ref, dx1b_ref, dg_ref):
        @pl.when(pl.program_id(0) == 0)
        def _():
            dg_ref[...] = jnp.zeros_like(dg_ref)

        dxb = dxb_ref[...]
        dh = jnp.zeros((tm, D_MODEL), F32)
        for j in range(D_FF // FF_TILE):
            cols = slice(j * FF_TILE, (j + 1) * FF_TILE)
            dff = _nt(dxb, wd_ref[cols, :])
            gate, up = gate_ref[:, cols].astype(F32), up_ref[:, cols].astype(F32)
            sg = jax.nn.sigmoid(gate)
            dgate = (dff * up * sg * (1.0 + gate * (1.0 - sg))).astype(BF16)
            dup = (dff * gate * sg).astype(BF16)
            dgate_ref[:, cols] = dgate
            dup_ref[:, cols] = dup
            dh += _nn(dgate, wg_ref[cols, :]) + _nn(dup, wu_ref[cols, :])
        dx1 = _rmsnorm_bwd(dh, x1_ref[...], g_ref[...], dx_ref[...], dg_ref)
        dx1_ref[...] = dx1
        dx1b_ref[...] = dx1.astype(BF16)

    wide = pl.BlockSpec((tm, D_MODEL), lambda i: (i, 0))
    ffw = pl.BlockSpec((tm, D_FF), lambda i: (i, 0))
    vec = pl.BlockSpec((1, D_MODEL), lambda i: (0, 0))
    w = _resident((D_FF, D_MODEL))
    ff_shape = _array((T, D_FF), BF16)
    return _pallas(
        body, name="ffn_bwd", grid=(T // tm,),
        out_shape=[ff_shape, ff_shape, _array((T, D_MODEL), F32),
                   _array((T, D_MODEL), BF16), _array((1, D_MODEL), F32)],
        in_specs=[wide, wide, w, w, w, ffw, ffw, wide, vec],
        out_specs=[ffw, ffw, wide, wide, vec], compiler_params=_params(("arbitrary",), 56),
    )(dx2, dx2b, w_down, wg_t, wu_t, gate, up, x1, g2)


def _mm_tn(a, b, *, tmm, tk, name, into=None, row_block0=0, rows_total=None):
    T, M = a.shape
    N = b.shape[1]
    rows_total = M if rows_total is None else rows_total

    def body(*refs):
        a_ref, b_ref, o_ref = refs[0], refs[1], refs[-1]

        @pl.when(pl.program_id(1) == 0)
        def _():
            o_ref[...] = jnp.zeros_like(o_ref)

        o_ref[...] += _tn(a_ref[...], b_ref[...])

    ins = [a, b] + ([] if into is None else [into])
    specs = [pl.BlockSpec((tk, tmm), lambda i, k: (k, i)), pl.BlockSpec((tk, N), lambda i, k: (k, 0))]
    return _pallas(
        body, name=name, grid=(M // tmm, T // tk),
        out_shape=_array((rows_total, N), F32),
        in_specs=specs + ([] if into is None else [ANY]),
        out_specs=pl.BlockSpec((tmm, N), lambda i, k: (row_block0 + i, 0)),
        input_output_aliases={} if into is None else {2: 0},
        compiler_params=_params(("parallel", "arbitrary"), 48),
    )(*ins)


def _in_bwd(dqkv, duv, dgates, win_t, x, g1, dx1, tm, carried=None):
    T = x.shape[0]
    uv0, gates0 = QKV_COLS, QKV_COLS + 2 * SGU_W

    def body(dq_ref, du_ref, dgt_ref, w_ref, x_ref, g_ref, d_ref, dx_ref, dg_ref):
        @pl.when(pl.program_id(0) == 0)
        def _():
            dg_ref[...] = jnp.zeros_like(dg_ref)

        dh = (_nn(dq_ref[...], w_ref[:uv0, :]) + _nn(du_ref[...], w_ref[uv0:gates0, :])
              + _nn(dgt_ref[...], w_ref[gates0:, :]))
        dx_ref[...] = _rmsnorm_bwd(dh, x_ref[...], g_ref[...], d_ref[...], dg_ref)

    def cols(n):
        return pl.BlockSpec((tm, n), lambda i: (i, 0))

    wide = cols(D_MODEL)
    vec = pl.BlockSpec((1, D_MODEL), lambda i: (0, 0))
    return _call(
        body, name="in_bwd", grid=(T // tm,),
        out_shape=[_array((T, D_MODEL), F32), _array((1, D_MODEL), F32)],
        in_specs=[cols(QKV_COLS), cols(2 * SGU_W), cols(2 * D_MODEL), _resident((IN_COLS, D_MODEL)), wide, vec, wide],
        out_specs=[wide, vec], vmem_mib=56, args=(dqkv, duv, dgates, win_t, x, g1, dx1), carried=carried)


def _merge_bwd(dx1b, w_out, pa, ps, uvg, wpa_t, wps_t, tm, carried=None):
    T = dx1b.shape[0]

    def body(dx_ref, w_ref, pa_ref, ps_ref, ga_ref, gb_ref, wpa_ref, wps_ref,
             dpa_ref, dps_ref, dg_ref, dattn_ref, dsgu_ref):
        dm = _nt(dx_ref[...], w_ref[...])
        ga, gb = jax.nn.sigmoid(ga_ref[...].astype(F32)), jax.nn.sigmoid(gb_ref[...].astype(F32))
        dpa, dps = (dm * ga).astype(BF16), (dm * gb).astype(BF16)
        dpa_ref[...] = dpa
        dps_ref[...] = dps
        dg_ref[:, :D_MODEL] = (dm * pa_ref[...].astype(F32) * ga * (1.0 - ga)).astype(BF16)
        dg_ref[:, D_MODEL:] = (dm * ps_ref[...].astype(F32) * gb * (1.0 - gb)).astype(BF16)
        dattn_ref[...] = _nn(dpa, wpa_ref[...])
        dsgu_ref[...] = _nn(dps, wps_ref[...])

    wide = pl.BlockSpec((tm, D_MODEL), lambda i: (i, 0))
    half = pl.BlockSpec((tm, ATTN_W), lambda i: (i, 0))
    w = _resident((D_MODEL, ATTN_W))
    res = _array((T, D_MODEL), BF16)
    res_half = _array((T, ATTN_W), F32)
    return _call(
        body, name="merge_bwd", grid=(T // tm,),
        out_shape=[res, res, _array((T, 2 * D_MODEL), BF16), res_half, res_half],
        in_specs=[wide, _resident((D_MODEL, D_MODEL)), wide, wide,
                  pl.BlockSpec((tm, D_MODEL), lambda i: (i, 1)), pl.BlockSpec((tm, D_MODEL), lambda i: (i, 2)), w, w],
        out_specs=[wide, wide, pl.BlockSpec((tm, 2 * D_MODEL), lambda i: (i, 0)), half, half],
        vmem_mib=48, args=(dx1b, w_out, pa, ps, uvg, uvg, wpa_t, wps_t), carried=carried)


def _sgu_bwd(uvg, dsgu, ln_g, ln_b, w_s, b_t, tm, carried=None):
    T = uvg.shape[0]
    nsteps = T // tm

    def body(uv_ref, ds_ref, g_ref, b_ref, w_ref, bt_ref, duv_ref, dw_ref, dbt_ref, dg_ref, db_ref,
             wc_ref, wct_ref, bias_ref, dbias_ref):
        step = pl.program_id(0)
        head0, head1 = _head_lanes()

        @pl.when(step == 0)
        def _():
            causal = _causal()
            for g in range(8):
                wc = jnp.where(causal, w_ref[g], 0.0)
                wc_ref[g] = wc.astype(BF16)
                wct_ref[g] = wc.T.astype(BF16)
            bias_ref[...] = _bias_lanes(bt_ref[...])
            dbias_ref[...] = jnp.zeros_like(dbias_ref)
            dw_ref[...] = jnp.zeros_like(dw_ref)
            dg_ref[...] = jnp.zeros_like(dg_ref)
            db_ref[...] = jnp.zeros_like(db_ref)

        uv = uv_ref[...].astype(F32)
        ln_gain = g_ref[...]
        u, xh, rstd, vn = _sgu_normalise(uv, ln_gain, b_ref[...])
        vb = vn.astype(BF16)
        mixed = _sgu_mix(wc_ref, vb, head0) + jnp.tile(bias_ref[...], (tm // BLK, 1))
        dout = ds_ref[...]
        du = dout * mixed
        dmixed = dout * u
        dmb = dmixed.astype(BF16)
        dvn_chunks = []
        for ch in range(tm // BLK):
            rows = slice(ch * BLK, (ch + 1) * BLK)
            dbias_ref[...] += dmixed[rows]
            pairs = []
            for p in range(SGU_W // HEAD_PAIR):
                lanes = slice(p * HEAD_PAIR, (p + 1) * HEAD_PAIR)
                dm_pair, v_pair = dmb[rows, lanes], vb[rows, lanes]
                acc = jnp.zeros((BLK, HEAD_PAIR), F32)
                for hh, half in enumerate((head0, head1)):
                    dm_h = jnp.where(half, dm_pair, jnp.zeros_like(dm_pair))
                    dw_ref[2 * p + hh] += _nt(dm_h, v_pair)
                    acc += _nn(wct_ref[2 * p + hh], dm_h)
                pairs.append(acc)
            dvn_chunks.append(jnp.concatenate(pairs, axis=1))
        dvn = jnp.concatenate(dvn_chunks, axis=0)
        dg_ref[...] += jnp.sum(dvn * xh, axis=0, keepdims=True)
        db_ref[...] += jnp.sum(dvn, axis=0, keepdims=True)
        dxh = dvn * ln_gain
        dv = rstd * (dxh - jnp.mean(dxh, axis=-1, keepdims=True) - xh * jnp.mean(dxh * xh, axis=-1, keepdims=True))
        dgelu = _gelu_grad(uv)
        duv_ref[:, :SGU_W] = (du * dgelu[:, :SGU_W]).astype(BF16)
        duv_ref[:, SGU_W:] = (dv * dgelu[:, SGU_W:]).astype(BF16)

        @pl.when(step == nsteps - 1)
        def _():
            causal = _causal()
            for g in range(8):
                dw_ref[g] = jnp.where(causal, dw_ref[g], 0.0)
            grp = lax.broadcasted_iota(jnp.int32, (1, SGU_W), 1) // 64
            col = lax.broadcasted_iota(jnp.int32, (1, 8), 1)
            dbias = dbias_ref[...]
            out = jnp.zeros((BLK, 8), F32)
            for g in range(8):
                s = jnp.sum(jnp.where(grp == g, dbias, 0.0), axis=1, keepdims=True)
                out = jnp.where(col == g, s, out)
            dbt_ref[...] = out

    vec = pl.BlockSpec((1, SGU_W), lambda i: (0, 0))
    w3 = pl.BlockSpec((8, BLK, BLK), lambda i: (0, 0, 0))
    bt = pl.BlockSpec((BLK, 8), lambda i: (0, 0))
    return _call(
        body, name="sgu_bwd", grid=(nsteps,),
        out_shape=[_array((T, 2 * SGU_W), BF16), _array((8, BLK, BLK), F32),
                   _array((BLK, 8), F32), _array((1, SGU_W), F32),
                   _array((1, SGU_W), F32)],
        in_specs=[pl.BlockSpec((tm, 2 * SGU_W), lambda i: (i, 0)), pl.BlockSpec((tm, SGU_W), lambda i: (i, 0)),
                  vec, vec, w3, bt],
        out_specs=[pl.BlockSpec((tm, 2 * SGU_W), lambda i: (i, 0)), w3, bt, vec, vec],
        scratch_shapes=[pltpu.VMEM((8, BLK, BLK), BF16), pltpu.VMEM((8, BLK, BLK), BF16),
                        pltpu.VMEM((BLK, SGU_W), F32), pltpu.VMEM((BLK, SGU_W), F32)],
        vmem_mib=40, args=(uvg, dsgu, ln_g, ln_b, w_s, b_t), carried=carried)


D_LANE0 = 8


def _attn_bwd_prepare(dattn, attn, lse, tm):
    T = dattn.shape[0]

    def body(da_ref, at_ref, lse_ref, *outs):
        scr_da, scr_st = outs[-2:]
        outs = outs[:-2]
        head0, _ = _head_lanes()
        lane = lax.broadcasted_iota(jnp.int32, (1, HEAD_PAIR), 1)
        stats = lse_ref[...]
        for s in range(N_SLABS):
            lanes = slice(s * HEAD_PAIR, (s + 1) * HEAD_PAIR)
            da = da_ref[:, lanes]
            pp = da * at_ref[:, lanes]
            d0 = jnp.sum(jnp.where(head0, pp, 0.0), axis=1, keepdims=True)
            d1 = jnp.sum(jnp.where(head0, 0.0, pp), axis=1, keepdims=True)
            stats = jnp.where(lane == D_LANE0 + 2 * s, d0, jnp.where(lane == D_LANE0 + 2 * s + 1, d1, stats))
            scr_da[s] = da
            outs[0][0, :, lanes] = da.astype(BF16)
        scr_st[0] = stats
        outs[1][0] = stats
        for g, r in enumerate(DILATIONS):
            if r > 1:
                _rows_by_residue(outs[2 * g], scr_da, r, tm // r, BF16)
                _rows_by_residue(outs[2 * g + 1], scr_st, r, tm // r, F32, n=1)

    tok = pl.BlockSpec((tm, ATTN_W), lambda i: (i, 0))
    shapes, specs = [], []
    for r in DILATIONS:
        shapes += [_array((r, T // r, ATTN_W), BF16), _array((r, T // r, HEAD_PAIR), F32)]
        specs += [pl.BlockSpec((r, tm // r, ATTN_W), lambda i: (0, i, 0)),
                  pl.BlockSpec((r, tm // r, HEAD_PAIR), lambda i: (0, i, 0))]
    return _pallas(
        body, name="attn_bwd_prepare", grid=(T // tm,), out_shape=shapes,
        in_specs=[tok, tok, pl.BlockSpec((tm, HEAD_PAIR), lambda i: (i, 0))],
        out_specs=specs, scratch_shapes=[_slab_scratch(tm), _slab_scratch(tm, 1)],
        compiler_params=_params(("parallel",), 40),
    )(dattn, attn, lse)


def _attn_bwd(qkv, dattn, stats, group, carried=None):
    r, L, _ = qkv.shape
    nsub = _attn_sub_blocks(L)
    rows = nsub * BLK
    nb = L // rows
    keep = rows - BLK
    whole = nb == 1

    def body(q_ref, kp_ref, kc_ref, vp_ref, vc_ref, da_ref, st_ref, dq_ref, dk_ref, dv_ref, carry_k, carry_v):
        n = pl.program_id(1)

        if not whole:
            @pl.when(n == 0)
            def _():
                carry_k[...] = jnp.zeros_like(carry_k)
                carry_v[...] = jnp.zeros_like(carry_v)

        @pl.when(n < nb)
        def _():
            head0, head1 = _head_lanes()
            for p in range(N_SLABS):
                sl = slice(p * HEAD_PAIR, (p + 1) * HEAD_PAIR)
                dk_parts = [jnp.zeros((BLK, HEAD_PAIR), F32) for _ in range(nsub + 1)]
                dv_parts = [jnp.zeros((BLK, HEAD_PAIR), F32) for _ in range(nsub + 1)]
                for j in range(nsub):
                    cur = slice(j * BLK, (j + 1) * BLK)
                    before = slice((j - 1) * BLK, j * BLK)
                    valid = _band_mask(n + j)
                    st = st_ref[cur, :]
                    k_prev, v_prev = (kp_ref[:, sl], vp_ref[:, sl]) if j == 0 else (kc_ref[before, sl], vc_ref[before, sl])
                    k2 = jnp.concatenate([k_prev, kc_ref[cur, sl]], axis=0)
                    v2 = jnp.concatenate([v_prev, vc_ref[cur, sl]], axis=0)
                    qs = _stack_heads(q_ref[cur, sl], head0, head1)
                    das = _stack_heads(da_ref[cur, sl], head0, head1)
                    prob = jnp.where(valid, jnp.exp(_nt(qs, k2) - _per_head(st, 2 * p)), 0.0)
                    ds = (prob * (_nt(das, v2) - _per_head(st, D_LANE0 + 2 * p))).astype(BF16)
                    dk2 = _tn(ds, qs)
                    dv2 = _tn(prob.astype(BF16), das)
                    dq_ref[cur, sl] = _unstack_heads(_nn(ds, k2), head0)
                    dk_parts[j] += dk2[:BLK]
                    dk_parts[j + 1] += dk2[BLK:]
                    dv_parts[j] += dv2[:BLK]
                    dv_parts[j + 1] += dv2[BLK:]
                if whole:
                    for j in range(nsub):
                        dk_ref[j * BLK:(j + 1) * BLK, sl] = dk_parts[j + 1]
                        dv_ref[j * BLK:(j + 1) * BLK, sl] = dv_parts[j + 1].astype(BF16)
                    continue
                if keep:
                    dk_ref[:keep, sl] = carry_k[:keep, sl]
                    dv_ref[:keep, sl] = carry_v[:keep, sl].astype(BF16)
                dk_ref[keep:, sl] = carry_k[keep:, sl] + dk_parts[0]
                dv_ref[keep:, sl] = (carry_v[keep:, sl] + dv_parts[0]).astype(BF16)
                for j in range(nsub):
                    carry_k[j * BLK:(j + 1) * BLK, sl] = dk_parts[j + 1]
                    carry_v[j * BLK:(j + 1) * BLK, sl] = dv_parts[j + 1]

        if not whole:
            @pl.when(n == nb)
            def _():
                dk_ref[...] = carry_k[...]
                dv_ref[...] = carry_v[...].astype(BF16)

    def cur_blk(kind, width=ATTN_W):
        return pl.BlockSpec((None, rows, width), lambda rho, n: (rho, jnp.minimum(n, nb - 1), kind))

    def last_blk(kind):
        return pl.BlockSpec((None, rows, ATTN_W), lambda rho, n: (rho, jnp.clip(n - 1, 0, nb - 1), kind))

    def prev_keys(kind):
        return pl.BlockSpec((None, BLK, ATTN_W),
                            lambda rho, n: (rho, jnp.clip(n * nsub - 1, 0, nb * nsub - 1), kind))

    res = _array((r, L, ATTN_W), F32)
    return _call(
        body, name=f"attn_bwd_g{group}", grid=(r, 1 if whole else nb + 1),
        out_shape=[res, res, _array((r, L, ATTN_W), BF16)],
        in_specs=[cur_blk(0), prev_keys(1), cur_blk(1), prev_keys(2), cur_blk(2), cur_blk(0), cur_blk(0, HEAD_PAIR)],
        out_specs=[cur_blk(0), last_blk(0), last_blk(0)],
        scratch_shapes=[pltpu.VMEM((rows, ATTN_W), F32), pltpu.VMEM((rows, ATTN_W), F32)],
        vmem_mib=32, args=(qkv, qkv, qkv, qkv, qkv, dattn, stats), carried=carried)


def _dqkv_token_order(dqkv_groups, tables, tm):
    T = tables[0].shape[0]

    def body(*refs):
        ins = refs[:9]
        cos_ref, s1_ref, s2_ref, o_ref, scr = refs[9:]
        cos, s1, s2 = cos_ref[...], s1_ref[...], s2_ref[...]
        for g, r in enumerate(DILATIONS):
            for kind in range(3):
                src = ins[3 * g + kind]
                if r > 1:
                    _rows_by_token(scr, src, r, tm // r)
                for s in range(N_SLABS):
                    val = scr[s] if r > 1 else src[0, :, s * HEAD_PAIR:(s + 1) * HEAD_PAIR]
                    if kind < 2:
                        val = _rope_transposed(val, cos, s1, s2)
                    if kind == 0:
                        val = val * 0.125
                    at = (3 * kind + g) * ATTN_W + s * HEAD_PAIR
                    o_ref[:, at:at + HEAD_PAIR] = val.astype(BF16)

    specs = []
    for r in DILATIONS:
        specs += [pl.BlockSpec((r, tm // r, ATTN_W), lambda i: (0, i, 0))] * 3
    row = pl.BlockSpec((tm, HEAD_PAIR), lambda i: (i, 0))
    flat = [a for grp in dqkv_groups for a in grp]
    return _pallas(
        body, name="dqkv_token_order", grid=(T // tm,), out_shape=_array((T, QKV_COLS), BF16),
        in_specs=specs + [row] * 3, out_specs=pl.BlockSpec((tm, QKV_COLS), lambda i: (i, 0)),
        scratch_shapes=[_slab_scratch(tm)], compiler_params=_params(("parallel",), 48),
    )(*flat, *tables)


def _row_tile(rows):
    for cand in (320, 256, 176, 128):
        if rows % cand == 0:
            return cand
    return rows


def _pair_sum(grad4, recv, chip, name):
    _, _, rows, cols = grad4.shape
    tr = _row_tile(rows)

    def body(ids_ref, g_ref, r_ref, gown_ref, rown_ref, sum_ref, own_ref):
        sum_ref[...] = (g_ref[...] + r_ref[...]).astype(BF16)

        @pl.when(pl.program_id(1) == 0)
        def _():
            own_ref[...] = gown_ref[...] + rown_ref[...]

    grid_spec = pltpu.PrefetchScalarGridSpec(
        num_scalar_prefetch=1, grid=(rows // tr, 4),
        in_specs=[pl.BlockSpec((None, None, tr, cols), lambda i, q, ids: (q, ids[1], i, 0)),
                  pl.BlockSpec((None, tr, cols), lambda i, q, ids: (q, i, 0)),
                  pl.BlockSpec((None, None, tr, cols), lambda i, q, ids: (ids[0], ids[1], i, 0)),
                  pl.BlockSpec((None, tr, cols), lambda i, q, ids: (ids[0], i, 0))],
        out_specs=[pl.BlockSpec((None, tr, cols), lambda i, q, ids: (q, i, 0)),
                   pl.BlockSpec((tr, cols), lambda i, q, ids: (i, 0))])
    return _pallas(
        body, name=name, grid_spec=grid_spec,
        out_shape=[_array((4, rows, cols), BF16), _array((rows, cols), F32)],
        compiler_params=_params(("arbitrary", "arbitrary"), 32),
    )(chip, grad4, recv, grad4, recv)


def _chip_sum(own, others, name):
    rows, cols = own.shape
    tr = _row_tile(rows)

    def body(own_ref, oth_ref, o_ref):
        total = own_ref[...]
        for j in range(3):
            total = total + oth_ref[j].astype(F32)
        o_ref[...] = total

    blk = pl.BlockSpec((tr, cols), lambda i: (i, 0))
    return _pallas(
        body, name=name, grid=(rows // tr,), out_shape=_array((rows, cols), F32),
        in_specs=[blk, pl.BlockSpec((3, tr, cols), lambda i: (0, i, 0))], out_specs=blk,
        compiler_params=_params(("parallel",), 32),
    )(own, others)


def _adam_math(w, g, m, v):
    m = ADAM_B1 * m + (1.0 - ADAM_B1) * g
    v = ADAM_B2 * v + (1.0 - ADAM_B2) * (g * g)
    m_hat = m / (1.0 - ADAM_B1 ** ADAM_STEP)
    v_hat = v / (1.0 - ADAM_B2 ** ADAM_STEP)
    delta = -ADAM_LR * (m_hat / (jnp.sqrt(v_hat) + ADAM_EPS) + ADAM_WD * w)
    return delta, m, v


def _adamw(w, g, m, v, name):
    rows, cols = w.shape
    tr = _row_tile(rows)

    def body(w_ref, g_ref, m_ref, v_ref, d_ref, nm_ref, nv_ref):
        d_ref[...], nm_ref[...], nv_ref[...] = _adam_math(w_ref[...], g_ref[...], m_ref[...], v_ref[...])

    blk = pl.BlockSpec((tr, cols), lambda i: (i, 0))
    res = _array((rows, cols), F32)
    return _pallas(
        body, name=name, grid=(rows // tr,), out_shape=[res, res, res], in_specs=[blk] * 4, out_specs=[blk] * 3,
        compiler_params=_params(("parallel",), 32),
    )(w, g, m, v)


def _small_update(late_parts, early_parts, w, m, v, loss_rows):
    rows = w.shape[0]

    def total(p_ref):
        n = p_ref.shape[0] // N_DEV
        acc = p_ref[0:n, :]
        for dev in range(1, N_DEV):
            acc = acc + p_ref[dev * n:(dev + 1) * n, :]
        return acc

    def body(late_ref, early_ref, w_ref, m_ref, v_ref, g_ref, d_ref, nm_ref, nv_ref, loss_ref):
        g = jnp.concatenate([total(late_ref), total(early_ref)], axis=0)
        g_ref[...] = g
        d_ref[...], nm_ref[...], nv_ref[...] = _adam_math(w_ref[...], g, m_ref[...], v_ref[...])
        loss_ref[...] = jnp.sum(jnp.sum(g[rows - loss_rows:, :], axis=1, keepdims=True), axis=0, keepdims=True)

    res = jax.ShapeDtypeStruct((rows, HEAD_PAIR), F32)
    return _pallas(
        body, name="small_update", out_shape=[res, res, res, res, jax.ShapeDtypeStruct((1, 1), F32)],
        compiler_params=pltpu.CompilerParams(vmem_limit_bytes=32 * MIB),
    )(late_parts, early_parts, w, m, v)


def kernel(x, positions, norm1_g, w_in, sgu_ln_g, sgu_ln_b, w_spatial, b_spatial, w_proj_attn, w_proj_sgu, w_out, norm2_g, w_ffn_gate, w_ffn_up, w_ffn_down, final_g, loss_target, m_norm1_g, m_w_in, m_sgu_ln_g, m_sgu_ln_b, m_w_spatial, m_b_spatial, m_w_proj_attn, m_w_proj_sgu, m_w_out, m_norm2_g, m_w_ffn_gate, m_w_ffn_up, m_w_ffn_down, m_final_g, v_norm1_g, v_w_in, v_sgu_ln_g, v_sgu_ln_b, v_w_spatial, v_b_spatial, v_w_proj_attn, v_w_proj_sgu, v_w_out, v_norm2_g, v_w_ffn_gate, v_w_ffn_up, v_w_ffn_down, v_final_g):
    T = x.shape[1]
    tm = 512
    xt = x[0]
    target = loss_target[0]
    chip = jnp.stack([2 * lax.axis_index("x") + lax.axis_index("y"), lax.axis_index("c")]).astype(jnp.int32)

    def bf16_rows(w, transpose):
        return (jnp.transpose(w[0]) if transpose else w[0]).astype(BF16)

    w_in_gather = _all_gather([bf16_rows(w_in, True)])
    ffn_in = _all_gather([bf16_rows(w_ffn_gate, True), bf16_rows(w_ffn_up, True)])
    mixers = _all_gather([bf16_rows(w_out, False), bf16_rows(w_proj_attn, True), bf16_rows(w_proj_sgu, True)])
    ffn_out = _all_gather([bf16_rows(w_ffn_down, False)])

    inv_freq = ROPE_THETA ** (-jnp.arange(0, 2 * ROPE_HALF, 2, dtype=F32) / (2 * ROPE_HALF))
    inv_freq_row = jnp.tile(jnp.concatenate([inv_freq, inv_freq, jnp.zeros((48,), F32)]), 2).reshape(1, HEAD_PAIR)
    tables = _rope_tables(positions.reshape(T, 1), inv_freq_row, tm)
    b_t = jnp.transpose(b_spatial[0])

    (h,), (win_t,) = _rmsnorm_fwd(xt, norm1_g, "norm1_fwd", tm, carried=w_in_gather)
    qkv, (wg_t, wu_t) = _qkv_proj(h, win_t, tables, tm, carried=ffn_in)
    (uvg,), (wout, wpa_t, wps_t) = _uv_gate_proj(h, win_t, tm, carried=mixers)
    fwd0, (wd,) = _attn_fwd(qkv[0], 0, carried=ffn_out)
    fwd = [fwd0, _attn_fwd(qkv[1], 1)[0], _attn_fwd(qkv[2], 2)[0]]
    attn, attn_b, lse = _attn_combine([f[0] for f in fwd], [f[1] for f in fwd], tm)
    sgu = _sgu_fwd(uvg, sgu_ln_g, sgu_ln_b, w_spatial[0], b_t, tm)
    pa, ps, merged, h2, x1 = _merge_fwd(attn_b, sgu, wpa_t, wps_t, uvg, wout, xt, norm2_g, tm)
    gate, up, ff = _ffn_fwd(h2, wg_t, wu_t, tm)
    dx2, dx2b, loss_cols, d_final_g = _down_loss(ff, wd, x1, final_g.reshape(1, D_MODEL), target, tm)

    dgate, dup, dx1, dx1b, d_norm2 = _ffn_bwd(dx2, dx2b, wd, wg_t, wu_t, gate, up, x1, norm2_g, tm // 2)
    tk = min(2048, T)
    d_wd = _mm_tn(ff, dx2b, tmm=FF_TILE, tk=tk, name="grad_w_ffn_down")
    d_wg_t = _mm_tn(dgate, h2, tmm=FF_TILE, tk=tk, name="grad_w_ffn_gate")
    d_wu_t = _mm_tn(dup, h2, tmm=FF_TILE, tk=tk, name="grad_w_ffn_up")

    def by_owner(grads):
        return [g.reshape(4, 2, g.shape[0] // N_DEV, g.shape[1]) for g in grads]

    def pair_sums(grads4, from_sibling, names):
        both = [_pair_sum(g4, rv, chip, "grad_pair_sum_" + nm) for g4, rv, nm in zip(grads4, from_sibling, names)]
        return [b[0] for b in both], [b[1] for b in both]

    ffn_names = ["w_ffn_gate", "w_ffn_up", "w_ffn_down"]
    ffn4 = by_owner([d_wg_t, d_wu_t, d_wd])
    (dpa, dps, dgates, dattn, dsgu), ffn_sib = _merge_bwd(dx1b, wout, pa, ps, uvg, wpa_t, wps_t, tm,
                                                          carried=_pair_exchange(ffn4))
    ffn_sums, ffn_own = pair_sums(ffn4, ffn_sib, ffn_names)

    d_wout = _mm_tn(merged, dx1b, tmm=D_MODEL, tk=tk, name="grad_w_out")
    d_wpa_t = _mm_tn(dpa, attn_b, tmm=D_MODEL, tk=tk, name="grad_w_proj_attn")
    d_wps_t = _mm_tn(dps, sgu, tmm=D_MODEL, tk=tk, name="grad_w_proj_sgu")
    mid_names = ["w_proj_attn", "w_proj_sgu", "w_out"]
    mid4 = by_owner([d_wpa_t, d_wps_t, d_wout])
    (duv, d_ws, d_bs_t, d_ln_g, d_ln_b), mid_sib = _sgu_bwd(uvg, dsgu, sgu_ln_g, sgu_ln_b, w_spatial[0], b_t, tm,
                                                           carried=_pair_exchange(mid4))
    mid_sums, mid_own = pair_sums(mid4, mid_sib, mid_names)

    prep = _attn_bwd_prepare(dattn, attn, lse, tm)
    dqkv0, ffn_far = _attn_bwd(qkv[0], prep[0], prep[1], 0, carried=_chip_exchange(ffn_sums))
    dqkv1, mid_far = _attn_bwd(qkv[1], prep[2], prep[3], 1, carried=_chip_exchange(mid_sums))
    def flat(parts):
        return jnp.concatenate([p.reshape(-1) for p in parts]).reshape(-1, HEAD_PAIR)

    early_part = flat([d_ln_g, d_ln_b, d_ws, jnp.transpose(d_bs_t), d_norm2, d_final_g, loss_cols])
    dqkv2, (early_parts,) = _attn_bwd(qkv[2], prep[4], prep[5], 2, carried=_all_gather([early_part]))
    dqkv = _dqkv_token_order([dqkv0, dqkv1, dqkv2], tables, tm)
    d_win_t = _mm_tn(dqkv, h, tmm=1536, tk=tk, name="grad_w_in_qkv", rows_total=IN_COLS)
    d_win_t = _mm_tn(duv, h, tmm=512, tk=tk, name="grad_w_in_uv", into=d_win_t, row_block0=9, rows_total=IN_COLS)
    d_win_t = _mm_tn(dgates, h, tmm=512, tk=tk, name="grad_w_in_gates", into=d_win_t, row_block0=11, rows_total=IN_COLS)
    in4 = by_owner([d_win_t])
    in_sib = _pair_exchange(in4).run_alone("w_in_grad_pair_exchange")
    in_sums, in_own = pair_sums(in4, in_sib, ["w_in"])
    (dx, d_norm1), in_far = _in_bwd(dqkv, duv, dgates, win_t, xt, norm1_g, dx1, tm, carried=_chip_exchange(in_sums))

    names = ["w_in"] + mid_names + ffn_names
    reduced = [_chip_sum(o, f, "grad_total_" + nm)
               for o, f, nm in zip(in_own + mid_own + ffn_own, in_far + mid_far + ffn_far, names)]
    transposed = (True, True, True, False, True, True, False)
    g_big = [jnp.transpose(r) if t else r for r, t in zip(reduced, transposed)]

    small_w = [norm1_g, sgu_ln_g, sgu_ln_b, w_spatial, b_spatial, norm2_g, final_g]
    small_m = [m_norm1_g, m_sgu_ln_g, m_sgu_ln_b, m_w_spatial, m_b_spatial, m_norm2_g, m_final_g]
    small_v = [v_norm1_g, v_sgu_ln_g, v_sgu_ln_b, v_w_spatial, v_b_spatial, v_norm2_g, v_final_g]
    zeros = jnp.zeros((D_MODEL,), F32)
    (late_parts,) = _all_gather([flat([d_norm1])]).run_alone("norm1_grad_all_gather", vmem=True)
    g_s, d_s, nm_s, nv_s, loss = _small_update(late_parts, early_parts, flat(small_w + [zeros]), flat(small_m + [zeros]),
                                               flat(small_v + [zeros]), D_MODEL // HEAD_PAIR)

    def unflat(vec):
        vec = vec.reshape(-1)
        out, at = [], 0
        for wgt in small_w:
            out.append(vec[at:at + wgt.size].reshape(wgt.shape))
            at += wgt.size
        return out

    small = [unflat(a) for a in (g_s, d_s, nm_s, nv_s)]

    big_w = [w_in, w_proj_attn, w_proj_sgu, w_out, w_ffn_gate, w_ffn_up, w_ffn_down]
    big_m = [m_w_in, m_w_proj_attn, m_w_proj_sgu, m_w_out, m_w_ffn_gate, m_w_ffn_up, m_w_ffn_down]
    big_v = [v_w_in, v_w_proj_attn, v_w_proj_sgu, v_w_out, v_w_ffn_gate, v_w_ffn_up, v_w_ffn_down]
    big_out = []
    for wgt, g, mm, vv, nm in zip(big_w, g_big, big_m, big_v, names):
        d, nm_, nv_ = _adamw(wgt[0], g, mm[0], vv[0], "adamw_" + nm)
        big_out.append([a[None] for a in (g, d, nm_, nv_)])

    small_at = {0: 0, 2: 1, 3: 2, 4: 3, 5: 4, 9: 5, 13: 6}
    big_at = {1: 0, 6: 1, 7: 2, 8: 3, 10: 4, 11: 5, 12: 6}
    outs = [loss[0, 0], dx[None]]
    for kind in range(4):
        for idx in range(14):
            outs.append(small[kind][small_at[idx]] if idx in small_at else big_out[big_at[idx]][kind])
    return tuple(outs)
```

```python
import functools
import math

import jax
import jax.numpy as jnp
from jax import lax
from jax.experimental import pallas as pl
from jax.experimental.pallas import tpu as pltpu

F32 = jnp.float32
BF16 = jnp.bfloat16

D_MODEL = 1024
HEAD_PAIR = 128
ATTN_W = 512
DILATIONS = (1, 4, 16)
BLK = 128
ROPE_HALF = 8
ROPE_THETA = 500000.0
SGU_W = 512
QKV_COLS = 4608
IN_COLS = 7680
D_FF = 2816
FF_TILE = 1408
EPS = 1e-6
N_DEV = 8
MASKED = -1e30

ADAM_LR = 0.001
ADAM_B1 = 0.9
ADAM_B2 = 0.999
ADAM_EPS = 1e-08
ADAM_WD = 0.01
ADAM_STEP = 10

MIB = 1024 * 1024
MESH = pl.DeviceIdType.MESH
ANY = pl.BlockSpec(memory_space=pl.ANY)


def _array(shape, dtype):
    return pltpu.HBM(tuple(shape), dtype)


PIN_BYTES = 4 * MIB


def _pin(x):
    if x.size * x.dtype.itemsize < PIN_BYTES:
        return x
    return pltpu.with_memory_space_constraint(x, pltpu.HBM)


def _pallas(body, **kwargs):
    call = pl.pallas_call(body, **kwargs)
    return lambda *args: call(*[_pin(a) for a in args])


def _params(sem, vmem_mib):
    return pltpu.CompilerParams(dimension_semantics=sem, vmem_limit_bytes=vmem_mib * MIB)


def _nt(a, b):
    return lax.dot_general(a, b, (((1,), (1,)), ((), ())), preferred_element_type=F32)


def _nn(a, b):
    return lax.dot_general(a, b, (((1,), (0,)), ((), ())), preferred_element_type=F32)


def _tn(a, b):
    return lax.dot_general(a, b, (((0,), (0,)), ((), ())), preferred_element_type=F32)


class _Exchange:
    def __init__(self, arrays, out_shapes, sem_shapes, phases):
        self.arrays, self.out_shapes, self.sem_shapes, self.phases = list(arrays), out_shapes, sem_shapes, phases

    def run_alone(self, name, vmem=False):
        n_in, n_out = len(self.arrays), len(self.out_shapes)

        def body(*refs):
            start, middle, finish = self.phases(refs[:n_in], refs[n_in:n_in + n_out], refs[n_in + n_out:])
            start()
            middle()
            finish()

        spec = pl.BlockSpec(memory_space=pltpu.VMEM) if vmem else ANY
        shapes = [jax.ShapeDtypeStruct(s.shape, s.dtype) for s in self.out_shapes] if vmem else self.out_shapes
        return (pl.pallas_call if vmem else _pallas)(
            body, name=name, out_shape=shapes, in_specs=[spec] * n_in, out_specs=[spec] * n_out,
            scratch_shapes=self.sem_shapes, compiler_params=pltpu.CompilerParams(vmem_limit_bytes=32 * MIB),
        )(*self.arrays)


def _call(body, *, name, grid, in_specs, out_specs, out_shape, args, vmem_mib, scratch_shapes=(), carried=None):
    n_in, n_out, n_scr = len(in_specs), len(out_specs), len(scratch_shapes)
    sem = ("arbitrary",) * len(grid)
    if carried is None:
        outs = _pallas(
            body, name=name, grid=grid, in_specs=in_specs, out_specs=out_specs, out_shape=out_shape,
            scratch_shapes=list(scratch_shapes), compiler_params=_params(sem, vmem_mib))(*args)
        return list(outs), []
    c_in, c_out = len(carried.arrays), len(carried.out_shapes)
    total = math.prod(grid)

    def full(*refs):
        own_in, car_in = refs[:n_in], refs[n_in:n_in + c_in]
        at = n_in + c_in
        own_out, car_out = refs[at:at + n_out], refs[at + n_out:at + n_out + c_out]
        at += n_out + c_out
        own_scr, sems = refs[at:at + n_scr], refs[at + n_scr:]
        step = pl.program_id(0)
        for axis in range(1, len(grid)):
            step = step * grid[axis] + pl.program_id(axis)
        start, middle, finish = carried.phases(car_in, car_out, sems)
        pl.when(step == 0)(start)
        pl.when(step == (3 * total) // 4)(middle)
        body(*own_in, *own_out, *own_scr)
        pl.when(step == total - 1)(finish)

    outs = _pallas(
        full, name=name, grid=grid, in_specs=list(in_specs) + [ANY] * c_in,
        out_specs=list(out_specs) + [ANY] * c_out, out_shape=list(out_shape) + list(carried.out_shapes),
        scratch_shapes=list(scratch_shapes) + list(carried.sem_shapes),
        compiler_params=_params(sem, vmem_mib))(*args, *carried.arrays)
    return list(outs[:n_out]), list(outs[n_out:])


def _all_gather(shards):
    n = len(shards)

    def phases(ins, outs, sems):
        send_sems, recv_sems, local_sems = sems
        x, y, c = lax.axis_index("x"), lax.axis_index("y"), lax.axis_index("c")
        me, sibling = (x, y, c), (x, y, 1 - c)
        chips = [(1 - x, y), (x, 1 - y), (1 - x, 1 - y)]

        def rows(m, px, py, pc):
            r = ins[m].shape[0]
            return outs[m].at[pl.ds((4 * px + 2 * py + pc) * r, r), :]

        def copy(m, k, block, to, src=None):
            return pltpu.make_async_remote_copy(
                src_ref=rows(m, *block) if src is None else src, dst_ref=rows(m, *block),
                send_sem=send_sems.at[m, k], recv_sem=recv_sems.at[m, k],
                device_id=to, device_id_type=MESH)

        def mine(m):
            return pltpu.make_async_copy(ins[m], rows(m, *me), local_sems.at[m])

        def first(m):
            return [copy(m, 0, me, sibling, src=ins[m])] + [
                copy(m, 1 + j, me, (*chip, c), src=ins[m]) for j, chip in enumerate(chips)]

        def passed(m):
            return [copy(m, 4 + j, (*chip, c), sibling) for j, chip in enumerate(chips)]

        def start():
            for m in range(n):
                mine(m).start()
            for m in range(n):
                for cp in first(m):
                    cp.start()

        def middle():
            for m in range(n):
                for j, chip in enumerate(chips):
                    copy(m, 1 + j, (*chip, c), me).wait_recv()
                    passed(m)[j].start()

        def finish():
            for m in range(n):
                copy(m, 0, sibling, me).wait_recv()
                for j, chip in enumerate(chips):
                    copy(m, 4 + j, (*chip, 1 - c), me).wait_recv()
            for m in range(n):
                for cp in first(m) + passed(m):
                    cp.wait_send()
                mine(m).wait()

        return start, middle, finish

    return _Exchange(
        shards, [_array((N_DEV * s.shape[0], s.shape[1]), s.dtype) for s in shards],
        [pltpu.SemaphoreType.DMA((n, 7)), pltpu.SemaphoreType.DMA((n, 7)), pltpu.SemaphoreType.DMA((n,))], phases)


def _pair_exchange(grads):
    n = len(grads)

    def phases(ins, outs, sems):
        send_sems, recv_sems = sems
        x, y, c = lax.axis_index("x"), lax.axis_index("y"), lax.axis_index("c")

        def copy(m):
            return pltpu.make_async_remote_copy(
                src_ref=ins[m].at[:, 1 - c], dst_ref=outs[m], send_sem=send_sems.at[m], recv_sem=recv_sems.at[m],
                device_id=(x, y, 1 - c), device_id_type=MESH)

        def start():
            for m in range(n):
                copy(m).start()

        def finish():
            for m in range(n):
                copy(m).wait()

        return start, lambda: None, finish

    return _Exchange(grads, [_array((4,) + g.shape[2:], g.dtype) for g in grads],
                     [pltpu.SemaphoreType.DMA((n,)), pltpu.SemaphoreType.DMA((n,))], phases)


def _chip_exchange(pair_sums):
    n = len(pair_sums)

    def phases(ins, outs, sems):
        send_sems, recv_sems = sems
        x, y, c = lax.axis_index("x"), lax.axis_index("y"), lax.axis_index("c")
        chips = [(1 - x, y), (x, 1 - y), (1 - x, 1 - y)]

        def copies():
            return [pltpu.make_async_remote_copy(
                src_ref=ins[m].at[2 * px + py], dst_ref=outs[m].at[j],
                send_sem=send_sems.at[m, j], recv_sem=recv_sems.at[m, j],
                device_id=(px, py, c), device_id_type=MESH)
                for m in range(n) for j, (px, py) in enumerate(chips)]

        def start():
            for cp in copies():
                cp.start()

        def finish():
            for cp in copies():
                cp.wait_recv()
            for cp in copies():
                cp.wait_send()

        return start, lambda: None, finish

    return _Exchange(pair_sums, [_array((3,) + p.shape[1:], p.dtype) for p in pair_sums],
                     [pltpu.SemaphoreType.DMA((n, 3)), pltpu.SemaphoreType.DMA((n, 3))], phases)


N_SLABS = ATTN_W // HEAD_PAIR


def _slab_scratch(tm, n=N_SLABS):
    return pltpu.VMEM((n, tm, HEAD_PAIR), F32)


def _rows_by_residue(dst_ref, slab_ref, r, tr, dtype, n=N_SLABS):
    for rho in range(r):
        for s in range(n):
            dst_ref[rho, :, s * HEAD_PAIR:(s + 1) * HEAD_PAIR] = (
                slab_ref[s, pl.ds(rho, tr, stride=r), :].astype(dtype))


def _rows_by_token(slab_ref, src_ref, r, tr, n=N_SLABS):
    for rho in range(r):
        for s in range(n):
            slab_ref[s, pl.ds(rho, tr, stride=r), :] = (
                src_ref[rho, :, s * HEAD_PAIR:(s + 1) * HEAD_PAIR].astype(F32))


def _rope_tables(pos_col, inv_freq_row, tm):
    T = pos_col.shape[0]

    def body(pos_ref, invf_ref, cos_ref, s1_ref, s2_ref):
        ang = pos_ref[...].astype(F32) * invf_ref[...]
        lane = lax.broadcasted_iota(jnp.int32, (1, HEAD_PAIR), 1) % 64
        cs, sn = jnp.cos(ang), jnp.sin(ang)
        cos_ref[...] = jnp.where(lane < 2 * ROPE_HALF, cs, 1.0)
        s1_ref[...] = jnp.where(lane < ROPE_HALF, -sn, 0.0)
        s2_ref[...] = jnp.where((lane >= ROPE_HALF) & (lane < 2 * ROPE_HALF), sn, 0.0)

    tab = _array((T, HEAD_PAIR), F32)
    row = pl.BlockSpec((tm, HEAD_PAIR), lambda i: (i, 0))
    return _pallas(
        body, name="rope_tables", grid=(T // tm,), out_shape=[tab] * 3,
        in_specs=[pl.BlockSpec((tm, 1), lambda i: (i, 0)), pl.BlockSpec((1, HEAD_PAIR), lambda i: (0, 0))],
        out_specs=[row] * 3, compiler_params=_params(("parallel",), 16),
    )(pos_col, inv_freq_row)


def _rope(y, cos, s1, s2):
    w = y.shape[1]
    rep = w // HEAD_PAIR
    return (y * jnp.tile(cos, (1, rep)) + pltpu.roll(y, w - ROPE_HALF, 1) * jnp.tile(s1, (1, rep))
            + pltpu.roll(y, ROPE_HALF, 1) * jnp.tile(s2, (1, rep)))


def _rope_transposed(dy, cos, s1, s2):
    w = dy.shape[1]
    rep = w // HEAD_PAIR
    return (dy * jnp.tile(cos, (1, rep)) + pltpu.roll(dy * jnp.tile(s1, (1, rep)), ROPE_HALF, 1)
            + pltpu.roll(dy * jnp.tile(s2, (1, rep)), w - ROPE_HALF, 1))


def _rmsnorm_fwd(x, g, name, tm, carried=None):
    T = x.shape[0]

    def body(x_ref, g_ref, h_ref):
        xf = x_ref[...]
        r = lax.rsqrt(jnp.mean(xf * xf, axis=-1, keepdims=True) + EPS)
        h_ref[...] = (xf * r * g_ref[...]).astype(BF16)

    return _call(
        body, name=name, grid=(T // tm,), out_shape=[_array((T, D_MODEL), BF16)],
        in_specs=[pl.BlockSpec((tm, D_MODEL), lambda i: (i, 0)), pl.BlockSpec((1, D_MODEL), lambda i: (0, 0))],
        out_specs=[pl.BlockSpec((tm, D_MODEL), lambda i: (i, 0))], vmem_mib=24, args=(x, g), carried=carried)


def _resident(shape, block=None):
    at = (0,) * len(shape) if block is None else block
    return pl.BlockSpec(shape, lambda *_: at, pipeline_mode=pl.Buffered(1))


def _qkv_proj(h, win_t, tables, tm, carried=None):
    T = h.shape[0]

    def body(h_ref, w_ref, cos_ref, s1_ref, s2_ref, o0, o1, o2, *slabs):
        hv = h_ref[...]
        cos, s1, s2 = cos_ref[...], s1_ref[...], s2_ref[...]
        for kind in range(3):
            for g, (o_ref, r) in enumerate(zip((o0, o1, o2), DILATIONS)):
                blk = 3 * kind + g
                y = _nt(hv, w_ref[blk * ATTN_W:(blk + 1) * ATTN_W, :])
                if kind < 2:
                    y = _rope(y, cos, s1, s2)
                if kind == 0:
                    y = y * 0.125
                cols = slice(kind * ATTN_W, (kind + 1) * ATTN_W)
                if r == 1:
                    o_ref[0, :, cols] = y.astype(BF16)
                    continue
                slab = slabs[blk % len(slabs)]
                for s in range(N_SLABS):
                    slab[s] = y[:, s * HEAD_PAIR:(s + 1) * HEAD_PAIR]
                for rho in range(r):
                    for s in range(N_SLABS):
                        at = kind * ATTN_W + s * HEAD_PAIR
                        o_ref[rho, :, at:at + HEAD_PAIR] = slab[s, pl.ds(rho, tm // r, stride=r), :].astype(BF16)

    row = pl.BlockSpec((tm, HEAD_PAIR), lambda i: (i, 0))
    return _call(
        body, name="qkv_proj", grid=(T // tm,),
        out_shape=[_array((r, T // r, 3 * ATTN_W), BF16) for r in DILATIONS],
        in_specs=[pl.BlockSpec((tm, D_MODEL), lambda i: (i, 0)), _resident((QKV_COLS, D_MODEL)), row, row, row],
        out_specs=[pl.BlockSpec((r, tm // r, 3 * ATTN_W), lambda i: (0, i, 0)) for r in DILATIONS],
        scratch_shapes=[_slab_scratch(tm)] * 3, vmem_mib=48, args=(h, win_t, *tables), carried=carried)


def _uv_gate_proj(h, win_t, tm, carried=None):
    T = h.shape[0]
    half = (IN_COLS - QKV_COLS) // 2

    def body(h_ref, wa_ref, wb_ref, o_ref):
        hv = h_ref[...]
        o_ref[:, :half] = _nt(hv, wa_ref[...]).astype(BF16)
        o_ref[:, half:] = _nt(hv, wb_ref[...]).astype(BF16)

    blk0 = QKV_COLS // half
    return _call(
        body, name="uv_gate_proj", grid=(T // tm,), out_shape=[_array((T, 2 * half), BF16)],
        in_specs=[pl.BlockSpec((tm, D_MODEL), lambda i: (i, 0)), _resident((half, D_MODEL), (blk0, 0)),
                  _resident((half, D_MODEL), (blk0 + 1, 0))],
        out_specs=[pl.BlockSpec((tm, 2 * half), lambda i: (i, 0))], vmem_mib=40, args=(h, win_t, win_t),
        carried=carried)


def _band_mask(n):
    row = lax.broadcasted_iota(jnp.int32, (2 * BLK, 2 * BLK), 0) & (BLK - 1)
    col = lax.broadcasted_iota(jnp.int32, (2 * BLK, 2 * BLK), 1)
    has_prev = (jnp.zeros_like(row) + n) > 0
    return ((col < BLK) & (col >= row) & has_prev) | ((col >= BLK) & (col - BLK <= row))


def _head_lanes():
    lane = lax.broadcasted_iota(jnp.int32, (1, HEAD_PAIR), 1)
    return lane < 64, lane >= 64


def _stack_heads(x, head0, head1):
    zero = jnp.zeros_like(x)
    return jnp.concatenate([jnp.where(head0, x, zero), jnp.where(head1, x, zero)], axis=0)


def _unstack_heads(y, head0):
    return jnp.where(head0, y[:BLK], y[BLK:])


def _per_head(stats, col):
    return jnp.concatenate([stats[:, col:col + 1], stats[:, col + 1:col + 2]], axis=0)


def _attn_sub_blocks(length):
    for n in (4, 2):
        if length % (n * BLK) == 0:
            return n
    return 1


def _attn_fwd(qkv, group, carried=None):
    r, L, _ = qkv.shape
    nsub = _attn_sub_blocks(L)
    rows = nsub * BLK

    def body(q_ref, kp_ref, kc_ref, vp_ref, vc_ref, o_ref, lse_ref):
        head0, head1 = _head_lanes()
        lane = lax.broadcasted_iota(jnp.int32, (1, HEAD_PAIR), 1)
        for j in range(nsub):
            cur = slice(j * BLK, (j + 1) * BLK)
            before = slice((j - 1) * BLK, j * BLK)
            valid = _band_mask(pl.program_id(1) + j)
            stats = jnp.zeros((BLK, HEAD_PAIR), F32)
            for p in range(N_SLABS):
                sl = slice(p * HEAD_PAIR, (p + 1) * HEAD_PAIR)
                k_prev, v_prev = (kp_ref[:, sl], vp_ref[:, sl]) if j == 0 else (kc_ref[before, sl], vc_ref[before, sl])
                k2 = jnp.concatenate([k_prev, kc_ref[cur, sl]], axis=0)
                v2 = jnp.concatenate([v_prev, vc_ref[cur, sl]], axis=0)
                s = jnp.where(valid, _nt(_stack_heads(q_ref[cur, sl], head0, head1), k2), MASKED)
                m = jnp.max(s, axis=1, keepdims=True)
                e = jnp.exp(s - m)
                den = jnp.sum(e, axis=1, keepdims=True)
                o_ref[cur, sl] = _unstack_heads(_nn(e.astype(BF16), v2) / den, head0)
                lse = m + jnp.log(den)
                stats = jnp.where(lane == 2 * p, lse[:BLK], jnp.where(lane == 2 * p + 1, lse[BLK:], stats))
            lse_ref[cur, :] = stats

    def cur_blk(kind, width=ATTN_W):
        return pl.BlockSpec((None, rows, width), lambda rho, n: (rho, n, kind))

    def prev_blk(kind):
        return pl.BlockSpec((None, BLK, ATTN_W), lambda rho, n: (rho, jnp.maximum(n * nsub - 1, 0), kind))

    return _call(
        body, name=f"attn_fwd_g{group}", grid=(r, L // rows),
        out_shape=[_array((r, L, ATTN_W), F32), _array((r, L, HEAD_PAIR), F32)],
        in_specs=[cur_blk(0), prev_blk(1), cur_blk(1), prev_blk(2), cur_blk(2)],
        out_specs=[cur_blk(0), cur_blk(0, HEAD_PAIR)],
        vmem_mib=24, args=(qkv, qkv, qkv, qkv, qkv), carried=carried)


def _attn_combine(outs, lses, tm):
    T = outs[0].shape[1]

    def body(o0, l0, o1, l1, o2, l2, attn_ref, attn_b_ref, lse_ref, so1, sl1, so2, sl2):
        for o_in, l_in, so, sl, r in ((o1, l1, so1, sl1, DILATIONS[1]), (o2, l2, so2, sl2, DILATIONS[2])):
            _rows_by_token(so, o_in, r, tm // r)
            _rows_by_token(sl, l_in, r, tm // r, n=1)
        head0, _ = _head_lanes()
        a0, a1, a2 = l0[0], sl1[0], sl2[0]
        mx = jnp.maximum(jnp.maximum(a0, a1), a2)
        e0, e1, e2 = jnp.exp(a0 - mx), jnp.exp(a1 - mx), jnp.exp(a2 - mx)
        tot = e0 + e1 + e2
        lse_ref[...] = mx + jnp.log(tot)
        w0, w1, w2 = e0 / tot, e1 / tot, e2 / tot
        for s in range(N_SLABS):
            lanes = slice(s * HEAD_PAIR, (s + 1) * HEAD_PAIR)

            def lanes_of(w):
                return jnp.where(head0, w[:, 2 * s:2 * s + 1], w[:, 2 * s + 1:2 * s + 2])

            mixed = lanes_of(w0) * o0[0, :, lanes] + lanes_of(w1) * so1[s] + lanes_of(w2) * so2[s]
            attn_ref[:, lanes] = mixed
            attn_b_ref[:, lanes] = mixed.astype(BF16)

    ins, specs = [], []
    for g, r in enumerate(DILATIONS):
        ins += [outs[g], lses[g]]
        specs += [pl.BlockSpec((r, tm // r, ATTN_W), lambda i: (0, i, 0)),
                  pl.BlockSpec((r, tm // r, HEAD_PAIR), lambda i: (0, i, 0))]
    return _pallas(
        body, name="attn_combine", grid=(T // tm,),
        out_shape=[_array((T, ATTN_W), F32), _array((T, ATTN_W), BF16), _array((T, HEAD_PAIR), F32)], in_specs=specs,
        out_specs=[pl.BlockSpec((tm, ATTN_W), lambda i: (i, 0)), pl.BlockSpec((tm, ATTN_W), lambda i: (i, 0)),
                   pl.BlockSpec((tm, HEAD_PAIR), lambda i: (i, 0))],
        scratch_shapes=[_slab_scratch(tm), _slab_scratch(tm, 1), _slab_scratch(tm), _slab_scratch(tm, 1)],
        compiler_params=_params(("parallel",), 32),
    )(*ins)


def _gelu(x):
    return 0.5 * x * (1.0 + lax.erf(x * (1.0 / math.sqrt(2.0))))


def _gelu_grad(x):
    return 0.5 * (1.0 + lax.erf(x * (1.0 / math.sqrt(2.0)))) + x * jnp.exp(-0.5 * x * x) * (1.0 / math.sqrt(2.0 * math.pi))


def _causal():
    row = lax.broadcasted_iota(jnp.int32, (BLK, BLK), 0)
    col = lax.broadcasted_iota(jnp.int32, (BLK, BLK), 1)
    return col <= row


def _bias_lanes(bt):
    grp = lax.broadcasted_iota(jnp.int32, (1, SGU_W), 1) // 64
    out = jnp.zeros((BLK, SGU_W), F32)
    for g in range(8):
        out = jnp.where(grp == g, bt[:, g:g + 1], out)
    return out


def _sgu_normalise(uv, ln_g, ln_b):
    z = _gelu(uv)
    u, v = z[:, :SGU_W], z[:, SGU_W:]
    mu = jnp.mean(v, axis=-1, keepdims=True)
    xc = v - mu
    rstd = lax.rsqrt(jnp.mean(xc * xc, axis=-1, keepdims=True) + EPS)
    xh = xc * rstd
    return u, xh, rstd, xh * ln_g + ln_b


def _sgu_mix(wc_ref, vb, head0):
    chunks = []
    for ch in range(vb.shape[0] // BLK):
        pairs = []
        for p in range(SGU_W // HEAD_PAIR):
            v_pair = vb[ch * BLK:(ch + 1) * BLK, p * HEAD_PAIR:(p + 1) * HEAD_PAIR]
            pairs.append(jnp.where(head0, _nn(wc_ref[2 * p], v_pair), _nn(wc_ref[2 * p + 1], v_pair)))
        chunks.append(jnp.concatenate(pairs, axis=1))
    return jnp.concatenate(chunks, axis=0)


def _sgu_fwd(uvg, ln_g, ln_b, w_s, b_t, tm):
    T = uvg.shape[0]

    def body(uv_ref, g_ref, b_ref, w_ref, bt_ref, o_ref, wc_ref, bias_ref):
        @pl.when(pl.program_id(0) == 0)
        def _():
            causal = _causal()
            for g in range(8):
                wc_ref[g] = jnp.where(causal, w_ref[g], 0.0).astype(BF16)
            bias_ref[...] = _bias_lanes(bt_ref[...])

        u, _, _, vn = _sgu_normalise(uv_ref[...].astype(F32), g_ref[...], b_ref[...])
        mixed = _sgu_mix(wc_ref, vn.astype(BF16), _head_lanes()[0])
        o_ref[...] = (u * (mixed + jnp.tile(bias_ref[...], (tm // BLK, 1)))).astype(BF16)

    vec = pl.BlockSpec((1, SGU_W), lambda i: (0, 0))
    return _pallas(
        body, name="sgu_fwd", grid=(T // tm,), out_shape=_array((T, SGU_W), BF16),
        in_specs=[pl.BlockSpec((tm, 2 * SGU_W), lambda i: (i, 0)), vec, vec,
                  pl.BlockSpec((8, BLK, BLK), lambda i: (0, 0, 0)), pl.BlockSpec((BLK, 8), lambda i: (0, 0))],
        out_specs=pl.BlockSpec((tm, SGU_W), lambda i: (i, 0)),
        scratch_shapes=[pltpu.VMEM((8, BLK, BLK), BF16), pltpu.VMEM((BLK, SGU_W), F32)],
        compiler_params=_params(("arbitrary",), 32),
    )(uvg, ln_g, ln_b, w_s, b_t)


def _merge_fwd(attn, sgu, wpa_t, wps_t, uvg, w_out, x, g2, tm):
    T = attn.shape[0]

    def body(attn_ref, sgu_ref, wpa_ref, wps_ref, ga_ref, gb_ref, wo_ref, x_ref, g_ref,
             pa_ref, ps_ref, m_ref, h_ref, x1_ref):
        pa = _nt(attn_ref[...], wpa_ref[...])
        ps = _nt(sgu_ref[...], wps_ref[...])
        pa_ref[...] = pa.astype(BF16)
        ps_ref[...] = ps.astype(BF16)
        ga, gb = ga_ref[...].astype(F32), gb_ref[...].astype(F32)
        merged = (jax.nn.sigmoid(ga) * pa + jax.nn.sigmoid(gb) * ps).astype(BF16)
        m_ref[...] = merged
        x1 = x_ref[...] + _nn(merged, wo_ref[...])
        x1_ref[...] = x1
        r = lax.rsqrt(jnp.mean(x1 * x1, axis=-1, keepdims=True) + EPS)
        h_ref[...] = (x1 * r * g_ref[...]).astype(BF16)

    half = pl.BlockSpec((tm, ATTN_W), lambda i: (i, 0))
    wide = pl.BlockSpec((tm, D_MODEL), lambda i: (i, 0))
    w = _resident((D_MODEL, ATTN_W))
    res = _array((T, D_MODEL), BF16)
    return _pallas(
        body, name="merge_fwd", grid=(T // tm,), out_shape=[res, res, res, res, _array((T, D_MODEL), F32)],
        in_specs=[half, half, w, w, pl.BlockSpec((tm, D_MODEL), lambda i: (i, 1)),
                  pl.BlockSpec((tm, D_MODEL), lambda i: (i, 2)), _resident((D_MODEL, D_MODEL)), wide,
                  pl.BlockSpec((1, D_MODEL), lambda i: (0, 0))],
        out_specs=[wide] * 5, compiler_params=_params(("parallel",), 48),
    )(attn, sgu, wpa_t, wps_t, uvg, uvg, w_out, x, g2)


def _ffn_fwd(h2, wg_t, wu_t, tm):
    T = h2.shape[0]

    def body(h_ref, wg_ref, wu_ref, gate_ref, up_ref, ff_ref):
        h = h_ref[...]
        for j in range(D_FF // FF_TILE):
            cols = slice(j * FF_TILE, (j + 1) * FF_TILE)
            gate, up = _nt(h, wg_ref[cols, :]), _nt(h, wu_ref[cols, :])
            gate_ref[:, cols] = gate.astype(BF16)
            up_ref[:, cols] = up.astype(BF16)
            ff_ref[:, cols] = (gate * jax.nn.sigmoid(gate) * up).astype(BF16)

    w = _resident((D_FF, D_MODEL))
    o = pl.BlockSpec((tm, D_FF), lambda i: (i, 0))
    res = _array((T, D_FF), BF16)
    return _pallas(
        body, name="ffn_fwd", grid=(T // tm,), out_shape=[res, res, res],
        in_specs=[pl.BlockSpec((tm, D_MODEL), lambda i: (i, 0)), w, w], out_specs=[o, o, o],
        compiler_params=_params(("parallel",), 52),
    )(h2, wg_t, wu_t)


def _down_loss(ff, w_down, x1, final_g, target, tm):
    T = x1.shape[0]

    def body(ff_ref, w_ref, x1_ref, g_ref, t_ref, dx_ref, dxb_ref, loss_ref, dg_ref):
        @pl.when(pl.program_id(0) == 0)
        def _():
            loss_ref[...] = jnp.zeros_like(loss_ref)
            dg_ref[...] = jnp.zeros_like(dg_ref)

        x2 = x1_ref[...] + _nn(ff_ref[...], w_ref[...])
        g = g_ref[...]
        r = lax.rsqrt(jnp.mean(x2 * x2, axis=-1, keepdims=True) + EPS)
        xh = x2 * r
        err = xh * g - t_ref[...]
        loss_ref[...] += jnp.sum(err * err, axis=0, keepdims=True) * (0.5 / D_MODEL)
        dy = err * (1.0 / D_MODEL)
        dg_ref[...] += jnp.sum(dy * xh, axis=0, keepdims=True)
        dxh = dy * g
        dx = r * (dxh - xh * jnp.mean(dxh * xh, axis=-1, keepdims=True))
        dx_ref[...] = dx
        dxb_ref[...] = dx.astype(BF16)

    wide = pl.BlockSpec((tm, D_MODEL), lambda i: (i, 0))
    vec = pl.BlockSpec((1, D_MODEL), lambda i: (0, 0))
    vec_shape = _array((1, D_MODEL), F32)
    return _pallas(
        body, name="down_loss", grid=(T // tm,),
        out_shape=[_array((T, D_MODEL), F32), _array((T, D_MODEL), BF16),
                   vec_shape, vec_shape],
        in_specs=[pl.BlockSpec((tm, D_FF), lambda i: (i, 0)), _resident((D_FF, D_MODEL)), wide, vec, wide],
        out_specs=[wide, wide, vec, vec],
        compiler_params=_params(("arbitrary",), 40),
    )(ff, w_down, x1, final_g, target)


def _rmsnorm_bwd(dh, xin, g, d_res, dg_ref):
    r = lax.rsqrt(jnp.mean(xin * xin, axis=-1, keepdims=True) + EPS)
    xh = xin * r
    dg_ref[...] += jnp.sum(dh * xh, axis=0, keepdims=True)
    dxh = dh * g
    return d_res + r * (dxh - xh * jnp.mean(dxh * xh, axis=-1, keepdims=True))


def _ffn_bwd_half(half, dx2b, w_down, wg_t, wu_t, gate, up, tm, so_far=None, tail=None):
    T = dx2b.shape[0]
    last = so_far is not None

    def body(dxb_ref, wd_ref, wg_ref, wu_ref, gate_ref, up_ref, *rest):
        dff = _nt(dxb_ref[...], wd_ref[...])
        gate, up = gate_ref[...].astype(F32), up_ref[...].astype(F32)
        sg = jax.nn.sigmoid(gate)
        dgate = (dff * up * sg * (1.0 + gate * (1.0 - sg))).astype(BF16)
        dup = (dff * gate * sg).astype(BF16)
        dh = _nn(dgate, wg_ref[...]) + _nn(dup, wu_ref[...])
        if not last:
            dgate_ref, dup_ref, dh_ref = rest
            dh_ref[...] = dh
        else:
            _, _, dh0_ref, x1_ref, g_ref, dx_ref, dgate_ref, dup_ref, dx1_ref, dx1b_ref, dg_ref = rest

            @pl.when(pl.program_id(0) == 0)
            def _():
                dg_ref[...] = jnp.zeros_like(dg_ref)

            dx1 = _rmsnorm_bwd(dh + dh0_ref[...], x1_ref[...], g_ref[...], dx_ref[...], dg_ref)
            dx1_ref[...] = dx1
            dx1b_ref[...] = dx1.astype(BF16)
        dgate_ref[...] = dgate
        dup_ref[...] = dup

    wide = pl.BlockSpec((tm, D_MODEL), lambda i: (i, 0))
    cols = pl.BlockSpec((tm, FF_TILE), lambda i: (i, half))
    vec = pl.BlockSpec((1, D_MODEL), lambda i: (0, 0))
    w = _resident((FF_TILE, D_MODEL), (half, 0))
    ff_shape = _array((T, D_FF), BF16)
    in_specs = [wide, w, w, w, cols, cols]
    args = [dx2b, w_down, wg_t, wu_t, gate, up]
    if not last:
        return _pallas(
            body, name="ffn_bwd_first", grid=(T // tm,), out_shape=[ff_shape, ff_shape, _array((T, D_MODEL), F32)],
            in_specs=in_specs, out_specs=[cols, cols, wide], compiler_params=_params(("parallel",), 48),
        )(*args)
    return _pallas(
        body, name="ffn_bwd_second", grid=(T // tm,),
        out_shape=[ff_shape, ff_shape, _array((T, D_MODEL), F32), _array((T, D_MODEL), BF16),
                   _array((1, D_MODEL), F32)],
        in_specs=in_specs + [ANY, ANY, wide, wide, vec, wide], out_specs=[cols, cols, wide, wide, vec],
        input_output_aliases={len(in_specs): 0, len(in_specs) + 1: 1},
        compiler_params=_params(("arbitrary",), 58),
    )(*args, *so_far, *tail)


def _mm_tn(a, b, *, tmm, tk, name, into=None, row_block0=0, rows_total=None):
    T, M = a.shape
    N = b.shape[1]
    rows_total = M if rows_total is None else rows_total

    def body(*refs):
        a_ref, b_ref, o_ref = refs[0], refs[1], refs[-1]

        @pl.when(pl.program_id(1) == 0)
        def _():
            o_ref[...] = jnp.zeros_like(o_ref)

        o_ref[...] += _tn(a_ref[...], b_ref[...])

    ins = [a, b] + ([] if into is None else [into])
    specs = [pl.BlockSpec((tk, tmm), lambda i, k: (k, i)), pl.BlockSpec((tk, N), lambda i, k: (k, 0))]
    return _pallas(
        body, name=name, grid=(M // tmm, T // tk),
        out_shape=_array((rows_total, N), F32),
        in_specs=specs + ([] if into is None else [ANY]),
        out_specs=pl.BlockSpec((tmm, N), lambda i, k: (row_block0 + i, 0)),
        input_output_aliases={} if into is None else {2: 0},
        compiler_params=_params(("parallel", "arbitrary"), 48),
    )(*ins)


def _in_bwd(dqkv, duv, dgates, win_t, x, g1, dx1, tm, carried=None):
    T = x.shape[0]
    uv0, gates0 = QKV_COLS, QKV_COLS + 2 * SGU_W

    def body(dq_ref, du_ref, dgt_ref, w_ref, x_ref, g_ref, d_ref, dx_ref, dg_ref):
        @pl.when(pl.program_id(0) == 0)
        def _():
            dg_ref[...] = jnp.zeros_like(dg_ref)

        dh = (_nn(dq_ref[...], w_ref[:uv0, :]) + _nn(du_ref[...], w_ref[uv0:gates0, :])
              + _nn(dgt_ref[...], w_ref[gates0:, :]))
        dx_ref[...] = _rmsnorm_bwd(dh, x_ref[...], g_ref[...], d_ref[...], dg_ref)

    def cols(n):
        return pl.BlockSpec((tm, n), lambda i: (i, 0))

    wide = cols(D_MODEL)
    vec = pl.BlockSpec((1, D_MODEL), lambda i: (0, 0))
    return _call(
        body, name="in_bwd", grid=(T // tm,),
        out_shape=[_array((T, D_MODEL), F32), _array((1, D_MODEL), F32)],
        in_specs=[cols(QKV_COLS), cols(2 * SGU_W), cols(2 * D_MODEL), _resident((IN_COLS, D_MODEL)), wide, vec, wide],
        out_specs=[wide, vec], vmem_mib=56, args=(dqkv, duv, dgates, win_t, x, g1, dx1), carried=carried)


def _merge_bwd(dx1b, w_out, pa, ps, uvg, wpa_t, wps_t, tm, carried=None):
    T = dx1b.shape[0]

    def body(dx_ref, w_ref, pa_ref, ps_ref, ga_ref, gb_ref, wpa_ref, wps_ref,
             dpa_ref, dps_ref, dg_ref, dattn_ref, dsgu_ref):
        dm = _nt(dx_ref[...], w_ref[...])
        ga, gb = jax.nn.sigmoid(ga_ref[...].astype(F32)), jax.nn.sigmoid(gb_ref[...].astype(F32))
        dpa, dps = (dm * ga).astype(BF16), (dm * gb).astype(BF16)
        dpa_ref[...] = dpa
        dps_ref[...] = dps
        dg_ref[:, :D_MODEL] = (dm * pa_ref[...].astype(F32) * ga * (1.0 - ga)).astype(BF16)
        dg_ref[:, D_MODEL:] = (dm * ps_ref[...].astype(F32) * gb * (1.0 - gb)).astype(BF16)
        dattn_ref[...] = _nn(dpa, wpa_ref[...])
        dsgu_ref[...] = _nn(dps, wps_ref[...])

    wide = pl.BlockSpec((tm, D_MODEL), lambda i: (i, 0))
    half = pl.BlockSpec((tm, ATTN_W), lambda i: (i, 0))
    w = _resident((D_MODEL, ATTN_W))
    res = _array((T, D_MODEL), BF16)
    res_half = _array((T, ATTN_W), F32)
    return _call(
        body, name="merge_bwd", grid=(T // tm,),
        out_shape=[res, res, _array((T, 2 * D_MODEL), BF16), res_half, res_half],
        in_specs=[wide, _resident((D_MODEL, D_MODEL)), wide, wide,
                  pl.BlockSpec((tm, D_MODEL), lambda i: (i, 1)), pl.BlockSpec((tm, D_MODEL), lambda i: (i, 2)), w, w],
        out_specs=[wide, wide, pl.BlockSpec((tm, 2 * D_MODEL), lambda i: (i, 0)), half, half],
        vmem_mib=48, args=(dx1b, w_out, pa, ps, uvg, uvg, wpa_t, wps_t), carried=carried)


def _sgu_bwd(uvg, dsgu, ln_g, ln_b, w_s, b_t, tm, carried=None):
    T = uvg.shape[0]
    nsteps = T // tm

    def body(uv_ref, ds_ref, g_ref, b_ref, w_ref, bt_ref, duv_ref, dw_ref, dbt_ref, dg_ref, db_ref,
             wc_ref, wct_ref, bias_ref, dbias_ref):
        step = pl.program_id(0)
        head0, head1 = _head_lanes()

        @pl.when(step == 0)
        def _():
            causal = _causal()
            for g in range(8):
                wc = jnp.where(causal, w_ref[g], 0.0)
                wc_ref[g] = wc.astype(BF16)
                wct_ref[g] = wc.T.astype(BF16)
            bias_ref[...] = _bias_lanes(bt_ref[...])
            dbias_ref[...] = jnp.zeros_like(dbias_ref)
            dw_ref[...] = jnp.zeros_like(dw_ref)
            dg_ref[...] = jnp.zeros_like(dg_ref)
            db_ref[...] = jnp.zeros_like(db_ref)

        uv = uv_ref[...].astype(F32)
        ln_gain = g_ref[...]
        u, xh, rstd, vn = _sgu_normalise(uv, ln_gain, b_ref[...])
        vb = vn.astype(BF16)
        mixed = _sgu_mix(wc_ref, vb, head0) + jnp.tile(bias_ref[...], (tm // BLK, 1))
        dout = ds_ref[...]
        du = dout * mixed
        dmixed = dout * u
        dmb = dmixed.astype(BF16)
        dvn_chunks = []
        for ch in range(tm // BLK):
            rows = slice(ch * BLK, (ch + 1) * BLK)
            dbias_ref[...] += dmixed[rows]
            pairs = []
            for p in range(SGU_W // HEAD_PAIR):
                lanes = slice(p * HEAD_PAIR, (p + 1) * HEAD_PAIR)
                dm_pair, v_pair = dmb[rows, lanes], vb[rows, lanes]
                acc = jnp.zeros((BLK, HEAD_PAIR), F32)
                for hh, half in enumerate((head0, head1)):
                    dm_h = jnp.where(half, dm_pair, jnp.zeros_like(dm_pair))
                    dw_ref[2 * p + hh] += _nt(dm_h, v_pair)
                    acc += _nn(wct_ref[2 * p + hh], dm_h)
                pairs.append(acc)
            dvn_chunks.append(jnp.concatenate(pairs, axis=1))
        dvn = jnp.concatenate(dvn_chunks, axis=0)
        dg_ref[...] += jnp.sum(dvn * xh, axis=0, keepdims=True)
        db_ref[...] += jnp.sum(dvn, axis=0, keepdims=True)
        dxh = dvn * ln_gain
        dv = rstd * (dxh - jnp.mean(dxh, axis=-1, keepdims=True) - xh * jnp.mean(dxh * xh, axis=-1, keepdims=True))
        dgelu = _gelu_grad(uv)
        duv_ref[:, :SGU_W] = (du * dgelu[:, :SGU_W]).astype(BF16)
        duv_ref[:, SGU_W:] = (dv * dgelu[:, SGU_W:]).astype(BF16)

        @pl.when(step == nsteps - 1)
        def _():
            causal = _causal()
            for g in range(8):
                dw_ref[g] = jnp.where(causal, dw_ref[g], 0.0)
            grp = lax.broadcasted_iota(jnp.int32, (1, SGU_W), 1) // 64
            col = lax.broadcasted_iota(jnp.int32, (1, 8), 1)
            dbias = dbias_ref[...]
            out = jnp.zeros((BLK, 8), F32)
            for g in range(8):
                s = jnp.sum(jnp.where(grp == g, dbias, 0.0), axis=1, keepdims=True)
                out = jnp.where(col == g, s, out)
            dbt_ref[...] = out

    vec = pl.BlockSpec((1, SGU_W), lambda i: (0, 0))
    w3 = pl.BlockSpec((8, BLK, BLK), lambda i: (0, 0, 0))
    bt = pl.BlockSpec((BLK, 8), lambda i: (0, 0))
    return _call(
        body, name="sgu_bwd", grid=(nsteps,),
        out_shape=[_array((T, 2 * SGU_W), BF16), _array((8, BLK, BLK), F32),
                   _array((BLK, 8), F32), _array((1, SGU_W), F32),
                   _array((1, SGU_W), F32)],
        in_specs=[pl.BlockSpec((tm, 2 * SGU_W), lambda i: (i, 0)), pl.BlockSpec((tm, SGU_W), lambda i: (i, 0)),
                  vec, vec, w3, bt],
        out_specs=[pl.BlockSpec((tm, 2 * SGU_W), lambda i: (i, 0)), w3, bt, vec, vec],
        scratch_shapes=[pltpu.VMEM((8, BLK, BLK), BF16), pltpu.VMEM((8, BLK, BLK), BF16),
                        pltpu.VMEM((BLK, SGU_W), F32), pltpu.VMEM((BLK, SGU_W), F32)],
        vmem_mib=40, args=(uvg, dsgu, ln_g, ln_b, w_s, b_t), carried=carried)


D_LANE0 = 8


def _attn_bwd_prepare(dattn, attn, lse, tm):
    T = dattn.shape[0]

    def body(da_ref, at_ref, lse_ref, *outs):
        scr_da, scr_st = outs[-2:]
        outs = outs[:-2]
        head0, _ = _head_lanes()
        lane = lax.broadcasted_iota(jnp.int32, (1, HEAD_PAIR), 1)
        stats = lse_ref[...]
        for s in range(N_SLABS):
            lanes = slice(s * HEAD_PAIR, (s + 1) * HEAD_PAIR)
            da = da_ref[:, lanes]
            pp = da * at_ref[:, lanes]
            d0 = jnp.sum(jnp.where(head0, pp, 0.0), axis=1, keepdims=True)
            d1 = jnp.sum(jnp.where(head0, 0.0, pp), axis=1, keepdims=True)
            stats = jnp.where(lane == D_LANE0 + 2 * s, d0, jnp.where(lane == D_LANE0 + 2 * s + 1, d1, stats))
            scr_da[s] = da
            outs[0][0, :, lanes] = da.astype(BF16)
        scr_st[0] = stats
        outs[1][0] = stats
        for g, r in enumerate(DILATIONS):
            if r > 1:
                _rows_by_residue(outs[2 * g], scr_da, r, tm // r, BF16)
                _rows_by_residue(outs[2 * g + 1], scr_st, r, tm // r, F32, n=1)

    tok = pl.BlockSpec((tm, ATTN_W), lambda i: (i, 0))
    shapes, specs = [], []
    for r in DILATIONS:
        shapes += [_array((r, T // r, ATTN_W), BF16), _array((r, T // r, HEAD_PAIR), F32)]
        specs += [pl.BlockSpec((r, tm // r, ATTN_W), lambda i: (0, i, 0)),
                  pl.BlockSpec((r, tm // r, HEAD_PAIR), lambda i: (0, i, 0))]
    return _pallas(
        body, name="attn_bwd_prepare", grid=(T // tm,), out_shape=shapes,
        in_specs=[tok, tok, pl.BlockSpec((tm, HEAD_PAIR), lambda i: (i, 0))],
        out_specs=specs, scratch_shapes=[_slab_scratch(tm), _slab_scratch(tm, 1)],
        compiler_params=_params(("parallel",), 40),
    )(dattn, attn, lse)


def _attn_bwd(qkv, dattn, stats, group, carried=None):
    r, L, _ = qkv.shape
    nsub = _attn_sub_blocks(L)
    rows = nsub * BLK
    nb = L // rows
    keep = rows - BLK
    whole = nb == 1

    def body(q_ref, kp_ref, kc_ref, vp_ref, vc_ref, da_ref, st_ref, dq_ref, dk_ref, dv_ref, carry_k, carry_v):
        n = pl.program_id(1)

        if not whole:
            @pl.when(n == 0)
            def _():
                carry_k[...] = jnp.zeros_like(carry_k)
                carry_v[...] = jnp.zeros_like(carry_v)

        @pl.when(n < nb)
        def _():
            head0, head1 = _head_lanes()
            for p in range(N_SLABS):
                sl = slice(p * HEAD_PAIR, (p + 1) * HEAD_PAIR)
                dk_parts = [jnp.zeros((BLK, HEAD_PAIR), F32) for _ in range(nsub + 1)]
                dv_parts = [jnp.zeros((BLK, HEAD_PAIR), F32) for _ in range(nsub + 1)]
                for j in range(nsub):
                    cur = slice(j * BLK, (j + 1) * BLK)
                    before = slice((j - 1) * BLK, j * BLK)
                    valid = _band_mask(n + j)
                    st = st_ref[cur, :]
                    k_prev, v_prev = (kp_ref[:, sl], vp_ref[:, sl]) if j == 0 else (kc_ref[before, sl], vc_ref[before, sl])
                    k2 = jnp.concatenate([k_prev, kc_ref[cur, sl]], axis=0)
                    v2 = jnp.concatenate([v_prev, vc_ref[cur, sl]], axis=0)
                    qs = _stack_heads(q_ref[cur, sl], head0, head1)
                    das = _stack_heads(da_ref[cur, sl], head0, head1)
                    prob = jnp.where(valid, jnp.exp(_nt(qs, k2) - _per_head(st, 2 * p)), 0.0)
                    ds = (prob * (_nt(das, v2) - _per_head(st, D_LANE0 + 2 * p))).astype(BF16)
                    dk2 = _tn(ds, qs)
                    dv2 = _tn(prob.astype(BF16), das)
                    dq_ref[cur, sl] = _unstack_heads(_nn(ds, k2), head0).astype(BF16)
                    dk_parts[j] += dk2[:BLK]
                    dk_parts[j + 1] += dk2[BLK:]
                    dv_parts[j] += dv2[:BLK]
                    dv_parts[j + 1] += dv2[BLK:]
                if whole:
                    for j in range(nsub):
                        dk_ref[j * BLK:(j + 1) * BLK, sl] = dk_parts[j + 1].astype(BF16)
                        dv_ref[j * BLK:(j + 1) * BLK, sl] = dv_parts[j + 1].astype(BF16)
                    continue
                if keep:
                    dk_ref[:keep, sl] = carry_k[:keep, sl].astype(BF16)
                    dv_ref[:keep, sl] = carry_v[:keep, sl].astype(BF16)
                dk_ref[keep:, sl] = (carry_k[keep:, sl] + dk_parts[0]).astype(BF16)
                dv_ref[keep:, sl] = (carry_v[keep:, sl] + dv_parts[0]).astype(BF16)
                for j in range(nsub):
                    carry_k[j * BLK:(j + 1) * BLK, sl] = dk_parts[j + 1]
                    carry_v[j * BLK:(j + 1) * BLK, sl] = dv_parts[j + 1]

        if not whole:
            @pl.when(n == nb)
            def _():
                dk_ref[...] = carry_k[...].astype(BF16)
                dv_ref[...] = carry_v[...].astype(BF16)

    def cur_blk(kind, width=ATTN_W):
        return pl.BlockSpec((None, rows, width), lambda rho, n: (rho, jnp.minimum(n, nb - 1), kind))

    def last_blk(kind):
        return pl.BlockSpec((None, rows, ATTN_W), lambda rho, n: (rho, jnp.clip(n - 1, 0, nb - 1), kind))

    def prev_keys(kind):
        return pl.BlockSpec((None, BLK, ATTN_W),
                            lambda rho, n: (rho, jnp.clip(n * nsub - 1, 0, nb * nsub - 1), kind))

    res = _array((r, L, ATTN_W), BF16)
    return _call(
        body, name=f"attn_bwd_g{group}", grid=(r, 1 if whole else nb + 1),
        out_shape=[res, res, res],
        in_specs=[cur_blk(0), prev_keys(1), cur_blk(1), prev_keys(2), cur_blk(2), cur_blk(0), cur_blk(0, HEAD_PAIR)],
        out_specs=[cur_blk(0), last_blk(0), last_blk(0)],
        scratch_shapes=[pltpu.VMEM((rows, ATTN_W), F32), pltpu.VMEM((rows, ATTN_W), F32)],
        vmem_mib=32, args=(qkv, qkv, qkv, qkv, qkv, dattn, stats), carried=carried)


def _dqkv_token_order(dqkv_groups, tables, tm):
    T = tables[0].shape[0]

    def body(*refs):
        ins = refs[:9]
        cos_ref, s1_ref, s2_ref, o_ref, scr = refs[9:]
        cos, s1, s2 = cos_ref[...], s1_ref[...], s2_ref[...]
        for g, r in enumerate(DILATIONS):
            for kind in range(3):
                src = ins[3 * g + kind]
                if r > 1:
                    _rows_by_token(scr, src, r, tm // r)
                for s in range(N_SLABS):
                    val = scr[s] if r > 1 else src[0, :, s * HEAD_PAIR:(s + 1) * HEAD_PAIR].astype(F32)
                    if kind < 2:
                        val = _rope_transposed(val, cos, s1, s2)
                    if kind == 0:
                        val = val * 0.125
                    at = (3 * kind + g) * ATTN_W + s * HEAD_PAIR
                    o_ref[:, at:at + HEAD_PAIR] = val.astype(BF16)

    specs = []
    for r in DILATIONS:
        specs += [pl.BlockSpec((r, tm // r, ATTN_W), lambda i: (0, i, 0))] * 3
    row = pl.BlockSpec((tm, HEAD_PAIR), lambda i: (i, 0))
    flat = [a for grp in dqkv_groups for a in grp]
    return _pallas(
        body, name="dqkv_token_order", grid=(T // tm,), out_shape=_array((T, QKV_COLS), BF16),
        in_specs=specs + [row] * 3, out_specs=pl.BlockSpec((tm, QKV_COLS), lambda i: (i, 0)),
        scratch_shapes=[_slab_scratch(tm)], compiler_params=_params(("parallel",), 48),
    )(*flat, *tables)


def _row_tile(rows):
    for cand in (320, 256, 176, 128):
        if rows % cand == 0:
            return cand
    return rows


def _pair_sum(grad4, recv, chip, name):
    _, _, rows, cols = grad4.shape
    tr = _row_tile(rows)

    def body(ids_ref, g_ref, r_ref, gown_ref, rown_ref, sum_ref, own_ref):
        sum_ref[...] = (g_ref[...] + r_ref[...]).astype(BF16)

        @pl.when(pl.program_id(1) == 0)
        def _():
            own_ref[...] = gown_ref[...] + rown_ref[...]

    grid_spec = pltpu.PrefetchScalarGridSpec(
        num_scalar_prefetch=1, grid=(rows // tr, 4),
        in_specs=[pl.BlockSpec((None, None, tr, cols), lambda i, q, ids: (q, ids[1], i, 0)),
                  pl.BlockSpec((None, tr, cols), lambda i, q, ids: (q, i, 0)),
                  pl.BlockSpec((None, None, tr, cols), lambda i, q, ids: (ids[0], ids[1], i, 0)),
                  pl.BlockSpec((None, tr, cols), lambda i, q, ids: (ids[0], i, 0))],
        out_specs=[pl.BlockSpec((None, tr, cols), lambda i, q, ids: (q, i, 0)),
                   pl.BlockSpec((tr, cols), lambda i, q, ids: (i, 0))])
    return _pallas(
        body, name=name, grid_spec=grid_spec,
        out_shape=[_array((4, rows, cols), BF16), _array((rows, cols), F32)],
        compiler_params=_params(("arbitrary", "arbitrary"), 32),
    )(chip, grad4, recv, grad4, recv)


def _chip_sum(own, others, name):
    rows, cols = own.shape
    tr = _row_tile(rows)

    def body(own_ref, oth_ref, o_ref):
        total = own_ref[...]
        for j in range(3):
            total = total + oth_ref[j].astype(F32)
        o_ref[...] = total

    blk = pl.BlockSpec((tr, cols), lambda i: (i, 0))
    return _pallas(
        body, name=name, grid=(rows // tr,), out_shape=_array((rows, cols), F32),
        in_specs=[blk, pl.BlockSpec((3, tr, cols), lambda i: (0, i, 0))], out_specs=blk,
        compiler_params=_params(("parallel",), 32),
    )(own, others)


def _adam_math(w, g, m, v):
    m = ADAM_B1 * m + (1.0 - ADAM_B1) * g
    v = ADAM_B2 * v + (1.0 - ADAM_B2) * (g * g)
    m_hat = m / (1.0 - ADAM_B1 ** ADAM_STEP)
    v_hat = v / (1.0 - ADAM_B2 ** ADAM_STEP)
    delta = -ADAM_LR * (m_hat / (jnp.sqrt(v_hat) + ADAM_EPS) + ADAM_WD * w)
    return delta, m, v


def _adamw(w, g, m, v, name):
    rows, cols = w.shape
    tr = _row_tile(rows)

    def body(w_ref, g_ref, m_ref, v_ref, d_ref, nm_ref, nv_ref):
        d_ref[...], nm_ref[...], nv_ref[...] = _adam_math(w_ref[...], g_ref[...], m_ref[...], v_ref[...])

    blk = pl.BlockSpec((tr, cols), lambda i: (i, 0))
    res = _array((rows, cols), F32)
    return _pallas(
        body, name=name, grid=(rows // tr,), out_shape=[res, res, res], in_specs=[blk] * 4, out_specs=[blk] * 3,
        compiler_params=_params(("parallel",), 32),
    )(w, g, m, v)


def _small_update(late_parts, early_parts, w, m, v, loss_rows):
    rows = w.shape[0]

    def total(p_ref):
        n = p_ref.shape[0] // N_DEV
        acc = p_ref[0:n, :]
        for dev in range(1, N_DEV):
            acc = acc + p_ref[dev * n:(dev + 1) * n, :]
        return acc

    def body(late_ref, early_ref, w_ref, m_ref, v_ref, g_ref, d_ref, nm_ref, nv_ref, loss_ref):
        g = jnp.concatenate([total(late_ref), total(early_ref)], axis=0)
        g_ref[...] = g
        d_ref[...], nm_ref[...], nv_ref[...] = _adam_math(w_ref[...], g, m_ref[...], v_ref[...])
        loss_ref[...] = jnp.sum(jnp.sum(g[rows - loss_rows:, :], axis=1, keepdims=True), axis=0, keepdims=True)

    res = jax.ShapeDtypeStruct((rows, HEAD_PAIR), F32)
    return _pallas(
        body, name="small_update", out_shape=[res, res, res, res, jax.ShapeDtypeStruct((1, 1), F32)],
        compiler_params=pltpu.CompilerParams(vmem_limit_bytes=32 * MIB),
    )(late_parts, early_parts, w, m, v)


def kernel(x, positions, norm1_g, w_in, sgu_ln_g, sgu_ln_b, w_spatial, b_spatial, w_proj_attn, w_proj_sgu, w_out, norm2_g, w_ffn_gate, w_ffn_up, w_ffn_down, final_g, loss_target, m_norm1_g, m_w_in, m_sgu_ln_g, m_sgu_ln_b, m_w_spatial, m_b_spatial, m_w_proj_attn, m_w_proj_sgu, m_w_out, m_norm2_g, m_w_ffn_gate, m_w_ffn_up, m_w_ffn_down, m_final_g, v_norm1_g, v_w_in, v_sgu_ln_g, v_sgu_ln_b, v_w_spatial, v_b_spatial, v_w_proj_attn, v_w_proj_sgu, v_w_out, v_norm2_g, v_w_ffn_gate, v_w_ffn_up, v_w_ffn_down, v_final_g):
    T = x.shape[1]
    tm = 512
    xt = x[0]
    target = loss_target[0]
    chip = jnp.stack([2 * lax.axis_index("x") + lax.axis_index("y"), lax.axis_index("c")]).astype(jnp.int32)

    def bf16_rows(w, transpose):
        return (jnp.transpose(w[0]) if transpose else w[0]).astype(BF16)

    w_in_gather = _all_gather([bf16_rows(w_in, True)])
    ffn_in = _all_gather([bf16_rows(w_ffn_gate, True), bf16_rows(w_ffn_up, True)])
    mixers = _all_gather([bf16_rows(w_out, False), bf16_rows(w_proj_attn, True), bf16_rows(w_proj_sgu, True)])
    ffn_out = _all_gather([bf16_rows(w_ffn_down, False)])

    inv_freq = ROPE_THETA ** (-jnp.arange(0, 2 * ROPE_HALF, 2, dtype=F32) / (2 * ROPE_HALF))
    inv_freq_row = jnp.tile(jnp.concatenate([inv_freq, inv_freq, jnp.zeros((48,), F32)]), 2).reshape(1, HEAD_PAIR)
    tables = _rope_tables(positions.reshape(T, 1), inv_freq_row, tm)
    b_t = jnp.transpose(b_spatial[0])

    (h,), (win_t,) = _rmsnorm_fwd(xt, norm1_g, "norm1_fwd", tm, carried=w_in_gather)
    qkv, (wg_t, wu_t) = _qkv_proj(h, win_t, tables, tm, carried=ffn_in)
    (uvg,), (wout, wpa_t, wps_t) = _uv_gate_proj(h, win_t, tm, carried=mixers)
    fwd0, (wd,) = _attn_fwd(qkv[0], 0, carried=ffn_out)
    fwd = [fwd0, _attn_fwd(qkv[1], 1)[0], _attn_fwd(qkv[2], 2)[0]]
    attn, attn_b, lse = _attn_combine([f[0] for f in fwd], [f[1] for f in fwd], tm)
    sgu = _sgu_fwd(uvg, sgu_ln_g, sgu_ln_b, w_spatial[0], b_t, tm)
    pa, ps, merged, h2, x1 = _merge_fwd(attn_b, sgu, wpa_t, wps_t, uvg, wout, xt, norm2_g, tm)
    gate, up, ff = _ffn_fwd(h2, wg_t, wu_t, tm)
    dx2, dx2b, loss_cols, d_final_g = _down_loss(ff, wd, x1, final_g.reshape(1, D_MODEL), target, tm)

    first = _ffn_bwd_half(0, dx2b, wd, wg_t, wu_t, gate, up, tm)
    dgate, dup, dx1, dx1b, d_norm2 = _ffn_bwd_half(1, dx2b, wd, wg_t, wu_t, gate, up, tm, so_far=first,
                                                   tail=(x1, norm2_g, dx2))
    tk = min(2048, T)
    d_wd = _mm_tn(ff, dx2b, tmm=FF_TILE, tk=tk, name="grad_w_ffn_down")
    d_wg_t = _mm_tn(dgate, h2, tmm=FF_TILE, tk=tk, name="grad_w_ffn_gate")
    d_wu_t = _mm_tn(dup, h2, tmm=FF_TILE, tk=tk, name="grad_w_ffn_up")

    def by_owner(grads):
        return [g.reshape(4, 2, g.shape[0] // N_DEV, g.shape[1]) for g in grads]

    def pair_sums(grads4, from_sibling, names):
        both = [_pair_sum(g4, rv, chip, "grad_pair_sum_" + nm) for g4, rv, nm in zip(grads4, from_sibling, names)]
        return [b[0] for b in both], [b[1] for b in both]

    ffn_names = ["w_ffn_gate", "w_ffn_up", "w_ffn_down"]
    ffn4 = by_owner([d_wg_t, d_wu_t, d_wd])
    (dpa, dps, dgates, dattn, dsgu), ffn_sib = _merge_bwd(dx1b, wout, pa, ps, uvg, wpa_t, wps_t, tm,
                                                          carried=_pair_exchange(ffn4))
    ffn_sums, ffn_own = pair_sums(ffn4, ffn_sib, ffn_names)

    d_wout = _mm_tn(merged, dx1b, tmm=D_MODEL, tk=tk, name="grad_w_out")
    d_wpa_t = _mm_tn(dpa, attn_b, tmm=D_MODEL, tk=tk, name="grad_w_proj_attn")
    d_wps_t = _mm_tn(dps, sgu, tmm=D_MODEL, tk=tk, name="grad_w_proj_sgu")
    mid_names = ["w_proj_attn", "w_proj_sgu", "w_out"]
    mid4 = by_owner([d_wpa_t, d_wps_t, d_wout])
    (duv, d_ws, d_bs_t, d_ln_g, d_ln_b), mid_sib = _sgu_bwd(uvg, dsgu, sgu_ln_g, sgu_ln_b, w_spatial[0], b_t, tm,
                                                           carried=_pair_exchange(mid4))
    mid_sums, mid_own = pair_sums(mid4, mid_sib, mid_names)

    prep = _attn_bwd_prepare(dattn, attn, lse, tm)
    dqkv0, ffn_far = _attn_bwd(qkv[0], prep[0], prep[1], 0, carried=_chip_exchange(ffn_sums))
    dqkv1, mid_far = _attn_bwd(qkv[1], prep[2], prep[3], 1, carried=_chip_exchange(mid_sums))
    def flat(parts):
        return jnp.concatenate([p.reshape(-1) for p in parts]).reshape(-1, HEAD_PAIR)

    early_part = flat([d_ln_g, d_ln_b, d_ws, jnp.transpose(d_bs_t), d_norm2, d_final_g, loss_cols])
    dqkv2, (early_parts,) = _attn_bwd(qkv[2], prep[4], prep[5], 2, carried=_all_gather([early_part]))
    dqkv = _dqkv_token_order([dqkv0, dqkv1, dqkv2], tables, tm)
    d_win_t = _mm_tn(dqkv, h, tmm=1536, tk=tk, name="grad_w_in_qkv", rows_total=IN_COLS)
    d_win_t = _mm_tn(duv, h, tmm=512, tk=tk, name="grad_w_in_uv", into=d_win_t, row_block0=9, rows_total=IN_COLS)
    d_win_t = _mm_tn(dgates, h, tmm=512, tk=tk, name="grad_w_in_gates", into=d_win_t, row_block0=11, rows_total=IN_COLS)
    in4 = by_owner([d_win_t])
    in_sib = _pair_exchange(in4).run_alone("w_in_grad_pair_exchange")
    in_sums, in_own = pair_sums(in4, in_sib, ["w_in"])
    (dx, d_norm1), in_far = _in_bwd(dqkv, duv, dgates, win_t, xt, norm1_g, dx1, tm, carried=_chip_exchange(in_sums))

    names = ["w_in"] + mid_names + ffn_names
    reduced = [_chip_sum(o, f, "grad_total_" + nm)
               for o, f, nm in zip(in_own + mid_own + ffn_own, in_far + mid_far + ffn_far, names)]
    transposed = (True, True, True, False, True, True, False)
    g_big = [jnp.transpose(r) if t else r for r, t in zip(reduced, transposed)]

    small_w = [norm1_g, sgu_ln_g, sgu_ln_b, w_spatial, b_spatial, norm2_g, final_g]
    small_m = [m_norm1_g, m_sgu_ln_g, m_sgu_ln_b, m_w_spatial, m_b_spatial, m_norm2_g, m_final_g]
    small_v = [v_norm1_g, v_sgu_ln_g, v_sgu_ln_b, v_w_spatial, v_b_spatial, v_norm2_g, v_final_g]
    zeros = jnp.zeros((D_MODEL,), F32)
    (late_parts,) = _all_gather([flat([d_norm1])]).run_alone("norm1_grad_all_gather", vmem=True)
    g_s, d_s, nm_s, nv_s, loss = _small_update(late_parts, early_parts, flat(small_w + [zeros]), flat(small_m + [zeros]),
                                               flat(small_v + [zeros]), D_MODEL // HEAD_PAIR)

    def unflat(vec):
        vec = vec.reshape(-1)
        out, at = [], 0
        for wgt in small_w:
            out.append(vec[at:at + wgt.size].reshape(wgt.shape))
            at += wgt.size
        return out

    small = [unflat(a) for a in (g_s, d_s, nm_s, nv_s)]

    big_w = [w_in, w_proj_attn, w_proj_sgu, w_out, w_ffn_gate, w_ffn_up, w_ffn_down]
    big_m = [m_w_in, m_w_proj_attn, m_w_proj_sgu, m_w_out, m_w_ffn_gate, m_w_ffn_up, m_w_ffn_down]
    big_v = [v_w_in, v_w_proj_attn, v_w_proj_sgu, v_w_out, v_w_ffn_gate, v_w_ffn_up, v_w_ffn_down]
    big_out = []
    for wgt, g, mm, vv, nm in zip(big_w, g_big, big_m, big_v, names):
        d, nm_, nv_ = _adamw(wgt[0], g, mm[0], vv[0], "adamw_" + nm)
        big_out.append([a[None] for a in (g, d, nm_, nv_)])

    small_at = {0: 0, 2: 1, 3: 2, 4: 3, 5: 4, 9: 5, 13: 6}
    big_at = {1: 0, 6: 1, 7: 2, 8: 3, 10: 4, 11: 5, 12: 6}
    outs = [loss[0, 0], dx[None]]
    for kind in range(4):
        for idx in range(14):
            outs.append(small[kind][small_at[idx]] if idx in small_at else big_out[big_at[idx]][kind])
    return tuple(outs)
```

```python
import functools
import math

import jax
import jax.numpy as jnp
from jax import lax
from jax.experimental import pallas as pl
from jax.experimental.pallas import tpu as pltpu

F32 = jnp.float32
BF16 = jnp.bfloat16

D_MODEL = 1024
HEAD_PAIR = 128
ATTN_W = 512
DILATIONS = (1, 4, 16)
BLK = 128
ROPE_HALF = 8
ROPE_THETA = 500000.0
SGU_W = 512
QKV_COLS = 4608
IN_COLS = 7680
D_FF = 2816
FF_TILE = 1408
EPS = 1e-6
N_DEV = 8
MASKED = -1e30

ADAM_LR = 0.001
ADAM_B1 = 0.9
ADAM_B2 = 0.999
ADAM_EPS = 1e-08
ADAM_WD = 0.01
ADAM_STEP = 10

MIB = 1024 * 1024
MESH = pl.DeviceIdType.MESH
ANY = pl.BlockSpec(memory_space=pl.ANY)


def _array(shape, dtype):
    return pltpu.HBM(tuple(shape), dtype)


PIN_BYTES = 4 * MIB


def _pin(x):
    if x.size * x.dtype.itemsize < PIN_BYTES:
        return x
    return pltpu.with_memory_space_constraint(x, pltpu.HBM)


def _pallas(body, **kwargs):
    call = pl.pallas_call(body, **kwargs)
    return lambda *args: call(*[_pin(a) for a in args])


def _params(sem, vmem_mib):
    return pltpu.CompilerParams(dimension_semantics=sem, vmem_limit_bytes=vmem_mib * MIB)


def _nt(a, b):
    return lax.dot_general(a, b, (((1,), (1,)), ((), ())), preferred_element_type=F32)


def _nn(a, b):
    return lax.dot_general(a, b, (((1,), (0,)), ((), ())), preferred_element_type=F32)


def _tn(a, b):
    return lax.dot_general(a, b, (((0,), (0,)), ((), ())), preferred_element_type=F32)


class _Exchange:
    def __init__(self, arrays, out_shapes, sem_shapes, phases):
        self.arrays, self.out_shapes, self.sem_shapes, self.phases = list(arrays), out_shapes, sem_shapes, phases

    def run_alone(self, name, vmem=False):
        n_in, n_out = len(self.arrays), len(self.out_shapes)

        def body(*refs):
            start, middle, finish = self.phases(refs[:n_in], refs[n_in:n_in + n_out], refs[n_in + n_out:])
            start()
            middle()
            finish()

        spec = pl.BlockSpec(memory_space=pltpu.VMEM) if vmem else ANY
        shapes = [jax.ShapeDtypeStruct(s.shape, s.dtype) for s in self.out_shapes] if vmem else self.out_shapes
        return (pl.pallas_call if vmem else _pallas)(
            body, name=name, out_shape=shapes, in_specs=[spec] * n_in, out_specs=[spec] * n_out,
            scratch_shapes=self.sem_shapes, compiler_params=pltpu.CompilerParams(vmem_limit_bytes=32 * MIB),
        )(*self.arrays)


def _call(body, *, name, grid, in_specs, out_specs, out_shape, args, vmem_mib, scratch_shapes=(), carried=None):
    n_in, n_out, n_scr = len(in_specs), len(out_specs), len(scratch_shapes)
    sem = ("arbitrary",) * len(grid)
    if carried is None:
        outs = _pallas(
            body, name=name, grid=grid, in_specs=in_specs, out_specs=out_specs, out_shape=out_shape,
            scratch_shapes=list(scratch_shapes), compiler_params=_params(sem, vmem_mib))(*args)
        return list(outs), []
    c_in, c_out = len(carried.arrays), len(carried.out_shapes)
    total = math.prod(grid)

    def full(*refs):
        own_in, car_in = refs[:n_in], refs[n_in:n_in + c_in]
        at = n_in + c_in
        own_out, car_out = refs[at:at + n_out], refs[at + n_out:at + n_out + c_out]
        at += n_out + c_out
        own_scr, sems = refs[at:at + n_scr], refs[at + n_scr:]
        step = pl.program_id(0)
        for axis in range(1, len(grid)):
            step = step * grid[axis] + pl.program_id(axis)
        start, middle, finish = carried.phases(car_in, car_out, sems)
        pl.when(step == 0)(start)
        pl.when(step == (3 * total) // 4)(middle)
        body(*own_in, *own_out, *own_scr)
        pl.when(step == total - 1)(finish)

    outs = _pallas(
        full, name=name, grid=grid, in_specs=list(in_specs) + [ANY] * c_in,
        out_specs=list(out_specs) + [ANY] * c_out, out_shape=list(out_shape) + list(carried.out_shapes),
        scratch_shapes=list(scratch_shapes) + list(carried.sem_shapes),
        compiler_params=_params(sem, vmem_mib))(*args, *carried.arrays)
    return list(outs[:n_out]), list(outs[n_out:])


def _all_gather(shards):
    n = len(shards)

    def phases(ins, outs, sems):
        send_sems, recv_sems, local_sems = sems
        x, y, c = lax.axis_index("x"), lax.axis_index("y"), lax.axis_index("c")
        me, sibling = (x, y, c), (x, y, 1 - c)
        chips = [(1 - x, y), (x, 1 - y), (1 - x, 1 - y)]

        def rows(m, px, py, pc):
            r = ins[m].shape[0]
            return outs[m].at[pl.ds((4 * px + 2 * py + pc) * r, r), :]

        def copy(m, k, block, to, src=None):
            return pltpu.make_async_remote_copy(
                src_ref=rows(m, *block) if src is None else src, dst_ref=rows(m, *block),
                send_sem=send_sems.at[m, k], recv_sem=recv_sems.at[m, k],
                device_id=to, device_id_type=MESH)

        def mine(m):
            return pltpu.make_async_copy(ins[m], rows(m, *me), local_sems.at[m])

        def first(m):
            return [copy(m, 0, me, sibling, src=ins[m])] + [
                copy(m, 1 + j, me, (*chip, c), src=ins[m]) for j, chip in enumerate(chips)]

        def passed(m):
            return [copy(m, 4 + j, (*chip, c), sibling) for j, chip in enumerate(chips)]

        def start():
            for m in range(n):
                mine(m).start()
            for m in range(n):
                for cp in first(m):
                    cp.start()

        def middle():
            for m in range(n):
                for j, chip in enumerate(chips):
                    copy(m, 1 + j, (*chip, c), me).wait_recv()
                    passed(m)[j].start()

        def finish():
            for m in range(n):
                copy(m, 0, sibling, me).wait_recv()
                for j, chip in enumerate(chips):
                    copy(m, 4 + j, (*chip, 1 - c), me).wait_recv()
            for m in range(n):
                for cp in first(m) + passed(m):
                    cp.wait_send()
                mine(m).wait()

        return start, middle, finish

    return _Exchange(
        shards, [_array((N_DEV * s.shape[0], s.shape[1]), s.dtype) for s in shards],
        [pltpu.SemaphoreType.DMA((n, 7)), pltpu.SemaphoreType.DMA((n, 7)), pltpu.SemaphoreType.DMA((n,))], phases)


def _pair_exchange(grads):
    n = len(grads)

    def phases(ins, outs, sems):
        send_sems, recv_sems = sems
        x, y, c = lax.axis_index("x"), lax.axis_index("y"), lax.axis_index("c")

        def copy(m):
            return pltpu.make_async_remote_copy(
                src_ref=ins[m].at[:, 1 - c], dst_ref=outs[m], send_sem=send_sems.at[m], recv_sem=recv_sems.at[m],
                device_id=(x, y, 1 - c), device_id_type=MESH)

        def start():
            for m in range(n):
                copy(m).start()

        def finish():
            for m in range(n):
                copy(m).wait()

        return start, lambda: None, finish

    return _Exchange(grads, [_array((4,) + g.shape[2:], g.dtype) for g in grads],
                     [pltpu.SemaphoreType.DMA((n,)), pltpu.SemaphoreType.DMA((n,))], phases)


def _chip_exchange(pair_sums):
    n = len(pair_sums)

    def phases(ins, outs, sems):
        send_sems, recv_sems = sems
        x, y, c = lax.axis_index("x"), lax.axis_index("y"), lax.axis_index("c")
        chips = [(1 - x, y), (x, 1 - y), (1 - x, 1 - y)]

        def copies():
            return [pltpu.make_async_remote_copy(
                src_ref=ins[m].at[2 * px + py], dst_ref=outs[m].at[j],
                send_sem=send_sems.at[m, j], recv_sem=recv_sems.at[m, j],
                device_id=(px, py, c), device_id_type=MESH)
                for m in range(n) for j, (px, py) in enumerate(chips)]

        def start():
            for cp in copies():
                cp.start()

        def finish():
            for cp in copies():
                cp.wait_recv()
            for cp in copies():
                cp.wait_send()

        return start, lambda: None, finish

    return _Exchange(pair_sums, [_array((3,) + p.shape[1:], p.dtype) for p in pair_sums],
                     [pltpu.SemaphoreType.DMA((n, 3)), pltpu.SemaphoreType.DMA((n, 3))], phases)


N_SLABS = ATTN_W // HEAD_PAIR


def _slab_scratch(tm, n=N_SLABS):
    return pltpu.VMEM((n, tm, HEAD_PAIR), F32)


def _rows_by_residue(dst_ref, slab_ref, r, tr, dtype, n=N_SLABS):
    for rho in range(r):
        for s in range(n):
            dst_ref[rho, :, s * HEAD_PAIR:(s + 1) * HEAD_PAIR] = (
                slab_ref[s, pl.ds(rho, tr, stride=r), :].astype(dtype))


def _rows_by_token(slab_ref, src_ref, r, tr, n=N_SLABS):
    for rho in range(r):
        for s in range(n):
            slab_ref[s, pl.ds(rho, tr, stride=r), :] = (
                src_ref[rho, :, s * HEAD_PAIR:(s + 1) * HEAD_PAIR].astype(F32))


def _norm1_and_rope_tables(x, g, pos_col, inv_freq_row, tm, carried=None):
    T = x.shape[0]

    def body(x_ref, g_ref, pos_ref, invf_ref, h_ref, cos_ref, s1_ref, s2_ref):
        xf = x_ref[...]
        r = lax.rsqrt(jnp.mean(xf * xf, axis=-1, keepdims=True) + EPS)
        h_ref[...] = (xf * r * g_ref[...]).astype(BF16)
        ang = pos_ref[...].astype(F32) * invf_ref[...]
        lane = lax.broadcasted_iota(jnp.int32, (1, HEAD_PAIR), 1) % 64
        cs, sn = jnp.cos(ang), jnp.sin(ang)
        cos_ref[...] = jnp.where(lane < 2 * ROPE_HALF, cs, 1.0)
        s1_ref[...] = jnp.where(lane < ROPE_HALF, -sn, 0.0)
        s2_ref[...] = jnp.where((lane >= ROPE_HALF) & (lane < 2 * ROPE_HALF), sn, 0.0)

    tab = _array((T, HEAD_PAIR), F32)
    row = pl.BlockSpec((tm, HEAD_PAIR), lambda i: (i, 0))
    wide = pl.BlockSpec((tm, D_MODEL), lambda i: (i, 0))
    return _call(
        body, name="norm1_rope_tables", grid=(T // tm,), out_shape=[_array((T, D_MODEL), BF16), tab, tab, tab],
        in_specs=[wide, pl.BlockSpec((1, D_MODEL), lambda i: (0, 0)), pl.BlockSpec((tm, 1), lambda i: (i, 0)),
                  pl.BlockSpec((1, HEAD_PAIR), lambda i: (0, 0))],
        out_specs=[wide, row, row, row], vmem_mib=24, args=(x, g, pos_col, inv_freq_row), carried=carried)


def _rope(y, cos, s1, s2):
    w = y.shape[1]
    rep = w // HEAD_PAIR
    return (y * jnp.tile(cos, (1, rep)) + pltpu.roll(y, w - ROPE_HALF, 1) * jnp.tile(s1, (1, rep))
            + pltpu.roll(y, ROPE_HALF, 1) * jnp.tile(s2, (1, rep)))


def _rope_transposed(dy, cos, s1, s2):
    w = dy.shape[1]
    rep = w // HEAD_PAIR
    return (dy * jnp.tile(cos, (1, rep)) + pltpu.roll(dy * jnp.tile(s1, (1, rep)), ROPE_HALF, 1)
            + pltpu.roll(dy * jnp.tile(s2, (1, rep)), w - ROPE_HALF, 1))


def _resident(shape, block=None):
    at = (0,) * len(shape) if block is None else block
    return pl.BlockSpec(shape, lambda *_: at, pipeline_mode=pl.Buffered(1))


def _qkv_proj(h, win_t, tables, tm, carried=None):
    T = h.shape[0]

    def body(h_ref, w_ref, cos_ref, s1_ref, s2_ref, o0, o1, o2, *slabs):
        hv = h_ref[...]
        cos, s1, s2 = cos_ref[...], s1_ref[...], s2_ref[...]
        for kind in range(3):
            for g, (o_ref, r) in enumerate(zip((o0, o1, o2), DILATIONS)):
                blk = 3 * kind + g
                y = _nt(hv, w_ref[blk * ATTN_W:(blk + 1) * ATTN_W, :])
                if kind < 2:
                    y = _rope(y, cos, s1, s2)
                if kind == 0:
                    y = y * 0.125
                cols = slice(kind * ATTN_W, (kind + 1) * ATTN_W)
                if r == 1:
                    o_ref[0, :, cols] = y.astype(BF16)
                    continue
                slab = slabs[blk % len(slabs)]
                for s in range(N_SLABS):
                    slab[s] = y[:, s * HEAD_PAIR:(s + 1) * HEAD_PAIR]
                for rho in range(r):
                    for s in range(N_SLABS):
                        at = kind * ATTN_W + s * HEAD_PAIR
                        o_ref[rho, :, at:at + HEAD_PAIR] = slab[s, pl.ds(rho, tm // r, stride=r), :].astype(BF16)

    row = pl.BlockSpec((tm, HEAD_PAIR), lambda i: (i, 0))
    return _call(
        body, name="qkv_proj", grid=(T // tm,),
        out_shape=[_array((r, T // r, 3 * ATTN_W), BF16) for r in DILATIONS],
        in_specs=[pl.BlockSpec((tm, D_MODEL), lambda i: (i, 0)), _resident((QKV_COLS, D_MODEL)), row, row, row],
        out_specs=[pl.BlockSpec((r, tm // r, 3 * ATTN_W), lambda i: (0, i, 0)) for r in DILATIONS],
        scratch_shapes=[_slab_scratch(tm)] * 3, vmem_mib=48, args=(h, win_t, *tables), carried=carried)


def _uv_gate_proj(h, win_t, tm, carried=None):
    T = h.shape[0]
    half = (IN_COLS - QKV_COLS) // 2

    def body(h_ref, wa_ref, wb_ref, o_ref):
        hv = h_ref[...]
        o_ref[:, :half] = _nt(hv, wa_ref[...]).astype(BF16)
        o_ref[:, half:] = _nt(hv, wb_ref[...]).astype(BF16)

    blk0 = QKV_COLS // half
    return _call(
        body, name="uv_gate_proj", grid=(T // tm,), out_shape=[_array((T, 2 * half), BF16)],
        in_specs=[pl.BlockSpec((tm, D_MODEL), lambda i: (i, 0)), _resident((half, D_MODEL), (blk0, 0)),
                  _resident((half, D_MODEL), (blk0 + 1, 0))],
        out_specs=[pl.BlockSpec((tm, 2 * half), lambda i: (i, 0))], vmem_mib=40, args=(h, win_t, win_t),
        carried=carried)


def _band_mask(n):
    row = lax.broadcasted_iota(jnp.int32, (2 * BLK, 2 * BLK), 0) & (BLK - 1)
    col = lax.broadcasted_iota(jnp.int32, (2 * BLK, 2 * BLK), 1)
    has_prev = (jnp.zeros_like(row) + n) > 0
    return ((col < BLK) & (col >= row) & has_prev) | ((col >= BLK) & (col - BLK <= row))


def _head_lanes():
    lane = lax.broadcasted_iota(jnp.int32, (1, HEAD_PAIR), 1)
    return lane < 64, lane >= 64


def _stack_heads(x, head0, head1):
    zero = jnp.zeros_like(x)
    return jnp.concatenate([jnp.where(head0, x, zero), jnp.where(head1, x, zero)], axis=0)


def _unstack_heads(y, head0):
    return jnp.where(head0, y[:BLK], y[BLK:])


def _per_head(stats, col):
    return jnp.concatenate([stats[:, col:col + 1], stats[:, col + 1:col + 2]], axis=0)


def _attn_sub_blocks(length):
    for n in (8, 4, 2):
        if length % (n * BLK) == 0:
            return n
    return 1


def _attn_fwd(qkv, group, carried=None):
    r, L, _ = qkv.shape
    nsub = _attn_sub_blocks(L)
    rows = nsub * BLK

    def body(q_ref, kp_ref, kc_ref, vp_ref, vc_ref, o_ref, lse_ref):
        head0, head1 = _head_lanes()
        lane = lax.broadcasted_iota(jnp.int32, (1, HEAD_PAIR), 1)
        for j in range(nsub):
            cur = slice(j * BLK, (j + 1) * BLK)
            before = slice((j - 1) * BLK, j * BLK)
            valid = _band_mask(pl.program_id(1) + j)
            stats = jnp.zeros((BLK, HEAD_PAIR), F32)
            for p in range(N_SLABS):
                sl = slice(p * HEAD_PAIR, (p + 1) * HEAD_PAIR)
                k_prev, v_prev = (kp_ref[:, sl], vp_ref[:, sl]) if j == 0 else (kc_ref[before, sl], vc_ref[before, sl])
                k2 = jnp.concatenate([k_prev, kc_ref[cur, sl]], axis=0)
                v2 = jnp.concatenate([v_prev, vc_ref[cur, sl]], axis=0)
                s = jnp.where(valid, _nt(_stack_heads(q_ref[cur, sl], head0, head1), k2), MASKED)
                m = jnp.max(s, axis=1, keepdims=True)
                e = jnp.exp(s - m)
                den = jnp.sum(e, axis=1, keepdims=True)
                o_ref[cur, sl] = _unstack_heads(_nn(e.astype(BF16), v2) / den, head0)
                lse = m + jnp.log(den)
                stats = jnp.where(lane == 2 * p, lse[:BLK], jnp.where(lane == 2 * p + 1, lse[BLK:], stats))
            lse_ref[cur, :] = stats

    def cur_blk(kind, width=ATTN_W):
        return pl.BlockSpec((None, rows, width), lambda rho, n: (rho, n, kind))

    def prev_blk(kind):
        return pl.BlockSpec((None, BLK, ATTN_W), lambda rho, n: (rho, jnp.maximum(n * nsub - 1, 0), kind))

    return _call(
        body, name=f"attn_fwd_g{group}", grid=(r, L // rows),
        out_shape=[_array((r, L, ATTN_W), F32), _array((r, L, HEAD_PAIR), F32)],
        in_specs=[cur_blk(0), prev_blk(1), cur_blk(1), prev_blk(2), cur_blk(2)],
        out_specs=[cur_blk(0), cur_blk(0, HEAD_PAIR)],
        vmem_mib=24, args=(qkv, qkv, qkv, qkv, qkv), carried=carried)


def _attn_combine(outs, lses, tm):
    T = outs[0].shape[1]

    def body(o0, l0, o1, l1, o2, l2, attn_ref, attn_b_ref, lse_ref, so1, sl1, so2, sl2):
        for o_in, l_in, so, sl, r in ((o1, l1, so1, sl1, DILATIONS[1]), (o2, l2, so2, sl2, DILATIONS[2])):
            _rows_by_token(so, o_in, r, tm // r)
            _rows_by_token(sl, l_in, r, tm // r, n=1)
        head0, _ = _head_lanes()
        a0, a1, a2 = l0[0], sl1[0], sl2[0]
        mx = jnp.maximum(jnp.maximum(a0, a1), a2)
        e0, e1, e2 = jnp.exp(a0 - mx), jnp.exp(a1 - mx), jnp.exp(a2 - mx)
        tot = e0 + e1 + e2
        lse_ref[...] = mx + jnp.log(tot)
        w0, w1, w2 = e0 / tot, e1 / tot, e2 / tot
        for s in range(N_SLABS):
            lanes = slice(s * HEAD_PAIR, (s + 1) * HEAD_PAIR)

            def lanes_of(w):
                return jnp.where(head0, w[:, 2 * s:2 * s + 1], w[:, 2 * s + 1:2 * s + 2])

            mixed = lanes_of(w0) * o0[0, :, lanes] + lanes_of(w1) * so1[s] + lanes_of(w2) * so2[s]
            attn_ref[:, lanes] = mixed
            attn_b_ref[:, lanes] = mixed.astype(BF16)

    ins, specs = [], []
    for g, r in enumerate(DILATIONS):
        ins += [outs[g], lses[g]]
        specs += [pl.BlockSpec((r, tm // r, ATTN_W), lambda i: (0, i, 0)),
                  pl.BlockSpec((r, tm // r, HEAD_PAIR), lambda i: (0, i, 0))]
    return _pallas(
        body, name="attn_combine", grid=(T // tm,),
        out_shape=[_array((T, ATTN_W), F32), _array((T, ATTN_W), BF16), _array((T, HEAD_PAIR), F32)], in_specs=specs,
        out_specs=[pl.BlockSpec((tm, ATTN_W), lambda i: (i, 0)), pl.BlockSpec((tm, ATTN_W), lambda i: (i, 0)),
                   pl.BlockSpec((tm, HEAD_PAIR), lambda i: (i, 0))],
        scratch_shapes=[_slab_scratch(tm), _slab_scratch(tm, 1), _slab_scratch(tm), _slab_scratch(tm, 1)],
        compiler_params=_params(("parallel",), 32),
    )(*ins)


def _gelu(x):
    return 0.5 * x * (1.0 + lax.erf(x * (1.0 / math.sqrt(2.0))))


def _gelu_grad(x):
    return 0.5 * (1.0 + lax.erf(x * (1.0 / math.sqrt(2.0)))) + x * jnp.exp(-0.5 * x * x) * (1.0 / math.sqrt(2.0 * math.pi))


def _causal():
    row = lax.broadcasted_iota(jnp.int32, (BLK, BLK), 0)
    col = lax.broadcasted_iota(jnp.int32, (BLK, BLK), 1)
    return col <= row


def _bias_lanes(bt):
    grp = lax.broadcasted_iota(jnp.int32, (1, SGU_W), 1) // 64
    out = jnp.zeros((BLK, SGU_W), F32)
    for g in range(8):
        out = jnp.where(grp == g, bt[:, g:g + 1], out)
    return out


def _sgu_normalise(uv, ln_g, ln_b):
    z = _gelu(uv)
    u, v = z[:, :SGU_W], z[:, SGU_W:]
    mu = jnp.mean(v, axis=-1, keepdims=True)
    xc = v - mu
    rstd = lax.rsqrt(jnp.mean(xc * xc, axis=-1, keepdims=True) + EPS)
    xh = xc * rstd
    return u, xh, rstd, xh * ln_g + ln_b


def _sgu_mix(wc_ref, vb, head0):
    chunks = []
    for ch in range(vb.shape[0] // BLK):
        pairs = []
        for p in range(SGU_W // HEAD_PAIR):
            v_pair = vb[ch * BLK:(ch + 1) * BLK, p * HEAD_PAIR:(p + 1) * HEAD_PAIR]
            pairs.append(jnp.where(head0, _nn(wc_ref[2 * p], v_pair), _nn(wc_ref[2 * p + 1], v_pair)))
        chunks.append(jnp.concatenate(pairs, axis=1))
    return jnp.concatenate(chunks, axis=0)


def _sgu_fwd(uvg, ln_g, ln_b, w_s, b_t, tm):
    T = uvg.shape[0]

    def body(uv_ref, g_ref, b_ref, w_ref, bt_ref, o_ref, wc_ref, bias_ref):
        @pl.when(pl.program_id(0) == 0)
        def _():
            causal = _causal()
            for g in range(8):
                wc_ref[g] = jnp.where(causal, w_ref[g], 0.0).astype(BF16)
            bias_ref[...] = _bias_lanes(bt_ref[...])

        u, _, _, vn = _sgu_normalise(uv_ref[...].astype(F32), g_ref[...], b_ref[...])
        mixed = _sgu_mix(wc_ref, vn.astype(BF16), _head_lanes()[0])
        o_ref[...] = (u * (mixed + jnp.tile(bias_ref[...], (tm // BLK, 1)))).astype(BF16)

    vec = pl.BlockSpec((1, SGU_W), lambda i: (0, 0))
    return _pallas(
        body, name="sgu_fwd", grid=(T // tm,), out_shape=_array((T, SGU_W), BF16),
        in_specs=[pl.BlockSpec((tm, 2 * SGU_W), lambda i: (i, 0)), vec, vec,
                  pl.BlockSpec((8, BLK, BLK), lambda i: (0, 0, 0)), pl.BlockSpec((BLK, 8), lambda i: (0, 0))],
        out_specs=pl.BlockSpec((tm, SGU_W), lambda i: (i, 0)),
        scratch_shapes=[pltpu.VMEM((8, BLK, BLK), BF16), pltpu.VMEM((BLK, SGU_W), F32)],
        compiler_params=_params(("arbitrary",), 32),
    )(uvg, ln_g, ln_b, w_s, b_t)


def _merge_fwd(attn, sgu, wpa_t, wps_t, uvg, w_out, x, g2, tm):
    T = attn.shape[0]

    def body(attn_ref, sgu_ref, wpa_ref, wps_ref, ga_ref, gb_ref, wo_ref, x_ref, g_ref,
             pa_ref, ps_ref, m_ref, h_ref, x1_ref):
        pa = _nt(attn_ref[...], wpa_ref[...])
        ps = _nt(sgu_ref[...], wps_ref[...])
        pa_ref[...] = pa.astype(BF16)
        ps_ref[...] = ps.astype(BF16)
        ga, gb = ga_ref[...].astype(F32), gb_ref[...].astype(F32)
        merged = (jax.nn.sigmoid(ga) * pa + jax.nn.sigmoid(gb) * ps).astype(BF16)
        m_ref[...] = merged
        x1 = x_ref[...] + _nn(merged, wo_ref[...])
        x1_ref[...] = x1
        r = lax.rsqrt(jnp.mean(x1 * x1, axis=-1, keepdims=True) + EPS)
        h_ref[...] = (x1 * r * g_ref[...]).astype(BF16)

    half = pl.BlockSpec((tm, ATTN_W), lambda i: (i, 0))
    wide = pl.BlockSpec((tm, D_MODEL), lambda i: (i, 0))
    w = _resident((D_MODEL, ATTN_W))
    res = _array((T, D_MODEL), BF16)
    return _pallas(
        body, name="merge_fwd", grid=(T // tm,), out_shape=[res, res, res, res, _array((T, D_MODEL), F32)],
        in_specs=[half, half, w, w, pl.BlockSpec((tm, D_MODEL), lambda i: (i, 1)),
                  pl.BlockSpec((tm, D_MODEL), lambda i: (i, 2)), _resident((D_MODEL, D_MODEL)), wide,
                  pl.BlockSpec((1, D_MODEL), lambda i: (0, 0))],
        out_specs=[wide] * 5, compiler_params=_params(("parallel",), 48),
    )(attn, sgu, wpa_t, wps_t, uvg, uvg, w_out, x, g2)


def _ffn_fwd(h2, wg_t, wu_t, tm):
    T = h2.shape[0]

    def body(h_ref, wg_ref, wu_ref, gate_ref, up_ref, ff_ref):
        h = h_ref[...]
        for j in range(D_FF // FF_TILE):
            cols = slice(j * FF_TILE, (j + 1) * FF_TILE)
            gate, up = _nt(h, wg_ref[cols, :]), _nt(h, wu_ref[cols, :])
            gate_ref[:, cols] = gate.astype(BF16)
            up_ref[:, cols] = up.astype(BF16)
            ff_ref[:, cols] = (gate * jax.nn.sigmoid(gate) * up).astype(BF16)

    w = _resident((D_FF, D_MODEL))
    o = pl.BlockSpec((tm, D_FF), lambda i: (i, 0))
    res = _array((T, D_FF), BF16)
    return _pallas(
        body, name="ffn_fwd", grid=(T // tm,), out_shape=[res, res, res],
        in_specs=[pl.BlockSpec((tm, D_MODEL), lambda i: (i, 0)), w, w], out_specs=[o, o, o],
        compiler_params=_params(("parallel",), 52),
    )(h2, wg_t, wu_t)


def _down_loss(ff, w_down, x1, final_g, target, tm):
    T = x1.shape[0]

    def body(ff_ref, w_ref, x1_ref, g_ref, t_ref, dx_ref, dxb_ref, loss_ref, dg_ref):
        @pl.when(pl.program_id(0) == 0)
        def _():
            loss_ref[...] = jnp.zeros_like(loss_ref)
            dg_ref[...] = jnp.zeros_like(dg_ref)

        x2 = x1_ref[...] + _nn(ff_ref[...], w_ref[...])
        g = g_ref[...]
        r = lax.rsqrt(jnp.mean(x2 * x2, axis=-1, keepdims=True) + EPS)
        xh = x2 * r
        err = xh * g - t_ref[...]
        loss_ref[...] += jnp.sum(err * err, axis=0, keepdims=True) * (0.5 / D_MODEL)
        dy = err * (1.0 / D_MODEL)
        dg_ref[...] += jnp.sum(dy * xh, axis=0, keepdims=True)
        dxh = dy * g
        dx = r * (dxh - xh * jnp.mean(dxh * xh, axis=-1, keepdims=True))
        dx_ref[...] = dx
        dxb_ref[...] = dx.astype(BF16)

    wide = pl.BlockSpec((tm, D_MODEL), lambda i: (i, 0))
    vec = pl.BlockSpec((1, D_MODEL), lambda i: (0, 0))
    vec_shape = _array((1, D_MODEL), F32)
    return _pallas(
        body, name="down_loss", grid=(T // tm,),
        out_shape=[_array((T, D_MODEL), F32), _array((T, D_MODEL), BF16),
                   vec_shape, vec_shape],
        in_specs=[pl.BlockSpec((tm, D_FF), lambda i: (i, 0)), _resident((D_FF, D_MODEL)), wide, vec, wide],
        out_specs=[wide, wide, vec, vec],
        compiler_params=_params(("arbitrary",), 40),
    )(ff, w_down, x1, final_g, target)


def _rmsnorm_bwd(dh, xin, g, d_res, dg_ref):
    r = lax.rsqrt(jnp.mean(xin * xin, axis=-1, keepdims=True) + EPS)
    xh = xin * r
    dg_ref[...] += jnp.sum(dh * xh, axis=0, keepdims=True)
    dxh = dh * g
    return d_res + r * (dxh - xh * jnp.mean(dxh * xh, axis=-1, keepdims=True))


def _ffn_bwd(dx2, dx2b, w_down, wg_t, wu_t, gate, up, x1, g2, tm):
    T = dx2.shape[0]

    def body(dx_ref, dxb_ref, wd_ref, wg_ref, wu_ref, gate_ref, up_ref, x1_ref, g_ref,
             dgate_ref, dup_ref, dx1_ref, dx1b_ref, dg_ref):
        @pl.when(pl.program_id(0) == 0)
        def _():
            dg_ref[...] = jnp.zeros_like(dg_ref)

        dxb = dxb_ref[...]
        dh = jnp.zeros((tm, D_MODEL), F32)
        for j in range(D_FF // FF_TILE):
            cols = slice(j * FF_TILE, (j + 1) * FF_TILE)
            dff = _nt(dxb, wd_ref[cols, :])
            gate, up = gate_ref[:, cols].astype(F32), up_ref[:, cols].astype(F32)
            sg = jax.nn.sigmoid(gate)
            dgate = (dff * up * sg * (1.0 + gate * (1.0 - sg))).astype(BF16)
            dup = (dff * gate * sg).astype(BF16)
            dgate_ref[:, cols] = dgate
            dup_ref[:, cols] = dup
            dh += _nn(dgate, wg_ref[cols, :]) + _nn(dup, wu_ref[cols, :])
        dx1 = _rmsnorm_bwd(dh, x1_ref[...], g_ref[...], dx_ref[...], dg_ref)
        dx1_ref[...] = dx1
        dx1b_ref[...] = dx1.astype(BF16)

    wide = pl.BlockSpec((tm, D_MODEL), lambda i: (i, 0))
    ffw = pl.BlockSpec((tm, D_FF), lambda i: (i, 0))
    vec = pl.BlockSpec((1, D_MODEL), lambda i: (0, 0))
    w = _resident((D_FF, D_MODEL))
    ff_shape = _array((T, D_FF), BF16)
    return _pallas(
        body, name="ffn_bwd", grid=(T // tm,),
        out_shape=[ff_shape, ff_shape, _array((T, D_MODEL), F32),
                   _array((T, D_MODEL), BF16), _array((1, D_MODEL), F32)],
        in_specs=[wide, wide, w, w, w, ffw, ffw, wide, vec],
        out_specs=[ffw, ffw, wide, wide, vec], compiler_params=_params(("arbitrary",), 56),
    )(dx2, dx2b, w_down, wg_t, wu_t, gate, up, x1, g2)


def _mm_tn(a, b, *, tmm, tk, name, into=None, row_block0=0, rows_total=None):
    T, M = a.shape
    N = b.shape[1]
    rows_total = M if rows_total is None else rows_total

    def body(*refs):
        a_ref, b_ref, o_ref = refs[0], refs[1], refs[-1]

        @pl.when(pl.program_id(1) == 0)
        def _():
            o_ref[...] = jnp.zeros_like(o_ref)

        o_ref[...] += _tn(a_ref[...], b_ref[...])

    ins = [a, b] + ([] if into is None else [into])
    specs = [pl.BlockSpec((tk, tmm), lambda i, k: (k, i)), pl.BlockSpec((tk, N), lambda i, k: (k, 0))]
    return _pallas(
        body, name=name, grid=(M // tmm, T // tk),
        out_shape=_array((rows_total, N), F32),
        in_specs=specs + ([] if into is None else [ANY]),
        out_specs=pl.BlockSpec((tmm, N), lambda i, k: (row_block0 + i, 0)),
        input_output_aliases={} if into is None else {2: 0},
        compiler_params=_params(("parallel", "arbitrary"), 48),
    )(*ins)


def _in_bwd(dqkv, duv, dgates, win_t, x, g1, dx1, tm, carried=None):
    T = x.shape[0]
    uv0, gates0 = QKV_COLS, QKV_COLS + 2 * SGU_W

    def body(dq_ref, du_ref, dgt_ref, w_ref, x_ref, g_ref, d_ref, dx_ref, dg_ref):
        @pl.when(pl.program_id(0) == 0)
        def _():
            dg_ref[...] = jnp.zeros_like(dg_ref)

        dh = (_nn(dq_ref[...], w_ref[:uv0, :]) + _nn(du_ref[...], w_ref[uv0:gates0, :])
              + _nn(dgt_ref[...], w_ref[gates0:, :]))
        dx_ref[...] = _rmsnorm_bwd(dh, x_ref[...], g_ref[...], d_ref[...], dg_ref)

    def cols(n):
        return pl.BlockSpec((tm, n), lambda i: (i, 0))

    wide = cols(D_MODEL)
    vec = pl.BlockSpec((1, D_MODEL), lambda i: (0, 0))
    return _call(
        body, name="in_bwd", grid=(T // tm,),
        out_shape=[_array((T, D_MODEL), F32), _array((1, D_MODEL), F32)],
        in_specs=[cols(QKV_COLS), cols(2 * SGU_W), cols(2 * D_MODEL), _resident((IN_COLS, D_MODEL)), wide, vec, wide],
        out_specs=[wide, vec], vmem_mib=56, args=(dqkv, duv, dgates, win_t, x, g1, dx1), carried=carried)


def _merge_bwd(dx1b, w_out, pa, ps, uvg, wpa_t, wps_t, tm, carried=None):
    T = dx1b.shape[0]

    def body(dx_ref, w_ref, pa_ref, ps_ref, ga_ref, gb_ref, wpa_ref, wps_ref,
             dpa_ref, dps_ref, dg_ref, dattn_ref, dsgu_ref):
        dm = _nt(dx_ref[...], w_ref[...])
        ga, gb = jax.nn.sigmoid(ga_ref[...].astype(F32)), jax.nn.sigmoid(gb_ref[...].astype(F32))
        dpa, dps = (dm * ga).astype(BF16), (dm * gb).astype(BF16)
        dpa_ref[...] = dpa
        dps_ref[...] = dps
        dg_ref[:, :D_MODEL] = (dm * pa_ref[...].astype(F32) * ga * (1.0 - ga)).astype(BF16)
        dg_ref[:, D_MODEL:] = (dm * ps_ref[...].astype(F32) * gb * (1.0 - gb)).astype(BF16)
        dattn_ref[...] = _nn(dpa, wpa_ref[...])
        dsgu_ref[...] = _nn(dps, wps_ref[...])

    wide = pl.BlockSpec((tm, D_MODEL), lambda i: (i, 0))
    half = pl.BlockSpec((tm, ATTN_W), lambda i: (i, 0))
    w = _resident((D_MODEL, ATTN_W))
    res = _array((T, D_MODEL), BF16)
    res_half = _array((T, ATTN_W), F32)
    return _call(
        body, name="merge_bwd", grid=(T // tm,),
        out_shape=[res, res, _array((T, 2 * D_MODEL), BF16), res_half, res_half],
        in_specs=[wide, _resident((D_MODEL, D_MODEL)), wide, wide,
                  pl.BlockSpec((tm, D_MODEL), lambda i: (i, 1)), pl.BlockSpec((tm, D_MODEL), lambda i: (i, 2)), w, w],
        out_specs=[wide, wide, pl.BlockSpec((tm, 2 * D_MODEL), lambda i: (i, 0)), half, half],
        vmem_mib=48, args=(dx1b, w_out, pa, ps, uvg, uvg, wpa_t, wps_t), carried=carried)


def _sgu_bwd(uvg, dsgu, ln_g, ln_b, w_s, b_t, tm, carried=None):
    T = uvg.shape[0]
    nsteps = T // tm

    def body(uv_ref, ds_ref, g_ref, b_ref, w_ref, bt_ref, duv_ref, dw_ref, dbt_ref, dg_ref, db_ref,
             wc_ref, wct_ref, bias_ref, dbias_ref):
        step = pl.program_id(0)
        head0, head1 = _head_lanes()

        @pl.when(step == 0)
        def _():
            causal = _causal()
            for g in range(8):
                wc = jnp.where(causal, w_ref[g], 0.0)
                wc_ref[g] = wc.astype(BF16)
                wct_ref[g] = wc.T.astype(BF16)
            bias_ref[...] = _bias_lanes(bt_ref[...])
            dbias_ref[...] = jnp.zeros_like(dbias_ref)
            dw_ref[...] = jnp.zeros_like(dw_ref)
            dg_ref[...] = jnp.zeros_like(dg_ref)
            db_ref[...] = jnp.zeros_like(db_ref)

        uv = uv_ref[...].astype(F32)
        ln_gain = g_ref[...]
        u, xh, rstd, vn = _sgu_normalise(uv, ln_gain, b_ref[...])
        vb = vn.astype(BF16)
        mixed = _sgu_mix(wc_ref, vb, head0) + jnp.tile(bias_ref[...], (tm // BLK, 1))
        dout = ds_ref[...]
        du = dout * mixed
        dmixed = dout * u
        dmb = dmixed.astype(BF16)
        dvn_chunks = []
        for ch in range(tm // BLK):
            rows = slice(ch * BLK, (ch + 1) * BLK)
            dbias_ref[...] += dmixed[rows]
            pairs = []
            for p in range(SGU_W // HEAD_PAIR):
                lanes = slice(p * HEAD_PAIR, (p + 1) * HEAD_PAIR)
                dm_pair, v_pair = dmb[rows, lanes], vb[rows, lanes]
                acc = jnp.zeros((BLK, HEAD_PAIR), F32)
                for hh, half in enumerate((head0, head1)):
                    dm_h = jnp.where(half, dm_pair, jnp.zeros_like(dm_pair))
                    dw_ref[2 * p + hh] += _nt(dm_h, v_pair)
                    acc += _nn(wct_ref[2 * p + hh], dm_h)
                pairs.append(acc)
            dvn_chunks.append(jnp.concatenate(pairs, axis=1))
        dvn = jnp.concatenate(dvn_chunks, axis=0)
        dg_ref[...] += jnp.sum(dvn * xh, axis=0, keepdims=True)
        db_ref[...] += jnp.sum(dvn, axis=0, keepdims=True)
        dxh = dvn * ln_gain
        dv = rstd * (dxh - jnp.mean(dxh, axis=-1, keepdims=True) - xh * jnp.mean(dxh * xh, axis=-1, keepdims=True))
        dgelu = _gelu_grad(uv)
        duv_ref[:, :SGU_W] = (du * dgelu[:, :SGU_W]).astype(BF16)
        duv_ref[:, SGU_W:] = (dv * dgelu[:, SGU_W:]).astype(BF16)

        @pl.when(step == nsteps - 1)
        def _():
            causal = _causal()
            for g in range(8):
                dw_ref[g] = jnp.where(causal, dw_ref[g], 0.0)
            grp = lax.broadcasted_iota(jnp.int32, (1, SGU_W), 1) // 64
            col = lax.broadcasted_iota(jnp.int32, (1, 8), 1)
            dbias = dbias_ref[...]
            out = jnp.zeros((BLK, 8), F32)
            for g in range(8):
                s = jnp.sum(jnp.where(grp == g, dbias, 0.0), axis=1, keepdims=True)
                out = jnp.where(col == g, s, out)
            dbt_ref[...] = out

    vec = pl.BlockSpec((1, SGU_W), lambda i: (0, 0))
    w3 = pl.BlockSpec((8, BLK, BLK), lambda i: (0, 0, 0))
    bt = pl.BlockSpec((BLK, 8), lambda i: (0, 0))
    return _call(
        body, name="sgu_bwd", grid=(nsteps,),
        out_shape=[_array((T, 2 * SGU_W), BF16), _array((8, BLK, BLK), F32),
                   _array((BLK, 8), F32), _array((1, SGU_W), F32),
                   _array((1, SGU_W), F32)],
        in_specs=[pl.BlockSpec((tm, 2 * SGU_W), lambda i: (i, 0)), pl.BlockSpec((tm, SGU_W), lambda i: (i, 0)),
                  vec, vec, w3, bt],
        out_specs=[pl.BlockSpec((tm, 2 * SGU_W), lambda i: (i, 0)), w3, bt, vec, vec],
        scratch_shapes=[pltpu.VMEM((8, BLK, BLK), BF16), pltpu.VMEM((8, BLK, BLK), BF16),
                        pltpu.VMEM((BLK, SGU_W), F32), pltpu.VMEM((BLK, SGU_W), F32)],
        vmem_mib=40, args=(uvg, dsgu, ln_g, ln_b, w_s, b_t), carried=carried)


D_LANE0 = 8


def _attn_bwd_prepare(dattn, attn, lse, tm):
    T = dattn.shape[0]

    def body(da_ref, at_ref, lse_ref, *outs):
        scr_da, scr_st = outs[-2:]
        outs = outs[:-2]
        head0, _ = _head_lanes()
        lane = lax.broadcasted_iota(jnp.int32, (1, HEAD_PAIR), 1)
        stats = lse_ref[...]
        for s in range(N_SLABS):
            lanes = slice(s * HEAD_PAIR, (s + 1) * HEAD_PAIR)
            da = da_ref[:, lanes]
            pp = da * at_ref[:, lanes]
            d0 = jnp.sum(jnp.where(head0, pp, 0.0), axis=1, keepdims=True)
            d1 = jnp.sum(jnp.where(head0, 0.0, pp), axis=1, keepdims=True)
            stats = jnp.where(lane == D_LANE0 + 2 * s, d0, jnp.where(lane == D_LANE0 + 2 * s + 1, d1, stats))
            scr_da[s] = da
            outs[0][0, :, lanes] = da.astype(BF16)
        scr_st[0] = stats
        outs[1][0] = stats
        for g, r in enumerate(DILATIONS):
            if r > 1:
                _rows_by_residue(outs[2 * g], scr_da, r, tm // r, BF16)
                _rows_by_residue(outs[2 * g + 1], scr_st, r, tm // r, F32, n=1)

    tok = pl.BlockSpec((tm, ATTN_W), lambda i: (i, 0))
    shapes, specs = [], []
    for r in DILATIONS:
        shapes += [_array((r, T // r, ATTN_W), BF16), _array((r, T // r, HEAD_PAIR), F32)]
        specs += [pl.BlockSpec((r, tm // r, ATTN_W), lambda i: (0, i, 0)),
                  pl.BlockSpec((r, tm // r, HEAD_PAIR), lambda i: (0, i, 0))]
    return _pallas(
        body, name="attn_bwd_prepare", grid=(T // tm,), out_shape=shapes,
        in_specs=[tok, tok, pl.BlockSpec((tm, HEAD_PAIR), lambda i: (i, 0))],
        out_specs=specs, scratch_shapes=[_slab_scratch(tm), _slab_scratch(tm, 1)],
        compiler_params=_params(("parallel",), 40),
    )(dattn, attn, lse)


def _attn_bwd(qkv, dattn, stats, group, carried=None):
    r, L, _ = qkv.shape
    nsub = _attn_sub_blocks(L)
    rows = nsub * BLK
    nb = L // rows
    keep = rows - BLK
    whole = nb == 1

    def body(q_ref, kp_ref, kc_ref, vp_ref, vc_ref, da_ref, st_ref, dq_ref, dk_ref, dv_ref, carry_k, carry_v):
        n = pl.program_id(1)

        if not whole:
            @pl.when(n == 0)
            def _():
                carry_k[...] = jnp.zeros_like(carry_k)
                carry_v[...] = jnp.zeros_like(carry_v)

        @pl.when(n < nb)
        def _():
            head0, head1 = _head_lanes()
            for p in range(N_SLABS):
                sl = slice(p * HEAD_PAIR, (p + 1) * HEAD_PAIR)
                dk_parts = [jnp.zeros((BLK, HEAD_PAIR), F32) for _ in range(nsub + 1)]
                dv_parts = [jnp.zeros((BLK, HEAD_PAIR), F32) for _ in range(nsub + 1)]
                for j in range(nsub):
                    cur = slice(j * BLK, (j + 1) * BLK)
                    before = slice((j - 1) * BLK, j * BLK)
                    valid = _band_mask(n + j)
                    st = st_ref[cur, :]
                    k_prev, v_prev = (kp_ref[:, sl], vp_ref[:, sl]) if j == 0 else (kc_ref[before, sl], vc_ref[before, sl])
                    k2 = jnp.concatenate([k_prev, kc_ref[cur, sl]], axis=0)
                    v2 = jnp.concatenate([v_prev, vc_ref[cur, sl]], axis=0)
                    qs = _stack_heads(q_ref[cur, sl], head0, head1)
                    das = _stack_heads(da_ref[cur, sl], head0, head1)
                    prob = jnp.where(valid, jnp.exp(_nt(qs, k2) - _per_head(st, 2 * p)), 0.0)
                    ds = (prob * (_nt(das, v2) - _per_head(st, D_LANE0 + 2 * p))).astype(BF16)
                    dk2 = _tn(ds, qs)
                    dv2 = _tn(prob.astype(BF16), das)
                    dq_ref[cur, sl] = _unstack_heads(_nn(ds, k2), head0).astype(BF16)
                    dk_parts[j] += dk2[:BLK]
                    dk_parts[j + 1] += dk2[BLK:]
                    dv_parts[j] += dv2[:BLK]
                    dv_parts[j + 1] += dv2[BLK:]
                if whole:
                    for j in range(nsub):
                        dk_ref[j * BLK:(j + 1) * BLK, sl] = dk_parts[j + 1].astype(BF16)
                        dv_ref[j * BLK:(j + 1) * BLK, sl] = dv_parts[j + 1].astype(BF16)
                    continue
                if keep:
                    dk_ref[:keep, sl] = carry_k[:keep, sl].astype(BF16)
                    dv_ref[:keep, sl] = carry_v[:keep, sl].astype(BF16)
                dk_ref[keep:, sl] = (carry_k[keep:, sl] + dk_parts[0]).astype(BF16)
                dv_ref[keep:, sl] = (carry_v[keep:, sl] + dv_parts[0]).astype(BF16)
                for j in range(nsub):
                    carry_k[j * BLK:(j + 1) * BLK, sl] = dk_parts[j + 1]
                    carry_v[j * BLK:(j + 1) * BLK, sl] = dv_parts[j + 1]

        if not whole:
            @pl.when(n == nb)
            def _():
                dk_ref[...] = carry_k[...].astype(BF16)
                dv_ref[...] = carry_v[...].astype(BF16)

    def cur_blk(kind, width=ATTN_W):
        return pl.BlockSpec((None, rows, width), lambda rho, n: (rho, jnp.minimum(n, nb - 1), kind))

    def last_blk(kind):
        return pl.BlockSpec((None, rows, ATTN_W), lambda rho, n: (rho, jnp.clip(n - 1, 0, nb - 1), kind))

    def prev_keys(kind):
        return pl.BlockSpec((None, BLK, ATTN_W),
                            lambda rho, n: (rho, jnp.clip(n * nsub - 1, 0, nb * nsub - 1), kind))

    res = _array((r, L, ATTN_W), BF16)
    return _call(
        body, name=f"attn_bwd_g{group}", grid=(r, 1 if whole else nb + 1),
        out_shape=[res, res, res],
        in_specs=[cur_blk(0), prev_keys(1), cur_blk(1), prev_keys(2), cur_blk(2), cur_blk(0), cur_blk(0, HEAD_PAIR)],
        out_specs=[cur_blk(0), last_blk(0), last_blk(0)],
        scratch_shapes=[pltpu.VMEM((rows, ATTN_W), F32), pltpu.VMEM((rows, ATTN_W), F32)],
        vmem_mib=32, args=(qkv, qkv, qkv, qkv, qkv, dattn, stats), carried=carried)


def _dqkv_token_order(dqkv_groups, tables, tm):
    T = tables[0].shape[0]

    def body(*refs):
        ins = refs[:9]
        cos_ref, s1_ref, s2_ref, o_ref, scr = refs[9:]
        cos, s1, s2 = cos_ref[...], s1_ref[...], s2_ref[...]
        for g, r in enumerate(DILATIONS):
            for kind in range(3):
                src = ins[3 * g + kind]
                if r > 1:
                    _rows_by_token(scr, src, r, tm // r)
                for s in range(N_SLABS):
                    val = scr[s] if r > 1 else src[0, :, s * HEAD_PAIR:(s + 1) * HEAD_PAIR].astype(F32)
                    if kind < 2:
                        val = _rope_transposed(val, cos, s1, s2)
                    if kind == 0:
                        val = val * 0.125
                    at = (3 * kind + g) * ATTN_W + s * HEAD_PAIR
                    o_ref[:, at:at + HEAD_PAIR] = val.astype(BF16)

    specs = []
    for r in DILATIONS:
        specs += [pl.BlockSpec((r, tm // r, ATTN_W), lambda i: (0, i, 0))] * 3
    row = pl.BlockSpec((tm, HEAD_PAIR), lambda i: (i, 0))
    flat = [a for grp in dqkv_groups for a in grp]
    return _pallas(
        body, name="dqkv_token_order", grid=(T // tm,), out_shape=_array((T, QKV_COLS), BF16),
        in_specs=specs + [row] * 3, out_specs=pl.BlockSpec((tm, QKV_COLS), lambda i: (i, 0)),
        scratch_shapes=[_slab_scratch(tm)], compiler_params=_params(("parallel",), 48),
    )(*flat, *tables)


def _row_tile(rows):
    for cand in (320, 256, 176, 128):
        if rows % cand == 0:
            return cand
    return rows


def _pair_sum(grad4, recv, chip, name):
    _, _, rows, cols = grad4.shape
    tr = _row_tile(rows)

    def body(ids_ref, g_ref, r_ref, gown_ref, rown_ref, sum_ref, own_ref):
        sum_ref[...] = (g_ref[...] + r_ref[...]).astype(BF16)

        @pl.when(pl.program_id(1) == 0)
        def _():
            own_ref[...] = gown_ref[...] + rown_ref[...]

    grid_spec = pltpu.PrefetchScalarGridSpec(
        num_scalar_prefetch=1, grid=(rows // tr, 4),
        in_specs=[pl.BlockSpec((None, None, tr, cols), lambda i, q, ids: (q, ids[1], i, 0)),
                  pl.BlockSpec((None, tr, cols), lambda i, q, ids: (q, i, 0)),
                  pl.BlockSpec((None, None, tr, cols), lambda i, q, ids: (ids[0], ids[1], i, 0)),
                  pl.BlockSpec((None, tr, cols), lambda i, q, ids: (ids[0], i, 0))],
        out_specs=[pl.BlockSpec((None, tr, cols), lambda i, q, ids: (q, i, 0)),
                   pl.BlockSpec((tr, cols), lambda i, q, ids: (i, 0))])
    return _pallas(
        body, name=name, grid_spec=grid_spec,
        out_shape=[_array((4, rows, cols), BF16), _array((rows, cols), F32)],
        compiler_params=_params(("arbitrary", "arbitrary"), 32),
    )(chip, grad4, recv, grad4, recv)


def _chip_sum(own, others, name):
    rows, cols = own.shape
    tr = _row_tile(rows)

    def body(own_ref, oth_ref, o_ref):
        total = own_ref[...]
        for j in range(3):
            total = total + oth_ref[j].astype(F32)
        o_ref[...] = total

    blk = pl.BlockSpec((tr, cols), lambda i: (i, 0))
    return _pallas(
        body, name=name, grid=(rows // tr,), out_shape=_array((rows, cols), F32),
        in_specs=[blk, pl.BlockSpec((3, tr, cols), lambda i: (0, i, 0))], out_specs=blk,
        compiler_params=_params(("parallel",), 32),
    )(own, others)


def _adam_math(w, g, m, v):
    m = ADAM_B1 * m + (1.0 - ADAM_B1) * g
    v = ADAM_B2 * v + (1.0 - ADAM_B2) * (g * g)
    m_hat = m / (1.0 - ADAM_B1 ** ADAM_STEP)
    v_hat = v / (1.0 - ADAM_B2 ** ADAM_STEP)
    delta = -ADAM_LR * (m_hat / (jnp.sqrt(v_hat) + ADAM_EPS) + ADAM_WD * w)
    return delta, m, v


def _adamw(w, g, m, v, name):
    rows, cols = w.shape
    tr = _row_tile(rows)

    def body(w_ref, g_ref, m_ref, v_ref, d_ref, nm_ref, nv_ref):
        d_ref[...], nm_ref[...], nv_ref[...] = _adam_math(w_ref[...], g_ref[...], m_ref[...], v_ref[...])

    blk = pl.BlockSpec((tr, cols), lambda i: (i, 0))
    res = _array((rows, cols), F32)
    return _pallas(
        body, name=name, grid=(rows // tr,), out_shape=[res, res, res], in_specs=[blk] * 4, out_specs=[blk] * 3,
        compiler_params=_params(("parallel",), 32),
    )(w, g, m, v)


def _small_update(late_parts, early_parts, w, m, v, loss_rows):
    rows = w.shape[0]

    def total(p_ref):
        n = p_ref.shape[0] // N_DEV
        acc = p_ref[0:n, :]
        for dev in range(1, N_DEV):
            acc = acc + p_ref[dev * n:(dev + 1) * n, :]
        return acc

    def body(late_ref, early_ref, w_ref, m_ref, v_ref, g_ref, d_ref, nm_ref, nv_ref, loss_ref):
        g = jnp.concatenate([total(late_ref), total(early_ref)], axis=0)
        g_ref[...] = g
        d_ref[...], nm_ref[...], nv_ref[...] = _adam_math(w_ref[...], g, m_ref[...], v_ref[...])
        loss_ref[...] = jnp.sum(jnp.sum(g[rows - loss_rows:, :], axis=1, keepdims=True), axis=0, keepdims=True)

    res = jax.ShapeDtypeStruct((rows, HEAD_PAIR), F32)
    return _pallas(
        body, name="small_update", out_shape=[res, res, res, res, jax.ShapeDtypeStruct((1, 1), F32)],
        compiler_params=pltpu.CompilerParams(vmem_limit_bytes=32 * MIB),
    )(late_parts, early_parts, w, m, v)


def kernel(x, positions, norm1_g, w_in, sgu_ln_g, sgu_ln_b, w_spatial, b_spatial, w_proj_attn, w_proj_sgu, w_out, norm2_g, w_ffn_gate, w_ffn_up, w_ffn_down, final_g, loss_target, m_norm1_g, m_w_in, m_sgu_ln_g, m_sgu_ln_b, m_w_spatial, m_b_spatial, m_w_proj_attn, m_w_proj_sgu, m_w_out, m_norm2_g, m_w_ffn_gate, m_w_ffn_up, m_w_ffn_down, m_final_g, v_norm1_g, v_w_in, v_sgu_ln_g, v_sgu_ln_b, v_w_spatial, v_b_spatial, v_w_proj_attn, v_w_proj_sgu, v_w_out, v_norm2_g, v_w_ffn_gate, v_w_ffn_up, v_w_ffn_down, v_final_g):
    T = x.shape[1]
    tm = 512
    xt = x[0]
    target = loss_target[0]
    chip = jnp.stack([2 * lax.axis_index("x") + lax.axis_index("y"), lax.axis_index("c")]).astype(jnp.int32)

    def bf16_rows(w, transpose):
        return (jnp.transpose(w[0]) if transpose else w[0]).astype(BF16)

    w_in_gather = _all_gather([bf16_rows(w_in, True)])
    ffn_in = _all_gather([bf16_rows(w_ffn_gate, True), bf16_rows(w_ffn_up, True)])
    mixers = _all_gather([bf16_rows(w_out, False), bf16_rows(w_proj_attn, True), bf16_rows(w_proj_sgu, True)])
    ffn_out = _all_gather([bf16_rows(w_ffn_down, False)])

    inv_freq = ROPE_THETA ** (-jnp.arange(0, 2 * ROPE_HALF, 2, dtype=F32) / (2 * ROPE_HALF))
    inv_freq_row = jnp.tile(jnp.concatenate([inv_freq, inv_freq, jnp.zeros((48,), F32)]), 2).reshape(1, HEAD_PAIR)
    b_t = jnp.transpose(b_spatial[0])

    (h, *tables), (win_t,) = _norm1_and_rope_tables(xt, norm1_g, positions.reshape(T, 1), inv_freq_row, tm,
                                                    carried=w_in_gather)
    qkv, (wg_t, wu_t) = _qkv_proj(h, win_t, tables, tm, carried=ffn_in)
    (uvg,), (wout, wpa_t, wps_t) = _uv_gate_proj(h, win_t, tm, carried=mixers)
    fwd0, (wd,) = _attn_fwd(qkv[0], 0, carried=ffn_out)
    fwd = [fwd0, _attn_fwd(qkv[1], 1)[0], _attn_fwd(qkv[2], 2)[0]]
    attn, attn_b, lse = _attn_combine([f[0] for f in fwd], [f[1] for f in fwd], tm)
    sgu = _sgu_fwd(uvg, sgu_ln_g, sgu_ln_b, w_spatial[0], b_t, tm)
    pa, ps, merged, h2, x1 = _merge_fwd(attn_b, sgu, wpa_t, wps_t, uvg, wout, xt, norm2_g, tm)
    gate, up, ff = _ffn_fwd(h2, wg_t, wu_t, tm)
    dx2, dx2b, loss_cols, d_final_g = _down_loss(ff, wd, x1, final_g.reshape(1, D_MODEL), target, tm)

    dgate, dup, dx1, dx1b, d_norm2 = _ffn_bwd(dx2, dx2b, wd, wg_t, wu_t, gate, up, x1, norm2_g, tm // 2)
    tk = min(2048, T)
    d_wd = _mm_tn(ff, dx2b, tmm=FF_TILE, tk=tk, name="grad_w_ffn_down")
    d_wg_t = _mm_tn(dgate, h2, tmm=FF_TILE, tk=tk, name="grad_w_ffn_gate")
    d_wu_t = _mm_tn(dup, h2, tmm=FF_TILE, tk=tk, name="grad_w_ffn_up")

    def by_owner(grads):
        return [g.reshape(4, 2, g.shape[0] // N_DEV, g.shape[1]) for g in grads]

    def pair_sums(grads4, from_sibling, names):
        both = [_pair_sum(g4, rv, chip, "grad_pair_sum_" + nm) for g4, rv, nm in zip(grads4, from_sibling, names)]
        return [b[0] for b in both], [b[1] for b in both]

    ffn_names = ["w_ffn_gate", "w_ffn_up", "w_ffn_down"]
    ffn4 = by_owner([d_wg_t, d_wu_t, d_wd])
    (dpa, dps, dgates, dattn, dsgu), ffn_sib = _merge_bwd(dx1b, wout, pa, ps, uvg, wpa_t, wps_t, tm,
                                                          carried=_pair_exchange(ffn4))
    ffn_sums, ffn_own = pair_sums(ffn4, ffn_sib, ffn_names)

    d_wout = _mm_tn(merged, dx1b, tmm=D_MODEL, tk=tk, name="grad_w_out")
    d_wpa_t = _mm_tn(dpa, attn_b, tmm=D_MODEL, tk=tk, name="grad_w_proj_attn")
    d_wps_t = _mm_tn(dps, sgu, tmm=D_MODEL, tk=tk, name="grad_w_proj_sgu")
    mid_names = ["w_proj_attn", "w_proj_sgu", "w_out"]
    mid4 = by_owner([d_wpa_t, d_wps_t, d_wout])
    (duv, d_ws, d_bs_t, d_ln_g, d_ln_b), mid_sib = _sgu_bwd(uvg, dsgu, sgu_ln_g, sgu_ln_b, w_spatial[0], b_t, tm,
                                                           carried=_pair_exchange(mid4))
    mid_sums, mid_own = pair_sums(mid4, mid_sib, mid_names)

    prep = _attn_bwd_prepare(dattn, attn, lse, tm)
    dqkv0, ffn_far = _attn_bwd(qkv[0], prep[0], prep[1], 0, carried=_chip_exchange(ffn_sums))
    dqkv1, mid_far = _attn_bwd(qkv[1], prep[2], prep[3], 1, carried=_chip_exchange(mid_sums))
    def flat(parts):
        return jnp.concatenate([p.reshape(-1) for p in parts]).reshape(-1, HEAD_PAIR)

    early_part = flat([d_ln_g, d_ln_b, d_ws, jnp.transpose(d_bs_t), d_norm2, d_final_g, loss_cols])
    dqkv2, (early_parts,) = _attn_bwd(qkv[2], prep[4], prep[5], 2, carried=_all_gather([early_part]))
    dqkv = _dqkv_token_order([dqkv0, dqkv1, dqkv2], tables, tm)
    d_win_t = _mm_tn(dqkv, h, tmm=1536, tk=tk, name="grad_w_in_qkv", rows_total=IN_COLS)
    d_win_t = _mm_tn(duv, h, tmm=512, tk=tk, name="grad_w_in_uv", into=d_win_t, row_block0=9, rows_total=IN_COLS)
    d_win_t = _mm_tn(dgates, h, tmm=512, tk=tk, name="grad_w_in_gates", into=d_win_t, row_block0=11, rows_total=IN_COLS)
    in4 = by_owner([d_win_t])
    in_sib = _pair_exchange(in4).run_alone("w_in_grad_pair_exchange")
    in_sums, in_own = pair_sums(in4, in_sib, ["w_in"])
    (dx, d_norm1), in_far = _in_bwd(dqkv, duv, dgates, win_t, xt, norm1_g, dx1, tm, carried=_chip_exchange(in_sums))

    names = ["w_in"] + mid_names + ffn_names
    reduced = [_chip_sum(o, f, "grad_total_" + nm)
               for o, f, nm in zip(in_own + mid_own + ffn_own, in_far + mid_far + ffn_far, names)]
    transposed = (True, True, True, False, True, True, False)
    g_big = [jnp.transpose(r) if t else r for r, t in zip(reduced, transposed)]

    small_w = [norm1_g, sgu_ln_g, sgu_ln_b, w_spatial, b_spatial, norm2_g, final_g]
    small_m = [m_norm1_g, m_sgu_ln_g, m_sgu_ln_b, m_w_spatial, m_b_spatial, m_norm2_g, m_final_g]
    small_v = [v_norm1_g, v_sgu_ln_g, v_sgu_ln_b, v_w_spatial, v_b_spatial, v_norm2_g, v_final_g]
    zeros = jnp.zeros((D_MODEL,), F32)
    (late_parts,) = _all_gather([flat([d_norm1])]).run_alone("norm1_grad_all_gather", vmem=True)
    g_s, d_s, nm_s, nv_s, loss = _small_update(late_parts, early_parts, flat(small_w + [zeros]), flat(small_m + [zeros]),
                                               flat(small_v + [zeros]), D_MODEL // HEAD_PAIR)

    def unflat(vec):
        vec = vec.reshape(-1)
        out, at = [], 0
        for wgt in small_w:
            out.append(vec[at:at + wgt.size].reshape(wgt.shape))
            at += wgt.size
        return out

    small = [unflat(a) for a in (g_s, d_s, nm_s, nv_s)]

    big_w = [w_in, w_proj_attn, w_proj_sgu, w_out, w_ffn_gate, w_ffn_up, w_ffn_down]
    big_m = [m_w_in, m_w_proj_attn, m_w_proj_sgu, m_w_out, m_w_ffn_gate, m_w_ffn_up, m_w_ffn_down]
    big_v = [v_w_in, v_w_proj_attn, v_w_proj_sgu, v_w_out, v_w_ffn_gate, v_w_ffn_up, v_w_ffn_down]
    big_out = []
    for wgt, g, mm, vv, nm in zip(big_w, g_big, big_m, big_v, names):
        d, nm_, nv_ = _adamw(wgt[0], g, mm[0], vv[0], "adamw_" + nm)
        big_out.append([a[None] for a in (g, d, nm_, nv_)])

    small_at = {0: 0, 2: 1, 3: 2, 4: 3, 5: 4, 9: 5, 13: 6}
    big_at = {1: 0, 6: 1, 7: 2, 8: 3, 10: 4, 11: 5, 12: 6}
    outs = [loss[0, 0], dx[None]]
    for kind in range(4):
        for idx in range(14):
            outs.append(small[kind][small_at[idx]] if idx in small_at else big_out[big_at[idx]][kind])
    return tuple(outs)
```

```python
import functools
import math

import jax
import jax.numpy as jnp
from jax import lax
from jax.experimental import pallas as pl
from jax.experimental.pallas import tpu as pltpu

F32 = jnp.float32
BF16 = jnp.bfloat16

D_MODEL = 1024
HEAD_PAIR = 128
ATTN_W = 512
DILATIONS = (1, 4, 16)
BLK = 128
ROPE_HALF = 8
ROPE_THETA = 500000.0
SGU_W = 512
QKV_COLS = 4608
IN_COLS = 7680
D_FF = 2816
FF_TILE = 1408
EPS = 1e-6
N_DEV = 8
MASKED = -1e30

ADAM_LR = 0.001
ADAM_B1 = 0.9
ADAM_B2 = 0.999
ADAM_EPS = 1e-08
ADAM_WD = 0.01
ADAM_STEP = 10

MIB = 1024 * 1024
MESH = pl.DeviceIdType.MESH
ANY = pl.BlockSpec(memory_space=pl.ANY)


def _array(shape, dtype):
    return pltpu.HBM(tuple(shape), dtype)


PIN_BYTES = 4 * MIB


def _pin(x):
    if x.size * x.dtype.itemsize < PIN_BYTES:
        return x
    return pltpu.with_memory_space_constraint(x, pltpu.HBM)


def _pallas(body, **kwargs):
    call = pl.pallas_call(body, **kwargs)
    return lambda *args: call(*[_pin(a) for a in args])


def _params(sem, vmem_mib):
    return pltpu.CompilerParams(dimension_semantics=sem, vmem_limit_bytes=vmem_mib * MIB)


def _nt(a, b):
    return lax.dot_general(a, b, (((1,), (1,)), ((), ())), preferred_element_type=F32)


def _nn(a, b):
    return lax.dot_general(a, b, (((1,), (0,)), ((), ())), preferred_element_type=F32)


def _tn(a, b):
    return lax.dot_general(a, b, (((0,), (0,)), ((), ())), preferred_element_type=F32)


class _Exchange:
    def __init__(self, arrays, out_shapes, sem_shapes, phases):
        self.arrays, self.out_shapes, self.sem_shapes, self.phases = list(arrays), out_shapes, sem_shapes, phases

    def run_alone(self, name, vmem=False):
        n_in, n_out = len(self.arrays), len(self.out_shapes)

        def body(*refs):
            start, middle, finish = self.phases(refs[:n_in], refs[n_in:n_in + n_out], refs[n_in + n_out:])
            start()
            middle()
            finish()

        spec = pl.BlockSpec(memory_space=pltpu.VMEM) if vmem else ANY
        shapes = [jax.ShapeDtypeStruct(s.shape, s.dtype) for s in self.out_shapes] if vmem else self.out_shapes
        return (pl.pallas_call if vmem else _pallas)(
            body, name=name, out_shape=shapes, in_specs=[spec] * n_in, out_specs=[spec] * n_out,
            scratch_shapes=self.sem_shapes, compiler_params=pltpu.CompilerParams(vmem_limit_bytes=32 * MIB),
        )(*self.arrays)


def _call(body, *, name, grid, in_specs, out_specs, out_shape, args, vmem_mib, scratch_shapes=(), carried=None):
    n_in, n_out, n_scr = len(in_specs), len(out_specs), len(scratch_shapes)
    sem = ("arbitrary",) * len(grid)
    if carried is None:
        outs = _pallas(
            body, name=name, grid=grid, in_specs=in_specs, out_specs=out_specs, out_shape=out_shape,
            scratch_shapes=list(scratch_shapes), compiler_params=_params(sem, vmem_mib))(*args)
        return list(outs), []
    c_in, c_out = len(carried.arrays), len(carried.out_shapes)
    total = math.prod(grid)

    def full(*refs):
        own_in, car_in = refs[:n_in], refs[n_in:n_in + c_in]
        at = n_in + c_in
        own_out, car_out = refs[at:at + n_out], refs[at + n_out:at + n_out + c_out]
        at += n_out + c_out
        own_scr, sems = refs[at:at + n_scr], refs[at + n_scr:]
        step = pl.program_id(0)
        for axis in range(1, len(grid)):
            step = step * grid[axis] + pl.program_id(axis)
        start, middle, finish = carried.phases(car_in, car_out, sems)
        pl.when(step == 0)(start)
        pl.when(step == (3 * total) // 4)(middle)
        body(*own_in, *own_out, *own_scr)
        pl.when(step == total - 1)(finish)

    outs = _pallas(
        full, name=name, grid=grid, in_specs=list(in_specs) + [ANY] * c_in,
        out_specs=list(out_specs) + [ANY] * c_out, out_shape=list(out_shape) + list(carried.out_shapes),
        scratch_shapes=list(scratch_shapes) + list(carried.sem_shapes),
        compiler_params=_params(sem, vmem_mib))(*args, *carried.arrays)
    return list(outs[:n_out]), list(outs[n_out:])


def _all_gather(shards):
    n = len(shards)

    def phases(ins, outs, sems):
        send_sems, recv_sems, local_sems = sems
        x, y, c = lax.axis_index("x"), lax.axis_index("y"), lax.axis_index("c")
        me, sibling = (x, y, c), (x, y, 1 - c)
        chips = [(1 - x, y), (x, 1 - y), (1 - x, 1 - y)]

        def rows(m, px, py, pc):
            r = ins[m].shape[0]
            return outs[m].at[pl.ds((4 * px + 2 * py + pc) * r, r), :]

        def copy(m, k, block, to, src=None):
            return pltpu.make_async_remote_copy(
                src_ref=rows(m, *block) if src is None else src, dst_ref=rows(m, *block),
                send_sem=send_sems.at[m, k], recv_sem=recv_sems.at[m, k],
                device_id=to, device_id_type=MESH)

        def mine(m):
            return pltpu.make_async_copy(ins[m], rows(m, *me), local_sems.at[m])

        def first(m):
            return [copy(m, 0, me, sibling, src=ins[m])] + [
                copy(m, 1 + j, me, (*chip, c), src=ins[m]) for j, chip in enumerate(chips)]

        def passed(m):
            return [copy(m, 4 + j, (*chip, c), sibling) for j, chip in enumerate(chips)]

        def start():
            for m in range(n):
                mine(m).start()
            for m in range(n):
                for cp in first(m):
                    cp.start()

        def middle():
            for m in range(n):
                for j, chip in enumerate(chips):
                    copy(m, 1 + j, (*chip, c), me).wait_recv()
                    passed(m)[j].start()

        def finish():
            for m in range(n):
                copy(m, 0, sibling, me).wait_recv()
                for j, chip in enumerate(chips):
                    copy(m, 4 + j, (*chip, 1 - c), me).wait_recv()
            for m in range(n):
                for cp in first(m) + passed(m):
                    cp.wait_send()
                mine(m).wait()

        return start, middle, finish

    return _Exchange(
        shards, [_array((N_DEV * s.shape[0], s.shape[1]), s.dtype) for s in shards],
        [pltpu.SemaphoreType.DMA((n, 7)), pltpu.SemaphoreType.DMA((n, 7)), pltpu.SemaphoreType.DMA((n,))], phases)


def _pair_exchange(grads):
    n = len(grads)

    def phases(ins, outs, sems):
        send_sems, recv_sems = sems
        x, y, c = lax.axis_index("x"), lax.axis_index("y"), lax.axis_index("c")

        def copy(m):
            return pltpu.make_async_remote_copy(
                src_ref=ins[m].at[:, 1 - c], dst_ref=outs[m], send_sem=send_sems.at[m], recv_sem=recv_sems.at[m],
                device_id=(x, y, 1 - c), device_id_type=MESH)

        def start():
            for m in range(n):
                copy(m).start()

        def finish():
            for m in range(n):
                copy(m).wait()

        return start, lambda: None, finish

    return _Exchange(grads, [_array((4,) + g.shape[2:], g.dtype) for g in grads],
                     [pltpu.SemaphoreType.DMA((n,)), pltpu.SemaphoreType.DMA((n,))], phases)


def _chip_exchange(pair_sums):
    n = len(pair_sums)

    def phases(ins, outs, sems):
        send_sems, recv_sems = sems
        x, y, c = lax.axis_index("x"), lax.axis_index("y"), lax.axis_index("c")
        chips = [(1 - x, y), (x, 1 - y), (1 - x, 1 - y)]

        def copies():
            return [pltpu.make_async_remote_copy(
                src_ref=ins[m].at[2 * px + py], dst_ref=outs[m].at[j],
                send_sem=send_sems.at[m, j], recv_sem=recv_sems.at[m, j],
                device_id=(px, py, c), device_id_type=MESH)
                for m in range(n) for j, (px, py) in enumerate(chips)]

        def start():
            for cp in copies():
                cp.start()

        def finish():
            for cp in copies():
                cp.wait_recv()
            for cp in copies():
                cp.wait_send()

        return start, lambda: None, finish

    return _Exchange(pair_sums, [_array((3,) + p.shape[1:], p.dtype) for p in pair_sums],
                     [pltpu.SemaphoreType.DMA((n, 3)), pltpu.SemaphoreType.DMA((n, 3))], phases)


N_SLABS = ATTN_W // HEAD_PAIR


def _slab_scratch(tm, n=N_SLABS):
    return pltpu.VMEM((n, tm, HEAD_PAIR), F32)


def _rows_by_residue(dst_ref, slab_ref, r, tr, dtype, n=N_SLABS):
    for rho in range(r):
        for s in range(n):
            dst_ref[rho, :, s * HEAD_PAIR:(s + 1) * HEAD_PAIR] = (
                slab_ref[s, pl.ds(rho, tr, stride=r), :].astype(dtype))


def _rows_by_token(slab_ref, src_ref, r, tr, n=N_SLABS):
    for rho in range(r):
        for s in range(n):
            slab_ref[s, pl.ds(rho, tr, stride=r), :] = (
                src_ref[rho, :, s * HEAD_PAIR:(s + 1) * HEAD_PAIR].astype(F32))


def _norm1_and_rope_tables(x, g, pos_col, inv_freq_row, tm, carried=None):
    T = x.shape[0]

    def body(x_ref, g_ref, pos_ref, invf_ref, h_ref, cos_ref, s1_ref, s2_ref):
        xf = x_ref[...]
        r = lax.rsqrt(jnp.mean(xf * xf, axis=-1, keepdims=True) + EPS)
        h_ref[...] = (xf * r * g_ref[...]).astype(BF16)
        ang = pos_ref[...].astype(F32) * invf_ref[...]
        lane = lax.broadcasted_iota(jnp.int32, (1, HEAD_PAIR), 1) % 64
        cs, sn = jnp.cos(ang), jnp.sin(ang)
        cos_ref[...] = jnp.where(lane < 2 * ROPE_HALF, cs, 1.0)
        s1_ref[...] = jnp.where(lane < ROPE_HALF, -sn, 0.0)
        s2_ref[...] = jnp.where((lane >= ROPE_HALF) & (lane < 2 * ROPE_HALF), sn, 0.0)

    tab = _array((T, HEAD_PAIR), F32)
    row = pl.BlockSpec((tm, HEAD_PAIR), lambda i: (i, 0))
    wide = pl.BlockSpec((tm, D_MODEL), lambda i: (i, 0))
    return _call(
        body, name="norm1_rope_tables", grid=(T // tm,), out_shape=[_array((T, D_MODEL), BF16), tab, tab, tab],
        in_specs=[wide, pl.BlockSpec((1, D_MODEL), lambda i: (0, 0)), pl.BlockSpec((tm, 1), lambda i: (i, 0)),
                  pl.BlockSpec((1, HEAD_PAIR), lambda i: (0, 0))],
        out_specs=[wide, row, row, row], vmem_mib=24, args=(x, g, pos_col, inv_freq_row), carried=carried)


def _rope(y, cos, s1, s2):
    w = y.shape[1]
    rep = w // HEAD_PAIR
    return (y * jnp.tile(cos, (1, rep)) + pltpu.roll(y, w - ROPE_HALF, 1) * jnp.tile(s1, (1, rep))
            + pltpu.roll(y, ROPE_HALF, 1) * jnp.tile(s2, (1, rep)))


def _rope_transposed(dy, cos, s1, s2):
    w = dy.shape[1]
    rep = w // HEAD_PAIR
    return (dy * jnp.tile(cos, (1, rep)) + pltpu.roll(dy * jnp.tile(s1, (1, rep)), ROPE_HALF, 1)
            + pltpu.roll(dy * jnp.tile(s2, (1, rep)), w - ROPE_HALF, 1))


def _resident(shape, block=None):
    at = (0,) * len(shape) if block is None else block
    return pl.BlockSpec(shape, lambda *_: at, pipeline_mode=pl.Buffered(1))


def _in_proj(h, win_t, tables, tm, carried=None):
    T = h.shape[0]
    other = (IN_COLS - QKV_COLS) // ATTN_W

    def body(h_ref, w_ref, cos_ref, s1_ref, s2_ref, o0, o1, o2, uvg_ref, *slabs):
        hv = h_ref[...]
        cos, s1, s2 = cos_ref[...], s1_ref[...], s2_ref[...]
        for kind in range(3):
            for g, (o_ref, r) in enumerate(zip((o0, o1, o2), DILATIONS)):
                blk = 3 * kind + g
                if blk < other:
                    rows = slice(QKV_COLS + blk * ATTN_W, QKV_COLS + (blk + 1) * ATTN_W)
                    uvg_ref[:, blk * ATTN_W:(blk + 1) * ATTN_W] = _nt(hv, w_ref[rows, :]).astype(BF16)
                y = _nt(hv, w_ref[blk * ATTN_W:(blk + 1) * ATTN_W, :])
                if kind < 2:
                    y = _rope(y, cos, s1, s2)
                if kind == 0:
                    y = y * 0.125
                cols = slice(kind * ATTN_W, (kind + 1) * ATTN_W)
                if r == 1:
                    o_ref[0, :, cols] = y.astype(BF16)
                    continue
                slab = slabs[blk % len(slabs)]
                for s in range(N_SLABS):
                    slab[s] = y[:, s * HEAD_PAIR:(s + 1) * HEAD_PAIR]
                for rho in range(r):
                    for s in range(N_SLABS):
                        at = kind * ATTN_W + s * HEAD_PAIR
                        o_ref[rho, :, at:at + HEAD_PAIR] = slab[s, pl.ds(rho, tm // r, stride=r), :].astype(BF16)

    row = pl.BlockSpec((tm, HEAD_PAIR), lambda i: (i, 0))
    return _call(
        body, name="in_proj", grid=(T // tm,),
        out_shape=[_array((r, T // r, 3 * ATTN_W), BF16) for r in DILATIONS] + [_array((T, IN_COLS - QKV_COLS), BF16)],
        in_specs=[pl.BlockSpec((tm, D_MODEL), lambda i: (i, 0)), _resident((IN_COLS, D_MODEL)), row, row, row],
        out_specs=[pl.BlockSpec((r, tm // r, 3 * ATTN_W), lambda i: (0, i, 0)) for r in DILATIONS]
        + [pl.BlockSpec((tm, IN_COLS - QKV_COLS), lambda i: (i, 0))],
        scratch_shapes=[_slab_scratch(tm)] * 3, vmem_mib=56, args=(h, win_t, *tables), carried=carried)


def _band_mask(n):
    row = lax.broadcasted_iota(jnp.int32, (2 * BLK, 2 * BLK), 0) & (BLK - 1)
    col = lax.broadcasted_iota(jnp.int32, (2 * BLK, 2 * BLK), 1)
    has_prev = (jnp.zeros_like(row) + n) > 0
    return ((col < BLK) & (col >= row) & has_prev) | ((col >= BLK) & (col - BLK <= row))


def _head_lanes():
    lane = lax.broadcasted_iota(jnp.int32, (1, HEAD_PAIR), 1)
    return lane < 64, lane >= 64


def _stack_heads(x, head0, head1):
    zero = jnp.zeros_like(x)
    return jnp.concatenate([jnp.where(head0, x, zero), jnp.where(head1, x, zero)], axis=0)


def _unstack_heads(y, head0):
    return jnp.where(head0, y[:BLK], y[BLK:])


def _per_head(stats, col):
    return jnp.concatenate([stats[:, col:col + 1], stats[:, col + 1:col + 2]], axis=0)


def _attn_sub_blocks(length):
    for n in (8, 4, 2):
        if length % (n * BLK) == 0:
            return n
    return 1


def _attn_fwd(qkv, group, carried=None):
    r, L, _ = qkv.shape
    nsub = _attn_sub_blocks(L)
    rows = nsub * BLK

    def body(q_ref, kp_ref, kc_ref, vp_ref, vc_ref, o_ref, lse_ref):
        head0, head1 = _head_lanes()
        lane = lax.broadcasted_iota(jnp.int32, (1, HEAD_PAIR), 1)
        for j in range(nsub):
            cur = slice(j * BLK, (j + 1) * BLK)
            before = slice((j - 1) * BLK, j * BLK)
            valid = _band_mask(pl.program_id(1) + j)
            stats = jnp.zeros((BLK, HEAD_PAIR), F32)
            for p in range(N_SLABS):
                sl = slice(p * HEAD_PAIR, (p + 1) * HEAD_PAIR)
                k_prev, v_prev = (kp_ref[:, sl], vp_ref[:, sl]) if j == 0 else (kc_ref[before, sl], vc_ref[before, sl])
                k2 = jnp.concatenate([k_prev, kc_ref[cur, sl]], axis=0)
                v2 = jnp.concatenate([v_prev, vc_ref[cur, sl]], axis=0)
                s = jnp.where(valid, _nt(_stack_heads(q_ref[cur, sl], head0, head1), k2), MASKED)
                m = jnp.max(s, axis=1, keepdims=True)
                e = jnp.exp(s - m)
                den = jnp.sum(e, axis=1, keepdims=True)
                o_ref[cur, sl] = _unstack_heads(_nn(e.astype(BF16), v2) / den, head0).astype(BF16)
                lse = m + jnp.log(den)
                stats = jnp.where(lane == 2 * p, lse[:BLK], jnp.where(lane == 2 * p + 1, lse[BLK:], stats))
            lse_ref[cur, :] = stats

    def cur_blk(kind, width=ATTN_W):
        return pl.BlockSpec((None, rows, width), lambda rho, n: (rho, n, kind))

    def prev_blk(kind):
        return pl.BlockSpec((None, BLK, ATTN_W), lambda rho, n: (rho, jnp.maximum(n * nsub - 1, 0), kind))

    return _call(
        body, name=f"attn_fwd_g{group}", grid=(r, L // rows),
        out_shape=[_array((r, L, ATTN_W), BF16), _array((r, L, HEAD_PAIR), F32)],
        in_specs=[cur_blk(0), prev_blk(1), cur_blk(1), prev_blk(2), cur_blk(2)],
        out_specs=[cur_blk(0), cur_blk(0, HEAD_PAIR)],
        vmem_mib=24, args=(qkv, qkv, qkv, qkv, qkv), carried=carried)


def _attn_combine(outs, lses, tm):
    T = outs[0].shape[1]

    def body(o0, l0, o1, l1, o2, l2, attn_ref, attn_b_ref, lse_ref, so1, sl1, so2, sl2):
        for o_in, l_in, so, sl, r in ((o1, l1, so1, sl1, DILATIONS[1]), (o2, l2, so2, sl2, DILATIONS[2])):
            _rows_by_token(so, o_in, r, tm // r)
            _rows_by_token(sl, l_in, r, tm // r, n=1)
        head0, _ = _head_lanes()
        a0, a1, a2 = l0[0], sl1[0], sl2[0]
        mx = jnp.maximum(jnp.maximum(a0, a1), a2)
        e0, e1, e2 = jnp.exp(a0 - mx), jnp.exp(a1 - mx), jnp.exp(a2 - mx)
        tot = e0 + e1 + e2
        lse_ref[...] = mx + jnp.log(tot)
        w0, w1, w2 = e0 / tot, e1 / tot, e2 / tot
        for s in range(N_SLABS):
            lanes = slice(s * HEAD_PAIR, (s + 1) * HEAD_PAIR)

            def lanes_of(w):
                return jnp.where(head0, w[:, 2 * s:2 * s + 1], w[:, 2 * s + 1:2 * s + 2])

            mixed = lanes_of(w0) * o0[0, :, lanes] + lanes_of(w1) * so1[s] + lanes_of(w2) * so2[s]
            attn_ref[:, lanes] = mixed
            attn_b_ref[:, lanes] = mixed.astype(BF16)

    ins, specs = [], []
    for g, r in enumerate(DILATIONS):
        ins += [outs[g], lses[g]]
        specs += [pl.BlockSpec((r, tm // r, ATTN_W), lambda i: (0, i, 0)),
                  pl.BlockSpec((r, tm // r, HEAD_PAIR), lambda i: (0, i, 0))]
    return _pallas(
        body, name="attn_combine", grid=(T // tm,),
        out_shape=[_array((T, ATTN_W), F32), _array((T, ATTN_W), BF16), _array((T, HEAD_PAIR), F32)], in_specs=specs,
        out_specs=[pl.BlockSpec((tm, ATTN_W), lambda i: (i, 0)), pl.BlockSpec((tm, ATTN_W), lambda i: (i, 0)),
                   pl.BlockSpec((tm, HEAD_PAIR), lambda i: (i, 0))],
        scratch_shapes=[_slab_scratch(tm), _slab_scratch(tm, 1), _slab_scratch(tm), _slab_scratch(tm, 1)],
        compiler_params=_params(("parallel",), 32),
    )(*ins)


def _gelu(x):
    return 0.5 * x * (1.0 + lax.erf(x * (1.0 / math.sqrt(2.0))))


def _gelu_grad(x):
    return 0.5 * (1.0 + lax.erf(x * (1.0 / math.sqrt(2.0)))) + x * jnp.exp(-0.5 * x * x) * (1.0 / math.sqrt(2.0 * math.pi))


def _causal():
    row = lax.broadcasted_iota(jnp.int32, (BLK, BLK), 0)
    col = lax.broadcasted_iota(jnp.int32, (BLK, BLK), 1)
    return col <= row


def _bias_lanes(bt):
    grp = lax.broadcasted_iota(jnp.int32, (1, SGU_W), 1) // 64
    out = jnp.zeros((BLK, SGU_W), F32)
    for g in range(8):
        out = jnp.where(grp == g, bt[:, g:g + 1], out)
    return out


def _sgu_normalise(uv, ln_g, ln_b):
    z = _gelu(uv)
    u, v = z[:, :SGU_W], z[:, SGU_W:]
    mu = jnp.mean(v, axis=-1, keepdims=True)
    xc = v - mu
    rstd = lax.rsqrt(jnp.mean(xc * xc, axis=-1, keepdims=True) + EPS)
    xh = xc * rstd
    return u, xh, rstd, xh * ln_g + ln_b


def _sgu_mix(wc_ref, vb, head0):
    chunks = []
    for ch in range(vb.shape[0] // BLK):
        pairs = []
        for p in range(SGU_W // HEAD_PAIR):
            v_pair = vb[ch * BLK:(ch + 1) * BLK, p * HEAD_PAIR:(p + 1) * HEAD_PAIR]
            pairs.append(jnp.where(head0, _nn(wc_ref[2 * p], v_pair), _nn(wc_ref[2 * p + 1], v_pair)))
        chunks.append(jnp.concatenate(pairs, axis=1))
    return jnp.concatenate(chunks, axis=0)


def _sgu_fwd(uvg, ln_g, ln_b, w_s, b_t, tm):
    T = uvg.shape[0]

    def body(uv_ref, g_ref, b_ref, w_ref, bt_ref, o_ref, wc_ref, bias_ref):
        @pl.when(pl.program_id(0) == 0)
        def _():
            causal = _causal()
            for g in range(8):
                wc_ref[g] = jnp.where(causal, w_ref[g], 0.0).astype(BF16)
            bias_ref[...] = _bias_lanes(bt_ref[...])

        u, _, _, vn = _sgu_normalise(uv_ref[...].astype(F32), g_ref[...], b_ref[...])
        mixed = _sgu_mix(wc_ref, vn.astype(BF16), _head_lanes()[0])
        o_ref[...] = (u * (mixed + jnp.tile(bias_ref[...], (tm // BLK, 1)))).astype(BF16)

    vec = pl.BlockSpec((1, SGU_W), lambda i: (0, 0))
    return _pallas(
        body, name="sgu_fwd", grid=(T // tm,), out_shape=_array((T, SGU_W), BF16),
        in_specs=[pl.BlockSpec((tm, 2 * SGU_W), lambda i: (i, 0)), vec, vec,
                  pl.BlockSpec((8, BLK, BLK), lambda i: (0, 0, 0)), pl.BlockSpec((BLK, 8), lambda i: (0, 0))],
        out_specs=pl.BlockSpec((tm, SGU_W), lambda i: (i, 0)),
        scratch_shapes=[pltpu.VMEM((8, BLK, BLK), BF16), pltpu.VMEM((BLK, SGU_W), F32)],
        compiler_params=_params(("arbitrary",), 32),
    )(uvg, ln_g, ln_b, w_s, b_t)


def _merge_fwd(attn, sgu, wpa_t, wps_t, uvg, w_out, x, g2, tm):
    T = attn.shape[0]

    def body(attn_ref, sgu_ref, wpa_ref, wps_ref, ga_ref, gb_ref, wo_ref, x_ref, g_ref,
             pa_ref, ps_ref, m_ref, h_ref, x1_ref):
        pa = _nt(attn_ref[...], wpa_ref[...])
        ps = _nt(sgu_ref[...], wps_ref[...])
        pa_ref[...] = pa.astype(BF16)
        ps_ref[...] = ps.astype(BF16)
        ga, gb = ga_ref[...].astype(F32), gb_ref[...].astype(F32)
        merged = (jax.nn.sigmoid(ga) * pa + jax.nn.sigmoid(gb) * ps).astype(BF16)
        m_ref[...] = merged
        x1 = x_ref[...] + _nn(merged, wo_ref[...])
        x1_ref[...] = x1
        r = lax.rsqrt(jnp.mean(x1 * x1, axis=-1, keepdims=True) + EPS)
        h_ref[...] = (x1 * r * g_ref[...]).astype(BF16)

    half = pl.BlockSpec((tm, ATTN_W), lambda i: (i, 0))
    wide = pl.BlockSpec((tm, D_MODEL), lambda i: (i, 0))
    w = _resident((D_MODEL, ATTN_W))
    res = _array((T, D_MODEL), BF16)
    return _pallas(
        body, name="merge_fwd", grid=(T // tm,), out_shape=[res, res, res, res, _array((T, D_MODEL), F32)],
        in_specs=[half, half, w, w, pl.BlockSpec((tm, D_MODEL), lambda i: (i, 1)),
                  pl.BlockSpec((tm, D_MODEL), lambda i: (i, 2)), _resident((D_MODEL, D_MODEL)), wide,
                  pl.BlockSpec((1, D_MODEL), lambda i: (0, 0))],
        out_specs=[wide] * 5, compiler_params=_params(("parallel",), 48),
    )(attn, sgu, wpa_t, wps_t, uvg, uvg, w_out, x, g2)


def _ffn_fwd(h2, wg_t, wu_t, tm):
    T = h2.shape[0]

    def body(h_ref, wg_ref, wu_ref, gate_ref, up_ref, ff_ref):
        h = h_ref[...]
        for j in range(D_FF // FF_TILE):
            cols = slice(j * FF_TILE, (j + 1) * FF_TILE)
            gate, up = _nt(h, wg_ref[cols, :]), _nt(h, wu_ref[cols, :])
            gate_ref[:, cols] = gate.astype(BF16)
            up_ref[:, cols] = up.astype(BF16)
            ff_ref[:, cols] = (gate * jax.nn.sigmoid(gate) * up).astype(BF16)

    w = _resident((D_FF, D_MODEL))
    o = pl.BlockSpec((tm, D_FF), lambda i: (i, 0))
    res = _array((T, D_FF), BF16)
    return _pallas(
        body, name="ffn_fwd", grid=(T // tm,), out_shape=[res, res, res],
        in_specs=[pl.BlockSpec((tm, D_MODEL), lambda i: (i, 0)), w, w], out_specs=[o, o, o],
        compiler_params=_params(("parallel",), 52),
    )(h2, wg_t, wu_t)


def _down_loss(ff, w_down, x1, final_g, target, tm):
    T = x1.shape[0]

    def body(ff_ref, w_ref, x1_ref, g_ref, t_ref, dx_ref, dxb_ref, loss_ref, dg_ref):
        @pl.when(pl.program_id(0) == 0)
        def _():
            loss_ref[...] = jnp.zeros_like(loss_ref)
            dg_ref[...] = jnp.zeros_like(dg_ref)

        x2 = x1_ref[...] + _nn(ff_ref[...], w_ref[...])
        g = g_ref[...]
        r = lax.rsqrt(jnp.mean(x2 * x2, axis=-1, keepdims=True) + EPS)
        xh = x2 * r
        err = xh * g - t_ref[...]
        loss_ref[...] += jnp.sum(err * err, axis=0, keepdims=True) * (0.5 / D_MODEL)
        dy = err * (1.0 / D_MODEL)
        dg_ref[...] += jnp.sum(dy * xh, axis=0, keepdims=True)
        dxh = dy * g
        dx = r * (dxh - xh * jnp.mean(dxh * xh, axis=-1, keepdims=True))
        dx_ref[...] = dx
        dxb_ref[...] = dx.astype(BF16)

    wide = pl.BlockSpec((tm, D_MODEL), lambda i: (i, 0))
    vec = pl.BlockSpec((1, D_MODEL), lambda i: (0, 0))
    vec_shape = _array((1, D_MODEL), F32)
    return _pallas(
        body, name="down_loss", grid=(T // tm,),
        out_shape=[_array((T, D_MODEL), F32), _array((T, D_MODEL), BF16),
                   vec_shape, vec_shape],
        in_specs=[pl.BlockSpec((tm, D_FF), lambda i: (i, 0)), _resident((D_FF, D_MODEL)), wide, vec, wide],
        out_specs=[wide, wide, vec, vec],
        compiler_params=_params(("arbitrary",), 40),
    )(ff, w_down, x1, final_g, target)


def _rmsnorm_bwd(dh, xin, g, d_res, dg_ref):
    r = lax.rsqrt(jnp.mean(xin * xin, axis=-1, keepdims=True) + EPS)
    xh = xin * r
    dg_ref[...] += jnp.sum(dh * xh, axis=0, keepdims=True)
    dxh = dh * g
    return d_res + r * (dxh - xh * jnp.mean(dxh * xh, axis=-1, keepdims=True))


def _ffn_bwd(dx2, dx2b, w_down, wg_t, wu_t, gate, up, x1, g2, tm):
    T = dx2.shape[0]

    def body(dx_ref, dxb_ref, wd_ref, wg_ref, wu_ref, gate_ref, up_ref, x1_ref, g_ref,
             dgate_ref, dup_ref, dx1_ref, dx1b_ref, dg_ref):
        @pl.when(pl.program_id(0) == 0)
        def _():
            dg_ref[...] = jnp.zeros_like(dg_ref)

        dxb = dxb_ref[...]
        dh = jnp.zeros((tm, D_MODEL), F32)
        for j in range(D_FF // FF_TILE):
            cols = slice(j * FF_TILE, (j + 1) * FF_TILE)
            dff = _nt(dxb, wd_ref[cols, :])
            gate, up = gate_ref[:, cols].astype(F32), up_ref[:, cols].astype(F32)
            sg = jax.nn.sigmoid(gate)
            dgate = (dff * up * sg * (1.0 + gate * (1.0 - sg))).astype(BF16)
            dup = (dff * gate * sg).astype(BF16)
            dgate_ref[:, cols] = dgate
            dup_ref[:, cols] = dup
            dh += _nn(dgate, wg_ref[cols, :]) + _nn(dup, wu_ref[cols, :])
        dx1 = _rmsnorm_bwd(dh, x1_ref[...], g_ref[...], dx_ref[...], dg_ref)
        dx1_ref[...] = dx1
        dx1b_ref[...] = dx1.astype(BF16)

    wide = pl.BlockSpec((tm, D_MODEL), lambda i: (i, 0))
    ffw = pl.BlockSpec((tm, D_FF), lambda i: (i, 0))
    vec = pl.BlockSpec((1, D_MODEL), lambda i: (0, 0))
    w = _resident((D_FF, D_MODEL))
    ff_shape = _array((T, D_FF), BF16)
    return _pallas(
        body, name="ffn_bwd", grid=(T // tm,),
        out_shape=[ff_shape, ff_shape, _array((T, D_MODEL), F32),
                   _array((T, D_MODEL), BF16), _array((1, D_MODEL), F32)],
        in_specs=[wide, wide, w, w, w, ffw, ffw, wide, vec],
        out_specs=[ffw, ffw, wide, wide, vec], compiler_params=_params(("arbitrary",), 56),
    )(dx2, dx2b, w_down, wg_t, wu_t, gate, up, x1, g2)


def _mm_tn(a, b, *, tmm, tk, name, into=None, row_block0=0, rows_total=None):
    T, M = a.shape
    N = b.shape[1]
    rows_total = M if rows_total is None else rows_total

    def body(*refs):
        a_ref, b_ref, o_ref = refs[0], refs[1], refs[-1]

        @pl.when(pl.program_id(1) == 0)
        def _():
            o_ref[...] = jnp.zeros_like(o_ref)

        o_ref[...] += _tn(a_ref[...], b_ref[...])

    ins = [a, b] + ([] if into is None else [into])
    specs = [pl.BlockSpec((tk, tmm), lambda i, k: (k, i)), pl.BlockSpec((tk, N), lambda i, k: (k, 0))]
    return _pallas(
        body, name=name, grid=(M // tmm, T // tk),
        out_shape=_array((rows_total, N), F32),
        in_specs=specs + ([] if into is None else [ANY]),
        out_specs=pl.BlockSpec((tmm, N), lambda i, k: (row_block0 + i, 0)),
        input_output_aliases={} if into is None else {2: 0},
        compiler_params=_params(("parallel", "arbitrary"), 48),
    )(*ins)


def _in_bwd(dqkv, duv, dgates, win_t, x, g1, dx1, tm, carried=None):
    T = x.shape[0]
    uv0, gates0 = QKV_COLS, QKV_COLS + 2 * SGU_W

    def body(dq_ref, du_ref, dgt_ref, w_ref, x_ref, g_ref, d_ref, dx_ref, dg_ref):
        @pl.when(pl.program_id(0) == 0)
        def _():
            dg_ref[...] = jnp.zeros_like(dg_ref)

        dh = (_nn(dq_ref[...], w_ref[:uv0, :]) + _nn(du_ref[...], w_ref[uv0:gates0, :])
              + _nn(dgt_ref[...], w_ref[gates0:, :]))
        dx_ref[...] = _rmsnorm_bwd(dh, x_ref[...], g_ref[...], d_ref[...], dg_ref)

    def cols(n):
        return pl.BlockSpec((tm, n), lambda i: (i, 0))

    wide = cols(D_MODEL)
    vec = pl.BlockSpec((1, D_MODEL), lambda i: (0, 0))
    return _call(
        body, name="in_bwd", grid=(T // tm,),
        out_shape=[_array((T, D_MODEL), F32), _array((1, D_MODEL), F32)],
        in_specs=[cols(QKV_COLS), cols(2 * SGU_W), cols(2 * D_MODEL), _resident((IN_COLS, D_MODEL)), wide, vec, wide],
        out_specs=[wide, vec], vmem_mib=56, args=(dqkv, duv, dgates, win_t, x, g1, dx1), carried=carried)


def _merge_bwd(dx1b, w_out, pa, ps, uvg, wpa_t, wps_t, tm, carried=None):
    T = dx1b.shape[0]

    def body(dx_ref, w_ref, pa_ref, ps_ref, ga_ref, gb_ref, wpa_ref, wps_ref,
             dpa_ref, dps_ref, dg_ref, dattn_ref, dsgu_ref):
        dm = _nt(dx_ref[...], w_ref[...])
        ga, gb = jax.nn.sigmoid(ga_ref[...].astype(F32)), jax.nn.sigmoid(gb_ref[...].astype(F32))
        dpa, dps = (dm * ga).astype(BF16), (dm * gb).astype(BF16)
        dpa_ref[...] = dpa
        dps_ref[...] = dps
        dg_ref[:, :D_MODEL] = (dm * pa_ref[...].astype(F32) * ga * (1.0 - ga)).astype(BF16)
        dg_ref[:, D_MODEL:] = (dm * ps_ref[...].astype(F32) * gb * (1.0 - gb)).astype(BF16)
        dattn_ref[...] = _nn(dpa, wpa_ref[...])
        dsgu_ref[...] = _nn(dps, wps_ref[...])

    wide = pl.BlockSpec((tm, D_MODEL), lambda i: (i, 0))
    half = pl.BlockSpec((tm, ATTN_W), lambda i: (i, 0))
    w = _resident((D_MODEL, ATTN_W))
    res = _array((T, D_MODEL), BF16)
    res_half = _array((T, ATTN_W), F32)
    return _call(
        body, name="merge_bwd", grid=(T // tm,),
        out_shape=[res, res, _array((T, 2 * D_MODEL), BF16), res_half, res_half],
        in_specs=[wide, _resident((D_MODEL, D_MODEL)), wide, wide,
                  pl.BlockSpec((tm, D_MODEL), lambda i: (i, 1)), pl.BlockSpec((tm, D_MODEL), lambda i: (i, 2)), w, w],
        out_specs=[wide, wide, pl.BlockSpec((tm, 2 * D_MODEL), lambda i: (i, 0)), half, half],
        vmem_mib=48, args=(dx1b, w_out, pa, ps, uvg, uvg, wpa_t, wps_t), carried=carried)


def _sgu_bwd(uvg, dsgu, ln_g, ln_b, w_s, b_t, tm, carried=None):
    T = uvg.shape[0]
    nsteps = T // tm

    def body(uv_ref, ds_ref, g_ref, b_ref, w_ref, bt_ref, duv_ref, dw_ref, dbt_ref, dg_ref, db_ref,
             wc_ref, wct_ref, bias_ref, dbias_ref):
        step = pl.program_id(0)
        head0, head1 = _head_lanes()

        @pl.when(step == 0)
        def _():
            causal = _causal()
            for g in range(8):
                wc = jnp.where(causal, w_ref[g], 0.0)
                wc_ref[g] = wc.astype(BF16)
                wct_ref[g] = wc.T.astype(BF16)
            bias_ref[...] = _bias_lanes(bt_ref[...])
            dbias_ref[...] = jnp.zeros_like(dbias_ref)
            dw_ref[...] = jnp.zeros_like(dw_ref)
            dg_ref[...] = jnp.zeros_like(dg_ref)
            db_ref[...] = jnp.zeros_like(db_ref)

        uv = uv_ref[...].astype(F32)
        ln_gain = g_ref[...]
        u, xh, rstd, vn = _sgu_normalise(uv, ln_gain, b_ref[...])
        vb = vn.astype(BF16)
        mixed = _sgu_mix(wc_ref, vb, head0) + jnp.tile(bias_ref[...], (tm // BLK, 1))
        dout = ds_ref[...]
        du = dout * mixed
        dmixed = dout * u
        dmb = dmixed.astype(BF16)
        dvn_chunks = []
        for ch in range(tm // BLK):
            rows = slice(ch * BLK, (ch + 1) * BLK)
            dbias_ref[...] += dmixed[rows]
            pairs = []
            for p in range(SGU_W // HEAD_PAIR):
                lanes = slice(p * HEAD_PAIR, (p + 1) * HEAD_PAIR)
                dm_pair, v_pair = dmb[rows, lanes], vb[rows, lanes]
                acc = jnp.zeros((BLK, HEAD_PAIR), F32)
                for hh, half in enumerate((head0, head1)):
                    dm_h = jnp.where(half, dm_pair, jnp.zeros_like(dm_pair))
                    dw_ref[2 * p + hh] += _nt(dm_h, v_pair)
                    acc += _nn(wct_ref[2 * p + hh], dm_h)
                pairs.append(acc)
            dvn_chunks.append(jnp.concatenate(pairs, axis=1))
        dvn = jnp.concatenate(dvn_chunks, axis=0)
        dg_ref[...] += jnp.sum(dvn * xh, axis=0, keepdims=True)
        db_ref[...] += jnp.sum(dvn, axis=0, keepdims=True)
        dxh = dvn * ln_gain
        dv = rstd * (dxh - jnp.mean(dxh, axis=-1, keepdims=True) - xh * jnp.mean(dxh * xh, axis=-1, keepdims=True))
        dgelu = _gelu_grad(uv)
        duv_ref[:, :SGU_W] = (du * dgelu[:, :SGU_W]).astype(BF16)
        duv_ref[:, SGU_W:] = (dv * dgelu[:, SGU_W:]).astype(BF16)

        @pl.when(step == nsteps - 1)
        def _():
            causal = _causal()
            for g in range(8):
                dw_ref[g] = jnp.where(causal, dw_ref[g], 0.0)
            grp = lax.broadcasted_iota(jnp.int32, (1, SGU_W), 1) // 64
            col = lax.broadcasted_iota(jnp.int32, (1, 8), 1)
            dbias = dbias_ref[...]
            out = jnp.zeros((BLK, 8), F32)
            for g in range(8):
                s = jnp.sum(jnp.where(grp == g, dbias, 0.0), axis=1, keepdims=True)
                out = jnp.where(col == g, s, out)
            dbt_ref[...] = out

    vec = pl.BlockSpec((1, SGU_W), lambda i: (0, 0))
    w3 = pl.BlockSpec((8, BLK, BLK), lambda i: (0, 0, 0))
    bt = pl.BlockSpec((BLK, 8), lambda i: (0, 0))
    return _call(
        body, name="sgu_bwd", grid=(nsteps,),
        out_shape=[_array((T, 2 * SGU_W), BF16), _array((8, BLK, BLK), F32),
                   _array((BLK, 8), F32), _array((1, SGU_W), F32),
                   _array((1, SGU_W), F32)],
        in_specs=[pl.BlockSpec((tm, 2 * SGU_W), lambda i: (i, 0)), pl.BlockSpec((tm, SGU_W), lambda i: (i, 0)),
                  vec, vec, w3, bt],
        out_specs=[pl.BlockSpec((tm, 2 * SGU_W), lambda i: (i, 0)), w3, bt, vec, vec],
        scratch_shapes=[pltpu.VMEM((8, BLK, BLK), BF16), pltpu.VMEM((8, BLK, BLK), BF16),
                        pltpu.VMEM((BLK, SGU_W), F32), pltpu.VMEM((BLK, SGU_W), F32)],
        vmem_mib=40, args=(uvg, dsgu, ln_g, ln_b, w_s, b_t), carried=carried)


D_LANE0 = 8


def _attn_bwd_prepare(dattn, attn, lse, tm):
    T = dattn.shape[0]

    def body(da_ref, at_ref, lse_ref, *outs):
        scr_da, scr_st = outs[-2:]
        outs = outs[:-2]
        head0, _ = _head_lanes()
        lane = lax.broadcasted_iota(jnp.int32, (1, HEAD_PAIR), 1)
        stats = lse_ref[...]
        for s in range(N_SLABS):
            lanes = slice(s * HEAD_PAIR, (s + 1) * HEAD_PAIR)
            da = da_ref[:, lanes]
            pp = da * at_ref[:, lanes]
            d0 = jnp.sum(jnp.where(head0, pp, 0.0), axis=1, keepdims=True)
            d1 = jnp.sum(jnp.where(head0, 0.0, pp), axis=1, keepdims=True)
            stats = jnp.where(lane == D_LANE0 + 2 * s, d0, jnp.where(lane == D_LANE0 + 2 * s + 1, d1, stats))
            scr_da[s] = da
            outs[0][0, :, lanes] = da.astype(BF16)
        scr_st[0] = stats
        outs[1][0] = stats
        for g, r in enumerate(DILATIONS):
            if r > 1:
                _rows_by_residue(outs[2 * g], scr_da, r, tm // r, BF16)
                _rows_by_residue(outs[2 * g + 1], scr_st, r, tm // r, F32, n=1)

    tok = pl.BlockSpec((tm, ATTN_W), lambda i: (i, 0))
    shapes, specs = [], []
    for r in DILATIONS:
        shapes += [_array((r, T // r, ATTN_W), BF16), _array((r, T // r, HEAD_PAIR), F32)]
        specs += [pl.BlockSpec((r, tm // r, ATTN_W), lambda i: (0, i, 0)),
                  pl.BlockSpec((r, tm // r, HEAD_PAIR), lambda i: (0, i, 0))]
    return _pallas(
        body, name="attn_bwd_prepare", grid=(T // tm,), out_shape=shapes,
        in_specs=[tok, tok, pl.BlockSpec((tm, HEAD_PAIR), lambda i: (i, 0))],
        out_specs=specs, scratch_shapes=[_slab_scratch(tm), _slab_scratch(tm, 1)],
        compiler_params=_params(("parallel",), 40),
    )(dattn, attn, lse)


def _attn_bwd(qkv, dattn, stats, group, carried=None):
    r, L, _ = qkv.shape
    nsub = _attn_sub_blocks(L)
    rows = nsub * BLK
    nb = L // rows
    keep = rows - BLK
    whole = nb == 1

    def body(q_ref, kp_ref, kc_ref, vp_ref, vc_ref, da_ref, st_ref, dq_ref, dk_ref, dv_ref, carry_k, carry_v):
        n = pl.program_id(1)

        if not whole:
            @pl.when(n == 0)
            def _():
                carry_k[...] = jnp.zeros_like(carry_k)
                carry_v[...] = jnp.zeros_like(carry_v)

        @pl.when(n < nb)
        def _():
            head0, head1 = _head_lanes()
            for p in range(N_SLABS):
                sl = slice(p * HEAD_PAIR, (p + 1) * HEAD_PAIR)
                dk_parts = [jnp.zeros((BLK, HEAD_PAIR), F32) for _ in range(nsub + 1)]
                dv_parts = [jnp.zeros((BLK, HEAD_PAIR), F32) for _ in range(nsub + 1)]
                for j in range(nsub):
                    cur = slice(j * BLK, (j + 1) * BLK)
                    before = slice((j - 1) * BLK, j * BLK)
                    valid = _band_mask(n + j)
                    st = st_ref[cur, :]
                    k_prev, v_prev = (kp_ref[:, sl], vp_ref[:, sl]) if j == 0 else (kc_ref[before, sl], vc_ref[before, sl])
                    k2 = jnp.concatenate([k_prev, kc_ref[cur, sl]], axis=0)
                    v2 = jnp.concatenate([v_prev, vc_ref[cur, sl]], axis=0)
                    qs = _stack_heads(q_ref[cur, sl], head0, head1)
                    das = _stack_heads(da_ref[cur, sl], head0, head1)
                    prob = jnp.where(valid, jnp.exp(_nt(qs, k2) - _per_head(st, 2 * p)), 0.0)
                    ds = (prob * (_nt(das, v2) - _per_head(st, D_LANE0 + 2 * p))).astype(BF16)
                    dk2 = _tn(ds, qs)
                    dv2 = _tn(prob.astype(BF16), das)
                    dq_ref[cur, sl] = _unstack_heads(_nn(ds, k2), head0).astype(BF16)
                    dk_parts[j] += dk2[:BLK]
                    dk_parts[j + 1] += dk2[BLK:]
                    dv_parts[j] += dv2[:BLK]
                    dv_parts[j + 1] += dv2[BLK:]
                if whole:
                    for j in range(nsub):
                        dk_ref[j * BLK:(j + 1) * BLK, sl] = dk_parts[j + 1].astype(BF16)
                        dv_ref[j * BLK:(j + 1) * BLK, sl] = dv_parts[j + 1].astype(BF16)
                    continue
                if keep:
                    dk_ref[:keep, sl] = carry_k[:keep, sl].astype(BF16)
                    dv_ref[:keep, sl] = carry_v[:keep, sl].astype(BF16)
                dk_ref[keep:, sl] = (carry_k[keep:, sl] + dk_parts[0]).astype(BF16)
                dv_ref[keep:, sl] = (carry_v[keep:, sl] + dv_parts[0]).astype(BF16)
                for j in range(nsub):
                    carry_k[j * BLK:(j + 1) * BLK, sl] = dk_parts[j + 1]
                    carry_v[j * BLK:(j + 1) * BLK, sl] = dv_parts[j + 1]

        if not whole:
            @pl.when(n == nb)
            def _():
                dk_ref[...] = carry_k[...].astype(BF16)
                dv_ref[...] = carry_v[...].astype(BF16)

    def cur_blk(kind, width=ATTN_W):
        return pl.BlockSpec((None, rows, width), lambda rho, n: (rho, jnp.minimum(n, nb - 1), kind))

    def last_blk(kind):
        return pl.BlockSpec((None, rows, ATTN_W), lambda rho, n: (rho, jnp.clip(n - 1, 0, nb - 1), kind))

    def prev_keys(kind):
        return pl.BlockSpec((None, BLK, ATTN_W),
                            lambda rho, n: (rho, jnp.clip(n * nsub - 1, 0, nb * nsub - 1), kind))

    res = _array((r, L, ATTN_W), BF16)
    return _call(
        body, name=f"attn_bwd_g{group}", grid=(r, 1 if whole else nb + 1),
        out_shape=[res, res, res],
        in_specs=[cur_blk(0), prev_keys(1), cur_blk(1), prev_keys(2), cur_blk(2), cur_blk(0), cur_blk(0, HEAD_PAIR)],
        out_specs=[cur_blk(0), last_blk(0), last_blk(0)],
        scratch_shapes=[pltpu.VMEM((rows, ATTN_W), F32), pltpu.VMEM((rows, ATTN_W), F32)],
        vmem_mib=32, args=(qkv, qkv, qkv, qkv, qkv, dattn, stats), carried=carried)


def _dqkv_token_order(dqkv_groups, tables, tm):
    T = tables[0].shape[0]

    def body(*refs):
        ins = refs[:9]
        cos_ref, s1_ref, s2_ref, o_ref, scr = refs[9:]
        cos, s1, s2 = cos_ref[...], s1_ref[...], s2_ref[...]
        for g, r in enumerate(DILATIONS):
            for kind in range(3):
                src = ins[3 * g + kind]
                if r > 1:
                    _rows_by_token(scr, src, r, tm // r)
                for s in range(N_SLABS):
                    val = scr[s] if r > 1 else src[0, :, s * HEAD_PAIR:(s + 1) * HEAD_PAIR].astype(F32)
                    if kind < 2:
                        val = _rope_transposed(val, cos, s1, s2)
                    if kind == 0:
                        val = val * 0.125
                    at = (3 * kind + g) * ATTN_W + s * HEAD_PAIR
                    o_ref[:, at:at + HEAD_PAIR] = val.astype(BF16)

    specs = []
    for r in DILATIONS:
        specs += [pl.BlockSpec((r, tm // r, ATTN_W), lambda i: (0, i, 0))] * 3
    row = pl.BlockSpec((tm, HEAD_PAIR), lambda i: (i, 0))
    flat = [a for grp in dqkv_groups for a in grp]
    return _pallas(
        body, name="dqkv_token_order", grid=(T // tm,), out_shape=_array((T, QKV_COLS), BF16),
        in_specs=specs + [row] * 3, out_specs=pl.BlockSpec((tm, QKV_COLS), lambda i: (i, 0)),
        scratch_shapes=[_slab_scratch(tm)], compiler_params=_params(("parallel",), 48),
    )(*flat, *tables)


def _row_tile(rows):
    for cand in (320, 256, 176, 128):
        if rows % cand == 0:
            return cand
    return rows


def _pair_sum(grad4, recv, chip, name):
    _, _, rows, cols = grad4.shape
    tr = _row_tile(rows)

    def body(ids_ref, g_ref, r_ref, gown_ref, rown_ref, sum_ref, own_ref):
        sum_ref[...] = (g_ref[...] + r_ref[...]).astype(BF16)

        @pl.when(pl.program_id(1) == 0)
        def _():
            own_ref[...] = gown_ref[...] + rown_ref[...]

    grid_spec = pltpu.PrefetchScalarGridSpec(
        num_scalar_prefetch=1, grid=(rows // tr, 4),
        in_specs=[pl.BlockSpec((None, None, tr, cols), lambda i, q, ids: (q, ids[1], i, 0)),
                  pl.BlockSpec((None, tr, cols), lambda i, q, ids: (q, i, 0)),
                  pl.BlockSpec((None, None, tr, cols), lambda i, q, ids: (ids[0], ids[1], i, 0)),
                  pl.BlockSpec((None, tr, cols), lambda i, q, ids: (ids[0], i, 0))],
        out_specs=[pl.BlockSpec((None, tr, cols), lambda i, q, ids: (q, i, 0)),
                   pl.BlockSpec((tr, cols), lambda i, q, ids: (i, 0))])
    return _pallas(
        body, name=name, grid_spec=grid_spec,
        out_shape=[_array((4, rows, cols), BF16), _array((rows, cols), F32)],
        compiler_params=_params(("arbitrary", "arbitrary"), 32),
    )(chip, grad4, recv, grad4, recv)


def _chip_sum(own, others, name):
    rows, cols = own.shape
    tr = _row_tile(rows)

    def body(own_ref, oth_ref, o_ref):
        total = own_ref[...]
        for j in range(3):
            total = total + oth_ref[j].astype(F32)
        o_ref[...] = total

    blk = pl.BlockSpec((tr, cols), lambda i: (i, 0))
    return _pallas(
        body, name=name, grid=(rows // tr,), out_shape=_array((rows, cols), F32),
        in_specs=[blk, pl.BlockSpec((3, tr, cols), lambda i: (0, i, 0))], out_specs=blk,
        compiler_params=_params(("parallel",), 32),
    )(own, others)


def _adam_math(w, g, m, v):
    m = ADAM_B1 * m + (1.0 - ADAM_B1) * g
    v = ADAM_B2 * v + (1.0 - ADAM_B2) * (g * g)
    m_hat = m / (1.0 - ADAM_B1 ** ADAM_STEP)
    v_hat = v / (1.0 - ADAM_B2 ** ADAM_STEP)
    delta = -ADAM_LR * (m_hat / (jnp.sqrt(v_hat) + ADAM_EPS) + ADAM_WD * w)
    return delta, m, v


def _adamw(w, g, m, v, name):
    rows, cols = w.shape
    tr = _row_tile(rows)

    def body(w_ref, g_ref, m_ref, v_ref, d_ref, nm_ref, nv_ref):
        d_ref[...], nm_ref[...], nv_ref[...] = _adam_math(w_ref[...], g_ref[...], m_ref[...], v_ref[...])

    blk = pl.BlockSpec((tr, cols), lambda i: (i, 0))
    res = _array((rows, cols), F32)
    return _pallas(
        body, name=name, grid=(rows // tr,), out_shape=[res, res, res], in_specs=[blk] * 4, out_specs=[blk] * 3,
        compiler_params=_params(("parallel",), 32),
    )(w, g, m, v)


def _small_update(late_parts, early_parts, w, m, v, loss_rows):
    rows = w.shape[0]

    def total(p_ref):
        n = p_ref.shape[0] // N_DEV
        acc = p_ref[0:n, :]
        for dev in range(1, N_DEV):
            acc = acc + p_ref[dev * n:(dev + 1) * n, :]
        return acc

    def body(late_ref, early_ref, w_ref, m_ref, v_ref, g_ref, d_ref, nm_ref, nv_ref, loss_ref):
        g = jnp.concatenate([total(late_ref), total(early_ref)], axis=0)
        g_ref[...] = g
        d_ref[...], nm_ref[...], nv_ref[...] = _adam_math(w_ref[...], g, m_ref[...], v_ref[...])
        loss_ref[...] = jnp.sum(jnp.sum(g[rows - loss_rows:, :], axis=1, keepdims=True), axis=0, keepdims=True)

    res = jax.ShapeDtypeStruct((rows, HEAD_PAIR), F32)
    return _pallas(
        body, name="small_update", out_shape=[res, res, res, res, jax.ShapeDtypeStruct((1, 1), F32)],
        compiler_params=pltpu.CompilerParams(vmem_limit_bytes=32 * MIB),
    )(late_parts, early_parts, w, m, v)


def kernel(x, positions, norm1_g, w_in, sgu_ln_g, sgu_ln_b, w_spatial, b_spatial, w_proj_attn, w_proj_sgu, w_out, norm2_g, w_ffn_gate, w_ffn_up, w_ffn_down, final_g, loss_target, m_norm1_g, m_w_in, m_sgu_ln_g, m_sgu_ln_b, m_w_spatial, m_b_spatial, m_w_proj_attn, m_w_proj_sgu, m_w_out, m_norm2_g, m_w_ffn_gate, m_w_ffn_up, m_w_ffn_down, m_final_g, v_norm1_g, v_w_in, v_sgu_ln_g, v_sgu_ln_b, v_w_spatial, v_b_spatial, v_w_proj_attn, v_w_proj_sgu, v_w_out, v_norm2_g, v_w_ffn_gate, v_w_ffn_up, v_w_ffn_down, v_final_g):
    T = x.shape[1]
    tm = 512
    xt = x[0]
    target = loss_target[0]
    chip = jnp.stack([2 * lax.axis_index("x") + lax.axis_index("y"), lax.axis_index("c")]).astype(jnp.int32)

    def bf16_rows(w, transpose):
        return (jnp.transpose(w[0]) if transpose else w[0]).astype(BF16)

    w_in_gather = _all_gather([bf16_rows(w_in, True)])
    later = _all_gather([bf16_rows(w_ffn_gate, True), bf16_rows(w_ffn_up, True), bf16_rows(w_out, False),
                         bf16_rows(w_proj_attn, True), bf16_rows(w_proj_sgu, True)])
    ffn_out = _all_gather([bf16_rows(w_ffn_down, False)])

    inv_freq = ROPE_THETA ** (-jnp.arange(0, 2 * ROPE_HALF, 2, dtype=F32) / (2 * ROPE_HALF))
    inv_freq_row = jnp.tile(jnp.concatenate([inv_freq, inv_freq, jnp.zeros((48,), F32)]), 2).reshape(1, HEAD_PAIR)
    b_t = jnp.transpose(b_spatial[0])

    (h, *tables), (win_t,) = _norm1_and_rope_tables(xt, norm1_g, positions.reshape(T, 1), inv_freq_row, tm,
                                                    carried=w_in_gather)
    (*qkv, uvg), (wg_t, wu_t, wout, wpa_t, wps_t) = _in_proj(h, win_t, tables, tm, carried=later)
    fwd0, (wd,) = _attn_fwd(qkv[0], 0, carried=ffn_out)
    fwd = [fwd0, _attn_fwd(qkv[1], 1)[0], _attn_fwd(qkv[2], 2)[0]]
    attn, attn_b, lse = _attn_combine([f[0] for f in fwd], [f[1] for f in fwd], tm)
    sgu = _sgu_fwd(uvg, sgu_ln_g, sgu_ln_b, w_spatial[0], b_t, tm)
    pa, ps, merged, h2, x1 = _merge_fwd(attn_b, sgu, wpa_t, wps_t, uvg, wout, xt, norm2_g, tm)
    gate, up, ff = _ffn_fwd(h2, wg_t, wu_t, tm)
    dx2, dx2b, loss_cols, d_final_g = _down_loss(ff, wd, x1, final_g.reshape(1, D_MODEL), target, tm)

    dgate, dup, dx1, dx1b, d_norm2 = _ffn_bwd(dx2, dx2b, wd, wg_t, wu_t, gate, up, x1, norm2_g, tm // 2)
    tk = min(2048, T)
    d_wd = _mm_tn(ff, dx2b, tmm=FF_TILE, tk=tk, name="grad_w_ffn_down")
    d_wg_t = _mm_tn(dgate, h2, tmm=FF_TILE, tk=tk, name="grad_w_ffn_gate")
    d_wu_t = _mm_tn(dup, h2, tmm=FF_TILE, tk=tk, name="grad_w_ffn_up")

    def by_owner(grads):
        return [g.reshape(4, 2, g.shape[0] // N_DEV, g.shape[1]) for g in grads]

    def pair_sums(grads4, from_sibling, names):
        both = [_pair_sum(g4, rv, chip, "grad_pair_sum_" + nm) for g4, rv, nm in zip(grads4, from_sibling, names)]
        return [b[0] for b in both], [b[1] for b in both]

    ffn_names = ["w_ffn_gate", "w_ffn_up", "w_ffn_down"]
    ffn4 = by_owner([d_wg_t, d_wu_t, d_wd])
    (dpa, dps, dgates, dattn, dsgu), ffn_sib = _merge_bwd(dx1b, wout, pa, ps, uvg, wpa_t, wps_t, tm,
                                                          carried=_pair_exchange(ffn4))
    ffn_sums, ffn_own = pair_sums(ffn4, ffn_sib, ffn_names)

    d_wout = _mm_tn(merged, dx1b, tmm=D_MODEL, tk=tk, name="grad_w_out")
    d_wpa_t = _mm_tn(dpa, attn_b, tmm=D_MODEL, tk=tk, name="grad_w_proj_attn")
    d_wps_t = _mm_tn(dps, sgu, tmm=D_MODEL, tk=tk, name="grad_w_proj_sgu")
    mid_names = ["w_proj_attn", "w_proj_sgu", "w_out"]
    mid4 = by_owner([d_wpa_t, d_wps_t, d_wout])
    (duv, d_ws, d_bs_t, d_ln_g, d_ln_b), mid_sib = _sgu_bwd(uvg, dsgu, sgu_ln_g, sgu_ln_b, w_spatial[0], b_t, tm,
                                                           carried=_pair_exchange(mid4))
    mid_sums, mid_own = pair_sums(mid4, mid_sib, mid_names)

    prep = _attn_bwd_prepare(dattn, attn, lse, tm)
    dqkv0, ffn_far = _attn_bwd(qkv[0], prep[0], prep[1], 0, carried=_chip_exchange(ffn_sums))
    dqkv1, mid_far = _attn_bwd(qkv[1], prep[2], prep[3], 1, carried=_chip_exchange(mid_sums))
    def flat(parts):
        return jnp.concatenate([p.reshape(-1) for p in parts]).reshape(-1, HEAD_PAIR)

    early_part = flat([d_ln_g, d_ln_b, d_ws, jnp.transpose(d_bs_t), d_norm2, d_final_g, loss_cols])
    dqkv2, (early_parts,) = _attn_bwd(qkv[2], prep[4], prep[5], 2, carried=_all_gather([early_part]))
    dqkv = _dqkv_token_order([dqkv0, dqkv1, dqkv2], tables, tm)
    d_win_t = _mm_tn(dqkv, h, tmm=1536, tk=tk, name="grad_w_in_qkv", rows_total=IN_COLS)
    d_win_t = _mm_tn(duv, h, tmm=512, tk=tk, name="grad_w_in_uv", into=d_win_t, row_block0=9, rows_total=IN_COLS)
    d_win_t = _mm_tn(dgates, h, tmm=512, tk=tk, name="grad_w_in_gates", into=d_win_t, row_block0=11, rows_total=IN_COLS)
    in4 = by_owner([d_win_t])
    in_sib = _pair_exchange(in4).run_alone("w_in_grad_pair_exchange")
    in_sums, in_own = pair_sums(in4, in_sib, ["w_in"])
    (dx, d_norm1), in_far = _in_bwd(dqkv, duv, dgates, win_t, xt, norm1_g, dx1, tm, carried=_chip_exchange(in_sums))

    names = ["w_in"] + mid_names + ffn_names
    reduced = [_chip_sum(o, f, "grad_total_" + nm)
               for o, f, nm in zip(in_own + mid_own + ffn_own, in_far + mid_far + ffn_far, names)]
    transposed = (True, True, True, False, True, True, False)
    g_big = [jnp.transpose(r) if t else r for r, t in zip(reduced, transposed)]

    small_w = [norm1_g, sgu_ln_g, sgu_ln_b, w_spatial, b_spatial, norm2_g, final_g]
    small_m = [m_norm1_g, m_sgu_ln_g, m_sgu_ln_b, m_w_spatial, m_b_spatial, m_norm2_g, m_final_g]
    small_v = [v_norm1_g, v_sgu_ln_g, v_sgu_ln_b, v_w_spatial, v_b_spatial, v_norm2_g, v_final_g]
    zeros = jnp.zeros((D_MODEL,), F32)
    (late_parts,) = _all_gather([flat([d_norm1])]).run_alone("norm1_grad_all_gather", vmem=True)
    g_s, d_s, nm_s, nv_s, loss = _small_update(late_parts, early_parts, flat(small_w + [zeros]), flat(small_m + [zeros]),
                                               flat(small_v + [zeros]), D_MODEL // HEAD_PAIR)

    def unflat(vec):
        vec = vec.reshape(-1)
        out, at = [], 0
        for wgt in small_w:
            out.append(vec[at:at + wgt.size].reshape(wgt.shape))
            at += wgt.size
        return out

    small = [unflat(a) for a in (g_s, d_s, nm_s, nv_s)]

    big_w = [w_in, w_proj_attn, w_proj_sgu, w_out, w_ffn_gate, w_ffn_up, w_ffn_down]
    big_m = [m_w_in, m_w_proj_attn, m_w_proj_sgu, m_w_out, m_w_ffn_gate, m_w_ffn_up, m_w_ffn_down]
    big_v = [v_w_in, v_w_proj_attn, v_w_proj_sgu, v_w_out, v_w_ffn_gate, v_w_ffn_up, v_w_ffn_down]
    big_out = []
    for wgt, g, mm, vv, nm in zip(big_w, g_big, big_m, big_v, names):
        d, nm_, nv_ = _adamw(wgt[0], g, mm[0], vv[0], "adamw_" + nm)
        big_out.append([a[None] for a in (g, d, nm_, nv_)])

    small_at = {0: 0, 2: 1, 3: 2, 4: 3, 5: 4, 9: 5, 13: 6}
    big_at = {1: 0, 6: 1, 7: 2, 8: 3, 10: 4, 11: 5, 12: 6}
    outs = [loss[0, 0], dx[None]]
    for kind in range(4):
        for idx in range(14):
            outs.append(small[kind][small_at[idx]] if idx in small_at else big_out[big_at[idx]][kind])
    return tuple(outs)
```

```python
import functools
import math

import jax
import jax.numpy as jnp
from jax import lax
from jax.experimental import pallas as pl
from jax.experimental.pallas import tpu as pltpu

F32 = jnp.float32
BF16 = jnp.bfloat16

D_MODEL = 1024
HEAD_PAIR = 128
ATTN_W = 512
DILATIONS = (1, 4, 16)
BLK = 128
ROPE_HALF = 8
ROPE_THETA = 500000.0
SGU_W = 512
QKV_COLS = 4608
IN_COLS = 7680
D_FF = 2816
FF_TILE = 1408
EPS = 1e-6
N_DEV = 8
MASKED = -1e30

ADAM_LR = 0.001
ADAM_B1 = 0.9
ADAM_B2 = 0.999
ADAM_EPS = 1e-08
ADAM_WD = 0.01
ADAM_STEP = 10

MIB = 1024 * 1024
MESH = pl.DeviceIdType.MESH
ANY = pl.BlockSpec(memory_space=pl.ANY)


def _array(shape, dtype):
    return pltpu.HBM(tuple(shape), dtype)


PIN_BYTES = 4 * MIB


def _pin(x):
    if x.size * x.dtype.itemsize < PIN_BYTES:
        return x
    return pltpu.with_memory_space_constraint(x, pltpu.HBM)


def _pallas(body, **kwargs):
    call = pl.pallas_call(body, **kwargs)
    return lambda *args: call(*[_pin(a) for a in args])


def _params(sem, vmem_mib):
    return pltpu.CompilerParams(dimension_semantics=sem, vmem_limit_bytes=vmem_mib * MIB)


def _nt(a, b):
    return lax.dot_general(a, b, (((1,), (1,)), ((), ())), preferred_element_type=F32)


def _nn(a, b):
    return lax.dot_general(a, b, (((1,), (0,)), ((), ())), preferred_element_type=F32)


def _tn(a, b):
    return lax.dot_general(a, b, (((0,), (0,)), ((), ())), preferred_element_type=F32)


class _Exchange:
    def __init__(self, arrays, out_shapes, sem_shapes, phases):
        self.arrays, self.out_shapes, self.sem_shapes, self.phases = list(arrays), out_shapes, sem_shapes, phases

    def run_alone(self, name, vmem=False):
        n_in, n_out = len(self.arrays), len(self.out_shapes)

        def body(*refs):
            start, middle, finish = self.phases(refs[:n_in], refs[n_in:n_in + n_out], refs[n_in + n_out:])
            start()
            middle()
            finish()

        spec = pl.BlockSpec(memory_space=pltpu.VMEM) if vmem else ANY
        shapes = [jax.ShapeDtypeStruct(s.shape, s.dtype) for s in self.out_shapes] if vmem else self.out_shapes
        return (pl.pallas_call if vmem else _pallas)(
            body, name=name, out_shape=shapes, in_specs=[spec] * n_in, out_specs=[spec] * n_out,
            scratch_shapes=self.sem_shapes, compiler_params=pltpu.CompilerParams(vmem_limit_bytes=32 * MIB),
        )(*self.arrays)


def _call(body, *, name, grid, in_specs, out_specs, out_shape, args, vmem_mib, scratch_shapes=(), carried=None,
          aliases=None):
    n_in, n_out, n_scr = len(in_specs), len(out_specs), len(scratch_shapes)
    sem = ("arbitrary",) * len(grid)
    if carried is None:
        outs = _pallas(
            body, name=name, grid=grid, in_specs=in_specs, out_specs=out_specs, out_shape=out_shape,
            scratch_shapes=list(scratch_shapes), input_output_aliases=aliases or {},
            compiler_params=_params(sem, vmem_mib))(*args)
        return list(outs), []
    c_in, c_out = len(carried.arrays), len(carried.out_shapes)
    total = math.prod(grid)

    def full(*refs):
        own_in, car_in = refs[:n_in], refs[n_in:n_in + c_in]
        at = n_in + c_in
        own_out, car_out = refs[at:at + n_out], refs[at + n_out:at + n_out + c_out]
        at += n_out + c_out
        own_scr, sems = refs[at:at + n_scr], refs[at + n_scr:]
        step = pl.program_id(0)
        for axis in range(1, len(grid)):
            step = step * grid[axis] + pl.program_id(axis)
        start, middle, finish = carried.phases(car_in, car_out, sems)
        pl.when(step == 0)(start)
        pl.when(step == (3 * total) // 4)(middle)
        body(*own_in, *own_out, *own_scr)
        pl.when(step == total - 1)(finish)

    outs = _pallas(
        full, name=name, grid=grid, in_specs=list(in_specs) + [ANY] * c_in,
        out_specs=list(out_specs) + [ANY] * c_out, out_shape=list(out_shape) + list(carried.out_shapes),
        scratch_shapes=list(scratch_shapes) + list(carried.sem_shapes), input_output_aliases=aliases or {},
        compiler_params=_params(sem, vmem_mib))(*args, *carried.arrays)
    return list(outs[:n_out]), list(outs[n_out:])


def _all_gather(shards):
    n = len(shards)

    def phases(ins, outs, sems):
        send_sems, recv_sems, local_sems = sems
        x, y, c = lax.axis_index("x"), lax.axis_index("y"), lax.axis_index("c")
        me, sibling = (x, y, c), (x, y, 1 - c)
        chips = [(1 - x, y), (x, 1 - y), (1 - x, 1 - y)]

        def rows(m, px, py, pc):
            r = ins[m].shape[0]
            return outs[m].at[pl.ds((4 * px + 2 * py + pc) * r, r), :]

        def copy(m, k, block, to, src=None):
            return pltpu.make_async_remote_copy(
                src_ref=rows(m, *block) if src is None else src, dst_ref=rows(m, *block),
                send_sem=send_sems.at[m, k], recv_sem=recv_sems.at[m, k],
                device_id=to, device_id_type=MESH)

        def mine(m):
            return pltpu.make_async_copy(ins[m], rows(m, *me), local_sems.at[m])

        def first(m):
            return [copy(m, 0, me, sibling, src=ins[m])] + [
                copy(m, 1 + j, me, (*chip, c), src=ins[m]) for j, chip in enumerate(chips)]

        def passed(m):
            return [copy(m, 4 + j, (*chip, c), sibling) for j, chip in enumerate(chips)]

        def start():
            for m in range(n):
                mine(m).start()
            for m in range(n):
                for cp in first(m):
                    cp.start()

        def middle():
            for m in range(n):
                for j, chip in enumerate(chips):
                    copy(m, 1 + j, (*chip, c), me).wait_recv()
                    passed(m)[j].start()

        def finish():
            for m in range(n):
                copy(m, 0, sibling, me).wait_recv()
                for j, chip in enumerate(chips):
                    copy(m, 4 + j, (*chip, 1 - c), me).wait_recv()
            for m in range(n):
                for cp in first(m) + passed(m):
                    cp.wait_send()
                mine(m).wait()

        return start, middle, finish

    return _Exchange(
        shards, [_array((N_DEV * s.shape[0], s.shape[1]), s.dtype) for s in shards],
        [pltpu.SemaphoreType.DMA((n, 7)), pltpu.SemaphoreType.DMA((n, 7)), pltpu.SemaphoreType.DMA((n,))], phases)


def _pair_exchange(grads):
    n = len(grads)

    def phases(ins, outs, sems):
        send_sems, recv_sems = sems
        x, y, c = lax.axis_index("x"), lax.axis_index("y"), lax.axis_index("c")

        def copy(m):
            return pltpu.make_async_remote_copy(
                src_ref=ins[m].at[:, 1 - c], dst_ref=outs[m], send_sem=send_sems.at[m], recv_sem=recv_sems.at[m],
                device_id=(x, y, 1 - c), device_id_type=MESH)

        def start():
            for m in range(n):
                copy(m).start()

        def finish():
            for m in range(n):
                copy(m).wait()

        return start, lambda: None, finish

    return _Exchange(grads, [_array((4,) + g.shape[2:], g.dtype) for g in grads],
                     [pltpu.SemaphoreType.DMA((n,)), pltpu.SemaphoreType.DMA((n,))], phases)


def _chip_exchange(pair_sums):
    n = len(pair_sums)

    def phases(ins, outs, sems):
        send_sems, recv_sems = sems
        x, y, c = lax.axis_index("x"), lax.axis_index("y"), lax.axis_index("c")
        chips = [(1 - x, y), (x, 1 - y), (1 - x, 1 - y)]

        def copies():
            return [pltpu.make_async_remote_copy(
                src_ref=ins[m].at[2 * px + py], dst_ref=outs[m].at[j],
                send_sem=send_sems.at[m, j], recv_sem=recv_sems.at[m, j],
                device_id=(px, py, c), device_id_type=MESH)
                for m in range(n) for j, (px, py) in enumerate(chips)]

        def start():
            for cp in copies():
                cp.start()

        def finish():
            for cp in copies():
                cp.wait_recv()
            for cp in copies():
                cp.wait_send()

        return start, lambda: None, finish

    return _Exchange(pair_sums, [_array((3,) + p.shape[1:], p.dtype) for p in pair_sums],
                     [pltpu.SemaphoreType.DMA((n, 3)), pltpu.SemaphoreType.DMA((n, 3))], phases)


N_SLABS = ATTN_W // HEAD_PAIR


def _slab_scratch(tm, n=N_SLABS):
    return pltpu.VMEM((n, tm, HEAD_PAIR), F32)


def _rows_by_residue(dst_ref, slab_ref, r, tr, dtype, n=N_SLABS):
    for rho in range(r):
        for s in range(n):
            dst_ref[rho, :, s * HEAD_PAIR:(s + 1) * HEAD_PAIR] = (
                slab_ref[s, pl.ds(rho, tr, stride=r), :].astype(dtype))


def _rows_by_token(slab_ref, src_ref, r, tr, n=N_SLABS):
    for rho in range(r):
        for s in range(n):
            slab_ref[s, pl.ds(rho, tr, stride=r), :] = (
                src_ref[rho, :, s * HEAD_PAIR:(s + 1) * HEAD_PAIR].astype(F32))


def _norm1_and_rope_tables(x, g, pos_col, inv_freq_row, tm, carried=None):
    T = x.shape[0]

    def body(x_ref, g_ref, pos_ref, invf_ref, h_ref, cos_ref, s1_ref, s2_ref):
        xf = x_ref[...]
        r = lax.rsqrt(jnp.mean(xf * xf, axis=-1, keepdims=True) + EPS)
        h_ref[...] = (xf * r * g_ref[...]).astype(BF16)
        ang = pos_ref[...].astype(F32) * invf_ref[...]
        lane = lax.broadcasted_iota(jnp.int32, (1, HEAD_PAIR), 1) % 64
        cs, sn = jnp.cos(ang), jnp.sin(ang)
        cos_ref[...] = jnp.where(lane < 2 * ROPE_HALF, cs, 1.0)
        s1_ref[...] = jnp.where(lane < ROPE_HALF, -sn, 0.0)
        s2_ref[...] = jnp.where((lane >= ROPE_HALF) & (lane < 2 * ROPE_HALF), sn, 0.0)

    tab = _array((T, HEAD_PAIR), F32)
    row = pl.BlockSpec((tm, HEAD_PAIR), lambda i: (i, 0))
    wide = pl.BlockSpec((tm, D_MODEL), lambda i: (i, 0))
    return _call(
        body, name="norm1_rope_tables", grid=(T // tm,), out_shape=[_array((T, D_MODEL), BF16), tab, tab, tab],
        in_specs=[wide, pl.BlockSpec((1, D_MODEL), lambda i: (0, 0)), pl.BlockSpec((tm, 1), lambda i: (i, 0)),
                  pl.BlockSpec((1, HEAD_PAIR), lambda i: (0, 0))],
        out_specs=[wide, row, row, row], vmem_mib=24, args=(x, g, pos_col, inv_freq_row), carried=carried)


def _rope(y, cos, s1, s2):
    w = y.shape[1]
    rep = w // HEAD_PAIR
    return (y * jnp.tile(cos, (1, rep)) + pltpu.roll(y, w - ROPE_HALF, 1) * jnp.tile(s1, (1, rep))
            + pltpu.roll(y, ROPE_HALF, 1) * jnp.tile(s2, (1, rep)))


def _rope_transposed(dy, cos, s1, s2):
    w = dy.shape[1]
    rep = w // HEAD_PAIR
    return (dy * jnp.tile(cos, (1, rep)) + pltpu.roll(dy * jnp.tile(s1, (1, rep)), ROPE_HALF, 1)
            + pltpu.roll(dy * jnp.tile(s2, (1, rep)), w - ROPE_HALF, 1))


def _resident(shape, block=None):
    at = (0,) * len(shape) if block is None else block
    return pl.BlockSpec(shape, lambda *_: at, pipeline_mode=pl.Buffered(1))


def _in_proj(h, win_t, tables, tm, carried=None):
    T = h.shape[0]
    other = (IN_COLS - QKV_COLS) // ATTN_W

    def body(h_ref, w_ref, cos_ref, s1_ref, s2_ref, o0, o1, o2, uvg_ref, *slabs):
        hv = h_ref[...]
        cos, s1, s2 = cos_ref[...], s1_ref[...], s2_ref[...]
        for kind in range(3):
            for g, (o_ref, r) in enumerate(zip((o0, o1, o2), DILATIONS)):
                blk = 3 * kind + g
                if blk < other:
                    rows = slice(QKV_COLS + blk * ATTN_W, QKV_COLS + (blk + 1) * ATTN_W)
                    uvg_ref[:, blk * ATTN_W:(blk + 1) * ATTN_W] = _nt(hv, w_ref[rows, :]).astype(BF16)
                y = _nt(hv, w_ref[blk * ATTN_W:(blk + 1) * ATTN_W, :])
                if kind < 2:
                    y = _rope(y, cos, s1, s2)
                if kind == 0:
                    y = y * 0.125
                cols = slice(kind * ATTN_W, (kind + 1) * ATTN_W)
                if r == 1:
                    o_ref[0, :, cols] = y.astype(BF16)
                    continue
                slab = slabs[blk % len(slabs)]
                for s in range(N_SLABS):
                    slab[s] = y[:, s * HEAD_PAIR:(s + 1) * HEAD_PAIR]
                for rho in range(r):
                    for s in range(N_SLABS):
                        at = kind * ATTN_W + s * HEAD_PAIR
                        o_ref[rho, :, at:at + HEAD_PAIR] = slab[s, pl.ds(rho, tm // r, stride=r), :].astype(BF16)

    row = pl.BlockSpec((tm, HEAD_PAIR), lambda i: (i, 0))
    return _call(
        body, name="in_proj", grid=(T // tm,),
        out_shape=[_array((r, T // r, 3 * ATTN_W), BF16) for r in DILATIONS] + [_array((T, IN_COLS - QKV_COLS), BF16)],
        in_specs=[pl.BlockSpec((tm, D_MODEL), lambda i: (i, 0)), _resident((IN_COLS, D_MODEL)), row, row, row],
        out_specs=[pl.BlockSpec((r, tm // r, 3 * ATTN_W), lambda i: (0, i, 0)) for r in DILATIONS]
        + [pl.BlockSpec((tm, IN_COLS - QKV_COLS), lambda i: (i, 0))],
        scratch_shapes=[_slab_scratch(tm)] * 3, vmem_mib=56, args=(h, win_t, *tables), carried=carried)


def _band_mask(n):
    row = lax.broadcasted_iota(jnp.int32, (2 * BLK, 2 * BLK), 0) & (BLK - 1)
    col = lax.broadcasted_iota(jnp.int32, (2 * BLK, 2 * BLK), 1)
    has_prev = (jnp.zeros_like(row) + n) > 0
    return ((col < BLK) & (col >= row) & has_prev) | ((col >= BLK) & (col - BLK <= row))


def _head_lanes():
    lane = lax.broadcasted_iota(jnp.int32, (1, HEAD_PAIR), 1)
    return lane < 64, lane >= 64


def _stack_heads(x, head0, head1):
    zero = jnp.zeros_like(x)
    return jnp.concatenate([jnp.where(head0, x, zero), jnp.where(head1, x, zero)], axis=0)


def _unstack_heads(y, head0):
    return jnp.where(head0, y[:BLK], y[BLK:])


def _per_head(stats, col):
    return jnp.concatenate([stats[:, col:col + 1], stats[:, col + 1:col + 2]], axis=0)


def _attn_sub_blocks(length):
    for n in (8, 4, 2):
        if length % (n * BLK) == 0:
            return n
    return 1


def _attn_fwd(qkv, group, carried=None):
    r, L, _ = qkv.shape
    nsub = _attn_sub_blocks(L)
    rows = nsub * BLK

    def body(q_ref, kp_ref, kc_ref, vp_ref, vc_ref, o_ref, lse_ref):
        head0, head1 = _head_lanes()
        lane = lax.broadcasted_iota(jnp.int32, (1, HEAD_PAIR), 1)
        for j in range(nsub):
            cur = slice(j * BLK, (j + 1) * BLK)
            before = slice((j - 1) * BLK, j * BLK)
            valid = _band_mask(pl.program_id(1) + j)
            stats = jnp.zeros((BLK, HEAD_PAIR), F32)
            for p in range(N_SLABS):
                sl = slice(p * HEAD_PAIR, (p + 1) * HEAD_PAIR)
                k_prev, v_prev = (kp_ref[:, sl], vp_ref[:, sl]) if j == 0 else (kc_ref[before, sl], vc_ref[before, sl])
                k2 = jnp.concatenate([k_prev, kc_ref[cur, sl]], axis=0)
                v2 = jnp.concatenate([v_prev, vc_ref[cur, sl]], axis=0)
                s = jnp.where(valid, _nt(_stack_heads(q_ref[cur, sl], head0, head1), k2), MASKED)
                m = jnp.max(s, axis=1, keepdims=True)
                e = jnp.exp(s - m)
                den = jnp.sum(e, axis=1, keepdims=True)
                o_ref[cur, sl] = _unstack_heads(_nn(e.astype(BF16), v2) / den, head0).astype(BF16)
                lse = m + jnp.log(den)
                stats = jnp.where(lane == 2 * p, lse[:BLK], jnp.where(lane == 2 * p + 1, lse[BLK:], stats))
            lse_ref[cur, :] = stats

    def cur_blk(kind, width=ATTN_W):
        return pl.BlockSpec((None, rows, width), lambda rho, n: (rho, n, kind))

    def prev_blk(kind):
        return pl.BlockSpec((None, BLK, ATTN_W), lambda rho, n: (rho, jnp.maximum(n * nsub - 1, 0), kind))

    return _call(
        body, name=f"attn_fwd_g{group}", grid=(r, L // rows),
        out_shape=[_array((r, L, ATTN_W), BF16), _array((r, L, HEAD_PAIR), F32)],
        in_specs=[cur_blk(0), prev_blk(1), cur_blk(1), prev_blk(2), cur_blk(2)],
        out_specs=[cur_blk(0), cur_blk(0, HEAD_PAIR)],
        vmem_mib=24, args=(qkv, qkv, qkv, qkv, qkv), carried=carried)


def _attn_combine(outs, lses, tm):
    T = outs[0].shape[1]

    def body(o0, l0, o1, l1, o2, l2, attn_ref, attn_b_ref, lse_ref, so1, sl1, so2, sl2):
        for o_in, l_in, so, sl, r in ((o1, l1, so1, sl1, DILATIONS[1]), (o2, l2, so2, sl2, DILATIONS[2])):
            _rows_by_token(so, o_in, r, tm // r)
            _rows_by_token(sl, l_in, r, tm // r, n=1)
        head0, _ = _head_lanes()
        a0, a1, a2 = l0[0], sl1[0], sl2[0]
        mx = jnp.maximum(jnp.maximum(a0, a1), a2)
        e0, e1, e2 = jnp.exp(a0 - mx), jnp.exp(a1 - mx), jnp.exp(a2 - mx)
        tot = e0 + e1 + e2
        lse_ref[...] = mx + jnp.log(tot)
        w0, w1, w2 = e0 / tot, e1 / tot, e2 / tot
        for s in range(N_SLABS):
            lanes = slice(s * HEAD_PAIR, (s + 1) * HEAD_PAIR)

            def lanes_of(w):
                return jnp.where(head0, w[:, 2 * s:2 * s + 1], w[:, 2 * s + 1:2 * s + 2])

            mixed = lanes_of(w0) * o0[0, :, lanes] + lanes_of(w1) * so1[s] + lanes_of(w2) * so2[s]
            attn_ref[:, lanes] = mixed
            attn_b_ref[:, lanes] = mixed.astype(BF16)

    ins, specs = [], []
    for g, r in enumerate(DILATIONS):
        ins += [outs[g], lses[g]]
        specs += [pl.BlockSpec((r, tm // r, ATTN_W), lambda i: (0, i, 0)),
                  pl.BlockSpec((r, tm // r, HEAD_PAIR), lambda i: (0, i, 0))]
    return _pallas(
        body, name="attn_combine", grid=(T // tm,),
        out_shape=[_array((T, ATTN_W), F32), _array((T, ATTN_W), BF16), _array((T, HEAD_PAIR), F32)], in_specs=specs,
        out_specs=[pl.BlockSpec((tm, ATTN_W), lambda i: (i, 0)), pl.BlockSpec((tm, ATTN_W), lambda i: (i, 0)),
                   pl.BlockSpec((tm, HEAD_PAIR), lambda i: (i, 0))],
        scratch_shapes=[_slab_scratch(tm), _slab_scratch(tm, 1), _slab_scratch(tm), _slab_scratch(tm, 1)],
        compiler_params=_params(("parallel",), 32),
    )(*ins)


def _gelu(x):
    return 0.5 * x * (1.0 + lax.erf(x * (1.0 / math.sqrt(2.0))))


def _gelu_grad(x):
    return 0.5 * (1.0 + lax.erf(x * (1.0 / math.sqrt(2.0)))) + x * jnp.exp(-0.5 * x * x) * (1.0 / math.sqrt(2.0 * math.pi))


def _causal():
    row = lax.broadcasted_iota(jnp.int32, (BLK, BLK), 0)
    col = lax.broadcasted_iota(jnp.int32, (BLK, BLK), 1)
    return col <= row


def _bias_lanes(bt):
    grp = lax.broadcasted_iota(jnp.int32, (1, SGU_W), 1) // 64
    out = jnp.zeros((BLK, SGU_W), F32)
    for g in range(8):
        out = jnp.where(grp == g, bt[:, g:g + 1], out)
    return out


def _sgu_normalise(uv, ln_g, ln_b):
    z = _gelu(uv)
    u, v = z[:, :SGU_W], z[:, SGU_W:]
    mu = jnp.mean(v, axis=-1, keepdims=True)
    xc = v - mu
    rstd = lax.rsqrt(jnp.mean(xc * xc, axis=-1, keepdims=True) + EPS)
    xh = xc * rstd
    return u, xh, rstd, xh * ln_g + ln_b


def _sgu_mix(wc_ref, vb, head0):
    chunks = []
    for ch in range(vb.shape[0] // BLK):
        pairs = []
        for p in range(SGU_W // HEAD_PAIR):
            v_pair = vb[ch * BLK:(ch + 1) * BLK, p * HEAD_PAIR:(p + 1) * HEAD_PAIR]
            pairs.append(jnp.where(head0, _nn(wc_ref[2 * p], v_pair), _nn(wc_ref[2 * p + 1], v_pair)))
        chunks.append(jnp.concatenate(pairs, axis=1))
    return jnp.concatenate(chunks, axis=0)


def _sgu_fwd(uvg, ln_g, ln_b, w_s, b_t, tm):
    T = uvg.shape[0]

    def body(uv_ref, g_ref, b_ref, w_ref, bt_ref, o_ref, wc_ref, bias_ref):
        @pl.when(pl.program_id(0) == 0)
        def _():
            causal = _causal()
            for g in range(8):
                wc_ref[g] = jnp.where(causal, w_ref[g], 0.0).astype(BF16)
            bias_ref[...] = _bias_lanes(bt_ref[...])

        u, _, _, vn = _sgu_normalise(uv_ref[...].astype(F32), g_ref[...], b_ref[...])
        mixed = _sgu_mix(wc_ref, vn.astype(BF16), _head_lanes()[0])
        o_ref[...] = (u * (mixed + jnp.tile(bias_ref[...], (tm // BLK, 1)))).astype(BF16)

    vec = pl.BlockSpec((1, SGU_W), lambda i: (0, 0))
    return _pallas(
        body, name="sgu_fwd", grid=(T // tm,), out_shape=_array((T, SGU_W), BF16),
        in_specs=[pl.BlockSpec((tm, 2 * SGU_W), lambda i: (i, 0)), vec, vec,
                  pl.BlockSpec((8, BLK, BLK), lambda i: (0, 0, 0)), pl.BlockSpec((BLK, 8), lambda i: (0, 0))],
        out_specs=pl.BlockSpec((tm, SGU_W), lambda i: (i, 0)),
        scratch_shapes=[pltpu.VMEM((8, BLK, BLK), BF16), pltpu.VMEM((BLK, SGU_W), F32)],
        compiler_params=_params(("arbitrary",), 32),
    )(uvg, ln_g, ln_b, w_s, b_t)


def _merge_fwd(attn, sgu, wpa_t, wps_t, uvg, w_out, x, g2, tm):
    T = attn.shape[0]

    def body(attn_ref, sgu_ref, wpa_ref, wps_ref, ga_ref, gb_ref, wo_ref, x_ref, g_ref,
             pa_ref, ps_ref, m_ref, h_ref, x1_ref):
        pa = _nt(attn_ref[...], wpa_ref[...])
        ps = _nt(sgu_ref[...], wps_ref[...])
        pa_ref[...] = pa.astype(BF16)
        ps_ref[...] = ps.astype(BF16)
        ga, gb = ga_ref[...].astype(F32), gb_ref[...].astype(F32)
        merged = (jax.nn.sigmoid(ga) * pa + jax.nn.sigmoid(gb) * ps).astype(BF16)
        m_ref[...] = merged
        x1 = x_ref[...] + _nn(merged, wo_ref[...])
        x1_ref[...] = x1
        r = lax.rsqrt(jnp.mean(x1 * x1, axis=-1, keepdims=True) + EPS)
        h_ref[...] = (x1 * r * g_ref[...]).astype(BF16)

    half = pl.BlockSpec((tm, ATTN_W), lambda i: (i, 0))
    wide = pl.BlockSpec((tm, D_MODEL), lambda i: (i, 0))
    w = _resident((D_MODEL, ATTN_W))
    res = _array((T, D_MODEL), BF16)
    return _pallas(
        body, name="merge_fwd", grid=(T // tm,), out_shape=[res, res, res, res, _array((T, D_MODEL), F32)],
        in_specs=[half, half, w, w, pl.BlockSpec((tm, D_MODEL), lambda i: (i, 1)),
                  pl.BlockSpec((tm, D_MODEL), lambda i: (i, 2)), _resident((D_MODEL, D_MODEL)), wide,
                  pl.BlockSpec((1, D_MODEL), lambda i: (0, 0))],
        out_specs=[wide] * 5, compiler_params=_params(("parallel",), 48),
    )(attn, sgu, wpa_t, wps_t, uvg, uvg, w_out, x, g2)


def _ffn_fwd(h2, wg_t, wu_t, tm):
    T = h2.shape[0]

    def body(h_ref, wg_ref, wu_ref, gate_ref, up_ref, ff_ref):
        h = h_ref[...]
        for j in range(D_FF // FF_TILE):
            cols = slice(j * FF_TILE, (j + 1) * FF_TILE)
            gate, up = _nt(h, wg_ref[cols, :]), _nt(h, wu_ref[cols, :])
            gate_ref[:, cols] = gate.astype(BF16)
            up_ref[:, cols] = up.astype(BF16)
            ff_ref[:, cols] = (gate * jax.nn.sigmoid(gate) * up).astype(BF16)

    w = _resident((D_FF, D_MODEL))
    o = pl.BlockSpec((tm, D_FF), lambda i: (i, 0))
    res = _array((T, D_FF), BF16)
    return _pallas(
        body, name="ffn_fwd", grid=(T // tm,), out_shape=[res, res, res],
        in_specs=[pl.BlockSpec((tm, D_MODEL), lambda i: (i, 0)), w, w], out_specs=[o, o, o],
        compiler_params=_params(("parallel",), 52),
    )(h2, wg_t, wu_t)


def _down_loss(ff, w_down, x1, final_g, target, tm):
    T = x1.shape[0]

    def body(ff_ref, w_ref, x1_ref, g_ref, t_ref, dx_ref, dxb_ref, loss_ref, dg_ref):
        @pl.when(pl.program_id(0) == 0)
        def _():
            loss_ref[...] = jnp.zeros_like(loss_ref)
            dg_ref[...] = jnp.zeros_like(dg_ref)

        x2 = x1_ref[...] + _nn(ff_ref[...], w_ref[...])
        g = g_ref[...]
        r = lax.rsqrt(jnp.mean(x2 * x2, axis=-1, keepdims=True) + EPS)
        xh = x2 * r
        err = xh * g - t_ref[...]
        loss_ref[...] += jnp.sum(err * err, axis=0, keepdims=True) * (0.5 / D_MODEL)
        dy = err * (1.0 / D_MODEL)
        dg_ref[...] += jnp.sum(dy * xh, axis=0, keepdims=True)
        dxh = dy * g
        dx = r * (dxh - xh * jnp.mean(dxh * xh, axis=-1, keepdims=True))
        dx_ref[...] = dx
        dxb_ref[...] = dx.astype(BF16)

    wide = pl.BlockSpec((tm, D_MODEL), lambda i: (i, 0))
    vec = pl.BlockSpec((1, D_MODEL), lambda i: (0, 0))
    vec_shape = _array((1, D_MODEL), F32)
    return _pallas(
        body, name="down_loss", grid=(T // tm,),
        out_shape=[_array((T, D_MODEL), F32), _array((T, D_MODEL), BF16),
                   vec_shape, vec_shape],
        in_specs=[pl.BlockSpec((tm, D_FF), lambda i: (i, 0)), _resident((D_FF, D_MODEL)), wide, vec, wide],
        out_specs=[wide, wide, vec, vec],
        compiler_params=_params(("arbitrary",), 40),
    )(ff, w_down, x1, final_g, target)


def _rmsnorm_bwd(dh, xin, g, d_res, dg_ref):
    r = lax.rsqrt(jnp.mean(xin * xin, axis=-1, keepdims=True) + EPS)
    xh = xin * r
    dg_ref[...] += jnp.sum(dh * xh, axis=0, keepdims=True)
    dxh = dh * g
    return d_res + r * (dxh - xh * jnp.mean(dxh * xh, axis=-1, keepdims=True))


def _ffn_bwd(dx2, dx2b, w_down, wg_t, wu_t, gate, up, x1, g2, tm):
    T = dx2.shape[0]

    def body(dx_ref, dxb_ref, wd_ref, wg_ref, wu_ref, gate_ref, up_ref, x1_ref, g_ref,
             dgate_ref, dup_ref, dx1_ref, dx1b_ref, dg_ref):
        @pl.when(pl.program_id(0) == 0)
        def _():
            dg_ref[...] = jnp.zeros_like(dg_ref)

        dxb = dxb_ref[...]
        dh = jnp.zeros((tm, D_MODEL), F32)
        for j in range(D_FF // FF_TILE):
            cols = slice(j * FF_TILE, (j + 1) * FF_TILE)
            dff = _nt(dxb, wd_ref[cols, :])
            gate, up = gate_ref[:, cols].astype(F32), up_ref[:, cols].astype(F32)
            sg = jax.nn.sigmoid(gate)
            dgate = (dff * up * sg * (1.0 + gate * (1.0 - sg))).astype(BF16)
            dup = (dff * gate * sg).astype(BF16)
            dgate_ref[:, cols] = dgate
            dup_ref[:, cols] = dup
            dh += _nn(dgate, wg_ref[cols, :]) + _nn(dup, wu_ref[cols, :])
        dx1 = _rmsnorm_bwd(dh, x1_ref[...], g_ref[...], dx_ref[...], dg_ref)
        dx1_ref[...] = dx1
        dx1b_ref[...] = dx1.astype(BF16)

    wide = pl.BlockSpec((tm, D_MODEL), lambda i: (i, 0))
    ffw = pl.BlockSpec((tm, D_FF), lambda i: (i, 0))
    vec = pl.BlockSpec((1, D_MODEL), lambda i: (0, 0))
    w = _resident((D_FF, D_MODEL))
    ff_shape = _array((T, D_FF), BF16)
    return _pallas(
        body, name="ffn_bwd", grid=(T // tm,),
        out_shape=[ff_shape, ff_shape, _array((T, D_MODEL), F32),
                   _array((T, D_MODEL), BF16), _array((1, D_MODEL), F32)],
        in_specs=[wide, wide, w, w, w, ffw, ffw, wide, vec],
        out_specs=[ffw, ffw, wide, wide, vec], compiler_params=_params(("arbitrary",), 56),
    )(dx2, dx2b, w_down, wg_t, wu_t, gate, up, x1, g2)


def _mm_tn(a, b, *, tmm, tk, name, into=None, row_block0=0, rows_total=None):
    T, M = a.shape
    N = b.shape[1]
    rows_total = M if rows_total is None else rows_total

    def body(*refs):
        a_ref, b_ref, o_ref = refs[0], refs[1], refs[-1]

        @pl.when(pl.program_id(1) == 0)
        def _():
            o_ref[...] = jnp.zeros_like(o_ref)

        o_ref[...] += _tn(a_ref[...], b_ref[...])

    ins = [a, b] + ([] if into is None else [into])
    specs = [pl.BlockSpec((tk, tmm), lambda i, k: (k, i)), pl.BlockSpec((tk, N), lambda i, k: (k, 0))]
    return _pallas(
        body, name=name, grid=(M // tmm, T // tk),
        out_shape=_array((rows_total, N), F32),
        in_specs=specs + ([] if into is None else [ANY]),
        out_specs=pl.BlockSpec((tmm, N), lambda i, k: (row_block0 + i, 0)),
        input_output_aliases={} if into is None else {2: 0},
        compiler_params=_params(("parallel", "arbitrary"), 48),
    )(*ins)


def _in_bwd(dqkv, dug, win_t, x, g1, dx1, tm, carried=None):
    T = x.shape[0]

    def body(dq_ref, dug_ref, w_ref, x_ref, g_ref, d_ref, dx_ref, dg_ref):
        @pl.when(pl.program_id(0) == 0)
        def _():
            dg_ref[...] = jnp.zeros_like(dg_ref)

        dh = _nn(dq_ref[...], w_ref[:QKV_COLS, :]) + _nn(dug_ref[...], w_ref[QKV_COLS:, :])
        dx_ref[...] = _rmsnorm_bwd(dh, x_ref[...], g_ref[...], d_ref[...], dg_ref)

    def cols(n):
        return pl.BlockSpec((tm, n), lambda i: (i, 0))

    wide = cols(D_MODEL)
    vec = pl.BlockSpec((1, D_MODEL), lambda i: (0, 0))
    return _call(
        body, name="in_bwd", grid=(T // tm,),
        out_shape=[_array((T, D_MODEL), F32), _array((1, D_MODEL), F32)],
        in_specs=[cols(QKV_COLS), cols(IN_COLS - QKV_COLS), _resident((IN_COLS, D_MODEL)), wide, vec, wide],
        out_specs=[wide, vec], vmem_mib=56, args=(dqkv, dug, win_t, x, g1, dx1), carried=carried)


D_LANE0 = 8


def _merge_bwd(dx1b, w_out, pa, ps, uvg, wpa_t, wps_t, attn, lse, tm, carried=None):
    T = dx1b.shape[0]

    def body(dx_ref, w_ref, pa_ref, ps_ref, ga_ref, gb_ref, wpa_ref, wps_ref, at_ref, lse_ref,
             dpa_ref, dps_ref, dug_ref, dsgu_ref, *rest):
        outs, (scr_da, scr_st) = rest[:-2], rest[-2:]
        dm = _nt(dx_ref[...], w_ref[...])
        ga, gb = jax.nn.sigmoid(ga_ref[...].astype(F32)), jax.nn.sigmoid(gb_ref[...].astype(F32))
        dpa, dps = (dm * ga).astype(BF16), (dm * gb).astype(BF16)
        dpa_ref[...] = dpa
        dps_ref[...] = dps
        dug_ref[:, 2 * SGU_W:2 * SGU_W + D_MODEL] = (dm * pa_ref[...].astype(F32) * ga * (1.0 - ga)).astype(BF16)
        dug_ref[:, 2 * SGU_W + D_MODEL:] = (dm * ps_ref[...].astype(F32) * gb * (1.0 - gb)).astype(BF16)
        dsgu_ref[...] = _nn(dps, wps_ref[...])
        dattn = _nn(dpa, wpa_ref[...])

        head0, _ = _head_lanes()
        lane = lax.broadcasted_iota(jnp.int32, (1, HEAD_PAIR), 1)
        stats = lse_ref[...]
        for s in range(N_SLABS):
            lanes = slice(s * HEAD_PAIR, (s + 1) * HEAD_PAIR)
            da = dattn[:, lanes]
            pp = da * at_ref[:, lanes]
            d0 = jnp.sum(jnp.where(head0, pp, 0.0), axis=1, keepdims=True)
            d1 = jnp.sum(jnp.where(head0, 0.0, pp), axis=1, keepdims=True)
            stats = jnp.where(lane == D_LANE0 + 2 * s, d0, jnp.where(lane == D_LANE0 + 2 * s + 1, d1, stats))
            scr_da[s] = da
            outs[0][0, :, lanes] = da.astype(BF16)
        scr_st[0] = stats
        outs[1][0] = stats
        for g, r in enumerate(DILATIONS):
            if r > 1:
                _rows_by_residue(outs[2 * g], scr_da, r, tm // r, BF16)
                _rows_by_residue(outs[2 * g + 1], scr_st, r, tm // r, F32, n=1)

    wide = pl.BlockSpec((tm, D_MODEL), lambda i: (i, 0))
    half = pl.BlockSpec((tm, ATTN_W), lambda i: (i, 0))
    w = _resident((D_MODEL, ATTN_W))
    res = _array((T, D_MODEL), BF16)
    shapes, specs = [], []
    for r in DILATIONS:
        shapes += [_array((r, T // r, ATTN_W), BF16), _array((r, T // r, HEAD_PAIR), F32)]
        specs += [pl.BlockSpec((r, tm // r, ATTN_W), lambda i: (0, i, 0)),
                  pl.BlockSpec((r, tm // r, HEAD_PAIR), lambda i: (0, i, 0))]
    return _call(
        body, name="merge_bwd", grid=(T // tm,),
        out_shape=[res, res, _array((T, IN_COLS - QKV_COLS), BF16), _array((T, ATTN_W), F32)] + shapes,
        in_specs=[wide, _resident((D_MODEL, D_MODEL)), wide, wide,
                  pl.BlockSpec((tm, D_MODEL), lambda i: (i, 1)), pl.BlockSpec((tm, D_MODEL), lambda i: (i, 2)), w, w,
                  half, pl.BlockSpec((tm, HEAD_PAIR), lambda i: (i, 0))],
        out_specs=[wide, wide, pl.BlockSpec((tm, IN_COLS - QKV_COLS), lambda i: (i, 0)), half] + specs,
        scratch_shapes=[_slab_scratch(tm), _slab_scratch(tm, 1)],
        vmem_mib=56, args=(dx1b, w_out, pa, ps, uvg, uvg, wpa_t, wps_t, attn, lse), carried=carried)


def _sgu_bwd(uvg, dsgu, ln_g, ln_b, w_s, b_t, dug, tm, carried=None):
    T = uvg.shape[0]
    nsteps = T // tm

    def body(uv_ref, ds_ref, g_ref, b_ref, w_ref, bt_ref, _, duv_ref, dw_ref, dbt_ref, dg_ref, db_ref,
             wc_ref, wct_ref, bias_ref, dbias_ref):
        step = pl.program_id(0)
        head0, head1 = _head_lanes()

        @pl.when(step == 0)
        def _():
            causal = _causal()
            for g in range(8):
                wc = jnp.where(causal, w_ref[g], 0.0)
                wc_ref[g] = wc.astype(BF16)
                wct_ref[g] = wc.T.astype(BF16)
            bias_ref[...] = _bias_lanes(bt_ref[...])
            dbias_ref[...] = jnp.zeros_like(dbias_ref)
            dw_ref[...] = jnp.zeros_like(dw_ref)
            dg_ref[...] = jnp.zeros_like(dg_ref)
            db_ref[...] = jnp.zeros_like(db_ref)

        uv = uv_ref[...].astype(F32)
        ln_gain = g_ref[...]
        u, xh, rstd, vn = _sgu_normalise(uv, ln_gain, b_ref[...])
        vb = vn.astype(BF16)
        mixed = _sgu_mix(wc_ref, vb, head0) + jnp.tile(bias_ref[...], (tm // BLK, 1))
        dout = ds_ref[...]
        du = dout * mixed
        dmixed = dout * u
        dmb = dmixed.astype(BF16)
        dvn_chunks = []
        for ch in range(tm // BLK):
            rows = slice(ch * BLK, (ch + 1) * BLK)
            dbias_ref[...] += dmixed[rows]
            pairs = []
            for p in range(SGU_W // HEAD_PAIR):
                lanes = slice(p * HEAD_PAIR, (p + 1) * HEAD_PAIR)
                dm_pair, v_pair = dmb[rows, lanes], vb[rows, lanes]
                acc = jnp.zeros((BLK, HEAD_PAIR), F32)
                for hh, half in enumerate((head0, head1)):
                    dm_h = jnp.where(half, dm_pair, jnp.zeros_like(dm_pair))
                    dw_ref[2 * p + hh] += _nt(dm_h, v_pair)
                    acc += _nn(wct_ref[2 * p + hh], dm_h)
                pairs.append(acc)
            dvn_chunks.append(jnp.concatenate(pairs, axis=1))
        dvn = jnp.concatenate(dvn_chunks, axis=0)
        dg_ref[...] += jnp.sum(dvn * xh, axis=0, keepdims=True)
        db_ref[...] += jnp.sum(dvn, axis=0, keepdims=True)
        dxh = dvn * ln_gain
        dv = rstd * (dxh - jnp.mean(dxh, axis=-1, keepdims=True) - xh * jnp.mean(dxh * xh, axis=-1, keepdims=True))
        dgelu = _gelu_grad(uv)
        duv_ref[:, :SGU_W] = (du * dgelu[:, :SGU_W]).astype(BF16)
        duv_ref[:, SGU_W:] = (dv * dgelu[:, SGU_W:]).astype(BF16)

        @pl.when(step == nsteps - 1)
        def _():
            causal = _causal()
            for g in range(8):
                dw_ref[g] = jnp.where(causal, dw_ref[g], 0.0)
            grp = lax.broadcasted_iota(jnp.int32, (1, SGU_W), 1) // 64
            col = lax.broadcasted_iota(jnp.int32, (1, 8), 1)
            dbias = dbias_ref[...]
            out = jnp.zeros((BLK, 8), F32)
            for g in range(8):
                s = jnp.sum(jnp.where(grp == g, dbias, 0.0), axis=1, keepdims=True)
                out = jnp.where(col == g, s, out)
            dbt_ref[...] = out

    vec = pl.BlockSpec((1, SGU_W), lambda i: (0, 0))
    w3 = pl.BlockSpec((8, BLK, BLK), lambda i: (0, 0, 0))
    bt = pl.BlockSpec((BLK, 8), lambda i: (0, 0))
    return _call(
        body, name="sgu_bwd", grid=(nsteps,),
        out_shape=[_array(dug.shape, BF16), _array((8, BLK, BLK), F32),
                   _array((BLK, 8), F32), _array((1, SGU_W), F32),
                   _array((1, SGU_W), F32)],
        in_specs=[pl.BlockSpec((tm, 2 * SGU_W), lambda i: (i, 0)), pl.BlockSpec((tm, SGU_W), lambda i: (i, 0)),
                  vec, vec, w3, bt, ANY],
        out_specs=[pl.BlockSpec((tm, 2 * SGU_W), lambda i: (i, 0)), w3, bt, vec, vec],
        scratch_shapes=[pltpu.VMEM((8, BLK, BLK), BF16), pltpu.VMEM((8, BLK, BLK), BF16),
                        pltpu.VMEM((BLK, SGU_W), F32), pltpu.VMEM((BLK, SGU_W), F32)],
        vmem_mib=40, args=(uvg, dsgu, ln_g, ln_b, w_s, b_t, dug), carried=carried, aliases={6: 0})


def _attn_bwd(qkv, dattn, stats, group, carried=None):
    r, L, _ = qkv.shape
    nsub = _attn_sub_blocks(L)
    rows = nsub * BLK
    nb = L // rows
    keep = rows - BLK
    whole = nb == 1

    def body(q_ref, kp_ref, kc_ref, vp_ref, vc_ref, da_ref, st_ref, dq_ref, dk_ref, dv_ref, carry_k, carry_v):
        n = pl.program_id(1)

        if not whole:
            @pl.when(n == 0)
            def _():
                carry_k[...] = jnp.zeros_like(carry_k)
                carry_v[...] = jnp.zeros_like(carry_v)

        @pl.when(n < nb)
        def _():
            head0, head1 = _head_lanes()
            for p in range(N_SLABS):
                sl = slice(p * HEAD_PAIR, (p + 1) * HEAD_PAIR)
                dk_parts = [jnp.zeros((BLK, HEAD_PAIR), F32) for _ in range(nsub + 1)]
                dv_parts = [jnp.zeros((BLK, HEAD_PAIR), F32) for _ in range(nsub + 1)]
                for j in range(nsub):
                    cur = slice(j * BLK, (j + 1) * BLK)
                    before = slice((j - 1) * BLK, j * BLK)
                    valid = _band_mask(n + j)
                    st = st_ref[cur, :]
                    k_prev, v_prev = (kp_ref[:, sl], vp_ref[:, sl]) if j == 0 else (kc_ref[before, sl], vc_ref[before, sl])
                    k2 = jnp.concatenate([k_prev, kc_ref[cur, sl]], axis=0)
                    v2 = jnp.concatenate([v_prev, vc_ref[cur, sl]], axis=0)
                    qs = _stack_heads(q_ref[cur, sl], head0, head1)
                    das = _stack_heads(da_ref[cur, sl], head0, head1)
                    prob = jnp.where(valid, jnp.exp(_nt(qs, k2) - _per_head(st, 2 * p)), 0.0)
                    ds = (prob * (_nt(das, v2) - _per_head(st, D_LANE0 + 2 * p))).astype(BF16)
                    dk2 = _tn(ds, qs)
                    dv2 = _tn(prob.astype(BF16), das)
                    dq_ref[cur, sl] = _unstack_heads(_nn(ds, k2), head0).astype(BF16)
                    dk_parts[j] += dk2[:BLK]
                    dk_parts[j + 1] += dk2[BLK:]
                    dv_parts[j] += dv2[:BLK]
                    dv_parts[j + 1] += dv2[BLK:]
                if whole:
                    for j in range(nsub):
                        dk_ref[j * BLK:(j + 1) * BLK, sl] = dk_parts[j + 1].astype(BF16)
                        dv_ref[j * BLK:(j + 1) * BLK, sl] = dv_parts[j + 1].astype(BF16)
                    continue
                if keep:
                    dk_ref[:keep, sl] = carry_k[:keep, sl].astype(BF16)
                    dv_ref[:keep, sl] = carry_v[:keep, sl].astype(BF16)
                dk_ref[keep:, sl] = (carry_k[keep:, sl] + dk_parts[0]).astype(BF16)
                dv_ref[keep:, sl] = (carry_v[keep:, sl] + dv_parts[0]).astype(BF16)
                for j in range(nsub):
                    carry_k[j * BLK:(j + 1) * BLK, sl] = dk_parts[j + 1]
                    carry_v[j * BLK:(j + 1) * BLK, sl] = dv_parts[j + 1]

        if not whole:
            @pl.when(n == nb)
            def _():
                dk_ref[...] = carry_k[...].astype(BF16)
                dv_ref[...] = carry_v[...].astype(BF16)

    def cur_blk(kind, width=ATTN_W):
        return pl.BlockSpec((None, rows, width), lambda rho, n: (rho, jnp.minimum(n, nb - 1), kind))

    def last_blk(kind):
        return pl.BlockSpec((None, rows, ATTN_W), lambda rho, n: (rho, jnp.clip(n - 1, 0, nb - 1), kind))

    def prev_keys(kind):
        return pl.BlockSpec((None, BLK, ATTN_W),
                            lambda rho, n: (rho, jnp.clip(n * nsub - 1, 0, nb * nsub - 1), kind))

    res = _array((r, L, ATTN_W), BF16)
    return _call(
        body, name=f"attn_bwd_g{group}", grid=(r, 1 if whole else nb + 1),
        out_shape=[res, res, res],
        in_specs=[cur_blk(0), prev_keys(1), cur_blk(1), prev_keys(2), cur_blk(2), cur_blk(0), cur_blk(0, HEAD_PAIR)],
        out_specs=[cur_blk(0), last_blk(0), last_blk(0)],
        scratch_shapes=[pltpu.VMEM((rows, ATTN_W), F32), pltpu.VMEM((rows, ATTN_W), F32)],
        vmem_mib=32, args=(qkv, qkv, qkv, qkv, qkv, dattn, stats), carried=carried)


def _dqkv_token_order(dqkv_groups, tables, tm):
    T = tables[0].shape[0]

    def body(*refs):
        ins = refs[:9]
        cos_ref, s1_ref, s2_ref, o_ref, scr = refs[9:]
        cos, s1, s2 = cos_ref[...], s1_ref[...], s2_ref[...]
        for g, r in enumerate(DILATIONS):
            for kind in range(3):
                src = ins[3 * g + kind]
                if r > 1:
                    _rows_by_token(scr, src, r, tm // r)
                for s in range(N_SLABS):
                    val = scr[s] if r > 1 else src[0, :, s * HEAD_PAIR:(s + 1) * HEAD_PAIR].astype(F32)
                    if kind < 2:
                        val = _rope_transposed(val, cos, s1, s2)
                    if kind == 0:
                        val = val * 0.125
                    at = (3 * kind + g) * ATTN_W + s * HEAD_PAIR
                    o_ref[:, at:at + HEAD_PAIR] = val.astype(BF16)

    specs = []
    for r in DILATIONS:
        specs += [pl.BlockSpec((r, tm // r, ATTN_W), lambda i: (0, i, 0))] * 3
    row = pl.BlockSpec((tm, HEAD_PAIR), lambda i: (i, 0))
    flat = [a for grp in dqkv_groups for a in grp]
    return _pallas(
        body, name="dqkv_token_order", grid=(T // tm,), out_shape=_array((T, QKV_COLS), BF16),
        in_specs=specs + [row] * 3, out_specs=pl.BlockSpec((tm, QKV_COLS), lambda i: (i, 0)),
        scratch_shapes=[_slab_scratch(tm)], compiler_params=_params(("parallel",), 48),
    )(*flat, *tables)


def _row_tile(rows):
    for cand in (320, 256, 176, 128):
        if rows % cand == 0:
            return cand
    return rows


def _pair_sum(grad4, recv, chip, name):
    _, _, rows, cols = grad4.shape
    tr = _row_tile(rows)

    def body(ids_ref, g_ref, r_ref, gown_ref, rown_ref, sum_ref, own_ref):
        sum_ref[...] = (g_ref[...] + r_ref[...]).astype(BF16)

        @pl.when(pl.program_id(1) == 0)
        def _():
            own_ref[...] = gown_ref[...] + rown_ref[...]

    grid_spec = pltpu.PrefetchScalarGridSpec(
        num_scalar_prefetch=1, grid=(rows // tr, 4),
        in_specs=[pl.BlockSpec((None, None, tr, cols), lambda i, q, ids: (q, ids[1], i, 0)),
                  pl.BlockSpec((None, tr, cols), lambda i, q, ids: (q, i, 0)),
                  pl.BlockSpec((None, None, tr, cols), lambda i, q, ids: (ids[0], ids[1], i, 0)),
                  pl.BlockSpec((None, tr, cols), lambda i, q, ids: (ids[0], i, 0))],
        out_specs=[pl.BlockSpec((None, tr, cols), lambda i, q, ids: (q, i, 0)),
                   pl.BlockSpec((tr, cols), lambda i, q, ids: (i, 0))])
    return _pallas(
        body, name=name, grid_spec=grid_spec,
        out_shape=[_array((4, rows, cols), BF16), _array((rows, cols), F32)],
        compiler_params=_params(("arbitrary", "arbitrary"), 32),
    )(chip, grad4, recv, grad4, recv)


def _chip_sum(own, others, name):
    rows, cols = own.shape
    tr = _row_tile(rows)

    def body(own_ref, oth_ref, o_ref):
        total = own_ref[...]
        for j in range(3):
            total = total + oth_ref[j].astype(F32)
        o_ref[...] = total

    blk = pl.BlockSpec((tr, cols), lambda i: (i, 0))
    return _pallas(
        body, name=name, grid=(rows // tr,), out_shape=_array((rows, cols), F32),
        in_specs=[blk, pl.BlockSpec((3, tr, cols), lambda i: (0, i, 0))], out_specs=blk,
        compiler_params=_params(("parallel",), 32),
    )(own, others)


def _adam_math(w, g, m, v):
    m = ADAM_B1 * m + (1.0 - ADAM_B1) * g
    v = ADAM_B2 * v + (1.0 - ADAM_B2) * (g * g)
    m_hat = m / (1.0 - ADAM_B1 ** ADAM_STEP)
    v_hat = v / (1.0 - ADAM_B2 ** ADAM_STEP)
    delta = -ADAM_LR * (m_hat / (jnp.sqrt(v_hat) + ADAM_EPS) + ADAM_WD * w)
    return delta, m, v


def _adamw(w, g, m, v, name):
    rows, cols = w.shape
    tr = _row_tile(rows)

    def body(w_ref, g_ref, m_ref, v_ref, d_ref, nm_ref, nv_ref):
        d_ref[...], nm_ref[...], nv_ref[...] = _adam_math(w_ref[...], g_ref[...], m_ref[...], v_ref[...])

    blk = pl.BlockSpec((tr, cols), lambda i: (i, 0))
    res = _array((rows, cols), F32)
    return _pallas(
        body, name=name, grid=(rows // tr,), out_shape=[res, res, res], in_specs=[blk] * 4, out_specs=[blk] * 3,
        compiler_params=_params(("parallel",), 32),
    )(w, g, m, v)


def _small_update(late_parts, early_parts, w, m, v, loss_rows):
    rows = w.shape[0]

    def total(p_ref):
        n = p_ref.shape[0] // N_DEV
        acc = p_ref[0:n, :]
        for dev in range(1, N_DEV):
            acc = acc + p_ref[dev * n:(dev + 1) * n, :]
        return acc

    def body(late_ref, early_ref, w_ref, m_ref, v_ref, g_ref, d_ref, nm_ref, nv_ref, loss_ref):
        g = jnp.concatenate([total(late_ref), total(early_ref)], axis=0)
        g_ref[...] = g
        d_ref[...], nm_ref[...], nv_ref[...] = _adam_math(w_ref[...], g, m_ref[...], v_ref[...])
        loss_ref[...] = jnp.sum(jnp.sum(g[rows - loss_rows:, :], axis=1, keepdims=True), axis=0, keepdims=True)

    res = jax.ShapeDtypeStruct((rows, HEAD_PAIR), F32)
    return _pallas(
        body, name="small_update", out_shape=[res, res, res, res, jax.ShapeDtypeStruct((1, 1), F32)],
        compiler_params=pltpu.CompilerParams(vmem_limit_bytes=32 * MIB),
    )(late_parts, early_parts, w, m, v)


def kernel(x, positions, norm1_g, w_in, sgu_ln_g, sgu_ln_b, w_spatial, b_spatial, w_proj_attn, w_proj_sgu, w_out, norm2_g, w_ffn_gate, w_ffn_up, w_ffn_down, final_g, loss_target, m_norm1_g, m_w_in, m_sgu_ln_g, m_sgu_ln_b, m_w_spatial, m_b_spatial, m_w_proj_attn, m_w_proj_sgu, m_w_out, m_norm2_g, m_w_ffn_gate, m_w_ffn_up, m_w_ffn_down, m_final_g, v_norm1_g, v_w_in, v_sgu_ln_g, v_sgu_ln_b, v_w_spatial, v_b_spatial, v_w_proj_attn, v_w_proj_sgu, v_w_out, v_norm2_g, v_w_ffn_gate, v_w_ffn_up, v_w_ffn_down, v_final_g):
    T = x.shape[1]
    tm = 512
    xt = x[0]
    target = loss_target[0]
    chip = jnp.stack([2 * lax.axis_index("x") + lax.axis_index("y"), lax.axis_index("c")]).astype(jnp.int32)

    def bf16_rows(w, transpose):
        return (jnp.transpose(w[0]) if transpose else w[0]).astype(BF16)

    w_in_gather = _all_gather([bf16_rows(w_in, True)])
    later = _all_gather([bf16_rows(w_ffn_gate, True), bf16_rows(w_ffn_up, True), bf16_rows(w_out, False),
                         bf16_rows(w_proj_attn, True), bf16_rows(w_proj_sgu, True)])
    ffn_out = _all_gather([bf16_rows(w_ffn_down, False)])

    inv_freq = ROPE_THETA ** (-jnp.arange(0, 2 * ROPE_HALF, 2, dtype=F32) / (2 * ROPE_HALF))
    inv_freq_row = jnp.tile(jnp.concatenate([inv_freq, inv_freq, jnp.zeros((48,), F32)]), 2).reshape(1, HEAD_PAIR)
    b_t = jnp.transpose(b_spatial[0])

    (h, *tables), (win_t,) = _norm1_and_rope_tables(xt, norm1_g, positions.reshape(T, 1), inv_freq_row, tm,
                                                    carried=w_in_gather)
    (*qkv, uvg), (wg_t, wu_t, wout, wpa_t, wps_t) = _in_proj(h, win_t, tables, tm, carried=later)
    fwd0, (wd,) = _attn_fwd(qkv[0], 0, carried=ffn_out)
    fwd = [fwd0, _attn_fwd(qkv[1], 1)[0], _attn_fwd(qkv[2], 2)[0]]
    attn, attn_b, lse = _attn_combine([f[0] for f in fwd], [f[1] for f in fwd], tm)
    sgu = _sgu_fwd(uvg, sgu_ln_g, sgu_ln_b, w_spatial[0], b_t, tm)
    pa, ps, merged, h2, x1 = _merge_fwd(attn_b, sgu, wpa_t, wps_t, uvg, wout, xt, norm2_g, tm)
    gate, up, ff = _ffn_fwd(h2, wg_t, wu_t, tm)
    dx2, dx2b, loss_cols, d_final_g = _down_loss(ff, wd, x1, final_g.reshape(1, D_MODEL), target, tm)

    dgate, dup, dx1, dx1b, d_norm2 = _ffn_bwd(dx2, dx2b, wd, wg_t, wu_t, gate, up, x1, norm2_g, tm // 2)
    tk = min(2048, T)
    d_wd = _mm_tn(ff, dx2b, tmm=FF_TILE, tk=tk, name="grad_w_ffn_down")
    d_wg_t = _mm_tn(dgate, h2, tmm=FF_TILE, tk=tk, name="grad_w_ffn_gate")
    d_wu_t = _mm_tn(dup, h2, tmm=FF_TILE, tk=tk, name="grad_w_ffn_up")

    def by_owner(grads):
        return [g.reshape(4, 2, g.shape[0] // N_DEV, g.shape[1]) for g in grads]

    def pair_sums(grads4, from_sibling, names):
        both = [_pair_sum(g4, rv, chip, "grad_pair_sum_" + nm) for g4, rv, nm in zip(grads4, from_sibling, names)]
        return [b[0] for b in both], [b[1] for b in both]

    ffn_names = ["w_ffn_gate", "w_ffn_up", "w_ffn_down"]
    ffn4 = by_owner([d_wg_t, d_wu_t, d_wd])
    (dpa, dps, dug, dsgu, *prep), ffn_sib = _merge_bwd(dx1b, wout, pa, ps, uvg, wpa_t, wps_t, attn, lse, tm,
                                                       carried=_pair_exchange(ffn4))
    ffn_sums, ffn_own = pair_sums(ffn4, ffn_sib, ffn_names)

    d_wout = _mm_tn(merged, dx1b, tmm=D_MODEL, tk=tk, name="grad_w_out")
    d_wpa_t = _mm_tn(dpa, attn_b, tmm=D_MODEL, tk=tk, name="grad_w_proj_attn")
    d_wps_t = _mm_tn(dps, sgu, tmm=D_MODEL, tk=tk, name="grad_w_proj_sgu")
    mid_names = ["w_proj_attn", "w_proj_sgu", "w_out"]
    mid4 = by_owner([d_wpa_t, d_wps_t, d_wout])
    (dug, d_ws, d_bs_t, d_ln_g, d_ln_b), mid_sib = _sgu_bwd(uvg, dsgu, sgu_ln_g, sgu_ln_b, w_spatial[0], b_t, dug, tm,
                                                           carried=_pair_exchange(mid4))
    mid_sums, mid_own = pair_sums(mid4, mid_sib, mid_names)

    dqkv0, ffn_far = _attn_bwd(qkv[0], prep[0], prep[1], 0, carried=_chip_exchange(ffn_sums))
    dqkv1, mid_far = _attn_bwd(qkv[1], prep[2], prep[3], 1, carried=_chip_exchange(mid_sums))
    def flat(parts):
        return jnp.concatenate([p.reshape(-1) for p in parts]).reshape(-1, HEAD_PAIR)

    early_part = flat([d_ln_g, d_ln_b, d_ws, jnp.transpose(d_bs_t), d_norm2, d_final_g, loss_cols])
    dqkv2, (early_parts,) = _attn_bwd(qkv[2], prep[4], prep[5], 2, carried=_all_gather([early_part]))
    dqkv = _dqkv_token_order([dqkv0, dqkv1, dqkv2], tables, tm)
    d_win_t = _mm_tn(dqkv, h, tmm=1536, tk=tk, name="grad_w_in_qkv", rows_total=IN_COLS)
    d_win_t = _mm_tn(dug, h, tmm=1536, tk=tk, name="grad_w_in_uv_gates", into=d_win_t, row_block0=3, rows_total=IN_COLS)
    in4 = by_owner([d_win_t])
    in_sib = _pair_exchange(in4).run_alone("w_in_grad_pair_exchange")
    in_sums, in_own = pair_sums(in4, in_sib, ["w_in"])
    (dx, d_norm1), in_far = _in_bwd(dqkv, dug, win_t, xt, norm1_g, dx1, tm, carried=_chip_exchange(in_sums))

    names = ["w_in"] + mid_names + ffn_names
    reduced = [_chip_sum(o, f, "grad_total_" + nm)
               for o, f, nm in zip(in_own + mid_own + ffn_own, in_far + mid_far + ffn_far, names)]
    transposed = (True, True, True, False, True, True, False)
    g_big = [jnp.transpose(r) if t else r for r, t in zip(reduced, transposed)]

    small_w = [norm1_g, sgu_ln_g, sgu_ln_b, w_spatial, b_spatial, norm2_g, final_g]
    small_m = [m_norm1_g, m_sgu_ln_g, m_sgu_ln_b, m_w_spatial, m_b_spatial, m_norm2_g, m_final_g]
    small_v = [v_norm1_g, v_sgu_ln_g, v_sgu_ln_b, v_w_spatial, v_b_spatial, v_norm2_g, v_final_g]
    zeros = jnp.zeros((D_MODEL,), F32)
    (late_parts,) = _all_gather([flat([d_norm1])]).run_alone("norm1_grad_all_gather", vmem=True)
    g_s, d_s, nm_s, nv_s, loss = _small_update(late_parts, early_parts, flat(small_w + [zeros]), flat(small_m + [zeros]),
                                               flat(small_v + [zeros]), D_MODEL // HEAD_PAIR)

    def unflat(vec):
        vec = vec.reshape(-1)
        out, at = [], 0
        for wgt in small_w:
            out.append(vec[at:at + wgt.size].reshape(wgt.shape))
            at += wgt.size
        return out

    small = [unflat(a) for a in (g_s, d_s, nm_s, nv_s)]

    big_w = [w_in, w_proj_attn, w_proj_sgu, w_out, w_ffn_gate, w_ffn_up, w_ffn_down]
    big_m = [m_w_in, m_w_proj_attn, m_w_proj_sgu, m_w_out, m_w_ffn_gate, m_w_ffn_up, m_w_ffn_down]
    big_v = [v_w_in, v_w_proj_attn, v_w_proj_sgu, v_w_out, v_w_ffn_gate, v_w_ffn_up, v_w_ffn_down]
    big_out = []
    for wgt, g, mm, vv, nm in zip(big_w, g_big, big_m, big_v, names):
        d, nm_, nv_ = _adamw(wgt[0], g, mm[0], vv[0], "adamw_" + nm)
        big_out.append([a[None] for a in (g, d, nm_, nv_)])

    small_at = {0: 0, 2: 1, 3: 2, 4: 3, 5: 4, 9: 5, 13: 6}
    big_at = {1: 0, 6: 1, 7: 2, 8: 3, 10: 4, 11: 5, 12: 6}
    outs = [loss[0, 0], dx[None]]
    for kind in range(4):
        for idx in range(14):
            outs.append(small[kind][small_at[idx]] if idx in small_at else big_out[big_at[idx]][kind])
    return tuple(outs)
```

```python
import functools
import math

import jax
import jax.numpy as jnp
from jax import lax
from jax.experimental import pallas as pl
from jax.experimental.pallas import tpu as pltpu

F32 = jnp.float32
BF16 = jnp.bfloat16

D_MODEL = 1024
HEAD_PAIR = 128
ATTN_W = 512
DILATIONS = (1, 4, 16)
BLK = 128
ROPE_HALF = 8
ROPE_THETA = 500000.0
SGU_W = 512
QKV_COLS = 4608
IN_COLS = 7680
D_FF = 2816
FF_TILE = 1408
EPS = 1e-6
N_DEV = 8
MASKED = -1e30

ADAM_LR = 0.001
ADAM_B1 = 0.9
ADAM_B2 = 0.999
ADAM_EPS = 1e-08
ADAM_WD = 0.01
ADAM_STEP = 10

MIB = 1024 * 1024
MESH = pl.DeviceIdType.MESH
ANY = pl.BlockSpec(memory_space=pl.ANY)


def _array(shape, dtype):
    return pltpu.HBM(tuple(shape), dtype)


PIN_BYTES = 4 * MIB


def _pin(x):
    if x.size * x.dtype.itemsize < PIN_BYTES:
        return x
    return pltpu.with_memory_space_constraint(x, pltpu.HBM)


def _pallas(body, **kwargs):
    call = pl.pallas_call(body, **kwargs)
    return lambda *args: call(*[_pin(a) for a in args])


def _params(sem, vmem_mib):
    return pltpu.CompilerParams(dimension_semantics=sem, vmem_limit_bytes=vmem_mib * MIB)


def _nt(a, b):
    return lax.dot_general(a, b, (((1,), (1,)), ((), ())), preferred_element_type=F32)


def _nn(a, b):
    return lax.dot_general(a, b, (((1,), (0,)), ((), ())), preferred_element_type=F32)


def _tn(a, b):
    return lax.dot_general(a, b, (((0,), (0,)), ((), ())), preferred_element_type=F32)


class _Exchange:
    def __init__(self, arrays, out_shapes, sem_shapes, phases):
        self.arrays, self.out_shapes, self.sem_shapes, self.phases = list(arrays), out_shapes, sem_shapes, phases

    def run_alone(self, name, vmem=False):
        n_in, n_out = len(self.arrays), len(self.out_shapes)

        def body(*refs):
            start, middle, finish = self.phases(refs[:n_in], refs[n_in:n_in + n_out], refs[n_in + n_out:])
            start()
            middle()
            finish()

        spec = pl.BlockSpec(memory_space=pltpu.VMEM) if vmem else ANY
        shapes = [jax.ShapeDtypeStruct(s.shape, s.dtype) for s in self.out_shapes] if vmem else self.out_shapes
        return (pl.pallas_call if vmem else _pallas)(
            body, name=name, out_shape=shapes, in_specs=[spec] * n_in, out_specs=[spec] * n_out,
            scratch_shapes=self.sem_shapes, compiler_params=pltpu.CompilerParams(vmem_limit_bytes=32 * MIB),
        )(*self.arrays)


def _call(body, *, name, grid, in_specs, out_specs, out_shape, args, vmem_mib, scratch_shapes=(), carried=None,
          aliases=None):
    n_in, n_out, n_scr = len(in_specs), len(out_specs), len(scratch_shapes)
    sem = ("arbitrary",) * len(grid)
    if carried is None:
        outs = _pallas(
            body, name=name, grid=grid, in_specs=in_specs, out_specs=out_specs, out_shape=out_shape,
            scratch_shapes=list(scratch_shapes), input_output_aliases=aliases or {},
            compiler_params=_params(sem, vmem_mib))(*args)
        return list(outs), []
    c_in, c_out = len(carried.arrays), len(carried.out_shapes)
    total = math.prod(grid)

    def full(*refs):
        own_in, car_in = refs[:n_in], refs[n_in:n_in + c_in]
        at = n_in + c_in
        own_out, car_out = refs[at:at + n_out], refs[at + n_out:at + n_out + c_out]
        at += n_out + c_out
        own_scr, sems = refs[at:at + n_scr], refs[at + n_scr:]
        step = pl.program_id(0)
        for axis in range(1, len(grid)):
            step = step * grid[axis] + pl.program_id(axis)
        start, middle, finish = carried.phases(car_in, car_out, sems)
        pl.when(step == 0)(start)
        pl.when(step == (3 * total) // 4)(middle)
        body(*own_in, *own_out, *own_scr)
        pl.when(step == total - 1)(finish)

    outs = _pallas(
        full, name=name, grid=grid, in_specs=list(in_specs) + [ANY] * c_in,
        out_specs=list(out_specs) + [ANY] * c_out, out_shape=list(out_shape) + list(carried.out_shapes),
        scratch_shapes=list(scratch_shapes) + list(carried.sem_shapes), input_output_aliases=aliases or {},
        compiler_params=_params(sem, vmem_mib))(*args, *carried.arrays)
    return list(outs[:n_out]), list(outs[n_out:])


def _all_gather(shards):
    n = len(shards)

    def phases(ins, outs, sems):
        send_sems, recv_sems, local_sems = sems
        x, y, c = lax.axis_index("x"), lax.axis_index("y"), lax.axis_index("c")
        me, sibling = (x, y, c), (x, y, 1 - c)
        chips = [(1 - x, y), (x, 1 - y), (1 - x, 1 - y)]

        def rows(m, px, py, pc):
            r = ins[m].shape[0]
            return outs[m].at[pl.ds((4 * px + 2 * py + pc) * r, r), :]

        def copy(m, k, block, to, src=None):
            return pltpu.make_async_remote_copy(
                src_ref=rows(m, *block) if src is None else src, dst_ref=rows(m, *block),
                send_sem=send_sems.at[m, k], recv_sem=recv_sems.at[m, k],
                device_id=to, device_id_type=MESH)

        def mine(m):
            return pltpu.make_async_copy(ins[m], rows(m, *me), local_sems.at[m])

        def first(m):
            return [copy(m, 0, me, sibling, src=ins[m])] + [
                copy(m, 1 + j, me, (*chip, c), src=ins[m]) for j, chip in enumerate(chips)]

        def passed(m):
            return [copy(m, 4 + j, (*chip, c), sibling) for j, chip in enumerate(chips)]

        def start():
            for m in range(n):
                mine(m).start()
            for m in range(n):
                for cp in first(m):
                    cp.start()

        def middle():
            for m in range(n):
                for j, chip in enumerate(chips):
                    copy(m, 1 + j, (*chip, c), me).wait_recv()
                    passed(m)[j].start()

        def finish():
            for m in range(n):
                copy(m, 0, sibling, me).wait_recv()
                for j, chip in enumerate(chips):
                    copy(m, 4 + j, (*chip, 1 - c), me).wait_recv()
            for m in range(n):
                for cp in first(m) + passed(m):
                    cp.wait_send()
                mine(m).wait()

        return start, middle, finish

    return _Exchange(
        shards, [_array((N_DEV * s.shape[0], s.shape[1]), s.dtype) for s in shards],
        [pltpu.SemaphoreType.DMA((n, 7)), pltpu.SemaphoreType.DMA((n, 7)), pltpu.SemaphoreType.DMA((n,))], phases)


def _pair_exchange(grads):
    n = len(grads)

    def phases(ins, outs, sems):
        send_sems, recv_sems = sems
        x, y, c = lax.axis_index("x"), lax.axis_index("y"), lax.axis_index("c")

        def copy(m):
            return pltpu.make_async_remote_copy(
                src_ref=ins[m].at[:, 1 - c], dst_ref=outs[m], send_sem=send_sems.at[m], recv_sem=recv_sems.at[m],
                device_id=(x, y, 1 - c), device_id_type=MESH)

        def start():
            for m in range(n):
                copy(m).start()

        def finish():
            for m in range(n):
                copy(m).wait()

        return start, lambda: None, finish

    return _Exchange(grads, [_array((4,) + g.shape[2:], g.dtype) for g in grads],
                     [pltpu.SemaphoreType.DMA((n,)), pltpu.SemaphoreType.DMA((n,))], phases)


def _chip_exchange(pair_sums):
    n = len(pair_sums)

    def phases(ins, outs, sems):
        send_sems, recv_sems = sems
        x, y, c = lax.axis_index("x"), lax.axis_index("y"), lax.axis_index("c")
        chips = [(1 - x, y), (x, 1 - y), (1 - x, 1 - y)]

        def copies():
            return [pltpu.make_async_remote_copy(
                src_ref=ins[m].at[2 * px + py], dst_ref=outs[m].at[j],
                send_sem=send_sems.at[m, j], recv_sem=recv_sems.at[m, j],
                device_id=(px, py, c), device_id_type=MESH)
                for m in range(n) for j, (px, py) in enumerate(chips)]

        def start():
            for cp in copies():
                cp.start()

        def finish():
            for cp in copies():
                cp.wait_recv()
            for cp in copies():
                cp.wait_send()

        return start, lambda: None, finish

    return _Exchange(pair_sums, [_array((3,) + p.shape[1:], p.dtype) for p in pair_sums],
                     [pltpu.SemaphoreType.DMA((n, 3)), pltpu.SemaphoreType.DMA((n, 3))], phases)


N_SLABS = ATTN_W // HEAD_PAIR


def _slab_scratch(tm, n=N_SLABS):
    return pltpu.VMEM((n, tm, HEAD_PAIR), F32)


def _rows_by_residue(dst_ref, slab_ref, r, tr, dtype, n=N_SLABS):
    for rho in range(r):
        for s in range(n):
            dst_ref[rho, :, s * HEAD_PAIR:(s + 1) * HEAD_PAIR] = (
                slab_ref[s, pl.ds(rho, tr, stride=r), :].astype(dtype))


def _rows_by_token(slab_ref, src_ref, r, tr, n=N_SLABS):
    for rho in range(r):
        for s in range(n):
            slab_ref[s, pl.ds(rho, tr, stride=r), :] = (
                src_ref[rho, :, s * HEAD_PAIR:(s + 1) * HEAD_PAIR].astype(F32))


def _norm1_and_rope_tables(x, g, pos_col, inv_freq_row, tm, carried=None):
    T = x.shape[0]

    def body(x_ref, g_ref, pos_ref, invf_ref, h_ref, cos_ref, s1_ref, s2_ref):
        xf = x_ref[...]
        r = lax.rsqrt(jnp.mean(xf * xf, axis=-1, keepdims=True) + EPS)
        h_ref[...] = (xf * r * g_ref[...]).astype(BF16)
        ang = pos_ref[...].astype(F32) * invf_ref[...]
        lane = lax.broadcasted_iota(jnp.int32, (1, HEAD_PAIR), 1) % 64
        cs, sn = jnp.cos(ang), jnp.sin(ang)
        cos_ref[...] = jnp.where(lane < 2 * ROPE_HALF, cs, 1.0)
        s1_ref[...] = jnp.where(lane < ROPE_HALF, -sn, 0.0)
        s2_ref[...] = jnp.where((lane >= ROPE_HALF) & (lane < 2 * ROPE_HALF), sn, 0.0)

    tab = _array((T, HEAD_PAIR), F32)
    row = pl.BlockSpec((tm, HEAD_PAIR), lambda i: (i, 0))
    wide = pl.BlockSpec((tm, D_MODEL), lambda i: (i, 0))
    return _call(
        body, name="norm1_rope_tables", grid=(T // tm,), out_shape=[_array((T, D_MODEL), BF16), tab, tab, tab],
        in_specs=[wide, pl.BlockSpec((1, D_MODEL), lambda i: (0, 0)), pl.BlockSpec((tm, 1), lambda i: (i, 0)),
                  pl.BlockSpec((1, HEAD_PAIR), lambda i: (0, 0))],
        out_specs=[wide, row, row, row], vmem_mib=24, args=(x, g, pos_col, inv_freq_row), carried=carried)


def _rope(y, cos, s1, s2):
    w = y.shape[1]
    rep = w // HEAD_PAIR
    return (y * jnp.tile(cos, (1, rep)) + pltpu.roll(y, w - ROPE_HALF, 1) * jnp.tile(s1, (1, rep))
            + pltpu.roll(y, ROPE_HALF, 1) * jnp.tile(s2, (1, rep)))


def _rope_transposed(dy, cos, s1, s2):
    w = dy.shape[1]
    rep = w // HEAD_PAIR
    return (dy * jnp.tile(cos, (1, rep)) + pltpu.roll(dy * jnp.tile(s1, (1, rep)), ROPE_HALF, 1)
            + pltpu.roll(dy * jnp.tile(s2, (1, rep)), w - ROPE_HALF, 1))


def _resident(shape, block=None):
    at = (0,) * len(shape) if block is None else block
    return pl.BlockSpec(shape, lambda *_: at, pipeline_mode=pl.Buffered(1))


def _in_proj(h, win_t, tables, tm, carried=None):
    T = h.shape[0]
    other = (IN_COLS - QKV_COLS) // ATTN_W

    def body(h_ref, w_ref, cos_ref, s1_ref, s2_ref, o0, o1, o2, uvg_ref, *slabs):
        hv = h_ref[...]
        cos, s1, s2 = cos_ref[...], s1_ref[...], s2_ref[...]
        for kind in range(3):
            for g, (o_ref, r) in enumerate(zip((o0, o1, o2), DILATIONS)):
                blk = 3 * kind + g
                if blk < other:
                    rows = slice(QKV_COLS + blk * ATTN_W, QKV_COLS + (blk + 1) * ATTN_W)
                    uvg_ref[:, blk * ATTN_W:(blk + 1) * ATTN_W] = _nt(hv, w_ref[rows, :]).astype(BF16)
                y = _nt(hv, w_ref[blk * ATTN_W:(blk + 1) * ATTN_W, :])
                if kind < 2:
                    y = _rope(y, cos, s1, s2)
                if kind == 0:
                    y = y * 0.125
                cols = slice(kind * ATTN_W, (kind + 1) * ATTN_W)
                if r == 1:
                    o_ref[0, :, cols] = y.astype(BF16)
                    continue
                slab = slabs[blk % len(slabs)]
                for s in range(N_SLABS):
                    slab[s] = y[:, s * HEAD_PAIR:(s + 1) * HEAD_PAIR]
                for rho in range(r):
                    for s in range(N_SLABS):
                        at = kind * ATTN_W + s * HEAD_PAIR
                        o_ref[rho, :, at:at + HEAD_PAIR] = slab[s, pl.ds(rho, tm // r, stride=r), :].astype(BF16)

    row = pl.BlockSpec((tm, HEAD_PAIR), lambda i: (i, 0))
    return _call(
        body, name="in_proj", grid=(T // tm,),
        out_shape=[_array((r, T // r, 3 * ATTN_W), BF16) for r in DILATIONS] + [_array((T, IN_COLS - QKV_COLS), BF16)],
        in_specs=[pl.BlockSpec((tm, D_MODEL), lambda i: (i, 0)), _resident((IN_COLS, D_MODEL)), row, row, row],
        out_specs=[pl.BlockSpec((r, tm // r, 3 * ATTN_W), lambda i: (0, i, 0)) for r in DILATIONS]
        + [pl.BlockSpec((tm, IN_COLS - QKV_COLS), lambda i: (i, 0))],
        scratch_shapes=[_slab_scratch(tm)] * 3, vmem_mib=56, args=(h, win_t, *tables), carried=carried)


def _band_mask(n):
    row = lax.broadcasted_iota(jnp.int32, (2 * BLK, 2 * BLK), 0) & (BLK - 1)
    col = lax.broadcasted_iota(jnp.int32, (2 * BLK, 2 * BLK), 1)
    has_prev = (jnp.zeros_like(row) + n) > 0
    return ((col < BLK) & (col >= row) & has_prev) | ((col >= BLK) & (col - BLK <= row))


def _head_lanes():
    lane = lax.broadcasted_iota(jnp.int32, (1, HEAD_PAIR), 1)
    return lane < 64, lane >= 64


def _stack_heads(x, head0, head1):
    zero = jnp.zeros_like(x)
    return jnp.concatenate([jnp.where(head0, x, zero), jnp.where(head1, x, zero)], axis=0)


def _unstack_heads(y, head0):
    return jnp.where(head0, y[:BLK], y[BLK:])


def _per_head(stats, col):
    return jnp.concatenate([stats[:, col:col + 1], stats[:, col + 1:col + 2]], axis=0)


def _attn_sub_blocks(length):
    for n in (8, 4, 2):
        if length % (n * BLK) == 0:
            return n
    return 1


def _attn_fwd(qkv, group, carried=None):
    r, L, _ = qkv.shape
    nsub = _attn_sub_blocks(L)
    rows = nsub * BLK

    def body(q_ref, kp_ref, kc_ref, vp_ref, vc_ref, o_ref, lse_ref):
        head0, head1 = _head_lanes()
        lane = lax.broadcasted_iota(jnp.int32, (1, HEAD_PAIR), 1)
        for j in range(nsub):
            cur = slice(j * BLK, (j + 1) * BLK)
            before = slice((j - 1) * BLK, j * BLK)
            valid = _band_mask(pl.program_id(1) + j)
            stats = jnp.zeros((BLK, HEAD_PAIR), F32)
            for p in range(N_SLABS):
                sl = slice(p * HEAD_PAIR, (p + 1) * HEAD_PAIR)
                k_prev, v_prev = (kp_ref[:, sl], vp_ref[:, sl]) if j == 0 else (kc_ref[before, sl], vc_ref[before, sl])
                k2 = jnp.concatenate([k_prev, kc_ref[cur, sl]], axis=0)
                v2 = jnp.concatenate([v_prev, vc_ref[cur, sl]], axis=0)
                s = jnp.where(valid, _nt(_stack_heads(q_ref[cur, sl], head0, head1), k2), MASKED)
                m = jnp.max(s, axis=1, keepdims=True)
                e = jnp.exp(s - m)
                den = jnp.sum(e, axis=1, keepdims=True)
                o_ref[cur, sl] = _unstack_heads(_nn(e.astype(BF16), v2) / den, head0).astype(BF16)
                lse = m + jnp.log(den)
                stats = jnp.where(lane == 2 * p, lse[:BLK], jnp.where(lane == 2 * p + 1, lse[BLK:], stats))
            lse_ref[cur, :] = stats

    def cur_blk(kind, width=ATTN_W):
        return pl.BlockSpec((None, rows, width), lambda rho, n: (rho, n, kind))

    def prev_blk(kind):
        return pl.BlockSpec((None, BLK, ATTN_W), lambda rho, n: (rho, jnp.maximum(n * nsub - 1, 0), kind))

    return _call(
        body, name=f"attn_fwd_g{group}", grid=(r, L // rows),
        out_shape=[_array((r, L, ATTN_W), BF16), _array((r, L, HEAD_PAIR), F32)],
        in_specs=[cur_blk(0), prev_blk(1), cur_blk(1), prev_blk(2), cur_blk(2)],
        out_specs=[cur_blk(0), cur_blk(0, HEAD_PAIR)],
        vmem_mib=24, args=(qkv, qkv, qkv, qkv, qkv), carried=carried)


def _gelu(x):
    return 0.5 * x * (1.0 + lax.erf(x * (1.0 / math.sqrt(2.0))))


def _gelu_grad(x):
    return 0.5 * (1.0 + lax.erf(x * (1.0 / math.sqrt(2.0)))) + x * jnp.exp(-0.5 * x * x) * (1.0 / math.sqrt(2.0 * math.pi))


def _causal():
    row = lax.broadcasted_iota(jnp.int32, (BLK, BLK), 0)
    col = lax.broadcasted_iota(jnp.int32, (BLK, BLK), 1)
    return col <= row


def _bias_lanes(bt):
    grp = lax.broadcasted_iota(jnp.int32, (1, SGU_W), 1) // 64
    out = jnp.zeros((BLK, SGU_W), F32)
    for g in range(8):
        out = jnp.where(grp == g, bt[:, g:g + 1], out)
    return out


def _sgu_normalise(uv, ln_g, ln_b):
    z = _gelu(uv)
    u, v = z[:, :SGU_W], z[:, SGU_W:]
    mu = jnp.mean(v, axis=-1, keepdims=True)
    xc = v - mu
    rstd = lax.rsqrt(jnp.mean(xc * xc, axis=-1, keepdims=True) + EPS)
    xh = xc * rstd
    return u, xh, rstd, xh * ln_g + ln_b


def _sgu_mix(wc_ref, vb, head0):
    chunks = []
    for ch in range(vb.shape[0] // BLK):
        pairs = []
        for p in range(SGU_W // HEAD_PAIR):
            v_pair = vb[ch * BLK:(ch + 1) * BLK, p * HEAD_PAIR:(p + 1) * HEAD_PAIR]
            pairs.append(jnp.where(head0, _nn(wc_ref[2 * p], v_pair), _nn(wc_ref[2 * p + 1], v_pair)))
        chunks.append(jnp.concatenate(pairs, axis=1))
    return jnp.concatenate(chunks, axis=0)


def _sgu_fwd(uvg, ln_g, ln_b, w_s, b_t, tm):
    T = uvg.shape[0]

    def body(uv_ref, g_ref, b_ref, w_ref, bt_ref, o_ref, wc_ref, bias_ref):
        @pl.when(pl.program_id(0) == 0)
        def _():
            causal = _causal()
            for g in range(8):
                wc_ref[g] = jnp.where(causal, w_ref[g], 0.0).astype(BF16)
            bias_ref[...] = _bias_lanes(bt_ref[...])

        u, _, _, vn = _sgu_normalise(uv_ref[...].astype(F32), g_ref[...], b_ref[...])
        mixed = _sgu_mix(wc_ref, vn.astype(BF16), _head_lanes()[0])
        o_ref[...] = (u * (mixed + jnp.tile(bias_ref[...], (tm // BLK, 1)))).astype(BF16)

    vec = pl.BlockSpec((1, SGU_W), lambda i: (0, 0))
    return _pallas(
        body, name="sgu_fwd", grid=(T // tm,), out_shape=_array((T, SGU_W), BF16),
        in_specs=[pl.BlockSpec((tm, 2 * SGU_W), lambda i: (i, 0)), vec, vec,
                  pl.BlockSpec((8, BLK, BLK), lambda i: (0, 0, 0)), pl.BlockSpec((BLK, 8), lambda i: (0, 0))],
        out_specs=pl.BlockSpec((tm, SGU_W), lambda i: (i, 0)),
        scratch_shapes=[pltpu.VMEM((8, BLK, BLK), BF16), pltpu.VMEM((BLK, SGU_W), F32)],
        compiler_params=_params(("arbitrary",), 32),
    )(uvg, ln_g, ln_b, w_s, b_t)


def _merge_fwd(outs, lses, sgu, wpa_t, wps_t, uvg, w_out, x, g2, tm):
    T = sgu.shape[0]

    def body(o0, l0, o1, l1, o2, l2, sgu_ref, wpa_ref, wps_ref, ga_ref, gb_ref, wo_ref, x_ref, g_ref,
             pa_ref, ps_ref, m_ref, h_ref, attn_b_ref, x1_ref, attn_ref, lse_ref, so1, sl1, so2, sl2):
        for o_in, l_in, so, sl, r in ((o1, l1, so1, sl1, DILATIONS[1]), (o2, l2, so2, sl2, DILATIONS[2])):
            _rows_by_token(so, o_in, r, tm // r)
            _rows_by_token(sl, l_in, r, tm // r, n=1)
        head0, _ = _head_lanes()
        a0, a1, a2 = l0[0], sl1[0], sl2[0]
        mx = jnp.maximum(jnp.maximum(a0, a1), a2)
        e0, e1, e2 = jnp.exp(a0 - mx), jnp.exp(a1 - mx), jnp.exp(a2 - mx)
        tot = e0 + e1 + e2
        lse_ref[...] = mx + jnp.log(tot)
        w0, w1, w2 = e0 / tot, e1 / tot, e2 / tot
        for s in range(N_SLABS):
            lanes = slice(s * HEAD_PAIR, (s + 1) * HEAD_PAIR)

            def lanes_of(w):
                return jnp.where(head0, w[:, 2 * s:2 * s + 1], w[:, 2 * s + 1:2 * s + 2])

            mixed = lanes_of(w0) * o0[0, :, lanes] + lanes_of(w1) * so1[s] + lanes_of(w2) * so2[s]
            attn_ref[:, lanes] = mixed
            attn_b_ref[:, lanes] = mixed.astype(BF16)

        pa = _nt(attn_b_ref[...], wpa_ref[...])
        ps = _nt(sgu_ref[...], wps_ref[...])
        pa_ref[...] = pa.astype(BF16)
        ps_ref[...] = ps.astype(BF16)
        ga, gb = ga_ref[...].astype(F32), gb_ref[...].astype(F32)
        merged = (jax.nn.sigmoid(ga) * pa + jax.nn.sigmoid(gb) * ps).astype(BF16)
        m_ref[...] = merged
        x1 = x_ref[...] + _nn(merged, wo_ref[...])
        x1_ref[...] = x1
        r = lax.rsqrt(jnp.mean(x1 * x1, axis=-1, keepdims=True) + EPS)
        h_ref[...] = (x1 * r * g_ref[...]).astype(BF16)

    ins, specs = [], []
    for g, r in enumerate(DILATIONS):
        ins += [outs[g], lses[g]]
        specs += [pl.BlockSpec((r, tm // r, ATTN_W), lambda i: (0, i, 0)),
                  pl.BlockSpec((r, tm // r, HEAD_PAIR), lambda i: (0, i, 0))]
    half = pl.BlockSpec((tm, ATTN_W), lambda i: (i, 0))
    wide = pl.BlockSpec((tm, D_MODEL), lambda i: (i, 0))
    w = _resident((D_MODEL, ATTN_W))
    res = _array((T, D_MODEL), BF16)
    return _pallas(
        body, name="merge_fwd", grid=(T // tm,),
        out_shape=[res, res, res, res, _array((T, ATTN_W), BF16), _array((T, D_MODEL), F32),
                   _array((T, ATTN_W), F32), _array((T, HEAD_PAIR), F32)],
        in_specs=specs + [half, w, w, pl.BlockSpec((tm, D_MODEL), lambda i: (i, 1)),
                          pl.BlockSpec((tm, D_MODEL), lambda i: (i, 2)), _resident((D_MODEL, D_MODEL)), wide,
                          pl.BlockSpec((1, D_MODEL), lambda i: (0, 0))],
        out_specs=[wide] * 4 + [half, wide, half, pl.BlockSpec((tm, HEAD_PAIR), lambda i: (i, 0))],
        scratch_shapes=[_slab_scratch(tm), _slab_scratch(tm, 1), _slab_scratch(tm), _slab_scratch(tm, 1)],
        compiler_params=_params(("parallel",), 56),
    )(*ins, sgu, wpa_t, wps_t, uvg, uvg, w_out, x, g2)


def _ffn_fwd(h2, wg_t, wu_t, tm):
    T = h2.shape[0]

    def body(h_ref, wg_ref, wu_ref, gate_ref, up_ref, ff_ref):
        h = h_ref[...]
        for j in range(D_FF // FF_TILE):
            cols = slice(j * FF_TILE, (j + 1) * FF_TILE)
            gate, up = _nt(h, wg_ref[cols, :]), _nt(h, wu_ref[cols, :])
            gate_ref[:, cols] = gate.astype(BF16)
            up_ref[:, cols] = up.astype(BF16)
            ff_ref[:, cols] = (gate * jax.nn.sigmoid(gate) * up).astype(BF16)

    w = _resident((D_FF, D_MODEL))
    o = pl.BlockSpec((tm, D_FF), lambda i: (i, 0))
    res = _array((T, D_FF), BF16)
    return _pallas(
        body, name="ffn_fwd", grid=(T // tm,), out_shape=[res, res, res],
        in_specs=[pl.BlockSpec((tm, D_MODEL), lambda i: (i, 0)), w, w], out_specs=[o, o, o],
        compiler_params=_params(("parallel",), 52),
    )(h2, wg_t, wu_t)


def _down_loss(ff, w_down, x1, final_g, target, tm):
    T = x1.shape[0]

    def body(ff_ref, w_ref, x1_ref, g_ref, t_ref, dx_ref, dxb_ref, loss_ref, dg_ref):
        @pl.when(pl.program_id(0) == 0)
        def _():
            loss_ref[...] = jnp.zeros_like(loss_ref)
            dg_ref[...] = jnp.zeros_like(dg_ref)

        x2 = x1_ref[...] + _nn(ff_ref[...], w_ref[...])
        g = g_ref[...]
        r = lax.rsqrt(jnp.mean(x2 * x2, axis=-1, keepdims=True) + EPS)
        xh = x2 * r
        err = xh * g - t_ref[...]
        loss_ref[...] += jnp.sum(err * err, axis=0, keepdims=True) * (0.5 / D_MODEL)
        dy = err * (1.0 / D_MODEL)
        dg_ref[...] += jnp.sum(dy * xh, axis=0, keepdims=True)
        dxh = dy * g
        dx = r * (dxh - xh * jnp.mean(dxh * xh, axis=-1, keepdims=True))
        dx_ref[...] = dx
        dxb_ref[...] = dx.astype(BF16)

    wide = pl.BlockSpec((tm, D_MODEL), lambda i: (i, 0))
    vec = pl.BlockSpec((1, D_MODEL), lambda i: (0, 0))
    vec_shape = _array((1, D_MODEL), F32)
    return _pallas(
        body, name="down_loss", grid=(T // tm,),
        out_shape=[_array((T, D_MODEL), F32), _array((T, D_MODEL), BF16),
                   vec_shape, vec_shape],
        in_specs=[pl.BlockSpec((tm, D_FF), lambda i: (i, 0)), _resident((D_FF, D_MODEL)), wide, vec, wide],
        out_specs=[wide, wide, vec, vec],
        compiler_params=_params(("arbitrary",), 40),
    )(ff, w_down, x1, final_g, target)


def _rmsnorm_bwd(dh, xin, g, d_res, dg_ref):
    r = lax.rsqrt(jnp.mean(xin * xin, axis=-1, keepdims=True) + EPS)
    xh = xin * r
    dg_ref[...] += jnp.sum(dh * xh, axis=0, keepdims=True)
    dxh = dh * g
    return d_res + r * (dxh - xh * jnp.mean(dxh * xh, axis=-1, keepdims=True))


def _ffn_bwd(dx2, dx2b, w_down, wg_t, wu_t, gate, up, x1, g2, tm):
    T = dx2.shape[0]

    def body(dx_ref, dxb_ref, wd_ref, wg_ref, wu_ref, gate_ref, up_ref, x1_ref, g_ref,
             dgate_ref, dup_ref, dx1_ref, dx1b_ref, dg_ref):
        @pl.when(pl.program_id(0) == 0)
        def _():
            dg_ref[...] = jnp.zeros_like(dg_ref)

        dxb = dxb_ref[...]
        dh = jnp.zeros((tm, D_MODEL), F32)
        for j in range(D_FF // FF_TILE):
            cols = slice(j * FF_TILE, (j + 1) * FF_TILE)
            dff = _nt(dxb, wd_ref[cols, :])
            gate, up = gate_ref[:, cols].astype(F32), up_ref[:, cols].astype(F32)
            sg = jax.nn.sigmoid(gate)
            dgate = (dff * up * sg * (1.0 + gate * (1.0 - sg))).astype(BF16)
            dup = (dff * gate * sg).astype(BF16)
            dgate_ref[:, cols] = dgate
            dup_ref[:, cols] = dup
            dh += _nn(dgate, wg_ref[cols, :]) + _nn(dup, wu_ref[cols, :])
        dx1 = _rmsnorm_bwd(dh, x1_ref[...], g_ref[...], dx_ref[...], dg_ref)
        dx1_ref[...] = dx1
        dx1b_ref[...] = dx1.astype(BF16)

    wide = pl.BlockSpec((tm, D_MODEL), lambda i: (i, 0))
    ffw = pl.BlockSpec((tm, D_FF), lambda i: (i, 0))
    vec = pl.BlockSpec((1, D_MODEL), lambda i: (0, 0))
    w = _resident((D_FF, D_MODEL))
    ff_shape = _array((T, D_FF), BF16)
    return _pallas(
        body, name="ffn_bwd", grid=(T // tm,),
        out_shape=[ff_shape, ff_shape, _array((T, D_MODEL), F32),
                   _array((T, D_MODEL), BF16), _array((1, D_MODEL), F32)],
        in_specs=[wide, wide, w, w, w, ffw, ffw, wide, vec],
        out_specs=[ffw, ffw, wide, wide, vec], compiler_params=_params(("arbitrary",), 56),
    )(dx2, dx2b, w_down, wg_t, wu_t, gate, up, x1, g2)


def _mm_tn(a, b, *, tmm, tk, name, into=None, row_block0=0, rows_total=None):
    T, M = a.shape
    N = b.shape[1]
    rows_total = M if rows_total is None else rows_total

    def body(*refs):
        a_ref, b_ref, o_ref = refs[0], refs[1], refs[-1]

        @pl.when(pl.program_id(1) == 0)
        def _():
            o_ref[...] = jnp.zeros_like(o_ref)

        o_ref[...] += _tn(a_ref[...], b_ref[...])

    ins = [a, b] + ([] if into is None else [into])
    specs = [pl.BlockSpec((tk, tmm), lambda i, k: (k, i)), pl.BlockSpec((tk, N), lambda i, k: (k, 0))]
    return _pallas(
        body, name=name, grid=(M // tmm, T // tk),
        out_shape=_array((rows_total, N), F32),
        in_specs=specs + ([] if into is None else [ANY]),
        out_specs=pl.BlockSpec((tmm, N), lambda i, k: (row_block0 + i, 0)),
        input_output_aliases={} if into is None else {2: 0},
        compiler_params=_params(("parallel", "arbitrary"), 48),
    )(*ins)


def _in_bwd(dqkv, dug, win_t, x, g1, dx1, tm, carried=None):
    T = x.shape[0]

    def body(dq_ref, dug_ref, w_ref, x_ref, g_ref, d_ref, dx_ref, dg_ref):
        @pl.when(pl.program_id(0) == 0)
        def _():
            dg_ref[...] = jnp.zeros_like(dg_ref)

        dh = _nn(dq_ref[...], w_ref[:QKV_COLS, :]) + _nn(dug_ref[...], w_ref[QKV_COLS:, :])
        dx_ref[...] = _rmsnorm_bwd(dh, x_ref[...], g_ref[...], d_ref[...], dg_ref)

    def cols(n):
        return pl.BlockSpec((tm, n), lambda i: (i, 0))

    wide = cols(D_MODEL)
    vec = pl.BlockSpec((1, D_MODEL), lambda i: (0, 0))
    return _call(
        body, name="in_bwd", grid=(T // tm,),
        out_shape=[_array((T, D_MODEL), F32), _array((1, D_MODEL), F32)],
        in_specs=[cols(QKV_COLS), cols(IN_COLS - QKV_COLS), _resident((IN_COLS, D_MODEL)), wide, vec, wide],
        out_specs=[wide, vec], vmem_mib=56, args=(dqkv, dug, win_t, x, g1, dx1), carried=carried)


D_LANE0 = 8


def _merge_bwd(dx1b, w_out, pa, ps, uvg, wpa_t, wps_t, attn, lse, tm, carried=None):
    T = dx1b.shape[0]

    def body(dx_ref, w_ref, pa_ref, ps_ref, ga_ref, gb_ref, wpa_ref, wps_ref, at_ref, lse_ref,
             dpa_ref, dps_ref, dug_ref, dsgu_ref, *rest):
        outs, (scr_da, scr_st) = rest[:-2], rest[-2:]
        dm = _nt(dx_ref[...], w_ref[...])
        ga, gb = jax.nn.sigmoid(ga_ref[...].astype(F32)), jax.nn.sigmoid(gb_ref[...].astype(F32))
        dpa, dps = (dm * ga).astype(BF16), (dm * gb).astype(BF16)
        dpa_ref[...] = dpa
        dps_ref[...] = dps
        dug_ref[:, 2 * SGU_W:2 * SGU_W + D_MODEL] = (dm * pa_ref[...].astype(F32) * ga * (1.0 - ga)).astype(BF16)
        dug_ref[:, 2 * SGU_W + D_MODEL:] = (dm * ps_ref[...].astype(F32) * gb * (1.0 - gb)).astype(BF16)
        dsgu_ref[...] = _nn(dps, wps_ref[...])
        dattn = _nn(dpa, wpa_ref[...])

        head0, _ = _head_lanes()
        lane = lax.broadcasted_iota(jnp.int32, (1, HEAD_PAIR), 1)
        stats = lse_ref[...]
        for s in range(N_SLABS):
            lanes = slice(s * HEAD_PAIR, (s + 1) * HEAD_PAIR)
            da = dattn[:, lanes]
            pp = da * at_ref[:, lanes]
            d0 = jnp.sum(jnp.where(head0, pp, 0.0), axis=1, keepdims=True)
            d1 = jnp.sum(jnp.where(head0, 0.0, pp), axis=1, keepdims=True)
            stats = jnp.where(lane == D_LANE0 + 2 * s, d0, jnp.where(lane == D_LANE0 + 2 * s + 1, d1, stats))
            scr_da[s] = da
            outs[0][0, :, lanes] = da.astype(BF16)
        scr_st[0] = stats
        outs[1][0] = stats
        for g, r in enumerate(DILATIONS):
            if r > 1:
                _rows_by_residue(outs[2 * g], scr_da, r, tm // r, BF16)
                _rows_by_residue(outs[2 * g + 1], scr_st, r, tm // r, F32, n=1)

    wide = pl.BlockSpec((tm, D_MODEL), lambda i: (i, 0))
    half = pl.BlockSpec((tm, ATTN_W), lambda i: (i, 0))
    w = _resident((D_MODEL, ATTN_W))
    res = _array((T, D_MODEL), BF16)
    shapes, specs = [], []
    for r in DILATIONS:
        shapes += [_array((r, T // r, ATTN_W), BF16), _array((r, T // r, HEAD_PAIR), F32)]
        specs += [pl.BlockSpec((r, tm // r, ATTN_W), lambda i: (0, i, 0)),
                  pl.BlockSpec((r, tm // r, HEAD_PAIR), lambda i: (0, i, 0))]
    return _call(
        body, name="merge_bwd", grid=(T // tm,),
        out_shape=[res, res, _array((T, IN_COLS - QKV_COLS), BF16), _array((T, ATTN_W), F32)] + shapes,
        in_specs=[wide, _resident((D_MODEL, D_MODEL)), wide, wide,
                  pl.BlockSpec((tm, D_MODEL), lambda i: (i, 1)), pl.BlockSpec((tm, D_MODEL), lambda i: (i, 2)), w, w,
                  half, pl.BlockSpec((tm, HEAD_PAIR), lambda i: (i, 0))],
        out_specs=[wide, wide, pl.BlockSpec((tm, IN_COLS - QKV_COLS), lambda i: (i, 0)), half] + specs,
        scratch_shapes=[_slab_scratch(tm), _slab_scratch(tm, 1)],
        vmem_mib=56, args=(dx1b, w_out, pa, ps, uvg, uvg, wpa_t, wps_t, attn, lse), carried=carried)


def _sgu_bwd(uvg, dsgu, ln_g, ln_b, w_s, b_t, dug, tm, carried=None):
    T = uvg.shape[0]
    nsteps = T // tm

    def body(uv_ref, ds_ref, g_ref, b_ref, w_ref, bt_ref, _, duv_ref, dw_ref, dbt_ref, dg_ref, db_ref,
             wc_ref, wct_ref, bias_ref, dbias_ref):
        step = pl.program_id(0)
        head0, head1 = _head_lanes()

        @pl.when(step == 0)
        def _():
            causal = _causal()
            for g in range(8):
                wc = jnp.where(causal, w_ref[g], 0.0)
                wc_ref[g] = wc.astype(BF16)
                wct_ref[g] = wc.T.astype(BF16)
            bias_ref[...] = _bias_lanes(bt_ref[...])
            dbias_ref[...] = jnp.zeros_like(dbias_ref)
            dw_ref[...] = jnp.zeros_like(dw_ref)
            dg_ref[...] = jnp.zeros_like(dg_ref)
            db_ref[...] = jnp.zeros_like(db_ref)

        uv = uv_ref[...].astype(F32)
        ln_gain = g_ref[...]
        u, xh, rstd, vn = _sgu_normalise(uv, ln_gain, b_ref[...])
        vb = vn.astype(BF16)
        mixed = _sgu_mix(wc_ref, vb, head0) + jnp.tile(bias_ref[...], (tm // BLK, 1))
        dout = ds_ref[...]
        du = dout * mixed
        dmixed = dout * u
        dmb = dmixed.astype(BF16)
        dvn_chunks = []
        for ch in range(tm // BLK):
            rows = slice(ch * BLK, (ch + 1) * BLK)
            dbias_ref[...] += dmixed[rows]
            pairs = []
            for p in range(SGU_W // HEAD_PAIR):
                lanes = slice(p * HEAD_PAIR, (p + 1) * HEAD_PAIR)
                dm_pair, v_pair = dmb[rows, lanes], vb[rows, lanes]
                acc = jnp.zeros((BLK, HEAD_PAIR), F32)
                for hh, half in enumerate((head0, head1)):
                    dm_h = jnp.where(half, dm_pair, jnp.zeros_like(dm_pair))
                    dw_ref[2 * p + hh] += _nt(dm_h, v_pair)
                    acc += _nn(wct_ref[2 * p + hh], dm_h)
                pairs.append(acc)
            dvn_chunks.append(jnp.concatenate(pairs, axis=1))
        dvn = jnp.concatenate(dvn_chunks, axis=0)
        dg_ref[...] += jnp.sum(dvn * xh, axis=0, keepdims=True)
        db_ref[...] += jnp.sum(dvn, axis=0, keepdims=True)
        dxh = dvn * ln_gain
        dv = rstd * (dxh - jnp.mean(dxh, axis=-1, keepdims=True) - xh * jnp.mean(dxh * xh, axis=-1, keepdims=True))
        dgelu = _gelu_grad(uv)
        duv_ref[:, :SGU_W] = (du * dgelu[:, :SGU_W]).astype(BF16)
        duv_ref[:, SGU_W:] = (dv * dgelu[:, SGU_W:]).astype(BF16)

        @pl.when(step == nsteps - 1)
        def _():
            causal = _causal()
            for g in range(8):
                dw_ref[g] = jnp.where(causal, dw_ref[g], 0.0)
            grp = lax.broadcasted_iota(jnp.int32, (1, SGU_W), 1) // 64
            col = lax.broadcasted_iota(jnp.int32, (1, 8), 1)
            dbias = dbias_ref[...]
            out = jnp.zeros((BLK, 8), F32)
            for g in range(8):
                s = jnp.sum(jnp.where(grp == g, dbias, 0.0), axis=1, keepdims=True)
                out = jnp.where(col == g, s, out)
            dbt_ref[...] = out

    vec = pl.BlockSpec((1, SGU_W), lambda i: (0, 0))
    w3 = pl.BlockSpec((8, BLK, BLK), lambda i: (0, 0, 0))
    bt = pl.BlockSpec((BLK, 8), lambda i: (0, 0))
    return _call(
        body, name="sgu_bwd", grid=(nsteps,),
        out_shape=[_array(dug.shape, BF16), _array((8, BLK, BLK), F32),
                   _array((BLK, 8), F32), _array((1, SGU_W), F32),
                   _array((1, SGU_W), F32)],
        in_specs=[pl.BlockSpec((tm, 2 * SGU_W), lambda i: (i, 0)), pl.BlockSpec((tm, SGU_W), lambda i: (i, 0)),
                  vec, vec, w3, bt, ANY],
        out_specs=[pl.BlockSpec((tm, 2 * SGU_W), lambda i: (i, 0)), w3, bt, vec, vec],
        scratch_shapes=[pltpu.VMEM((8, BLK, BLK), BF16), pltpu.VMEM((8, BLK, BLK), BF16),
                        pltpu.VMEM((BLK, SGU_W), F32), pltpu.VMEM((BLK, SGU_W), F32)],
        vmem_mib=40, args=(uvg, dsgu, ln_g, ln_b, w_s, b_t, dug), carried=carried, aliases={6: 0})


def _attn_bwd(qkv, dattn, stats, group, carried=None):
    r, L, _ = qkv.shape
    nsub = _attn_sub_blocks(L)
    rows = nsub * BLK
    nb = L // rows
    keep = rows - BLK
    whole = nb == 1

    def body(q_ref, kp_ref, kc_ref, vp_ref, vc_ref, da_ref, st_ref, dq_ref, dk_ref, dv_ref, carry_k, carry_v):
        n = pl.program_id(1)

        if not whole:
            @pl.when(n == 0)
            def _():
                carry_k[...] = jnp.zeros_like(carry_k)
                carry_v[...] = jnp.zeros_like(carry_v)

        @pl.when(n < nb)
        def _():
            head0, head1 = _head_lanes()
            for p in range(N_SLABS):
                sl = slice(p * HEAD_PAIR, (p + 1) * HEAD_PAIR)
                dk_parts = [jnp.zeros((BLK, HEAD_PAIR), F32) for _ in range(nsub + 1)]
                dv_parts = [jnp.zeros((BLK, HEAD_PAIR), F32) for _ in range(nsub + 1)]
                for j in range(nsub):
                    cur = slice(j * BLK, (j + 1) * BLK)
                    before = slice((j - 1) * BLK, j * BLK)
                    valid = _band_mask(n + j)
                    st = st_ref[cur, :]
                    k_prev, v_prev = (kp_ref[:, sl], vp_ref[:, sl]) if j == 0 else (kc_ref[before, sl], vc_ref[before, sl])
                    k2 = jnp.concatenate([k_prev, kc_ref[cur, sl]], axis=0)
                    v2 = jnp.concatenate([v_prev, vc_ref[cur, sl]], axis=0)
                    qs = _stack_heads(q_ref[cur, sl], head0, head1)
                    das = _stack_heads(da_ref[cur, sl], head0, head1)
                    prob = jnp.where(valid, jnp.exp(_nt(qs, k2) - _per_head(st, 2 * p)), 0.0)
                    ds = (prob * (_nt(das, v2) - _per_head(st, D_LANE0 + 2 * p))).astype(BF16)
                    dk2 = _tn(ds, qs)
                    dv2 = _tn(prob.astype(BF16), das)
                    dq_ref[cur, sl] = _unstack_heads(_nn(ds, k2), head0).astype(BF16)
                    dk_parts[j] += dk2[:BLK]
                    dk_parts[j + 1] += dk2[BLK:]
                    dv_parts[j] += dv2[:BLK]
                    dv_parts[j + 1] += dv2[BLK:]
                if whole:
                    for j in range(nsub):
                        dk_ref[j * BLK:(j + 1) * BLK, sl] = dk_parts[j + 1].astype(BF16)
                        dv_ref[j * BLK:(j + 1) * BLK, sl] = dv_parts[j + 1].astype(BF16)
                    continue
                if keep:
                    dk_ref[:keep, sl] = carry_k[:keep, sl].astype(BF16)
                    dv_ref[:keep, sl] = carry_v[:keep, sl].astype(BF16)
                dk_ref[keep:, sl] = (carry_k[keep:, sl] + dk_parts[0]).astype(BF16)
                dv_ref[keep:, sl] = (carry_v[keep:, sl] + dv_parts[0]).astype(BF16)
                for j in range(nsub):
                    carry_k[j * BLK:(j + 1) * BLK, sl] = dk_parts[j + 1]
                    carry_v[j * BLK:(j + 1) * BLK, sl] = dv_parts[j + 1]

        if not whole:
            @pl.when(n == nb)
            def _():
                dk_ref[...] = carry_k[...].astype(BF16)
                dv_ref[...] = carry_v[...].astype(BF16)

    def cur_blk(kind, width=ATTN_W):
        return pl.BlockSpec((None, rows, width), lambda rho, n: (rho, jnp.minimum(n, nb - 1), kind))

    def last_blk(kind):
        return pl.BlockSpec((None, rows, ATTN_W), lambda rho, n: (rho, jnp.clip(n - 1, 0, nb - 1), kind))

    def prev_keys(kind):
        return pl.BlockSpec((None, BLK, ATTN_W),
                            lambda rho, n: (rho, jnp.clip(n * nsub - 1, 0, nb * nsub - 1), kind))

    res = _array((r, L, ATTN_W), BF16)
    return _call(
        body, name=f"attn_bwd_g{group}", grid=(r, 1 if whole else nb + 1),
        out_shape=[res, res, res],
        in_specs=[cur_blk(0), prev_keys(1), cur_blk(1), prev_keys(2), cur_blk(2), cur_blk(0), cur_blk(0, HEAD_PAIR)],
        out_specs=[cur_blk(0), last_blk(0), last_blk(0)],
        scratch_shapes=[pltpu.VMEM((rows, ATTN_W), F32), pltpu.VMEM((rows, ATTN_W), F32)],
        vmem_mib=32, args=(qkv, qkv, qkv, qkv, qkv, dattn, stats), carried=carried)


def _dqkv_token_order(dqkv_groups, tables, tm):
    T = tables[0].shape[0]

    def body(*refs):
        ins = refs[:9]
        cos_ref, s1_ref, s2_ref, o_ref, scr = refs[9:]
        cos, s1, s2 = cos_ref[...], s1_ref[...], s2_ref[...]
        for g, r in enumerate(DILATIONS):
            for kind in range(3):
                src = ins[3 * g + kind]
                if r > 1:
                    _rows_by_token(scr, src, r, tm // r)
                for s in range(N_SLABS):
                    val = scr[s] if r > 1 else src[0, :, s * HEAD_PAIR:(s + 1) * HEAD_PAIR].astype(F32)
                    if kind < 2:
                        val = _rope_transposed(val, cos, s1, s2)
                    if kind == 0:
                        val = val * 0.125
                    at = (3 * kind + g) * ATTN_W + s * HEAD_PAIR
                    o_ref[:, at:at + HEAD_PAIR] = val.astype(BF16)

    specs = []
    for r in DILATIONS:
        specs += [pl.BlockSpec((r, tm // r, ATTN_W), lambda i: (0, i, 0))] * 3
    row = pl.BlockSpec((tm, HEAD_PAIR), lambda i: (i, 0))
    flat = [a for grp in dqkv_groups for a in grp]
    return _pallas(
        body, name="dqkv_token_order", grid=(T // tm,), out_shape=_array((T, QKV_COLS), BF16),
        in_specs=specs + [row] * 3, out_specs=pl.BlockSpec((tm, QKV_COLS), lambda i: (i, 0)),
        scratch_shapes=[_slab_scratch(tm)], compiler_params=_params(("parallel",), 48),
    )(*flat, *tables)


def _row_tile(rows):
    for cand in (320, 256, 176, 128):
        if rows % cand == 0:
            return cand
    return rows


def _pair_sum(grad4, recv, chip, name):
    _, _, rows, cols = grad4.shape
    tr = _row_tile(rows)

    def body(ids_ref, g_ref, r_ref, gown_ref, rown_ref, sum_ref, own_ref):
        sum_ref[...] = (g_ref[...] + r_ref[...]).astype(BF16)

        @pl.when(pl.program_id(1) == 0)
        def _():
            own_ref[...] = gown_ref[...] + rown_ref[...]

    grid_spec = pltpu.PrefetchScalarGridSpec(
        num_scalar_prefetch=1, grid=(rows // tr, 4),
        in_specs=[pl.BlockSpec((None, None, tr, cols), lambda i, q, ids: (q, ids[1], i, 0)),
                  pl.BlockSpec((None, tr, cols), lambda i, q, ids: (q, i, 0)),
                  pl.BlockSpec((None, None, tr, cols), lambda i, q, ids: (ids[0], ids[1], i, 0)),
                  pl.BlockSpec((None, tr, cols), lambda i, q, ids: (ids[0], i, 0))],
        out_specs=[pl.BlockSpec((None, tr, cols), lambda i, q, ids: (q, i, 0)),
                   pl.BlockSpec((tr, cols), lambda i, q, ids: (i, 0))])
    return _pallas(
        body, name=name, grid_spec=grid_spec,
        out_shape=[_array((4, rows, cols), BF16), _array((rows, cols), F32)],
        compiler_params=_params(("arbitrary", "arbitrary"), 32),
    )(chip, grad4, recv, grad4, recv)


def _chip_sum(own, others, name):
    rows, cols = own.shape
    tr = _row_tile(rows)

    def body(own_ref, oth_ref, o_ref):
        total = own_ref[...]
        for j in range(3):
            total = total + oth_ref[j].astype(F32)
        o_ref[...] = total

    blk = pl.BlockSpec((tr, cols), lambda i: (i, 0))
    return _pallas(
        body, name=name, grid=(rows // tr,), out_shape=_array((rows, cols), F32),
        in_specs=[blk, pl.BlockSpec((3, tr, cols), lambda i: (0, i, 0))], out_specs=blk,
        compiler_params=_params(("parallel",), 32),
    )(own, others)


def _adam_math(w, g, m, v):
    m = ADAM_B1 * m + (1.0 - ADAM_B1) * g
    v = ADAM_B2 * v + (1.0 - ADAM_B2) * (g * g)
    m_hat = m / (1.0 - ADAM_B1 ** ADAM_STEP)
    v_hat = v / (1.0 - ADAM_B2 ** ADAM_STEP)
    delta = -ADAM_LR * (m_hat / (jnp.sqrt(v_hat) + ADAM_EPS) + ADAM_WD * w)
    return delta, m, v


def _adamw(w, g, m, v, name):
    rows, cols = w.shape
    tr = _row_tile(rows)

    def body(w_ref, g_ref, m_ref, v_ref, d_ref, nm_ref, nv_ref):
        d_ref[...], nm_ref[...], nv_ref[...] = _adam_math(w_ref[...], g_ref[...], m_ref[...], v_ref[...])

    blk = pl.BlockSpec((tr, cols), lambda i: (i, 0))
    res = _array((rows, cols), F32)
    return _pallas(
        body, name=name, grid=(rows // tr,), out_shape=[res, res, res], in_specs=[blk] * 4, out_specs=[blk] * 3,
        compiler_params=_params(("parallel",), 32),
    )(w, g, m, v)


def _small_update(late_parts, early_parts, w, m, v, loss_rows):
    rows = w.shape[0]

    def total(p_ref):
        n = p_ref.shape[0] // N_DEV
        acc = p_ref[0:n, :]
        for dev in range(1, N_DEV):
            acc = acc + p_ref[dev * n:(dev + 1) * n, :]
        return acc

    def body(late_ref, early_ref, w_ref, m_ref, v_ref, g_ref, d_ref, nm_ref, nv_ref, loss_ref):
        g = jnp.concatenate([total(late_ref), total(early_ref)], axis=0)
        g_ref[...] = g
        d_ref[...], nm_ref[...], nv_ref[...] = _adam_math(w_ref[...], g, m_ref[...], v_ref[...])
        loss_ref[...] = jnp.sum(jnp.sum(g[rows - loss_rows:, :], axis=1, keepdims=True), axis=0, keepdims=True)

    res = jax.ShapeDtypeStruct((rows, HEAD_PAIR), F32)
    return _pallas(
        body, name="small_update", out_shape=[res, res, res, res, jax.ShapeDtypeStruct((1, 1), F32)],
        compiler_params=pltpu.CompilerParams(vmem_limit_bytes=32 * MIB),
    )(late_parts, early_parts, w, m, v)


def kernel(x, positions, norm1_g, w_in, sgu_ln_g, sgu_ln_b, w_spatial, b_spatial, w_proj_attn, w_proj_sgu, w_out, norm2_g, w_ffn_gate, w_ffn_up, w_ffn_down, final_g, loss_target, m_norm1_g, m_w_in, m_sgu_ln_g, m_sgu_ln_b, m_w_spatial, m_b_spatial, m_w_proj_attn, m_w_proj_sgu, m_w_out, m_norm2_g, m_w_ffn_gate, m_w_ffn_up, m_w_ffn_down, m_final_g, v_norm1_g, v_w_in, v_sgu_ln_g, v_sgu_ln_b, v_w_spatial, v_b_spatial, v_w_proj_attn, v_w_proj_sgu, v_w_out, v_norm2_g, v_w_ffn_gate, v_w_ffn_up, v_w_ffn_down, v_final_g):
    T = x.shape[1]
    tm = 512
    xt = x[0]
    target = loss_target[0]
    chip = jnp.stack([2 * lax.axis_index("x") + lax.axis_index("y"), lax.axis_index("c")]).astype(jnp.int32)

    def bf16_rows(w, transpose):
        return (jnp.transpose(w[0]) if transpose else w[0]).astype(BF16)

    w_in_gather = _all_gather([bf16_rows(w_in, True)])
    later = _all_gather([bf16_rows(w_ffn_gate, True), bf16_rows(w_ffn_up, True), bf16_rows(w_out, False),
                         bf16_rows(w_proj_attn, True), bf16_rows(w_proj_sgu, True)])
    ffn_out = _all_gather([bf16_rows(w_ffn_down, False)])

    inv_freq = ROPE_THETA ** (-jnp.arange(0, 2 * ROPE_HALF, 2, dtype=F32) / (2 * ROPE_HALF))
    inv_freq_row = jnp.tile(jnp.concatenate([inv_freq, inv_freq, jnp.zeros((48,), F32)]), 2).reshape(1, HEAD_PAIR)
    b_t = jnp.transpose(b_spatial[0])

    (h, *tables), (win_t,) = _norm1_and_rope_tables(xt, norm1_g, positions.reshape(T, 1), inv_freq_row, tm,
                                                    carried=w_in_gather)
    (*qkv, uvg), (wg_t, wu_t, wout, wpa_t, wps_t) = _in_proj(h, win_t, tables, tm, carried=later)
    fwd0, (wd,) = _attn_fwd(qkv[0], 0, carried=ffn_out)
    fwd = [fwd0, _attn_fwd(qkv[1], 1)[0], _attn_fwd(qkv[2], 2)[0]]
    sgu = _sgu_fwd(uvg, sgu_ln_g, sgu_ln_b, w_spatial[0], b_t, tm)
    pa, ps, merged, h2, attn_b, x1, attn, lse = _merge_fwd([f[0] for f in fwd], [f[1] for f in fwd], sgu, wpa_t, wps_t,
                                                           uvg, wout, xt, norm2_g, tm)
    gate, up, ff = _ffn_fwd(h2, wg_t, wu_t, tm)
    dx2, dx2b, loss_cols, d_final_g = _down_loss(ff, wd, x1, final_g.reshape(1, D_MODEL), target, tm)

    dgate, dup, dx1, dx1b, d_norm2 = _ffn_bwd(dx2, dx2b, wd, wg_t, wu_t, gate, up, x1, norm2_g, tm // 2)
    tk = min(2048, T)
    d_wd = _mm_tn(ff, dx2b, tmm=FF_TILE, tk=tk, name="grad_w_ffn_down")
    d_wg_t = _mm_tn(dgate, h2, tmm=FF_TILE, tk=tk, name="grad_w_ffn_gate")
    d_wu_t = _mm_tn(dup, h2, tmm=FF_TILE, tk=tk, name="grad_w_ffn_up")

    def by_owner(grads):
        return [g.reshape(4, 2, g.shape[0] // N_DEV, g.shape[1]) for g in grads]

    def pair_sums(grads4, from_sibling, names):
        both = [_pair_sum(g4, rv, chip, "grad_pair_sum_" + nm) for g4, rv, nm in zip(grads4, from_sibling, names)]
        return [b[0] for b in both], [b[1] for b in both]

    ffn_names = ["w_ffn_gate", "w_ffn_up", "w_ffn_down"]
    ffn4 = by_owner([d_wg_t, d_wu_t, d_wd])
    (dpa, dps, dug, dsgu, *prep), ffn_sib = _merge_bwd(dx1b, wout, pa, ps, uvg, wpa_t, wps_t, attn, lse, tm,
                                                       carried=_pair_exchange(ffn4))
    ffn_sums, ffn_own = pair_sums(ffn4, ffn_sib, ffn_names)

    d_wout = _mm_tn(merged, dx1b, tmm=D_MODEL, tk=tk, name="grad_w_out")
    d_wpa_t = _mm_tn(dpa, attn_b, tmm=D_MODEL, tk=tk, name="grad_w_proj_attn")
    d_wps_t = _mm_tn(dps, sgu, tmm=D_MODEL, tk=tk, name="grad_w_proj_sgu")
    mid_names = ["w_proj_attn", "w_proj_sgu", "w_out"]
    mid4 = by_owner([d_wpa_t, d_wps_t, d_wout])
    (dug, d_ws, d_bs_t, d_ln_g, d_ln_b), mid_sib = _sgu_bwd(uvg, dsgu, sgu_ln_g, sgu_ln_b, w_spatial[0], b_t, dug, tm,
                                                           carried=_pair_exchange(mid4))
    mid_sums, mid_own = pair_sums(mid4, mid_sib, mid_names)

    dqkv0, ffn_far = _attn_bwd(qkv[0], prep[0], prep[1], 0, carried=_chip_exchange(ffn_sums))
    dqkv1, mid_far = _attn_bwd(qkv[1], prep[2], prep[3], 1, carried=_chip_exchange(mid_sums))
    def flat(parts):
        return jnp.concatenate([p.reshape(-1) for p in parts]).reshape(-1, HEAD_PAIR)

    early_part = flat([d_ln_g, d_ln_b, d_ws, jnp.transpose(d_bs_t), d_norm2, d_final_g, loss_cols])
    dqkv2, (early_parts,) = _attn_bwd(qkv[2], prep[4], prep[5], 2, carried=_all_gather([early_part]))
    dqkv = _dqkv_token_order([dqkv0, dqkv1, dqkv2], tables, tm)
    d_win_t = _mm_tn(dqkv, h, tmm=1536, tk=tk, name="grad_w_in_qkv", rows_total=IN_COLS)
    d_win_t = _mm_tn(dug, h, tmm=1536, tk=tk, name="grad_w_in_uv_gates", into=d_win_t, row_block0=3, rows_total=IN_COLS)
    in4 = by_owner([d_win_t])
    in_sib = _pair_exchange(in4).run_alone("w_in_grad_pair_exchange")
    in_sums, in_own = pair_sums(in4, in_sib, ["w_in"])
    (dx, d_norm1), in_far = _in_bwd(dqkv, dug, win_t, xt, norm1_g, dx1, tm, carried=_chip_exchange(in_sums))

    names = ["w_in"] + mid_names + ffn_names
    reduced = [_chip_sum(o, f, "grad_total_" + nm)
               for o, f, nm in zip(in_own + mid_own + ffn_own, in_far + mid_far + ffn_far, names)]
    transposed = (True, True, True, False, True, True, False)
    g_big = [jnp.transpose(r) if t else r for r, t in zip(reduced, transposed)]

    small_w = [norm1_g, sgu_ln_g, sgu_ln_b, w_spatial, b_spatial, norm2_g, final_g]
    small_m = [m_norm1_g, m_sgu_ln_g, m_sgu_ln_b, m_w_spatial, m_b_spatial, m_norm2_g, m_final_g]
    small_v = [v_norm1_g, v_sgu_ln_g, v_sgu_ln_b, v_w_spatial, v_b_spatial, v_norm2_g, v_final_g]
    zeros = jnp.zeros((D_MODEL,), F32)
    (late_parts,) = _all_gather([flat([d_norm1])]).run_alone("norm1_grad_all_gather", vmem=True)
    g_s, d_s, nm_s, nv_s, loss = _small_update(late_parts, early_parts, flat(small_w + [zeros]), flat(small_m + [zeros]),
                                               flat(small_v + [zeros]), D_MODEL // HEAD_PAIR)

    def unflat(vec):
        vec = vec.reshape(-1)
        out, at = [], 0
        for wgt in small_w:
            out.append(vec[at:at + wgt.size].reshape(wgt.shape))
            at += wgt.size
        return out

    small = [unflat(a) for a in (g_s, d_s, nm_s, nv_s)]

    big_w = [w_in, w_proj_attn, w_proj_sgu, w_out, w_ffn_gate, w_ffn_up, w_ffn_down]
    big_m = [m_w_in, m_w_proj_attn, m_w_proj_sgu, m_w_out, m_w_ffn_gate, m_w_ffn_up, m_w_ffn_down]
    big_v = [v_w_in, v_w_proj_attn, v_w_proj_sgu, v_w_out, v_w_ffn_gate, v_w_ffn_up, v_w_ffn_down]
    big_out = []
    for wgt, g, mm, vv, nm in zip(big_w, g_big, big_m, big_v, names):
        d, nm_, nv_ = _adamw(wgt[0], g, mm[0], vv[0], "adamw_" + nm)
        big_out.append([a[None] for a in (g, d, nm_, nv_)])

    small_at = {0: 0, 2: 1, 3: 2, 4: 3, 5: 4, 9: 5, 13: 6}
    big_at = {1: 0, 6: 1, 7: 2, 8: 3, 10: 4, 11: 5, 12: 6}
    outs = [loss[0, 0], dx[None]]
    for kind in range(4):
        for idx in range(14):
            outs.append(small[kind][small_at[idx]] if idx in small_at else big_out[big_at[idx]][kind])
    return tuple(outs)
```

```python
import functools
import math

import jax
import jax.numpy as jnp
from jax import lax
from jax.experimental import pallas as pl
from jax.experimental.pallas import tpu as pltpu

F32 = jnp.float32
BF16 = jnp.bfloat16

D_MODEL = 1024
HEAD_PAIR = 128
ATTN_W = 512
DILATIONS = (1, 4, 16)
BLK = 128
ROPE_HALF = 8
ROPE_THETA = 500000.0
SGU_W = 512
QKV_COLS = 4608
IN_COLS = 7680
D_FF = 2816
FF_TILE = 1408
EPS = 1e-6
N_DEV = 8
MASKED = -1e30

ADAM_LR = 0.001
ADAM_B1 = 0.9
ADAM_B2 = 0.999
ADAM_EPS = 1e-08
ADAM_WD = 0.01
ADAM_STEP = 10

MIB = 1024 * 1024
MESH = pl.DeviceIdType.MESH
ANY = pl.BlockSpec(memory_space=pl.ANY)


def _array(shape, dtype):
    return pltpu.HBM(tuple(shape), dtype)


PIN_BYTES = 4 * MIB


def _pin(x):
    if x.size * x.dtype.itemsize < PIN_BYTES:
        return x
    return pltpu.with_memory_space_constraint(x, pltpu.HBM)


def _pallas(body, **kwargs):
    call = pl.pallas_call(body, **kwargs)
    return lambda *args: call(*[_pin(a) for a in args])


def _params(sem, vmem_mib):
    return pltpu.CompilerParams(dimension_semantics=sem, vmem_limit_bytes=vmem_mib * MIB)


def _nt(a, b):
    return lax.dot_general(a, b, (((1,), (1,)), ((), ())), preferred_element_type=F32)


def _nn(a, b):
    return lax.dot_general(a, b, (((1,), (0,)), ((), ())), preferred_element_type=F32)


def _tn(a, b):
    return lax.dot_general(a, b, (((0,), (0,)), ((), ())), preferred_element_type=F32)


class _Exchange:
    def __init__(self, arrays, out_shapes, sem_shapes, phases):
        self.arrays, self.out_shapes, self.sem_shapes, self.phases = list(arrays), out_shapes, sem_shapes, phases

    def run_alone(self, name, vmem=False):
        n_in, n_out = len(self.arrays), len(self.out_shapes)

        def body(*refs):
            start, middle, finish = self.phases(refs[:n_in], refs[n_in:n_in + n_out], refs[n_in + n_out:])
            start()
            middle()
            finish()

        spec = pl.BlockSpec(memory_space=pltpu.VMEM) if vmem else ANY
        shapes = [jax.ShapeDtypeStruct(s.shape, s.dtype) for s in self.out_shapes] if vmem else self.out_shapes
        return (pl.pallas_call if vmem else _pallas)(
            body, name=name, out_shape=shapes, in_specs=[spec] * n_in, out_specs=[spec] * n_out,
            scratch_shapes=self.sem_shapes, compiler_params=pltpu.CompilerParams(vmem_limit_bytes=32 * MIB),
        )(*self.arrays)


def _call(body, *, name, grid, in_specs, out_specs, out_shape, args, vmem_mib, scratch_shapes=(), carried=None,
          aliases=None):
    n_in, n_out, n_scr = len(in_specs), len(out_specs), len(scratch_shapes)
    sem = ("arbitrary",) * len(grid)
    if carried is None:
        outs = _pallas(
            body, name=name, grid=grid, in_specs=in_specs, out_specs=out_specs, out_shape=out_shape,
            scratch_shapes=list(scratch_shapes), input_output_aliases=aliases or {},
            compiler_params=_params(sem, vmem_mib))(*args)
        return list(outs), []
    c_in, c_out = len(carried.arrays), len(carried.out_shapes)
    total = math.prod(grid)

    def full(*refs):
        own_in, car_in = refs[:n_in], refs[n_in:n_in + c_in]
        at = n_in + c_in
        own_out, car_out = refs[at:at + n_out], refs[at + n_out:at + n_out + c_out]
        at += n_out + c_out
        own_scr, sems = refs[at:at + n_scr], refs[at + n_scr:]
        step = pl.program_id(0)
        for axis in range(1, len(grid)):
            step = step * grid[axis] + pl.program_id(axis)
        start, middle, finish = carried.phases(car_in, car_out, sems)
        pl.when(step == 0)(start)
        pl.when(step == (3 * total) // 4)(middle)
        body(*own_in, *own_out, *own_scr)
        pl.when(step == total - 1)(finish)

    outs = _pallas(
        full, name=name, grid=grid, in_specs=list(in_specs) + [ANY] * c_in,
        out_specs=list(out_specs) + [ANY] * c_out, out_shape=list(out_shape) + list(carried.out_shapes),
        scratch_shapes=list(scratch_shapes) + list(carried.sem_shapes), input_output_aliases=aliases or {},
        compiler_params=_params(sem, vmem_mib))(*args, *carried.arrays)
    return list(outs[:n_out]), list(outs[n_out:])


def _all_gather(shards):
    n = len(shards)

    def phases(ins, outs, sems):
        send_sems, recv_sems, local_sems = sems
        x, y, c = lax.axis_index("x"), lax.axis_index("y"), lax.axis_index("c")
        me, sibling = (x, y, c), (x, y, 1 - c)
        chips = [(1 - x, y), (x, 1 - y), (1 - x, 1 - y)]

        def rows(m, px, py, pc):
            r = ins[m].shape[0]
            return outs[m].at[pl.ds((4 * px + 2 * py + pc) * r, r), :]

        def copy(m, k, block, to, src=None):
            return pltpu.make_async_remote_copy(
                src_ref=rows(m, *block) if src is None else src, dst_ref=rows(m, *block),
                send_sem=send_sems.at[m, k], recv_sem=recv_sems.at[m, k],
                device_id=to, device_id_type=MESH)

        def mine(m):
            return pltpu.make_async_copy(ins[m], rows(m, *me), local_sems.at[m])

        def first(m):
            return [copy(m, 0, me, sibling, src=ins[m])] + [
                copy(m, 1 + j, me, (*chip, c), src=ins[m]) for j, chip in enumerate(chips)]

        def passed(m):
            return [copy(m, 4 + j, (*chip, c), sibling) for j, chip in enumerate(chips)]

        def start():
            for m in range(n):
                mine(m).start()
            for m in range(n):
                for cp in first(m):
                    cp.start()

        def middle():
            for m in range(n):
                for j, chip in enumerate(chips):
                    copy(m, 1 + j, (*chip, c), me).wait_recv()
                    passed(m)[j].start()

        def finish():
            for m in range(n):
                copy(m, 0, sibling, me).wait_recv()
                for j, chip in enumerate(chips):
                    copy(m, 4 + j, (*chip, 1 - c), me).wait_recv()
            for m in range(n):
                for cp in first(m) + passed(m):
                    cp.wait_send()
                mine(m).wait()

        return start, middle, finish

    return _Exchange(
        shards, [_array((N_DEV * s.shape[0], s.shape[1]), s.dtype) for s in shards],
        [pltpu.SemaphoreType.DMA((n, 7)), pltpu.SemaphoreType.DMA((n, 7)), pltpu.SemaphoreType.DMA((n,))], phases)


def _pair_exchange(grads):
    n = len(grads)

    def phases(ins, outs, sems):
        send_sems, recv_sems = sems
        x, y, c = lax.axis_index("x"), lax.axis_index("y"), lax.axis_index("c")

        def copy(m):
            return pltpu.make_async_remote_copy(
                src_ref=ins[m].at[:, 1 - c], dst_ref=outs[m], send_sem=send_sems.at[m], recv_sem=recv_sems.at[m],
                device_id=(x, y, 1 - c), device_id_type=MESH)

        def start():
            for m in range(n):
                copy(m).start()

        def finish():
            for m in range(n):
                copy(m).wait()

        return start, lambda: None, finish

    return _Exchange(grads, [_array((4,) + g.shape[2:], g.dtype) for g in grads],
                     [pltpu.SemaphoreType.DMA((n,)), pltpu.SemaphoreType.DMA((n,))], phases)


def _chip_exchange(pair_sums):
    n = len(pair_sums)

    def phases(ins, outs, sems):
        send_sems, recv_sems = sems
        x, y, c = lax.axis_index("x"), lax.axis_index("y"), lax.axis_index("c")
        chips = [(1 - x, y), (x, 1 - y), (1 - x, 1 - y)]

        def copies():
            return [pltpu.make_async_remote_copy(
                src_ref=ins[m].at[2 * px + py], dst_ref=outs[m].at[j],
                send_sem=send_sems.at[m, j], recv_sem=recv_sems.at[m, j],
                device_id=(px, py, c), device_id_type=MESH)
                for m in range(n) for j, (px, py) in enumerate(chips)]

        def start():
            for cp in copies():
                cp.start()

        def finish():
            for cp in copies():
                cp.wait_recv()
            for cp in copies():
                cp.wait_send()

        return start, lambda: None, finish

    return _Exchange(pair_sums, [_array((3,) + p.shape[1:], p.dtype) for p in pair_sums],
                     [pltpu.SemaphoreType.DMA((n, 3)), pltpu.SemaphoreType.DMA((n, 3))], phases)


N_SLABS = ATTN_W // HEAD_PAIR


def _slab_scratch(tm, n=N_SLABS):
    return pltpu.VMEM((n, tm, HEAD_PAIR), F32)


def _rows_by_residue(dst_ref, slab_ref, r, tr, dtype, n=N_SLABS):
    for rho in range(r):
        for s in range(n):
            dst_ref[rho, :, s * HEAD_PAIR:(s + 1) * HEAD_PAIR] = (
                slab_ref[s, pl.ds(rho, tr, stride=r), :].astype(dtype))


def _rows_by_token(slab_ref, src_ref, r, tr, n=N_SLABS):
    for rho in range(r):
        for s in range(n):
            slab_ref[s, pl.ds(rho, tr, stride=r), :] = (
                src_ref[rho, :, s * HEAD_PAIR:(s + 1) * HEAD_PAIR].astype(F32))


def _norm1_and_rope_tables(x, g, pos_col, inv_freq_row, tm, carried=None):
    T = x.shape[0]

    def body(x_ref, g_ref, pos_ref, invf_ref, h_ref, cos_ref, s1_ref, s2_ref):
        xf = x_ref[...]
        r = lax.rsqrt(jnp.mean(xf * xf, axis=-1, keepdims=True) + EPS)
        h_ref[...] = (xf * r * g_ref[...]).astype(BF16)
        ang = pos_ref[...].astype(F32) * invf_ref[...]
        lane = lax.broadcasted_iota(jnp.int32, (1, HEAD_PAIR), 1) % 64
        cs, sn = jnp.cos(ang), jnp.sin(ang)
        cos_ref[...] = jnp.where(lane < 2 * ROPE_HALF, cs, 1.0)
        s1_ref[...] = jnp.where(lane < ROPE_HALF, -sn, 0.0)
        s2_ref[...] = jnp.where((lane >= ROPE_HALF) & (lane < 2 * ROPE_HALF), sn, 0.0)

    tab = _array((T, HEAD_PAIR), F32)
    row = pl.BlockSpec((tm, HEAD_PAIR), lambda i: (i, 0))
    wide = pl.BlockSpec((tm, D_MODEL), lambda i: (i, 0))
    return _call(
        body, name="norm1_rope_tables", grid=(T // tm,), out_shape=[_array((T, D_MODEL), BF16), tab, tab, tab],
        in_specs=[wide, pl.BlockSpec((1, D_MODEL), lambda i: (0, 0)), pl.BlockSpec((tm, 1), lambda i: (i, 0)),
                  pl.BlockSpec((1, HEAD_PAIR), lambda i: (0, 0))],
        out_specs=[wide, row, row, row], vmem_mib=24, args=(x, g, pos_col, inv_freq_row), carried=carried)


def _rope(y, cos, s1, s2):
    w = y.shape[1]
    rep = w // HEAD_PAIR
    return (y * jnp.tile(cos, (1, rep)) + pltpu.roll(y, w - ROPE_HALF, 1) * jnp.tile(s1, (1, rep))
            + pltpu.roll(y, ROPE_HALF, 1) * jnp.tile(s2, (1, rep)))


def _rope_transposed(dy, cos, s1, s2):
    w = dy.shape[1]
    rep = w // HEAD_PAIR
    return (dy * jnp.tile(cos, (1, rep)) + pltpu.roll(dy * jnp.tile(s1, (1, rep)), ROPE_HALF, 1)
            + pltpu.roll(dy * jnp.tile(s2, (1, rep)), w - ROPE_HALF, 1))


def _resident(shape, block=None):
    at = (0,) * len(shape) if block is None else block
    return pl.BlockSpec(shape, lambda *_: at, pipeline_mode=pl.Buffered(1))


def _in_proj(h, win_t, tables, tm, carried=None):
    T = h.shape[0]
    other = (IN_COLS - QKV_COLS) // ATTN_W

    def body(h_ref, w_ref, cos_ref, s1_ref, s2_ref, o0, o1, o2, uvg_ref, *slabs):
        hv = h_ref[...]
        cos, s1, s2 = cos_ref[...], s1_ref[...], s2_ref[...]
        for kind in range(3):
            for g, (o_ref, r) in enumerate(zip((o0, o1, o2), DILATIONS)):
                blk = 3 * kind + g
                if blk < other:
                    rows = slice(QKV_COLS + blk * ATTN_W, QKV_COLS + (blk + 1) * ATTN_W)
                    uvg_ref[:, blk * ATTN_W:(blk + 1) * ATTN_W] = _nt(hv, w_ref[rows, :]).astype(BF16)
                y = _nt(hv, w_ref[blk * ATTN_W:(blk + 1) * ATTN_W, :])
                if kind < 2:
                    y = _rope(y, cos, s1, s2)
                if kind == 0:
                    y = y * 0.125
                cols = slice(kind * ATTN_W, (kind + 1) * ATTN_W)
                if r == 1:
                    o_ref[0, :, cols] = y.astype(BF16)
                    continue
                slab = slabs[blk % len(slabs)]
                for s in range(N_SLABS):
                    slab[s] = y[:, s * HEAD_PAIR:(s + 1) * HEAD_PAIR]
                for rho in range(r):
                    for s in range(N_SLABS):
                        at = kind * ATTN_W + s * HEAD_PAIR
                        o_ref[rho, :, at:at + HEAD_PAIR] = slab[s, pl.ds(rho, tm // r, stride=r), :].astype(BF16)

    row = pl.BlockSpec((tm, HEAD_PAIR), lambda i: (i, 0))
    return _call(
        body, name="in_proj", grid=(T // tm,),
        out_shape=[_array((r, T // r, 3 * ATTN_W), BF16) for r in DILATIONS] + [_array((T, IN_COLS - QKV_COLS), BF16)],
        in_specs=[pl.BlockSpec((tm, D_MODEL), lambda i: (i, 0)), _resident((IN_COLS, D_MODEL)), row, row, row],
        out_specs=[pl.BlockSpec((r, tm // r, 3 * ATTN_W), lambda i: (0, i, 0)) for r in DILATIONS]
        + [pl.BlockSpec((tm, IN_COLS - QKV_COLS), lambda i: (i, 0))],
        scratch_shapes=[_slab_scratch(tm)] * 3, vmem_mib=56, args=(h, win_t, *tables), carried=carried)


def _band_mask(n):
    row = lax.broadcasted_iota(jnp.int32, (2 * BLK, 2 * BLK), 0) & (BLK - 1)
    col = lax.broadcasted_iota(jnp.int32, (2 * BLK, 2 * BLK), 1)
    has_prev = (jnp.zeros_like(row) + n) > 0
    return ((col < BLK) & (col >= row) & has_prev) | ((col >= BLK) & (col - BLK <= row))


def _head_lanes():
    lane = lax.broadcasted_iota(jnp.int32, (1, HEAD_PAIR), 1)
    return lane < 64, lane >= 64


def _stack_heads(x, head0, head1):
    zero = jnp.zeros_like(x)
    return jnp.concatenate([jnp.where(head0, x, zero), jnp.where(head1, x, zero)], axis=0)


def _unstack_heads(y, head0):
    return jnp.where(head0, y[:BLK], y[BLK:])


def _per_head(stats, col):
    return jnp.concatenate([stats[:, col:col + 1], stats[:, col + 1:col + 2]], axis=0)


def _attn_sub_blocks(length):
    for n in (8, 4, 2):
        if length % (n * BLK) == 0:
            return n
    return 1


def _attn_fwd(qkv, group, carried=None):
    r, L, _ = qkv.shape
    nsub = _attn_sub_blocks(L)
    rows = nsub * BLK

    def body(q_ref, kp_ref, kc_ref, vp_ref, vc_ref, o_ref, lse_ref):
        head0, head1 = _head_lanes()
        lane = lax.broadcasted_iota(jnp.int32, (1, HEAD_PAIR), 1)
        for j in range(nsub):
            cur = slice(j * BLK, (j + 1) * BLK)
            before = slice((j - 1) * BLK, j * BLK)
            valid = _band_mask(pl.program_id(1) + j)
            stats = jnp.zeros((BLK, HEAD_PAIR), F32)
            for p in range(N_SLABS):
                sl = slice(p * HEAD_PAIR, (p + 1) * HEAD_PAIR)
                k_prev, v_prev = (kp_ref[:, sl], vp_ref[:, sl]) if j == 0 else (kc_ref[before, sl], vc_ref[before, sl])
                k2 = jnp.concatenate([k_prev, kc_ref[cur, sl]], axis=0)
                v2 = jnp.concatenate([v_prev, vc_ref[cur, sl]], axis=0)
                s = jnp.where(valid, _nt(_stack_heads(q_ref[cur, sl], head0, head1), k2), MASKED)
                m = jnp.max(s, axis=1, keepdims=True)
                e = jnp.exp(s - m)
                den = jnp.sum(e, axis=1, keepdims=True)
                o_ref[cur, sl] = _unstack_heads(_nn(e.astype(BF16), v2) / den, head0).astype(BF16)
                lse = m + jnp.log(den)
                stats = jnp.where(lane == 2 * p, lse[:BLK], jnp.where(lane == 2 * p + 1, lse[BLK:], stats))
            lse_ref[cur, :] = stats

    def cur_blk(kind, width=ATTN_W):
        return pl.BlockSpec((None, rows, width), lambda rho, n: (rho, n, kind))

    def prev_blk(kind):
        return pl.BlockSpec((None, BLK, ATTN_W), lambda rho, n: (rho, jnp.maximum(n * nsub - 1, 0), kind))

    return _call(
        body, name=f"attn_fwd_g{group}", grid=(r, L // rows),
        out_shape=[_array((r, L, ATTN_W), BF16), _array((r, L, HEAD_PAIR), F32)],
        in_specs=[cur_blk(0), prev_blk(1), cur_blk(1), prev_blk(2), cur_blk(2)],
        out_specs=[cur_blk(0), cur_blk(0, HEAD_PAIR)],
        vmem_mib=24, args=(qkv, qkv, qkv, qkv, qkv), carried=carried)


def _gelu(x):
    return 0.5 * x * (1.0 + lax.erf(x * (1.0 / math.sqrt(2.0))))


def _gelu_grad(x):
    return 0.5 * (1.0 + lax.erf(x * (1.0 / math.sqrt(2.0)))) + x * jnp.exp(-0.5 * x * x) * (1.0 / math.sqrt(2.0 * math.pi))


def _causal():
    row = lax.broadcasted_iota(jnp.int32, (BLK, BLK), 0)
    col = lax.broadcasted_iota(jnp.int32, (BLK, BLK), 1)
    return col <= row


def _bias_lanes(bt):
    grp = lax.broadcasted_iota(jnp.int32, (1, SGU_W), 1) // 64
    out = jnp.zeros((BLK, SGU_W), F32)
    for g in range(8):
        out = jnp.where(grp == g, bt[:, g:g + 1], out)
    return out


def _sgu_normalise(uv, ln_g, ln_b):
    z = _gelu(uv)
    u, v = z[:, :SGU_W], z[:, SGU_W:]
    mu = jnp.mean(v, axis=-1, keepdims=True)
    xc = v - mu
    rstd = lax.rsqrt(jnp.mean(xc * xc, axis=-1, keepdims=True) + EPS)
    xh = xc * rstd
    return u, xh, rstd, xh * ln_g + ln_b


def _sgu_mix(wc_ref, vb, head0):
    chunks = []
    for ch in range(vb.shape[0] // BLK):
        pairs = []
        for p in range(SGU_W // HEAD_PAIR):
            v_pair = vb[ch * BLK:(ch + 1) * BLK, p * HEAD_PAIR:(p + 1) * HEAD_PAIR]
            pairs.append(jnp.where(head0, _nn(wc_ref[2 * p], v_pair), _nn(wc_ref[2 * p + 1], v_pair)))
        chunks.append(jnp.concatenate(pairs, axis=1))
    return jnp.concatenate(chunks, axis=0)


def _sgu_fwd(uvg, ln_g, ln_b, w_s, b_t, tm):
    T = uvg.shape[0]

    def body(uv_ref, g_ref, b_ref, w_ref, bt_ref, o_ref, wc_ref, bias_ref):
        @pl.when(pl.program_id(0) == 0)
        def _():
            causal = _causal()
            for g in range(8):
                wc_ref[g] = jnp.where(causal, w_ref[g], 0.0).astype(BF16)
            bias_ref[...] = _bias_lanes(bt_ref[...])

        u, _, _, vn = _sgu_normalise(uv_ref[...].astype(F32), g_ref[...], b_ref[...])
        mixed = _sgu_mix(wc_ref, vn.astype(BF16), _head_lanes()[0])
        o_ref[...] = (u * (mixed + jnp.tile(bias_ref[...], (tm // BLK, 1)))).astype(BF16)

    vec = pl.BlockSpec((1, SGU_W), lambda i: (0, 0))
    return _pallas(
        body, name="sgu_fwd", grid=(T // tm,), out_shape=_array((T, SGU_W), BF16),
        in_specs=[pl.BlockSpec((tm, 2 * SGU_W), lambda i: (i, 0)), vec, vec,
                  pl.BlockSpec((8, BLK, BLK), lambda i: (0, 0, 0)), pl.BlockSpec((BLK, 8), lambda i: (0, 0))],
        out_specs=pl.BlockSpec((tm, SGU_W), lambda i: (i, 0)),
        scratch_shapes=[pltpu.VMEM((8, BLK, BLK), BF16), pltpu.VMEM((BLK, SGU_W), F32)],
        compiler_params=_params(("arbitrary",), 32),
    )(uvg, ln_g, ln_b, w_s, b_t)


def _merge_fwd(outs, lses, sgu, wpa_t, wps_t, uvg, w_out, x, g2, tm):
    T = sgu.shape[0]

    def body(o0, l0, o1, l1, o2, l2, sgu_ref, wpa_ref, wps_ref, ga_ref, gb_ref, wo_ref, x_ref, g_ref,
             pa_ref, ps_ref, m_ref, h_ref, attn_b_ref, x1_ref, attn_ref, lse_ref, so1, sl1, so2, sl2):
        for o_in, l_in, so, sl, r in ((o1, l1, so1, sl1, DILATIONS[1]), (o2, l2, so2, sl2, DILATIONS[2])):
            _rows_by_token(so, o_in, r, tm // r)
            _rows_by_token(sl, l_in, r, tm // r, n=1)
        head0, _ = _head_lanes()
        a0, a1, a2 = l0[0], sl1[0], sl2[0]
        mx = jnp.maximum(jnp.maximum(a0, a1), a2)
        e0, e1, e2 = jnp.exp(a0 - mx), jnp.exp(a1 - mx), jnp.exp(a2 - mx)
        tot = e0 + e1 + e2
        lse_ref[...] = mx + jnp.log(tot)
        w0, w1, w2 = e0 / tot, e1 / tot, e2 / tot
        for s in range(N_SLABS):
            lanes = slice(s * HEAD_PAIR, (s + 1) * HEAD_PAIR)

            def lanes_of(w):
                return jnp.where(head0, w[:, 2 * s:2 * s + 1], w[:, 2 * s + 1:2 * s + 2])

            mixed = lanes_of(w0) * o0[0, :, lanes] + lanes_of(w1) * so1[s] + lanes_of(w2) * so2[s]
            attn_ref[:, lanes] = mixed
            attn_b_ref[:, lanes] = mixed.astype(BF16)

        pa = _nt(attn_b_ref[...], wpa_ref[...])
        ps = _nt(sgu_ref[...], wps_ref[...])
        pa_ref[...] = pa.astype(BF16)
        ps_ref[...] = ps.astype(BF16)
        ga, gb = ga_ref[...].astype(F32), gb_ref[...].astype(F32)
        merged = (jax.nn.sigmoid(ga) * pa + jax.nn.sigmoid(gb) * ps).astype(BF16)
        m_ref[...] = merged
        x1 = x_ref[...] + _nn(merged, wo_ref[...])
        x1_ref[...] = x1
        r = lax.rsqrt(jnp.mean(x1 * x1, axis=-1, keepdims=True) + EPS)
        h_ref[...] = (x1 * r * g_ref[...]).astype(BF16)

    ins, specs = [], []
    for g, r in enumerate(DILATIONS):
        ins += [outs[g], lses[g]]
        specs += [pl.BlockSpec((r, tm // r, ATTN_W), lambda i: (0, i, 0)),
                  pl.BlockSpec((r, tm // r, HEAD_PAIR), lambda i: (0, i, 0))]
    half = pl.BlockSpec((tm, ATTN_W), lambda i: (i, 0))
    wide = pl.BlockSpec((tm, D_MODEL), lambda i: (i, 0))
    w = _resident((D_MODEL, ATTN_W))
    res = _array((T, D_MODEL), BF16)
    return _pallas(
        body, name="merge_fwd", grid=(T // tm,),
        out_shape=[res, res, res, res, _array((T, ATTN_W), BF16), _array((T, D_MODEL), F32),
                   _array((T, ATTN_W), F32), _array((T, HEAD_PAIR), F32)],
        in_specs=specs + [half, w, w, pl.BlockSpec((tm, D_MODEL), lambda i: (i, 1)),
                          pl.BlockSpec((tm, D_MODEL), lambda i: (i, 2)), _resident((D_MODEL, D_MODEL)), wide,
                          pl.BlockSpec((1, D_MODEL), lambda i: (0, 0))],
        out_specs=[wide] * 4 + [half, wide, half, pl.BlockSpec((tm, HEAD_PAIR), lambda i: (i, 0))],
        scratch_shapes=[_slab_scratch(tm), _slab_scratch(tm, 1), _slab_scratch(tm), _slab_scratch(tm, 1)],
        compiler_params=_params(("parallel",), 56),
    )(*ins, sgu, wpa_t, wps_t, uvg, uvg, w_out, x, g2)


def _ffn_fwd(h2, wg_t, wu_t, tm):
    T = h2.shape[0]

    def body(h_ref, wg_ref, wu_ref, gate_ref, up_ref, ff_ref):
        h = h_ref[...]
        for j in range(D_FF // FF_TILE):
            cols = slice(j * FF_TILE, (j + 1) * FF_TILE)
            gate, up = _nt(h, wg_ref[cols, :]), _nt(h, wu_ref[cols, :])
            gate_ref[:, cols] = gate.astype(BF16)
            up_ref[:, cols] = up.astype(BF16)
            ff_ref[:, cols] = (gate * jax.nn.sigmoid(gate) * up).astype(BF16)

    w = _resident((D_FF, D_MODEL))
    o = pl.BlockSpec((tm, D_FF), lambda i: (i, 0))
    res = _array((T, D_FF), BF16)
    return _pallas(
        body, name="ffn_fwd", grid=(T // tm,), out_shape=[res, res, res],
        in_specs=[pl.BlockSpec((tm, D_MODEL), lambda i: (i, 0)), w, w], out_specs=[o, o, o],
        compiler_params=_params(("parallel",), 52),
    )(h2, wg_t, wu_t)


def _down_loss(ff, w_down, x1, final_g, target, tm):
    T = x1.shape[0]

    def body(ff_ref, w_ref, x1_ref, g_ref, t_ref, dx_ref, dxb_ref, loss_ref, dg_ref):
        @pl.when(pl.program_id(0) == 0)
        def _():
            loss_ref[...] = jnp.zeros_like(loss_ref)
            dg_ref[...] = jnp.zeros_like(dg_ref)

        x2 = x1_ref[...] + _nn(ff_ref[...], w_ref[...])
        g = g_ref[...]
        r = lax.rsqrt(jnp.mean(x2 * x2, axis=-1, keepdims=True) + EPS)
        xh = x2 * r
        err = xh * g - t_ref[...]
        loss_ref[...] += jnp.sum(err * err, axis=0, keepdims=True) * (0.5 / D_MODEL)
        dy = err * (1.0 / D_MODEL)
        dg_ref[...] += jnp.sum(dy * xh, axis=0, keepdims=True)
        dxh = dy * g
        dx = r * (dxh - xh * jnp.mean(dxh * xh, axis=-1, keepdims=True))
        dx_ref[...] = dx
        dxb_ref[...] = dx.astype(BF16)

    wide = pl.BlockSpec((tm, D_MODEL), lambda i: (i, 0))
    vec = pl.BlockSpec((1, D_MODEL), lambda i: (0, 0))
    vec_shape = _array((1, D_MODEL), F32)
    return _pallas(
        body, name="down_loss", grid=(T // tm,),
        out_shape=[_array((T, D_MODEL), F32), _array((T, D_MODEL), BF16),
                   vec_shape, vec_shape],
        in_specs=[pl.BlockSpec((tm, D_FF), lambda i: (i, 0)), _resident((D_FF, D_MODEL)), wide, vec, wide],
        out_specs=[wide, wide, vec, vec],
        compiler_params=_params(("arbitrary",), 40),
    )(ff, w_down, x1, final_g, target)


def _rmsnorm_bwd(dh, xin, g, d_res, dg_ref):
    r = lax.rsqrt(jnp.mean(xin * xin, axis=-1, keepdims=True) + EPS)
    xh = xin * r
    dg_ref[...] += jnp.sum(dh * xh, axis=0, keepdims=True)
    dxh = dh * g
    return d_res + r * (dxh - xh * jnp.mean(dxh * xh, axis=-1, keepdims=True))


def _ffn_bwd(dx2, dx2b, w_down, wg_t, wu_t, gate, up, x1, g2, tm):
    T = dx2.shape[0]

    def body(dx_ref, dxb_ref, wd_ref, wg_ref, wu_ref, gate_ref, up_ref, x1_ref, g_ref,
             dgate_ref, dup_ref, dx1_ref, dx1b_ref, dg_ref):
        @pl.when(pl.program_id(0) == 0)
        def _():
            dg_ref[...] = jnp.zeros_like(dg_ref)

        dxb = dxb_ref[...]
        dh = jnp.zeros((tm, D_MODEL), F32)
        for j in range(D_FF // FF_TILE):
            cols = slice(j * FF_TILE, (j + 1) * FF_TILE)
            dff = _nt(dxb, wd_ref[cols, :])
            gate, up = gate_ref[:, cols].astype(F32), up_ref[:, cols].astype(F32)
            sg = jax.nn.sigmoid(gate)
            dgate = (dff * up * sg * (1.0 + gate * (1.0 - sg))).astype(BF16)
            dup = (dff * gate * sg).astype(BF16)
            dgate_ref[:, cols] = dgate
            dup_ref[:, cols] = dup
            dh += _nn(dgate, wg_ref[cols, :]) + _nn(dup, wu_ref[cols, :])
        dx1 = _rmsnorm_bwd(dh, x1_ref[...], g_ref[...], dx_ref[...], dg_ref)
        dx1_ref[...] = dx1
        dx1b_ref[...] = dx1.astype(BF16)

    wide = pl.BlockSpec((tm, D_MODEL), lambda i: (i, 0))
    ffw = pl.BlockSpec((tm, D_FF), lambda i: (i, 0))
    vec = pl.BlockSpec((1, D_MODEL), lambda i: (0, 0))
    w = _resident((D_FF, D_MODEL))
    ff_shape = _array((T, D_FF), BF16)
    return _pallas(
        body, name="ffn_bwd", grid=(T // tm,),
        out_shape=[ff_shape, ff_shape, _array((T, D_MODEL), F32),
                   _array((T, D_MODEL), BF16), _array((1, D_MODEL), F32)],
        in_specs=[wide, wide, w, w, w, ffw, ffw, wide, vec],
        out_specs=[ffw, ffw, wide, wide, vec], compiler_params=_params(("arbitrary",), 56),
    )(dx2, dx2b, w_down, wg_t, wu_t, gate, up, x1, g2)


def _mm_tn(a, b, *, tmm, tk, name, into=None, row_block0=0, rows_total=None):
    T, M = a.shape
    N = b.shape[1]
    rows_total = M if rows_total is None else rows_total

    def body(*refs):
        a_ref, b_ref, o_ref = refs[0], refs[1], refs[-1]

        @pl.when(pl.program_id(1) == 0)
        def _():
            o_ref[...] = jnp.zeros_like(o_ref)

        o_ref[...] += _tn(a_ref[...], b_ref[...])

    ins = [a, b] + ([] if into is None else [into])
    specs = [pl.BlockSpec((tk, tmm), lambda i, k: (k, i)), pl.BlockSpec((tk, N), lambda i, k: (k, 0))]
    return _pallas(
        body, name=name, grid=(M // tmm, T // tk),
        out_shape=_array((rows_total, N), F32),
        in_specs=specs + ([] if into is None else [ANY]),
        out_specs=pl.BlockSpec((tmm, N), lambda i, k: (row_block0 + i, 0)),
        input_output_aliases={} if into is None else {2: 0},
        compiler_params=_params(("parallel", "arbitrary"), 48),
    )(*ins)


def _in_bwd(dqkv, dug, win_t, x, g1, dx1, tm, carried=None):
    T = x.shape[0]

    def body(dq_ref, dug_ref, w_ref, x_ref, g_ref, d_ref, dx_ref, dg_ref):
        @pl.when(pl.program_id(0) == 0)
        def _():
            dg_ref[...] = jnp.zeros_like(dg_ref)

        dh = _nn(dq_ref[...], w_ref[:QKV_COLS, :]) + _nn(dug_ref[...], w_ref[QKV_COLS:, :])
        dx_ref[...] = _rmsnorm_bwd(dh, x_ref[...], g_ref[...], d_ref[...], dg_ref)

    def cols(n):
        return pl.BlockSpec((tm, n), lambda i: (i, 0))

    wide = cols(D_MODEL)
    vec = pl.BlockSpec((1, D_MODEL), lambda i: (0, 0))
    return _call(
        body, name="in_bwd", grid=(T // tm,),
        out_shape=[_array((T, D_MODEL), F32), _array((1, D_MODEL), F32)],
        in_specs=[cols(QKV_COLS), cols(IN_COLS - QKV_COLS), _resident((IN_COLS, D_MODEL)), wide, vec, wide],
        out_specs=[wide, vec], vmem_mib=56, args=(dqkv, dug, win_t, x, g1, dx1), carried=carried)


D_LANE0 = 8


def _merge_bwd(dx1b, w_out, pa, ps, uvg, wpa_t, wps_t, attn, lse, tm, carried=None):
    T = dx1b.shape[0]

    def body(dx_ref, w_ref, pa_ref, ps_ref, ga_ref, gb_ref, wpa_ref, wps_ref, at_ref, lse_ref,
             dpa_ref, dps_ref, dug_ref, dsgu_ref, *rest):
        outs, (scr_da, scr_st) = rest[:-2], rest[-2:]
        dm = _nt(dx_ref[...], w_ref[...])
        ga, gb = jax.nn.sigmoid(ga_ref[...].astype(F32)), jax.nn.sigmoid(gb_ref[...].astype(F32))
        dpa, dps = (dm * ga).astype(BF16), (dm * gb).astype(BF16)
        dpa_ref[...] = dpa
        dps_ref[...] = dps
        dug_ref[:, 2 * SGU_W:2 * SGU_W + D_MODEL] = (dm * pa_ref[...].astype(F32) * ga * (1.0 - ga)).astype(BF16)
        dug_ref[:, 2 * SGU_W + D_MODEL:] = (dm * ps_ref[...].astype(F32) * gb * (1.0 - gb)).astype(BF16)
        dsgu_ref[...] = _nn(dps, wps_ref[...])
        dattn = _nn(dpa, wpa_ref[...])

        head0, _ = _head_lanes()
        lane = lax.broadcasted_iota(jnp.int32, (1, HEAD_PAIR), 1)
        stats = lse_ref[...]
        for s in range(N_SLABS):
            lanes = slice(s * HEAD_PAIR, (s + 1) * HEAD_PAIR)
            da = dattn[:, lanes]
            pp = da * at_ref[:, lanes]
            d0 = jnp.sum(jnp.where(head0, pp, 0.0), axis=1, keepdims=True)
            d1 = jnp.sum(jnp.where(head0, 0.0, pp), axis=1, keepdims=True)
            stats = jnp.where(lane == D_LANE0 + 2 * s, d0, jnp.where(lane == D_LANE0 + 2 * s + 1, d1, stats))
            scr_da[s] = da
            outs[0][0, :, lanes] = da.astype(BF16)
        scr_st[0] = stats
        outs[1][0] = stats
        for g, r in enumerate(DILATIONS):
            if r > 1:
                _rows_by_residue(outs[2 * g], scr_da, r, tm // r, BF16)
                _rows_by_residue(outs[2 * g + 1], scr_st, r, tm // r, F32, n=1)

    wide = pl.BlockSpec((tm, D_MODEL), lambda i: (i, 0))
    half = pl.BlockSpec((tm, ATTN_W), lambda i: (i, 0))
    w = _resident((D_MODEL, ATTN_W))
    res = _array((T, D_MODEL), BF16)
    shapes, specs = [], []
    for r in DILATIONS:
        shapes += [_array((r, T // r, ATTN_W), BF16), _array((r, T // r, HEAD_PAIR), F32)]
        specs += [pl.BlockSpec((r, tm // r, ATTN_W), lambda i: (0, i, 0)),
                  pl.BlockSpec((r, tm // r, HEAD_PAIR), lambda i: (0, i, 0))]
    return _call(
        body, name="merge_bwd", grid=(T // tm,),
        out_shape=[res, res, _array((T, IN_COLS - QKV_COLS), BF16), _array((T, ATTN_W), F32)] + shapes,
        in_specs=[wide, _resident((D_MODEL, D_MODEL)), wide, wide,
                  pl.BlockSpec((tm, D_MODEL), lambda i: (i, 1)), pl.BlockSpec((tm, D_MODEL), lambda i: (i, 2)), w, w,
                  half, pl.BlockSpec((tm, HEAD_PAIR), lambda i: (i, 0))],
        out_specs=[wide, wide, pl.BlockSpec((tm, IN_COLS - QKV_COLS), lambda i: (i, 0)), half] + specs,
        scratch_shapes=[_slab_scratch(tm), _slab_scratch(tm, 1)],
        vmem_mib=56, args=(dx1b, w_out, pa, ps, uvg, uvg, wpa_t, wps_t, attn, lse), carried=carried)


def _sgu_bwd(uvg, dsgu, ln_g, ln_b, w_s, b_t, dug, tm, carried=None):
    T = uvg.shape[0]
    nsteps = T // tm

    def body(uv_ref, ds_ref, g_ref, b_ref, w_ref, bt_ref, _, duv_ref, dw_ref, dbt_ref, dg_ref, db_ref,
             wc_ref, wct_ref, bias_ref, dbias_ref):
        step = pl.program_id(0)
        head0, head1 = _head_lanes()

        @pl.when(step == 0)
        def _():
            causal = _causal()
            for g in range(8):
                wc = jnp.where(causal, w_ref[g], 0.0)
                wc_ref[g] = wc.astype(BF16)
                wct_ref[g] = wc.T.astype(BF16)
            bias_ref[...] = _bias_lanes(bt_ref[...])
            dbias_ref[...] = jnp.zeros_like(dbias_ref)
            dw_ref[...] = jnp.zeros_like(dw_ref)
            dg_ref[...] = jnp.zeros_like(dg_ref)
            db_ref[...] = jnp.zeros_like(db_ref)

        uv = uv_ref[...].astype(F32)
        ln_gain = g_ref[...]
        u, xh, rstd, vn = _sgu_normalise(uv, ln_gain, b_ref[...])
        vb = vn.astype(BF16)
        mixed = _sgu_mix(wc_ref, vb, head0) + jnp.tile(bias_ref[...], (tm // BLK, 1))
        dout = ds_ref[...]
        du = dout * mixed
        dmixed = dout * u
        dmb = dmixed.astype(BF16)
        dvn_chunks = []
        for ch in range(tm // BLK):
            rows = slice(ch * BLK, (ch + 1) * BLK)
            dbias_ref[...] += dmixed[rows]
            pairs = []
            for p in range(SGU_W // HEAD_PAIR):
                lanes = slice(p * HEAD_PAIR, (p + 1) * HEAD_PAIR)
                dm_pair, v_pair = dmb[rows, lanes], vb[rows, lanes]
                acc = jnp.zeros((BLK, HEAD_PAIR), F32)
                for hh, half in enumerate((head0, head1)):
                    dm_h = jnp.where(half, dm_pair, jnp.zeros_like(dm_pair))
                    dw_ref[2 * p + hh] += _nt(dm_h, v_pair)
                    acc += _nn(wct_ref[2 * p + hh], dm_h)
                pairs.append(acc)
            dvn_chunks.append(jnp.concatenate(pairs, axis=1))
        dvn = jnp.concatenate(dvn_chunks, axis=0)
        dg_ref[...] += jnp.sum(dvn * xh, axis=0, keepdims=True)
        db_ref[...] += jnp.sum(dvn, axis=0, keepdims=True)
        dxh = dvn * ln_gain
        dv = rstd * (dxh - jnp.mean(dxh, axis=-1, keepdims=True) - xh * jnp.mean(dxh * xh, axis=-1, keepdims=True))
        dgelu = _gelu_grad(uv)
        duv_ref[:, :SGU_W] = (du * dgelu[:, :SGU_W]).astype(BF16)
        duv_ref[:, SGU_W:] = (dv * dgelu[:, SGU_W:]).astype(BF16)

        @pl.when(step == nsteps - 1)
        def _():
            causal = _causal()
            for g in range(8):
                dw_ref[g] = jnp.where(causal, dw_ref[g], 0.0)
            grp = lax.broadcasted_iota(jnp.int32, (1, SGU_W), 1) // 64
            col = lax.broadcasted_iota(jnp.int32, (1, 8), 1)
            dbias = dbias_ref[...]
            out = jnp.zeros((BLK, 8), F32)
            for g in range(8):
                s = jnp.sum(jnp.where(grp == g, dbias, 0.0), axis=1, keepdims=True)
                out = jnp.where(col == g, s, out)
            dbt_ref[...] = out

    vec = pl.BlockSpec((1, SGU_W), lambda i: (0, 0))
    w3 = pl.BlockSpec((8, BLK, BLK), lambda i: (0, 0, 0))
    bt = pl.BlockSpec((BLK, 8), lambda i: (0, 0))
    return _call(
        body, name="sgu_bwd", grid=(nsteps,),
        out_shape=[_array(dug.shape, BF16), _array((8, BLK, BLK), F32),
                   _array((BLK, 8), F32), _array((1, SGU_W), F32),
                   _array((1, SGU_W), F32)],
        in_specs=[pl.BlockSpec((tm, 2 * SGU_W), lambda i: (i, 0)), pl.BlockSpec((tm, SGU_W), lambda i: (i, 0)),
                  vec, vec, w3, bt, ANY],
        out_specs=[pl.BlockSpec((tm, 2 * SGU_W), lambda i: (i, 0)), w3, bt, vec, vec],
        scratch_shapes=[pltpu.VMEM((8, BLK, BLK), BF16), pltpu.VMEM((8, BLK, BLK), BF16),
                        pltpu.VMEM((BLK, SGU_W), F32), pltpu.VMEM((BLK, SGU_W), F32)],
        vmem_mib=40, args=(uvg, dsgu, ln_g, ln_b, w_s, b_t, dug), carried=carried, aliases={6: 0})


def _attn_bwd(qkv, dattn, stats, group, carried=None):
    r, L, _ = qkv.shape
    nsub = _attn_sub_blocks(L)
    rows = nsub * BLK
    nb = L // rows
    keep = rows - BLK
    whole = nb == 1

    def body(q_ref, kp_ref, kc_ref, vp_ref, vc_ref, da_ref, st_ref, dq_ref, dk_ref, dv_ref, carry_k, carry_v):
        n = pl.program_id(1)

        if not whole:
            @pl.when(n == 0)
            def _():
                carry_k[...] = jnp.zeros_like(carry_k)
                carry_v[...] = jnp.zeros_like(carry_v)

        @pl.when(n < nb)
        def _():
            head0, head1 = _head_lanes()
            for p in range(N_SLABS):
                sl = slice(p * HEAD_PAIR, (p + 1) * HEAD_PAIR)
                dk_parts = [jnp.zeros((BLK, HEAD_PAIR), F32) for _ in range(nsub + 1)]
                dv_parts = [jnp.zeros((BLK, HEAD_PAIR), F32) for _ in range(nsub + 1)]
                for j in range(nsub):
                    cur = slice(j * BLK, (j + 1) * BLK)
                    before = slice((j - 1) * BLK, j * BLK)
                    valid = _band_mask(n + j)
                    st = st_ref[cur, :]
                    k_prev, v_prev = (kp_ref[:, sl], vp_ref[:, sl]) if j == 0 else (kc_ref[before, sl], vc_ref[before, sl])
                    k2 = jnp.concatenate([k_prev, kc_ref[cur, sl]], axis=0)
                    v2 = jnp.concatenate([v_prev, vc_ref[cur, sl]], axis=0)
                    qs = _stack_heads(q_ref[cur, sl], head0, head1)
                    das = _stack_heads(da_ref[cur, sl], head0, head1)
                    prob = jnp.where(valid, jnp.exp(_nt(qs, k2) - _per_head(st, 2 * p)), 0.0)
                    ds = (prob * (_nt(das, v2) - _per_head(st, D_LANE0 + 2 * p))).astype(BF16)
                    dk2 = _tn(ds, qs)
                    dv2 = _tn(prob.astype(BF16), das)
                    dq_ref[cur, sl] = _unstack_heads(_nn(ds, k2), head0).astype(BF16)
                    dk_parts[j] += dk2[:BLK]
                    dk_parts[j + 1] += dk2[BLK:]
                    dv_parts[j] += dv2[:BLK]
                    dv_parts[j + 1] += dv2[BLK:]
                if whole:
                    for j in range(nsub):
                        dk_ref[j * BLK:(j + 1) * BLK, sl] = dk_parts[j + 1].astype(BF16)
                        dv_ref[j * BLK:(j + 1) * BLK, sl] = dv_parts[j + 1].astype(BF16)
                    continue
                if keep:
                    dk_ref[:keep, sl] = carry_k[:keep, sl].astype(BF16)
                    dv_ref[:keep, sl] = carry_v[:keep, sl].astype(BF16)
                dk_ref[keep:, sl] = (carry_k[keep:, sl] + dk_parts[0]).astype(BF16)
                dv_ref[keep:, sl] = (carry_v[keep:, sl] + dv_parts[0]).astype(BF16)
                for j in range(nsub):
                    carry_k[j * BLK:(j + 1) * BLK, sl] = dk_parts[j + 1]
                    carry_v[j * BLK:(j + 1) * BLK, sl] = dv_parts[j + 1]

        if not whole:
            @pl.when(n == nb)
            def _():
                dk_ref[...] = carry_k[...].astype(BF16)
                dv_ref[...] = carry_v[...].astype(BF16)

    def cur_blk(kind, width=ATTN_W):
        return pl.BlockSpec((None, rows, width), lambda rho, n: (rho, jnp.minimum(n, nb - 1), kind))

    def last_blk(kind):
        return pl.BlockSpec((None, rows, ATTN_W), lambda rho, n: (rho, jnp.clip(n - 1, 0, nb - 1), kind))

    def prev_keys(kind):
        return pl.BlockSpec((None, BLK, ATTN_W),
                            lambda rho, n: (rho, jnp.clip(n * nsub - 1, 0, nb * nsub - 1), kind))

    res = _array((r, L, ATTN_W), BF16)
    return _call(
        body, name=f"attn_bwd_g{group}", grid=(r, 1 if whole else nb + 1),
        out_shape=[res, res, res],
        in_specs=[cur_blk(0), prev_keys(1), cur_blk(1), prev_keys(2), cur_blk(2), cur_blk(0), cur_blk(0, HEAD_PAIR)],
        out_specs=[cur_blk(0), last_blk(0), last_blk(0)],
        scratch_shapes=[pltpu.VMEM((rows, ATTN_W), F32), pltpu.VMEM((rows, ATTN_W), F32)],
        vmem_mib=32, args=(qkv, qkv, qkv, qkv, qkv, dattn, stats), carried=carried)


def _dqkv_token_order(dqkv_groups, tables, tm):
    T = tables[0].shape[0]

    def body(*refs):
        ins = refs[:9]
        cos_ref, s1_ref, s2_ref, o_ref, scr = refs[9:]
        cos, s1, s2 = cos_ref[...], s1_ref[...], s2_ref[...]
        for g, r in enumerate(DILATIONS):
            for kind in range(3):
                src = ins[3 * g + kind]
                if r > 1:
                    _rows_by_token(scr, src, r, tm // r)
                for s in range(N_SLABS):
                    val = scr[s] if r > 1 else src[0, :, s * HEAD_PAIR:(s + 1) * HEAD_PAIR].astype(F32)
                    if kind < 2:
                        val = _rope_transposed(val, cos, s1, s2)
                    if kind == 0:
                        val = val * 0.125
                    at = (3 * kind + g) * ATTN_W + s * HEAD_PAIR
                    o_ref[:, at:at + HEAD_PAIR] = val.astype(BF16)

    specs = []
    for r in DILATIONS:
        specs += [pl.BlockSpec((r, tm // r, ATTN_W), lambda i: (0, i, 0))] * 3
    row = pl.BlockSpec((tm, HEAD_PAIR), lambda i: (i, 0))
    flat = [a for grp in dqkv_groups for a in grp]
    return _pallas(
        body, name="dqkv_token_order", grid=(T // tm,), out_shape=_array((T, QKV_COLS), BF16),
        in_specs=specs + [row] * 3, out_specs=pl.BlockSpec((tm, QKV_COLS), lambda i: (i, 0)),
        scratch_shapes=[_slab_scratch(tm)], compiler_params=_params(("parallel",), 48),
    )(*flat, *tables)


def _row_tile(rows):
    for cand in (320, 256, 176, 128):
        if rows % cand == 0:
            return cand
    return rows


def _pair_sum(grad4, recv, chip, name):
    _, _, rows, cols = grad4.shape
    tr = _row_tile(rows)

    def body(ids_ref, g_ref, r_ref, gown_ref, rown_ref, sum_ref, own_ref):
        sum_ref[...] = (g_ref[...] + r_ref[...]).astype(BF16)

        @pl.when(pl.program_id(1) == 0)
        def _():
            own_ref[...] = gown_ref[...] + rown_ref[...]

    grid_spec = pltpu.PrefetchScalarGridSpec(
        num_scalar_prefetch=1, grid=(rows // tr, 4),
        in_specs=[pl.BlockSpec((None, None, tr, cols), lambda i, q, ids: (q, ids[1], i, 0)),
                  pl.BlockSpec((None, tr, cols), lambda i, q, ids: (q, i, 0)),
                  pl.BlockSpec((None, None, tr, cols), lambda i, q, ids: (ids[0], ids[1], i, 0)),
                  pl.BlockSpec((None, tr, cols), lambda i, q, ids: (ids[0], i, 0))],
        out_specs=[pl.BlockSpec((None, tr, cols), lambda i, q, ids: (q, i, 0)),
                   pl.BlockSpec((tr, cols), lambda i, q, ids: (i, 0))])
    return _pallas(
        body, name=name, grid_spec=grid_spec,
        out_shape=[_array((4, rows, cols), BF16), _array((rows, cols), F32)],
        compiler_params=_params(("arbitrary", "arbitrary"), 32),
    )(chip, grad4, recv, grad4, recv)


def _chip_sum(own, others, name):
    rows, cols = own.shape
    tr = _row_tile(rows)

    def body(own_ref, oth_ref, o_ref):
        total = own_ref[...]
        for j in range(3):
            total = total + oth_ref[j].astype(F32)
        o_ref[...] = total

    blk = pl.BlockSpec((tr, cols), lambda i: (i, 0))
    return _pallas(
        body, name=name, grid=(rows // tr,), out_shape=_array((rows, cols), F32),
        in_specs=[blk, pl.BlockSpec((3, tr, cols), lambda i: (0, i, 0))], out_specs=blk,
        compiler_params=_params(("parallel",), 32),
    )(own, others)


def _adam_math(w, g, m, v):
    m = ADAM_B1 * m + (1.0 - ADAM_B1) * g
    v = ADAM_B2 * v + (1.0 - ADAM_B2) * (g * g)
    m_hat = m / (1.0 - ADAM_B1 ** ADAM_STEP)
    v_hat = v / (1.0 - ADAM_B2 ** ADAM_STEP)
    delta = -ADAM_LR * (m_hat / (jnp.sqrt(v_hat) + ADAM_EPS) + ADAM_WD * w)
    return delta, m, v


def _adamw(w, g, m, v, name):
    rows, cols = w.shape
    tr = _row_tile(rows)

    def body(w_ref, g_ref, m_ref, v_ref, d_ref, nm_ref, nv_ref):
        d_ref[...], nm_ref[...], nv_ref[...] = _adam_math(w_ref[...], g_ref[...], m_ref[...], v_ref[...])

    blk = pl.BlockSpec((tr, cols), lambda i: (i, 0))
    res = _array((rows, cols), F32)
    return _pallas(
        body, name=name, grid=(rows // tr,), out_shape=[res, res, res], in_specs=[blk] * 4, out_specs=[blk] * 3,
        compiler_params=_params(("parallel",), 32),
    )(w, g, m, v)


def _small_update(late_parts, early_parts, w, m, v, loss_rows):
    rows = w.shape[0]

    def total(p_ref):
        n = p_ref.shape[0] // N_DEV
        acc = p_ref[0:n, :]
        for dev in range(1, N_DEV):
            acc = acc + p_ref[dev * n:(dev + 1) * n, :]
        return acc

    def body(late_ref, early_ref, w_ref, m_ref, v_ref, g_ref, d_ref, nm_ref, nv_ref, loss_ref):
        g = jnp.concatenate([total(late_ref), total(early_ref)], axis=0)
        g_ref[...] = g
        d_ref[...], nm_ref[...], nv_ref[...] = _adam_math(w_ref[...], g, m_ref[...], v_ref[...])
        loss_ref[...] = jnp.sum(jnp.sum(g[rows - loss_rows:, :], axis=1, keepdims=True), axis=0, keepdims=True)

    res = jax.ShapeDtypeStruct((rows, HEAD_PAIR), F32)
    return _pallas(
        body, name="small_update", out_shape=[res, res, res, res, jax.ShapeDtypeStruct((1, 1), F32)],
        compiler_params=pltpu.CompilerParams(vmem_limit_bytes=32 * MIB),
    )(late_parts, early_parts, w, m, v)


def kernel(x, positions, norm1_g, w_in, sgu_ln_g, sgu_ln_b, w_spatial, b_spatial, w_proj_attn, w_proj_sgu, w_out, norm2_g, w_ffn_gate, w_ffn_up, w_ffn_down, final_g, loss_target, m_norm1_g, m_w_in, m_sgu_ln_g, m_sgu_ln_b, m_w_spatial, m_b_spatial, m_w_proj_attn, m_w_proj_sgu, m_w_out, m_norm2_g, m_w_ffn_gate, m_w_ffn_up, m_w_ffn_down, m_final_g, v_norm1_g, v_w_in, v_sgu_ln_g, v_sgu_ln_b, v_w_spatial, v_b_spatial, v_w_proj_attn, v_w_proj_sgu, v_w_out, v_norm2_g, v_w_ffn_gate, v_w_ffn_up, v_w_ffn_down, v_final_g):
    T = x.shape[1]
    tm = 512
    xt = x[0]
    target = loss_target[0]
    chip = jnp.stack([2 * lax.axis_index("x") + lax.axis_index("y"), lax.axis_index("c")]).astype(jnp.int32)

    def bf16_rows(w, transpose):
        return (jnp.transpose(w[0]) if transpose else w[0]).astype(BF16)

    w_in_gather = _all_gather([bf16_rows(w_in, True)])
    later = _all_gather([bf16_rows(w_ffn_gate, True), bf16_rows(w_ffn_up, True), bf16_rows(w_out, False),
                         bf16_rows(w_proj_attn, True), bf16_rows(w_proj_sgu, True), bf16_rows(w_ffn_down, False)])

    inv_freq = ROPE_THETA ** (-jnp.arange(0, 2 * ROPE_HALF, 2, dtype=F32) / (2 * ROPE_HALF))
    inv_freq_row = jnp.tile(jnp.concatenate([inv_freq, inv_freq, jnp.zeros((48,), F32)]), 2).reshape(1, HEAD_PAIR)
    b_t = jnp.transpose(b_spatial[0])

    (h, *tables), (win_t,) = _norm1_and_rope_tables(xt, norm1_g, positions.reshape(T, 1), inv_freq_row, tm,
                                                    carried=w_in_gather)
    (*qkv, uvg), (wg_t, wu_t, wout, wpa_t, wps_t, wd) = _in_proj(h, win_t, tables, tm, carried=later)
    fwd = [_attn_fwd(qkv[g], g)[0] for g in range(3)]
    sgu = _sgu_fwd(uvg, sgu_ln_g, sgu_ln_b, w_spatial[0], b_t, tm)
    pa, ps, merged, h2, attn_b, x1, attn, lse = _merge_fwd([f[0] for f in fwd], [f[1] for f in fwd], sgu, wpa_t, wps_t,
                                                           uvg, wout, xt, norm2_g, tm)
    gate, up, ff = _ffn_fwd(h2, wg_t, wu_t, tm)
    dx2, dx2b, loss_cols, d_final_g = _down_loss(ff, wd, x1, final_g.reshape(1, D_MODEL), target, tm)

    dgate, dup, dx1, dx1b, d_norm2 = _ffn_bwd(dx2, dx2b, wd, wg_t, wu_t, gate, up, x1, norm2_g, tm // 2)
    tk = min(2048, T)
    d_wd = _mm_tn(ff, dx2b, tmm=FF_TILE, tk=tk, name="grad_w_ffn_down")
    d_wg_t = _mm_tn(dgate, h2, tmm=FF_TILE, tk=tk, name="grad_w_ffn_gate")
    d_wu_t = _mm_tn(dup, h2, tmm=FF_TILE, tk=tk, name="grad_w_ffn_up")

    def by_owner(grads):
        return [g.reshape(4, 2, g.shape[0] // N_DEV, g.shape[1]) for g in grads]

    def pair_sums(grads4, from_sibling, names):
        both = [_pair_sum(g4, rv, chip, "grad_pair_sum_" + nm) for g4, rv, nm in zip(grads4, from_sibling, names)]
        return [b[0] for b in both], [b[1] for b in both]

    ffn_names = ["w_ffn_gate", "w_ffn_up", "w_ffn_down"]
    ffn4 = by_owner([d_wg_t, d_wu_t, d_wd])
    (dpa, dps, dug, dsgu, *prep), ffn_sib = _merge_bwd(dx1b, wout, pa, ps, uvg, wpa_t, wps_t, attn, lse, tm,
                                                       carried=_pair_exchange(ffn4))
    ffn_sums, ffn_own = pair_sums(ffn4, ffn_sib, ffn_names)

    d_wout = _mm_tn(merged, dx1b, tmm=D_MODEL, tk=tk, name="grad_w_out")
    d_wpa_t = _mm_tn(dpa, attn_b, tmm=D_MODEL, tk=tk, name="grad_w_proj_attn")
    d_wps_t = _mm_tn(dps, sgu, tmm=D_MODEL, tk=tk, name="grad_w_proj_sgu")
    mid_names = ["w_proj_attn", "w_proj_sgu", "w_out"]
    mid4 = by_owner([d_wpa_t, d_wps_t, d_wout])
    (dug, d_ws, d_bs_t, d_ln_g, d_ln_b), mid_sib = _sgu_bwd(uvg, dsgu, sgu_ln_g, sgu_ln_b, w_spatial[0], b_t, dug, tm,
                                                           carried=_pair_exchange(mid4))
    mid_sums, mid_own = pair_sums(mid4, mid_sib, mid_names)

    dqkv0, ffn_far = _attn_bwd(qkv[0], prep[0], prep[1], 0, carried=_chip_exchange(ffn_sums))
    dqkv1, mid_far = _attn_bwd(qkv[1], prep[2], prep[3], 1, carried=_chip_exchange(mid_sums))
    def flat(parts):
        return jnp.concatenate([p.reshape(-1) for p in parts]).reshape(-1, HEAD_PAIR)

    early_part = flat([d_ln_g, d_ln_b, d_ws, jnp.transpose(d_bs_t), d_norm2, d_final_g, loss_cols])
    dqkv2, (early_parts,) = _attn_bwd(qkv[2], prep[4], prep[5], 2, carried=_all_gather([early_part]))
    dqkv = _dqkv_token_order([dqkv0, dqkv1, dqkv2], tables, tm)
    d_win_t = _mm_tn(dqkv, h, tmm=1536, tk=tk, name="grad_w_in_qkv", rows_total=IN_COLS)
    d_win_t = _mm_tn(dug, h, tmm=1536, tk=tk, name="grad_w_in_uv_gates", into=d_win_t, row_block0=3, rows_total=IN_COLS)
    in4 = by_owner([d_win_t])
    in_sib = _pair_exchange(in4).run_alone("w_in_grad_pair_exchange")
    in_sums, in_own = pair_sums(in4, in_sib, ["w_in"])
    (dx, d_norm1), in_far = _in_bwd(dqkv, dug, win_t, xt, norm1_g, dx1, tm, carried=_chip_exchange(in_sums))

    names = ["w_in"] + mid_names + ffn_names
    reduced = [_chip_sum(o, f, "grad_total_" + nm)
               for o, f, nm in zip(in_own + mid_own + ffn_own, in_far + mid_far + ffn_far, names)]
    transposed = (True, True, True, False, True, True, False)
    g_big = [jnp.transpose(r) if t else r for r, t in zip(reduced, transposed)]

    small_w = [norm1_g, sgu_ln_g, sgu_ln_b, w_spatial, b_spatial, norm2_g, final_g]
    small_m = [m_norm1_g, m_sgu_ln_g, m_sgu_ln_b, m_w_spatial, m_b_spatial, m_norm2_g, m_final_g]
    small_v = [v_norm1_g, v_sgu_ln_g, v_sgu_ln_b, v_w_spatial, v_b_spatial, v_norm2_g, v_final_g]
    zeros = jnp.zeros((D_MODEL,), F32)
    (late_parts,) = _all_gather([flat([d_norm1])]).run_alone("norm1_grad_all_gather", vmem=True)
    g_s, d_s, nm_s, nv_s, loss = _small_update(late_parts, early_parts, flat(small_w + [zeros]), flat(small_m + [zeros]),
                                               flat(small_v + [zeros]), D_MODEL // HEAD_PAIR)

    def unflat(vec):
        vec = vec.reshape(-1)
        out, at = [], 0
        for wgt in small_w:
            out.append(vec[at:at + wgt.size].reshape(wgt.shape))
            at += wgt.size
        return out

    small = [unflat(a) for a in (g_s, d_s, nm_s, nv_s)]

    big_w = [w_in, w_proj_attn, w_proj_sgu, w_out, w_ffn_gate, w_ffn_up, w_ffn_down]
    big_m = [m_w_in, m_w_proj_attn, m_w_proj_sgu, m_w_out, m_w_ffn_gate, m_w_ffn_up, m_w_ffn_down]
    big_v = [v_w_in, v_w_proj_attn, v_w_proj_sgu, v_w_out, v_w_ffn_gate, v_w_ffn_up, v_w_ffn_down]
    big_out = []
    for wgt, g, mm, vv, nm in zip(big_w, g_big, big_m, big_v, names):
        d, nm_, nv_ = _adamw(wgt[0], g, mm[0], vv[0], "adamw_" + nm)
        big_out.append([a[None] for a in (g, d, nm_, nv_)])

    small_at = {0: 0, 2: 1, 3: 2, 4: 3, 5: 4, 9: 5, 13: 6}
    big_at = {1: 0, 6: 1, 7: 2, 8: 3, 10: 4, 11: 5, 12: 6}
    outs = [loss[0, 0], dx[None]]
    for kind in range(4):
        for idx in range(14):
            outs.append(small[kind][small_at[idx]] if idx in small_at else big_out[big_at[idx]][kind])
    return tuple(outs)
```

```python
import functools
import math

import jax
import jax.numpy as jnp
from jax import lax
from jax.experimental import pallas as pl
from jax.experimental.pallas import tpu as pltpu

F32 = jnp.float32
BF16 = jnp.bfloat16

D_MODEL = 1024
HEAD_PAIR = 128
ATTN_W = 512
DILATIONS = (1, 4, 16)
BLK = 128
ROPE_HALF = 8
ROPE_THETA = 500000.0
SGU_W = 512
QKV_COLS = 4608
IN_COLS = 7680
D_FF = 2816
FF_TILE = 1408
EPS = 1e-6
N_DEV = 8
MASKED = -1e30

ADAM_LR = 0.001
ADAM_B1 = 0.9
ADAM_B2 = 0.999
ADAM_EPS = 1e-08
ADAM_WD = 0.01
ADAM_STEP = 10

MIB = 1024 * 1024
MESH = pl.DeviceIdType.MESH
ANY = pl.BlockSpec(memory_space=pl.ANY)


def _array(shape, dtype):
    return pltpu.HBM(tuple(shape), dtype)


PIN_BYTES = 4 * MIB


def _pin(x):
    if x.size * x.dtype.itemsize < PIN_BYTES:
        return x
    return pltpu.with_memory_space_constraint(x, pltpu.HBM)


def _pallas(body, **kwargs):
    call = pl.pallas_call(body, **kwargs)
    return lambda *args: call(*[_pin(a) for a in args])


def _params(sem, vmem_mib):
    return pltpu.CompilerParams(dimension_semantics=sem, vmem_limit_bytes=vmem_mib * MIB)


def _nt(a, b):
    return lax.dot_general(a, b, (((1,), (1,)), ((), ())), preferred_element_type=F32)


def _nn(a, b):
    return lax.dot_general(a, b, (((1,), (0,)), ((), ())), preferred_element_type=F32)


def _tn(a, b):
    return lax.dot_general(a, b, (((0,), (0,)), ((), ())), preferred_element_type=F32)


class _Exchange:
    def __init__(self, arrays, out_shapes, sem_shapes, phases):
        self.arrays, self.out_shapes, self.sem_shapes, self.phases = list(arrays), out_shapes, sem_shapes, phases

    def run_alone(self, name, vmem=False):
        n_in, n_out = len(self.arrays), len(self.out_shapes)

        def body(*refs):
            start, middle, finish = self.phases(refs[:n_in], refs[n_in:n_in + n_out], refs[n_in + n_out:])
            start()
            middle()
            finish()

        spec = pl.BlockSpec(memory_space=pltpu.VMEM) if vmem else ANY
        shapes = [jax.ShapeDtypeStruct(s.shape, s.dtype) for s in self.out_shapes] if vmem else self.out_shapes
        return (pl.pallas_call if vmem else _pallas)(
            body, name=name, out_shape=shapes, in_specs=[spec] * n_in, out_specs=[spec] * n_out,
            scratch_shapes=self.sem_shapes, compiler_params=pltpu.CompilerParams(vmem_limit_bytes=32 * MIB),
        )(*self.arrays)


def _call(body, *, name, grid, in_specs, out_specs, out_shape, args, vmem_mib, scratch_shapes=(), carried=None,
          aliases=None):
    n_in, n_out, n_scr = len(in_specs), len(out_specs), len(scratch_shapes)
    sem = ("arbitrary",) * len(grid)
    if carried is None:
        outs = _pallas(
            body, name=name, grid=grid, in_specs=in_specs, out_specs=out_specs, out_shape=out_shape,
            scratch_shapes=list(scratch_shapes), input_output_aliases=aliases or {},
            compiler_params=_params(sem, vmem_mib))(*args)
        return list(outs), []
    c_in, c_out = len(carried.arrays), len(carried.out_shapes)
    total = math.prod(grid)

    def full(*refs):
        own_in, car_in = refs[:n_in], refs[n_in:n_in + c_in]
        at = n_in + c_in
        own_out, car_out = refs[at:at + n_out], refs[at + n_out:at + n_out + c_out]
        at += n_out + c_out
        own_scr, sems = refs[at:at + n_scr], refs[at + n_scr:]
        step = pl.program_id(0)
        for axis in range(1, len(grid)):
            step = step * grid[axis] + pl.program_id(axis)
        start, middle, finish = carried.phases(car_in, car_out, sems)
        pl.when(step == 0)(start)
        pl.when(step == (3 * total) // 4)(middle)
        body(*own_in, *own_out, *own_scr)
        pl.when(step == total - 1)(finish)

    outs = _pallas(
        full, name=name, grid=grid, in_specs=list(in_specs) + [ANY] * c_in,
        out_specs=list(out_specs) + [ANY] * c_out, out_shape=list(out_shape) + list(carried.out_shapes),
        scratch_shapes=list(scratch_shapes) + list(carried.sem_shapes), input_output_aliases=aliases or {},
        compiler_params=_params(sem, vmem_mib))(*args, *carried.arrays)
    return list(outs[:n_out]), list(outs[n_out:])


def _all_gather(shards):
    n = len(shards)

    def phases(ins, outs, sems):
        send_sems, recv_sems, local_sems = sems
        x, y, c = lax.axis_index("x"), lax.axis_index("y"), lax.axis_index("c")
        me, sibling = (x, y, c), (x, y, 1 - c)
        chips = [(1 - x, y), (x, 1 - y), (1 - x, 1 - y)]

        def rows(m, px, py, pc):
            r = ins[m].shape[0]
            return outs[m].at[pl.ds((4 * px + 2 * py + pc) * r, r), :]

        def copy(m, k, block, to, src=None):
            return pltpu.make_async_remote_copy(
                src_ref=rows(m, *block) if src is None else src, dst_ref=rows(m, *block),
                send_sem=send_sems.at[m, k], recv_sem=recv_sems.at[m, k],
                device_id=to, device_id_type=MESH)

        def mine(m):
            return pltpu.make_async_copy(ins[m], rows(m, *me), local_sems.at[m])

        def first(m):
            return [copy(m, 0, me, sibling, src=ins[m])] + [
                copy(m, 1 + j, me, (*chip, c), src=ins[m]) for j, chip in enumerate(chips)]

        def passed(m):
            return [copy(m, 4 + j, (*chip, c), sibling) for j, chip in enumerate(chips)]

        def start():
            for m in range(n):
                mine(m).start()
            for m in range(n):
                for cp in first(m):
                    cp.start()

        def middle():
            for m in range(n):
                for j, chip in enumerate(chips):
                    copy(m, 1 + j, (*chip, c), me).wait_recv()
                    passed(m)[j].start()

        def finish():
            for m in range(n):
                copy(m, 0, sibling, me).wait_recv()
                for j, chip in enumerate(chips):
                    copy(m, 4 + j, (*chip, 1 - c), me).wait_recv()
            for m in range(n):
                for cp in first(m) + passed(m):
                    cp.wait_send()
                mine(m).wait()

        return start, middle, finish

    return _Exchange(
        shards, [_array((N_DEV * s.shape[0], s.shape[1]), s.dtype) for s in shards],
        [pltpu.SemaphoreType.DMA((n, 7)), pltpu.SemaphoreType.DMA((n, 7)), pltpu.SemaphoreType.DMA((n,))], phases)


def _pair_exchange(grads):
    n = len(grads)

    def phases(ins, outs, sems):
        send_sems, recv_sems = sems
        x, y, c = lax.axis_index("x"), lax.axis_index("y"), lax.axis_index("c")

        def copy(m):
            return pltpu.make_async_remote_copy(
                src_ref=ins[m].at[:, 1 - c], dst_ref=outs[m], send_sem=send_sems.at[m], recv_sem=recv_sems.at[m],
                device_id=(x, y, 1 - c), device_id_type=MESH)

        def start():
            for m in range(n):
                copy(m).start()

        def finish():
            for m in range(n):
                copy(m).wait()

        return start, lambda: None, finish

    return _Exchange(grads, [_array((4,) + g.shape[2:], g.dtype) for g in grads],
                     [pltpu.SemaphoreType.DMA((n,)), pltpu.SemaphoreType.DMA((n,))], phases)


def _chip_exchange(pair_sums):
    n = len(pair_sums)

    def phases(ins, outs, sems):
        send_sems, recv_sems = sems
        x, y, c = lax.axis_index("x"), lax.axis_index("y"), lax.axis_index("c")
        chips = [(1 - x, y), (x, 1 - y), (1 - x, 1 - y)]

        def copies():
            return [pltpu.make_async_remote_copy(
                src_ref=ins[m].at[2 * px + py], dst_ref=outs[m].at[j],
                send_sem=send_sems.at[m, j], recv_sem=recv_sems.at[m, j],
                device_id=(px, py, c), device_id_type=MESH)
                for m in range(n) for j, (px, py) in enumerate(chips)]

        def start():
            for cp in copies():
                cp.start()

        def finish():
            for cp in copies():
                cp.wait_recv()
            for cp in copies():
                cp.wait_send()

        return start, lambda: None, finish

    return _Exchange(pair_sums, [_array((3,) + p.shape[1:], p.dtype) for p in pair_sums],
                     [pltpu.SemaphoreType.DMA((n, 3)), pltpu.SemaphoreType.DMA((n, 3))], phases)


N_SLABS = ATTN_W // HEAD_PAIR


def _slab_scratch(tm, n=N_SLABS):
    return pltpu.VMEM((n, tm, HEAD_PAIR), F32)


def _rows_by_residue(dst_ref, slab_ref, r, tr, dtype, n=N_SLABS):
    for rho in range(r):
        for s in range(n):
            dst_ref[rho, :, s * HEAD_PAIR:(s + 1) * HEAD_PAIR] = (
                slab_ref[s, pl.ds(rho, tr, stride=r), :].astype(dtype))


def _rows_by_token(slab_ref, src_ref, r, tr, n=N_SLABS):
    for rho in range(r):
        for s in range(n):
            slab_ref[s, pl.ds(rho, tr, stride=r), :] = (
                src_ref[rho, :, s * HEAD_PAIR:(s + 1) * HEAD_PAIR].astype(F32))


def _norm1_and_rope_tables(x, g, pos_col, inv_freq_row, tm, carried=None):
    T = x.shape[0]

    def body(x_ref, g_ref, pos_ref, invf_ref, h_ref, cos_ref, s1_ref, s2_ref):
        xf = x_ref[...]
        r = lax.rsqrt(jnp.mean(xf * xf, axis=-1, keepdims=True) + EPS)
        h_ref[...] = (xf * r * g_ref[...]).astype(BF16)
        ang = pos_ref[...].astype(F32) * invf_ref[...]
        lane = lax.broadcasted_iota(jnp.int32, (1, HEAD_PAIR), 1) % 64
        cs, sn = jnp.cos(ang), jnp.sin(ang)
        cos_ref[...] = jnp.where(lane < 2 * ROPE_HALF, cs, 1.0)
        s1_ref[...] = jnp.where(lane < ROPE_HALF, -sn, 0.0)
        s2_ref[...] = jnp.where((lane >= ROPE_HALF) & (lane < 2 * ROPE_HALF), sn, 0.0)

    tab = _array((T, HEAD_PAIR), F32)
    row = pl.BlockSpec((tm, HEAD_PAIR), lambda i: (i, 0))
    wide = pl.BlockSpec((tm, D_MODEL), lambda i: (i, 0))
    return _call(
        body, name="norm1_rope_tables", grid=(T // tm,), out_shape=[_array((T, D_MODEL), BF16), tab, tab, tab],
        in_specs=[wide, pl.BlockSpec((1, D_MODEL), lambda i: (0, 0)), pl.BlockSpec((tm, 1), lambda i: (i, 0)),
                  pl.BlockSpec((1, HEAD_PAIR), lambda i: (0, 0))],
        out_specs=[wide, row, row, row], vmem_mib=24, args=(x, g, pos_col, inv_freq_row), carried=carried)


def _rope(y, cos, s1, s2):
    w = y.shape[1]
    rep = w // HEAD_PAIR
    return (y * jnp.tile(cos, (1, rep)) + pltpu.roll(y, w - ROPE_HALF, 1) * jnp.tile(s1, (1, rep))
            + pltpu.roll(y, ROPE_HALF, 1) * jnp.tile(s2, (1, rep)))


def _rope_transposed(dy, cos, s1, s2):
    w = dy.shape[1]
    rep = w // HEAD_PAIR
    return (dy * jnp.tile(cos, (1, rep)) + pltpu.roll(dy * jnp.tile(s1, (1, rep)), ROPE_HALF, 1)
            + pltpu.roll(dy * jnp.tile(s2, (1, rep)), w - ROPE_HALF, 1))


def _resident(shape, block=None):
    at = (0,) * len(shape) if block is None else block
    return pl.BlockSpec(shape, lambda *_: at, pipeline_mode=pl.Buffered(1))


def _in_proj(h, win_t, tables, tm, carried=None):
    T = h.shape[0]
    other = (IN_COLS - QKV_COLS) // ATTN_W

    def body(h_ref, w_ref, cos_ref, s1_ref, s2_ref, o0, o1, o2, uvg_ref, *slabs):
        hv = h_ref[...]
        cos, s1, s2 = cos_ref[...], s1_ref[...], s2_ref[...]
        for kind in range(3):
            for g, (o_ref, r) in enumerate(zip((o0, o1, o2), DILATIONS)):
                blk = 3 * kind + g
                if blk < other:
                    rows = slice(QKV_COLS + blk * ATTN_W, QKV_COLS + (blk + 1) * ATTN_W)
                    uvg_ref[:, blk * ATTN_W:(blk + 1) * ATTN_W] = _nt(hv, w_ref[rows, :]).astype(BF16)
                y = _nt(hv, w_ref[blk * ATTN_W:(blk + 1) * ATTN_W, :])
                if kind < 2:
                    y = _rope(y, cos, s1, s2)
                if kind == 0:
                    y = y * 0.125
                cols = slice(kind * ATTN_W, (kind + 1) * ATTN_W)
                if r == 1:
                    o_ref[0, :, cols] = y.astype(BF16)
                    continue
                slab = slabs[blk % len(slabs)]
                for s in range(N_SLABS):
                    slab[s] = y[:, s * HEAD_PAIR:(s + 1) * HEAD_PAIR]
                for rho in range(r):
                    for s in range(N_SLABS):
                        at = kind * ATTN_W + s * HEAD_PAIR
                        o_ref[rho, :, at:at + HEAD_PAIR] = slab[s, pl.ds(rho, tm // r, stride=r), :].astype(BF16)

    row = pl.BlockSpec((tm, HEAD_PAIR), lambda i: (i, 0))
    return _call(
        body, name="in_proj", grid=(T // tm,),
        out_shape=[_array((r, T // r, 3 * ATTN_W), BF16) for r in DILATIONS] + [_array((T, IN_COLS - QKV_COLS), BF16)],
        in_specs=[pl.BlockSpec((tm, D_MODEL), lambda i: (i, 0)), _resident((IN_COLS, D_MODEL)), row, row, row],
        out_specs=[pl.BlockSpec((r, tm // r, 3 * ATTN_W), lambda i: (0, i, 0)) for r in DILATIONS]
        + [pl.BlockSpec((tm, IN_COLS - QKV_COLS), lambda i: (i, 0))],
        scratch_shapes=[_slab_scratch(tm)] * 3, vmem_mib=56, args=(h, win_t, *tables), carried=carried)


def _band_mask(n, row0=0, rows=2 * BLK):
    row = (lax.broadcasted_iota(jnp.int32, (rows, 2 * BLK), 0) + row0) & (BLK - 1)
    col = lax.broadcasted_iota(jnp.int32, (rows, 2 * BLK), 1)
    has_prev = (jnp.zeros_like(row) + n) > 0
    return ((col < BLK) & (col >= row) & has_prev) | ((col >= BLK) & (col - BLK <= row))


def _head_lanes():
    lane = lax.broadcasted_iota(jnp.int32, (1, HEAD_PAIR), 1)
    return lane < 64, lane >= 64


def _stack_heads(x, head0, head1):
    zero = jnp.zeros_like(x)
    return jnp.concatenate([jnp.where(head0, x, zero), jnp.where(head1, x, zero)], axis=0)


def _unstack_heads(y, head0):
    return jnp.where(head0, y[:BLK], y[BLK:])


def _per_head(stats, col):
    return jnp.concatenate([stats[:, col:col + 1], stats[:, col + 1:col + 2]], axis=0)


Q_CHUNK = 128


def _attn_sub_blocks(length):
    for n in (8, 4, 2):
        if length % (n * BLK) == 0:
            return n
    return 1


def _attn_fwd(qkv, group, carried=None):
    r, L, _ = qkv.shape
    nsub = _attn_sub_blocks(L)
    rows = nsub * BLK

    def body(q_ref, kp_ref, kc_ref, vp_ref, vc_ref, o_ref, lse_ref):
        head0, head1 = _head_lanes()
        lane = lax.broadcasted_iota(jnp.int32, (1, HEAD_PAIR), 1)
        for j in range(nsub):
            cur = slice(j * BLK, (j + 1) * BLK)
            before = slice((j - 1) * BLK, j * BLK)
            stats = jnp.zeros((BLK, HEAD_PAIR), F32)
            for p in range(N_SLABS):
                sl = slice(p * HEAD_PAIR, (p + 1) * HEAD_PAIR)
                k_prev, v_prev = (kp_ref[:, sl], vp_ref[:, sl]) if j == 0 else (kc_ref[before, sl], vc_ref[before, sl])
                k2 = jnp.concatenate([k_prev, kc_ref[cur, sl]], axis=0)
                v2 = jnp.concatenate([v_prev, vc_ref[cur, sl]], axis=0)
                qs = _stack_heads(q_ref[cur, sl], head0, head1)
                outs, lses = [], []
                for c in range(2 * BLK // Q_CHUNK):
                    valid = _band_mask(pl.program_id(1) + j, c * Q_CHUNK, Q_CHUNK)
                    s = jnp.where(valid, _nt(qs[c * Q_CHUNK:(c + 1) * Q_CHUNK], k2), MASKED)
                    m = jnp.max(s, axis=1, keepdims=True)
                    e = jnp.exp(s - m)
                    den = jnp.sum(e, axis=1, keepdims=True)
                    outs.append(_nn(e.astype(BF16), v2) / den)
                    lses.append(m + jnp.log(den))
                o_ref[cur, sl] = _unstack_heads(jnp.concatenate(outs, axis=0), head0).astype(BF16)
                lse = jnp.concatenate(lses, axis=0)
                stats = jnp.where(lane == 2 * p, lse[:BLK], jnp.where(lane == 2 * p + 1, lse[BLK:], stats))
            lse_ref[cur, :] = stats

    def cur_blk(kind, width=ATTN_W):
        return pl.BlockSpec((None, rows, width), lambda rho, n: (rho, n, kind))

    def prev_blk(kind):
        return pl.BlockSpec((None, BLK, ATTN_W), lambda rho, n: (rho, jnp.maximum(n * nsub - 1, 0), kind))

    return _call(
        body, name=f"attn_fwd_g{group}", grid=(r, L // rows),
        out_shape=[_array((r, L, ATTN_W), BF16), _array((r, L, HEAD_PAIR), F32)],
        in_specs=[cur_blk(0), prev_blk(1), cur_blk(1), prev_blk(2), cur_blk(2)],
        out_specs=[cur_blk(0), cur_blk(0, HEAD_PAIR)],
        vmem_mib=24, args=(qkv, qkv, qkv, qkv, qkv), carried=carried)


def _gelu(x):
    return 0.5 * x * (1.0 + lax.erf(x * (1.0 / math.sqrt(2.0))))


def _gelu_grad(x):
    return 0.5 * (1.0 + lax.erf(x * (1.0 / math.sqrt(2.0)))) + x * jnp.exp(-0.5 * x * x) * (1.0 / math.sqrt(2.0 * math.pi))


def _causal():
    row = lax.broadcasted_iota(jnp.int32, (BLK, BLK), 0)
    col = lax.broadcasted_iota(jnp.int32, (BLK, BLK), 1)
    return col <= row


def _bias_lanes(bt):
    grp = lax.broadcasted_iota(jnp.int32, (1, SGU_W), 1) // 64
    out = jnp.zeros((BLK, SGU_W), F32)
    for g in range(8):
        out = jnp.where(grp == g, bt[:, g:g + 1], out)
    return out


def _sgu_normalise(uv, ln_g, ln_b):
    z = _gelu(uv)
    u, v = z[:, :SGU_W], z[:, SGU_W:]
    mu = jnp.mean(v, axis=-1, keepdims=True)
    xc = v - mu
    rstd = lax.rsqrt(jnp.mean(xc * xc, axis=-1, keepdims=True) + EPS)
    xh = xc * rstd
    return u, xh, rstd, xh * ln_g + ln_b


def _sgu_mix(wc_ref, vb, head0):
    chunks = []
    for ch in range(vb.shape[0] // BLK):
        pairs = []
        for p in range(SGU_W // HEAD_PAIR):
            v_pair = vb[ch * BLK:(ch + 1) * BLK, p * HEAD_PAIR:(p + 1) * HEAD_PAIR]
            pairs.append(jnp.where(head0, _nn(wc_ref[2 * p], v_pair), _nn(wc_ref[2 * p + 1], v_pair)))
        chunks.append(jnp.concatenate(pairs, axis=1))
    return jnp.concatenate(chunks, axis=0)


def _sgu_fwd(uvg, ln_g, ln_b, w_s, b_t, tm):
    T = uvg.shape[0]

    def body(uv_ref, g_ref, b_ref, w_ref, bt_ref, o_ref, wc_ref, bias_ref):
        @pl.when(pl.program_id(0) == 0)
        def _():
            causal = _causal()
            for g in range(8):
                wc_ref[g] = jnp.where(causal, w_ref[g], 0.0).astype(BF16)
            bias_ref[...] = _bias_lanes(bt_ref[...])

        u, _, _, vn = _sgu_normalise(uv_ref[...].astype(F32), g_ref[...], b_ref[...])
        mixed = _sgu_mix(wc_ref, vn.astype(BF16), _head_lanes()[0])
        o_ref[...] = (u * (mixed + jnp.tile(bias_ref[...], (tm // BLK, 1)))).astype(BF16)

    vec = pl.BlockSpec((1, SGU_W), lambda i: (0, 0))
    return _pallas(
        body, name="sgu_fwd", grid=(T // tm,), out_shape=_array((T, SGU_W), BF16),
        in_specs=[pl.BlockSpec((tm, 2 * SGU_W), lambda i: (i, 0)), vec, vec,
                  pl.BlockSpec((8, BLK, BLK), lambda i: (0, 0, 0)), pl.BlockSpec((BLK, 8), lambda i: (0, 0))],
        out_specs=pl.BlockSpec((tm, SGU_W), lambda i: (i, 0)),
        scratch_shapes=[pltpu.VMEM((8, BLK, BLK), BF16), pltpu.VMEM((BLK, SGU_W), F32)],
        compiler_params=_params(("arbitrary",), 32),
    )(uvg, ln_g, ln_b, w_s, b_t)


def _merge_fwd(outs, lses, sgu, wpa_t, wps_t, uvg, w_out, x, g2, tm):
    T = sgu.shape[0]

    def body(o0, l0, o1, l1, o2, l2, sgu_ref, wpa_ref, wps_ref, ga_ref, gb_ref, wo_ref, x_ref, g_ref,
             pa_ref, ps_ref, m_ref, h_ref, attn_b_ref, x1_ref, attn_ref, lse_ref, so1, sl1, so2, sl2):
        for o_in, l_in, so, sl, r in ((o1, l1, so1, sl1, DILATIONS[1]), (o2, l2, so2, sl2, DILATIONS[2])):
            _rows_by_token(so, o_in, r, tm // r)
            _rows_by_token(sl, l_in, r, tm // r, n=1)
        head0, _ = _head_lanes()
        a0, a1, a2 = l0[0], sl1[0], sl2[0]
        mx = jnp.maximum(jnp.maximum(a0, a1), a2)
        e0, e1, e2 = jnp.exp(a0 - mx), jnp.exp(a1 - mx), jnp.exp(a2 - mx)
        tot = e0 + e1 + e2
        lse_ref[...] = mx + jnp.log(tot)
        w0, w1, w2 = e0 / tot, e1 / tot, e2 / tot
        for s in range(N_SLABS):
            lanes = slice(s * HEAD_PAIR, (s + 1) * HEAD_PAIR)

            def lanes_of(w):
                return jnp.where(head0, w[:, 2 * s:2 * s + 1], w[:, 2 * s + 1:2 * s + 2])

            mixed = lanes_of(w0) * o0[0, :, lanes] + lanes_of(w1) * so1[s] + lanes_of(w2) * so2[s]
            attn_ref[:, lanes] = mixed
            attn_b_ref[:, lanes] = mixed.astype(BF16)

        pa = _nt(attn_b_ref[...], wpa_ref[...])
        ps = _nt(sgu_ref[...], wps_ref[...])
        pa_ref[...] = pa.astype(BF16)
        ps_ref[...] = ps.astype(BF16)
        ga, gb = ga_ref[...].astype(F32), gb_ref[...].astype(F32)
        merged = (jax.nn.sigmoid(ga) * pa + jax.nn.sigmoid(gb) * ps).astype(BF16)
        m_ref[...] = merged
        x1 = x_ref[...] + _nn(merged, wo_ref[...])
        x1_ref[...] = x1
        r = lax.rsqrt(jnp.mean(x1 * x1, axis=-1, keepdims=True) + EPS)
        h_ref[...] = (x1 * r * g_ref[...]).astype(BF16)

    ins, specs = [], []
    for g, r in enumerate(DILATIONS):
        ins += [outs[g], lses[g]]
        specs += [pl.BlockSpec((r, tm // r, ATTN_W), lambda i: (0, i, 0)),
                  pl.BlockSpec((r, tm // r, HEAD_PAIR), lambda i: (0, i, 0))]
    half = pl.BlockSpec((tm, ATTN_W), lambda i: (i, 0))
    wide = pl.BlockSpec((tm, D_MODEL), lambda i: (i, 0))
    w = _resident((D_MODEL, ATTN_W))
    res = _array((T, D_MODEL), BF16)
    return _pallas(
        body, name="merge_fwd", grid=(T // tm,),
        out_shape=[res, res, res, res, _array((T, ATTN_W), BF16), _array((T, D_MODEL), F32),
                   _array((T, ATTN_W), F32), _array((T, HEAD_PAIR), F32)],
        in_specs=specs + [half, w, w, pl.BlockSpec((tm, D_MODEL), lambda i: (i, 1)),
                          pl.BlockSpec((tm, D_MODEL), lambda i: (i, 2)), _resident((D_MODEL, D_MODEL)), wide,
                          pl.BlockSpec((1, D_MODEL), lambda i: (0, 0))],
        out_specs=[wide] * 4 + [half, wide, half, pl.BlockSpec((tm, HEAD_PAIR), lambda i: (i, 0))],
        scratch_shapes=[_slab_scratch(tm), _slab_scratch(tm, 1), _slab_scratch(tm), _slab_scratch(tm, 1)],
        compiler_params=_params(("parallel",), 56),
    )(*ins, sgu, wpa_t, wps_t, uvg, uvg, w_out, x, g2)


def _ffn_fwd(h2, wg_t, wu_t, tm):
    T = h2.shape[0]

    def body(h_ref, wg_ref, wu_ref, gate_ref, up_ref, ff_ref):
        h = h_ref[...]
        for j in range(D_FF // FF_TILE):
            cols = slice(j * FF_TILE, (j + 1) * FF_TILE)
            gate, up = _nt(h, wg_ref[cols, :]), _nt(h, wu_ref[cols, :])
            gate_ref[:, cols] = gate.astype(BF16)
            up_ref[:, cols] = up.astype(BF16)
            ff_ref[:, cols] = (gate * jax.nn.sigmoid(gate) * up).astype(BF16)

    w = _resident((D_FF, D_MODEL))
    o = pl.BlockSpec((tm, D_FF), lambda i: (i, 0))
    res = _array((T, D_FF), BF16)
    return _pallas(
        body, name="ffn_fwd", grid=(T // tm,), out_shape=[res, res, res],
        in_specs=[pl.BlockSpec((tm, D_MODEL), lambda i: (i, 0)), w, w], out_specs=[o, o, o],
        compiler_params=_params(("parallel",), 52),
    )(h2, wg_t, wu_t)


def _down_loss(ff, w_down, x1, final_g, target, tm):
    T = x1.shape[0]

    def body(ff_ref, w_ref, x1_ref, g_ref, t_ref, dx_ref, dxb_ref, loss_ref, dg_ref):
        @pl.when(pl.program_id(0) == 0)
        def _():
            loss_ref[...] = jnp.zeros_like(loss_ref)
            dg_ref[...] = jnp.zeros_like(dg_ref)

        x2 = x1_ref[...] + _nn(ff_ref[...], w_ref[...])
        g = g_ref[...]
        r = lax.rsqrt(jnp.mean(x2 * x2, axis=-1, keepdims=True) + EPS)
        xh = x2 * r
        err = xh * g - t_ref[...]
        loss_ref[...] += jnp.sum(err * err, axis=0, keepdims=True) * (0.5 / D_MODEL)
        dy = err * (1.0 / D_MODEL)
        dg_ref[...] += jnp.sum(dy * xh, axis=0, keepdims=True)
        dxh = dy * g
        dx = r * (dxh - xh * jnp.mean(dxh * xh, axis=-1, keepdims=True))
        dx_ref[...] = dx
        dxb_ref[...] = dx.astype(BF16)

    wide = pl.BlockSpec((tm, D_MODEL), lambda i: (i, 0))
    vec = pl.BlockSpec((1, D_MODEL), lambda i: (0, 0))
    vec_shape = _array((1, D_MODEL), F32)
    return _pallas(
        body, name="down_loss", grid=(T // tm,),
        out_shape=[_array((T, D_MODEL), F32), _array((T, D_MODEL), BF16),
                   vec_shape, vec_shape],
        in_specs=[pl.BlockSpec((tm, D_FF), lambda i: (i, 0)), _resident((D_FF, D_MODEL)), wide, vec, wide],
        out_specs=[wide, wide, vec, vec],
        compiler_params=_params(("arbitrary",), 40),
    )(ff, w_down, x1, final_g, target)


def _rmsnorm_bwd(dh, xin, g, d_res, dg_ref):
    r = lax.rsqrt(jnp.mean(xin * xin, axis=-1, keepdims=True) + EPS)
    xh = xin * r
    dg_ref[...] += jnp.sum(dh * xh, axis=0, keepdims=True)
    dxh = dh * g
    return d_res + r * (dxh - xh * jnp.mean(dxh * xh, axis=-1, keepdims=True))


def _ffn_bwd(dx2, dx2b, w_down, wg_t, wu_t, gate, up, x1, g2, tm):
    T = dx2.shape[0]

    def body(dx_ref, dxb_ref, wd_ref, wg_ref, wu_ref, gate_ref, up_ref, x1_ref, g_ref,
             dgate_ref, dup_ref, dx1_ref, dx1b_ref, dg_ref):
        @pl.when(pl.program_id(0) == 0)
        def _():
            dg_ref[...] = jnp.zeros_like(dg_ref)

        dxb = dxb_ref[...]
        dh = jnp.zeros((tm, D_MODEL), F32)
        for j in range(D_FF // FF_TILE):
            cols = slice(j * FF_TILE, (j + 1) * FF_TILE)
            dff = _nt(dxb, wd_ref[cols, :])
            gate, up = gate_ref[:, cols].astype(F32), up_ref[:, cols].astype(F32)
            sg = jax.nn.sigmoid(gate)
            dgate = (dff * up * sg * (1.0 + gate * (1.0 - sg))).astype(BF16)
            dup = (dff * gate * sg).astype(BF16)
            dgate_ref[:, cols] = dgate
            dup_ref[:, cols] = dup
            dh += _nn(dgate, wg_ref[cols, :]) + _nn(dup, wu_ref[cols, :])
        dx1 = _rmsnorm_bwd(dh, x1_ref[...], g_ref[...], dx_ref[...], dg_ref)
        dx1_ref[...] = dx1
        dx1b_ref[...] = dx1.astype(BF16)

    wide = pl.BlockSpec((tm, D_MODEL), lambda i: (i, 0))
    ffw = pl.BlockSpec((tm, D_FF), lambda i: (i, 0))
    vec = pl.BlockSpec((1, D_MODEL), lambda i: (0, 0))
    w = _resident((D_FF, D_MODEL))
    ff_shape = _array((T, D_FF), BF16)
    return _pallas(
        body, name="ffn_bwd", grid=(T // tm,),
        out_shape=[ff_shape, ff_shape, _array((T, D_MODEL), F32),
                   _array((T, D_MODEL), BF16), _array((1, D_MODEL), F32)],
        in_specs=[wide, wide, w, w, w, ffw, ffw, wide, vec],
        out_specs=[ffw, ffw, wide, wide, vec], compiler_params=_params(("arbitrary",), 56),
    )(dx2, dx2b, w_down, wg_t, wu_t, gate, up, x1, g2)


def _mm_tn(a, b, *, tmm, tk, name, into=None, row_block0=0, rows_total=None):
    T, M = a.shape
    N = b.shape[1]
    rows_total = M if rows_total is None else rows_total

    def body(*refs):
        a_ref, b_ref, o_ref = refs[0], refs[1], refs[-1]

        @pl.when(pl.program_id(1) == 0)
        def _():
            o_ref[...] = jnp.zeros_like(o_ref)

        o_ref[...] += _tn(a_ref[...], b_ref[...])

    ins = [a, b] + ([] if into is None else [into])
    specs = [pl.BlockSpec((tk, tmm), lambda i, k: (k, i)), pl.BlockSpec((tk, N), lambda i, k: (k, 0))]
    return _pallas(
        body, name=name, grid=(M // tmm, T // tk),
        out_shape=_array((rows_total, N), F32),
        in_specs=specs + ([] if into is None else [ANY]),
        out_specs=pl.BlockSpec((tmm, N), lambda i, k: (row_block0 + i, 0)),
        input_output_aliases={} if into is None else {2: 0},
        compiler_params=_params(("parallel", "arbitrary"), 48),
    )(*ins)


def _in_bwd(dqkv, dug, win_t, x, g1, dx1, tm, carried=None):
    T = x.shape[0]

    def body(dq_ref, dug_ref, w_ref, x_ref, g_ref, d_ref, dx_ref, dg_ref):
        @pl.when(pl.program_id(0) == 0)
        def _():
            dg_ref[...] = jnp.zeros_like(dg_ref)

        dh = _nn(dq_ref[...], w_ref[:QKV_COLS, :]) + _nn(dug_ref[...], w_ref[QKV_COLS:, :])
        dx_ref[...] = _rmsnorm_bwd(dh, x_ref[...], g_ref[...], d_ref[...], dg_ref)

    def cols(n):
        return pl.BlockSpec((tm, n), lambda i: (i, 0))

    wide = cols(D_MODEL)
    vec = pl.BlockSpec((1, D_MODEL), lambda i: (0, 0))
    return _call(
        body, name="in_bwd", grid=(T // tm,),
        out_shape=[_array((T, D_MODEL), F32), _array((1, D_MODEL), F32)],
        in_specs=[cols(QKV_COLS), cols(IN_COLS - QKV_COLS), _resident((IN_COLS, D_MODEL)), wide, vec, wide],
        out_specs=[wide, vec], vmem_mib=56, args=(dqkv, dug, win_t, x, g1, dx1), carried=carried)


D_LANE0 = 8


def _merge_bwd(dx1b, w_out, pa, ps, uvg, wpa_t, wps_t, attn, lse, tm, carried=None):
    T = dx1b.shape[0]

    def body(dx_ref, w_ref, pa_ref, ps_ref, ga_ref, gb_ref, wpa_ref, wps_ref, at_ref, lse_ref,
             dpa_ref, dps_ref, dug_ref, dsgu_ref, *rest):
        outs, (scr_da, scr_st) = rest[:-2], rest[-2:]
        dm = _nt(dx_ref[...], w_ref[...])
        ga, gb = jax.nn.sigmoid(ga_ref[...].astype(F32)), jax.nn.sigmoid(gb_ref[...].astype(F32))
        dpa, dps = (dm * ga).astype(BF16), (dm * gb).astype(BF16)
        dpa_ref[...] = dpa
        dps_ref[...] = dps
        dug_ref[:, 2 * SGU_W:2 * SGU_W + D_MODEL] = (dm * pa_ref[...].astype(F32) * ga * (1.0 - ga)).astype(BF16)
        dug_ref[:, 2 * SGU_W + D_MODEL:] = (dm * ps_ref[...].astype(F32) * gb * (1.0 - gb)).astype(BF16)
        dsgu_ref[...] = _nn(dps, wps_ref[...])
        dattn = _nn(dpa, wpa_ref[...])

        head0, _ = _head_lanes()
        lane = lax.broadcasted_iota(jnp.int32, (1, HEAD_PAIR), 1)
        stats = lse_ref[...]
        for s in range(N_SLABS):
            lanes = slice(s * HEAD_PAIR, (s + 1) * HEAD_PAIR)
            da = dattn[:, lanes]
            pp = da * at_ref[:, lanes]
            d0 = jnp.sum(jnp.where(head0, pp, 0.0), axis=1, keepdims=True)
            d1 = jnp.sum(jnp.where(head0, 0.0, pp), axis=1, keepdims=True)
            stats = jnp.where(lane == D_LANE0 + 2 * s, d0, jnp.where(lane == D_LANE0 + 2 * s + 1, d1, stats))
            scr_da[s] = da
            outs[0][0, :, lanes] = da.astype(BF16)
        scr_st[0] = stats
        outs[1][0] = stats
        for g, r in enumerate(DILATIONS):
            if r > 1:
                _rows_by_residue(outs[2 * g], scr_da, r, tm // r, BF16)
                _rows_by_residue(outs[2 * g + 1], scr_st, r, tm // r, F32, n=1)

    wide = pl.BlockSpec((tm, D_MODEL), lambda i: (i, 0))
    half = pl.BlockSpec((tm, ATTN_W), lambda i: (i, 0))
    w = _resident((D_MODEL, ATTN_W))
    res = _array((T, D_MODEL), BF16)
    shapes, specs = [], []
    for r in DILATIONS:
        shapes += [_array((r, T // r, ATTN_W), BF16), _array((r, T // r, HEAD_PAIR), F32)]
        specs += [pl.BlockSpec((r, tm // r, ATTN_W), lambda i: (0, i, 0)),
                  pl.BlockSpec((r, tm // r, HEAD_PAIR), lambda i: (0, i, 0))]
    return _call(
        body, name="merge_bwd", grid=(T // tm,),
        out_shape=[res, res, _array((T, IN_COLS - QKV_COLS), BF16), _array((T, ATTN_W), F32)] + shapes,
        in_specs=[wide, _resident((D_MODEL, D_MODEL)), wide, wide,
                  pl.BlockSpec((tm, D_MODEL), lambda i: (i, 1)), pl.BlockSpec((tm, D_MODEL), lambda i: (i, 2)), w, w,
                  half, pl.BlockSpec((tm, HEAD_PAIR), lambda i: (i, 0))],
        out_specs=[wide, wide, pl.BlockSpec((tm, IN_COLS - QKV_COLS), lambda i: (i, 0)), half] + specs,
        scratch_shapes=[_slab_scratch(tm), _slab_scratch(tm, 1)],
        vmem_mib=56, args=(dx1b, w_out, pa, ps, uvg, uvg, wpa_t, wps_t, attn, lse), carried=carried)


def _sgu_bwd(uvg, dsgu, ln_g, ln_b, w_s, b_t, dug, tm, carried=None):
    T = uvg.shape[0]
    nsteps = T // tm

    def body(uv_ref, ds_ref, g_ref, b_ref, w_ref, bt_ref, _, duv_ref, dw_ref, dbt_ref, dg_ref, db_ref,
             wc_ref, wct_ref, bias_ref, dbias_ref):
        step = pl.program_id(0)
        head0, head1 = _head_lanes()

        @pl.when(step == 0)
        def _():
            causal = _causal()
            for g in range(8):
                wc = jnp.where(causal, w_ref[g], 0.0)
                wc_ref[g] = wc.astype(BF16)
                wct_ref[g] = wc.T.astype(BF16)
            bias_ref[...] = _bias_lanes(bt_ref[...])
            dbias_ref[...] = jnp.zeros_like(dbias_ref)
            dw_ref[...] = jnp.zeros_like(dw_ref)
            dg_ref[...] = jnp.zeros_like(dg_ref)
            db_ref[...] = jnp.zeros_like(db_ref)

        uv = uv_ref[...].astype(F32)
        ln_gain = g_ref[...]
        u, xh, rstd, vn = _sgu_normalise(uv, ln_gain, b_ref[...])
        vb = vn.astype(BF16)
        mixed = _sgu_mix(wc_ref, vb, head0) + jnp.tile(bias_ref[...], (tm // BLK, 1))
        dout = ds_ref[...]
        du = dout * mixed
        dmixed = dout * u
        dmb = dmixed.astype(BF16)
        dvn_chunks = []
        for ch in range(tm // BLK):
            rows = slice(ch * BLK, (ch + 1) * BLK)
            dbias_ref[...] += dmixed[rows]
            pairs = []
            for p in range(SGU_W // HEAD_PAIR):
                lanes = slice(p * HEAD_PAIR, (p + 1) * HEAD_PAIR)
                dm_pair, v_pair = dmb[rows, lanes], vb[rows, lanes]
                acc = jnp.zeros((BLK, HEAD_PAIR), F32)
                for hh, half in enumerate((head0, head1)):
                    dm_h = jnp.where(half, dm_pair, jnp.zeros_like(dm_pair))
                    dw_ref[2 * p + hh] += _nt(dm_h, v_pair)
                    acc += _nn(wct_ref[2 * p + hh], dm_h)
                pairs.append(acc)
            dvn_chunks.append(jnp.concatenate(pairs, axis=1))
        dvn = jnp.concatenate(dvn_chunks, axis=0)
        dg_ref[...] += jnp.sum(dvn * xh, axis=0, keepdims=True)
        db_ref[...] += jnp.sum(dvn, axis=0, keepdims=True)
        dxh = dvn * ln_gain
        dv = rstd * (dxh - jnp.mean(dxh, axis=-1, keepdims=True) - xh * jnp.mean(dxh * xh, axis=-1, keepdims=True))
        dgelu = _gelu_grad(uv)
        duv_ref[:, :SGU_W] = (du * dgelu[:, :SGU_W]).astype(BF16)
        duv_ref[:, SGU_W:] = (dv * dgelu[:, SGU_W:]).astype(BF16)

        @pl.when(step == nsteps - 1)
        def _():
            causal = _causal()
            for g in range(8):
                dw_ref[g] = jnp.where(causal, dw_ref[g], 0.0)
            grp = lax.broadcasted_iota(jnp.int32, (1, SGU_W), 1) // 64
            col = lax.broadcasted_iota(jnp.int32, (1, 8), 1)
            dbias = dbias_ref[...]
            out = jnp.zeros((BLK, 8), F32)
            for g in range(8):
                s = jnp.sum(jnp.where(grp == g, dbias, 0.0), axis=1, keepdims=True)
                out = jnp.where(col == g, s, out)
            dbt_ref[...] = out

    vec = pl.BlockSpec((1, SGU_W), lambda i: (0, 0))
    w3 = pl.BlockSpec((8, BLK, BLK), lambda i: (0, 0, 0))
    bt = pl.BlockSpec((BLK, 8), lambda i: (0, 0))
    return _call(
        body, name="sgu_bwd", grid=(nsteps,),
        out_shape=[_array(dug.shape, BF16), _array((8, BLK, BLK), F32),
                   _array((BLK, 8), F32), _array((1, SGU_W), F32),
                   _array((1, SGU_W), F32)],
        in_specs=[pl.BlockSpec((tm, 2 * SGU_W), lambda i: (i, 0)), pl.BlockSpec((tm, SGU_W), lambda i: (i, 0)),
                  vec, vec, w3, bt, ANY],
        out_specs=[pl.BlockSpec((tm, 2 * SGU_W), lambda i: (i, 0)), w3, bt, vec, vec],
        scratch_shapes=[pltpu.VMEM((8, BLK, BLK), BF16), pltpu.VMEM((8, BLK, BLK), BF16),
                        pltpu.VMEM((BLK, SGU_W), F32), pltpu.VMEM((BLK, SGU_W), F32)],
        vmem_mib=40, args=(uvg, dsgu, ln_g, ln_b, w_s, b_t, dug), carried=carried, aliases={6: 0})


def _attn_bwd(qkv, dattn, stats, group, carried=None):
    r, L, _ = qkv.shape
    nsub = _attn_sub_blocks(L)
    rows = nsub * BLK
    nb = L // rows
    keep = rows - BLK
    whole = nb == 1

    def body(q_ref, kp_ref, kc_ref, vp_ref, vc_ref, da_ref, st_ref, dq_ref, dk_ref, dv_ref, carry_k, carry_v):
        n = pl.program_id(1)

        if not whole:
            @pl.when(n == 0)
            def _():
                carry_k[...] = jnp.zeros_like(carry_k)
                carry_v[...] = jnp.zeros_like(carry_v)

        @pl.when(n < nb)
        def _():
            head0, head1 = _head_lanes()
            for p in range(N_SLABS):
                sl = slice(p * HEAD_PAIR, (p + 1) * HEAD_PAIR)
                dk_parts = [jnp.zeros((BLK, HEAD_PAIR), F32) for _ in range(nsub + 1)]
                dv_parts = [jnp.zeros((BLK, HEAD_PAIR), F32) for _ in range(nsub + 1)]
                for j in range(nsub):
                    cur = slice(j * BLK, (j + 1) * BLK)
                    before = slice((j - 1) * BLK, j * BLK)
                    valid = _band_mask(n + j)
                    st = st_ref[cur, :]
                    k_prev, v_prev = (kp_ref[:, sl], vp_ref[:, sl]) if j == 0 else (kc_ref[before, sl], vc_ref[before, sl])
                    k2 = jnp.concatenate([k_prev, kc_ref[cur, sl]], axis=0)
                    v2 = jnp.concatenate([v_prev, vc_ref[cur, sl]], axis=0)
                    qs = _stack_heads(q_ref[cur, sl], head0, head1)
                    das = _stack_heads(da_ref[cur, sl], head0, head1)
                    prob = jnp.where(valid, jnp.exp(_nt(qs, k2) - _per_head(st, 2 * p)), 0.0)
                    ds = (prob * (_nt(das, v2) - _per_head(st, D_LANE0 + 2 * p))).astype(BF16)
                    dk2 = _tn(ds, qs)
                    dv2 = _tn(prob.astype(BF16), das)
                    dq_ref[cur, sl] = _unstack_heads(_nn(ds, k2), head0).astype(BF16)
                    dk_parts[j] += dk2[:BLK]
                    dk_parts[j + 1] += dk2[BLK:]
                    dv_parts[j] += dv2[:BLK]
                    dv_parts[j + 1] += dv2[BLK:]
                if whole:
                    for j in range(nsub):
                        dk_ref[j * BLK:(j + 1) * BLK, sl] = dk_parts[j + 1].astype(BF16)
                        dv_ref[j * BLK:(j + 1) * BLK, sl] = dv_parts[j + 1].astype(BF16)
                    continue
                if keep:
                    dk_ref[:keep, sl] = carry_k[:keep, sl].astype(BF16)
                    dv_ref[:keep, sl] = carry_v[:keep, sl].astype(BF16)
                dk_ref[keep:, sl] = (carry_k[keep:, sl] + dk_parts[0]).astype(BF16)
                dv_ref[keep:, sl] = (carry_v[keep:, sl] + dv_parts[0]).astype(BF16)
                for j in range(nsub):
                    carry_k[j * BLK:(j + 1) * BLK, sl] = dk_parts[j + 1]
                    carry_v[j * BLK:(j + 1) * BLK, sl] = dv_parts[j + 1]

        if not whole:
            @pl.when(n == nb)
            def _():
                dk_ref[...] = carry_k[...].astype(BF16)
                dv_ref[...] = carry_v[...].astype(BF16)

    def cur_blk(kind, width=ATTN_W):
        return pl.BlockSpec((None, rows, width), lambda rho, n: (rho, jnp.minimum(n, nb - 1), kind))

    def last_blk(kind):
        return pl.BlockSpec((None, rows, ATTN_W), lambda rho, n: (rho, jnp.clip(n - 1, 0, nb - 1), kind))

    def prev_keys(kind):
        return pl.BlockSpec((None, BLK, ATTN_W),
                            lambda rho, n: (rho, jnp.clip(n * nsub - 1, 0, nb * nsub - 1), kind))

    res = _array((r, L, ATTN_W), BF16)
    return _call(
        body, name=f"attn_bwd_g{group}", grid=(r, 1 if whole else nb + 1),
        out_shape=[res, res, res],
        in_specs=[cur_blk(0), prev_keys(1), cur_blk(1), prev_keys(2), cur_blk(2), cur_blk(0), cur_blk(0, HEAD_PAIR)],
        out_specs=[cur_blk(0), last_blk(0), last_blk(0)],
        scratch_shapes=[pltpu.VMEM((rows, ATTN_W), F32), pltpu.VMEM((rows, ATTN_W), F32)],
        vmem_mib=32, args=(qkv, qkv, qkv, qkv, qkv, dattn, stats), carried=carried)


def _dqkv_token_order(dqkv_groups, tables, tm):
    T = tables[0].shape[0]

    def body(*refs):
        ins = refs[:9]
        cos_ref, s1_ref, s2_ref, o_ref, scr = refs[9:]
        cos, s1, s2 = cos_ref[...], s1_ref[...], s2_ref[...]
        for g, r in enumerate(DILATIONS):
            for kind in range(3):
                src = ins[3 * g + kind]
                if r > 1:
                    _rows_by_token(scr, src, r, tm // r)
                for s in range(N_SLABS):
                    val = scr[s] if r > 1 else src[0, :, s * HEAD_PAIR:(s + 1) * HEAD_PAIR].astype(F32)
                    if kind < 2:
                        val = _rope_transposed(val, cos, s1, s2)
                    if kind == 0:
                        val = val * 0.125
                    at = (3 * kind + g) * ATTN_W + s * HEAD_PAIR
                    o_ref[:, at:at + HEAD_PAIR] = val.astype(BF16)

    specs = []
    for r in DILATIONS:
        specs += [pl.BlockSpec((r, tm // r, ATTN_W), lambda i: (0, i, 0))] * 3
    row = pl.BlockSpec((tm, HEAD_PAIR), lambda i: (i, 0))
    flat = [a for grp in dqkv_groups for a in grp]
    return _pallas(
        body, name="dqkv_token_order", grid=(T // tm,), out_shape=_array((T, QKV_COLS), BF16),
        in_specs=specs + [row] * 3, out_specs=pl.BlockSpec((tm, QKV_COLS), lambda i: (i, 0)),
        scratch_shapes=[_slab_scratch(tm)], compiler_params=_params(("parallel",), 48),
    )(*flat, *tables)


def _row_tile(rows):
    for cand in (320, 256, 176, 128):
        if rows % cand == 0:
            return cand
    return rows


def _pair_sum(grad4, recv, chip, name):
    _, _, rows, cols = grad4.shape
    tr = _row_tile(rows)

    def body(ids_ref, g_ref, r_ref, gown_ref, rown_ref, sum_ref, own_ref):
        sum_ref[...] = (g_ref[...] + r_ref[...]).astype(BF16)

        @pl.when(pl.program_id(1) == 0)
        def _():
            own_ref[...] = gown_ref[...] + rown_ref[...]

    grid_spec = pltpu.PrefetchScalarGridSpec(
        num_scalar_prefetch=1, grid=(rows // tr, 4),
        in_specs=[pl.BlockSpec((None, None, tr, cols), lambda i, q, ids: (q, ids[1], i, 0)),
                  pl.BlockSpec((None, tr, cols), lambda i, q, ids: (q, i, 0)),
                  pl.BlockSpec((None, None, tr, cols), lambda i, q, ids: (ids[0], ids[1], i, 0)),
                  pl.BlockSpec((None, tr, cols), lambda i, q, ids: (ids[0], i, 0))],
        out_specs=[pl.BlockSpec((None, tr, cols), lambda i, q, ids: (q, i, 0)),
                   pl.BlockSpec((tr, cols), lambda i, q, ids: (i, 0))])
    return _pallas(
        body, name=name, grid_spec=grid_spec,
        out_shape=[_array((4, rows, cols), BF16), _array((rows, cols), F32)],
        compiler_params=_params(("arbitrary", "arbitrary"), 32),
    )(chip, grad4, recv, grad4, recv)


def _chip_sum(own, others, name):
    rows, cols = own.shape
    tr = _row_tile(rows)

    def body(own_ref, oth_ref, o_ref):
        total = own_ref[...]
        for j in range(3):
            total = total + oth_ref[j].astype(F32)
        o_ref[...] = total

    blk = pl.BlockSpec((tr, cols), lambda i: (i, 0))
    return _pallas(
        body, name=name, grid=(rows // tr,), out_shape=_array((rows, cols), F32),
        in_specs=[blk, pl.BlockSpec((3, tr, cols), lambda i: (0, i, 0))], out_specs=blk,
        compiler_params=_params(("parallel",), 32),
    )(own, others)


def _adam_math(w, g, m, v):
    m = ADAM_B1 * m + (1.0 - ADAM_B1) * g
    v = ADAM_B2 * v + (1.0 - ADAM_B2) * (g * g)
    m_hat = m / (1.0 - ADAM_B1 ** ADAM_STEP)
    v_hat = v / (1.0 - ADAM_B2 ** ADAM_STEP)
    delta = -ADAM_LR * (m_hat / (jnp.sqrt(v_hat) + ADAM_EPS) + ADAM_WD * w)
    return delta, m, v


def _adamw(w, g, m, v, name):
    rows, cols = w.shape
    tr = _row_tile(rows)

    def body(w_ref, g_ref, m_ref, v_ref, d_ref, nm_ref, nv_ref):
        d_ref[...], nm_ref[...], nv_ref[...] = _adam_math(w_ref[...], g_ref[...], m_ref[...], v_ref[...])

    blk = pl.BlockSpec((tr, cols), lambda i: (i, 0))
    res = _array((rows, cols), F32)
    return _pallas(
        body, name=name, grid=(rows // tr,), out_shape=[res, res, res], in_specs=[blk] * 4, out_specs=[blk] * 3,
        compiler_params=_params(("parallel",), 32),
    )(w, g, m, v)


def _small_update(late_parts, early_parts, w, m, v, loss_rows):
    rows = w.shape[0]

    def total(p_ref):
        n = p_ref.shape[0] // N_DEV
        acc = p_ref[0:n, :]
        for dev in range(1, N_DEV):
            acc = acc + p_ref[dev * n:(dev + 1) * n, :]
        return acc

    def body(late_ref, early_ref, w_ref, m_ref, v_ref, g_ref, d_ref, nm_ref, nv_ref, loss_ref):
        g = jnp.concatenate([total(late_ref), total(early_ref)], axis=0)
        g_ref[...] = g
        d_ref[...], nm_ref[...], nv_ref[...] = _adam_math(w_ref[...], g, m_ref[...], v_ref[...])
        loss_ref[...] = jnp.sum(jnp.sum(g[rows - loss_rows:, :], axis=1, keepdims=True), axis=0, keepdims=True)

    res = jax.ShapeDtypeStruct((rows, HEAD_PAIR), F32)
    return _pallas(
        body, name="small_update", out_shape=[res, res, res, res, jax.ShapeDtypeStruct((1, 1), F32)],
        compiler_params=pltpu.CompilerParams(vmem_limit_bytes=32 * MIB),
    )(late_parts, early_parts, w, m, v)


def kernel(x, positions, norm1_g, w_in, sgu_ln_g, sgu_ln_b, w_spatial, b_spatial, w_proj_attn, w_proj_sgu, w_out, norm2_g, w_ffn_gate, w_ffn_up, w_ffn_down, final_g, loss_target, m_norm1_g, m_w_in, m_sgu_ln_g, m_sgu_ln_b, m_w_spatial, m_b_spatial, m_w_proj_attn, m_w_proj_sgu, m_w_out, m_norm2_g, m_w_ffn_gate, m_w_ffn_up, m_w_ffn_down, m_final_g, v_norm1_g, v_w_in, v_sgu_ln_g, v_sgu_ln_b, v_w_spatial, v_b_spatial, v_w_proj_attn, v_w_proj_sgu, v_w_out, v_norm2_g, v_w_ffn_gate, v_w_ffn_up, v_w_ffn_down, v_final_g):
    T = x.shape[1]
    tm = 512
    xt = x[0]
    target = loss_target[0]
    chip = jnp.stack([2 * lax.axis_index("x") + lax.axis_index("y"), lax.axis_index("c")]).astype(jnp.int32)

    def bf16_rows(w, transpose):
        return (jnp.transpose(w[0]) if transpose else w[0]).astype(BF16)

    w_in_gather = _all_gather([bf16_rows(w_in, True)])
    later = _all_gather([bf16_rows(w_ffn_gate, True), bf16_rows(w_ffn_up, True), bf16_rows(w_out, False),
                         bf16_rows(w_proj_attn, True), bf16_rows(w_proj_sgu, True), bf16_rows(w_ffn_down, False)])

    inv_freq = ROPE_THETA ** (-jnp.arange(0, 2 * ROPE_HALF, 2, dtype=F32) / (2 * ROPE_HALF))
    inv_freq_row = jnp.tile(jnp.concatenate([inv_freq, inv_freq, jnp.zeros((48,), F32)]), 2).reshape(1, HEAD_PAIR)
    b_t = jnp.transpose(b_spatial[0])

    (h, *tables), (win_t,) = _norm1_and_rope_tables(xt, norm1_g, positions.reshape(T, 1), inv_freq_row, tm,
                                                    carried=w_in_gather)
    (*qkv, uvg), (wg_t, wu_t, wout, wpa_t, wps_t, wd) = _in_proj(h, win_t, tables, tm, carried=later)
    fwd = [_attn_fwd(qkv[g], g)[0] for g in range(3)]
    sgu = _sgu_fwd(uvg, sgu_ln_g, sgu_ln_b, w_spatial[0], b_t, tm)
    pa, ps, merged, h2, attn_b, x1, attn, lse = _merge_fwd([f[0] for f in fwd], [f[1] for f in fwd], sgu, wpa_t, wps_t,
                                                           uvg, wout, xt, norm2_g, tm)
    gate, up, ff = _ffn_fwd(h2, wg_t, wu_t, tm)
    dx2, dx2b, loss_cols, d_final_g = _down_loss(ff, wd, x1, final_g.reshape(1, D_MODEL), target, tm)

    dgate, dup, dx1, dx1b, d_norm2 = _ffn_bwd(dx2, dx2b, wd, wg_t, wu_t, gate, up, x1, norm2_g, tm // 2)
    tk = min(2048, T)
    d_wd = _mm_tn(ff, dx2b, tmm=FF_TILE, tk=tk, name="grad_w_ffn_down")
    d_wg_t = _mm_tn(dgate, h2, tmm=FF_TILE, tk=tk, name="grad_w_ffn_gate")
    d_wu_t = _mm_tn(dup, h2, tmm=FF_TILE, tk=tk, name="grad_w_ffn_up")

    def by_owner(grads):
        return [g.reshape(4, 2, g.shape[0] // N_DEV, g.shape[1]) for g in grads]

    def pair_sums(grads4, from_sibling, names):
        both = [_pair_sum(g4, rv, chip, "grad_pair_sum_" + nm) for g4, rv, nm in zip(grads4, from_sibling, names)]
        return [b[0] for b in both], [b[1] for b in both]

    ffn_names = ["w_ffn_gate", "w_ffn_up", "w_ffn_down"]
    ffn4 = by_owner([d_wg_t, d_wu_t, d_wd])
    (dpa, dps, dug, dsgu, *prep), ffn_sib = _merge_bwd(dx1b, wout, pa, ps, uvg, wpa_t, wps_t, attn, lse, tm,
                                                       carried=_pair_exchange(ffn4))
    ffn_sums, ffn_own = pair_sums(ffn4, ffn_sib, ffn_names)

    d_wout = _mm_tn(merged, dx1b, tmm=D_MODEL, tk=tk, name="grad_w_out")
    d_wpa_t = _mm_tn(dpa, attn_b, tmm=D_MODEL, tk=tk, name="grad_w_proj_attn")
    d_wps_t = _mm_tn(dps, sgu, tmm=D_MODEL, tk=tk, name="grad_w_proj_sgu")
    mid_names = ["w_proj_attn", "w_proj_sgu", "w_out"]
    mid4 = by_owner([d_wpa_t, d_wps_t, d_wout])
    (dug, d_ws, d_bs_t, d_ln_g, d_ln_b), mid_sib = _sgu_bwd(uvg, dsgu, sgu_ln_g, sgu_ln_b, w_spatial[0], b_t, dug, tm,
                                                           carried=_pair_exchange(mid4))
    mid_sums, mid_own = pair_sums(mid4, mid_sib, mid_names)

    dqkv0, ffn_far = _attn_bwd(qkv[0], prep[0], prep[1], 0, carried=_chip_exchange(ffn_sums))
    dqkv1, mid_far = _attn_bwd(qkv[1], prep[2], prep[3], 1, carried=_chip_exchange(mid_sums))
    def flat(parts):
        return jnp.concatenate([p.reshape(-1) for p in parts]).reshape(-1, HEAD_PAIR)

    early_part = flat([d_ln_g, d_ln_b, d_ws, jnp.transpose(d_bs_t), d_norm2, d_final_g, loss_cols])
    dqkv2, (early_parts,) = _attn_bwd(qkv[2], prep[4], prep[5], 2, carried=_all_gather([early_part]))
    dqkv = _dqkv_token_order([dqkv0, dqkv1, dqkv2], tables, tm)
    d_win_t = _mm_tn(dqkv, h, tmm=1536, tk=tk, name="grad_w_in_qkv", rows_total=IN_COLS)
    d_win_t = _mm_tn(dug, h, tmm=1536, tk=tk, name="grad_w_in_uv_gates", into=d_win_t, row_block0=3, rows_total=IN_COLS)
    in4 = by_owner([d_win_t])
    in_sib = _pair_exchange(in4).run_alone("w_in_grad_pair_exchange")
    in_sums, in_own = pair_sums(in4, in_sib, ["w_in"])
    (dx, d_norm1), in_far = _in_bwd(dqkv, dug, win_t, xt, norm1_g, dx1, tm, carried=_chip_exchange(in_sums))

    names = ["w_in"] + mid_names + ffn_names
    reduced = [_chip_sum(o, f, "grad_total_" + nm)
               for o, f, nm in zip(in_own + mid_own + ffn_own, in_far + mid_far + ffn_far, names)]
    transposed = (True, True, True, False, True, True, False)
    g_big = [jnp.transpose(r) if t else r for r, t in zip(reduced, transposed)]

    small_w = [norm1_g, sgu_ln_g, sgu_ln_b, w_spatial, b_spatial, norm2_g, final_g]
    small_m = [m_norm1_g, m_sgu_ln_g, m_sgu_ln_b, m_w_spatial, m_b_spatial, m_norm2_g, m_final_g]
    small_v = [v_norm1_g, v_sgu_ln_g, v_sgu_ln_b, v_w_spatial, v_b_spatial, v_norm2_g, v_final_g]
    zeros = jnp.zeros((D_MODEL,), F32)
    (late_parts,) = _all_gather([flat([d_norm1])]).run_alone("norm1_grad_all_gather", vmem=True)
    g_s, d_s, nm_s, nv_s, loss = _small_update(late_parts, early_parts, flat(small_w + [zeros]), flat(small_m + [zeros]),
                                               flat(small_v + [zeros]), D_MODEL // HEAD_PAIR)

    def unflat(vec):
        vec = vec.reshape(-1)
        out, at = [], 0
        for wgt in small_w:
            out.append(vec[at:at + wgt.size].reshape(wgt.shape))
            at += wgt.size
        return out

    small = [unflat(a) for a in (g_s, d_s, nm_s, nv_s)]

    big_w = [w_in, w_proj_attn, w_proj_sgu, w_out, w_ffn_gate, w_ffn_up, w_ffn_down]
    big_m = [m_w_in, m_w_proj_attn, m_w_proj_sgu, m_w_out, m_w_ffn_gate, m_w_ffn_up, m_w_ffn_down]
    big_v = [v_w_in, v_w_proj_attn, v_w_proj_sgu, v_w_out, v_w_ffn_gate, v_w_ffn_up, v_w_ffn_down]
    big_out = []
    for wgt, g, mm, vv, nm in zip(big_w, g_big, big_m, big_v, names):
        d, nm_, nv_ = _adamw(wgt[0], g, mm[0], vv[0], "adamw_" + nm)
        big_out.append([a[None] for a in (g, d, nm_, nv_)])

    small_at = {0: 0, 2: 1, 3: 2, 4: 3, 5: 4, 9: 5, 13: 6}
    big_at = {1: 0, 6: 1, 7: 2, 8: 3, 10: 4, 11: 5, 12: 6}
    outs = [loss[0, 0], dx[None]]
    for kind in range(4):
        for idx in range(14):
            outs.append(small[kind][small_at[idx]] if idx in small_at else big_out[big_at[idx]][kind])
    return tuple(outs)
```
